```python
import math
import jax, jax.numpy as jnp
from jax import lax
import numpy as np

D_MODEL = 1024
BATCH = 8
SEQ = 4096
DEPTH = 2

N_A_LAYERS = DEPTH // 2
N_B_LAYERS = DEPTH - N_A_LAYERS
CONV_WIDTH = 31
N_HEADS = 16
N_KV_HEADS = 4
HEAD_DIM = 64
Q_PER_KV = N_HEADS // N_KV_HEADS
WINDOW = 128
BLOCK = 128
ROPE_DIM = HEAD_DIM // 4
ROPE_THETA = 500000.0
D_FF = 4 * D_MODEL
PLE_DIM = 256
DEEPNORM_ALPHA = (2 * DEPTH) ** 0.25
DEEPNORM_BETA = (8 * DEPTH) ** -0.25
LN_EPS = 1e-5

kernel_name = "yoco_conformer_swa_sink_deepnorm"


def layer_norm(x, g, b):
    xf = x.astype(jnp.float32)
    mu = jnp.mean(xf, axis=-1, keepdims=True)
    var = jnp.mean(jnp.square(xf - mu), axis=-1, keepdims=True)
    y = (xf - mu) * lax.rsqrt(var + LN_EPS)
    return (y * g.astype(jnp.float32) + b.astype(jnp.float32)).astype(x.dtype)


def rope_tables(seq_len):
    pos = jnp.arange(seq_len, dtype=jnp.float32)
    inv_freq = ROPE_THETA ** (-jnp.arange(0, ROPE_DIM, 2, dtype=jnp.float32) / ROPE_DIM)
    ang = pos[:, None] * inv_freq[None, :]
    return jnp.cos(ang)[:, None, :], jnp.sin(ang)[:, None, :]


def partial_rope(t, cos, sin):
    half = ROPE_DIM // 2
    x1 = t[..., :half].astype(jnp.float32)
    x2 = t[..., half:ROPE_DIM].astype(jnp.float32)
    rot = jnp.concatenate([x1 * cos - x2 * sin, x2 * cos + x1 * sin], axis=-1).astype(t.dtype)
    return jnp.concatenate([rot, t[..., ROPE_DIM:]], axis=-1)


def conformer_conv(x, w_in, b_in, w_dw, b_dw, ln_g, ln_b, w_out, b_out):
    h = x @ w_in + b_in
    a, gate = jnp.split(h, 2, axis=-1)
    h = a * jax.nn.sigmoid(gate)
    h = lax.conv_general_dilated(
        h, w_dw[:, None, :], window_strides=(1,), padding=[(CONV_WIDTH - 1, 0)],
        dimension_numbers=("NWC", "WIO", "NWC"), feature_group_count=D_MODEL) + b_dw
    h = jax.nn.silu(layer_norm(h, ln_g, ln_b))
    return h @ w_out + b_out


def shared_banded_kv(x, w_k, w_v, cos, sin):
    B, T, _ = x.shape
    nb = T // BLOCK
    k = partial_rope((x @ w_k).reshape(B, T, N_KV_HEADS, HEAD_DIM), cos, sin)
    v = (x @ w_v).reshape(B, T, N_KV_HEADS, HEAD_DIM)

    def band(t):
        tb = t.reshape(B, nb, BLOCK, N_KV_HEADS, HEAD_DIM)
        prev = jnp.pad(tb, ((0, 0), (1, 0), (0, 0), (0, 0), (0, 0)))[:, :-1]
        return jnp.concatenate([prev, tb], axis=2)

    return band(k), band(v)


def band_mask(nb):
    n = jnp.arange(nb)[:, None, None]
    a = jnp.arange(BLOCK)[None, :, None]
    s = jnp.arange(2 * BLOCK)[None, None, :]
    qpos = n * BLOCK + a
    kpos = (n - 1) * BLOCK + s
    rel = qpos - kpos
    return (kpos >= 0) & (rel >= 0) & (rel < WINDOW)


def swa_sink_attention(x, w_q, sinks, w_o, kk, vv, cos, sin):
    B, T, _ = x.shape
    nb = T // BLOCK
    q = partial_rope((x @ w_q).reshape(B, T, N_HEADS, HEAD_DIM), cos, sin)
    q = q.reshape(B, nb, BLOCK, N_KV_HEADS, Q_PER_KV, HEAD_DIM)
    scores = jnp.einsum("bnqkgd,bnskd->bnkgqs", q, kk,
                        preferred_element_type=jnp.float32) * (1.0 / math.sqrt(HEAD_DIM))
    mask = band_mask(nb)[None, :, None, None]
    scores = jnp.where(mask, scores, -jnp.inf)
    sink = sinks.astype(jnp.float32).reshape(1, 1, N_KV_HEADS, Q_PER_KV, 1, 1)
    lse = jnp.logaddexp(jax.nn.logsumexp(scores, axis=-1, keepdims=True), sink)
    probs = jnp.exp(scores - lse).astype(vv.dtype)
    out = jnp.einsum("bnkgqs,bnskd->bnqkgd", probs, vv)
    return out.reshape(B, T, N_HEADS * HEAD_DIM) @ w_o


def sq_relu_mlp(x, w_up, w_down):
    return jnp.square(jax.nn.relu(x @ w_up)) @ w_down


def _fwd_setup_inputs(seed: int = 0) -> dict:
    key = jax.random.key(seed)
    ks = jax.random.split(key, 32)
    f32 = jnp.float32

    def nrm(k, shape, scale):
        return jax.random.normal(k, shape, f32) * scale

    def gain(k, shape):
        return 1.0 + 0.02 * jax.random.normal(k, shape, f32)

    D = D_MODEL
    HD = N_HEADS * HEAD_DIM
    KVD = N_KV_HEADS * HEAD_DIM
    return {
        "x": nrm(ks[0], (BATCH, SEQ, D), 1.0),
        "p": nrm(ks[1], (DEPTH, BATCH, SEQ, PLE_DIM), 1.0),
        "conv_w_in": nrm(ks[2], (N_A_LAYERS, D, 2 * D), D ** -0.5),
        "conv_b_in": nrm(ks[3], (N_A_LAYERS, 2 * D), 0.02),
        "conv_w_dw": nrm(ks[4], (N_A_LAYERS, CONV_WIDTH, D), CONV_WIDTH ** -0.5),
        "conv_b_dw": nrm(ks[5], (N_A_LAYERS, D), 0.02),
        "conv_ln_g": gain(ks[6], (N_A_LAYERS, D)),
        "conv_ln_b": nrm(ks[7], (N_A_LAYERS, D), 0.02),
        "conv_w_out": nrm(ks[8], (N_A_LAYERS, D, D), D ** -0.5 * DEEPNORM_BETA),
        "conv_b_out": nrm(ks[9], (N_A_LAYERS, D), 0.02),
        "kv_w_k": nrm(ks[10], (D, KVD), D ** -0.5),
        "kv_w_v": nrm(ks[11], (D, KVD), D ** -0.5),
        "attn_w_q": nrm(ks[12], (N_B_LAYERS, D, HD), D ** -0.5),
        "attn_sinks": nrm(ks[13], (N_B_LAYERS, N_HEADS), 0.5),
        "attn_w_o": nrm(ks[14], (N_B_LAYERS, HD, D), HD ** -0.5 * DEEPNORM_BETA),
        "mix_ln_g": gain(ks[15], (DEPTH, D)),
        "mix_ln_b": nrm(ks[16], (DEPTH, D), 0.02),
        "mlp_w_up": nrm(ks[17], (DEPTH, D, D_FF), D ** -0.5),
        "mlp_w_down": nrm(ks[18], (DEPTH, D_FF, D), D_FF ** -0.5 * DEEPNORM_BETA),
        "mlp_ln_g": gain(ks[19], (DEPTH, D)),
        "mlp_ln_b": nrm(ks[20], (DEPTH, D), 0.02),
        "ple_w_proj": nrm(ks[21], (DEPTH, PLE_DIM, D), PLE_DIM ** -0.5),
        "ple_w_gate": nrm(ks[22], (DEPTH, D, D), D ** -0.5),
    }


def _fwd_reference(x, p, conv_w_in, conv_b_in, conv_w_dw, conv_b_dw, conv_ln_g, conv_ln_b,
              conv_w_out, conv_b_out, kv_w_k, kv_w_v, attn_w_q, attn_sinks, attn_w_o,
              mix_ln_g, mix_ln_b, mlp_w_up, mlp_w_down, mlp_ln_g, mlp_ln_b,
              ple_w_proj, ple_w_gate):
    T = x.shape[1]
    cos, sin = rope_tables(T)
    kk = vv = None
    for i in range(DEPTH):
        if i < N_A_LAYERS:
            y = conformer_conv(x, conv_w_in[i], conv_b_in[i], conv_w_dw[i], conv_b_dw[i],
                               conv_ln_g[i], conv_ln_b[i], conv_w_out[i], conv_b_out[i])
        else:
            if i == N_A_LAYERS:
                kk, vv = shared_banded_kv(x, kv_w_k, kv_w_v, cos, sin)
            j = i - N_A_LAYERS
            y = swa_sink_attention(x, attn_w_q[j], attn_sinks[j], attn_w_o[j], kk, vv, cos, sin)
        x = layer_norm(DEEPNORM_ALPHA * x + y, mix_ln_g[i], mix_ln_b[i])
        x = layer_norm(DEEPNORM_ALPHA * x + sq_relu_mlp(x, mlp_w_up[i], mlp_w_down[i]),
                       mlp_ln_g[i], mlp_ln_b[i])
        x = x + (p[i] @ ple_w_proj[i]) * jax.nn.sigmoid(x @ ple_w_gate[i])
    return x


import jax as _jax
import jax.numpy as _jnp

TWIN_FORMAT = 'train_step'
FWD_PARAMS = ['x', 'p', 'conv_w_in', 'conv_b_in', 'conv_w_dw', 'conv_b_dw', 'conv_ln_g', 'conv_ln_b', 'conv_w_out', 'conv_b_out', 'kv_w_k', 'kv_w_v', 'attn_w_q', 'attn_sinks', 'attn_w_o', 'mix_ln_g', 'mix_ln_b', 'mlp_w_up', 'mlp_w_down', 'mlp_ln_g', 'mlp_ln_b', 'ple_w_proj', 'ple_w_gate']
TWIN_WEIGHTS = ['conv_w_in', 'conv_b_in', 'conv_w_dw', 'conv_b_dw', 'conv_ln_g', 'conv_ln_b', 'conv_w_out', 'conv_b_out', 'kv_w_k', 'kv_w_v', 'attn_w_q', 'attn_sinks', 'attn_w_o', 'mix_ln_g', 'mix_ln_b', 'mlp_w_up', 'mlp_w_down', 'mlp_ln_g', 'mlp_ln_b', 'ple_w_proj', 'ple_w_gate']
TWIN_DIFF_INPUT = 'x'
TWIN_INPUTS = ['x', 'p', 'conv_w_in', 'conv_b_in', 'conv_w_dw', 'conv_b_dw', 'conv_ln_g', 'conv_ln_b', 'conv_w_out', 'conv_b_out', 'kv_w_k', 'kv_w_v', 'attn_w_q', 'attn_sinks', 'attn_w_o', 'mix_ln_g', 'mix_ln_b', 'mlp_w_up', 'mlp_w_down', 'mlp_ln_g', 'mlp_ln_b', 'ple_w_proj', 'ple_w_gate', 'loss_target', 'm_conv_w_in', 'm_conv_b_in', 'm_conv_w_dw', 'm_conv_b_dw', 'm_conv_ln_g', 'm_conv_ln_b', 'm_conv_w_out', 'm_conv_b_out', 'm_kv_w_k', 'm_kv_w_v', 'm_attn_w_q', 'm_attn_sinks', 'm_attn_w_o', 'm_mix_ln_g', 'm_mix_ln_b', 'm_mlp_w_up', 'm_mlp_w_down', 'm_mlp_ln_g', 'm_mlp_ln_b', 'm_ple_w_proj', 'm_ple_w_gate', 'v_conv_w_in', 'v_conv_b_in', 'v_conv_w_dw', 'v_conv_b_dw', 'v_conv_ln_g', 'v_conv_ln_b', 'v_conv_w_out', 'v_conv_b_out', 'v_kv_w_k', 'v_kv_w_v', 'v_attn_w_q', 'v_attn_sinks', 'v_attn_w_o', 'v_mix_ln_g', 'v_mix_ln_b', 'v_mlp_w_up', 'v_mlp_w_down', 'v_mlp_ln_g', 'v_mlp_ln_b', 'v_ple_w_proj', 'v_ple_w_gate']
TWIN_OUTPUTS = ['loss', 'grad_x', 'grad_conv_w_in', 'grad_conv_b_in', 'grad_conv_w_dw', 'grad_conv_b_dw', 'grad_conv_ln_g', 'grad_conv_ln_b', 'grad_conv_w_out', 'grad_conv_b_out', 'grad_kv_w_k', 'grad_kv_w_v', 'grad_attn_w_q', 'grad_attn_sinks', 'grad_attn_w_o', 'grad_mix_ln_g', 'grad_mix_ln_b', 'grad_mlp_w_up', 'grad_mlp_w_down', 'grad_mlp_ln_g', 'grad_mlp_ln_b', 'grad_ple_w_proj', 'grad_ple_w_gate', 'delta_conv_w_in', 'delta_conv_b_in', 'delta_conv_w_dw', 'delta_conv_b_dw', 'delta_conv_ln_g', 'delta_conv_ln_b', 'delta_conv_w_out', 'delta_conv_b_out', 'delta_kv_w_k', 'delta_kv_w_v', 'delta_attn_w_q', 'delta_attn_sinks', 'delta_attn_w_o', 'delta_mix_ln_g', 'delta_mix_ln_b', 'delta_mlp_w_up', 'delta_mlp_w_down', 'delta_mlp_ln_g', 'delta_mlp_ln_b', 'delta_ple_w_proj', 'delta_ple_w_gate', 'new_m_conv_w_in', 'new_m_conv_b_in', 'new_m_conv_w_dw', 'new_m_conv_b_dw', 'new_m_conv_ln_g', 'new_m_conv_ln_b', 'new_m_conv_w_out', 'new_m_conv_b_out', 'new_m_kv_w_k', 'new_m_kv_w_v', 'new_m_attn_w_q', 'new_m_attn_sinks', 'new_m_attn_w_o', 'new_m_mix_ln_g', 'new_m_mix_ln_b', 'new_m_mlp_w_up', 'new_m_mlp_w_down', 'new_m_mlp_ln_g', 'new_m_mlp_ln_b', 'new_m_ple_w_proj', 'new_m_ple_w_gate', 'new_v_conv_w_in', 'new_v_conv_b_in', 'new_v_conv_w_dw', 'new_v_conv_b_dw', 'new_v_conv_ln_g', 'new_v_conv_ln_b', 'new_v_conv_w_out', 'new_v_conv_b_out', 'new_v_kv_w_k', 'new_v_kv_w_v', 'new_v_attn_w_q', 'new_v_attn_sinks', 'new_v_attn_w_o', 'new_v_mix_ln_g', 'new_v_mix_ln_b', 'new_v_mlp_w_up', 'new_v_mlp_w_down', 'new_v_mlp_ln_g', 'new_v_mlp_ln_b', 'new_v_ple_w_proj', 'new_v_ple_w_gate']
TWIN_LEAF_KINDS = {'loss': 'loss', 'grad_x': 'grad_x', 'grad_conv_w_in': 'grad_w', 'grad_conv_b_in': 'grad_w', 'grad_conv_w_dw': 'grad_w', 'grad_conv_b_dw': 'grad_w', 'grad_conv_ln_g': 'grad_w', 'grad_conv_ln_b': 'grad_w', 'grad_conv_w_out': 'grad_w', 'grad_conv_b_out': 'grad_w', 'grad_kv_w_k': 'grad_w', 'grad_kv_w_v': 'grad_w', 'grad_attn_w_q': 'grad_w', 'grad_attn_sinks': 'grad_w', 'grad_attn_w_o': 'grad_w', 'grad_mix_ln_g': 'grad_w', 'grad_mix_ln_b': 'grad_w', 'grad_mlp_w_up': 'grad_w', 'grad_mlp_w_down': 'grad_w', 'grad_mlp_ln_g': 'grad_w', 'grad_mlp_ln_b': 'grad_w', 'grad_ple_w_proj': 'grad_w', 'grad_ple_w_gate': 'grad_w', 'delta_conv_w_in': 'delta_w', 'delta_conv_b_in': 'delta_w', 'delta_conv_w_dw': 'delta_w', 'delta_conv_b_dw': 'delta_w', 'delta_conv_ln_g': 'delta_w', 'delta_conv_ln_b': 'delta_w', 'delta_conv_w_out': 'delta_w', 'delta_conv_b_out': 'delta_w', 'delta_kv_w_k': 'delta_w', 'delta_kv_w_v': 'delta_w', 'delta_attn_w_q': 'delta_w', 'delta_attn_sinks': 'delta_w', 'delta_attn_w_o': 'delta_w', 'delta_mix_ln_g': 'delta_w', 'delta_mix_ln_b': 'delta_w', 'delta_mlp_w_up': 'delta_w', 'delta_mlp_w_down': 'delta_w', 'delta_mlp_ln_g': 'delta_w', 'delta_mlp_ln_b': 'delta_w', 'delta_ple_w_proj': 'delta_w', 'delta_ple_w_gate': 'delta_w', 'new_m_conv_w_in': 'new_m', 'new_m_conv_b_in': 'new_m', 'new_m_conv_w_dw': 'new_m', 'new_m_conv_b_dw': 'new_m', 'new_m_conv_ln_g': 'new_m', 'new_m_conv_ln_b': 'new_m', 'new_m_conv_w_out': 'new_m', 'new_m_conv_b_out': 'new_m', 'new_m_kv_w_k': 'new_m', 'new_m_kv_w_v': 'new_m', 'new_m_attn_w_q': 'new_m', 'new_m_attn_sinks': 'new_m', 'new_m_attn_w_o': 'new_m', 'new_m_mix_ln_g': 'new_m', 'new_m_mix_ln_b': 'new_m', 'new_m_mlp_w_up': 'new_m', 'new_m_mlp_w_down': 'new_m', 'new_m_mlp_ln_g': 'new_m', 'new_m_mlp_ln_b': 'new_m', 'new_m_ple_w_proj': 'new_m', 'new_m_ple_w_gate': 'new_m', 'new_v_conv_w_in': 'new_v', 'new_v_conv_b_in': 'new_v', 'new_v_conv_w_dw': 'new_v', 'new_v_conv_b_dw': 'new_v', 'new_v_conv_ln_g': 'new_v', 'new_v_conv_ln_b': 'new_v', 'new_v_conv_w_out': 'new_v', 'new_v_conv_b_out': 'new_v', 'new_v_kv_w_k': 'new_v', 'new_v_kv_w_v': 'new_v', 'new_v_attn_w_q': 'new_v', 'new_v_attn_sinks': 'new_v', 'new_v_attn_w_o': 'new_v', 'new_v_mix_ln_g': 'new_v', 'new_v_mix_ln_b': 'new_v', 'new_v_mlp_w_up': 'new_v', 'new_v_mlp_w_down': 'new_v', 'new_v_mlp_ln_g': 'new_v', 'new_v_mlp_ln_b': 'new_v', 'new_v_ple_w_proj': 'new_v', 'new_v_ple_w_gate': 'new_v'}


def _forward(args):
    return _fwd_reference(*[args[k] for k in FWD_PARAMS])


def _output_shape():
    out = _jax.eval_shape(lambda: _forward(_fwd_setup_inputs(0)))
    return out.shape, out.dtype

N_MICROBATCH = 1
ADAM_LR = 0.001
ADAM_B1 = 0.9
ADAM_B2 = 0.999
ADAM_EPS = 1e-08
ADAM_WD = 0.01
ADAM_STEP = 10
PER_EXAMPLE_BATCH_AXIS = {'x': 0, 'p': 1, 'loss_target': 0}
SHARED_INPUTS = []
_WEIGHT_DTYPES = {'conv_w_in': _jnp.float32, 'conv_b_in': _jnp.float32, 'conv_w_dw': _jnp.float32, 'conv_b_dw': _jnp.float32, 'conv_ln_g': _jnp.float32, 'conv_ln_b': _jnp.float32, 'conv_w_out': _jnp.float32, 'conv_b_out': _jnp.float32, 'kv_w_k': _jnp.float32, 'kv_w_v': _jnp.float32, 'attn_w_q': _jnp.float32, 'attn_sinks': _jnp.float32, 'attn_w_o': _jnp.float32, 'mix_ln_g': _jnp.float32, 'mix_ln_b': _jnp.float32, 'mlp_w_up': _jnp.float32, 'mlp_w_down': _jnp.float32, 'mlp_ln_g': _jnp.float32, 'mlp_ln_b': _jnp.float32, 'ple_w_proj': _jnp.float32, 'ple_w_gate': _jnp.float32}
MOMENT_SCALE = {'conv_w_in': 3.304219e-02, 'conv_b_in': 3.174272e-01, 'conv_w_dw': 5.569767e-02, 'conv_b_dw': 8.118616e-01, 'conv_ln_g': 3.065029e-01, 'conv_ln_b': 4.815984e-01, 'conv_w_out': 3.296857e-01, 'conv_b_out': 1.931180e+00, 'kv_w_k': 3.987897e-02, 'kv_w_v': 2.929378e-01, 'attn_w_q': 1.980564e-02, 'attn_sinks': 1.389930e-02, 'attn_w_o': 2.719597e-01, 'mix_ln_g': 7.559018e-01, 'mix_ln_b': 2.958796e+00, 'mlp_w_up': 5.783181e-02, 'mlp_w_down': 9.003780e-01, 'mlp_ln_g': 2.340815e+01, 'mlp_ln_b': 6.251816e+00, 'ple_w_proj': 2.984339e-01, 'ple_w_gate': 4.947698e-01}


def _to_microbatches(a, axis):
    t = _jnp.moveaxis(a, axis, 0)
    t = t.reshape((N_MICROBATCH, t.shape[0] // N_MICROBATCH) + t.shape[1:])
    return _jnp.moveaxis(t, 1, axis + 1)


def setup_inputs(seed: int = 0) -> dict:
    inp = _fwd_setup_inputs(seed)
    key = _jax.random.fold_in(_jax.random.key(seed), 7919)
    shape, _ = _output_shape()
    out = dict(inp)
    out["loss_target"] = _jax.random.normal(_jax.random.fold_in(key, 0), shape, _jnp.float32)
    for i, name in enumerate(TWIN_WEIGHTS):
        w = inp[name].astype(_jnp.float32)
        if MOMENT_SCALE is None:
            s = _jnp.sqrt(_jnp.mean(_jnp.square(w)) + 1e-30)
        else:
            s = MOMENT_SCALE[name]
        km, kv = _jax.random.split(_jax.random.fold_in(key, i + 1))
        out[name] = w
        out["m_" + name] = s * _jax.random.normal(km, w.shape, _jnp.float32)
        out["v_" + name] = (s * s) * _jax.random.uniform(kv, w.shape, _jnp.float32, 0.5, 1.5)
    if N_MICROBATCH > 1:
        for name, axis in PER_EXAMPLE_BATCH_AXIS.items():
            out[name] = _to_microbatches(out[name], axis)
    return {'x': out['x'], 'p': out['p'], 'conv_w_in': out['conv_w_in'], 'conv_b_in': out['conv_b_in'], 'conv_w_dw': out['conv_w_dw'], 'conv_b_dw': out['conv_b_dw'], 'conv_ln_g': out['conv_ln_g'], 'conv_ln_b': out['conv_ln_b'], 'conv_w_out': out['conv_w_out'], 'conv_b_out': out['conv_b_out'], 'kv_w_k': out['kv_w_k'], 'kv_w_v': out['kv_w_v'], 'attn_w_q': out['attn_w_q'], 'attn_sinks': out['attn_sinks'], 'attn_w_o': out['attn_w_o'], 'mix_ln_g': out['mix_ln_g'], 'mix_ln_b': out['mix_ln_b'], 'mlp_w_up': out['mlp_w_up'], 'mlp_w_down': out['mlp_w_down'], 'mlp_ln_g': out['mlp_ln_g'], 'mlp_ln_b': out['mlp_ln_b'], 'ple_w_proj': out['ple_w_proj'], 'ple_w_gate': out['ple_w_gate'], 'loss_target': out['loss_target'], 'm_conv_w_in': out['m_conv_w_in'], 'm_conv_b_in': out['m_conv_b_in'], 'm_conv_w_dw': out['m_conv_w_dw'], 'm_conv_b_dw': out['m_conv_b_dw'], 'm_conv_ln_g': out['m_conv_ln_g'], 'm_conv_ln_b': out['m_conv_ln_b'], 'm_conv_w_out': out['m_conv_w_out'], 'm_conv_b_out': out['m_conv_b_out'], 'm_kv_w_k': out['m_kv_w_k'], 'm_kv_w_v': out['m_kv_w_v'], 'm_attn_w_q': out['m_attn_w_q'], 'm_attn_sinks': out['m_attn_sinks'], 'm_attn_w_o': out['m_attn_w_o'], 'm_mix_ln_g': out['m_mix_ln_g'], 'm_mix_ln_b': out['m_mix_ln_b'], 'm_mlp_w_up': out['m_mlp_w_up'], 'm_mlp_w_down': out['m_mlp_w_down'], 'm_mlp_ln_g': out['m_mlp_ln_g'], 'm_mlp_ln_b': out['m_mlp_ln_b'], 'm_ple_w_proj': out['m_ple_w_proj'], 'm_ple_w_gate': out['m_ple_w_gate'], 'v_conv_w_in': out['v_conv_w_in'], 'v_conv_b_in': out['v_conv_b_in'], 'v_conv_w_dw': out['v_conv_w_dw'], 'v_conv_b_dw': out['v_conv_b_dw'], 'v_conv_ln_g': out['v_conv_ln_g'], 'v_conv_ln_b': out['v_conv_ln_b'], 'v_conv_w_out': out['v_conv_w_out'], 'v_conv_b_out': out['v_conv_b_out'], 'v_kv_w_k': out['v_kv_w_k'], 'v_kv_w_v': out['v_kv_w_v'], 'v_attn_w_q': out['v_attn_w_q'], 'v_attn_sinks': out['v_attn_sinks'], 'v_attn_w_o': out['v_attn_w_o'], 'v_mix_ln_g': out['v_mix_ln_g'], 'v_mix_ln_b': out['v_mix_ln_b'], 'v_mlp_w_up': out['v_mlp_w_up'], 'v_mlp_w_down': out['v_mlp_w_down'], 'v_mlp_ln_g': out['v_mlp_ln_g'], 'v_mlp_ln_b': out['v_mlp_ln_b'], 'v_ple_w_proj': out['v_ple_w_proj'], 'v_ple_w_gate': out['v_ple_w_gate']}


def _loss(weights, diff, rest, loss_target):
    with _jax.named_scope("forward"):
        args = {**rest, TWIN_DIFF_INPUT: diff, **{k: w.astype(_WEIGHT_DTYPES[k]) for k, w in weights.items()}}
        y = _forward(args)
    with _jax.named_scope("loss_head"):
        err = _jnp.square(y.astype(_jnp.float32) - loss_target)
        return 0.5 * _jnp.sum(_jnp.mean(err, axis=-1)) if err.ndim else 0.5 * err


def _adamw(w, g, m, v):
    m = ADAM_B1 * m + (1.0 - ADAM_B1) * g
    v = ADAM_B2 * v + (1.0 - ADAM_B2) * _jnp.square(g)
    m_hat = m / (1.0 - ADAM_B1 ** ADAM_STEP)
    v_hat = v / (1.0 - ADAM_B2 ** ADAM_STEP)
    delta = -ADAM_LR * (m_hat / (_jnp.sqrt(v_hat) + ADAM_EPS) + ADAM_WD * w)
    return delta, m, v


def reference(x, p, conv_w_in, conv_b_in, conv_w_dw, conv_b_dw, conv_ln_g, conv_ln_b, conv_w_out, conv_b_out, kv_w_k, kv_w_v, attn_w_q, attn_sinks, attn_w_o, mix_ln_g, mix_ln_b, mlp_w_up, mlp_w_down, mlp_ln_g, mlp_ln_b, ple_w_proj, ple_w_gate, loss_target, m_conv_w_in, m_conv_b_in, m_conv_w_dw, m_conv_b_dw, m_conv_ln_g, m_conv_ln_b, m_conv_w_out, m_conv_b_out, m_kv_w_k, m_kv_w_v, m_attn_w_q, m_attn_sinks, m_attn_w_o, m_mix_ln_g, m_mix_ln_b, m_mlp_w_up, m_mlp_w_down, m_mlp_ln_g, m_mlp_ln_b, m_ple_w_proj, m_ple_w_gate, v_conv_w_in, v_conv_b_in, v_conv_w_dw, v_conv_b_dw, v_conv_ln_g, v_conv_ln_b, v_conv_w_out, v_conv_b_out, v_kv_w_k, v_kv_w_v, v_attn_w_q, v_attn_sinks, v_attn_w_o, v_mix_ln_g, v_mix_ln_b, v_mlp_w_up, v_mlp_w_down, v_mlp_ln_g, v_mlp_ln_b, v_ple_w_proj, v_ple_w_gate):
    given = dict(x=x, p=p, conv_w_in=conv_w_in, conv_b_in=conv_b_in, conv_w_dw=conv_w_dw, conv_b_dw=conv_b_dw, conv_ln_g=conv_ln_g, conv_ln_b=conv_ln_b, conv_w_out=conv_w_out, conv_b_out=conv_b_out, kv_w_k=kv_w_k, kv_w_v=kv_w_v, attn_w_q=attn_w_q, attn_sinks=attn_sinks, attn_w_o=attn_w_o, mix_ln_g=mix_ln_g, mix_ln_b=mix_ln_b, mlp_w_up=mlp_w_up, mlp_w_down=mlp_w_down, mlp_ln_g=mlp_ln_g, mlp_ln_b=mlp_ln_b, ple_w_proj=ple_w_proj, ple_w_gate=ple_w_gate, loss_target=loss_target, m_conv_w_in=m_conv_w_in, m_conv_b_in=m_conv_b_in, m_conv_w_dw=m_conv_w_dw, m_conv_b_dw=m_conv_b_dw, m_conv_ln_g=m_conv_ln_g, m_conv_ln_b=m_conv_ln_b, m_conv_w_out=m_conv_w_out, m_conv_b_out=m_conv_b_out, m_kv_w_k=m_kv_w_k, m_kv_w_v=m_kv_w_v, m_attn_w_q=m_attn_w_q, m_attn_sinks=m_attn_sinks, m_attn_w_o=m_attn_w_o, m_mix_ln_g=m_mix_ln_g, m_mix_ln_b=m_mix_ln_b, m_mlp_w_up=m_mlp_w_up, m_mlp_w_down=m_mlp_w_down, m_mlp_ln_g=m_mlp_ln_g, m_mlp_ln_b=m_mlp_ln_b, m_ple_w_proj=m_ple_w_proj, m_ple_w_gate=m_ple_w_gate, v_conv_w_in=v_conv_w_in, v_conv_b_in=v_conv_b_in, v_conv_w_dw=v_conv_w_dw, v_conv_b_dw=v_conv_b_dw, v_conv_ln_g=v_conv_ln_g, v_conv_ln_b=v_conv_ln_b, v_conv_w_out=v_conv_w_out, v_conv_b_out=v_conv_b_out, v_kv_w_k=v_kv_w_k, v_kv_w_v=v_kv_w_v, v_attn_w_q=v_attn_w_q, v_attn_sinks=v_attn_sinks, v_attn_w_o=v_attn_w_o, v_mix_ln_g=v_mix_ln_g, v_mix_ln_b=v_mix_ln_b, v_mlp_w_up=v_mlp_w_up, v_mlp_w_down=v_mlp_w_down, v_mlp_ln_g=v_mlp_ln_g, v_mlp_ln_b=v_mlp_ln_b, v_ple_w_proj=v_ple_w_proj, v_ple_w_gate=v_ple_w_gate)
    weights = {n: given[n] for n in TWIN_WEIGHTS}
    shared = {n: given[n] for n in SHARED_INPUTS}
    per_example = {n: given[n] for n in ['x', 'p']}
    grad_fn = _jax.value_and_grad(_loss, argnums=(0, 1))

    def one_microbatch(ex, loss_target):
        ex = dict(ex)
        diff = ex.pop(TWIN_DIFF_INPUT)
        return grad_fn(weights, diff, {**shared, **ex}, loss_target)

    if N_MICROBATCH == 1:
        loss, (grad_w, grad_x) = one_microbatch(per_example, given["loss_target"])
    else:
        def body(carry, xs):
            loss_sum, grad_sum = carry
            l_k, (gw_k, gx_k) = one_microbatch(xs[0], xs[1])
            with _jax.named_scope("update"):
                return (loss_sum + l_k, _jax.tree.map(_jnp.add, grad_sum, gw_k)), gx_k

        init = (_jnp.zeros((), _jnp.float32), _jax.tree.map(_jnp.zeros_like, weights))
        (loss, grad_w), grad_x = _jax.lax.scan(body, init, (per_example, given["loss_target"]))
    with _jax.named_scope("update"):
        delta_w, new_m, new_v = {}, {}, {}
        for n in TWIN_WEIGHTS:
            delta_w[n], new_m[n], new_v[n] = _adamw(weights[n], grad_w[n], given["m_" + n], given["v_" + n])
    return (loss, grad_x, *[grad_w[n] for n in TWIN_WEIGHTS], *[delta_w[n] for n in TWIN_WEIGHTS],
            *[new_m[n] for n in TWIN_WEIGHTS], *[new_v[n] for n in TWIN_WEIGHTS])
```

```python
import functools
import math

import jax
import jax.numpy as jnp
from jax import lax
from jax.experimental import pallas as pl
from jax.experimental.pallas import tpu as pltpu

F32 = jnp.float32
BF16 = jnp.bfloat16

N_DEV = 8
HEAD_DIM = 64
ROPE_DIM = HEAD_DIM // 4
ROPE_HALF = ROPE_DIM // 2
ROPE_THETA = 500000.0
ATT_BLOCK = 128
CONV_WIDTH = 31
CONV_HALO = 32
LN_EPS = 1e-5
DEPTH = 2
DEEPNORM_ALPHA = (2 * DEPTH) ** 0.25
MASK_VALUE = -1e30

ADAM_LR = 0.001
ADAM_B1 = 0.9
ADAM_B2 = 0.999
ADAM_EPS = 1e-08
ADAM_WD = 0.01
ADAM_STEP = 10

LANES = 128
SUBLANES = 8
VMEM_LIMIT_BYTES = 52 * 1024 * 1024
TOKEN_TILE = 512
MESH_ID = pl.DeviceIdType.MESH

WEIGHT_NAMES = ['conv_w_in', 'conv_b_in', 'conv_w_dw', 'conv_b_dw', 'conv_ln_g', 'conv_ln_b', 'conv_w_out',
                'conv_b_out', 'kv_w_k', 'kv_w_v', 'attn_w_q', 'attn_sinks', 'attn_w_o', 'mix_ln_g', 'mix_ln_b',
                'mlp_w_up', 'mlp_w_down', 'mlp_ln_g', 'mlp_ln_b', 'ple_w_proj', 'ple_w_gate']
BIG_WEIGHTS = ['conv_w_in', 'conv_w_out', 'kv_w_k', 'kv_w_v', 'attn_w_q', 'attn_w_o', 'mlp_w_up', 'mlp_w_down',
               'ple_w_proj', 'ple_w_gate']
SMALL_SHARDED = [('conv_b_in', 8), ('conv_w_dw', 32), ('conv_b_dw', 8), ('conv_ln_g', 8), ('conv_ln_b', 8),
                 ('conv_b_out', 8)]
REPLICATED = ['mix_ln_g', 'mix_ln_b', 'mlp_ln_g', 'mlp_ln_b', 'attn_sinks']


def _params(sem):
    return pltpu.CompilerParams(dimension_semantics=sem, vmem_limit_bytes=VMEM_LIMIT_BYTES)


def _dot(a, b, dims):
    if a.dtype != BF16:
        a = a.astype(BF16)
    if b.dtype != BF16:
        b = b.astype(BF16)
    return lax.dot_general(a, b, (dims, ((), ())), preferred_element_type=F32)


def _sigmoid(v):
    return 1.0 / (1.0 + jnp.exp(-v))


def _ln_stats(z):
    mu = jnp.mean(z, axis=-1, keepdims=True)
    zc = z - mu
    var = jnp.mean(zc * zc, axis=-1, keepdims=True)
    return zc * lax.rsqrt(var + LN_EPS)


def _ln_fwd(z, g, b):
    return _ln_stats(z) * g + b


def _ln_bwd(dy, z, g):
    xhat = _ln_stats(z)
    mu = jnp.mean(z, axis=-1, keepdims=True)
    zc = z - mu
    rstd = lax.rsqrt(jnp.mean(zc * zc, axis=-1, keepdims=True) + LN_EPS)
    dxh = dy * g
    m1 = jnp.mean(dxh, axis=-1, keepdims=True)
    m2 = jnp.mean(dxh * xhat, axis=-1, keepdims=True)
    dz = rstd * (dxh - m1 - xhat * m2)
    return dz, jnp.sum(dy * xhat, axis=0, keepdims=True), jnp.sum(dy, axis=0, keepdims=True)


def _extra_spec(shape, kind, tm, tn, ij):
    if kind == 'tile':
        return pl.BlockSpec((tm, tn), lambda *g: (ij(g)[0], ij(g)[1]))
    if kind in ('row', 'rowacc'):
        return pl.BlockSpec((1, tn), lambda *g: (0, ij(g)[1]))
    if kind == 'tab':
        return pl.BlockSpec((tm, LANES), lambda *g: (ij(g)[0], 0))
    raise ValueError(kind)


def mm_nn(name, pairs, extras, outs, epi, *, tm, tn, tk=None):
    M = pairs[0][0].shape[0]
    N = outs[0][0].shape[1]
    n_pairs = len(pairs)
    K0 = pairs[0][0].shape[1]
    tk = K0 if tk is None else tk
    nk = K0 // tk
    assert nk == 1 or n_pairs == 1
    assert M % tm == 0 and N % tn == 0 and K0 % tk == 0
    has_rowacc = any(kind == 'rowacc' for _, kind in outs)
    assert not has_rowacc or (N == tn and nk == 1)
    in_specs, operands = [], []
    for a, b, off in pairs:
        K = a.shape[1]
        ktile = K if n_pairs > 1 else tk
        n = b.shape[2]
        assert n % tn == 0 and b.shape[1] == K
        r = n // tn
        in_specs.append(pl.BlockSpec((tm, ktile), lambda i, j, k: (i, k)))
        in_specs.append(pl.BlockSpec((None, ktile, tn), lambda i, j, k, r=r, off=off: ((j + off) // r, k, (j + off) % r)))
        operands += [a, b]
    ij = lambda g: (g[0], g[1])
    for arr, kind in extras:
        in_specs.append(_extra_spec(arr.shape, kind, tm, tn, ij))
        operands.append(arr)
    out_specs = [_extra_spec(o.shape, kind, tm, tn, ij) for o, kind in outs]
    n_ex, n_out = len(extras), len(outs)

    def body(*refs):
        ab = refs[:2 * n_pairs]
        ex = refs[2 * n_pairs:2 * n_pairs + n_ex]
        out = refs[2 * n_pairs + n_ex:2 * n_pairs + n_ex + n_out]
        i = pl.program_id(0)
        if nk == 1:
            accs = [_dot(ab[2 * q][...], ab[2 * q + 1][...], ((1,), (0,))) for q in range(n_pairs)]
            epi(accs, ex, out, i)
        else:
            acc_ref = refs[-1]
            k = pl.program_id(2)

            @pl.when(k == 0)
            def _():
                acc_ref[...] = jnp.zeros_like(acc_ref)

            acc_ref[...] += _dot(ab[0][...], ab[1][...], ((1,), (0,)))

            @pl.when(k == nk - 1)
            def _():
                epi([acc_ref[...]], ex, out, i)

    scratch = [pltpu.VMEM((tm, tn), F32)] if nk > 1 else []
    sem = ("arbitrary",) * 3 if has_rowacc else ("parallel", "parallel", "arbitrary")
    res = pl.pallas_call(
        body, name=name, grid=(M // tm, N // tn, nk), in_specs=in_specs, out_specs=out_specs,
        out_shape=[o for o, _ in outs], scratch_shapes=scratch, compiler_params=_params(sem))(*operands)
    return res


def mm_nt(name, a, b, extras, outs, epi, *, tm, tko, tc):
    M, N = a.shape
    J, K, n = b.shape
    assert J * n == N and n % tc == 0 and M % tm == 0 and K % tko == 0
    r = n // tc
    nc = N // tc
    has_rowacc = any(kind == 'rowacc' for _, kind in outs)
    assert not has_rowacc or K == tko
    in_specs = [pl.BlockSpec((tm, tc), lambda i, j, c: (i, c)),
                pl.BlockSpec((None, tko, tc), lambda i, j, c: (c // r, j, c % r))]
    ij = lambda g: (g[0], g[1])
    operands = [a, b]
    for arr, kind in extras:
        in_specs.append(_extra_spec(arr.shape, kind, tm, tko, ij))
        operands.append(arr)
    out_specs = [_extra_spec(o.shape, kind, tm, tko, ij) for o, kind in outs]
    n_ex, n_out = len(extras), len(outs)

    def body(*refs):
        a_ref, b_ref = refs[:2]
        ex = refs[2:2 + n_ex]
        out = refs[2 + n_ex:2 + n_ex + n_out]
        i = pl.program_id(0)
        if nc == 1:
            epi(_dot(a_ref[...], b_ref[...], ((1,), (1,))), ex, out, i)
        else:
            acc_ref = refs[-1]
            c = pl.program_id(2)

            @pl.when(c == 0)
            def _():
                acc_ref[...] = jnp.zeros_like(acc_ref)

            acc_ref[...] += _dot(a_ref[...], b_ref[...], ((1,), (1,)))

            @pl.when(c == nc - 1)
            def _():
                epi(acc_ref[...], ex, out, i)

    scratch = [pltpu.VMEM((tm, tko), F32)] if nc > 1 else []
    sem = ("arbitrary",) * 3 if has_rowacc else ("parallel", "parallel", "arbitrary")
    return pl.pallas_call(
        body, name=name, grid=(M // tm, K // tko, nc), in_specs=in_specs, out_specs=out_specs,
        out_shape=[o for o, _ in outs], scratch_shapes=scratch, compiler_params=_params(sem))(*operands)


def mm_tn(name, a, d, n, out_dtype, *, tm, tk, tn):
    M, K = a.shape
    N = d.shape[1]
    assert d.shape[0] == M and N % n == 0 and n % tn == 0 and K % tk == 0 and M % tm == 0
    r = n // tn
    nm = M // tm

    def body(a_ref, d_ref, o_ref, acc_ref):
        m = pl.program_id(2)

        @pl.when(m == 0)
        def _():
            acc_ref[...] = jnp.zeros_like(acc_ref)

        acc_ref[...] += _dot(a_ref[...], d_ref[...], ((0,), (0,)))

        @pl.when(m == nm - 1)
        def _():
            o_ref[...] = acc_ref[...].astype(o_ref.dtype)

    return pl.pallas_call(
        body, name=name, grid=(K // tk, N // tn, nm),
        in_specs=[pl.BlockSpec((tm, tk), lambda kk, j, m: (m, kk)), pl.BlockSpec((tm, tn), lambda kk, j, m: (m, j))],
        out_specs=pl.BlockSpec((None, tk, tn), lambda kk, j, m: (j // r, kk, j % r)),
        out_shape=jax.ShapeDtypeStruct((N // n, K, n), out_dtype),
        scratch_shapes=[pltpu.VMEM((tk, tn), F32)],
        compiler_params=_params(("parallel", "parallel", "arbitrary")))(a, d)


def _sds(shape, dtype):
    return jax.ShapeDtypeStruct(shape, dtype)


def _init_or_add(ref, i, value):
    @pl.when(i == 0)
    def _():
        ref[...] = value

    @pl.when(i > 0)
    def _():
        ref[...] += value


def _rope_tables(T):
    pos = jnp.arange(T, dtype=F32)
    inv_freq = ROPE_THETA ** (-jnp.arange(0, ROPE_DIM, 2, dtype=F32) / ROPE_DIM)
    ang = pos[:, None] * inv_freq[None, :]
    cos, sin = jnp.cos(ang), jnp.sin(ang)
    ones = jnp.ones((T, HEAD_DIM - ROPE_DIM), F32)
    zeros = jnp.zeros((T, HEAD_DIM - ROPE_DIM), F32)
    zh = jnp.zeros((T, ROPE_HALF), F32)
    c_head = jnp.concatenate([cos, cos, ones], axis=1)
    s_up = jnp.concatenate([-sin, zh, zeros], axis=1)
    s_dn = jnp.concatenate([zh, sin, zeros], axis=1)
    rep = LANES // HEAD_DIM
    return tuple(jnp.tile(t, (1, rep)) for t in (c_head, s_up, s_dn))


def _rope_chunk(t, c, s_up, s_dn):
    return t * c + pltpu.roll(t, LANES - ROPE_HALF, 1) * s_up + pltpu.roll(t, ROPE_HALF, 1) * s_dn


def _rope_chunk_bwd(d, c, s_up, s_dn):
    return d * c + pltpu.roll(d * s_up, ROPE_HALF, 1) + pltpu.roll(d * s_dn, LANES - ROPE_HALF, 1)


def _rope_wide(t, c, s_up, s_dn, fn):
    chunks = [fn(t[:, q * LANES:(q + 1) * LANES], c, s_up, s_dn) for q in range(t.shape[1] // LANES)]
    return chunks[0] if len(chunks) == 1 else jnp.concatenate(chunks, axis=1)


def _halo_before_spec(tm, D):
    return pl.BlockSpec((CONV_HALO, D), lambda i: (jnp.maximum(i * (tm // CONV_HALO) - 1, 0), 0))


def dwconv_fwd(g, w_dw, b_dw, ln_g, ln_b, *, tm):
    T, D = g.shape
    nl = D // LANES

    def body(g_ref, gh_ref, w_ref, b_ref, lg_ref, lb_ref, c_ref, s_ref, win_ref):
        i = pl.program_id(0)
        win_ref[0:CONV_HALO, :] = jnp.where(i > 0, gh_ref[...], 0.0)
        win_ref[CONV_HALO:, :] = g_ref[...]

        def lane_chunk(q, carry):
            ls = pl.ds(pl.multiple_of(q * LANES, LANES), LANES)
            acc = jnp.broadcast_to(b_ref[:, ls], (tm, LANES))
            for k in range(CONV_WIDTH):
                acc = acc + win_ref[pl.ds(CONV_HALO - (CONV_WIDTH - 1) + k, tm), ls] * w_ref[k:k + 1, ls]
            c_ref[:, ls] = acc
            return carry

        lax.fori_loop(0, nl, lane_chunk, 0)
        n = _ln_fwd(c_ref[...], lg_ref[...], lb_ref[...])
        s_ref[...] = (n * _sigmoid(n)).astype(s_ref.dtype)

    row = pl.BlockSpec((1, D), lambda i: (0, 0))
    return pl.pallas_call(
        body, name="dwconv_fwd", grid=(T // tm,),
        in_specs=[pl.BlockSpec((tm, D), lambda i: (i, 0)), _halo_before_spec(tm, D),
                  pl.BlockSpec((CONV_HALO, D), lambda i: (0, 0)), row, row, row],
        out_specs=[pl.BlockSpec((tm, D), lambda i: (i, 0)), pl.BlockSpec((tm, D), lambda i: (i, 0))],
        out_shape=[_sds((T, D), F32), _sds((T, D), BF16)],
        scratch_shapes=[pltpu.VMEM((tm + CONV_HALO, D), F32)],
        compiler_params=_params(("parallel",)))(g, g, w_dw, b_dw, ln_g, ln_b)


def dwconv_bwd(dc, g, ha, hg, w_dw, *, tm):
    T, D = g.shape
    nl = D // LANES
    last = T // CONV_HALO - 1
    nt = T // tm

    def body(dc_ref, dcn_ref, g_ref, gh_ref, ha_ref, hg_ref, w_ref, dh_ref, dw_ref, dbdw_ref, dbin_ref,
             win_ref, dwin_ref, dg_ref):
        i = pl.program_id(0)
        win_ref[0:CONV_HALO, :] = jnp.where(i > 0, gh_ref[...], 0.0)
        win_ref[CONV_HALO:, :] = g_ref[...]
        dwin_ref[0:tm, :] = dc_ref[...]
        dwin_ref[tm:, :] = jnp.where(i < nt - 1, dcn_ref[...], 0.0)

        @pl.when(i == 0)
        def _():
            dw_ref[...] = jnp.zeros_like(dw_ref)

        def lane_chunk(q, carry):
            ls = pl.ds(pl.multiple_of(q * LANES, LANES), LANES)
            dcv = dwin_ref[0:tm, ls]
            acc = jnp.zeros((tm, LANES), F32)
            for k in range(CONV_WIDTH):
                acc = acc + dwin_ref[pl.ds(CONV_WIDTH - 1 - k, tm), ls] * w_ref[k:k + 1, ls]
                tap = win_ref[pl.ds(CONV_HALO - (CONV_WIDTH - 1) + k, tm), ls]
                dw_ref[k:k + 1, ls] += jnp.sum(dcv * tap, axis=0, keepdims=True)
            dg_ref[:, ls] = acc
            return carry

        lax.fori_loop(0, nl, lane_chunk, 0)
        dg = dg_ref[...]
        ha = ha_ref[...].astype(F32)
        sg = _sigmoid(hg_ref[...].astype(F32))
        d_ha = dg * sg
        d_hg = dg * ha * sg * (1.0 - sg)
        dh_ref[:, 0:D] = d_ha.astype(dh_ref.dtype)
        dh_ref[:, D:] = d_hg.astype(dh_ref.dtype)
        _init_or_add(dbdw_ref, i, jnp.sum(dc_ref[...], axis=0, keepdims=True))
        _init_or_add(dbin_ref, i, jnp.concatenate([jnp.sum(d_ha, axis=0, keepdims=True),
                                                   jnp.sum(d_hg, axis=0, keepdims=True)], axis=1))

    tile = pl.BlockSpec((tm, D), lambda i: (i, 0))
    return pl.pallas_call(
        body, name="dwconv_bwd", grid=(nt,),
        in_specs=[tile,
                  pl.BlockSpec((CONV_HALO, D), lambda i: (jnp.minimum((i + 1) * (tm // CONV_HALO), last), 0)),
                  tile, _halo_before_spec(tm, D),
                  tile, tile, pl.BlockSpec((CONV_HALO, D), lambda i: (0, 0))],
        out_specs=[pl.BlockSpec((tm, 2 * D), lambda i: (i, 0)), pl.BlockSpec((CONV_HALO, D), lambda i: (0, 0)),
                   pl.BlockSpec((1, D), lambda i: (0, 0)), pl.BlockSpec((1, 2 * D), lambda i: (0, 0))],
        out_shape=[_sds((T, 2 * D), BF16), _sds((CONV_HALO, D), F32), _sds((1, D), F32), _sds((1, 2 * D), F32)],
        scratch_shapes=[pltpu.VMEM((tm + CONV_HALO, D), F32), pltpu.VMEM((tm + CONV_HALO, D), F32),
                        pltpu.VMEM((tm, D), F32)],
        compiler_params=_params(("arbitrary",)))(dc, dc, g, g, ha, hg, w_dw)


def _attn_specs(HD, KVD):
    B = ATT_BLOCK
    kv_col = HD // (2 * KVD)
    return [pl.BlockSpec((B, HD), lambda n: (n, 0)),
            pl.BlockSpec((B, 2 * KVD), lambda n: (n, kv_col)),
            pl.BlockSpec((B, 2 * KVD), lambda n: (jnp.maximum(n - 1, 0), kv_col))]


def _band_mask(n, rows):
    r = lax.broadcasted_iota(jnp.int32, (rows, 2 * ATT_BLOCK), 0) % ATT_BLOCK
    j = lax.broadcasted_iota(jnp.int32, (rows, 2 * ATT_BLOCK), 1)
    return (j > r) & (j <= r + ATT_BLOCK) & ((n > 0) | (j >= ATT_BLOCK))


def _group_rows(ref_or_val, g, q_per_kv):
    return jnp.concatenate([ref_or_val[:, (g * q_per_kv + i) * HEAD_DIM:(g * q_per_kv + i + 1) * HEAD_DIM]
                            for i in range(q_per_kv)], axis=0)


def _group_col(ref, g, q_per_kv):
    return jnp.concatenate([ref[:, g * q_per_kv + i:g * q_per_kv + i + 1] for i in range(q_per_kv)], axis=0)


def attn_fwd(qkv, sinks, n_heads, n_kv):
    T = qkv.shape[0]
    HD, KVD = n_heads * HEAD_DIM, n_kv * HEAD_DIM
    B = ATT_BLOCK
    qpk = n_heads // n_kv
    scale = 1.0 / math.sqrt(HEAD_DIM)

    def body(q_ref, kvc_ref, kvp_ref, sink_ref, o_ref, lse_ref):
        n = pl.program_id(0)
        mask = _band_mask(n, qpk * B)
        for g in range(n_kv):
            ks = slice(g * HEAD_DIM, (g + 1) * HEAD_DIM)
            vs = slice(KVD + g * HEAD_DIM, KVD + (g + 1) * HEAD_DIM)
            k_band = jnp.concatenate([kvp_ref[:, ks], kvc_ref[:, ks]], axis=0)
            v_band = jnp.concatenate([kvp_ref[:, vs], kvc_ref[:, vs]], axis=0)
            q_g = _group_rows(q_ref, g, qpk)
            sink = jnp.concatenate([jnp.broadcast_to(sink_ref[:, g * qpk + i:g * qpk + i + 1], (B, 1))
                                    for i in range(qpk)], axis=0)
            s = jnp.where(mask, _dot(q_g, k_band, ((1,), (1,))) * scale, MASK_VALUE)
            m = jnp.maximum(jnp.max(s, axis=-1, keepdims=True), sink)
            e = jnp.exp(s - m)
            l = jnp.sum(e, axis=-1, keepdims=True) + jnp.exp(sink - m)
            lse = m + jnp.log(l)
            probs = jnp.exp(s - lse)
            o_g = _dot(probs, v_band, ((1,), (0,)))
            for i in range(qpk):
                h = g * qpk + i
                o_ref[:, h * HEAD_DIM:(h + 1) * HEAD_DIM] = o_g[i * B:(i + 1) * B, :].astype(o_ref.dtype)
                lse_ref[:, h:h + 1] = lse[i * B:(i + 1) * B, :]

    return pl.pallas_call(
        body, name="attn_fwd", grid=(T // B,),
        in_specs=_attn_specs(HD, KVD) + [pl.BlockSpec((1, n_heads), lambda n: (0, 0))],
        out_specs=[pl.BlockSpec((B, HD), lambda n: (n, 0)), pl.BlockSpec((B, n_heads), lambda n: (n, 0))],
        out_shape=[_sds((T, HD), BF16), _sds((T, n_heads), F32)],
        compiler_params=_params(("parallel",)))(qkv, qkv, qkv, sinks)


def attn_bwd(qkv, do, lse, sinks, rope, n_heads, n_kv):
    T = qkv.shape[0]
    HD, KVD = n_heads * HEAD_DIM, n_kv * HEAD_DIM
    B = ATT_BLOCK
    qpk = n_heads // n_kv
    scale = 1.0 / math.sqrt(HEAD_DIM)
    nb = T // B

    def body(q_ref, kvc_ref, kvp_ref, do_ref, lse_ref, sink_ref, c_ref, su_ref, sd_ref, dq_ref, dkv_ref, dsink_ref):
        n = pl.program_id(0)
        mask = _band_mask(n, qpk * B)

        @pl.when(n == 0)
        def _():
            dkv_ref[...] = jnp.zeros_like(dkv_ref)
            dsink_ref[...] = jnp.zeros_like(dsink_ref)

        dks, dvs, dqs, dsk = [], [], [None] * n_heads, []
        for g in range(n_kv):
            ks = slice(g * HEAD_DIM, (g + 1) * HEAD_DIM)
            vs = slice(KVD + g * HEAD_DIM, KVD + (g + 1) * HEAD_DIM)
            k_band = jnp.concatenate([kvp_ref[:, ks], kvc_ref[:, ks]], axis=0)
            v_band = jnp.concatenate([kvp_ref[:, vs], kvc_ref[:, vs]], axis=0)
            q_g = _group_rows(q_ref, g, qpk)
            do_g = _group_rows(do_ref, g, qpk)
            lse_g = _group_col(lse_ref, g, qpk)
            s = jnp.where(mask, _dot(q_g, k_band, ((1,), (1,))) * scale, MASK_VALUE)
            probs = jnp.exp(s - lse_g)
            dp = _dot(do_g, v_band, ((1,), (1,)))
            delta = jnp.sum(probs * dp, axis=-1, keepdims=True)
            ds = (probs * (dp - delta) * scale).astype(BF16)
            dq_g = _dot(ds, k_band, ((1,), (0,)))
            dks.append(_dot(ds, q_g, ((0,), (0,))))
            dvs.append(_dot(probs.astype(BF16), do_g, ((0,), (0,))))
            for i in range(qpk):
                h = g * qpk + i
                dqs[h] = dq_g[i * B:(i + 1) * B, :]
                p_sink = jnp.exp(sink_ref[:, h:h + 1] - lse_g[i * B:(i + 1) * B, :])
                dsk.append(-jnp.sum(p_sink * delta[i * B:(i + 1) * B, :], axis=0, keepdims=True))
        dq = jnp.concatenate(dqs, axis=1)
        dq_ref[...] = _rope_wide(dq, c_ref[...], su_ref[...], sd_ref[...], _rope_chunk_bwd).astype(dq_ref.dtype)
        dkv = jnp.concatenate(dks + dvs, axis=1)
        prev = pl.ds(pl.multiple_of(jnp.maximum(n - 1, 0) * B, B), B)
        cur = pl.ds(pl.multiple_of(n * B, B), B)
        dkv_ref[prev, :] += dkv[0:B, :]
        dkv_ref[cur, :] += dkv[B:, :]
        dsink_ref[...] += jnp.concatenate(dsk, axis=1)

    tab = pl.BlockSpec((B, LANES), lambda n: (n, 0))
    return pl.pallas_call(
        body, name="attn_bwd", grid=(nb,),
        in_specs=_attn_specs(HD, KVD) + [pl.BlockSpec((B, HD), lambda n: (n, 0)),
                                         pl.BlockSpec((B, n_heads), lambda n: (n, 0)),
                                         pl.BlockSpec((1, n_heads), lambda n: (0, 0)), tab, tab, tab],
        out_specs=[pl.BlockSpec((B, HD), lambda n: (n, 0)), pl.BlockSpec((T, 2 * KVD), lambda n: (0, 0)),
                   pl.BlockSpec((1, n_heads), lambda n: (0, 0))],
        out_shape=[_sds((T, HD + 2 * KVD), BF16), _sds((T, 2 * KVD), F32), _sds((1, n_heads), F32)],
        compiler_params=_params(("arbitrary",)))(qkv, qkv, qkv, do, lse, sinks, *rope)


def dkv_finish(d_qkv, dkv, rope, HD, KVD, *, tm):
    T = dkv.shape[0]
    kv_col = HD // (2 * KVD)

    def body(alias_ref, dkv_ref, c_ref, su_ref, sd_ref, o_ref):
        del alias_ref
        dk = _rope_wide(dkv_ref[:, 0:KVD], c_ref[...], su_ref[...], sd_ref[...], _rope_chunk_bwd)
        o_ref[:, 0:KVD] = dk.astype(o_ref.dtype)
        o_ref[:, KVD:] = dkv_ref[:, KVD:].astype(o_ref.dtype)

    tab = pl.BlockSpec((tm, LANES), lambda i: (i, 0))
    return pl.pallas_call(
        body, name="dkv_finish", grid=(T // tm,),
        in_specs=[pl.BlockSpec(memory_space=pl.ANY), pl.BlockSpec((tm, 2 * KVD), lambda i: (i, 0)), tab, tab, tab],
        out_specs=pl.BlockSpec((tm, 2 * KVD), lambda i: (i, kv_col)),
        out_shape=_sds(d_qkv.shape, d_qkv.dtype), input_output_aliases={0: 0},
        compiler_params=_params(("parallel",)))(d_qkv, dkv, *rope)


def ple_bwd_elem(d_out, pp, gg, *, tm):
    T, D = d_out.shape

    def body(d_ref, pp_ref, gg_ref, dpp_ref, dgg_ref):
        d = d_ref[...]
        sg = _sigmoid(gg_ref[...].astype(F32))
        dpp_ref[...] = (d * sg).astype(dpp_ref.dtype)
        dgg_ref[...] = (d * pp_ref[...].astype(F32) * sg * (1.0 - sg)).astype(dgg_ref.dtype)

    tile = pl.BlockSpec((tm, D), lambda i: (i, 0))
    return pl.pallas_call(
        body, name="ple_bwd_elem", grid=(T // tm,), in_specs=[tile, tile, tile], out_specs=[tile, tile],
        out_shape=[_sds((T, D), BF16), _sds((T, D), BF16)], compiler_params=_params(("parallel",)))(d_out, pp, gg)


def adamw(name, recv, w, m, v, *, ta):
    L, a, b = w.shape
    assert a % ta == 0
    c1 = 1.0 - ADAM_B1 ** ADAM_STEP
    c2 = 1.0 - ADAM_B2 ** ADAM_STEP

    def body(r_ref, w_ref, m_ref, v_ref, g_ref, d_ref, nm_ref, nv_ref):
        g = r_ref[0].astype(F32)
        for s in range(1, N_DEV):
            g = g + r_ref[s].astype(F32)
        nm = ADAM_B1 * m_ref[...] + (1.0 - ADAM_B1) * g
        nv = ADAM_B2 * v_ref[...] + (1.0 - ADAM_B2) * jnp.square(g)
        m_hat = nm / c1
        v_hat = nv / c2
        g_ref[...] = g
        d_ref[...] = -ADAM_LR * (m_hat / (jnp.sqrt(v_hat) + ADAM_EPS) + ADAM_WD * w_ref[...])
        nm_ref[...] = nm
        nv_ref[...] = nv

    blk = pl.BlockSpec((None, ta, b), lambda l, i: (l, i, 0))
    out = _sds((L, a, b), F32)
    return pl.pallas_call(
        body, name=name, grid=(L, a // ta),
        in_specs=[pl.BlockSpec((N_DEV, None, ta, b), lambda l, i: (0, l, i, 0)), blk, blk, blk],
        out_specs=[blk, blk, blk, blk], out_shape=[out, out, out, out],
        compiler_params=_params(("parallel", "parallel")))(recv, w, m, v)


def _my_place():
    x, y, c = lax.axis_index("x"), lax.axis_index("y"), lax.axis_index("c")
    return x, y, c, 4 * x + 2 * y + c


def _peers(x, y, c):
    out = []
    for dx in (0, 1):
        for dy in (0, 1):
            for dc in (0, 1):
                if dx or dy or dc:
                    px, py, pc = x ^ dx, y ^ dy, c ^ dc
                    out.append(((px, py, pc), 4 * px + 2 * py + pc))
    return out


def all_gather(shards):
    n = len(shards)
    any_spec = pl.BlockSpec(memory_space=pl.ANY)

    def body(*refs):
        ins, outs = refs[:n], refs[n:2 * n]
        send_sems, recv_sems, local_sems = refs[2 * n:]
        x, y, c, me = _my_place()
        local = []
        for k in range(n):
            mine = outs[k].at[:, pl.ds(me, 1)]
            cp = pltpu.make_async_copy(ins[k], mine, local_sems.at[k])
            cp.start()
            local.append(cp)
            for peer, _ in _peers(x, y, c):
                pltpu.make_async_remote_copy(src_ref=ins[k], dst_ref=mine, send_sem=send_sems.at[k],
                                             recv_sem=recv_sems.at[k], device_id=peer, device_id_type=MESH_ID).start()
        for k in range(n):
            local[k].wait()
            seven = outs[k].at[:, pl.ds(0, N_DEV - 1)]
            pltpu.make_async_remote_copy(src_ref=seven, dst_ref=seven, send_sem=send_sems.at[k],
                                         recv_sem=recv_sems.at[k], device_id=(x, y, c), device_id_type=MESH_ID).wait()

    return pl.pallas_call(
        body, name="all_gather_weights", in_specs=[any_spec] * n, out_specs=[any_spec] * n,
        out_shape=[_sds((s.shape[0], N_DEV) + s.shape[2:], s.dtype) for s in shards],
        scratch_shapes=[pltpu.SemaphoreType.DMA((n,)), pltpu.SemaphoreType.DMA((n,)), pltpu.SemaphoreType.DMA((n,))],
    )(*shards)


def exchange_grads(pieces, outs_meta):
    n_p, n_o = len(pieces), len(outs_meta)
    any_spec = pl.BlockSpec(memory_space=pl.ANY)

    def body(*refs):
        ins, outs = refs[:n_p], refs[n_p:n_p + n_o]
        send_sems, recv_sems, local_sems = refs[n_p + n_o:]
        x, y, c, me = _my_place()
        local = []
        for q, (_, k, l) in enumerate(pieces):
            here = outs[k].at[pl.ds(me, 1), l]
            cp = pltpu.make_async_copy(ins[q].at[pl.ds(me, 1)], here, local_sems.at[q])
            cp.start()
            local.append(cp)
            for peer, idx in _peers(x, y, c):
                pltpu.make_async_remote_copy(src_ref=ins[q].at[pl.ds(idx, 1)], dst_ref=here, send_sem=send_sems.at[k],
                                             recv_sem=recv_sems.at[k], device_id=peer, device_id_type=MESH_ID).start()
        for cp in local:
            cp.wait()
        for k in range(n_o):
            seven = outs[k].at[pl.ds(0, N_DEV - 1)]
            pltpu.make_async_remote_copy(src_ref=seven, dst_ref=seven, send_sem=send_sems.at[k],
                                         recv_sem=recv_sems.at[k], device_id=(x, y, c), device_id_type=MESH_ID).wait()

    return pl.pallas_call(
        body, name="exchange_grads", in_specs=[any_spec] * n_p, out_specs=[any_spec] * n_o,
        out_shape=[_sds((N_DEV, L, a, b), dt) for (L, a, b, dt) in outs_meta],
        scratch_shapes=[pltpu.SemaphoreType.DMA((n_o,)), pltpu.SemaphoreType.DMA((n_o,)),
                        pltpu.SemaphoreType.DMA((n_p,))],
    )(*[p[0] for p in pieces])


def _pack_rows(parts):
    out = []
    for arr, rows in parts:
        arr = arr.reshape(-1, LANES).astype(F32)
        out.append(jnp.pad(arr, ((0, rows - arr.shape[0]), (0, 0))))
    return jnp.concatenate(out, axis=0)


def _small_rows(a, prefix):
    return _pack_rows([(a[prefix + name], rows) for name, rows in SMALL_SHARDED])


def _unpack_small(packed, a):
    out, r0 = {}, 0
    for name, rows in SMALL_SHARDED:
        shape = a[name].shape
        used = math.prod(shape) // LANES
        out[name] = packed[r0:r0 + used].reshape(shape)
        r0 += rows
    return out


def _rep_rows(a, prefix):
    parts = []
    for name in REPLICATED:
        arr = a[prefix + name]
        if arr.size % LANES:
            arr = jnp.pad(arr.reshape(1, -1), ((0, 0), (0, LANES - arr.size % LANES)))
        rows = -(-arr.size // LANES)
        parts.append((arr, -(-rows // SUBLANES) * SUBLANES))
    return _pack_rows(parts)


def _unpack_rep(packed, a):
    out, r0 = {}, 0
    for name in REPLICATED:
        shape = a[name].shape
        size = math.prod(shape)
        rows = -(-size // LANES)
        out[name] = packed[r0:r0 + rows].reshape(-1)[:size].reshape(shape)
        r0 += -(-rows // SUBLANES) * SUBLANES
    return out


def _step(a):
    x = a['x'][0]
    T, D = x.shape
    tgt = a['loss_target'][0]
    p_in = [a['p'][i, 0] for i in range(DEPTH)]
    PLE = p_in[0].shape[1]
    n_heads = a['attn_sinks'].shape[1]
    HD = n_heads * HEAD_DIM
    KVD = a['kv_w_k'].shape[1]
    n_kv = KVD // HEAD_DIM
    F = a['mlp_w_down'].shape[1] * N_DEV
    tm = min(TOKEN_TILE, T)
    tmc = min(256, T)
    alpha = DEEPNORM_ALPHA

    def shard3(w):
        return w.reshape((1,) + w.shape) if w.ndim == 2 else w

    shards = [shard3(a[nm]).astype(BF16) for nm in BIG_WEIGHTS] + [_small_rows(a, '')[None]]
    gathered = all_gather([s.reshape(s.shape[0], 1, s.shape[1], s.shape[2]) for s in shards])
    G = dict(zip(BIG_WEIGHTS, gathered[:-1]))
    small_full = gathered[-1][0]
    r0, small = 0, {}
    for name, rows in SMALL_SHARDED:
        small[name] = small_full[:, r0:r0 + rows]
        r0 += rows
    b_in = small['conv_b_in'][:, 0:2 * D // N_DEV // LANES].reshape(1, 2 * D)
    w_dw = jnp.transpose(small['conv_w_dw'], (1, 0, 2)).reshape(CONV_HALO, D)
    b_dw, cln_g, cln_b, b_out = (small[nm][:, 0].reshape(1, D) for nm in
                                 ('conv_b_dw', 'conv_ln_g', 'conv_ln_b', 'conv_b_out'))

    W_in = G['conv_w_in'][0]
    W_out = G['conv_w_out'][0].reshape(1, D, D)
    W_qkv = jnp.concatenate([G['attn_w_q'][0].reshape(D, HD), G['kv_w_k'][0].reshape(D, KVD),
                             G['kv_w_v'][0].reshape(D, KVD)], axis=1)[None]
    W_o = G['attn_w_o'][0].reshape(1, HD, D)
    W_up = [G['mlp_w_up'][i] for i in range(DEPTH)]
    W_down = [G['mlp_w_down'][i].reshape(1, F, D) for i in range(DEPTH)]
    W_proj = [jnp.transpose(G['ple_w_proj'][i], (1, 0, 2)).reshape(1, PLE, D) for i in range(DEPTH)]
    W_gate = [G['ple_w_gate'][i].reshape(1, D, D) for i in range(DEPTH)]
    mix_g, mix_b, mlp_g, mlp_b = a['mix_ln_g'], a['mix_ln_b'], a['mlp_ln_g'], a['mlp_ln_b']
    rope = _rope_tables(T)

    def row(v, i):
        return v[i:i + 1]

    def res_ln_epi(coef):
        def epi(accs, ex, out, i):
            acc = accs[0] if isinstance(accs, list) else accs
            n_ex = len(ex)
            res_ref, g_ref, b_ref = ex[n_ex - 3], ex[n_ex - 2], ex[n_ex - 1]
            z = coef * res_ref[...] + acc
            if n_ex == 4:
                z = z + ex[0][...]
            out[0][...] = z
            out[1][...] = _ln_fwd(z, g_ref[...], b_ref[...])
        return epi

    def mlp_fwd(li, xin):
        def up_epi(accs, ex, out, i):
            u = accs[0]
            out[0][...] = u.astype(BF16)
            out[1][...] = jnp.square(jnp.maximum(u, 0.0)).astype(BF16)
        u, act = mm_nn(f"mlp_up_{li}", [(xin, W_up[li], 0)], [], [(_sds((T, F), BF16), 'tile')] * 2, up_epi,
                       tm=tm, tn=min(512, F // N_DEV))
        z, xo = mm_nn(f"mlp_down_{li}", [(act, W_down[li], 0)],
                      [(xin, 'tile'), (row(mlp_g, li), 'row'), (row(mlp_b, li), 'row')],
                      [(_sds((T, D), F32), 'tile')] * 2, res_ln_epi(alpha), tm=tm, tn=D, tk=min(1024, F))
        return u, act, z, xo

    def ple_fwd(li, xin, with_loss):
        def epi(accs, ex, out, i):
            pp, gg = accs
            xo = ex[0][...] + pp * _sigmoid(gg)
            out[1][...] = pp.astype(BF16)
            out[2][...] = gg.astype(BF16)
            if with_loss:
                err = xo - ex[1][...]
                out[0][...] = err * (1.0 / D)
                _init_or_add(out[3], i, jnp.sum(err * err, axis=0, keepdims=True) * (0.5 / D))
            else:
                out[0][...] = xo
        extras = [(xin, 'tile')] + ([(tgt, 'tile')] if with_loss else [])
        outs = [(_sds((T, D), F32), 'tile'), (_sds((T, D), BF16), 'tile'), (_sds((T, D), BF16), 'tile')]
        if with_loss:
            outs.append((_sds((1, D), F32), 'rowacc'))
        return mm_nn(f"ple_{li}", [(p_in[li], W_proj[li], 0), (xin, W_gate[li], 0)], extras, outs, epi, tm=tm, tn=D)

    assert D // N_DEV == LANES
    g0, ha0, hg0 = _glu(x, W_in, b_in, T, D, tm)
    c0, s0 = dwconv_fwd(g0, w_dw, b_dw, cln_g, cln_b, tm=tmc)
    z1, x1 = mm_nn("conv_out", [(s0, W_out, 0)],
                   [(b_out, 'row'), (x, 'tile'), (row(mix_g, 0), 'row'), (row(mix_b, 0), 'row')],
                   [(_sds((T, D), F32), 'tile')] * 2, res_ln_epi(alpha), tm=tm, tn=D)
    u0, act0, z2, x2 = mlp_fwd(0, x1)
    x3, pp0, gg0 = ple_fwd(0, x2, False)

    def qkv_epi(accs, ex, out, i):
        t = accs[0]
        c, su, sd = ex[0][...], ex[1][...], ex[2][...]
        out[0][:, 0:HD + KVD] = _rope_wide(t[:, 0:HD + KVD], c, su, sd, _rope_chunk).astype(BF16)
        out[0][:, HD + KVD:] = t[:, HD + KVD:].astype(BF16)
    NQ = HD + 2 * KVD
    qkv, = mm_nn("qkv_rope", [(x3, W_qkv, 0)], [(t, 'tab') for t in rope], [(_sds((T, NQ), BF16), 'tile')], qkv_epi,
                 tm=tm, tn=NQ)
    o1, lse1 = attn_fwd(qkv, a['attn_sinks'], n_heads, n_kv)
    z3, x4 = mm_nn("attn_out", [(o1, W_o, 0)], [(x3, 'tile'), (row(mix_g, 1), 'row'), (row(mix_b, 1), 'row')],
                   [(_sds((T, D), F32), 'tile')] * 2, res_ln_epi(alpha), tm=tm, tn=D)
    u1, act1, z4, x5 = mlp_fwd(1, x4)
    dy, pp1, gg1, loss_row = ple_fwd(1, x5, True)
    loss_local = jnp.sum(loss_row)

    grads = {}

    def ln_bwd_epi(coef, with_colsum):
        def epi(acc, ex, out, i):
            d_x = acc + coef * ex[0][...]
            dz, dg, db = _ln_bwd(d_x, ex[1][...], ex[2][...])
            out[0][...] = dz
            out[1][...] = dz.astype(BF16)
            _init_or_add(out[2], i, dg)
            _init_or_add(out[3], i, db)
            if with_colsum:
                _init_or_add(out[4], i, jnp.sum(dz, axis=0, keepdims=True))
        return epi

    def ln_bwd_outs(with_colsum):
        outs = [(_sds((T, D), F32), 'tile'), (_sds((T, D), BF16), 'tile'), (_sds((1, D), F32), 'rowacc'),
                (_sds((1, D), F32), 'rowacc')]
        return outs + ([(_sds((1, D), F32), 'rowacc')] if with_colsum else [])

    def ple_bwd(li, d_out, xin, pp, gg, z_mlp):
        d_pp, d_gg = ple_bwd_elem(d_out, pp, gg, tm=tm)
        grads[('ple_w_proj', li)] = mm_tn(f"d_ple_proj_{li}", p_in[li], d_pp, D, F32, tm=tm, tk=PLE, tn=min(512, D))
        grads[('ple_w_gate', li)] = mm_tn(f"d_ple_gate_{li}", xin, d_gg, D, BF16, tm=tm, tk=min(512, D), tn=D)
        dz, dzb, dg, db = mm_nt(f"ple_dx_{li}", d_gg, W_gate[li],
                                [(d_out, 'tile'), (z_mlp, 'tile'), (row(mlp_g, li), 'row')],
                                ln_bwd_outs(False), ln_bwd_epi(1.0, False), tm=tmc, tko=D, tc=min(512, D))
        grads[('mlp_ln_g', li)], grads[('mlp_ln_b', li)] = dg, db
        return dz, dzb

    def mlp_bwd(li, dz, dzb, xin, u, act, z_mix, with_colsum):
        grads[('mlp_w_down', li)] = mm_tn(f"d_mlp_down_{li}", act, dzb, D, BF16, tm=tm, tk=min(512, F), tn=D)

        def du_epi(acc, ex, out, i):
            out[0][...] = (acc * (2.0 * jnp.maximum(ex[0][...].astype(F32), 0.0))).astype(BF16)
        du, = mm_nt(f"mlp_du_{li}", dzb, W_down[li], [(u, 'tile')], [(_sds((T, F), BF16), 'tile')], du_epi,
                    tm=tm, tko=min(1024, F), tc=D)
        grads[('mlp_w_up', li)] = mm_tn(f"d_mlp_up_{li}", xin, du, F // N_DEV, BF16, tm=tm, tk=min(512, D),
                                        tn=F // N_DEV)
        res = mm_nt(f"mlp_dx_{li}", du, W_up[li], [(dz, 'tile'), (z_mix, 'tile'), (row(mix_g, li), 'row')],
                    ln_bwd_outs(with_colsum), ln_bwd_epi(alpha, with_colsum), tm=tmc, tko=D, tc=F // N_DEV)
        grads[('mix_ln_g', li)], grads[('mix_ln_b', li)] = res[2], res[3]
        return res

    dz4, dz4b = ple_bwd(1, dy, x5, pp1, gg1, z4)
    dz3, dz3b, _, _ = mlp_bwd(1, dz4, dz4b, x4, u1, act1, z3, False)
    grads['attn_w_o'] = mm_tn("d_attn_wo", o1, dz3b, D, BF16, tm=tm, tk=min(512, HD), tn=D)

    def do_epi(acc, ex, out, i):
        out[0][...] = acc.astype(BF16)
    do1, = mm_nt("attn_do", dz3b, W_o, [], [(_sds((T, HD), BF16), 'tile')], do_epi, tm=tm, tko=HD, tc=D)
    d_qkv, dkv, d_sinks = attn_bwd(qkv, do1, lse1, a['attn_sinks'], rope, n_heads, n_kv)
    d_qkv = dkv_finish(d_qkv, dkv, rope, HD, KVD, tm=tm)
    d_wqkv = mm_tn("d_wqkv", x3, d_qkv, NQ, F32, tm=tm, tk=min(512, D), tn=2 * KVD)[0]

    def dx3_epi(acc, ex, out, i):
        out[0][...] = acc + alpha * ex[0][...]
    dx3, = mm_nt("attn_dx", d_qkv, W_qkv, [(dz3, 'tile')], [(_sds((T, D), F32), 'tile')], dx3_epi,
                 tm=tm, tko=D, tc=2 * KVD)

    dz2, dz2b = ple_bwd(0, dx3, x2, pp0, gg0, z2)
    dz1, dz1b, _, _, db_out = mlp_bwd(0, dz2, dz2b, x1, u0, act0, z1, True)
    grads['conv_w_out'] = mm_tn("d_conv_wout", s0, dz1b, D, BF16, tm=tm, tk=min(512, D), tn=D)

    def ds_epi(acc, ex, out, i):
        n = _ln_fwd(ex[0][...], ex[1][...], ex[2][...])
        sg = _sigmoid(n)
        dn = acc * (sg * (1.0 + n * (1.0 - sg)))
        dc, dg, db = _ln_bwd(dn, ex[0][...], ex[1][...])
        out[0][...] = dc
        _init_or_add(out[1], i, dg)
        _init_or_add(out[2], i, db)
    dc0, d_cln_g, d_cln_b = mm_nt("conv_ds", dz1b, W_out, [(c0, 'tile'), (cln_g, 'row'), (cln_b, 'row')],
                                  [(_sds((T, D), F32), 'tile'), (_sds((1, D), F32), 'rowacc'),
                                   (_sds((1, D), F32), 'rowacc')], ds_epi, tm=tmc, tko=D, tc=min(512, D))
    dh0, d_wdw, d_bdw, d_bin = dwconv_bwd(dc0, g0, ha0, hg0, w_dw, tm=tmc)
    grads['conv_w_in'] = mm_tn("d_conv_win", x, dh0, 2 * D // N_DEV, BF16, tm=tm, tk=min(512, D), tn=2 * D // N_DEV)

    def dx_epi(acc, ex, out, i):
        out[0][...] = acc + alpha * ex[0][...]
    grad_x, = mm_nt("conv_dx", dh0, W_in, [(dz1, 'tile')], [(_sds((T, D), F32), 'tile')], dx_epi,
                    tm=tm, tko=D, tc=2 * D // N_DEV)

    pieces, metas, out_names = [], [], []

    def add_output(name, L, shape2, dtype):
        metas.append((L, shape2[0], shape2[1], dtype))
        out_names.append(name)
        return len(metas) - 1

    def dev_major(arr, n_rows):
        return arr.reshape(N_DEV, n_rows, arr.shape[-1])

    k = add_output('conv_w_in', 1, (D, 2 * D // N_DEV), BF16)
    pieces.append((grads['conv_w_in'], k, 0))
    k = add_output('conv_w_out', 1, (D // N_DEV, D), BF16)
    pieces.append((dev_major(grads['conv_w_out'], D // N_DEV), k, 0))
    k = add_output('kv_w_k', 1, (D // N_DEV, KVD), F32)
    pieces.append((dev_major(d_wqkv[:, HD:HD + KVD], D // N_DEV), k, 0))
    k = add_output('kv_w_v', 1, (D // N_DEV, KVD), F32)
    pieces.append((dev_major(d_wqkv[:, HD + KVD:], D // N_DEV), k, 0))
    k = add_output('attn_w_q', 1, (D // N_DEV, HD), F32)
    pieces.append((dev_major(d_wqkv[:, 0:HD], D // N_DEV), k, 0))
    k = add_output('attn_w_o', 1, (HD // N_DEV, D), BF16)
    pieces.append((dev_major(grads['attn_w_o'], HD // N_DEV), k, 0))
    k = add_output('mlp_w_up', DEPTH, (D, F // N_DEV), BF16)
    pieces += [(grads[('mlp_w_up', li)], k, li) for li in range(DEPTH)]
    k = add_output('mlp_w_down', DEPTH, (F // N_DEV, D), BF16)
    pieces += [(dev_major(grads[('mlp_w_down', li)], F // N_DEV), k, li) for li in range(DEPTH)]
    k = add_output('ple_w_proj', DEPTH, (PLE, D // N_DEV), F32)
    pieces += [(jnp.transpose(grads[('ple_w_proj', li)][0].reshape(PLE, N_DEV, D // N_DEV), (1, 0, 2)), k, li)
               for li in range(DEPTH)]
    k = add_output('ple_w_gate', DEPTH, (D // N_DEV, D), BF16)
    pieces += [(dev_major(grads[('ple_w_gate', li)], D // N_DEV), k, li) for li in range(DEPTH)]

    def own_rows(vec, rows_used, rows):
        arr = vec.reshape(N_DEV, rows_used, LANES)
        return jnp.pad(arr, ((0, 0), (0, rows - rows_used), (0, 0)))
    dwdw_dev = jnp.transpose(d_wdw.reshape(CONV_HALO, N_DEV, D // N_DEV), (1, 0, 2))
    lane_rows = D // N_DEV // LANES
    small_grad = jnp.concatenate([
        own_rows(d_bin, 2 * lane_rows, 8), dwdw_dev if lane_rows == 1 else dwdw_dev.reshape(N_DEV, -1, LANES),
        own_rows(d_bdw, lane_rows, 8), own_rows(d_cln_g, lane_rows, 8), own_rows(d_cln_b, lane_rows, 8),
        own_rows(db_out, lane_rows, 8)], axis=1)
    n_small = small_grad.shape[1]
    k_small = add_output('__small__', 1, (n_small, LANES), F32)
    pieces.append((small_grad, k_small, 0))

    rep_local = {'mix_ln_g': jnp.concatenate([grads[('mix_ln_g', li)] for li in range(DEPTH)], axis=0),
                 'mix_ln_b': jnp.concatenate([grads[('mix_ln_b', li)] for li in range(DEPTH)], axis=0),
                 'mlp_ln_g': jnp.concatenate([grads[('mlp_ln_g', li)] for li in range(DEPTH)], axis=0),
                 'mlp_ln_b': jnp.concatenate([grads[('mlp_ln_b', li)] for li in range(DEPTH)], axis=0),
                 'attn_sinks': d_sinks}
    rep_grad = _rep_rows(rep_local, '')
    n_rep = rep_grad.shape[0]
    k_rep = add_output('__rep__', 1, (n_rep, LANES), F32)
    pieces.append((jnp.broadcast_to(rep_grad[None], (N_DEV, n_rep, LANES)), k_rep, 0))

    received = exchange_grads(pieces, metas)

    result = {}
    for name, recv in zip(out_names, received):
        if name == '__small__':
            w, m, v = (_small_rows(a, pre)[None] for pre in ('', 'm_', 'v_'))
            outs4 = adamw("adamw_small", recv, w, m, v, ta=n_small)
            for kind, arr in zip(('grad', 'delta', 'new_m', 'new_v'), outs4):
                for pname, val in _unpack_small(arr[0], a).items():
                    result[(kind, pname)] = val
        elif name == '__rep__':
            w, m, v = (_rep_rows(a, pre)[None] for pre in ('', 'm_', 'v_'))
            outs4 = adamw("adamw_rep", recv, w, m, v, ta=n_rep)
            for kind, arr in zip(('grad', 'delta', 'new_m', 'new_v'), outs4):
                for pname, val in _unpack_rep(arr[0], a).items():
                    result[(kind, pname)] = val
        else:
            w, m, v = (shard3(a[pre + name]) for pre in ('', 'm_', 'v_'))
            rows = w.shape[1]
            outs4 = adamw("adamw_" + name, recv, w, m, v, ta=min(256, rows))
            for kind, arr in zip(('grad', 'delta', 'new_m', 'new_v'), outs4):
                result[(kind, name)] = arr.reshape(a[name].shape)

    loss = lax.psum(loss_local, ("x", "y", "c"))
    out = [loss, grad_x[None]]
    for kind in ('grad', 'delta', 'new_m', 'new_v'):
        out += [result[(kind, name)] for name in WEIGHT_NAMES]
    return tuple(out)


def _glu(x, W_in, b_in, T, D, tm):
    n = W_in.shape[2]
    nt = D // n

    def body(x_ref, wa_ref, wg_ref, ba_ref, bg_ref, g_ref, ha_ref, hg_ref):
        xb = x_ref[...]
        ha = _dot(xb, wa_ref[...], ((1,), (0,))) + ba_ref[...]
        hg = _dot(xb, wg_ref[...], ((1,), (0,))) + bg_ref[...]
        g_ref[...] = ha * _sigmoid(hg)
        ha_ref[...] = ha.astype(ha_ref.dtype)
        hg_ref[...] = hg.astype(hg_ref.dtype)

    return pl.pallas_call(
        body, name="conv_in_glu", grid=(T // tm, nt),
        in_specs=[pl.BlockSpec((tm, D), lambda i, j: (i, 0)),
                  pl.BlockSpec((None, D, n), lambda i, j: (j, 0, 0)),
                  pl.BlockSpec((None, D, n), lambda i, j: (j + nt, 0, 0)),
                  pl.BlockSpec((1, n), lambda i, j: (0, j)), pl.BlockSpec((1, n), lambda i, j: (0, j + nt))],
        out_specs=[pl.BlockSpec((tm, n), lambda i, j: (i, j))] * 3,
        out_shape=[_sds((T, D), F32), _sds((T, D), BF16), _sds((T, D), BF16)],
        compiler_params=_params(("parallel", "parallel")))(x, W_in, W_in, b_in, b_in)


def kernel(x, p, conv_w_in, conv_b_in, conv_w_dw, conv_b_dw, conv_ln_g, conv_ln_b, conv_w_out, conv_b_out, kv_w_k, kv_w_v, attn_w_q, attn_sinks, attn_w_o, mix_ln_g, mix_ln_b, mlp_w_up, mlp_w_down, mlp_ln_g, mlp_ln_b, ple_w_proj, ple_w_gate, loss_target, m_conv_w_in, m_conv_b_in, m_conv_w_dw, m_conv_b_dw, m_conv_ln_g, m_conv_ln_b, m_conv_w_out, m_conv_b_out, m_kv_w_k, m_kv_w_v, m_attn_w_q, m_attn_sinks, m_attn_w_o, m_mix_ln_g, m_mix_ln_b, m_mlp_w_up, m_mlp_w_down, m_mlp_ln_g, m_mlp_ln_b, m_ple_w_proj, m_ple_w_gate, v_conv_w_in, v_conv_b_in, v_conv_w_dw, v_conv_b_dw, v_conv_ln_g, v_conv_ln_b, v_conv_w_out, v_conv_b_out, v_kv_w_k, v_kv_w_v, v_attn_w_q, v_attn_sinks, v_attn_w_o, v_mix_ln_g, v_mix_ln_b, v_mlp_w_up, v_mlp_w_down, v_mlp_ln_g, v_mlp_ln_b, v_ple_w_proj, v_ple_w_gate):
    return _step(dict(locals()))
```

```python
import functools
import math

import jax
import jax.numpy as jnp
from jax import lax
from jax.experimental import pallas as pl
from jax.experimental.pallas import tpu as pltpu

F32 = jnp.float32
BF16 = jnp.bfloat16

N_DEV = 8
HEAD_DIM = 64
ROPE_DIM = HEAD_DIM // 4
ROPE_HALF = ROPE_DIM // 2
ROPE_THETA = 500000.0
ATT_BLOCK = 128
CONV_WIDTH = 31
CONV_HALO = 32
LN_EPS = 1e-5
DEPTH = 2
DEEPNORM_ALPHA = (2 * DEPTH) ** 0.25
MASK_VALUE = -1e30

ADAM_LR = 0.001
ADAM_B1 = 0.9
ADAM_B2 = 0.999
ADAM_EPS = 1e-08
ADAM_WD = 0.01
ADAM_STEP = 10

LANES = 128
SUBLANES = 8
VMEM_LIMIT_BYTES = 52 * 1024 * 1024
TOKEN_TILE = 512
MESH_ID = pl.DeviceIdType.MESH

WEIGHT_NAMES = ['conv_w_in', 'conv_b_in', 'conv_w_dw', 'conv_b_dw', 'conv_ln_g', 'conv_ln_b', 'conv_w_out',
                'conv_b_out', 'kv_w_k', 'kv_w_v', 'attn_w_q', 'attn_sinks', 'attn_w_o', 'mix_ln_g', 'mix_ln_b',
                'mlp_w_up', 'mlp_w_down', 'mlp_ln_g', 'mlp_ln_b', 'ple_w_proj', 'ple_w_gate']
BIG_WEIGHTS = ['conv_w_in', 'conv_w_out', 'kv_w_k', 'kv_w_v', 'attn_w_q', 'attn_w_o', 'mlp_w_up', 'mlp_w_down',
               'ple_w_proj', 'ple_w_gate']
SMALL_SHARDED = [('conv_b_in', 8), ('conv_w_dw', 32), ('conv_b_dw', 8), ('conv_ln_g', 8), ('conv_ln_b', 8),
                 ('conv_b_out', 8)]
REPLICATED = ['mix_ln_g', 'mix_ln_b', 'mlp_ln_g', 'mlp_ln_b', 'attn_sinks']


def _params(sem):
    return pltpu.CompilerParams(dimension_semantics=sem, vmem_limit_bytes=VMEM_LIMIT_BYTES)


def _dot(a, b, dims):
    if a.dtype != BF16:
        a = a.astype(BF16)
    if b.dtype != BF16:
        b = b.astype(BF16)
    return lax.dot_general(a, b, (dims, ((), ())), preferred_element_type=F32)


def _sigmoid(v):
    return 1.0 / (1.0 + jnp.exp(-v))


def _ln_stats(z):
    mu = jnp.mean(z, axis=-1, keepdims=True)
    zc = z - mu
    var = jnp.mean(zc * zc, axis=-1, keepdims=True)
    return zc * lax.rsqrt(var + LN_EPS)


def _ln_fwd(z, g, b):
    return _ln_stats(z) * g + b


def _ln_bwd(dy, z, g):
    xhat = _ln_stats(z)
    mu = jnp.mean(z, axis=-1, keepdims=True)
    zc = z - mu
    rstd = lax.rsqrt(jnp.mean(zc * zc, axis=-1, keepdims=True) + LN_EPS)
    dxh = dy * g
    m1 = jnp.mean(dxh, axis=-1, keepdims=True)
    m2 = jnp.mean(dxh * xhat, axis=-1, keepdims=True)
    dz = rstd * (dxh - m1 - xhat * m2)
    return dz, jnp.sum(dy * xhat, axis=0, keepdims=True), jnp.sum(dy, axis=0, keepdims=True)


def _extra_spec(shape, kind, tm, tn, ij):
    if kind == 'tile':
        return pl.BlockSpec((tm, tn), lambda *g: (ij(g)[0], ij(g)[1]))
    if kind in ('row', 'rowacc'):
        return pl.BlockSpec((1, tn), lambda *g: (0, ij(g)[1]))
    if kind == 'tab':
        return pl.BlockSpec((tm, LANES), lambda *g: (ij(g)[0], 0))
    raise ValueError(kind)


def mm_nn(name, pairs, extras, outs, epi, *, tm, tn, tk=None):
    M = pairs[0][0].shape[0]
    N = outs[0][0].shape[1]
    n_pairs = len(pairs)
    K0 = pairs[0][0].shape[1]
    tk = K0 if tk is None else tk
    nk = K0 // tk
    assert nk == 1 or n_pairs == 1
    assert M % tm == 0 and N % tn == 0 and K0 % tk == 0
    has_rowacc = any(kind == 'rowacc' for _, kind in outs)
    assert not has_rowacc or (N == tn and nk == 1)
    in_specs, operands, slabs = [], [], []
    for a, b, off in pairs:
        K = a.shape[1]
        ktile = K if n_pairs > 1 else tk
        n = b.shape[2]
        assert b.shape[1] == K
        in_specs.append(pl.BlockSpec((tm, ktile), lambda i, j, k: (i, k)))
        if tn <= n:
            assert n % tn == 0
            r = n // tn
            in_specs.append(pl.BlockSpec((None, ktile, tn),
                                         lambda i, j, k, r=r, off=off: ((j + off) // r, k, (j + off) % r)))
            slabs.append(0)
        else:
            assert tn % n == 0
            in_specs.append(pl.BlockSpec((tn // n, ktile, n), lambda i, j, k, off=off: (j + off, k, 0)))
            slabs.append(tn // n)
        operands += [a, b]
    ij = lambda g: (g[0], g[1])
    for arr, kind in extras:
        in_specs.append(_extra_spec(arr.shape, kind, tm, tn, ij))
        operands.append(arr)
    out_specs = [_extra_spec(o.shape, kind, tm, tn, ij) for o, kind in outs]
    n_ex, n_out = len(extras), len(outs)

    def pair_dot(ab, q):
        a = ab[2 * q][...]
        if not slabs[q]:
            return _dot(a, ab[2 * q + 1][...], ((1,), (0,)))
        return jnp.concatenate([_dot(a, ab[2 * q + 1][s], ((1,), (0,))) for s in range(slabs[q])], axis=1)

    def body(*refs):
        ab = refs[:2 * n_pairs]
        ex = refs[2 * n_pairs:2 * n_pairs + n_ex]
        out = refs[2 * n_pairs + n_ex:2 * n_pairs + n_ex + n_out]
        i = pl.program_id(0)
        if nk == 1:
            accs = [pair_dot(ab, q) for q in range(n_pairs)]
            epi(accs, ex, out, i)
        else:
            acc_ref = refs[-1]
            k = pl.program_id(2)

            @pl.when(k == 0)
            def _():
                acc_ref[...] = jnp.zeros_like(acc_ref)

            acc_ref[...] += pair_dot(ab, 0)

            @pl.when(k == nk - 1)
            def _():
                epi([acc_ref[...]], ex, out, i)

    scratch = [pltpu.VMEM((tm, tn), F32)] if nk > 1 else []
    sem = ("arbitrary",) * 3 if has_rowacc else ("parallel", "parallel", "arbitrary")
    res = pl.pallas_call(
        body, name=name, grid=(M // tm, N // tn, nk), in_specs=in_specs, out_specs=out_specs,
        out_shape=[o for o, _ in outs], scratch_shapes=scratch, compiler_params=_params(sem))(*operands)
    return res


def mm_nt(name, a, b, extras, outs, epi, *, tm, tko, tc):
    M, N = a.shape
    J, K, n = b.shape
    assert J * n == N and M % tm == 0 and K % tko == 0 and N % tc == 0
    nc = N // tc
    has_rowacc = any(kind == 'rowacc' for _, kind in outs)
    assert not has_rowacc or K == tko
    if tc <= n:
        assert n % tc == 0
        r = n // tc
        slabs = 0
        b_spec = pl.BlockSpec((None, tko, tc), lambda i, j, c: (c // r, j, c % r))
    else:
        assert tc % n == 0
        slabs = tc // n
        b_spec = pl.BlockSpec((slabs, tko, n), lambda i, j, c: (c, j, 0))
    in_specs = [pl.BlockSpec((tm, tc), lambda i, j, c: (i, c)), b_spec]

    def nt_dot(a_ref, b_ref):
        if not slabs:
            return _dot(a_ref[...], b_ref[...], ((1,), (1,)))
        acc = _dot(a_ref[:, 0:n], b_ref[0], ((1,), (1,)))
        for s in range(1, slabs):
            acc = acc + _dot(a_ref[:, s * n:(s + 1) * n], b_ref[s], ((1,), (1,)))
        return acc

    ij = lambda g: (g[0], g[1])
    operands = [a, b]
    for arr, kind in extras:
        in_specs.append(_extra_spec(arr.shape, kind, tm, tko, ij))
        operands.append(arr)
    out_specs = [_extra_spec(o.shape, kind, tm, tko, ij) for o, kind in outs]
    n_ex, n_out = len(extras), len(outs)

    def body(*refs):
        a_ref, b_ref = refs[:2]
        ex = refs[2:2 + n_ex]
        out = refs[2 + n_ex:2 + n_ex + n_out]
        i = pl.program_id(0)
        if nc == 1:
            epi(nt_dot(a_ref, b_ref), ex, out, i)
        else:
            acc_ref = refs[-1]
            c = pl.program_id(2)

            @pl.when(c == 0)
            def _():
                acc_ref[...] = jnp.zeros_like(acc_ref)

            acc_ref[...] += nt_dot(a_ref, b_ref)

            @pl.when(c == nc - 1)
            def _():
                epi(acc_ref[...], ex, out, i)

    scratch = [pltpu.VMEM((tm, tko), F32)] if nc > 1 else []
    sem = ("arbitrary",) * 3 if has_rowacc else ("parallel", "parallel", "arbitrary")
    return pl.pallas_call(
        body, name=name, grid=(M // tm, K // tko, nc), in_specs=in_specs, out_specs=out_specs,
        out_shape=[o for o, _ in outs], scratch_shapes=scratch, compiler_params=_params(sem))(*operands)


def mm_tn(name, a, d, n, out_dtype, *, tm, tk, tn):
    M, K = a.shape
    N = d.shape[1]
    assert d.shape[0] == M and N % n == 0 and N % tn == 0 and K % tk == 0 and M % tm == 0
    nm = M // tm
    if tn <= n:
        assert n % tn == 0
        r = n // tn
        slabs = 0
        o_spec = pl.BlockSpec((None, tk, tn), lambda kk, j, m: (j // r, kk, j % r))
    else:
        assert tn % n == 0
        slabs = tn // n
        o_spec = pl.BlockSpec((slabs, tk, n), lambda kk, j, m: (j, kk, 0))

    def body(a_ref, d_ref, o_ref, acc_ref):
        m = pl.program_id(2)

        @pl.when(m == 0)
        def _():
            acc_ref[...] = jnp.zeros_like(acc_ref)

        acc_ref[...] += _dot(a_ref[...], d_ref[...], ((0,), (0,)))

        @pl.when(m == nm - 1)
        def _():
            if not slabs:
                o_ref[...] = acc_ref[...].astype(o_ref.dtype)
            else:
                for s in range(slabs):
                    o_ref[s] = acc_ref[:, s * n:(s + 1) * n].astype(o_ref.dtype)

    return pl.pallas_call(
        body, name=name, grid=(K // tk, N // tn, nm),
        in_specs=[pl.BlockSpec((tm, tk), lambda kk, j, m: (m, kk)), pl.BlockSpec((tm, tn), lambda kk, j, m: (m, j))],
        out_specs=o_spec,
        out_shape=jax.ShapeDtypeStruct((N // n, K, n), out_dtype),
        scratch_shapes=[pltpu.VMEM((tk, tn), F32)],
        compiler_params=_params(("parallel", "parallel", "arbitrary")))(a, d)


def _sds(shape, dtype):
    return jax.ShapeDtypeStruct(shape, dtype)


def _init_or_add(ref, i, value):
    @pl.when(i == 0)
    def _():
        ref[...] = value

    @pl.when(i > 0)
    def _():
        ref[...] += value


def _rope_tables(T):
    pos = jnp.arange(T, dtype=F32)
    inv_freq = ROPE_THETA ** (-jnp.arange(0, ROPE_DIM, 2, dtype=F32) / ROPE_DIM)
    ang = pos[:, None] * inv_freq[None, :]
    cos, sin = jnp.cos(ang), jnp.sin(ang)
    ones = jnp.ones((T, HEAD_DIM - ROPE_DIM), F32)
    zeros = jnp.zeros((T, HEAD_DIM - ROPE_DIM), F32)
    zh = jnp.zeros((T, ROPE_HALF), F32)
    c_head = jnp.concatenate([cos, cos, ones], axis=1)
    s_up = jnp.concatenate([-sin, zh, zeros], axis=1)
    s_dn = jnp.concatenate([zh, sin, zeros], axis=1)
    rep = LANES // HEAD_DIM
    return tuple(jnp.tile(t, (1, rep)) for t in (c_head, s_up, s_dn))


def _rope_chunk(t, c, s_up, s_dn):
    return t * c + pltpu.roll(t, LANES - ROPE_HALF, 1) * s_up + pltpu.roll(t, ROPE_HALF, 1) * s_dn


def _rope_chunk_bwd(d, c, s_up, s_dn):
    return d * c + pltpu.roll(d * s_up, ROPE_HALF, 1) + pltpu.roll(d * s_dn, LANES - ROPE_HALF, 1)


def _rope_wide(t, c, s_up, s_dn, fn):
    chunks = [fn(t[:, q * LANES:(q + 1) * LANES], c, s_up, s_dn) for q in range(t.shape[1] // LANES)]
    return chunks[0] if len(chunks) == 1 else jnp.concatenate(chunks, axis=1)


def _halo_before_spec(tm, D):
    return pl.BlockSpec((CONV_HALO, D), lambda i: (jnp.maximum(i * (tm // CONV_HALO) - 1, 0), 0))


def dwconv_fwd(g, w_dw, b_dw, ln_g, ln_b, *, tm):
    T, D = g.shape
    nl = D // LANES

    def body(g_ref, gh_ref, w_ref, b_ref, lg_ref, lb_ref, c_ref, s_ref, win_ref):
        i = pl.program_id(0)
        win_ref[0:CONV_HALO, :] = jnp.where(i > 0, gh_ref[...], 0.0)
        win_ref[CONV_HALO:, :] = g_ref[...]

        def lane_chunk(q, carry):
            ls = pl.ds(pl.multiple_of(q * LANES, LANES), LANES)
            acc = jnp.broadcast_to(b_ref[:, ls], (tm, LANES))
            for k in range(CONV_WIDTH):
                acc = acc + win_ref[pl.ds(CONV_HALO - (CONV_WIDTH - 1) + k, tm), ls] * w_ref[k:k + 1, ls]
            c_ref[:, ls] = acc
            return carry

        lax.fori_loop(0, nl, lane_chunk, 0)
        n = _ln_fwd(c_ref[...], lg_ref[...], lb_ref[...])
        s_ref[...] = (n * _sigmoid(n)).astype(s_ref.dtype)

    row = pl.BlockSpec((1, D), lambda i: (0, 0))
    return pl.pallas_call(
        body, name="dwconv_fwd", grid=(T // tm,),
        in_specs=[pl.BlockSpec((tm, D), lambda i: (i, 0)), _halo_before_spec(tm, D),
                  pl.BlockSpec((CONV_HALO, D), lambda i: (0, 0)), row, row, row],
        out_specs=[pl.BlockSpec((tm, D), lambda i: (i, 0)), pl.BlockSpec((tm, D), lambda i: (i, 0))],
        out_shape=[_sds((T, D), F32), _sds((T, D), BF16)],
        scratch_shapes=[pltpu.VMEM((tm + CONV_HALO, D), F32)],
        compiler_params=_params(("parallel",)))(g, g, w_dw, b_dw, ln_g, ln_b)


def dwconv_bwd(dc, g, ha, hg, w_dw, *, tm):
    T, D = g.shape
    nl = D // LANES
    last = T // CONV_HALO - 1
    nt = T // tm

    def body(dc_ref, dcn_ref, g_ref, gh_ref, ha_ref, hg_ref, w_ref, dh_ref, dw_ref, dbdw_ref, dbin_ref,
             win_ref, dwin_ref, dg_ref):
        i = pl.program_id(0)
        win_ref[0:CONV_HALO, :] = jnp.where(i > 0, gh_ref[...], 0.0)
        win_ref[CONV_HALO:, :] = g_ref[...]
        dwin_ref[0:tm, :] = dc_ref[...]
        dwin_ref[tm:, :] = jnp.where(i < nt - 1, dcn_ref[...], 0.0)

        @pl.when(i == 0)
        def _():
            dw_ref[...] = jnp.zeros_like(dw_ref)

        def lane_chunk(q, carry):
            ls = pl.ds(pl.multiple_of(q * LANES, LANES), LANES)
            dcv = dwin_ref[0:tm, ls]
            acc = jnp.zeros((tm, LANES), F32)
            for k in range(CONV_WIDTH):
                acc = acc + dwin_ref[pl.ds(CONV_WIDTH - 1 - k, tm), ls] * w_ref[k:k + 1, ls]
                tap = win_ref[pl.ds(CONV_HALO - (CONV_WIDTH - 1) + k, tm), ls]
                dw_ref[k:k + 1, ls] += jnp.sum(dcv * tap, axis=0, keepdims=True)
            dg_ref[:, ls] = acc
            return carry

        lax.fori_loop(0, nl, lane_chunk, 0)
        dg = dg_ref[...]
        ha = ha_ref[...].astype(F32)
        sg = _sigmoid(hg_ref[...].astype(F32))
        d_ha = dg * sg
        d_hg = dg * ha * sg * (1.0 - sg)
        dh_ref[:, 0:D] = d_ha.astype(dh_ref.dtype)
        dh_ref[:, D:] = d_hg.astype(dh_ref.dtype)
        _init_or_add(dbdw_ref, i, jnp.sum(dc_ref[...], axis=0, keepdims=True))
        _init_or_add(dbin_ref, i, jnp.concatenate([jnp.sum(d_ha, axis=0, keepdims=True),
                                                   jnp.sum(d_hg, axis=0, keepdims=True)], axis=1))

    tile = pl.BlockSpec((tm, D), lambda i: (i, 0))
    return pl.pallas_call(
        body, name="dwconv_bwd", grid=(nt,),
        in_specs=[tile,
                  pl.BlockSpec((CONV_HALO, D), lambda i: (jnp.minimum((i + 1) * (tm // CONV_HALO), last), 0)),
                  tile, _halo_before_spec(tm, D),
                  tile, tile, pl.BlockSpec((CONV_HALO, D), lambda i: (0, 0))],
        out_specs=[pl.BlockSpec((tm, 2 * D), lambda i: (i, 0)), pl.BlockSpec((CONV_HALO, D), lambda i: (0, 0)),
                   pl.BlockSpec((1, D), lambda i: (0, 0)), pl.BlockSpec((1, 2 * D), lambda i: (0, 0))],
        out_shape=[_sds((T, 2 * D), BF16), _sds((CONV_HALO, D), F32), _sds((1, D), F32), _sds((1, 2 * D), F32)],
        scratch_shapes=[pltpu.VMEM((tm + CONV_HALO, D), F32), pltpu.VMEM((tm + CONV_HALO, D), F32),
                        pltpu.VMEM((tm, D), F32)],
        compiler_params=_params(("arbitrary",)))(dc, dc, g, g, ha, hg, w_dw)


def _attn_specs(HD, KVD):
    B = ATT_BLOCK
    kv_col = HD // (2 * KVD)
    return [pl.BlockSpec((B, HD), lambda n: (n, 0)),
            pl.BlockSpec((B, 2 * KVD), lambda n: (n, kv_col)),
            pl.BlockSpec((B, 2 * KVD), lambda n: (jnp.maximum(n - 1, 0), kv_col))]


def _band_mask(n, rows):
    r = lax.broadcasted_iota(jnp.int32, (rows, 2 * ATT_BLOCK), 0) % ATT_BLOCK
    j = lax.broadcasted_iota(jnp.int32, (rows, 2 * ATT_BLOCK), 1)
    return (j > r) & (j <= r + ATT_BLOCK) & ((n > 0) | (j >= ATT_BLOCK))


def _group_rows(ref_or_val, g, q_per_kv):
    return jnp.concatenate([ref_or_val[:, (g * q_per_kv + i) * HEAD_DIM:(g * q_per_kv + i + 1) * HEAD_DIM]
                            for i in range(q_per_kv)], axis=0)


def _group_col(ref, g, q_per_kv):
    return jnp.concatenate([ref[:, g * q_per_kv + i:g * q_per_kv + i + 1] for i in range(q_per_kv)], axis=0)


def attn_fwd(qkv, sinks, n_heads, n_kv):
    T = qkv.shape[0]
    HD, KVD = n_heads * HEAD_DIM, n_kv * HEAD_DIM
    B = ATT_BLOCK
    qpk = n_heads // n_kv
    scale = 1.0 / math.sqrt(HEAD_DIM)

    def body(q_ref, kvc_ref, kvp_ref, sink_ref, o_ref, lse_ref):
        n = pl.program_id(0)
        mask = _band_mask(n, qpk * B)
        for g in range(n_kv):
            ks = slice(g * HEAD_DIM, (g + 1) * HEAD_DIM)
            vs = slice(KVD + g * HEAD_DIM, KVD + (g + 1) * HEAD_DIM)
            k_band = jnp.concatenate([kvp_ref[:, ks], kvc_ref[:, ks]], axis=0)
            v_band = jnp.concatenate([kvp_ref[:, vs], kvc_ref[:, vs]], axis=0)
            q_g = _group_rows(q_ref, g, qpk)
            sink = jnp.concatenate([jnp.broadcast_to(sink_ref[:, g * qpk + i:g * qpk + i + 1], (B, 1))
                                    for i in range(qpk)], axis=0)
            s = jnp.where(mask, _dot(q_g, k_band, ((1,), (1,))) * scale, MASK_VALUE)
            m = jnp.maximum(jnp.max(s, axis=-1, keepdims=True), sink)
            e = jnp.exp(s - m)
            l = jnp.sum(e, axis=-1, keepdims=True) + jnp.exp(sink - m)
            lse = m + jnp.log(l)
            probs = jnp.exp(s - lse)
            o_g = _dot(probs, v_band, ((1,), (0,)))
            for i in range(qpk):
                h = g * qpk + i
                o_ref[:, h * HEAD_DIM:(h + 1) * HEAD_DIM] = o_g[i * B:(i + 1) * B, :].astype(o_ref.dtype)
                lse_ref[:, h:h + 1] = lse[i * B:(i + 1) * B, :]

    return pl.pallas_call(
        body, name="attn_fwd", grid=(T // B,),
        in_specs=_attn_specs(HD, KVD) + [pl.BlockSpec((1, n_heads), lambda n: (0, 0))],
        out_specs=[pl.BlockSpec((B, HD), lambda n: (n, 0)), pl.BlockSpec((B, n_heads), lambda n: (n, 0))],
        out_shape=[_sds((T, HD), BF16), _sds((T, n_heads), F32)],
        compiler_params=_params(("parallel",)))(qkv, qkv, qkv, sinks)


def attn_bwd(qkv, do, lse, sinks, rope, n_heads, n_kv):
    T = qkv.shape[0]
    HD, KVD = n_heads * HEAD_DIM, n_kv * HEAD_DIM
    B = ATT_BLOCK
    qpk = n_heads // n_kv
    scale = 1.0 / math.sqrt(HEAD_DIM)
    nb = T // B

    def body(q_ref, kvc_ref, kvp_ref, do_ref, lse_ref, sink_ref, c_ref, su_ref, sd_ref, dq_ref, dkv_ref, dsink_ref):
        n = pl.program_id(0)
        mask = _band_mask(n, qpk * B)

        @pl.when(n == 0)
        def _():
            dkv_ref[...] = jnp.zeros_like(dkv_ref)
            dsink_ref[...] = jnp.zeros_like(dsink_ref)

        dks, dvs, dqs, dsk = [], [], [None] * n_heads, []
        for g in range(n_kv):
            ks = slice(g * HEAD_DIM, (g + 1) * HEAD_DIM)
            vs = slice(KVD + g * HEAD_DIM, KVD + (g + 1) * HEAD_DIM)
            k_band = jnp.concatenate([kvp_ref[:, ks], kvc_ref[:, ks]], axis=0)
            v_band = jnp.concatenate([kvp_ref[:, vs], kvc_ref[:, vs]], axis=0)
            q_g = _group_rows(q_ref, g, qpk)
            do_g = _group_rows(do_ref, g, qpk)
            lse_g = _group_col(lse_ref, g, qpk)
            s = jnp.where(mask, _dot(q_g, k_band, ((1,), (1,))) * scale, MASK_VALUE)
            probs = jnp.exp(s - lse_g)
            dp = _dot(do_g, v_band, ((1,), (1,)))
            delta = jnp.sum(probs * dp, axis=-1, keepdims=True)
            ds = (probs * (dp - delta) * scale).astype(BF16)
            dq_g = _dot(ds, k_band, ((1,), (0,)))
            dks.append(_dot(ds, q_g, ((0,), (0,))))
            dvs.append(_dot(probs.astype(BF16), do_g, ((0,), (0,))))
            for i in range(qpk):
                h = g * qpk + i
                dqs[h] = dq_g[i * B:(i + 1) * B, :]
                p_sink = jnp.exp(sink_ref[:, h:h + 1] - lse_g[i * B:(i + 1) * B, :])
                dsk.append(-jnp.sum(p_sink * delta[i * B:(i + 1) * B, :], axis=0, keepdims=True))
        dq = jnp.concatenate(dqs, axis=1)
        dq_ref[...] = _rope_wide(dq, c_ref[...], su_ref[...], sd_ref[...], _rope_chunk_bwd).astype(dq_ref.dtype)
        dkv = jnp.concatenate(dks + dvs, axis=1)
        prev = pl.ds(pl.multiple_of(jnp.maximum(n - 1, 0) * B, B), B)
        cur = pl.ds(pl.multiple_of(n * B, B), B)
        dkv_ref[prev, :] += dkv[0:B, :]
        dkv_ref[cur, :] += dkv[B:, :]
        dsink_ref[...] += jnp.concatenate(dsk, axis=1)

    tab = pl.BlockSpec((B, LANES), lambda n: (n, 0))
    return pl.pallas_call(
        body, name="attn_bwd", grid=(nb,),
        in_specs=_attn_specs(HD, KVD) + [pl.BlockSpec((B, HD), lambda n: (n, 0)),
                                         pl.BlockSpec((B, n_heads), lambda n: (n, 0)),
                                         pl.BlockSpec((1, n_heads), lambda n: (0, 0)), tab, tab, tab],
        out_specs=[pl.BlockSpec((B, HD), lambda n: (n, 0)), pl.BlockSpec((T, 2 * KVD), lambda n: (0, 0)),
                   pl.BlockSpec((1, n_heads), lambda n: (0, 0))],
        out_shape=[_sds((T, HD + 2 * KVD), BF16), _sds((T, 2 * KVD), F32), _sds((1, n_heads), F32)],
        compiler_params=_params(("arbitrary",)))(qkv, qkv, qkv, do, lse, sinks, *rope)


def dkv_finish(d_qkv, dkv, rope, HD, KVD, *, tm):
    T = dkv.shape[0]
    kv_col = HD // (2 * KVD)

    def body(alias_ref, dkv_ref, c_ref, su_ref, sd_ref, o_ref):
        del alias_ref
        dk = _rope_wide(dkv_ref[:, 0:KVD], c_ref[...], su_ref[...], sd_ref[...], _rope_chunk_bwd)
        o_ref[:, 0:KVD] = dk.astype(o_ref.dtype)
        o_ref[:, KVD:] = dkv_ref[:, KVD:].astype(o_ref.dtype)

    tab = pl.BlockSpec((tm, LANES), lambda i: (i, 0))
    return pl.pallas_call(
        body, name="dkv_finish", grid=(T // tm,),
        in_specs=[pl.BlockSpec(memory_space=pl.ANY), pl.BlockSpec((tm, 2 * KVD), lambda i: (i, 0)), tab, tab, tab],
        out_specs=pl.BlockSpec((tm, 2 * KVD), lambda i: (i, kv_col)),
        out_shape=_sds(d_qkv.shape, d_qkv.dtype), input_output_aliases={0: 0},
        compiler_params=_params(("parallel",)))(d_qkv, dkv, *rope)


def ple_bwd_elem(d_out, pp, gg, *, tm):
    T, D = d_out.shape

    def body(d_ref, pp_ref, gg_ref, dpp_ref, dgg_ref):
        d = d_ref[...]
        sg = _sigmoid(gg_ref[...].astype(F32))
        dpp_ref[...] = (d * sg).astype(dpp_ref.dtype)
        dgg_ref[...] = (d * pp_ref[...].astype(F32) * sg * (1.0 - sg)).astype(dgg_ref.dtype)

    tile = pl.BlockSpec((tm, D), lambda i: (i, 0))
    return pl.pallas_call(
        body, name="ple_bwd_elem", grid=(T // tm,), in_specs=[tile, tile, tile], out_specs=[tile, tile],
        out_shape=[_sds((T, D), BF16), _sds((T, D), BF16)], compiler_params=_params(("parallel",)))(d_out, pp, gg)


def adamw(name, recv, w, m, v, *, ta):
    L, a, b = w.shape
    assert a % ta == 0
    c1 = 1.0 - ADAM_B1 ** ADAM_STEP
    c2 = 1.0 - ADAM_B2 ** ADAM_STEP

    def body(r_ref, w_ref, m_ref, v_ref, g_ref, d_ref, nm_ref, nv_ref):
        g = r_ref[0].astype(F32)
        for s in range(1, N_DEV):
            g = g + r_ref[s].astype(F32)
        nm = ADAM_B1 * m_ref[...] + (1.0 - ADAM_B1) * g
        nv = ADAM_B2 * v_ref[...] + (1.0 - ADAM_B2) * jnp.square(g)
        m_hat = nm / c1
        v_hat = nv / c2
        g_ref[...] = g
        d_ref[...] = -ADAM_LR * (m_hat / (jnp.sqrt(v_hat) + ADAM_EPS) + ADAM_WD * w_ref[...])
        nm_ref[...] = nm
        nv_ref[...] = nv

    blk = pl.BlockSpec((None, ta, b), lambda l, i: (l, i, 0))
    out = _sds((L, a, b), F32)
    return pl.pallas_call(
        body, name=name, grid=(L, a // ta),
        in_specs=[pl.BlockSpec((N_DEV, None, ta, b), lambda l, i: (0, l, i, 0)), blk, blk, blk],
        out_specs=[blk, blk, blk, blk], out_shape=[out, out, out, out],
        compiler_params=_params(("parallel", "parallel")))(recv, w, m, v)


def _my_place():
    x, y, c = lax.axis_index("x"), lax.axis_index("y"), lax.axis_index("c")
    return x, y, c, 4 * x + 2 * y + c


def _peers(x, y, c):
    out = []
    for dx in (0, 1):
        for dy in (0, 1):
            for dc in (0, 1):
                if dx or dy or dc:
                    px, py, pc = x ^ dx, y ^ dy, c ^ dc
                    out.append(((px, py, pc), 4 * px + 2 * py + pc))
    return out


def all_gather(shards):
    n = len(shards)
    any_spec = pl.BlockSpec(memory_space=pl.ANY)

    def body(*refs):
        ins, outs = refs[:n], refs[n:2 * n]
        send_sems, recv_sems, local_sems = refs[2 * n:]
        x, y, c, me = _my_place()
        local = []
        for k in range(n):
            mine = outs[k].at[:, pl.ds(me, 1)]
            cp = pltpu.make_async_copy(ins[k], mine, local_sems.at[k])
            cp.start()
            local.append(cp)
            for peer, _ in _peers(x, y, c):
                pltpu.make_async_remote_copy(src_ref=ins[k], dst_ref=mine, send_sem=send_sems.at[k],
                                             recv_sem=recv_sems.at[k], device_id=peer, device_id_type=MESH_ID).start()
        for k in range(n):
            local[k].wait()
            seven = outs[k].at[:, pl.ds(0, N_DEV - 1)]
            pltpu.make_async_remote_copy(src_ref=seven, dst_ref=seven, send_sem=send_sems.at[k],
                                         recv_sem=recv_sems.at[k], device_id=(x, y, c), device_id_type=MESH_ID).wait()

    return pl.pallas_call(
        body, name="all_gather_weights", in_specs=[any_spec] * n, out_specs=[any_spec] * n,
        out_shape=[_sds((s.shape[0], N_DEV) + s.shape[2:], s.dtype) for s in shards],
        scratch_shapes=[pltpu.SemaphoreType.DMA((n,)), pltpu.SemaphoreType.DMA((n,)), pltpu.SemaphoreType.DMA((n,))],
    )(*shards)


def exchange_grads(pieces, outs_meta):
    n_p, n_o = len(pieces), len(outs_meta)
    any_spec = pl.BlockSpec(memory_space=pl.ANY)

    def body(*refs):
        ins, outs = refs[:n_p], refs[n_p:n_p + n_o]
        send_sems, recv_sems, local_sems = refs[n_p + n_o:]
        x, y, c, me = _my_place()
        local = []
        for q, (_, k, l) in enumerate(pieces):
            here = outs[k].at[pl.ds(me, 1), l]
            cp = pltpu.make_async_copy(ins[q].at[pl.ds(me, 1)], here, local_sems.at[q])
            cp.start()
            local.append(cp)
            for peer, idx in _peers(x, y, c):
                pltpu.make_async_remote_copy(src_ref=ins[q].at[pl.ds(idx, 1)], dst_ref=here, send_sem=send_sems.at[k],
                                             recv_sem=recv_sems.at[k], device_id=peer, device_id_type=MESH_ID).start()
        for cp in local:
            cp.wait()
        for k in range(n_o):
            seven = outs[k].at[pl.ds(0, N_DEV - 1)]
            pltpu.make_async_remote_copy(src_ref=seven, dst_ref=seven, send_sem=send_sems.at[k],
                                         recv_sem=recv_sems.at[k], device_id=(x, y, c), device_id_type=MESH_ID).wait()

    return pl.pallas_call(
        body, name="exchange_grads", in_specs=[any_spec] * n_p, out_specs=[any_spec] * n_o,
        out_shape=[_sds((N_DEV, L, a, b), dt) for (L, a, b, dt) in outs_meta],
        scratch_shapes=[pltpu.SemaphoreType.DMA((n_o,)), pltpu.SemaphoreType.DMA((n_o,)),
                        pltpu.SemaphoreType.DMA((n_p,))],
    )(*[p[0] for p in pieces])


def _pack_rows(parts):
    out = []
    for arr, rows in parts:
        arr = arr.reshape(-1, LANES).astype(F32)
        out.append(jnp.pad(arr, ((0, rows - arr.shape[0]), (0, 0))))
    return jnp.concatenate(out, axis=0)


def _small_rows(a, prefix):
    return _pack_rows([(a[prefix + name], rows) for name, rows in SMALL_SHARDED])


def _unpack_small(packed, a):
    out, r0 = {}, 0
    for name, rows in SMALL_SHARDED:
        shape = a[name].shape
        used = math.prod(shape) // LANES
        out[name] = packed[r0:r0 + used].reshape(shape)
        r0 += rows
    return out


def _rep_rows(a, prefix):
    parts = []
    for name in REPLICATED:
        arr = a[prefix + name]
        if arr.size % LANES:
            arr = jnp.pad(arr.reshape(1, -1), ((0, 0), (0, LANES - arr.size % LANES)))
        rows = -(-arr.size // LANES)
        parts.append((arr, -(-rows // SUBLANES) * SUBLANES))
    return _pack_rows(parts)


def _unpack_rep(packed, a):
    out, r0 = {}, 0
    for name in REPLICATED:
        shape = a[name].shape
        size = math.prod(shape)
        rows = -(-size // LANES)
        out[name] = packed[r0:r0 + rows].reshape(-1)[:size].reshape(shape)
        r0 += -(-rows // SUBLANES) * SUBLANES
    return out


def _step(a):
    x = a['x'][0]
    T, D = x.shape
    tgt = a['loss_target'][0]
    p_in = [a['p'][i, 0] for i in range(DEPTH)]
    PLE = p_in[0].shape[1]
    n_heads = a['attn_sinks'].shape[1]
    HD = n_heads * HEAD_DIM
    KVD = a['kv_w_k'].shape[1]
    n_kv = KVD // HEAD_DIM
    F = a['mlp_w_down'].shape[1] * N_DEV
    tm = min(TOKEN_TILE, T)
    tm2 = min(2 * TOKEN_TILE, T)
    tmc = min(256, T)
    alpha = DEEPNORM_ALPHA
    xb = x.astype(BF16)
    p_b = [p.astype(BF16) for p in p_in]

    def shard3(w):
        return w.reshape((1,) + w.shape) if w.ndim == 2 else w

    shards = [shard3(a[nm]).astype(BF16) for nm in BIG_WEIGHTS] + [_small_rows(a, '')[None]]
    gathered = all_gather([s.reshape(s.shape[0], 1, s.shape[1], s.shape[2]) for s in shards])
    G = dict(zip(BIG_WEIGHTS, gathered[:-1]))
    small_full = gathered[-1][0]
    r0, small = 0, {}
    for name, rows in SMALL_SHARDED:
        small[name] = small_full[:, r0:r0 + rows]
        r0 += rows
    b_in = small['conv_b_in'][:, 0:2 * D // N_DEV // LANES].reshape(1, 2 * D)
    w_dw = jnp.transpose(small['conv_w_dw'], (1, 0, 2)).reshape(CONV_HALO, D)
    b_dw, cln_g, cln_b, b_out = (small[nm][:, 0].reshape(1, D) for nm in
                                 ('conv_b_dw', 'conv_ln_g', 'conv_ln_b', 'conv_b_out'))

    W_in = G['conv_w_in'][0]
    W_out = G['conv_w_out'][0].reshape(1, D, D)
    W_qkv = jnp.concatenate([G['attn_w_q'][0].reshape(D, HD), G['kv_w_k'][0].reshape(D, KVD),
                             G['kv_w_v'][0].reshape(D, KVD)], axis=1)[None]
    W_o = G['attn_w_o'][0].reshape(1, HD, D)
    W_up = [G['mlp_w_up'][i] for i in range(DEPTH)]
    W_down = [G['mlp_w_down'][i].reshape(1, F, D) for i in range(DEPTH)]
    W_proj = [jnp.transpose(G['ple_w_proj'][i], (1, 0, 2)).reshape(1, PLE, D) for i in range(DEPTH)]
    W_gate = [G['ple_w_gate'][i].reshape(1, D, D) for i in range(DEPTH)]
    mix_g, mix_b, mlp_g, mlp_b = a['mix_ln_g'], a['mix_ln_b'], a['mlp_ln_g'], a['mlp_ln_b']
    rope = _rope_tables(T)

    def row(v, i):
        return v[i:i + 1]

    def res_ln_epi(coef):
        def epi(accs, ex, out, i):
            acc = accs[0] if isinstance(accs, list) else accs
            n_ex = len(ex)
            res_ref, g_ref, b_ref = ex[n_ex - 3], ex[n_ex - 2], ex[n_ex - 1]
            z = coef * res_ref[...] + acc
            if n_ex == 4:
                z = z + ex[0][...]
            out[0][...] = z
            xo = _ln_fwd(z, g_ref[...], b_ref[...])
            out[1][...] = xo
            out[2][...] = xo.astype(BF16)
        return epi

    res_ln_outs = [(_sds((T, D), F32), 'tile'), (_sds((T, D), F32), 'tile'), (_sds((T, D), BF16), 'tile')]

    def mlp_fwd(li, xin, xin_b):
        def up_epi(accs, ex, out, i):
            u = accs[0]
            out[0][...] = u.astype(BF16)
            out[1][...] = jnp.square(jnp.maximum(u, 0.0)).astype(BF16)
        u, act = mm_nn(f"mlp_up_{li}", [(xin_b, W_up[li], 0)], [], [(_sds((T, F), BF16), 'tile')] * 2, up_epi,
                       tm=tm2, tn=min(1024, F))
        z, xo, xo_b = mm_nn(f"mlp_down_{li}", [(act, W_down[li], 0)],
                            [(xin, 'tile'), (row(mlp_g, li), 'row'), (row(mlp_b, li), 'row')],
                            res_ln_outs, res_ln_epi(alpha), tm=tm, tn=D, tk=min(1024, F))
        return u, act, z, xo, xo_b

    def ple_fwd(li, xin, xin_b, with_loss):
        def epi(accs, ex, out, i):
            pp, gg = accs
            xo = ex[0][...] + pp * _sigmoid(gg)
            out[1][...] = pp.astype(BF16)
            out[2][...] = gg.astype(BF16)
            if with_loss:
                err = xo - ex[1][...]
                out[0][...] = err * (1.0 / D)
                _init_or_add(out[3], i, jnp.sum(err * err, axis=0, keepdims=True) * (0.5 / D))
            else:
                out[0][...] = xo
                out[3][...] = xo.astype(BF16)
        extras = [(xin, 'tile')] + ([(tgt, 'tile')] if with_loss else [])
        outs = [(_sds((T, D), F32), 'tile'), (_sds((T, D), BF16), 'tile'), (_sds((T, D), BF16), 'tile')]
        outs.append((_sds((1, D), F32), 'rowacc') if with_loss else (_sds((T, D), BF16), 'tile'))
        return mm_nn(f"ple_{li}", [(p_b[li], W_proj[li], 0), (xin_b, W_gate[li], 0)], extras, outs, epi, tm=tm, tn=D)

    assert D // N_DEV == LANES
    g0, ha0, hg0 = _glu(xb, W_in, b_in, T, D, tm)
    c0, s0 = dwconv_fwd(g0, w_dw, b_dw, cln_g, cln_b, tm=tmc)
    z1, x1, x1b = mm_nn("conv_out", [(s0, W_out, 0)],
                        [(b_out, 'row'), (x, 'tile'), (row(mix_g, 0), 'row'), (row(mix_b, 0), 'row')],
                        res_ln_outs, res_ln_epi(alpha), tm=tm, tn=D)
    u0, act0, z2, x2, x2b = mlp_fwd(0, x1, x1b)
    x3, pp0, gg0, x3b = ple_fwd(0, x2, x2b, False)

    def qkv_epi(accs, ex, out, i):
        t = accs[0]
        c, su, sd = ex[0][...], ex[1][...], ex[2][...]
        out[0][:, 0:HD + KVD] = _rope_wide(t[:, 0:HD + KVD], c, su, sd, _rope_chunk).astype(BF16)
        out[0][:, HD + KVD:] = t[:, HD + KVD:].astype(BF16)
    NQ = HD + 2 * KVD
    qkv, = mm_nn("qkv_rope", [(x3b, W_qkv, 0)], [(t, 'tab') for t in rope], [(_sds((T, NQ), BF16), 'tile')], qkv_epi,
                 tm=tm, tn=NQ)
    o1, lse1 = attn_fwd(qkv, a['attn_sinks'], n_heads, n_kv)
    z3, x4, x4b = mm_nn("attn_out", [(o1, W_o, 0)], [(x3, 'tile'), (row(mix_g, 1), 'row'), (row(mix_b, 1), 'row')],
                        res_ln_outs, res_ln_epi(alpha), tm=tm, tn=D)
    u1, act1, z4, x5, x5b = mlp_fwd(1, x4, x4b)
    dy, pp1, gg1, loss_row = ple_fwd(1, x5, x5b, True)
    loss_local = jnp.sum(loss_row)

    grads = {}

    def ln_bwd_epi(coef, with_colsum):
        def epi(acc, ex, out, i):
            d_x = acc + coef * ex[0][...]
            dz, dg, db = _ln_bwd(d_x, ex[1][...], ex[2][...])
            out[0][...] = dz
            out[1][...] = dz.astype(BF16)
            _init_or_add(out[2], i, dg)
            _init_or_add(out[3], i, db)
            if with_colsum:
                _init_or_add(out[4], i, jnp.sum(dz, axis=0, keepdims=True))
        return epi

    def ln_bwd_outs(with_colsum):
        outs = [(_sds((T, D), F32), 'tile'), (_sds((T, D), BF16), 'tile'), (_sds((1, D), F32), 'rowacc'),
                (_sds((1, D), F32), 'rowacc')]
        return outs + ([(_sds((1, D), F32), 'rowacc')] if with_colsum else [])

    def ple_bwd(li, d_out, xin, pp, gg, z_mlp):
        d_pp, d_gg = ple_bwd_elem(d_out, pp, gg, tm=tm)
        grads[('ple_w_proj', li)] = mm_tn(f"d_ple_proj_{li}", p_b[li], d_pp, D, F32, tm=tm, tk=PLE, tn=D)
        grads[('ple_w_gate', li)] = mm_tn(f"d_ple_gate_{li}", xin, d_gg, D, BF16, tm=tm, tk=D, tn=D)
        dz, dzb, dg, db = mm_nt(f"ple_dx_{li}", d_gg, W_gate[li],
                                [(d_out, 'tile'), (z_mlp, 'tile'), (row(mlp_g, li), 'row')],
                                ln_bwd_outs(False), ln_bwd_epi(1.0, False), tm=tm, tko=D, tc=D)
        grads[('mlp_ln_g', li)], grads[('mlp_ln_b', li)] = dg, db
        return dz, dzb

    def mlp_bwd(li, dz, dzb, xin, u, act, z_mix, with_colsum):
        grads[('mlp_w_down', li)] = mm_tn(f"d_mlp_down_{li}", act, dzb, D, BF16, tm=tm, tk=min(1024, F), tn=D)

        def du_epi(acc, ex, out, i):
            out[0][...] = (acc * (2.0 * jnp.maximum(ex[0][...].astype(F32), 0.0))).astype(BF16)
        du, = mm_nt(f"mlp_du_{li}", dzb, W_down[li], [(u, 'tile')], [(_sds((T, F), BF16), 'tile')], du_epi,
                    tm=tm2, tko=min(1024, F), tc=D)
        grads[('mlp_w_up', li)] = mm_tn(f"d_mlp_up_{li}", xin, du, F // N_DEV, BF16, tm=tm, tk=D, tn=min(1024, F))
        res = mm_nt(f"mlp_dx_{li}", du, W_up[li], [(dz, 'tile'), (z_mix, 'tile'), (row(mix_g, li), 'row')],
                    ln_bwd_outs(with_colsum), ln_bwd_epi(alpha, with_colsum), tm=tm, tko=D, tc=min(1024, F))
        grads[('mix_ln_g', li)], grads[('mix_ln_b', li)] = res[2], res[3]
        return res

    dz4, dz4b = ple_bwd(1, dy, x5b, pp1, gg1, z4)
    dz3, dz3b, _, _ = mlp_bwd(1, dz4, dz4b, x4b, u1, act1, z3, False)
    grads['attn_w_o'] = mm_tn("d_attn_wo", o1, dz3b, D, BF16, tm=tm, tk=HD, tn=D)

    def do_epi(acc, ex, out, i):
        out[0][...] = acc.astype(BF16)
    do1, = mm_nt("attn_do", dz3b, W_o, [], [(_sds((T, HD), BF16), 'tile')], do_epi, tm=tm, tko=HD, tc=D)
    d_qkv, dkv, d_sinks = attn_bwd(qkv, do1, lse1, a['attn_sinks'], rope, n_heads, n_kv)
    d_qkv = dkv_finish(d_qkv, dkv, rope, HD, KVD, tm=tm)
    d_wqkv = mm_tn("d_wqkv", x3b, d_qkv, NQ, F32, tm=tm, tk=D, tn=NQ)[0]

    def dx3_epi(acc, ex, out, i):
        out[0][...] = acc + alpha * ex[0][...]
    dx3, = mm_nt("attn_dx", d_qkv, W_qkv, [(dz3, 'tile')], [(_sds((T, D), F32), 'tile')], dx3_epi,
                 tm=tm, tko=D, tc=NQ)

    dz2, dz2b = ple_bwd(0, dx3, x2b, pp0, gg0, z2)
    dz1, dz1b, _, _, db_out = mlp_bwd(0, dz2, dz2b, x1b, u0, act0, z1, True)
    grads['conv_w_out'] = mm_tn("d_conv_wout", s0, dz1b, D, BF16, tm=tm, tk=D, tn=D)

    def ds_epi(acc, ex, out, i):
        n = _ln_fwd(ex[0][...], ex[1][...], ex[2][...])
        sg = _sigmoid(n)
        dn = acc * (sg * (1.0 + n * (1.0 - sg)))
        dc, dg, db = _ln_bwd(dn, ex[0][...], ex[1][...])
        out[0][...] = dc
        _init_or_add(out[1], i, dg)
        _init_or_add(out[2], i, db)
    dc0, d_cln_g, d_cln_b = mm_nt("conv_ds", dz1b, W_out, [(c0, 'tile'), (cln_g, 'row'), (cln_b, 'row')],
                                  [(_sds((T, D), F32), 'tile'), (_sds((1, D), F32), 'rowacc'),
                                   (_sds((1, D), F32), 'rowacc')], ds_epi, tm=tm, tko=D, tc=D)
    dh0, d_wdw, d_bdw, d_bin = dwconv_bwd(dc0, g0, ha0, hg0, w_dw, tm=tmc)
    grads['conv_w_in'] = mm_tn("d_conv_win", xb, dh0, 2 * D // N_DEV, BF16, tm=tm, tk=D, tn=D)

    def dx_epi(acc, ex, out, i):
        out[0][...] = acc + alpha * ex[0][...]
    grad_x, = mm_nt("conv_dx", dh0, W_in, [(dz1, 'tile')], [(_sds((T, D), F32), 'tile')], dx_epi,
                    tm=tm, tko=D, tc=D)

    pieces, metas, out_names = [], [], []

    def add_output(name, L, shape2, dtype):
        metas.append((L, shape2[0], shape2[1], dtype))
        out_names.append(name)
        return len(metas) - 1

    def dev_major(arr, n_rows):
        return arr.reshape(N_DEV, n_rows, arr.shape[-1])

    k = add_output('conv_w_in', 1, (D, 2 * D // N_DEV), BF16)
    pieces.append((grads['conv_w_in'], k, 0))
    k = add_output('conv_w_out', 1, (D // N_DEV, D), BF16)
    pieces.append((dev_major(grads['conv_w_out'], D // N_DEV), k, 0))
    k = add_output('kv_w_k', 1, (D // N_DEV, KVD), F32)
    pieces.append((dev_major(d_wqkv[:, HD:HD + KVD], D // N_DEV), k, 0))
    k = add_output('kv_w_v', 1, (D // N_DEV, KVD), F32)
    pieces.append((dev_major(d_wqkv[:, HD + KVD:], D // N_DEV), k, 0))
    k = add_output('attn_w_q', 1, (D // N_DEV, HD), F32)
    pieces.append((dev_major(d_wqkv[:, 0:HD], D // N_DEV), k, 0))
    k = add_output('attn_w_o', 1, (HD // N_DEV, D), BF16)
    pieces.append((dev_major(grads['attn_w_o'], HD // N_DEV), k, 0))
    k = add_output('mlp_w_up', DEPTH, (D, F // N_DEV), BF16)
    pieces += [(grads[('mlp_w_up', li)], k, li) for li in range(DEPTH)]
    k = add_output('mlp_w_down', DEPTH, (F // N_DEV, D), BF16)
    pieces += [(dev_major(grads[('mlp_w_down', li)], F // N_DEV), k, li) for li in range(DEPTH)]
    k = add_output('ple_w_proj', DEPTH, (PLE, D // N_DEV), F32)
    pieces += [(jnp.transpose(grads[('ple_w_proj', li)][0].reshape(PLE, N_DEV, D // N_DEV), (1, 0, 2)), k, li)
               for li in range(DEPTH)]
    k = add_output('ple_w_gate', DEPTH, (D // N_DEV, D), BF16)
    pieces += [(dev_major(grads[('ple_w_gate', li)], D // N_DEV), k, li) for li in range(DEPTH)]

    def own_rows(vec, rows_used, rows):
        arr = vec.reshape(N_DEV, rows_used, LANES)
        return jnp.pad(arr, ((0, 0), (0, rows - rows_used), (0, 0)))
    dwdw_dev = jnp.transpose(d_wdw.reshape(CONV_HALO, N_DEV, D // N_DEV), (1, 0, 2))
    lane_rows = D // N_DEV // LANES
    small_grad = jnp.concatenate([
        own_rows(d_bin, 2 * lane_rows, 8), dwdw_dev if lane_rows == 1 else dwdw_dev.reshape(N_DEV, -1, LANES),
        own_rows(d_bdw, lane_rows, 8), own_rows(d_cln_g, lane_rows, 8), own_rows(d_cln_b, lane_rows, 8),
        own_rows(db_out, lane_rows, 8)], axis=1)
    n_small = small_grad.shape[1]
    k_small = add_output('__small__', 1, (n_small, LANES), F32)
    pieces.append((small_grad, k_small, 0))

    rep_local = {'mix_ln_g': jnp.concatenate([grads[('mix_ln_g', li)] for li in range(DEPTH)], axis=0),
                 'mix_ln_b': jnp.concatenate([grads[('mix_ln_b', li)] for li in range(DEPTH)], axis=0),
                 'mlp_ln_g': jnp.concatenate([grads[('mlp_ln_g', li)] for li in range(DEPTH)], axis=0),
                 'mlp_ln_b': jnp.concatenate([grads[('mlp_ln_b', li)] for li in range(DEPTH)], axis=0),
                 'attn_sinks': d_sinks}
    rep_grad = _rep_rows(rep_local, '')
    n_rep = rep_grad.shape[0]
    k_rep = add_output('__rep__', 1, (n_rep, LANES), F32)
    pieces.append((jnp.broadcast_to(rep_grad[None], (N_DEV, n_rep, LANES)), k_rep, 0))

    received = exchange_grads(pieces, metas)

    result = {}
    for name, recv in zip(out_names, received):
        if name == '__small__':
            w, m, v = (_small_rows(a, pre)[None] for pre in ('', 'm_', 'v_'))
            outs4 = adamw("adamw_small", recv, w, m, v, ta=n_small)
            for kind, arr in zip(('grad', 'delta', 'new_m', 'new_v'), outs4):
                for pname, val in _unpack_small(arr[0], a).items():
                    result[(kind, pname)] = val
        elif name == '__rep__':
            w, m, v = (_rep_rows(a, pre)[None] for pre in ('', 'm_', 'v_'))
            outs4 = adamw("adamw_rep", recv, w, m, v, ta=n_rep)
            for kind, arr in zip(('grad', 'delta', 'new_m', 'new_v'), outs4):
                for pname, val in _unpack_rep(arr[0], a).items():
                    result[(kind, pname)] = val
        else:
            w, m, v = (shard3(a[pre + name]) for pre in ('', 'm_', 'v_'))
            rows = w.shape[1]
            outs4 = adamw("adamw_" + name, recv, w, m, v, ta=min(256, rows))
            for kind, arr in zip(('grad', 'delta', 'new_m', 'new_v'), outs4):
                result[(kind, name)] = arr.reshape(a[name].shape)

    loss = lax.psum(loss_local, ("x", "y", "c"))
    out = [loss, grad_x[None]]
    for kind in ('grad', 'delta', 'new_m', 'new_v'):
        out += [result[(kind, name)] for name in WEIGHT_NAMES]
    return tuple(out)


def _glu(x, W_in, b_in, T, D, tm):
    n = W_in.shape[2]
    q = 2 if D // n % 2 == 0 else 1
    nt = D // (q * n)
    tn = q * n

    def body(x_ref, wa_ref, wg_ref, ba_ref, bg_ref, g_ref, ha_ref, hg_ref):
        xb = x_ref[...]
        ha = jnp.concatenate([_dot(xb, wa_ref[s], ((1,), (0,))) for s in range(q)], axis=1) + ba_ref[...]
        hg = jnp.concatenate([_dot(xb, wg_ref[s], ((1,), (0,))) for s in range(q)], axis=1) + bg_ref[...]
        g_ref[...] = ha * _sigmoid(hg)
        ha_ref[...] = ha.astype(ha_ref.dtype)
        hg_ref[...] = hg.astype(hg_ref.dtype)

    return pl.pallas_call(
        body, name="conv_in_glu", grid=(T // tm, nt),
        in_specs=[pl.BlockSpec((tm, D), lambda i, j: (i, 0)),
                  pl.BlockSpec((q, D, n), lambda i, j: (j, 0, 0)),
                  pl.BlockSpec((q, D, n), lambda i, j: (j + nt, 0, 0)),
                  pl.BlockSpec((1, tn), lambda i, j: (0, j)), pl.BlockSpec((1, tn), lambda i, j: (0, j + nt))],
        out_specs=[pl.BlockSpec((tm, tn), lambda i, j: (i, j))] * 3,
        out_shape=[_sds((T, D), F32), _sds((T, D), BF16), _sds((T, D), BF16)],
        compiler_params=_params(("parallel", "parallel")))(x, W_in, W_in, b_in, b_in)


def kernel(x, p, conv_w_in, conv_b_in, conv_w_dw, conv_b_dw, conv_ln_g, conv_ln_b, conv_w_out, conv_b_out, kv_w_k, kv_w_v, attn_w_q, attn_sinks, attn_w_o, mix_ln_g, mix_ln_b, mlp_w_up, mlp_w_down, mlp_ln_g, mlp_ln_b, ple_w_proj, ple_w_gate, loss_target, m_conv_w_in, m_conv_b_in, m_conv_w_dw, m_conv_b_dw, m_conv_ln_g, m_conv_ln_b, m_conv_w_out, m_conv_b_out, m_kv_w_k, m_kv_w_v, m_attn_w_q, m_attn_sinks, m_attn_w_o, m_mix_ln_g, m_mix_ln_b, m_mlp_w_up, m_mlp_w_down, m_mlp_ln_g, m_mlp_ln_b, m_ple_w_proj, m_ple_w_gate, v_conv_w_in, v_conv_b_in, v_conv_w_dw, v_conv_b_dw, v_conv_ln_g, v_conv_ln_b, v_conv_w_out, v_conv_b_out, v_kv_w_k, v_kv_w_v, v_attn_w_q, v_attn_sinks, v_attn_w_o, v_mix_ln_g, v_mix_ln_b, v_mlp_w_up, v_mlp_w_down, v_mlp_ln_g, v_mlp_ln_b, v_ple_w_proj, v_ple_w_gate):
    return _step(dict(locals()))
```

```python
import functools
import math

import jax
import jax.numpy as jnp
from jax import lax
from jax.experimental import pallas as pl
from jax.experimental.pallas import tpu as pltpu

F32 = jnp.float32
BF16 = jnp.bfloat16

N_DEV = 8
HEAD_DIM = 64
ROPE_DIM = HEAD_DIM // 4
ROPE_HALF = ROPE_DIM // 2
ROPE_THETA = 500000.0
ATT_BLOCK = 128
CONV_WIDTH = 31
CONV_HALO = 32
LN_EPS = 1e-5
DEPTH = 2
DEEPNORM_ALPHA = (2 * DEPTH) ** 0.25
MASK_VALUE = -1e30

ADAM_LR = 0.001
ADAM_B1 = 0.9
ADAM_B2 = 0.999
ADAM_EPS = 1e-08
ADAM_WD = 0.01
ADAM_STEP = 10

LANES = 128
SUBLANES = 8
VMEM_LIMIT_BYTES = 52 * 1024 * 1024
TOKEN_TILE = 512
MESH_ID = pl.DeviceIdType.MESH

WEIGHT_NAMES = ['conv_w_in', 'conv_b_in', 'conv_w_dw', 'conv_b_dw', 'conv_ln_g', 'conv_ln_b', 'conv_w_out',
                'conv_b_out', 'kv_w_k', 'kv_w_v', 'attn_w_q', 'attn_sinks', 'attn_w_o', 'mix_ln_g', 'mix_ln_b',
                'mlp_w_up', 'mlp_w_down', 'mlp_ln_g', 'mlp_ln_b', 'ple_w_proj', 'ple_w_gate']
BIG_WEIGHTS = ['conv_w_in', 'conv_w_out', 'kv_w_k', 'kv_w_v', 'attn_w_q', 'attn_w_o', 'mlp_w_up', 'mlp_w_down',
               'ple_w_proj', 'ple_w_gate']
SMALL_SHARDED = [('conv_b_in', 8), ('conv_w_dw', 32), ('conv_b_dw', 8), ('conv_ln_g', 8), ('conv_ln_b', 8),
                 ('conv_b_out', 8)]
REPLICATED = ['mix_ln_g', 'mix_ln_b', 'mlp_ln_g', 'mlp_ln_b', 'attn_sinks']


def _params(sem):
    return pltpu.CompilerParams(dimension_semantics=sem, vmem_limit_bytes=VMEM_LIMIT_BYTES)


def _sds(shape, dtype):
    return jax.ShapeDtypeStruct(shape, dtype)


def _my_place():
    x, y, c = lax.axis_index("x"), lax.axis_index("y"), lax.axis_index("c")
    return x, y, c, 4 * x + 2 * y + c


def _peers(x, y, c):
    out = []
    for dx in (0, 1):
        for dy in (0, 1):
            for dc in (0, 1):
                if dx or dy or dc:
                    px, py, pc = x ^ dx, y ^ dy, c ^ dc
                    out.append(((px, py, pc), 4 * px + 2 * py + pc))
    return out


class _Job:
    def __init__(self, items, gather):
        self.gather = gather
        self.sources = [it.reshape((1,) + it.shape) for it in items] if gather else list(items)
        shapes = [it.shape for it in items] if gather else [it.shape[1:] for it in items]
        self.dests = [_sds((N_DEV,) + tuple(s), it.dtype) for s, it in zip(shapes, items)]
        self.n = len(items)

    def sem_shapes(self):
        return [pltpu.SemaphoreType.DMA((self.n,))] * 3

    def _mine(self, src, dst, k, me):
        here = dst[k].at[pl.ds(me, 1)]
        return (src[k] if self.gather else src[k].at[pl.ds(me, 1)]), here

    def start(self, src, dst, send, recv, loc):
        x, y, c, me = _my_place()
        for k in range(self.n):
            own, here = self._mine(src, dst, k, me)
            pltpu.make_async_copy(own, here, loc.at[k]).start()
            for peer, idx in _peers(x, y, c):
                what = own if self.gather else src[k].at[pl.ds(idx, 1)]
                pltpu.make_async_remote_copy(src_ref=what, dst_ref=here, send_sem=send.at[k], recv_sem=recv.at[k],
                                             device_id=peer, device_id_type=MESH_ID).start()

    def finish(self, src, dst, send, recv, loc):
        x, y, c, me = _my_place()
        for k in range(self.n):
            own, here = self._mine(src, dst, k, me)
            pltpu.make_async_copy(own, here, loc.at[k]).wait()
            seven = dst[k].at[pl.ds(0, N_DEV - 1)]
            pltpu.make_async_remote_copy(src_ref=seven, dst_ref=seven, send_sem=send.at[k], recv_sem=recv.at[k],
                                         device_id=(x, y, c), device_id_type=MESH_ID).wait()


def _call(body, comm, *, name, grid, in_specs, out_specs, out_shape, operands, sem, scratch_shapes=(), aliases=None):
    single = not isinstance(out_shape, (list, tuple))
    out_shape = [out_shape] if single else list(out_shape)
    out_specs = [out_specs] if single else list(out_specs)
    if comm is None:
        res = pl.pallas_call(body, name=name, grid=grid, in_specs=list(in_specs), out_specs=out_specs,
                             out_shape=out_shape, scratch_shapes=list(scratch_shapes),
                             input_output_aliases=aliases or {}, compiler_params=_params(sem))(*operands)
        return res[0] if single else res
    n_in, n_out, n_scr, n_c = len(in_specs), len(out_shape), len(scratch_shapes), comm.n
    any_spec = pl.BlockSpec(memory_space=pl.ANY)

    def hosted(*refs):
        ins, c_src = refs[:n_in], refs[n_in:n_in + n_c]
        outs = refs[n_in + n_c:n_in + n_c + n_out]
        c_dst = refs[n_in + n_c + n_out:n_in + 2 * n_c + n_out]
        scr = refs[n_in + 2 * n_c + n_out:n_in + 2 * n_c + n_out + n_scr]
        sems = refs[n_in + 2 * n_c + n_out + n_scr:]
        first = pl.program_id(0) == 0
        last = pl.program_id(0) == grid[0] - 1
        for d in range(1, len(grid)):
            first = first & (pl.program_id(d) == 0)
            last = last & (pl.program_id(d) == grid[d] - 1)

        @pl.when(first)
        def _():
            comm.start(c_src, c_dst, *sems)

        body(*ins, *outs, *scr)

        @pl.when(last)
        def _():
            comm.finish(c_src, c_dst, *sems)

    res = pl.pallas_call(hosted, name=name, grid=grid, in_specs=list(in_specs) + [any_spec] * n_c,
                         out_specs=out_specs + [any_spec] * n_c, out_shape=out_shape + comm.dests,
                         scratch_shapes=list(scratch_shapes) + comm.sem_shapes(), input_output_aliases=aliases or {},
                         compiler_params=_params(("arbitrary",) * len(grid)))(*operands, *comm.sources)
    main = res[:n_out]
    return (main[0] if single else main), res[n_out:]


def comm_only(name, job):
    any_spec = pl.BlockSpec(memory_space=pl.ANY)

    def body(*refs):
        src, dst, sems = refs[:job.n], refs[job.n:2 * job.n], refs[2 * job.n:]
        job.start(src, dst, *sems)
        job.finish(src, dst, *sems)

    return pl.pallas_call(body, name=name, in_specs=[any_spec] * job.n, out_specs=[any_spec] * job.n,
                          out_shape=job.dests, scratch_shapes=job.sem_shapes())(*job.sources)


def _dot(a, b, dims):
    if a.dtype != BF16:
        a = a.astype(BF16)
    if b.dtype != BF16:
        b = b.astype(BF16)
    return lax.dot_general(a, b, (dims, ((), ())), preferred_element_type=F32)


def _sigmoid(v):
    return 1.0 / (1.0 + jnp.exp(-v))


def _ln_stats(z):
    mu = jnp.mean(z, axis=-1, keepdims=True)
    zc = z - mu
    var = jnp.mean(zc * zc, axis=-1, keepdims=True)
    return zc * lax.rsqrt(var + LN_EPS)


def _ln_fwd(z, g, b):
    return _ln_stats(z) * g + b


def _ln_bwd(dy, z, g):
    xhat = _ln_stats(z)
    mu = jnp.mean(z, axis=-1, keepdims=True)
    zc = z - mu
    rstd = lax.rsqrt(jnp.mean(zc * zc, axis=-1, keepdims=True) + LN_EPS)
    dxh = dy * g
    m1 = jnp.mean(dxh, axis=-1, keepdims=True)
    m2 = jnp.mean(dxh * xhat, axis=-1, keepdims=True)
    dz = rstd * (dxh - m1 - xhat * m2)
    return dz, jnp.sum(dy * xhat, axis=0, keepdims=True), jnp.sum(dy, axis=0, keepdims=True)


def _extra_spec(shape, kind, tm, tn, ij):
    if kind == 'tile':
        return pl.BlockSpec((tm, tn), lambda *g: (ij(g)[0], ij(g)[1]))
    if kind in ('row', 'rowacc'):
        return pl.BlockSpec((1, tn), lambda *g: (0, ij(g)[1]))
    if kind == 'tab':
        return pl.BlockSpec((tm, LANES), lambda *g: (ij(g)[0], 0))
    raise ValueError(kind)


def mm_nn(name, pairs, extras, outs, epi, *, tm, tn, tk=None, comm=None):
    M = pairs[0][0].shape[0]
    N = outs[0][0].shape[1]
    n_pairs = len(pairs)
    K0 = pairs[0][0].shape[1]
    tk = K0 if tk is None else tk
    nk = K0 // tk
    assert nk == 1 or n_pairs == 1
    assert M % tm == 0 and N % tn == 0 and K0 % tk == 0
    has_rowacc = any(kind == 'rowacc' for _, kind in outs)
    assert not has_rowacc or (N == tn and nk == 1)
    in_specs, operands, slabs = [], [], []
    for a, b, off in pairs:
        K = a.shape[1]
        ktile = K if n_pairs > 1 else tk
        n = b.shape[2]
        assert b.shape[1] == K
        in_specs.append(pl.BlockSpec((tm, ktile), lambda i, j, k: (i, k)))
        if tn <= n:
            assert n % tn == 0
            r = n // tn
            in_specs.append(pl.BlockSpec((None, ktile, tn),
                                         lambda i, j, k, r=r, off=off: ((j + off) // r, k, (j + off) % r)))
            slabs.append(0)
        else:
            assert tn % n == 0
            in_specs.append(pl.BlockSpec((tn // n, ktile, n), lambda i, j, k, off=off: (j + off, k, 0)))
            slabs.append(tn // n)
        operands += [a, b]
    ij = lambda g: (g[0], g[1])
    for arr, kind in extras:
        in_specs.append(_extra_spec(arr.shape, kind, tm, tn, ij))
        operands.append(arr)
    out_specs = [_extra_spec(o.shape, kind, tm, tn, ij) for o, kind in outs]
    n_ex, n_out = len(extras), len(outs)

    def pair_dot(ab, q):
        a = ab[2 * q][...]
        if not slabs[q]:
            return _dot(a, ab[2 * q + 1][...], ((1,), (0,)))
        return jnp.concatenate([_dot(a, ab[2 * q + 1][s], ((1,), (0,))) for s in range(slabs[q])], axis=1)

    def body(*refs):
        ab = refs[:2 * n_pairs]
        ex = refs[2 * n_pairs:2 * n_pairs + n_ex]
        out = refs[2 * n_pairs + n_ex:2 * n_pairs + n_ex + n_out]
        i = pl.program_id(0)
        if nk == 1:
            accs = [pair_dot(ab, q) for q in range(n_pairs)]
            epi(accs, ex, out, i)
        else:
            acc_ref = refs[-1]
            k = pl.program_id(2)

            @pl.when(k == 0)
            def _():
                acc_ref[...] = jnp.zeros_like(acc_ref)

            acc_ref[...] += pair_dot(ab, 0)

            @pl.when(k == nk - 1)
            def _():
                epi([acc_ref[...]], ex, out, i)

    scratch = [pltpu.VMEM((tm, tn), F32)] if nk > 1 else []
    sem = ("arbitrary",) * 3 if has_rowacc else ("parallel", "parallel", "arbitrary")
    return _call(body, comm, name=name, grid=(M // tm, N // tn, nk), in_specs=in_specs, out_specs=out_specs,
                 out_shape=[o for o, _ in outs], scratch_shapes=scratch, sem=sem, operands=operands)


def mm_nt(name, a, b, extras, outs, epi, *, tm, tko, tc, comm=None):
    M, N = a.shape
    J, K, n = b.shape
    assert J * n == N and M % tm == 0 and K % tko == 0 and N % tc == 0
    nc = N // tc
    has_rowacc = any(kind == 'rowacc' for _, kind in outs)
    assert not has_rowacc or K == tko
    if tc <= n:
        assert n % tc == 0
        r = n // tc
        slabs = 0
        b_spec = pl.BlockSpec((None, tko, tc), lambda i, j, c: (c // r, j, c % r))
    else:
        assert tc % n == 0
        slabs = tc // n
        b_spec = pl.BlockSpec((slabs, tko, n), lambda i, j, c: (c, j, 0))
    in_specs = [pl.BlockSpec((tm, tc), lambda i, j, c: (i, c)), b_spec]

    def nt_dot(a_ref, b_ref):
        if not slabs:
            return _dot(a_ref[...], b_ref[...], ((1,), (1,)))
        acc = _dot(a_ref[:, 0:n], b_ref[0], ((1,), (1,)))
        for s in range(1, slabs):
            acc = acc + _dot(a_ref[:, s * n:(s + 1) * n], b_ref[s], ((1,), (1,)))
        return acc

    ij = lambda g: (g[0], g[1])
    operands = [a, b]
    for arr, kind in extras:
        in_specs.append(_extra_spec(arr.shape, kind, tm, tko, ij))
        operands.append(arr)
    out_specs = [_extra_spec(o.shape, kind, tm, tko, ij) for o, kind in outs]
    n_ex, n_out = len(extras), len(outs)

    def body(*refs):
        a_ref, b_ref = refs[:2]
        ex = refs[2:2 + n_ex]
        out = refs[2 + n_ex:2 + n_ex + n_out]
        i = pl.program_id(0)
        if nc == 1:
            epi(nt_dot(a_ref, b_ref), ex, out, i)
        else:
            acc_ref = refs[-1]
            c = pl.program_id(2)

            @pl.when(c == 0)
            def _():
                acc_ref[...] = jnp.zeros_like(acc_ref)

            acc_ref[...] += nt_dot(a_ref, b_ref)

            @pl.when(c == nc - 1)
            def _():
                epi(acc_ref[...], ex, out, i)

    scratch = [pltpu.VMEM((tm, tko), F32)] if nc > 1 else []
    sem = ("arbitrary",) * 3 if has_rowacc else ("parallel", "parallel", "arbitrary")
    return _call(body, comm, name=name, grid=(M // tm, K // tko, nc), in_specs=in_specs, out_specs=out_specs,
                 out_shape=[o for o, _ in outs], scratch_shapes=scratch, sem=sem, operands=operands)


def mm_tn(name, a, d, n, out_dtype, *, tm, tk, tn, comm=None):
    M, K = a.shape
    N = d.shape[1]
    assert d.shape[0] == M and N % n == 0 and N % tn == 0 and K % tk == 0 and M % tm == 0
    nm = M // tm
    if tn <= n:
        assert n % tn == 0
        r = n // tn
        slabs = 0
        o_spec = pl.BlockSpec((None, tk, tn), lambda kk, j, m: (j // r, kk, j % r))
    else:
        assert tn % n == 0
        slabs = tn // n
        o_spec = pl.BlockSpec((slabs, tk, n), lambda kk, j, m: (j, kk, 0))

    def body(a_ref, d_ref, o_ref, acc_ref):
        m = pl.program_id(2)

        @pl.when(m == 0)
        def _():
            acc_ref[...] = jnp.zeros_like(acc_ref)

        acc_ref[...] += _dot(a_ref[...], d_ref[...], ((0,), (0,)))

        @pl.when(m == nm - 1)
        def _():
            if not slabs:
                o_ref[...] = acc_ref[...].astype(o_ref.dtype)
            else:
                for s in range(slabs):
                    o_ref[s] = acc_ref[:, s * n:(s + 1) * n].astype(o_ref.dtype)

    return _call(
        body, comm, name=name, grid=(K // tk, N // tn, nm),
        in_specs=[pl.BlockSpec((tm, tk), lambda kk, j, m: (m, kk)), pl.BlockSpec((tm, tn), lambda kk, j, m: (m, j))],
        out_specs=o_spec, out_shape=_sds((N // n, K, n), out_dtype), scratch_shapes=[pltpu.VMEM((tk, tn), F32)],
        sem=("parallel", "parallel", "arbitrary"), operands=[a, d])


def _init_or_add(ref, i, value):
    @pl.when(i == 0)
    def _():
        ref[...] = value

    @pl.when(i > 0)
    def _():
        ref[...] += value


def _rope_tables(T):
    pos = jnp.arange(T, dtype=F32)
    inv_freq = ROPE_THETA ** (-jnp.arange(0, ROPE_DIM, 2, dtype=F32) / ROPE_DIM)
    ang = pos[:, None] * inv_freq[None, :]
    cos, sin = jnp.cos(ang), jnp.sin(ang)
    ones = jnp.ones((T, HEAD_DIM - ROPE_DIM), F32)
    zeros = jnp.zeros((T, HEAD_DIM - ROPE_DIM), F32)
    zh = jnp.zeros((T, ROPE_HALF), F32)
    c_head = jnp.concatenate([cos, cos, ones], axis=1)
    s_up = jnp.concatenate([-sin, zh, zeros], axis=1)
    s_dn = jnp.concatenate([zh, sin, zeros], axis=1)
    rep = LANES // HEAD_DIM
    return tuple(jnp.tile(t, (1, rep)) for t in (c_head, s_up, s_dn))


def _rope_chunk(t, c, s_up, s_dn):
    return t * c + pltpu.roll(t, LANES - ROPE_HALF, 1) * s_up + pltpu.roll(t, ROPE_HALF, 1) * s_dn


def _rope_chunk_bwd(d, c, s_up, s_dn):
    return d * c + pltpu.roll(d * s_up, ROPE_HALF, 1) + pltpu.roll(d * s_dn, LANES - ROPE_HALF, 1)


def _rope_wide(t, c, s_up, s_dn, fn):
    chunks = [fn(t[:, q * LANES:(q + 1) * LANES], c, s_up, s_dn) for q in range(t.shape[1] // LANES)]
    return chunks[0] if len(chunks) == 1 else jnp.concatenate(chunks, axis=1)


def _halo_before_spec(tm, D):
    return pl.BlockSpec((CONV_HALO, D), lambda i: (jnp.maximum(i * (tm // CONV_HALO) - 1, 0), 0))


def dwconv_fwd(g, w_dw, b_dw, ln_g, ln_b, *, tm, comm=None):
    T, D = g.shape
    nl = D // LANES

    def body(g_ref, gh_ref, w_ref, b_ref, lg_ref, lb_ref, c_ref, s_ref, win_ref):
        i = pl.program_id(0)
        win_ref[0:CONV_HALO, :] = jnp.where(i > 0, gh_ref[...], 0.0)
        win_ref[CONV_HALO:, :] = g_ref[...]

        def lane_chunk(q, carry):
            ls = pl.ds(pl.multiple_of(q * LANES, LANES), LANES)
            acc = jnp.broadcast_to(b_ref[:, ls], (tm, LANES))
            for k in range(CONV_WIDTH):
                acc = acc + win_ref[pl.ds(CONV_HALO - (CONV_WIDTH - 1) + k, tm), ls] * w_ref[k:k + 1, ls]
            c_ref[:, ls] = acc
            return carry

        lax.fori_loop(0, nl, lane_chunk, 0)
        n = _ln_fwd(c_ref[...], lg_ref[...], lb_ref[...])
        s_ref[...] = (n * _sigmoid(n)).astype(s_ref.dtype)

    row = pl.BlockSpec((1, D), lambda i: (0, 0))
    return _call(
        body, comm, name="dwconv_fwd", grid=(T // tm,),
        in_specs=[pl.BlockSpec((tm, D), lambda i: (i, 0)), _halo_before_spec(tm, D),
                  pl.BlockSpec((CONV_HALO, D), lambda i: (0, 0)), row, row, row],
        out_specs=[pl.BlockSpec((tm, D), lambda i: (i, 0)), pl.BlockSpec((tm, D), lambda i: (i, 0))],
        out_shape=[_sds((T, D), F32), _sds((T, D), BF16)],
        scratch_shapes=[pltpu.VMEM((tm + CONV_HALO, D), F32)],
        sem=("parallel",), operands=[g, g, w_dw, b_dw, ln_g, ln_b])


def dwconv_bwd(dc, g, ha, hg, w_dw, *, tm, comm=None):
    T, D = g.shape
    nl = D // LANES
    last = T // CONV_HALO - 1
    nt = T // tm

    def body(dc_ref, dcn_ref, g_ref, gh_ref, ha_ref, hg_ref, w_ref, dh_ref, dw_ref, dbdw_ref, dbin_ref,
             win_ref, dwin_ref, dg_ref):
        i = pl.program_id(0)
        win_ref[0:CONV_HALO, :] = jnp.where(i > 0, gh_ref[...], 0.0)
        win_ref[CONV_HALO:, :] = g_ref[...]
        dwin_ref[0:tm, :] = dc_ref[...]
        dwin_ref[tm:, :] = jnp.where(i < nt - 1, dcn_ref[...], 0.0)

        @pl.when(i == 0)
        def _():
            dw_ref[...] = jnp.zeros_like(dw_ref)

        def lane_chunk(q, carry):
            ls = pl.ds(pl.multiple_of(q * LANES, LANES), LANES)
            dcv = dwin_ref[0:tm, ls]
            acc = jnp.zeros((tm, LANES), F32)
            for k in range(CONV_WIDTH):
                acc = acc + dwin_ref[pl.ds(CONV_WIDTH - 1 - k, tm), ls] * w_ref[k:k + 1, ls]
                tap = win_ref[pl.ds(CONV_HALO - (CONV_WIDTH - 1) + k, tm), ls]
                dw_ref[k:k + 1, ls] += jnp.sum(dcv * tap, axis=0, keepdims=True)
            dg_ref[:, ls] = acc
            return carry

        lax.fori_loop(0, nl, lane_chunk, 0)
        dg = dg_ref[...]
        ha = ha_ref[...].astype(F32)
        sg = _sigmoid(hg_ref[...].astype(F32))
        d_ha = dg * sg
        d_hg = dg * ha * sg * (1.0 - sg)
        dh_ref[:, 0:D] = d_ha.astype(dh_ref.dtype)
        dh_ref[:, D:] = d_hg.astype(dh_ref.dtype)
        _init_or_add(dbdw_ref, i, jnp.sum(dc_ref[...], axis=0, keepdims=True))
        _init_or_add(dbin_ref, i, jnp.concatenate([jnp.sum(d_ha, axis=0, keepdims=True),
                                                   jnp.sum(d_hg, axis=0, keepdims=True)], axis=1))

    tile = pl.BlockSpec((tm, D), lambda i: (i, 0))
    return _call(
        body, comm, name="dwconv_bwd", grid=(nt,),
        in_specs=[tile,
                  pl.BlockSpec((CONV_HALO, D), lambda i: (jnp.minimum((i + 1) * (tm // CONV_HALO), last), 0)),
                  tile, _halo_before_spec(tm, D),
                  tile, tile, pl.BlockSpec((CONV_HALO, D), lambda i: (0, 0))],
        out_specs=[pl.BlockSpec((tm, 2 * D), lambda i: (i, 0)), pl.BlockSpec((CONV_HALO, D), lambda i: (0, 0)),
                   pl.BlockSpec((1, D), lambda i: (0, 0)), pl.BlockSpec((1, 2 * D), lambda i: (0, 0))],
        out_shape=[_sds((T, 2 * D), BF16), _sds((CONV_HALO, D), F32), _sds((1, D), F32), _sds((1, 2 * D), F32)],
        scratch_shapes=[pltpu.VMEM((tm + CONV_HALO, D), F32), pltpu.VMEM((tm + CONV_HALO, D), F32),
                        pltpu.VMEM((tm, D), F32)],
        sem=("arbitrary",), operands=[dc, dc, g, g, ha, hg, w_dw])


def _attn_specs(HD, KVD):
    B = ATT_BLOCK
    kv_col = HD // (2 * KVD)
    return [pl.BlockSpec((B, HD), lambda n: (n, 0)),
            pl.BlockSpec((B, 2 * KVD), lambda n: (n, kv_col)),
            pl.BlockSpec((B, 2 * KVD), lambda n: (jnp.maximum(n - 1, 0), kv_col))]


def _band_mask(n, rows):
    r = lax.broadcasted_iota(jnp.int32, (rows, 2 * ATT_BLOCK), 0) % ATT_BLOCK
    j = lax.broadcasted_iota(jnp.int32, (rows, 2 * ATT_BLOCK), 1)
    return (j > r) & (j <= r + ATT_BLOCK) & ((n > 0) | (j >= ATT_BLOCK))


def _group_rows(ref_or_val, g, q_per_kv):
    return jnp.concatenate([ref_or_val[:, (g * q_per_kv + i) * HEAD_DIM:(g * q_per_kv + i + 1) * HEAD_DIM]
                            for i in range(q_per_kv)], axis=0)


def _group_col(ref, g, q_per_kv):
    return jnp.concatenate([ref[:, g * q_per_kv + i:g * q_per_kv + i + 1] for i in range(q_per_kv)], axis=0)


def attn_fwd(qkv, sinks, n_heads, n_kv, comm=None):
    T = qkv.shape[0]
    HD, KVD = n_heads * HEAD_DIM, n_kv * HEAD_DIM
    B = ATT_BLOCK
    qpk = n_heads // n_kv
    scale = 1.0 / math.sqrt(HEAD_DIM)

    def body(q_ref, kvc_ref, kvp_ref, sink_ref, o_ref, lse_ref):
        n = pl.program_id(0)
        mask = _band_mask(n, qpk * B)
        for g in range(n_kv):
            ks = slice(g * HEAD_DIM, (g + 1) * HEAD_DIM)
            vs = slice(KVD + g * HEAD_DIM, KVD + (g + 1) * HEAD_DIM)
            k_band = jnp.concatenate([kvp_ref[:, ks], kvc_ref[:, ks]], axis=0)
            v_band = jnp.concatenate([kvp_ref[:, vs], kvc_ref[:, vs]], axis=0)
            q_g = _group_rows(q_ref, g, qpk)
            sink = jnp.concatenate([jnp.broadcast_to(sink_ref[:, g * qpk + i:g * qpk + i + 1], (B, 1))
                                    for i in range(qpk)], axis=0)
            s = jnp.where(mask, _dot(q_g, k_band, ((1,), (1,))) * scale, MASK_VALUE)
            m = jnp.maximum(jnp.max(s, axis=-1, keepdims=True), sink)
            e = jnp.exp(s - m)
            l = jnp.sum(e, axis=-1, keepdims=True) + jnp.exp(sink - m)
            lse = m + jnp.log(l)
            probs = jnp.exp(s - lse)
            o_g = _dot(probs, v_band, ((1,), (0,)))
            for i in range(qpk):
                h = g * qpk + i
                o_ref[:, h * HEAD_DIM:(h + 1) * HEAD_DIM] = o_g[i * B:(i + 1) * B, :].astype(o_ref.dtype)
                lse_ref[:, h:h + 1] = lse[i * B:(i + 1) * B, :]

    return _call(
        body, comm, name="attn_fwd", grid=(T // B,),
        in_specs=_attn_specs(HD, KVD) + [pl.BlockSpec((1, n_heads), lambda n: (0, 0))],
        out_specs=[pl.BlockSpec((B, HD), lambda n: (n, 0)), pl.BlockSpec((B, n_heads), lambda n: (n, 0))],
        out_shape=[_sds((T, HD), BF16), _sds((T, n_heads), F32)],
        sem=("parallel",), operands=[qkv, qkv, qkv, sinks])


def attn_bwd(qkv, do, lse, sinks, rope, n_heads, n_kv, comm=None):
    T = qkv.shape[0]
    HD, KVD = n_heads * HEAD_DIM, n_kv * HEAD_DIM
    B = ATT_BLOCK
    qpk = n_heads // n_kv
    scale = 1.0 / math.sqrt(HEAD_DIM)
    nb = T // B

    def body(q_ref, kvc_ref, kvp_ref, do_ref, lse_ref, sink_ref, c_ref, su_ref, sd_ref, dq_ref, dkv_ref, dsink_ref):
        n = pl.program_id(0)
        mask = _band_mask(n, qpk * B)

        @pl.when(n == 0)
        def _():
            dkv_ref[...] = jnp.zeros_like(dkv_ref)
            dsink_ref[...] = jnp.zeros_like(dsink_ref)

        dks, dvs, dqs, dsk = [], [], [None] * n_heads, []
        for g in range(n_kv):
            ks = slice(g * HEAD_DIM, (g + 1) * HEAD_DIM)
            vs = slice(KVD + g * HEAD_DIM, KVD + (g + 1) * HEAD_DIM)
            k_band = jnp.concatenate([kvp_ref[:, ks], kvc_ref[:, ks]], axis=0)
            v_band = jnp.concatenate([kvp_ref[:, vs], kvc_ref[:, vs]], axis=0)
            q_g = _group_rows(q_ref, g, qpk)
            do_g = _group_rows(do_ref, g, qpk)
            lse_g = _group_col(lse_ref, g, qpk)
            s = jnp.where(mask, _dot(q_g, k_band, ((1,), (1,))) * scale, MASK_VALUE)
            probs = jnp.exp(s - lse_g)
            dp = _dot(do_g, v_band, ((1,), (1,)))
            delta = jnp.sum(probs * dp, axis=-1, keepdims=True)
            ds = (probs * (dp - delta) * scale).astype(BF16)
            dq_g = _dot(ds, k_band, ((1,), (0,)))
            dks.append(_dot(ds, q_g, ((0,), (0,))))
            dvs.append(_dot(probs.astype(BF16), do_g, ((0,), (0,))))
            for i in range(qpk):
                h = g * qpk + i
                dqs[h] = dq_g[i * B:(i + 1) * B, :]
                p_sink = jnp.exp(sink_ref[:, h:h + 1] - lse_g[i * B:(i + 1) * B, :])
                dsk.append(-jnp.sum(p_sink * delta[i * B:(i + 1) * B, :], axis=0, keepdims=True))
        dq = jnp.concatenate(dqs, axis=1)
        dq_ref[...] = _rope_wide(dq, c_ref[...], su_ref[...], sd_ref[...], _rope_chunk_bwd).astype(dq_ref.dtype)
        dkv = jnp.concatenate(dks + dvs, axis=1)
        prev = pl.ds(pl.multiple_of(jnp.maximum(n - 1, 0) * B, B), B)
        cur = pl.ds(pl.multiple_of(n * B, B), B)
        dkv_ref[prev, :] += dkv[0:B, :]
        dkv_ref[cur, :] += dkv[B:, :]
        dsink_ref[...] += jnp.concatenate(dsk, axis=1)

    tab = pl.BlockSpec((B, LANES), lambda n: (n, 0))
    return _call(
        body, comm, name="attn_bwd", grid=(nb,),
        in_specs=_attn_specs(HD, KVD) + [pl.BlockSpec((B, HD), lambda n: (n, 0)),
                                         pl.BlockSpec((B, n_heads), lambda n: (n, 0)),
                                         pl.BlockSpec((1, n_heads), lambda n: (0, 0)), tab, tab, tab],
        out_specs=[pl.BlockSpec((B, HD), lambda n: (n, 0)), pl.BlockSpec((T, 2 * KVD), lambda n: (0, 0)),
                   pl.BlockSpec((1, n_heads), lambda n: (0, 0))],
        out_shape=[_sds((T, HD + 2 * KVD), BF16), _sds((T, 2 * KVD), F32), _sds((1, n_heads), F32)],
        sem=("arbitrary",), operands=[qkv, qkv, qkv, do, lse, sinks, *rope])


def dkv_finish(d_qkv, dkv, rope, HD, KVD, *, tm):
    T = dkv.shape[0]
    kv_col = HD // (2 * KVD)

    def body(alias_ref, dkv_ref, c_ref, su_ref, sd_ref, o_ref):
        del alias_ref
        dk = _rope_wide(dkv_ref[:, 0:KVD], c_ref[...], su_ref[...], sd_ref[...], _rope_chunk_bwd)
        o_ref[:, 0:KVD] = dk.astype(o_ref.dtype)
        o_ref[:, KVD:] = dkv_ref[:, KVD:].astype(o_ref.dtype)

    tab = pl.BlockSpec((tm, LANES), lambda i: (i, 0))
    return pl.pallas_call(
        body, name="dkv_finish", grid=(T // tm,),
        in_specs=[pl.BlockSpec(memory_space=pl.ANY), pl.BlockSpec((tm, 2 * KVD), lambda i: (i, 0)), tab, tab, tab],
        out_specs=pl.BlockSpec((tm, 2 * KVD), lambda i: (i, kv_col)),
        out_shape=_sds(d_qkv.shape, d_qkv.dtype), input_output_aliases={0: 0},
        compiler_params=_params(("parallel",)))(d_qkv, dkv, *rope)


def ple_bwd_elem(d_out, pp, gg, *, tm):
    T, D = d_out.shape

    def body(d_ref, pp_ref, gg_ref, dpp_ref, dgg_ref):
        d = d_ref[...]
        sg = _sigmoid(gg_ref[...].astype(F32))
        dpp_ref[...] = (d * sg).astype(dpp_ref.dtype)
        dgg_ref[...] = (d * pp_ref[...].astype(F32) * sg * (1.0 - sg)).astype(dgg_ref.dtype)

    tile = pl.BlockSpec((tm, D), lambda i: (i, 0))
    return pl.pallas_call(
        body, name="ple_bwd_elem", grid=(T // tm,), in_specs=[tile, tile, tile], out_specs=[tile, tile],
        out_shape=[_sds((T, D), BF16), _sds((T, D), BF16)], compiler_params=_params(("parallel",)))(d_out, pp, gg)


def adamw(name, recvs, w, m, v, *, ta):
    L, a, b = w.shape
    assert a % ta == 0 and len(recvs) == L
    c1 = 1.0 - ADAM_B1 ** ADAM_STEP
    c2 = 1.0 - ADAM_B2 ** ADAM_STEP

    def body(*refs):
        r_refs = refs[:L]
        w_ref, m_ref, v_ref, g_ref, d_ref, nm_ref, nv_ref = refs[L:]
        layer = pl.program_id(0)
        for l in range(L):
            @pl.when(layer == l)
            def _(r_ref=r_refs[l]):
                g = r_ref[0].astype(F32)
                for s in range(1, N_DEV):
                    g = g + r_ref[s].astype(F32)
                nm = ADAM_B1 * m_ref[...] + (1.0 - ADAM_B1) * g
                nv = ADAM_B2 * v_ref[...] + (1.0 - ADAM_B2) * jnp.square(g)
                m_hat = nm / c1
                v_hat = nv / c2
                g_ref[...] = g
                d_ref[...] = -ADAM_LR * (m_hat / (jnp.sqrt(v_hat) + ADAM_EPS) + ADAM_WD * w_ref[...])
                nm_ref[...] = nm
                nv_ref[...] = nv

    blk = pl.BlockSpec((None, ta, b), lambda l, i: (l, i, 0))
    out = _sds((L, a, b), F32)
    r_specs = [pl.BlockSpec((N_DEV, ta, b), lambda l, i, ll=ll: (0, jnp.where(l == ll, i, 0), 0)) for ll in range(L)]
    return pl.pallas_call(
        body, name=name, grid=(L, a // ta), in_specs=r_specs + [blk, blk, blk],
        out_specs=[blk, blk, blk, blk], out_shape=[out, out, out, out],
        compiler_params=_params(("arbitrary", "arbitrary")))(*recvs, w, m, v)


def _pack_rows(parts):
    out = []
    for arr, rows in parts:
        arr = arr.reshape(-1, LANES).astype(F32)
        out.append(jnp.pad(arr, ((0, rows - arr.shape[0]), (0, 0))))
    return jnp.concatenate(out, axis=0)


def _small_rows(a, prefix):
    return _pack_rows([(a[prefix + name], rows) for name, rows in SMALL_SHARDED])


def _unpack_small(packed, a):
    out, r0 = {}, 0
    for name, rows in SMALL_SHARDED:
        shape = a[name].shape
        used = math.prod(shape) // LANES
        out[name] = packed[r0:r0 + used].reshape(shape)
        r0 += rows
    return out


def _rep_rows(a, prefix):
    parts = []
    for name in REPLICATED:
        arr = a[prefix + name]
        if arr.size % LANES:
            arr = jnp.pad(arr.reshape(1, -1), ((0, 0), (0, LANES - arr.size % LANES)))
        rows = -(-arr.size // LANES)
        parts.append((arr, -(-rows // SUBLANES) * SUBLANES))
    return _pack_rows(parts)


def _unpack_rep(packed, a):
    out, r0 = {}, 0
    for name in REPLICATED:
        shape = a[name].shape
        size = math.prod(shape)
        rows = -(-size // LANES)
        out[name] = packed[r0:r0 + rows].reshape(-1)[:size].reshape(shape)
        r0 += -(-rows // SUBLANES) * SUBLANES
    return out


def _step(a):
    x = a['x'][0]
    T, D = x.shape
    tgt = a['loss_target'][0]
    p_in = [a['p'][i, 0] for i in range(DEPTH)]
    PLE = p_in[0].shape[1]
    n_heads = a['attn_sinks'].shape[1]
    HD = n_heads * HEAD_DIM
    KVD = a['kv_w_k'].shape[1]
    n_kv = KVD // HEAD_DIM
    F = a['mlp_w_down'].shape[1] * N_DEV
    tm = min(TOKEN_TILE, T)
    tm2 = min(2 * TOKEN_TILE, T)
    tmc = min(256, T)
    alpha = DEEPNORM_ALPHA
    xb = x.astype(BF16)
    p_b = [p.astype(BF16) for p in p_in]

    def shard3(w):
        return w.reshape((1,) + w.shape) if w.ndim == 2 else w

    def gather(*specs):
        return _Job([shard3(a[nm])[l].astype(BF16) for nm, l in specs], gather=True)

    W_in, small_full = comm_only("gather_first", _Job([a['conv_w_in'][0].astype(BF16), _small_rows(a, '')], True))
    r0, small = 0, {}
    for name, rows in SMALL_SHARDED:
        small[name] = small_full[:, r0:r0 + rows]
        r0 += rows
    b_in = small['conv_b_in'][:, 0:2 * D // N_DEV // LANES].reshape(1, 2 * D)
    w_dw = jnp.transpose(small['conv_w_dw'], (1, 0, 2)).reshape(CONV_HALO, D)
    b_dw, cln_g, cln_b, b_out = (small[nm][:, 0].reshape(1, D) for nm in
                                 ('conv_b_dw', 'conv_ln_g', 'conv_ln_b', 'conv_b_out'))
    W_up, W_down, W_proj, W_gate = {}, {}, {}, {}
    mix_g, mix_b, mlp_g, mlp_b = a['mix_ln_g'], a['mix_ln_b'], a['mlp_ln_g'], a['mlp_ln_b']
    rope = _rope_tables(T)

    def set_ple_weights(li, g_proj, g_gate):
        W_proj[li] = jnp.transpose(g_proj, (1, 0, 2)).reshape(1, PLE, D)
        W_gate[li] = g_gate.reshape(1, D, D)

    def row(v, i):
        return v[i:i + 1]

    def res_ln_epi(coef):
        def epi(accs, ex, out, i):
            acc = accs[0] if isinstance(accs, list) else accs
            n_ex = len(ex)
            res_ref, g_ref, b_ref = ex[n_ex - 3], ex[n_ex - 2], ex[n_ex - 1]
            z = coef * res_ref[...] + acc
            if n_ex == 4:
                z = z + ex[0][...]
            out[0][...] = z
            xo = _ln_fwd(z, g_ref[...], b_ref[...])
            out[1][...] = xo
            out[2][...] = xo.astype(BF16)
        return epi

    res_ln_outs = [(_sds((T, D), F32), 'tile'), (_sds((T, D), F32), 'tile'), (_sds((T, D), BF16), 'tile')]

    def mlp_fwd(li, xin, xin_b, up_comm=None, down_comm=None):
        def up_epi(accs, ex, out, i):
            u = accs[0]
            out[0][...] = u.astype(BF16)
            out[1][...] = jnp.square(jnp.maximum(u, 0.0)).astype(BF16)
        res = mm_nn(f"mlp_up_{li}", [(xin_b, W_up[li], 0)], [], [(_sds((T, F), BF16), 'tile')] * 2, up_epi,
                    tm=tm2, tn=min(1024, F), comm=up_comm)
        (u, act), got_up = res if up_comm is not None else (res, ())
        if li not in W_down:
            W_down[li] = got_up[0].reshape(1, F, D)
        res = mm_nn(f"mlp_down_{li}", [(act, W_down[li], 0)],
                    [(xin, 'tile'), (row(mlp_g, li), 'row'), (row(mlp_b, li), 'row')],
                    res_ln_outs, res_ln_epi(alpha), tm=tm, tn=D, tk=min(1024, F), comm=down_comm)
        (z, xo, xo_b), got_down = res if down_comm is not None else (res, ())
        return u, act, z, xo, xo_b, got_up, got_down

    def ple_fwd(li, xin, xin_b, with_loss):
        def epi(accs, ex, out, i):
            pp, gg = accs
            xo = ex[0][...] + pp * _sigmoid(gg)
            out[1][...] = pp.astype(BF16)
            out[2][...] = gg.astype(BF16)
            if with_loss:
                err = xo - ex[1][...]
                out[0][...] = err * (1.0 / D)
                _init_or_add(out[3], i, jnp.sum(err * err, axis=0, keepdims=True) * (0.5 / D))
            else:
                out[0][...] = xo
                out[3][...] = xo.astype(BF16)
        extras = [(xin, 'tile')] + ([(tgt, 'tile')] if with_loss else [])
        outs = [(_sds((T, D), F32), 'tile'), (_sds((T, D), BF16), 'tile'), (_sds((T, D), BF16), 'tile')]
        outs.append((_sds((1, D), F32), 'rowacc') if with_loss else (_sds((T, D), BF16), 'tile'))
        return mm_nn(f"ple_{li}", [(p_b[li], W_proj[li], 0), (xin_b, W_gate[li], 0)], extras, outs, epi, tm=tm, tn=D)

    assert D // N_DEV == LANES
    (g0, ha0, hg0), (g_wout,) = _glu(xb, W_in, b_in, T, D, tm, gather(('conv_w_out', 0)))
    W_out = g_wout.reshape(1, D, D)
    (c0, s0), (W_up[0],) = dwconv_fwd(g0, w_dw, b_dw, cln_g, cln_b, tm=tmc, comm=gather(('mlp_w_up', 0)))
    (z1, x1, x1b), got = mm_nn("conv_out", [(s0, W_out, 0)],
                               [(b_out, 'row'), (x, 'tile'), (row(mix_g, 0), 'row'), (row(mix_b, 0), 'row')],
                               res_ln_outs, res_ln_epi(alpha), tm=tm, tn=D,
                               comm=gather(('ple_w_proj', 0), ('ple_w_gate', 0)))
    set_ple_weights(0, *got)
    u0, act0, z2, x2, x2b, _, got = mlp_fwd(
        0, x1, x1b, up_comm=gather(('mlp_w_down', 0)),
        down_comm=gather(('attn_w_q', 0), ('kv_w_k', 0), ('kv_w_v', 0), ('attn_w_o', 0)))
    W_qkv = jnp.concatenate([got[0].reshape(D, HD), got[1].reshape(D, KVD), got[2].reshape(D, KVD)], axis=1)[None]
    W_o = got[3].reshape(1, HD, D)
    x3, pp0, gg0, x3b = ple_fwd(0, x2, x2b, False)

    def qkv_epi(accs, ex, out, i):
        t = accs[0]
        c, su, sd = ex[0][...], ex[1][...], ex[2][...]
        out[0][:, 0:HD + KVD] = _rope_wide(t[:, 0:HD + KVD], c, su, sd, _rope_chunk).astype(BF16)
        out[0][:, HD + KVD:] = t[:, HD + KVD:].astype(BF16)
    NQ = HD + 2 * KVD
    qkv, = mm_nn("qkv_rope", [(x3b, W_qkv, 0)], [(t, 'tab') for t in rope], [(_sds((T, NQ), BF16), 'tile')], qkv_epi,
                 tm=tm, tn=NQ)
    (o1, lse1), (W_up[1], g_down1) = attn_fwd(qkv, a['attn_sinks'], n_heads, n_kv,
                                              comm=gather(('mlp_w_up', 1), ('mlp_w_down', 1)))
    W_down[1] = g_down1.reshape(1, F, D)
    (z3, x4, x4b), got = mm_nn("attn_out", [(o1, W_o, 0)],
                               [(x3, 'tile'), (row(mix_g, 1), 'row'), (row(mix_b, 1), 'row')],
                               res_ln_outs, res_ln_epi(alpha), tm=tm, tn=D,
                               comm=gather(('ple_w_proj', 1), ('ple_w_gate', 1)))
    set_ple_weights(1, *got)
    u1, act1, z4, x5, x5b, _, _ = mlp_fwd(1, x4, x4b)
    dy, pp1, gg1, loss_row = ple_fwd(1, x5, x5b, True)
    loss_local = jnp.sum(loss_row)

    grads = {}

    def ln_bwd_epi(coef, with_colsum):
        def epi(acc, ex, out, i):
            d_x = acc + coef * ex[0][...]
            dz, dg, db = _ln_bwd(d_x, ex[1][...], ex[2][...])
            out[0][...] = dz
            out[1][...] = dz.astype(BF16)
            _init_or_add(out[2], i, dg)
            _init_or_add(out[3], i, db)
            if with_colsum:
                _init_or_add(out[4], i, jnp.sum(dz, axis=0, keepdims=True))
        return epi

    def ln_bwd_outs(with_colsum):
        outs = [(_sds((T, D), F32), 'tile'), (_sds((T, D), BF16), 'tile'), (_sds((1, D), F32), 'rowacc'),
                (_sds((1, D), F32), 'rowacc')]
        return outs + ([(_sds((1, D), F32), 'rowacc')] if with_colsum else [])

    def ple_bwd(li, d_out, xin, pp, gg, z_mlp):
        d_pp, d_gg = ple_bwd_elem(d_out, pp, gg, tm=tm)
        grads[('ple_w_proj', li)] = mm_tn(f"d_ple_proj_{li}", p_b[li], d_pp, D, F32, tm=tm, tk=PLE, tn=D)
        grads[('ple_w_gate', li)] = mm_tn(f"d_ple_gate_{li}", xin, d_gg, D, BF16, tm=tm, tk=D, tn=D)
        dz, dzb, dg, db = mm_nt(f"ple_dx_{li}", d_gg, W_gate[li],
                                [(d_out, 'tile'), (z_mlp, 'tile'), (row(mlp_g, li), 'row')],
                                ln_bwd_outs(False), ln_bwd_epi(1.0, False), tm=tm, tko=D, tc=D)
        grads[('mlp_ln_g', li)], grads[('mlp_ln_b', li)] = dg, db
        return dz, dzb

    recv = {}
    wqkv_cols = {'attn_w_q': (0, HD), 'kv_w_k': (HD, HD + KVD), 'kv_w_v': (HD + KVD, NQ)}

    def piece(name, li):
        if name == 'conv_w_in':
            return grads['conv_w_in']
        if name == 'mlp_w_up':
            return grads[('mlp_w_up', li)]
        if name == 'ple_w_proj':
            return jnp.transpose(grads[('ple_w_proj', li)][0].reshape(PLE, N_DEV, D // N_DEV), (1, 0, 2))
        if name in wqkv_cols:
            g = grads['w_qkv'][:, wqkv_cols[name][0]:wqkv_cols[name][1]]
        else:
            g = grads[name] if name in grads else grads[(name, li)]
            g = g[0]
        return g.reshape(N_DEV, g.shape[0] // N_DEV, g.shape[1])

    def scatter(*specs):
        job = _Job([piece(nm, li) for nm, li in specs], gather=False)
        job.specs = specs
        return job

    def keep(job, got):
        for spec, r in zip(job.specs, got):
            recv[spec] = r

    def mlp_bwd(li, dz, dzb, xin, u, act, z_mix, with_colsum, make_comm):
        grads[('mlp_w_down', li)] = mm_tn(f"d_mlp_down_{li}", act, dzb, D, BF16, tm=tm, tk=min(1024, F), tn=D)

        def du_epi(acc, ex, out, i):
            out[0][...] = (acc * (2.0 * jnp.maximum(ex[0][...].astype(F32), 0.0))).astype(BF16)
        du, = mm_nt(f"mlp_du_{li}", dzb, W_down[li], [(u, 'tile')], [(_sds((T, F), BF16), 'tile')], du_epi,
                    tm=tm2, tko=min(1024, F), tc=D)
        grads[('mlp_w_up', li)] = mm_tn(f"d_mlp_up_{li}", xin, du, F // N_DEV, BF16, tm=tm, tk=D, tn=min(1024, F))
        job = make_comm()
        res, got = mm_nt(f"mlp_dx_{li}", du, W_up[li], [(dz, 'tile'), (z_mix, 'tile'), (row(mix_g, li), 'row')],
                         ln_bwd_outs(with_colsum), ln_bwd_epi(alpha, with_colsum), tm=tm, tko=D, tc=min(1024, F),
                         comm=job)
        keep(job, got)
        grads[('mix_ln_g', li)], grads[('mix_ln_b', li)] = res[2], res[3]
        return res

    dz4, dz4b = ple_bwd(1, dy, x5b, pp1, gg1, z4)
    dz3, dz3b, _, _ = mlp_bwd(1, dz4, dz4b, x4b, u1, act1, z3, False, lambda: scatter(('mlp_w_up', 1)))
    grads['attn_w_o'] = mm_tn("d_attn_wo", o1, dz3b, D, BF16, tm=tm, tk=HD, tn=D)

    def do_epi(acc, ex, out, i):
        out[0][...] = acc.astype(BF16)
    do1, = mm_nt("attn_do", dz3b, W_o, [], [(_sds((T, HD), BF16), 'tile')], do_epi, tm=tm, tko=HD, tc=D)
    job = scatter(('mlp_w_down', 1), ('ple_w_gate', 1), ('ple_w_proj', 1))
    (d_qkv, dkv, d_sinks), got = attn_bwd(qkv, do1, lse1, a['attn_sinks'], rope, n_heads, n_kv, comm=job)
    keep(job, got)
    d_qkv = dkv_finish(d_qkv, dkv, rope, HD, KVD, tm=tm)
    grads['w_qkv'] = mm_tn("d_wqkv", x3b, d_qkv, NQ, BF16, tm=tm, tk=D, tn=NQ)[0]

    def dx3_epi(acc, ex, out, i):
        out[0][...] = acc + alpha * ex[0][...]
    dx3, = mm_nt("attn_dx", d_qkv, W_qkv, [(dz3, 'tile')], [(_sds((T, D), F32), 'tile')], dx3_epi,
                 tm=tm, tko=D, tc=NQ)

    dz2, dz2b = ple_bwd(0, dx3, x2b, pp0, gg0, z2)
    dz1, dz1b, _, _, db_out = mlp_bwd(
        0, dz2, dz2b, x1b, u0, act0, z1, True,
        lambda: scatter(('attn_w_o', 0), ('attn_w_q', 0), ('kv_w_k', 0), ('kv_w_v', 0)))
    grads['conv_w_out'] = mm_tn("d_conv_wout", s0, dz1b, D, BF16, tm=tm, tk=D, tn=D)

    def ds_epi(acc, ex, out, i):
        n = _ln_fwd(ex[0][...], ex[1][...], ex[2][...])
        sg = _sigmoid(n)
        dn = acc * (sg * (1.0 + n * (1.0 - sg)))
        dc, dg, db = _ln_bwd(dn, ex[0][...], ex[1][...])
        out[0][...] = dc
        _init_or_add(out[1], i, dg)
        _init_or_add(out[2], i, db)
    job = scatter(('ple_w_gate', 0), ('ple_w_proj', 0))
    (dc0, d_cln_g, d_cln_b), got = mm_nt("conv_ds", dz1b, W_out, [(c0, 'tile'), (cln_g, 'row'), (cln_b, 'row')],
                                         [(_sds((T, D), F32), 'tile'), (_sds((1, D), F32), 'rowacc'),
                                          (_sds((1, D), F32), 'rowacc')], ds_epi, tm=tm, tko=D, tc=D, comm=job)
    keep(job, got)
    job = scatter(('mlp_w_down', 0), ('mlp_w_up', 0))
    (dh0, d_wdw, d_bdw, d_bin), got = dwconv_bwd(dc0, g0, ha0, hg0, w_dw, tm=tmc, comm=job)
    keep(job, got)
    grads['conv_w_in'] = mm_tn("d_conv_win", xb, dh0, 2 * D // N_DEV, BF16, tm=tm, tk=D, tn=D)

    def dx_epi(acc, ex, out, i):
        out[0][...] = acc + alpha * ex[0][...]
    grad_x, = mm_nt("conv_dx", dh0, W_in, [(dz1, 'tile')], [(_sds((T, D), F32), 'tile')], dx_epi,
                    tm=tm, tko=D, tc=D)

    def own_rows(vec, rows_used, rows):
        arr = vec.reshape(N_DEV, rows_used, LANES)
        return jnp.pad(arr, ((0, 0), (0, rows - rows_used), (0, 0)))
    dwdw_dev = jnp.transpose(d_wdw.reshape(CONV_HALO, N_DEV, D // N_DEV), (1, 0, 2))
    lane_rows = D // N_DEV // LANES
    small_grad = jnp.concatenate([
        own_rows(d_bin, 2 * lane_rows, 8), dwdw_dev if lane_rows == 1 else dwdw_dev.reshape(N_DEV, -1, LANES),
        own_rows(d_bdw, lane_rows, 8), own_rows(d_cln_g, lane_rows, 8), own_rows(d_cln_b, lane_rows, 8),
        own_rows(db_out, lane_rows, 8)], axis=1)
    n_small = small_grad.shape[1]

    rep_local = {'mix_ln_g': jnp.concatenate([grads[('mix_ln_g', li)] for li in range(DEPTH)], axis=0),
                 'mix_ln_b': jnp.concatenate([grads[('mix_ln_b', li)] for li in range(DEPTH)], axis=0),
                 'mlp_ln_g': jnp.concatenate([grads[('mlp_ln_g', li)] for li in range(DEPTH)], axis=0),
                 'mlp_ln_b': jnp.concatenate([grads[('mlp_ln_b', li)] for li in range(DEPTH)], axis=0),
                 'attn_sinks': d_sinks}
    rep_grad = _rep_rows(rep_local, '')
    n_rep = rep_grad.shape[0]
    last = _Job([piece('conv_w_in', 0), piece('conv_w_out', 0), small_grad,
                 jnp.broadcast_to(rep_grad[None], (N_DEV, n_rep, LANES))], gather=False)
    recv[('conv_w_in', 0)], recv[('conv_w_out', 0)], recv_small, recv_rep = comm_only("exchange_last", last)

    result = {}
    kinds = ('grad', 'delta', 'new_m', 'new_v')
    w, m, v = (_small_rows(a, pre)[None] for pre in ('', 'm_', 'v_'))
    for kind, arr in zip(kinds, adamw("adamw_small", [recv_small], w, m, v, ta=n_small)):
        for pname, val in _unpack_small(arr[0], a).items():
            result[(kind, pname)] = val
    w, m, v = (_rep_rows(a, pre)[None] for pre in ('', 'm_', 'v_'))
    for kind, arr in zip(kinds, adamw("adamw_rep", [recv_rep], w, m, v, ta=n_rep)):
        for pname, val in _unpack_rep(arr[0], a).items():
            result[(kind, pname)] = val
    for name in BIG_WEIGHTS:
        w, m, v = (shard3(a[pre + name]) for pre in ('', 'm_', 'v_'))
        recvs = [recv[(name, li)] for li in range(w.shape[0])]
        for kind, arr in zip(kinds, adamw("adamw_" + name, recvs, w, m, v, ta=min(256, w.shape[1]))):
            result[(kind, name)] = arr.reshape(a[name].shape)

    loss = lax.psum(loss_local, ("x", "y", "c"))
    out = [loss, grad_x[None]]
    for kind in ('grad', 'delta', 'new_m', 'new_v'):
        out += [result[(kind, name)] for name in WEIGHT_NAMES]
    return tuple(out)


def _glu(x, W_in, b_in, T, D, tm, comm=None):
    n = W_in.shape[2]
    q = 2 if D // n % 2 == 0 else 1
    nt = D // (q * n)
    tn = q * n

    def body(x_ref, wa_ref, wg_ref, ba_ref, bg_ref, g_ref, ha_ref, hg_ref):
        xb = x_ref[...]
        ha = jnp.concatenate([_dot(xb, wa_ref[s], ((1,), (0,))) for s in range(q)], axis=1) + ba_ref[...]
        hg = jnp.concatenate([_dot(xb, wg_ref[s], ((1,), (0,))) for s in range(q)], axis=1) + bg_ref[...]
        g_ref[...] = ha * _sigmoid(hg)
        ha_ref[...] = ha.astype(ha_ref.dtype)
        hg_ref[...] = hg.astype(hg_ref.dtype)

    return _call(
        body, comm, name="conv_in_glu", grid=(T // tm, nt),
        in_specs=[pl.BlockSpec((tm, D), lambda i, j: (i, 0)),
                  pl.BlockSpec((q, D, n), lambda i, j: (j, 0, 0)),
                  pl.BlockSpec((q, D, n), lambda i, j: (j + nt, 0, 0)),
                  pl.BlockSpec((1, tn), lambda i, j: (0, j)), pl.BlockSpec((1, tn), lambda i, j: (0, j + nt))],
        out_specs=[pl.BlockSpec((tm, tn), lambda i, j: (i, j))] * 3,
        out_shape=[_sds((T, D), F32), _sds((T, D), BF16), _sds((T, D), BF16)],
        sem=("parallel", "parallel"), operands=[x, W_in, W_in, b_in, b_in])


def kernel(x, p, conv_w_in, conv_b_in, conv_w_dw, conv_b_dw, conv_ln_g, conv_ln_b, conv_w_out, conv_b_out, kv_w_k, kv_w_v, attn_w_q, attn_sinks, attn_w_o, mix_ln_g, mix_ln_b, mlp_w_up, mlp_w_down, mlp_ln_g, mlp_ln_b, ple_w_proj, ple_w_gate, loss_target, m_conv_w_in, m_conv_b_in, m_conv_w_dw, m_conv_b_dw, m_conv_ln_g, m_conv_ln_b, m_conv_w_out, m_conv_b_out, m_kv_w_k, m_kv_w_v, m_attn_w_q, m_attn_sinks, m_attn_w_o, m_mix_ln_g, m_mix_ln_b, m_mlp_w_up, m_mlp_w_down, m_mlp_ln_g, m_mlp_ln_b, m_ple_w_proj, m_ple_w_gate, v_conv_w_in, v_conv_b_in, v_conv_w_dw, v_conv_b_dw, v_conv_ln_g, v_conv_ln_b, v_conv_w_out, v_conv_b_out, v_kv_w_k, v_kv_w_v, v_attn_w_q, v_attn_sinks, v_attn_w_o, v_mix_ln_g, v_mix_ln_b, v_mlp_w_up, v_mlp_w_down, v_mlp_ln_g, v_mlp_ln_b, v_ple_w_proj, v_ple_w_gate):
    return _step(dict(locals()))
```

```python
import functools
import math

import jax
import jax.numpy as jnp
from jax import lax
from jax.experimental import pallas as pl
from jax.experimental.pallas import tpu as pltpu

F32 = jnp.float32
BF16 = jnp.bfloat16

N_DEV = 8
HEAD_DIM = 64
ROPE_DIM = HEAD_DIM // 4
ROPE_HALF = ROPE_DIM // 2
ROPE_THETA = 500000.0
ATT_BLOCK = 128
CONV_WIDTH = 31
CONV_HALO = 32
LN_EPS = 1e-5
DEPTH = 2
DEEPNORM_ALPHA = (2 * DEPTH) ** 0.25
MASK_VALUE = -1e30

ADAM_LR = 0.001
ADAM_B1 = 0.9
ADAM_B2 = 0.999
ADAM_EPS = 1e-08
ADAM_WD = 0.01
ADAM_STEP = 10

LANES = 128
SUBLANES = 8
VMEM_LIMIT_BYTES = 52 * 1024 * 1024
TOKEN_TILE = 512
MESH_ID = pl.DeviceIdType.MESH
FORWARD_AT_TENTHS = 6

WEIGHT_NAMES = ['conv_w_in', 'conv_b_in', 'conv_w_dw', 'conv_b_dw', 'conv_ln_g', 'conv_ln_b', 'conv_w_out',
                'conv_b_out', 'kv_w_k', 'kv_w_v', 'attn_w_q', 'attn_sinks', 'attn_w_o', 'mix_ln_g', 'mix_ln_b',
                'mlp_w_up', 'mlp_w_down', 'mlp_ln_g', 'mlp_ln_b', 'ple_w_proj', 'ple_w_gate']
BIG_WEIGHTS = ['conv_w_in', 'conv_w_out', 'kv_w_k', 'kv_w_v', 'attn_w_q', 'attn_w_o', 'mlp_w_up', 'mlp_w_down',
               'ple_w_proj', 'ple_w_gate']
SMALL_SHARDED = [('conv_b_in', 8), ('conv_w_dw', 32), ('conv_b_dw', 8), ('conv_ln_g', 8), ('conv_ln_b', 8),
                 ('conv_b_out', 8)]
REPLICATED = ['mix_ln_g', 'mix_ln_b', 'mlp_ln_g', 'mlp_ln_b', 'attn_sinks']


def _params(sem):
    return pltpu.CompilerParams(dimension_semantics=sem, vmem_limit_bytes=VMEM_LIMIT_BYTES)


def _sds(shape, dtype):
    return jax.ShapeDtypeStruct(shape, dtype)


def _my_place():
    x, y, c = lax.axis_index("x"), lax.axis_index("y"), lax.axis_index("c")
    return x, y, c, 4 * x + 2 * y + c


def _peers(x, y, c):
    out = []
    for dx in (0, 1):
        for dy in (0, 1):
            for dc in (0, 1):
                if dx or dy or dc:
                    px, py, pc = x ^ dx, y ^ dy, c ^ dc
                    out.append(((px, py, pc), 4 * px + 2 * py + pc))
    return out


N_CHIPS = 4


def _other_chips(x, y):
    return [((x ^ dx, y ^ dy), 2 * (x ^ dx) + (y ^ dy)) for dx, dy in ((1, 0), (0, 1), (1, 1))]


def _remote(src, dst, send, recv, to):
    return pltpu.make_async_remote_copy(src_ref=src, dst_ref=dst, send_sem=send, recv_sem=recv, device_id=to,
                                        device_id_type=MESH_ID)


def _wait_slabs(buf, count, send, recv, me, sent=True, received=True):
    part = buf.at[pl.ds(0, count)]
    cp = _remote(part, part, send, recv, me)
    if sent:
        cp.wait_send()
    if received:
        cp.wait_recv()


class _DirectJob:
    n_sems = 3

    def __init__(self, items):
        self.sources = list(items)
        self.dests = [_sds(it.shape, it.dtype) for it in items]
        self.n = len(items)

    def start(self, src, dst, send, recv, loc):
        x, y, c, me = _my_place()
        for k in range(self.n):
            here = dst[k].at[pl.ds(me, 1)]
            pltpu.make_async_copy(src[k].at[pl.ds(me, 1)], here, loc.at[k]).start()
            for peer, idx in _peers(x, y, c):
                _remote(src[k].at[pl.ds(idx, 1)], here, send.at[k], recv.at[k], peer).start()

    def forward(self, src, dst, send, recv, loc):
        pass

    def finish(self, src, dst, send, recv, loc):
        x, y, c, me = _my_place()
        for k in range(self.n):
            pltpu.make_async_copy(src[k].at[pl.ds(me, 1)], dst[k].at[pl.ds(me, 1)], loc.at[k]).wait()
            _wait_slabs(dst[k], N_DEV - 1, send.at[k], recv.at[k], (x, y, c))


class _GatherJob:
    n_sems = 5

    def __init__(self, items):
        self.sources = [it.reshape((1,) + it.shape) for it in items]
        self.dests = [_sds((N_DEV,) + it.shape, it.dtype) for it in items]
        self.n = len(items)

    def start(self, src, dst, send_ici, recv_ici, send_d2d, recv_d2d, loc):
        x, y, c, me = _my_place()
        for k in range(self.n):
            here = dst[k].at[pl.ds(me, 1)]
            pltpu.make_async_copy(src[k], here, loc.at[k]).start()
            _remote(src[k], here, send_d2d.at[k], recv_d2d.at[k], (x, y, 1 - c)).start()
            for (px, py), _ in _other_chips(x, y):
                _remote(src[k], here, send_ici.at[k], recv_ici.at[k], (px, py, c)).start()

    def forward(self, src, dst, send_ici, recv_ici, send_d2d, recv_d2d, loc):
        x, y, c, me = _my_place()
        for k in range(self.n):
            _wait_slabs(dst[k], N_CHIPS - 1, send_ici.at[k], recv_ici.at[k], (x, y, c), sent=False)
            for _, q in _other_chips(x, y):
                rows = dst[k].at[pl.ds(2 * q + c, 1)]
                _remote(rows, rows, send_d2d.at[k], recv_d2d.at[k], (x, y, 1 - c)).start()

    def finish(self, src, dst, send_ici, recv_ici, send_d2d, recv_d2d, loc):
        x, y, c, me = _my_place()
        for k in range(self.n):
            pltpu.make_async_copy(src[k], dst[k].at[pl.ds(me, 1)], loc.at[k]).wait()
            _wait_slabs(dst[k], N_CHIPS - 1, send_ici.at[k], recv_ici.at[k], (x, y, c), received=False)
            _wait_slabs(dst[k], N_CHIPS, send_d2d.at[k], recv_d2d.at[k], (x, y, c))


class _PairJob:
    n_sems = 2

    def __init__(self, items):
        self.sources = list(items)
        self.dests = [_sds((N_CHIPS,) + it.shape[1:], it.dtype) for it in items]
        self.n = len(items)

    def start(self, src, dst, send, recv):
        x, y, c, me = _my_place()
        for k in range(self.n):
            for q in range(N_CHIPS):
                _remote(src[k].at[pl.ds(2 * q + 1 - c, 1)], dst[k].at[pl.ds(q, 1)], send.at[k], recv.at[k],
                        (x, y, 1 - c)).start()

    def forward(self, src, dst, send, recv):
        pass

    def finish(self, src, dst, send, recv):
        x, y, c, me = _my_place()
        for k in range(self.n):
            _wait_slabs(dst[k], N_CHIPS, send.at[k], recv.at[k], (x, y, c))


class _ChipJob:
    n_sems = 3

    def __init__(self, items):
        self.sources = list(items)
        self.dests = [_sds(it.shape, it.dtype) for it in items]
        self.n = len(items)

    def start(self, src, dst, send, recv, loc):
        x, y, c, me = _my_place()
        mine = 2 * x + y
        for k in range(self.n):
            here = dst[k].at[pl.ds(mine, 1)]
            pltpu.make_async_copy(src[k].at[pl.ds(mine, 1)], here, loc.at[k]).start()
            for (px, py), q in _other_chips(x, y):
                _remote(src[k].at[pl.ds(q, 1)], here, send.at[k], recv.at[k], (px, py, c)).start()

    def forward(self, src, dst, send, recv, loc):
        pass

    def finish(self, src, dst, send, recv, loc):
        x, y, c, me = _my_place()
        mine = 2 * x + y
        for k in range(self.n):
            pltpu.make_async_copy(src[k].at[pl.ds(mine, 1)], dst[k].at[pl.ds(mine, 1)], loc.at[k]).wait()
            _wait_slabs(dst[k], N_CHIPS - 1, send.at[k], recv.at[k], (x, y, c))


def _job_sems(job):
    return [pltpu.SemaphoreType.DMA((job.n,))] * job.n_sems


def _call(body, comm, *, name, grid, in_specs, out_specs, out_shape, operands, sem, scratch_shapes=(), aliases=None):
    single = not isinstance(out_shape, (list, tuple))
    out_shape = [out_shape] if single else list(out_shape)
    out_specs = [out_specs] if single else list(out_specs)
    if comm is None:
        res = pl.pallas_call(body, name=name, grid=grid, in_specs=list(in_specs), out_specs=out_specs,
                             out_shape=out_shape, scratch_shapes=list(scratch_shapes),
                             input_output_aliases=aliases or {}, compiler_params=_params(sem))(*operands)
        return res[0] if single else res
    n_in, n_out, n_scr, n_c = len(in_specs), len(out_shape), len(scratch_shapes), comm.n
    any_spec = pl.BlockSpec(memory_space=pl.ANY)
    steps = math.prod(grid)
    mid = min(steps - 1, (steps * FORWARD_AT_TENTHS) // 10)

    def hosted(*refs):
        ins, c_src = refs[:n_in], refs[n_in:n_in + n_c]
        outs = refs[n_in + n_c:n_in + n_c + n_out]
        c_dst = refs[n_in + n_c + n_out:n_in + 2 * n_c + n_out]
        scr = refs[n_in + 2 * n_c + n_out:n_in + 2 * n_c + n_out + n_scr]
        sems = refs[n_in + 2 * n_c + n_out + n_scr:]
        step = pl.program_id(0)
        for d in range(1, len(grid)):
            step = step * grid[d] + pl.program_id(d)

        @pl.when(step == 0)
        def _():
            comm.start(c_src, c_dst, *sems)

        @pl.when(step == mid)
        def _():
            comm.forward(c_src, c_dst, *sems)

        body(*ins, *outs, *scr)

        @pl.when(step == steps - 1)
        def _():
            comm.finish(c_src, c_dst, *sems)

    res = pl.pallas_call(hosted, name=name, grid=grid, in_specs=list(in_specs) + [any_spec] * n_c,
                         out_specs=out_specs + [any_spec] * n_c, out_shape=out_shape + comm.dests,
                         scratch_shapes=list(scratch_shapes) + _job_sems(comm), input_output_aliases=aliases or {},
                         compiler_params=_params(("arbitrary",) * len(grid)))(*operands, *comm.sources)
    main = res[:n_out]
    return (main[0] if single else main), res[n_out:]


def comm_only(name, jobs):
    any_spec = pl.BlockSpec(memory_space=pl.ANY)
    n_all = sum(job.n for job in jobs)

    def body(*refs):
        srcs, dsts, sems = refs[:n_all], refs[n_all:2 * n_all], refs[2 * n_all:]
        parts, k0, s0 = [], 0, 0
        for job in jobs:
            parts.append((job, srcs[k0:k0 + job.n], dsts[k0:k0 + job.n], sems[s0:s0 + job.n_sems]))
            k0 += job.n
            s0 += job.n_sems
        for stage in ('start', 'forward', 'finish'):
            for job, src, dst, sem in parts:
                getattr(job, stage)(src, dst, *sem)

    res = pl.pallas_call(body, name=name, in_specs=[any_spec] * n_all, out_specs=[any_spec] * n_all,
                         out_shape=[d for job in jobs for d in job.dests],
                         scratch_shapes=[s for job in jobs for s in _job_sems(job)],
                         )(*[s for job in jobs for s in job.sources])
    out, k0 = [], 0
    for job in jobs:
        out.append(res[k0:k0 + job.n])
        k0 += job.n
    return out


def _dot(a, b, dims):
    if a.dtype != BF16:
        a = a.astype(BF16)
    if b.dtype != BF16:
        b = b.astype(BF16)
    return lax.dot_general(a, b, (dims, ((), ())), preferred_element_type=F32)


def _sigmoid(v):
    return 1.0 / (1.0 + jnp.exp(-v))


def _ln_stats(z):
    mu = jnp.mean(z, axis=-1, keepdims=True)
    zc = z - mu
    var = jnp.mean(zc * zc, axis=-1, keepdims=True)
    return zc * lax.rsqrt(var + LN_EPS)


def _ln_fwd(z, g, b):
    return _ln_stats(z) * g + b


def _ln_bwd(dy, z, g):
    xhat = _ln_stats(z)
    mu = jnp.mean(z, axis=-1, keepdims=True)
    zc = z - mu
    rstd = lax.rsqrt(jnp.mean(zc * zc, axis=-1, keepdims=True) + LN_EPS)
    dxh = dy * g
    m1 = jnp.mean(dxh, axis=-1, keepdims=True)
    m2 = jnp.mean(dxh * xhat, axis=-1, keepdims=True)
    dz = rstd * (dxh - m1 - xhat * m2)
    return dz, jnp.sum(dy * xhat, axis=0, keepdims=True), jnp.sum(dy, axis=0, keepdims=True)


def _extra_spec(shape, kind, tm, tn, ij):
    if kind == 'tile':
        return pl.BlockSpec((tm, tn), lambda *g: (ij(g)[0], ij(g)[1]))
    if kind in ('row', 'rowacc'):
        return pl.BlockSpec((1, tn), lambda *g: (0, ij(g)[1]))
    if kind == 'tab':
        return pl.BlockSpec((tm, LANES), lambda *g: (ij(g)[0], 0))
    raise ValueError(kind)


def mm_nn(name, pairs, extras, outs, epi, *, tm, tn, tk=None, comm=None):
    M = pairs[0][0].shape[0]
    N = outs[0][0].shape[1]
    n_pairs = len(pairs)
    K0 = pairs[0][0].shape[1]
    tk = K0 if tk is None else tk
    nk = K0 // tk
    assert nk == 1 or n_pairs == 1
    assert M % tm == 0 and N % tn == 0 and K0 % tk == 0
    has_rowacc = any(kind == 'rowacc' for _, kind in outs)
    assert not has_rowacc or (N == tn and nk == 1)
    in_specs, operands, slabs = [], [], []
    for a, b, off in pairs:
        K = a.shape[1]
        ktile = K if n_pairs > 1 else tk
        n = b.shape[2]
        assert b.shape[1] == K
        in_specs.append(pl.BlockSpec((tm, ktile), lambda i, j, k: (i, k)))
        if tn <= n:
            assert n % tn == 0
            r = n // tn
            in_specs.append(pl.BlockSpec((None, ktile, tn),
                                         lambda i, j, k, r=r, off=off: ((j + off) // r, k, (j + off) % r)))
            slabs.append(0)
        else:
            assert tn % n == 0
            in_specs.append(pl.BlockSpec((tn // n, ktile, n), lambda i, j, k, off=off: (j + off, k, 0)))
            slabs.append(tn // n)
        operands += [a, b]
    ij = lambda g: (g[0], g[1])
    for arr, kind in extras:
        in_specs.append(_extra_spec(arr.shape, kind, tm, tn, ij))
        operands.append(arr)
    out_specs = [_extra_spec(o.shape, kind, tm, tn, ij) for o, kind in outs]
    n_ex, n_out = len(extras), len(outs)

    def pair_dot(ab, q):
        a = ab[2 * q][...]
        if not slabs[q]:
            return _dot(a, ab[2 * q + 1][...], ((1,), (0,)))
        return jnp.concatenate([_dot(a, ab[2 * q + 1][s], ((1,), (0,))) for s in range(slabs[q])], axis=1)

    def body(*refs):
        ab = refs[:2 * n_pairs]
        ex = refs[2 * n_pairs:2 * n_pairs + n_ex]
        out = refs[2 * n_pairs + n_ex:2 * n_pairs + n_ex + n_out]
        i = pl.program_id(0)
        if nk == 1:
            accs = [pair_dot(ab, q) for q in range(n_pairs)]
            epi(accs, ex, out, i)
        else:
            acc_ref = refs[-1]
            k = pl.program_id(2)

            @pl.when(k == 0)
            def _():
                acc_ref[...] = jnp.zeros_like(acc_ref)

            acc_ref[...] += pair_dot(ab, 0)

            @pl.when(k == nk - 1)
            def _():
                epi([acc_ref[...]], ex, out, i)

    scratch = [pltpu.VMEM((tm, tn), F32)] if nk > 1 else []
    sem = ("arbitrary",) * 3 if has_rowacc else ("parallel", "parallel", "arbitrary")
    return _call(body, comm, name=name, grid=(M // tm, N // tn, nk), in_specs=in_specs, out_specs=out_specs,
                 out_shape=[o for o, _ in outs], scratch_shapes=scratch, sem=sem, operands=operands)


def mm_nt(name, a, b, extras, outs, epi, *, tm, tko, tc, comm=None):
    M, N = a.shape
    J, K, n = b.shape
    assert J * n == N and M % tm == 0 and K % tko == 0 and N % tc == 0
    nc = N // tc
    has_rowacc = any(kind == 'rowacc' for _, kind in outs)
    assert not has_rowacc or K == tko
    if tc <= n:
        assert n % tc == 0
        r = n // tc
        slabs = 0
        b_spec = pl.BlockSpec((None, tko, tc), lambda i, j, c: (c // r, j, c % r))
    else:
        assert tc % n == 0
        slabs = tc // n
        b_spec = pl.BlockSpec((slabs, tko, n), lambda i, j, c: (c, j, 0))
    in_specs = [pl.BlockSpec((tm, tc), lambda i, j, c: (i, c)), b_spec]

    def nt_dot(a_ref, b_ref):
        if not slabs:
            return _dot(a_ref[...], b_ref[...], ((1,), (1,)))
        acc = _dot(a_ref[:, 0:n], b_ref[0], ((1,), (1,)))
        for s in range(1, slabs):
            acc = acc + _dot(a_ref[:, s * n:(s + 1) * n], b_ref[s], ((1,), (1,)))
        return acc

    ij = lambda g: (g[0], g[1])
    operands = [a, b]
    for arr, kind in extras:
        in_specs.append(_extra_spec(arr.shape, kind, tm, tko, ij))
        operands.append(arr)
    out_specs = [_extra_spec(o.shape, kind, tm, tko, ij) for o, kind in outs]
    n_ex, n_out = len(extras), len(outs)

    def body(*refs):
        a_ref, b_ref = refs[:2]
        ex = refs[2:2 + n_ex]
        out = refs[2 + n_ex:2 + n_ex + n_out]
        i = pl.program_id(0)
        if nc == 1:
            epi(nt_dot(a_ref, b_ref), ex, out, i)
        else:
            acc_ref = refs[-1]
            c = pl.program_id(2)

            @pl.when(c == 0)
            def _():
                acc_ref[...] = jnp.zeros_like(acc_ref)

            acc_ref[...] += nt_dot(a_ref, b_ref)

            @pl.when(c == nc - 1)
            def _():
                epi(acc_ref[...], ex, out, i)

    scratch = [pltpu.VMEM((tm, tko), F32)] if nc > 1 else []
    sem = ("arbitrary",) * 3 if has_rowacc else ("parallel", "parallel", "arbitrary")
    return _call(body, comm, name=name, grid=(M // tm, K // tko, nc), in_specs=in_specs, out_specs=out_specs,
                 out_shape=[o for o, _ in outs], scratch_shapes=scratch, sem=sem, operands=operands)


def mm_tn(name, a, d, n, out_dtype, *, tm, tk, tn, comm=None):
    M, K = a.shape
    N = d.shape[1]
    assert d.shape[0] == M and N % n == 0 and N % tn == 0 and K % tk == 0 and M % tm == 0
    nm = M // tm
    if tn <= n:
        assert n % tn == 0
        r = n // tn
        slabs = 0
        o_spec = pl.BlockSpec((None, tk, tn), lambda kk, j, m: (j // r, kk, j % r))
    else:
        assert tn % n == 0
        slabs = tn // n
        o_spec = pl.BlockSpec((slabs, tk, n), lambda kk, j, m: (j, kk, 0))

    def body(a_ref, d_ref, o_ref, acc_ref):
        m = pl.program_id(2)

        @pl.when(m == 0)
        def _():
            acc_ref[...] = jnp.zeros_like(acc_ref)

        acc_ref[...] += _dot(a_ref[...], d_ref[...], ((0,), (0,)))

        @pl.when(m == nm - 1)
        def _():
            if not slabs:
                o_ref[...] = acc_ref[...].astype(o_ref.dtype)
            else:
                for s in range(slabs):
                    o_ref[s] = acc_ref[:, s * n:(s + 1) * n].astype(o_ref.dtype)

    return _call(
        body, comm, name=name, grid=(K // tk, N // tn, nm),
        in_specs=[pl.BlockSpec((tm, tk), lambda kk, j, m: (m, kk)), pl.BlockSpec((tm, tn), lambda kk, j, m: (m, j))],
        out_specs=o_spec, out_shape=_sds((N // n, K, n), out_dtype), scratch_shapes=[pltpu.VMEM((tk, tn), F32)],
        sem=("parallel", "parallel", "arbitrary"), operands=[a, d])


def _init_or_add(ref, i, value):
    @pl.when(i == 0)
    def _():
        ref[...] = value

    @pl.when(i > 0)
    def _():
        ref[...] += value


def _rope_tables(T):
    pos = jnp.arange(T, dtype=F32)
    inv_freq = ROPE_THETA ** (-jnp.arange(0, ROPE_DIM, 2, dtype=F32) / ROPE_DIM)
    ang = pos[:, None] * inv_freq[None, :]
    cos, sin = jnp.cos(ang), jnp.sin(ang)
    ones = jnp.ones((T, HEAD_DIM - ROPE_DIM), F32)
    zeros = jnp.zeros((T, HEAD_DIM - ROPE_DIM), F32)
    zh = jnp.zeros((T, ROPE_HALF), F32)
    c_head = jnp.concatenate([cos, cos, ones], axis=1)
    s_up = jnp.concatenate([-sin, zh, zeros], axis=1)
    s_dn = jnp.concatenate([zh, sin, zeros], axis=1)
    rep = LANES // HEAD_DIM
    return tuple(jnp.tile(t, (1, rep)) for t in (c_head, s_up, s_dn))


def _rope_chunk(t, c, s_up, s_dn):
    return t * c + pltpu.roll(t, LANES - ROPE_HALF, 1) * s_up + pltpu.roll(t, ROPE_HALF, 1) * s_dn


def _rope_chunk_bwd(d, c, s_up, s_dn):
    return d * c + pltpu.roll(d * s_up, ROPE_HALF, 1) + pltpu.roll(d * s_dn, LANES - ROPE_HALF, 1)


def _rope_wide(t, c, s_up, s_dn, fn):
    chunks = [fn(t[:, q * LANES:(q + 1) * LANES], c, s_up, s_dn) for q in range(t.shape[1] // LANES)]
    return chunks[0] if len(chunks) == 1 else jnp.concatenate(chunks, axis=1)


def _halo_before_spec(tm, D):
    return pl.BlockSpec((CONV_HALO, D), lambda i: (jnp.maximum(i * (tm // CONV_HALO) - 1, 0), 0))


def dwconv_fwd(g, w_dw, b_dw, ln_g, ln_b, *, tm, comm=None):
    T, D = g.shape
    nl = D // LANES

    def body(g_ref, gh_ref, w_ref, b_ref, lg_ref, lb_ref, c_ref, s_ref, win_ref):
        i = pl.program_id(0)
        win_ref[0:CONV_HALO, :] = jnp.where(i > 0, gh_ref[...], 0.0)
        win_ref[CONV_HALO:, :] = g_ref[...]

        def lane_chunk(q, carry):
            ls = pl.ds(pl.multiple_of(q * LANES, LANES), LANES)
            acc = jnp.broadcast_to(b_ref[:, ls], (tm, LANES))
            for k in range(CONV_WIDTH):
                acc = acc + win_ref[pl.ds(CONV_HALO - (CONV_WIDTH - 1) + k, tm), ls] * w_ref[k:k + 1, ls]
            c_ref[:, ls] = acc
            return carry

        lax.fori_loop(0, nl, lane_chunk, 0)
        n = _ln_fwd(c_ref[...], lg_ref[...], lb_ref[...])
        s_ref[...] = (n * _sigmoid(n)).astype(s_ref.dtype)

    row = pl.BlockSpec((1, D), lambda i: (0, 0))
    return _call(
        body, comm, name="dwconv_fwd", grid=(T // tm,),
        in_specs=[pl.BlockSpec((tm, D), lambda i: (i, 0)), _halo_before_spec(tm, D),
                  pl.BlockSpec((CONV_HALO, D), lambda i: (0, 0)), row, row, row],
        out_specs=[pl.BlockSpec((tm, D), lambda i: (i, 0)), pl.BlockSpec((tm, D), lambda i: (i, 0))],
        out_shape=[_sds((T, D), F32), _sds((T, D), BF16)],
        scratch_shapes=[pltpu.VMEM((tm + CONV_HALO, D), F32)],
        sem=("parallel",), operands=[g, g, w_dw, b_dw, ln_g, ln_b])


def dwconv_bwd(dc, g, ha, hg, w_dw, *, tm, comm=None):
    T, D = g.shape
    nl = D // LANES
    last = T // CONV_HALO - 1
    nt = T // tm

    def body(dc_ref, dcn_ref, g_ref, gh_ref, ha_ref, hg_ref, w_ref, dh_ref, dw_ref, dbdw_ref, dbin_ref,
             win_ref, dwin_ref, dg_ref):
        i = pl.program_id(0)
        win_ref[0:CONV_HALO, :] = jnp.where(i > 0, gh_ref[...], 0.0)
        win_ref[CONV_HALO:, :] = g_ref[...]
        dwin_ref[0:tm, :] = dc_ref[...]
        dwin_ref[tm:, :] = jnp.where(i < nt - 1, dcn_ref[...], 0.0)

        @pl.when(i == 0)
        def _():
            dw_ref[...] = jnp.zeros_like(dw_ref)

        def lane_chunk(q, carry):
            ls = pl.ds(pl.multiple_of(q * LANES, LANES), LANES)
            dcv = dwin_ref[0:tm, ls]
            acc = jnp.zeros((tm, LANES), F32)
            for k in range(CONV_WIDTH):
                acc = acc + dwin_ref[pl.ds(CONV_WIDTH - 1 - k, tm), ls] * w_ref[k:k + 1, ls]
                tap = win_ref[pl.ds(CONV_HALO - (CONV_WIDTH - 1) + k, tm), ls]
                dw_ref[k:k + 1, ls] += jnp.sum(dcv * tap, axis=0, keepdims=True)
            dg_ref[:, ls] = acc
            return carry

        lax.fori_loop(0, nl, lane_chunk, 0)
        dg = dg_ref[...]
        ha = ha_ref[...].astype(F32)
        sg = _sigmoid(hg_ref[...].astype(F32))
        d_ha = dg * sg
        d_hg = dg * ha * sg * (1.0 - sg)
        dh_ref[:, 0:D] = d_ha.astype(dh_ref.dtype)
        dh_ref[:, D:] = d_hg.astype(dh_ref.dtype)
        _init_or_add(dbdw_ref, i, jnp.sum(dc_ref[...], axis=0, keepdims=True))
        _init_or_add(dbin_ref, i, jnp.concatenate([jnp.sum(d_ha, axis=0, keepdims=True),
                                                   jnp.sum(d_hg, axis=0, keepdims=True)], axis=1))

    tile = pl.BlockSpec((tm, D), lambda i: (i, 0))
    return _call(
        body, comm, name="dwconv_bwd", grid=(nt,),
        in_specs=[tile,
                  pl.BlockSpec((CONV_HALO, D), lambda i: (jnp.minimum((i + 1) * (tm // CONV_HALO), last), 0)),
                  tile, _halo_before_spec(tm, D),
                  tile, tile, pl.BlockSpec((CONV_HALO, D), lambda i: (0, 0))],
        out_specs=[pl.BlockSpec((tm, 2 * D), lambda i: (i, 0)), pl.BlockSpec((CONV_HALO, D), lambda i: (0, 0)),
                   pl.BlockSpec((1, D), lambda i: (0, 0)), pl.BlockSpec((1, 2 * D), lambda i: (0, 0))],
        out_shape=[_sds((T, 2 * D), BF16), _sds((CONV_HALO, D), F32), _sds((1, D), F32), _sds((1, 2 * D), F32)],
        scratch_shapes=[pltpu.VMEM((tm + CONV_HALO, D), F32), pltpu.VMEM((tm + CONV_HALO, D), F32),
                        pltpu.VMEM((tm, D), F32)],
        sem=("arbitrary",), operands=[dc, dc, g, g, ha, hg, w_dw])


def _attn_specs(HD, KVD):
    B = ATT_BLOCK
    kv_col = HD // (2 * KVD)
    return [pl.BlockSpec((B, HD), lambda n: (n, 0)),
            pl.BlockSpec((B, 2 * KVD), lambda n: (n, kv_col)),
            pl.BlockSpec((B, 2 * KVD), lambda n: (jnp.maximum(n - 1, 0), kv_col))]


def _band_mask(n, rows):
    r = lax.broadcasted_iota(jnp.int32, (rows, 2 * ATT_BLOCK), 0) % ATT_BLOCK
    j = lax.broadcasted_iota(jnp.int32, (rows, 2 * ATT_BLOCK), 1)
    return (j > r) & (j <= r + ATT_BLOCK) & ((n > 0) | (j >= ATT_BLOCK))


def _group_rows(ref_or_val, g, q_per_kv):
    return jnp.concatenate([ref_or_val[:, (g * q_per_kv + i) * HEAD_DIM:(g * q_per_kv + i + 1) * HEAD_DIM]
                            for i in range(q_per_kv)], axis=0)


def _group_col(ref, g, q_per_kv):
    return jnp.concatenate([ref[:, g * q_per_kv + i:g * q_per_kv + i + 1] for i in range(q_per_kv)], axis=0)


def attn_fwd(qkv, sinks, n_heads, n_kv, comm=None):
    T = qkv.shape[0]
    HD, KVD = n_heads * HEAD_DIM, n_kv * HEAD_DIM
    B = ATT_BLOCK
    qpk = n_heads // n_kv
    scale = 1.0 / math.sqrt(HEAD_DIM)

    def body(q_ref, kvc_ref, kvp_ref, sink_ref, o_ref, lse_ref):
        n = pl.program_id(0)
        mask = _band_mask(n, qpk * B)
        for g in range(n_kv):
            ks = slice(g * HEAD_DIM, (g + 1) * HEAD_DIM)
            vs = slice(KVD + g * HEAD_DIM, KVD + (g + 1) * HEAD_DIM)
            k_band = jnp.concatenate([kvp_ref[:, ks], kvc_ref[:, ks]], axis=0)
            v_band = jnp.concatenate([kvp_ref[:, vs], kvc_ref[:, vs]], axis=0)
            q_g = _group_rows(q_ref, g, qpk)
            sink = jnp.concatenate([jnp.broadcast_to(sink_ref[:, g * qpk + i:g * qpk + i + 1], (B, 1))
                                    for i in range(qpk)], axis=0)
            s = jnp.where(mask, _dot(q_g, k_band, ((1,), (1,))) * scale, MASK_VALUE)
            m = jnp.maximum(jnp.max(s, axis=-1, keepdims=True), sink)
            e = jnp.exp(s - m)
            l = jnp.sum(e, axis=-1, keepdims=True) + jnp.exp(sink - m)
            lse = m + jnp.log(l)
            probs = jnp.exp(s - lse)
            o_g = _dot(probs, v_band, ((1,), (0,)))
            for i in range(qpk):
                h = g * qpk + i
                o_ref[:, h * HEAD_DIM:(h + 1) * HEAD_DIM] = o_g[i * B:(i + 1) * B, :].astype(o_ref.dtype)
                lse_ref[:, h:h + 1] = lse[i * B:(i + 1) * B, :]

    return _call(
        body, comm, name="attn_fwd", grid=(T // B,),
        in_specs=_attn_specs(HD, KVD) + [pl.BlockSpec((1, n_heads), lambda n: (0, 0))],
        out_specs=[pl.BlockSpec((B, HD), lambda n: (n, 0)), pl.BlockSpec((B, n_heads), lambda n: (n, 0))],
        out_shape=[_sds((T, HD), BF16), _sds((T, n_heads), F32)],
        sem=("parallel",), operands=[qkv, qkv, qkv, sinks])


def attn_bwd(qkv, do, lse, sinks, rope, n_heads, n_kv, comm=None):
    T = qkv.shape[0]
    HD, KVD = n_heads * HEAD_DIM, n_kv * HEAD_DIM
    B = ATT_BLOCK
    qpk = n_heads // n_kv
    scale = 1.0 / math.sqrt(HEAD_DIM)
    nb = T // B

    def body(q_ref, kvc_ref, kvp_ref, do_ref, lse_ref, sink_ref, c_ref, su_ref, sd_ref, dq_ref, dkv_ref, dsink_ref):
        n = pl.program_id(0)
        mask = _band_mask(n, qpk * B)

        @pl.when(n == 0)
        def _():
            dkv_ref[...] = jnp.zeros_like(dkv_ref)
            dsink_ref[...] = jnp.zeros_like(dsink_ref)

        dks, dvs, dqs, dsk = [], [], [None] * n_heads, []
        for g in range(n_kv):
            ks = slice(g * HEAD_DIM, (g + 1) * HEAD_DIM)
            vs = slice(KVD + g * HEAD_DIM, KVD + (g + 1) * HEAD_DIM)
            k_band = jnp.concatenate([kvp_ref[:, ks], kvc_ref[:, ks]], axis=0)
            v_band = jnp.concatenate([kvp_ref[:, vs], kvc_ref[:, vs]], axis=0)
            q_g = _group_rows(q_ref, g, qpk)
            do_g = _group_rows(do_ref, g, qpk)
            lse_g = _group_col(lse_ref, g, qpk)
            s = jnp.where(mask, _dot(q_g, k_band, ((1,), (1,))) * scale, MASK_VALUE)
            probs = jnp.exp(s - lse_g)
            dp = _dot(do_g, v_band, ((1,), (1,)))
            delta = jnp.sum(probs * dp, axis=-1, keepdims=True)
            ds = (probs * (dp - delta) * scale).astype(BF16)
            dq_g = _dot(ds, k_band, ((1,), (0,)))
            dks.append(_dot(ds, q_g, ((0,), (0,))))
            dvs.append(_dot(probs.astype(BF16), do_g, ((0,), (0,))))
            for i in range(qpk):
                h = g * qpk + i
                dqs[h] = dq_g[i * B:(i + 1) * B, :]
                p_sink = jnp.exp(sink_ref[:, h:h + 1] - lse_g[i * B:(i + 1) * B, :])
                dsk.append(-jnp.sum(p_sink * delta[i * B:(i + 1) * B, :], axis=0, keepdims=True))
        dq = jnp.concatenate(dqs, axis=1)
        dq_ref[...] = _rope_wide(dq, c_ref[...], su_ref[...], sd_ref[...], _rope_chunk_bwd).astype(dq_ref.dtype)
        dkv = jnp.concatenate(dks + dvs, axis=1)
        prev = pl.ds(pl.multiple_of(jnp.maximum(n - 1, 0) * B, B), B)
        cur = pl.ds(pl.multiple_of(n * B, B), B)
        dkv_ref[prev, :] += dkv[0:B, :]
        dkv_ref[cur, :] += dkv[B:, :]
        dsink_ref[...] += jnp.concatenate(dsk, axis=1)

    tab = pl.BlockSpec((B, LANES), lambda n: (n, 0))
    return _call(
        body, comm, name="attn_bwd", grid=(nb,),
        in_specs=_attn_specs(HD, KVD) + [pl.BlockSpec((B, HD), lambda n: (n, 0)),
                                         pl.BlockSpec((B, n_heads), lambda n: (n, 0)),
                                         pl.BlockSpec((1, n_heads), lambda n: (0, 0)), tab, tab, tab],
        out_specs=[pl.BlockSpec((B, HD), lambda n: (n, 0)), pl.BlockSpec((T, 2 * KVD), lambda n: (0, 0)),
                   pl.BlockSpec((1, n_heads), lambda n: (0, 0))],
        out_shape=[_sds((T, HD + 2 * KVD), BF16), _sds((T, 2 * KVD), F32), _sds((1, n_heads), F32)],
        sem=("arbitrary",), operands=[qkv, qkv, qkv, do, lse, sinks, *rope])


def dkv_finish(d_qkv, dkv, rope, HD, KVD, *, tm):
    T = dkv.shape[0]
    kv_col = HD // (2 * KVD)

    def body(alias_ref, dkv_ref, c_ref, su_ref, sd_ref, o_ref):
        del alias_ref
        dk = _rope_wide(dkv_ref[:, 0:KVD], c_ref[...], su_ref[...], sd_ref[...], _rope_chunk_bwd)
        o_ref[:, 0:KVD] = dk.astype(o_ref.dtype)
        o_ref[:, KVD:] = dkv_ref[:, KVD:].astype(o_ref.dtype)

    tab = pl.BlockSpec((tm, LANES), lambda i: (i, 0))
    return pl.pallas_call(
        body, name="dkv_finish", grid=(T // tm,),
        in_specs=[pl.BlockSpec(memory_space=pl.ANY), pl.BlockSpec((tm, 2 * KVD), lambda i: (i, 0)), tab, tab, tab],
        out_specs=pl.BlockSpec((tm, 2 * KVD), lambda i: (i, kv_col)),
        out_shape=_sds(d_qkv.shape, d_qkv.dtype), input_output_aliases={0: 0},
        compiler_params=_params(("parallel",)))(d_qkv, dkv, *rope)


def ple_bwd_elem(d_out, pp, gg, *, tm):
    T, D = d_out.shape

    def body(d_ref, pp_ref, gg_ref, dpp_ref, dgg_ref):
        d = d_ref[...]
        sg = _sigmoid(gg_ref[...].astype(F32))
        dpp_ref[...] = (d * sg).astype(dpp_ref.dtype)
        dgg_ref[...] = (d * pp_ref[...].astype(F32) * sg * (1.0 - sg)).astype(dgg_ref.dtype)

    tile = pl.BlockSpec((tm, D), lambda i: (i, 0))
    return pl.pallas_call(
        body, name="ple_bwd_elem", grid=(T // tm,), in_specs=[tile, tile, tile], out_specs=[tile, tile],
        out_shape=[_sds((T, D), BF16), _sds((T, D), BF16)], compiler_params=_params(("parallel",)))(d_out, pp, gg)


def chip_sum(name, g, p_sib, *, ta):
    _, a, b = g.shape
    assert a % ta == 0

    def body(core_ref, g_ref, p_ref, o_ref):
        del core_ref
        o_ref[...] = (g_ref[...].astype(F32) + p_ref[...].astype(F32)).astype(o_ref.dtype)

    blk = pl.BlockSpec((None, ta, b), lambda q, i, core: (q, i, 0))
    my_core = lax.axis_index("c").astype(jnp.int32).reshape(1)
    return pl.pallas_call(
        body, name=name, out_shape=_sds((N_CHIPS, a, b), g.dtype),
        grid_spec=pltpu.PrefetchScalarGridSpec(
            num_scalar_prefetch=1, grid=(N_CHIPS, a // ta),
            in_specs=[pl.BlockSpec((None, None, ta, b), lambda q, i, core: (q, core[0], i, 0)), blk],
            out_specs=blk),
        compiler_params=_params(("arbitrary", "arbitrary")))(my_core, g.reshape(N_CHIPS, 2, a, b), p_sib)


def adamw(name, recvs, w, m, v, *, ta):
    L, a, b = w.shape
    n_terms = recvs[0].shape[0]
    assert a % ta == 0 and len(recvs) == L
    c1 = 1.0 - ADAM_B1 ** ADAM_STEP
    c2 = 1.0 - ADAM_B2 ** ADAM_STEP

    def body(*refs):
        r_refs = refs[:L]
        w_ref, m_ref, v_ref, g_ref, d_ref, nm_ref, nv_ref = refs[L:]
        layer = pl.program_id(0)
        for l in range(L):
            @pl.when(layer == l)
            def _(r_ref=r_refs[l]):
                g = r_ref[0].astype(F32)
                for s in range(1, n_terms):
                    g = g + r_ref[s].astype(F32)
                nm = ADAM_B1 * m_ref[...] + (1.0 - ADAM_B1) * g
                nv = ADAM_B2 * v_ref[...] + (1.0 - ADAM_B2) * jnp.square(g)
                m_hat = nm / c1
                v_hat = nv / c2
                g_ref[...] = g
                d_ref[...] = -ADAM_LR * (m_hat / (jnp.sqrt(v_hat) + ADAM_EPS) + ADAM_WD * w_ref[...])
                nm_ref[...] = nm
                nv_ref[...] = nv

    blk = pl.BlockSpec((None, ta, b), lambda l, i: (l, i, 0))
    out = _sds((L, a, b), F32)
    r_specs = [pl.BlockSpec((n_terms, ta, b), lambda l, i, ll=ll: (0, jnp.where(l == ll, i, 0), 0)) for ll in range(L)]
    return pl.pallas_call(
        body, name=name, grid=(L, a // ta), in_specs=r_specs + [blk, blk, blk],
        out_specs=[blk, blk, blk, blk], out_shape=[out, out, out, out],
        compiler_params=_params(("arbitrary", "arbitrary")))(*recvs, w, m, v)


def _pack_rows(parts):
    out = []
    for arr, rows in parts:
        arr = arr.reshape(-1, LANES).astype(F32)
        out.append(jnp.pad(arr, ((0, rows - arr.shape[0]), (0, 0))))
    return jnp.concatenate(out, axis=0)


def _small_rows(a, prefix):
    return _pack_rows([(a[prefix + name], rows) for name, rows in SMALL_SHARDED])


def _unpack_small(packed, a):
    out, r0 = {}, 0
    for name, rows in SMALL_SHARDED:
        shape = a[name].shape
        used = math.prod(shape) // LANES
        out[name] = packed[r0:r0 + used].reshape(shape)
        r0 += rows
    return out


def _rep_rows(a, prefix):
    parts = []
    for name in REPLICATED:
        arr = a[prefix + name]
        if arr.size % LANES:
            arr = jnp.pad(arr.reshape(1, -1), ((0, 0), (0, LANES - arr.size % LANES)))
        rows = -(-arr.size // LANES)
        parts.append((arr, -(-rows // SUBLANES) * SUBLANES))
    return _pack_rows(parts)


def _unpack_rep(packed, a):
    out, r0 = {}, 0
    for name in REPLICATED:
        shape = a[name].shape
        size = math.prod(shape)
        rows = -(-size // LANES)
        out[name] = packed[r0:r0 + rows].reshape(-1)[:size].reshape(shape)
        r0 += -(-rows // SUBLANES) * SUBLANES
    return out


def _step(a):
    x = a['x'][0]
    T, D = x.shape
    tgt = a['loss_target'][0]
    p_in = [a['p'][i, 0] for i in range(DEPTH)]
    PLE = p_in[0].shape[1]
    n_heads = a['attn_sinks'].shape[1]
    HD = n_heads * HEAD_DIM
    KVD = a['kv_w_k'].shape[1]
    n_kv = KVD // HEAD_DIM
    F = a['mlp_w_down'].shape[1] * N_DEV
    tm = min(TOKEN_TILE, T)
    tm2 = min(2 * TOKEN_TILE, T)
    tmc = min(256, T)
    alpha = DEEPNORM_ALPHA
    xb = x.astype(BF16)
    p_b = [p.astype(BF16) for p in p_in]

    def shard3(w):
        return w.reshape((1,) + w.shape) if w.ndim == 2 else w

    def gather(*specs):
        return _GatherJob([shard3(a[nm])[l].astype(BF16) for nm, l in specs])

    (W_in, small_full), = comm_only("gather_first",
                                    [_GatherJob([a['conv_w_in'][0].astype(BF16), _small_rows(a, '')])])
    r0, small = 0, {}
    for name, rows in SMALL_SHARDED:
        small[name] = small_full[:, r0:r0 + rows]
        r0 += rows
    b_in = small['conv_b_in'][:, 0:2 * D // N_DEV // LANES].reshape(1, 2 * D)
    w_dw = jnp.transpose(small['conv_w_dw'], (1, 0, 2)).reshape(CONV_HALO, D)
    b_dw, cln_g, cln_b, b_out = (small[nm][:, 0].reshape(1, D) for nm in
                                 ('conv_b_dw', 'conv_ln_g', 'conv_ln_b', 'conv_b_out'))
    W_up, W_down, W_proj, W_gate = {}, {}, {}, {}
    mix_g, mix_b, mlp_g, mlp_b = a['mix_ln_g'], a['mix_ln_b'], a['mlp_ln_g'], a['mlp_ln_b']
    rope = _rope_tables(T)

    def set_ple_weights(li, g_proj, g_gate):
        W_proj[li] = jnp.transpose(g_proj, (1, 0, 2)).reshape(1, PLE, D)
        W_gate[li] = g_gate.reshape(1, D, D)

    def row(v, i):
        return v[i:i + 1]

    def res_ln_epi(coef):
        def epi(accs, ex, out, i):
            acc = accs[0] if isinstance(accs, list) else accs
            n_ex = len(ex)
            res_ref, g_ref, b_ref = ex[n_ex - 3], ex[n_ex - 2], ex[n_ex - 1]
            z = coef * res_ref[...] + acc
            if n_ex == 4:
                z = z + ex[0][...]
            out[0][...] = z
            xo = _ln_fwd(z, g_ref[...], b_ref[...])
            out[1][...] = xo
            out[2][...] = xo.astype(BF16)
        return epi

    res_ln_outs = [(_sds((T, D), F32), 'tile'), (_sds((T, D), F32), 'tile'), (_sds((T, D), BF16), 'tile')]

    def mlp_fwd(li, xin, xin_b, up_comm=None, down_comm=None):
        def up_epi(accs, ex, out, i):
            u = accs[0]
            out[0][...] = u.astype(BF16)
            out[1][...] = jnp.square(jnp.maximum(u, 0.0)).astype(BF16)
        res = mm_nn(f"mlp_up_{li}", [(xin_b, W_up[li], 0)], [], [(_sds((T, F), BF16), 'tile')] * 2, up_epi,
                    tm=tm2, tn=min(1024, F), comm=up_comm)
        (u, act), got_up = res if up_comm is not None else (res, ())
        if li not in W_down:
            W_down[li] = got_up[0].reshape(1, F, D)
        res = mm_nn(f"mlp_down_{li}", [(act, W_down[li], 0)],
                    [(xin, 'tile'), (row(mlp_g, li), 'row'), (row(mlp_b, li), 'row')],
                    res_ln_outs, res_ln_epi(alpha), tm=tm, tn=D, tk=min(1024, F), comm=down_comm)
        (z, xo, xo_b), got_down = res if down_comm is not None else (res, ())
        return u, act, z, xo, xo_b, got_up, got_down

    def ple_fwd(li, xin, xin_b, with_loss):
        def epi(accs, ex, out, i):
            pp, gg = accs
            xo = ex[0][...] + pp * _sigmoid(gg)
            out[1][...] = pp.astype(BF16)
            out[2][...] = gg.astype(BF16)
            if with_loss:
                err = xo - ex[1][...]
                out[0][...] = err * (1.0 / D)
                _init_or_add(out[3], i, jnp.sum(err * err, axis=0, keepdims=True) * (0.5 / D))
            else:
                out[0][...] = xo
                out[3][...] = xo.astype(BF16)
        extras = [(xin, 'tile')] + ([(tgt, 'tile')] if with_loss else [])
        outs = [(_sds((T, D), F32), 'tile'), (_sds((T, D), BF16), 'tile'), (_sds((T, D), BF16), 'tile')]
        outs.append((_sds((1, D), F32), 'rowacc') if with_loss else (_sds((T, D), BF16), 'tile'))
        return mm_nn(f"ple_{li}", [(p_b[li], W_proj[li], 0), (xin_b, W_gate[li], 0)], extras, outs, epi, tm=tm, tn=D)

    assert D // N_DEV == LANES
    (g0, ha0, hg0), (g_wout,) = _glu(xb, W_in, b_in, T, D, tm, gather(('conv_w_out', 0)))
    W_out = g_wout.reshape(1, D, D)
    (c0, s0), (W_up[0],) = dwconv_fwd(g0, w_dw, b_dw, cln_g, cln_b, tm=tmc, comm=gather(('mlp_w_up', 0)))
    (z1, x1, x1b), got = mm_nn("conv_out", [(s0, W_out, 0)],
                               [(b_out, 'row'), (x, 'tile'), (row(mix_g, 0), 'row'), (row(mix_b, 0), 'row')],
                               res_ln_outs, res_ln_epi(alpha), tm=tm, tn=D,
                               comm=gather(('ple_w_proj', 0), ('ple_w_gate', 0)))
    set_ple_weights(0, *got)
    u0, act0, z2, x2, x2b, _, got = mlp_fwd(
        0, x1, x1b, up_comm=gather(('mlp_w_down', 0)),
        down_comm=gather(('attn_w_q', 0), ('kv_w_k', 0), ('kv_w_v', 0), ('attn_w_o', 0)))
    W_qkv = jnp.concatenate([got[0].reshape(D, HD), got[1].reshape(D, KVD), got[2].reshape(D, KVD)], axis=1)[None]
    W_o = got[3].reshape(1, HD, D)
    x3, pp0, gg0, x3b = ple_fwd(0, x2, x2b, False)

    def qkv_epi(accs, ex, out, i):
        t = accs[0]
        c, su, sd = ex[0][...], ex[1][...], ex[2][...]
        out[0][:, 0:HD + KVD] = _rope_wide(t[:, 0:HD + KVD], c, su, sd, _rope_chunk).astype(BF16)
        out[0][:, HD + KVD:] = t[:, HD + KVD:].astype(BF16)
    NQ = HD + 2 * KVD
    qkv, = mm_nn("qkv_rope", [(x3b, W_qkv, 0)], [(t, 'tab') for t in rope], [(_sds((T, NQ), BF16), 'tile')], qkv_epi,
                 tm=tm, tn=NQ)
    (o1, lse1), (W_up[1], g_down1) = attn_fwd(qkv, a['attn_sinks'], n_heads, n_kv,
                                              comm=gather(('mlp_w_up', 1), ('mlp_w_down', 1)))
    W_down[1] = g_down1.reshape(1, F, D)
    (z3, x4, x4b), got = mm_nn("attn_out", [(o1, W_o, 0)],
                               [(x3, 'tile'), (row(mix_g, 1), 'row'), (row(mix_b, 1), 'row')],
                               res_ln_outs, res_ln_epi(alpha), tm=tm, tn=D,
                               comm=gather(('ple_w_proj', 1), ('ple_w_gate', 1)))
    set_ple_weights(1, *got)
    u1, act1, z4, x5, x5b, _, _ = mlp_fwd(1, x4, x4b)
    dy, pp1, gg1, loss_row = ple_fwd(1, x5, x5b, True)
    loss_local = jnp.sum(loss_row)

    grads = {}

    def ln_bwd_epi(coef, with_colsum):
        def epi(acc, ex, out, i):
            d_x = acc + coef * ex[0][...]
            dz, dg, db = _ln_bwd(d_x, ex[1][...], ex[2][...])
            out[0][...] = dz
            out[1][...] = dz.astype(BF16)
            _init_or_add(out[2], i, dg)
            _init_or_add(out[3], i, db)
            if with_colsum:
                _init_or_add(out[4], i, jnp.sum(dz, axis=0, keepdims=True))
        return epi

    def ln_bwd_outs(with_colsum):
        outs = [(_sds((T, D), F32), 'tile'), (_sds((T, D), BF16), 'tile'), (_sds((1, D), F32), 'rowacc'),
                (_sds((1, D), F32), 'rowacc')]
        return outs + ([(_sds((1, D), F32), 'rowacc')] if with_colsum else [])

    def ple_bwd(li, d_out, xin, pp, gg, z_mlp, pair_specs=None):
        d_pp, d_gg = ple_bwd_elem(d_out, pp, gg, tm=tm)
        grads[('ple_w_proj', li)] = mm_tn(f"d_ple_proj_{li}", p_b[li], d_pp, D, BF16, tm=tm, tk=PLE, tn=D)
        grads[('ple_w_gate', li)] = mm_tn(f"d_ple_gate_{li}", xin, d_gg, D, BF16, tm=tm, tk=D, tn=D)
        job = pair_stage(*pair_specs) if pair_specs else None
        res = mm_nt(f"ple_dx_{li}", d_gg, W_gate[li], [(d_out, 'tile'), (z_mlp, 'tile'), (row(mlp_g, li), 'row')],
                    ln_bwd_outs(False), ln_bwd_epi(1.0, False), tm=tm, tko=D, tc=D, comm=job)
        (dz, dzb, dg, db), got = res if job is not None else (res, None)
        grads[('mlp_ln_g', li)], grads[('mlp_ln_b', li)] = dg, db
        return dz, dzb, (chip_stage(job, got) if job is not None else None)

    recv = {}
    wqkv_cols = {'attn_w_q': (0, HD), 'kv_w_k': (HD, HD + KVD), 'kv_w_v': (HD + KVD, NQ)}

    def piece(name, li):
        if name == 'conv_w_in':
            return grads['conv_w_in']
        if name == 'mlp_w_up':
            return grads[('mlp_w_up', li)]
        if name == 'ple_w_proj':
            return jnp.transpose(grads[('ple_w_proj', li)][0].reshape(PLE, N_DEV, D // N_DEV), (1, 0, 2))
        if name in wqkv_cols:
            g = grads['w_qkv'][:, wqkv_cols[name][0]:wqkv_cols[name][1]]
        else:
            g = grads[name] if name in grads else grads[(name, li)]
            g = g[0]
        return g.reshape(N_DEV, g.shape[0] // N_DEV, g.shape[1])

    def pair_stage(*specs):
        job = _PairJob([piece(nm, li) for nm, li in specs])
        job.specs = specs
        return job

    def chip_stage(pair_job, got):
        sums = [chip_sum(f"chip_sum_{nm}_{li}", mine, theirs, ta=min(512, mine.shape[1]))
                for (nm, li), mine, theirs in zip(pair_job.specs, pair_job.sources, got)]
        job = _ChipJob(sums)
        job.specs = pair_job.specs
        return job

    def keep(job, got):
        for spec, r in zip(job.specs, got):
            recv[spec] = r

    def mlp_bwd(li, dz, dzb, xin, u, act, z_mix, with_colsum, pair_specs, chip_job, down_job=None):
        res = mm_tn(f"d_mlp_down_{li}", act, dzb, D, BF16, tm=tm, tk=min(1024, F), tn=D, comm=down_job)
        if down_job is not None:
            res, got = res
            keep(down_job, got)
        grads[('mlp_w_down', li)] = res

        def du_epi(acc, ex, out, i):
            out[0][...] = (acc * (2.0 * jnp.maximum(ex[0][...].astype(F32), 0.0))).astype(BF16)
        pair = pair_stage(*pair_specs)
        (du,), got = mm_nt(f"mlp_du_{li}", dzb, W_down[li], [(u, 'tile')], [(_sds((T, F), BF16), 'tile')], du_epi,
                           tm=tm2, tko=min(1024, F), tc=D, comm=pair)
        pending = chip_stage(pair, got)
        grads[('mlp_w_up', li)] = mm_tn(f"d_mlp_up_{li}", xin, du, F // N_DEV, BF16, tm=tm, tk=D, tn=min(1024, F))
        if chip_job is None:
            chip_job, pending = pending, None
        res, got = mm_nt(f"mlp_dx_{li}", du, W_up[li], [(dz, 'tile'), (z_mix, 'tile'), (row(mix_g, li), 'row')],
                         ln_bwd_outs(with_colsum), ln_bwd_epi(alpha, with_colsum), tm=tm, tko=D, tc=min(1024, F),
                         comm=chip_job)
        keep(chip_job, got)
        grads[('mix_ln_g', li)], grads[('mix_ln_b', li)] = res[2], res[3]
        return res, pending

    dz4, dz4b, _ = ple_bwd(1, dy, x5b, pp1, gg1, z4)
    (dz3, dz3b, _, _), _ = mlp_bwd(1, dz4, dz4b, x4b, u1, act1, z3, False,
                                   [('mlp_w_down', 1), ('ple_w_gate', 1), ('ple_w_proj', 1)], None)
    grads['attn_w_o'] = mm_tn("d_attn_wo", o1, dz3b, D, BF16, tm=tm, tk=HD, tn=D)

    def do_epi(acc, ex, out, i):
        out[0][...] = acc.astype(BF16)
    pair = pair_stage(('mlp_w_up', 1), ('attn_w_o', 0))
    (do1,), got = mm_nt("attn_do", dz3b, W_o, [], [(_sds((T, HD), BF16), 'tile')], do_epi, tm=tm, tko=HD, tc=D,
                        comm=pair)
    job = chip_stage(pair, got)
    (d_qkv, dkv, d_sinks), got = attn_bwd(qkv, do1, lse1, a['attn_sinks'], rope, n_heads, n_kv, comm=job)
    keep(job, got)
    d_qkv = dkv_finish(d_qkv, dkv, rope, HD, KVD, tm=tm)
    grads['w_qkv'] = mm_tn("d_wqkv", x3b, d_qkv, NQ, BF16, tm=tm, tk=D, tn=NQ)[0]

    def dx3_epi(acc, ex, out, i):
        out[0][...] = acc + alpha * ex[0][...]
    dx3, = mm_nt("attn_dx", d_qkv, W_qkv, [(dz3, 'tile')], [(_sds((T, D), F32), 'tile')], dx3_epi,
                 tm=tm, tko=D, tc=NQ)

    dz2, dz2b, job = ple_bwd(0, dx3, x2b, pp0, gg0, z2,
                             [('attn_w_q', 0), ('kv_w_k', 0), ('kv_w_v', 0), ('ple_w_gate', 0), ('ple_w_proj', 0)])
    (dz1, dz1b, _, _, db_out), _ = mlp_bwd(0, dz2, dz2b, x1b, u0, act0, z1, True, [('mlp_w_down', 0)], None, job)
    grads['conv_w_out'] = mm_tn("d_conv_wout", s0, dz1b, D, BF16, tm=tm, tk=D, tn=D)

    def ds_epi(acc, ex, out, i):
        n = _ln_fwd(ex[0][...], ex[1][...], ex[2][...])
        sg = _sigmoid(n)
        dn = acc * (sg * (1.0 + n * (1.0 - sg)))
        dc, dg, db = _ln_bwd(dn, ex[0][...], ex[1][...])
        out[0][...] = dc
        _init_or_add(out[1], i, dg)
        _init_or_add(out[2], i, db)
    pair = pair_stage(('mlp_w_up', 0), ('conv_w_out', 0))
    (dc0, d_cln_g, d_cln_b), got = mm_nt("conv_ds", dz1b, W_out, [(c0, 'tile'), (cln_g, 'row'), (cln_b, 'row')],
                                         [(_sds((T, D), F32), 'tile'), (_sds((1, D), F32), 'rowacc'),
                                          (_sds((1, D), F32), 'rowacc')], ds_epi, tm=tm, tko=D, tc=D, comm=pair)
    job = chip_stage(pair, got)
    (dh0, d_wdw, d_bdw, d_bin), got = dwconv_bwd(dc0, g0, ha0, hg0, w_dw, tm=tmc, comm=job)
    keep(job, got)
    grads['conv_w_in'] = mm_tn("d_conv_win", xb, dh0, 2 * D // N_DEV, BF16, tm=tm, tk=D, tn=D)

    def dx_epi(acc, ex, out, i):
        out[0][...] = acc + alpha * ex[0][...]
    pair = pair_stage(('conv_w_in', 0))
    (grad_x,), got = mm_nt("conv_dx", dh0, W_in, [(dz1, 'tile')], [(_sds((T, D), F32), 'tile')], dx_epi,
                           tm=tm, tko=D, tc=D, comm=pair)
    last_chip = chip_stage(pair, got)

    def own_rows(vec, rows_used, rows):
        arr = vec.reshape(N_DEV, rows_used, LANES)
        return jnp.pad(arr, ((0, 0), (0, rows - rows_used), (0, 0)))
    dwdw_dev = jnp.transpose(d_wdw.reshape(CONV_HALO, N_DEV, D // N_DEV), (1, 0, 2))
    lane_rows = D // N_DEV // LANES
    small_grad = jnp.concatenate([
        own_rows(d_bin, 2 * lane_rows, 8), dwdw_dev if lane_rows == 1 else dwdw_dev.reshape(N_DEV, -1, LANES),
        own_rows(d_bdw, lane_rows, 8), own_rows(d_cln_g, lane_rows, 8), own_rows(d_cln_b, lane_rows, 8),
        own_rows(db_out, lane_rows, 8)], axis=1)
    n_small = small_grad.shape[1]

    rep_local = {'mix_ln_g': jnp.concatenate([grads[('mix_ln_g', li)] for li in range(DEPTH)], axis=0),
                 'mix_ln_b': jnp.concatenate([grads[('mix_ln_b', li)] for li in range(DEPTH)], axis=0),
                 'mlp_ln_g': jnp.concatenate([grads[('mlp_ln_g', li)] for li in range(DEPTH)], axis=0),
                 'mlp_ln_b': jnp.concatenate([grads[('mlp_ln_b', li)] for li in range(DEPTH)], axis=0),
                 'attn_sinks': d_sinks}
    rep_grad = _rep_rows(rep_local, '')
    n_rep = rep_grad.shape[0]
    got, (recv_small, recv_rep) = comm_only("exchange_last", [last_chip, _DirectJob(
        [small_grad, jnp.broadcast_to(rep_grad[None], (N_DEV, n_rep, LANES))])])
    keep(last_chip, got)

    result = {}
    kinds = ('grad', 'delta', 'new_m', 'new_v')
    w, m, v = (_small_rows(a, pre)[None] for pre in ('', 'm_', 'v_'))
    for kind, arr in zip(kinds, adamw("adamw_small", [recv_small], w, m, v, ta=n_small)):
        for pname, val in _unpack_small(arr[0], a).items():
            result[(kind, pname)] = val
    w, m, v = (_rep_rows(a, pre)[None] for pre in ('', 'm_', 'v_'))
    for kind, arr in zip(kinds, adamw("adamw_rep", [recv_rep], w, m, v, ta=n_rep)):
        for pname, val in _unpack_rep(arr[0], a).items():
            result[(kind, pname)] = val
    for name in BIG_WEIGHTS:
        w, m, v = (shard3(a[pre + name]) for pre in ('', 'm_', 'v_'))
        recvs = [recv[(name, li)] for li in range(w.shape[0])]
        for kind, arr in zip(kinds, adamw("adamw_" + name, recvs, w, m, v, ta=min(256, w.shape[1]))):
            result[(kind, name)] = arr.reshape(a[name].shape)

    loss = lax.psum(loss_local, ("x", "y", "c"))
    out = [loss, grad_x[None]]
    for kind in ('grad', 'delta', 'new_m', 'new_v'):
        out += [result[(kind, name)] for name in WEIGHT_NAMES]
    return tuple(out)


def _glu(x, W_in, b_in, T, D, tm, comm=None):
    n = W_in.shape[2]
    q = 2 if D // n % 2 == 0 else 1
    nt = D // (q * n)
    tn = q * n

    def body(x_ref, wa_ref, wg_ref, ba_ref, bg_ref, g_ref, ha_ref, hg_ref):
        xb = x_ref[...]
        ha = jnp.concatenate([_dot(xb, wa_ref[s], ((1,), (0,))) for s in range(q)], axis=1) + ba_ref[...]
        hg = jnp.concatenate([_dot(xb, wg_ref[s], ((1,), (0,))) for s in range(q)], axis=1) + bg_ref[...]
        g_ref[...] = ha * _sigmoid(hg)
        ha_ref[...] = ha.astype(ha_ref.dtype)
        hg_ref[...] = hg.astype(hg_ref.dtype)

    return _call(
        body, comm, name="conv_in_glu", grid=(T // tm, nt),
        in_specs=[pl.BlockSpec((tm, D), lambda i, j: (i, 0)),
                  pl.BlockSpec((q, D, n), lambda i, j: (j, 0, 0)),
                  pl.BlockSpec((q, D, n), lambda i, j: (j + nt, 0, 0)),
                  pl.BlockSpec((1, tn), lambda i, j: (0, j)), pl.BlockSpec((1, tn), lambda i, j: (0, j + nt))],
        out_specs=[pl.BlockSpec((tm, tn), lambda i, j: (i, j))] * 3,
        out_shape=[_sds((T, D), F32), _sds((T, D), BF16), _sds((T, D), BF16)],
        sem=("parallel", "parallel"), operands=[x, W_in, W_in, b_in, b_in])


def kernel(x, p, conv_w_in, conv_b_in, conv_w_dw, conv_b_dw, conv_ln_g, conv_ln_b, conv_w_out, conv_b_out, kv_w_k, kv_w_v, attn_w_q, attn_sinks, attn_w_o, mix_ln_g, mix_ln_b, mlp_w_up, mlp_w_down, mlp_ln_g, mlp_ln_b, ple_w_proj, ple_w_gate, loss_target, m_conv_w_in, m_conv_b_in, m_conv_w_dw, m_conv_b_dw, m_conv_ln_g, m_conv_ln_b, m_conv_w_out, m_conv_b_out, m_kv_w_k, m_kv_w_v, m_attn_w_q, m_attn_sinks, m_attn_w_o, m_mix_ln_g, m_mix_ln_b, m_mlp_w_up, m_mlp_w_down, m_mlp_ln_g, m_mlp_ln_b, m_ple_w_proj, m_ple_w_gate, v_conv_w_in, v_conv_b_in, v_conv_w_dw, v_conv_b_dw, v_conv_ln_g, v_conv_ln_b, v_conv_w_out, v_conv_b_out, v_kv_w_k, v_kv_w_v, v_attn_w_q, v_attn_sinks, v_attn_w_o, v_mix_ln_g, v_mix_ln_b, v_mlp_w_up, v_mlp_w_down, v_mlp_ln_g, v_mlp_ln_b, v_ple_w_proj, v_ple_w_gate):
    return _step(dict(locals()))
```

```python
import functools
import math

import jax
import jax.numpy as jnp
from jax import lax
from jax.experimental import pallas as pl
from jax.experimental.pallas import tpu as pltpu

F32 = jnp.float32
BF16 = jnp.bfloat16

N_DEV = 8
HEAD_DIM = 64
ROPE_DIM = HEAD_DIM // 4
ROPE_HALF = ROPE_DIM // 2
ROPE_THETA = 500000.0
ATT_BLOCK = 128
CONV_WIDTH = 31
CONV_HALO = 32
CONV_ROWS = 128
LN_EPS = 1e-5
DEPTH = 2
DEEPNORM_ALPHA = (2 * DEPTH) ** 0.25
MASK_VALUE = -1e30

ADAM_LR = 0.001
ADAM_B1 = 0.9
ADAM_B2 = 0.999
ADAM_EPS = 1e-08
ADAM_WD = 0.01
ADAM_STEP = 10

LANES = 128
SUBLANES = 8
VMEM_LIMIT_BYTES = 52 * 1024 * 1024
TOKEN_TILE = 512
MESH_ID = pl.DeviceIdType.MESH
FORWARD_AT_TENTHS = 6

WEIGHT_NAMES = ['conv_w_in', 'conv_b_in', 'conv_w_dw', 'conv_b_dw', 'conv_ln_g', 'conv_ln_b', 'conv_w_out',
                'conv_b_out', 'kv_w_k', 'kv_w_v', 'attn_w_q', 'attn_sinks', 'attn_w_o', 'mix_ln_g', 'mix_ln_b',
                'mlp_w_up', 'mlp_w_down', 'mlp_ln_g', 'mlp_ln_b', 'ple_w_proj', 'ple_w_gate']
BIG_WEIGHTS = ['conv_w_in', 'conv_w_out', 'kv_w_k', 'kv_w_v', 'attn_w_q', 'attn_w_o', 'mlp_w_up', 'mlp_w_down',
               'ple_w_proj', 'ple_w_gate']
SMALL_SHARDED = [('conv_b_in', 8), ('conv_w_dw', 32), ('conv_b_dw', 8), ('conv_ln_g', 8), ('conv_ln_b', 8),
                 ('conv_b_out', 8)]
REPLICATED = ['mix_ln_g', 'mix_ln_b', 'mlp_ln_g', 'mlp_ln_b', 'attn_sinks']


def _params(sem):
    return pltpu.CompilerParams(dimension_semantics=sem, vmem_limit_bytes=VMEM_LIMIT_BYTES)


def _sds(shape, dtype):
    return jax.ShapeDtypeStruct(shape, dtype)


def _my_place():
    x, y, c = lax.axis_index("x"), lax.axis_index("y"), lax.axis_index("c")
    return x, y, c, 4 * x + 2 * y + c


def _peers(x, y, c):
    out = []
    for dx in (0, 1):
        for dy in (0, 1):
            for dc in (0, 1):
                if dx or dy or dc:
                    px, py, pc = x ^ dx, y ^ dy, c ^ dc
                    out.append(((px, py, pc), 4 * px + 2 * py + pc))
    return out


N_CHIPS = 4


def _other_chips(x, y):
    return [((x ^ dx, y ^ dy), 2 * (x ^ dx) + (y ^ dy)) for dx, dy in ((1, 0), (0, 1), (1, 1))]


def _remote(src, dst, send, recv, to):
    return pltpu.make_async_remote_copy(src_ref=src, dst_ref=dst, send_sem=send, recv_sem=recv, device_id=to,
                                        device_id_type=MESH_ID)


def _wait_slabs(buf, count, send, recv, me, sent=True, received=True):
    part = buf.at[pl.ds(0, count)]
    cp = _remote(part, part, send, recv, me)
    if sent:
        cp.wait_send()
    if received:
        cp.wait_recv()


class _DirectJob:
    n_sems = 3

    def __init__(self, items):
        self.sources = list(items)
        self.dests = [_sds(it.shape, it.dtype) for it in items]
        self.n = len(items)

    def start(self, src, dst, send, recv, loc):
        x, y, c, me = _my_place()
        for k in range(self.n):
            here = dst[k].at[pl.ds(me, 1)]
            pltpu.make_async_copy(src[k].at[pl.ds(me, 1)], here, loc.at[k]).start()
            for peer, idx in _peers(x, y, c):
                _remote(src[k].at[pl.ds(idx, 1)], here, send.at[k], recv.at[k], peer).start()

    def forward(self, src, dst, send, recv, loc):
        pass

    def finish(self, src, dst, send, recv, loc):
        x, y, c, me = _my_place()
        for k in range(self.n):
            pltpu.make_async_copy(src[k].at[pl.ds(me, 1)], dst[k].at[pl.ds(me, 1)], loc.at[k]).wait()
            _wait_slabs(dst[k], N_DEV - 1, send.at[k], recv.at[k], (x, y, c))


class _GatherJob:
    n_sems = 5

    def __init__(self, items):
        self.sources = [it.reshape((1,) + it.shape) for it in items]
        self.dests = [_sds((N_DEV,) + it.shape, it.dtype) for it in items]
        self.n = len(items)

    def start(self, src, dst, send_ici, recv_ici, send_d2d, recv_d2d, loc):
        x, y, c, me = _my_place()
        for k in range(self.n):
            here = dst[k].at[pl.ds(me, 1)]
            pltpu.make_async_copy(src[k], here, loc.at[k]).start()
            _remote(src[k], here, send_d2d.at[k], recv_d2d.at[k], (x, y, 1 - c)).start()
            for (px, py), _ in _other_chips(x, y):
                _remote(src[k], here, send_ici.at[k], recv_ici.at[k], (px, py, c)).start()

    def forward(self, src, dst, send_ici, recv_ici, send_d2d, recv_d2d, loc):
        x, y, c, me = _my_place()
        for k in range(self.n):
            _wait_slabs(dst[k], N_CHIPS - 1, send_ici.at[k], recv_ici.at[k], (x, y, c), sent=False)
            for _, q in _other_chips(x, y):
                rows = dst[k].at[pl.ds(2 * q + c, 1)]
                _remote(rows, rows, send_d2d.at[k], recv_d2d.at[k], (x, y, 1 - c)).start()

    def finish(self, src, dst, send_ici, recv_ici, send_d2d, recv_d2d, loc):
        x, y, c, me = _my_place()
        for k in range(self.n):
            pltpu.make_async_copy(src[k], dst[k].at[pl.ds(me, 1)], loc.at[k]).wait()
            _wait_slabs(dst[k], N_CHIPS - 1, send_ici.at[k], recv_ici.at[k], (x, y, c), received=False)
            _wait_slabs(dst[k], N_CHIPS, send_d2d.at[k], recv_d2d.at[k], (x, y, c))


class _PairJob:
    n_sems = 2

    def __init__(self, items):
        self.sources = list(items)
        self.dests = [_sds((N_CHIPS,) + it.shape[1:], it.dtype) for it in items]
        self.n = len(items)

    def start(self, src, dst, send, recv):
        x, y, c, me = _my_place()
        for k in range(self.n):
            for q in range(N_CHIPS):
                _remote(src[k].at[pl.ds(2 * q + 1 - c, 1)], dst[k].at[pl.ds(q, 1)], send.at[k], recv.at[k],
                        (x, y, 1 - c)).start()

    def forward(self, src, dst, send, recv):
        pass

    def finish(self, src, dst, send, recv):
        x, y, c, me = _my_place()
        for k in range(self.n):
            _wait_slabs(dst[k], N_CHIPS, send.at[k], recv.at[k], (x, y, c))


class _ChipJob:
    n_sems = 3

    def __init__(self, items):
        self.sources = list(items)
        self.dests = [_sds(it.shape, it.dtype) for it in items]
        self.n = len(items)

    def start(self, src, dst, send, recv, loc):
        x, y, c, me = _my_place()
        mine = 2 * x + y
        for k in range(self.n):
            here = dst[k].at[pl.ds(mine, 1)]
            pltpu.make_async_copy(src[k].at[pl.ds(mine, 1)], here, loc.at[k]).start()
            for (px, py), q in _other_chips(x, y):
                _remote(src[k].at[pl.ds(q, 1)], here, send.at[k], recv.at[k], (px, py, c)).start()

    def forward(self, src, dst, send, recv, loc):
        pass

    def finish(self, src, dst, send, recv, loc):
        x, y, c, me = _my_place()
        mine = 2 * x + y
        for k in range(self.n):
            pltpu.make_async_copy(src[k].at[pl.ds(mine, 1)], dst[k].at[pl.ds(mine, 1)], loc.at[k]).wait()
            _wait_slabs(dst[k], N_CHIPS - 1, send.at[k], recv.at[k], (x, y, c))


def _job_sems(job):
    return [pltpu.SemaphoreType.DMA((job.n,))] * job.n_sems


def _call(body, comm, *, name, grid, in_specs, out_specs, out_shape, operands, sem, scratch_shapes=(), aliases=None):
    single = not isinstance(out_shape, (list, tuple))
    out_shape = [out_shape] if single else list(out_shape)
    out_specs = [out_specs] if single else list(out_specs)
    if comm is None:
        res = pl.pallas_call(body, name=name, grid=grid, in_specs=list(in_specs), out_specs=out_specs,
                             out_shape=out_shape, scratch_shapes=list(scratch_shapes),
                             input_output_aliases=aliases or {}, compiler_params=_params(sem))(*operands)
        return res[0] if single else res
    n_in, n_out, n_scr, n_c = len(in_specs), len(out_shape), len(scratch_shapes), comm.n
    any_spec = pl.BlockSpec(memory_space=pl.ANY)
    steps = math.prod(grid)
    mid = min(steps - 1, (steps * FORWARD_AT_TENTHS) // 10)

    def hosted(*refs):
        ins, c_src = refs[:n_in], refs[n_in:n_in + n_c]
        outs = refs[n_in + n_c:n_in + n_c + n_out]
        c_dst = refs[n_in + n_c + n_out:n_in + 2 * n_c + n_out]
        scr = refs[n_in + 2 * n_c + n_out:n_in + 2 * n_c + n_out + n_scr]
        sems = refs[n_in + 2 * n_c + n_out + n_scr:]
        step = pl.program_id(0)
        for d in range(1, len(grid)):
            step = step * grid[d] + pl.program_id(d)

        @pl.when(step == 0)
        def _():
            comm.start(c_src, c_dst, *sems)

        @pl.when(step == mid)
        def _():
            comm.forward(c_src, c_dst, *sems)

        body(*ins, *outs, *scr)

        @pl.when(step == steps - 1)
        def _():
            comm.finish(c_src, c_dst, *sems)

    res = pl.pallas_call(hosted, name=name, grid=grid, in_specs=list(in_specs) + [any_spec] * n_c,
                         out_specs=out_specs + [any_spec] * n_c, out_shape=out_shape + comm.dests,
                         scratch_shapes=list(scratch_shapes) + _job_sems(comm), input_output_aliases=aliases or {},
                         compiler_params=_params(("arbitrary",) * len(grid)))(*operands, *comm.sources)
    main = res[:n_out]
    return (main[0] if single else main), res[n_out:]


def comm_only(name, jobs):
    any_spec = pl.BlockSpec(memory_space=pl.ANY)
    n_all = sum(job.n for job in jobs)

    def body(*refs):
        srcs, dsts, sems = refs[:n_all], refs[n_all:2 * n_all], refs[2 * n_all:]
        parts, k0, s0 = [], 0, 0
        for job in jobs:
            parts.append((job, srcs[k0:k0 + job.n], dsts[k0:k0 + job.n], sems[s0:s0 + job.n_sems]))
            k0 += job.n
            s0 += job.n_sems
        for stage in ('start', 'forward', 'finish'):
            for job, src, dst, sem in parts:
                getattr(job, stage)(src, dst, *sem)

    res = pl.pallas_call(body, name=name, in_specs=[any_spec] * n_all, out_specs=[any_spec] * n_all,
                         out_shape=[d for job in jobs for d in job.dests],
                         scratch_shapes=[s for job in jobs for s in _job_sems(job)],
                         )(*[s for job in jobs for s in job.sources])
    out, k0 = [], 0
    for job in jobs:
        out.append(res[k0:k0 + job.n])
        k0 += job.n
    return out


def _dot(a, b, dims):
    if a.dtype != BF16:
        a = a.astype(BF16)
    if b.dtype != BF16:
        b = b.astype(BF16)
    return lax.dot_general(a, b, (dims, ((), ())), preferred_element_type=F32)


def _sigmoid(v):
    return 1.0 / (1.0 + jnp.exp(-v))


def _ln_stats(z):
    mu = jnp.mean(z, axis=-1, keepdims=True)
    zc = z - mu
    var = jnp.mean(zc * zc, axis=-1, keepdims=True)
    return zc * lax.rsqrt(var + LN_EPS)


def _ln_fwd(z, g, b):
    return _ln_stats(z) * g + b


def _ln_bwd(dy, z, g):
    xhat = _ln_stats(z)
    mu = jnp.mean(z, axis=-1, keepdims=True)
    zc = z - mu
    rstd = lax.rsqrt(jnp.mean(zc * zc, axis=-1, keepdims=True) + LN_EPS)
    dxh = dy * g
    m1 = jnp.mean(dxh, axis=-1, keepdims=True)
    m2 = jnp.mean(dxh * xhat, axis=-1, keepdims=True)
    dz = rstd * (dxh - m1 - xhat * m2)
    return dz, jnp.sum(dy * xhat, axis=0, keepdims=True), jnp.sum(dy, axis=0, keepdims=True)


def _extra_spec(shape, kind, tm, tn, ij):
    if kind == 'tile':
        return pl.BlockSpec((tm, tn), lambda *g: (ij(g)[0], ij(g)[1]))
    if kind in ('row', 'rowacc'):
        return pl.BlockSpec((1, tn), lambda *g: (0, ij(g)[1]))
    if kind == 'tab':
        return pl.BlockSpec((tm, LANES), lambda *g: (ij(g)[0], 0))
    if kind == 'rows':
        return pl.BlockSpec((tm, shape[1]), lambda *g: (ij(g)[0], 0))
    raise ValueError(kind)


def mm_nn(name, pairs, extras, outs, epi, *, tm, tn, tk=None, comm=None):
    M = pairs[0][0].shape[0]
    N = pairs[0][1].shape[0] * pairs[0][1].shape[2]
    n_pairs = len(pairs)
    K0 = pairs[0][0].shape[1]
    tk = K0 if tk is None else tk
    nk = K0 // tk
    assert nk == 1 or n_pairs == 1
    assert M % tm == 0 and N % tn == 0 and K0 % tk == 0
    has_rowacc = any(kind == 'rowacc' for _, kind in outs)
    assert not has_rowacc or (N == tn and nk == 1)
    in_specs, operands, slabs = [], [], []
    for a, b, off in pairs:
        K = a.shape[1]
        ktile = K if n_pairs > 1 else tk
        n = b.shape[2]
        assert b.shape[1] == K
        in_specs.append(pl.BlockSpec((tm, ktile), lambda i, j, k: (i, k)))
        if tn <= n:
            assert n % tn == 0
            r = n // tn
            in_specs.append(pl.BlockSpec((None, ktile, tn),
                                         lambda i, j, k, r=r, off=off: ((j + off) // r, k, (j + off) % r)))
            slabs.append(0)
        else:
            assert tn % n == 0
            in_specs.append(pl.BlockSpec((tn // n, ktile, n), lambda i, j, k, off=off: (j + off, k, 0)))
            slabs.append(tn // n)
        operands += [a, b]
    ij = lambda g: (g[0], g[1])
    for arr, kind in extras:
        in_specs.append(_extra_spec(arr.shape, kind, tm, tn, ij))
        operands.append(arr)
    out_specs = [_extra_spec(o.shape, kind, tm, tn, ij) for o, kind in outs]
    n_ex, n_out = len(extras), len(outs)

    def pair_dot(ab, q):
        a = ab[2 * q][...]
        if not slabs[q]:
            return _dot(a, ab[2 * q + 1][...], ((1,), (0,)))
        return jnp.concatenate([_dot(a, ab[2 * q + 1][s], ((1,), (0,))) for s in range(slabs[q])], axis=1)

    def body(*refs):
        ab = refs[:2 * n_pairs]
        ex = refs[2 * n_pairs:2 * n_pairs + n_ex]
        out = refs[2 * n_pairs + n_ex:2 * n_pairs + n_ex + n_out]
        i = pl.program_id(0)
        if nk == 1:
            accs = [pair_dot(ab, q) for q in range(n_pairs)]
            epi(accs, ex, out, i)
        else:
            acc_ref = refs[-1]
            k = pl.program_id(2)

            @pl.when(k == 0)
            def _():
                acc_ref[...] = jnp.zeros_like(acc_ref)

            acc_ref[...] += pair_dot(ab, 0)

            @pl.when(k == nk - 1)
            def _():
                epi([acc_ref[...]], ex, out, i)

    scratch = [pltpu.VMEM((tm, tn), F32)] if nk > 1 else []
    sem = ("arbitrary",) * 3 if has_rowacc else ("parallel", "parallel", "arbitrary")
    return _call(body, comm, name=name, grid=(M // tm, N // tn, nk), in_specs=in_specs, out_specs=out_specs,
                 out_shape=[o for o, _ in outs], scratch_shapes=scratch, sem=sem, operands=operands)


def mm_nt(name, a, b, extras, outs, epi, *, tm, tko, tc, comm=None):
    M, N = a.shape
    J, K, n = b.shape
    assert J * n == N and M % tm == 0 and K % tko == 0 and N % tc == 0
    nc = N // tc
    has_rowacc = any(kind == 'rowacc' for _, kind in outs)
    assert not has_rowacc or K == tko
    if tc <= n:
        assert n % tc == 0
        r = n // tc
        slabs = 0
        b_spec = pl.BlockSpec((None, tko, tc), lambda i, j, c: (c // r, j, c % r))
    else:
        assert tc % n == 0
        slabs = tc // n
        b_spec = pl.BlockSpec((slabs, tko, n), lambda i, j, c: (c, j, 0))
    in_specs = [pl.BlockSpec((tm, tc), lambda i, j, c: (i, c)), b_spec]

    def nt_dot(a_ref, b_ref):
        if not slabs:
            return _dot(a_ref[...], b_ref[...], ((1,), (1,)))
        acc = _dot(a_ref[:, 0:n], b_ref[0], ((1,), (1,)))
        for s in range(1, slabs):
            acc = acc + _dot(a_ref[:, s * n:(s + 1) * n], b_ref[s], ((1,), (1,)))
        return acc

    ij = lambda g: (g[0], g[1])
    operands = [a, b]
    for arr, kind in extras:
        in_specs.append(_extra_spec(arr.shape, kind, tm, tko, ij))
        operands.append(arr)
    out_specs = [_extra_spec(o.shape, kind, tm, tko, ij) for o, kind in outs]
    n_ex, n_out = len(extras), len(outs)

    def body(*refs):
        a_ref, b_ref = refs[:2]
        ex = refs[2:2 + n_ex]
        out = refs[2 + n_ex:2 + n_ex + n_out]
        i = pl.program_id(0)
        if nc == 1:
            epi(nt_dot(a_ref, b_ref), ex, out, i)
        else:
            acc_ref = refs[-1]
            c = pl.program_id(2)

            @pl.when(c == 0)
            def _():
                acc_ref[...] = jnp.zeros_like(acc_ref)

            acc_ref[...] += nt_dot(a_ref, b_ref)

            @pl.when(c == nc - 1)
            def _():
                epi(acc_ref[...], ex, out, i)

    scratch = [pltpu.VMEM((tm, tko), F32)] if nc > 1 else []
    sem = ("arbitrary",) * 3 if has_rowacc else ("parallel", "parallel", "arbitrary")
    return _call(body, comm, name=name, grid=(M // tm, K // tko, nc), in_specs=in_specs, out_specs=out_specs,
                 out_shape=[o for o, _ in outs], scratch_shapes=scratch, sem=sem, operands=operands)


def mm_tn(name, a, d, n, out_dtype, *, tm, tk, tn, comm=None):
    M, K = a.shape
    N = d.shape[1]
    assert d.shape[0] == M and N % n == 0 and N % tn == 0 and K % tk == 0 and M % tm == 0
    nm = M // tm
    if tn <= n:
        assert n % tn == 0
        r = n // tn
        slabs = 0
        o_spec = pl.BlockSpec((None, tk, tn), lambda kk, j, m: (j // r, kk, j % r))
    else:
        assert tn % n == 0
        slabs = tn // n
        o_spec = pl.BlockSpec((slabs, tk, n), lambda kk, j, m: (j, kk, 0))

    def body(a_ref, d_ref, o_ref, acc_ref):
        m = pl.program_id(2)

        @pl.when(m == 0)
        def _():
            acc_ref[...] = jnp.zeros_like(acc_ref)

        acc_ref[...] += _dot(a_ref[...], d_ref[...], ((0,), (0,)))

        @pl.when(m == nm - 1)
        def _():
            if not slabs:
                o_ref[...] = acc_ref[...].astype(o_ref.dtype)
            else:
                for s in range(slabs):
                    o_ref[s] = acc_ref[:, s * n:(s + 1) * n].astype(o_ref.dtype)

    return _call(
        body, comm, name=name, grid=(K // tk, N // tn, nm),
        in_specs=[pl.BlockSpec((tm, tk), lambda kk, j, m: (m, kk)), pl.BlockSpec((tm, tn), lambda kk, j, m: (m, j))],
        out_specs=o_spec, out_shape=_sds((N // n, K, n), out_dtype), scratch_shapes=[pltpu.VMEM((tk, tn), F32)],
        sem=("parallel", "parallel", "arbitrary"), operands=[a, d])


def _init_or_add(ref, i, value):
    @pl.when(i == 0)
    def _():
        ref[...] = value

    @pl.when(i > 0)
    def _():
        ref[...] += value


def _rope_tables(T):
    pos = jnp.arange(T, dtype=F32)
    inv_freq = ROPE_THETA ** (-jnp.arange(0, ROPE_DIM, 2, dtype=F32) / ROPE_DIM)
    ang = pos[:, None] * inv_freq[None, :]
    cos, sin = jnp.cos(ang), jnp.sin(ang)
    ones = jnp.ones((T, HEAD_DIM - ROPE_DIM), F32)
    zeros = jnp.zeros((T, HEAD_DIM - ROPE_DIM), F32)
    zh = jnp.zeros((T, ROPE_HALF), F32)
    c_head = jnp.concatenate([cos, cos, ones], axis=1)
    s_up = jnp.concatenate([-sin, zh, zeros], axis=1)
    s_dn = jnp.concatenate([zh, sin, zeros], axis=1)
    rep = LANES // HEAD_DIM
    return tuple(jnp.tile(t, (1, rep)) for t in (c_head, s_up, s_dn))


def _rope_chunk(t, c, s_up, s_dn):
    return t * c + pltpu.roll(t, LANES - ROPE_HALF, 1) * s_up + pltpu.roll(t, ROPE_HALF, 1) * s_dn


def _rope_chunk_bwd(d, c, s_up, s_dn):
    return d * c + pltpu.roll(d * s_up, ROPE_HALF, 1) + pltpu.roll(d * s_dn, LANES - ROPE_HALF, 1)


def _rope_wide(t, c, s_up, s_dn, fn):
    chunks = [fn(t[:, q * LANES:(q + 1) * LANES], c, s_up, s_dn) for q in range(t.shape[1] // LANES)]
    return chunks[0] if len(chunks) == 1 else jnp.concatenate(chunks, axis=1)


def _taps_by_residue(first):
    groups = {}
    for o in range(first, first + CONV_WIDTH):
        groups.setdefault(o % SUBLANES, []).append(o)
    return sorted(groups.items())


def _shifted_taps(win_ref, span_ref, r0, res, offs, rb, ls):
    if res == 0:
        return [functools.partial(lambda o: win_ref[pl.ds(r0 + o, rb), ls], o) for o in offs]
    n = rb + offs[-1] - res
    span_ref[0:n, :] = win_ref[pl.ds(r0 + res, n), ls]
    return [functools.partial(lambda o: span_ref[pl.ds(o - res, rb), :], o) for o in offs]


def _halo_before_spec(tm, D):
    return pl.BlockSpec((CONV_HALO, D), lambda i: (jnp.maximum(i * (tm // CONV_HALO) - 1, 0), 0))


def dwconv_fwd(g, w_dw, b_dw, ln_g, ln_b, *, tm, comm=None):
    T, D = g.shape
    nl = D // LANES

    rb = min(CONV_ROWS, tm)

    def body(g_ref, gh_ref, w_ref, b_ref, lg_ref, lb_ref, c_ref, s_ref, win_ref, span_ref):
        i = pl.program_id(0)
        win_ref[0:CONV_HALO, :] = jnp.where(i > 0, gh_ref[...], 0.0)
        win_ref[CONV_HALO:, :] = g_ref[...]

        def lane_chunk(q, carry):
            ls = pl.ds(pl.multiple_of(q * LANES, LANES), LANES)
            for r0 in range(0, tm, rb):
                acc = jnp.broadcast_to(b_ref[:, ls], (rb, LANES))
                for res, offs in _taps_by_residue(CONV_HALO - (CONV_WIDTH - 1)):
                    taps = _shifted_taps(win_ref, span_ref, r0, res, offs, rb, ls)
                    for o, tap in zip(offs, taps):
                        k = o - (CONV_HALO - (CONV_WIDTH - 1))
                        acc = acc + tap() * w_ref[k:k + 1, ls]
                c_ref[pl.ds(r0, rb), ls] = acc
            return carry

        lax.fori_loop(0, nl, lane_chunk, 0)
        n = _ln_fwd(c_ref[...], lg_ref[...], lb_ref[...])
        s_ref[...] = (n * _sigmoid(n)).astype(s_ref.dtype)

    row = pl.BlockSpec((1, D), lambda i: (0, 0))
    return _call(
        body, comm, name="dwconv_fwd", grid=(T // tm,),
        in_specs=[pl.BlockSpec((tm, D), lambda i: (i, 0)), _halo_before_spec(tm, D),
                  pl.BlockSpec((CONV_HALO, D), lambda i: (0, 0)), row, row, row],
        out_specs=[pl.BlockSpec((tm, D), lambda i: (i, 0)), pl.BlockSpec((tm, D), lambda i: (i, 0))],
        out_shape=[_sds((T, D), F32), _sds((T, D), BF16)],
        scratch_shapes=[pltpu.VMEM((tm + CONV_HALO, D), F32), pltpu.VMEM((rb + CONV_HALO, LANES), F32)],
        sem=("parallel",), operands=[g, g, w_dw, b_dw, ln_g, ln_b])


def dwconv_bwd(dc, g, ha, hg, w_dw, *, tm, comm=None):
    T, D = g.shape
    nl = D // LANES
    last = T // CONV_HALO - 1
    nt = T // tm

    rb = min(CONV_ROWS, tm)

    def body(dc_ref, dcn_ref, g_ref, gh_ref, ha_ref, hg_ref, w_ref, dh_ref, dw_ref, dbdw_ref, dbin_ref,
             win_ref, dwin_ref, dg_ref, dwp_ref, span_ref):
        i = pl.program_id(0)
        win_ref[0:CONV_HALO, :] = jnp.where(i > 0, gh_ref[...], 0.0)
        win_ref[CONV_HALO:, :] = g_ref[...]
        dwin_ref[0:tm, :] = dc_ref[...]
        dwin_ref[tm:, :] = jnp.where(i < nt - 1, dcn_ref[...], 0.0)

        @pl.when(i == 0)
        def _():
            dwp_ref[...] = jnp.zeros_like(dwp_ref)

        first = CONV_HALO - (CONV_WIDTH - 1)

        def lane_chunk(q, carry):
            ls = pl.ds(pl.multiple_of(q * LANES, LANES), LANES)
            for r0 in range(0, tm, rb):
                acc = jnp.zeros((rb, LANES), F32)
                for res, offs in _taps_by_residue(0):
                    taps = _shifted_taps(dwin_ref, span_ref, r0, res, offs, rb, ls)
                    for o, tap in zip(offs, taps):
                        k = CONV_WIDTH - 1 - o
                        acc = acc + tap() * w_ref[k:k + 1, ls]
                dg_ref[pl.ds(r0, rb), ls] = acc
                dcv = dwin_ref[pl.ds(r0, rb), ls]
                for res, offs in _taps_by_residue(first):
                    taps = _shifted_taps(win_ref, span_ref, r0, res, offs, rb, ls)
                    for o, tap in zip(offs, taps):
                        k = o - first
                        prod = dcv * tap()
                        dwp_ref[k, :, ls] += jnp.sum(prod.reshape(rb // SUBLANES, SUBLANES, LANES), axis=0)
            return carry

        lax.fori_loop(0, nl, lane_chunk, 0)

        @pl.when(i == nt - 1)
        def _():
            for k in range(CONV_WIDTH):
                dw_ref[k:k + 1, :] = jnp.sum(dwp_ref[k], axis=0, keepdims=True)
            dw_ref[CONV_WIDTH:, :] = jnp.zeros((CONV_HALO - CONV_WIDTH, D), F32)
        dg = dg_ref[...]
        ha = ha_ref[...].astype(F32)
        sg = _sigmoid(hg_ref[...].astype(F32))
        d_ha = dg * sg
        d_hg = dg * ha * sg * (1.0 - sg)
        dh_ref[:, 0:D] = d_ha.astype(dh_ref.dtype)
        dh_ref[:, D:] = d_hg.astype(dh_ref.dtype)
        _init_or_add(dbdw_ref, i, jnp.sum(dc_ref[...], axis=0, keepdims=True))
        _init_or_add(dbin_ref, i, jnp.concatenate([jnp.sum(d_ha, axis=0, keepdims=True),
                                                   jnp.sum(d_hg, axis=0, keepdims=True)], axis=1))

    tile = pl.BlockSpec((tm, D), lambda i: (i, 0))
    return _call(
        body, comm, name="dwconv_bwd", grid=(nt,),
        in_specs=[tile,
                  pl.BlockSpec((CONV_HALO, D), lambda i: (jnp.minimum((i + 1) * (tm // CONV_HALO), last), 0)),
                  tile, _halo_before_spec(tm, D),
                  tile, tile, pl.BlockSpec((CONV_HALO, D), lambda i: (0, 0))],
        out_specs=[pl.BlockSpec((tm, 2 * D), lambda i: (i, 0)), pl.BlockSpec((CONV_HALO, D), lambda i: (0, 0)),
                   pl.BlockSpec((1, D), lambda i: (0, 0)), pl.BlockSpec((1, 2 * D), lambda i: (0, 0))],
        out_shape=[_sds((T, 2 * D), BF16), _sds((CONV_HALO, D), F32), _sds((1, D), F32), _sds((1, 2 * D), F32)],
        scratch_shapes=[pltpu.VMEM((tm + CONV_HALO, D), F32), pltpu.VMEM((tm + CONV_HALO, D), F32),
                        pltpu.VMEM((tm, D), F32), pltpu.VMEM((CONV_WIDTH, SUBLANES, D), F32),
                        pltpu.VMEM((rb + CONV_HALO, LANES), F32)],
        sem=("arbitrary",), operands=[dc, dc, g, g, ha, hg, w_dw])


def _attn_specs(HD, W):
    B = ATT_BLOCK
    return [pl.BlockSpec((B, HD), lambda n: (n, 0)),
            pl.BlockSpec((B, 4 * W), lambda n: (n, 0)),
            pl.BlockSpec((B, 4 * W), lambda n: (jnp.maximum(n - 1, 0), 0))]


def _band_mask(n):
    r = lax.broadcasted_iota(jnp.int32, (ATT_BLOCK, 2 * ATT_BLOCK), 0)
    j = lax.broadcasted_iota(jnp.int32, (ATT_BLOCK, 2 * ATT_BLOCK), 1)
    return (j > r) & (j <= r + ATT_BLOCK) & ((n > 0) | (j >= ATT_BLOCK))


def _band(kvc_ref, kvp_ref, part, g, parity, W):
    lanes = slice((2 * part + parity) * W + g * LANES, (2 * part + parity) * W + (g + 1) * LANES)
    return jnp.concatenate([kvp_ref[:, lanes], kvc_ref[:, lanes]], axis=0)


def _half_mask(parity):
    lane = lax.broadcasted_iota(jnp.int32, (1, LANES), 1)
    return (lane < HEAD_DIM) if parity == 0 else (lane >= HEAD_DIM)


def widen_kv(k, v, out_ref, n_kv):
    W = n_kv * LANES
    low = _half_mask(0)
    for part, src in enumerate((k, v)):
        for cg in range(n_kv * HEAD_DIM // LANES):
            chunk = src[:, cg * LANES:(cg + 1) * LANES]
            swapped = pltpu.roll(chunk, HEAD_DIM, 1)
            for g, lo, hi in ((2 * cg, chunk, swapped), (2 * cg + 1, swapped, chunk)):
                base = 2 * part * W + g * LANES
                out_ref[:, base:base + LANES] = jnp.where(low, lo, 0.0).astype(out_ref.dtype)
                out_ref[:, base + W:base + W + LANES] = jnp.where(low, 0.0, hi).astype(out_ref.dtype)


def attn_fwd(q, kvx, sinks, n_heads, n_kv, comm=None):
    T, HD = q.shape
    W = n_kv * LANES
    B = ATT_BLOCK
    chunks_per_group = n_heads // n_kv // 2
    scale = 1.0 / math.sqrt(HEAD_DIM)

    def body(q_ref, kvc_ref, kvp_ref, sink_ref, o_ref, lse_ref):
        n = pl.program_id(0)
        mask = jnp.tile(_band_mask(n), (n_heads, 1))
        s = jnp.concatenate(
            [_dot(q_ref[:, (h // 2) * LANES:(h // 2 + 1) * LANES],
                  _band(kvc_ref, kvp_ref, 0, h // 2 // chunks_per_group, h % 2, W), ((1,), (1,)))
             for h in range(n_heads)], axis=0)
        sink = jnp.concatenate([jnp.broadcast_to(sink_ref[:, h:h + 1], (B, 1)) for h in range(n_heads)], axis=0)
        s = jnp.where(mask, s * scale, MASK_VALUE)
        m = jnp.maximum(jnp.max(s, axis=-1, keepdims=True), sink)
        e = jnp.exp(s - m)
        total = _dot(e, jnp.ones((2 * B, LANES), BF16), ((1,), (0,))) + jnp.exp(sink - m)
        lse = m + jnp.log(total[:, 0:1])
        inv = 1.0 / total
        probs = (e * jnp.concatenate([inv, inv], axis=1)).astype(BF16)
        for c in range(n_heads // 2):
            g = c // chunks_per_group
            out = (_dot(probs[2 * c * B:(2 * c + 1) * B], _band(kvc_ref, kvp_ref, 1, g, 0, W), ((1,), (0,)))
                   + _dot(probs[(2 * c + 1) * B:(2 * c + 2) * B], _band(kvc_ref, kvp_ref, 1, g, 1, W), ((1,), (0,))))
            o_ref[:, c * LANES:(c + 1) * LANES] = out.astype(o_ref.dtype)
        lse_ref[...] = jnp.concatenate([lse[h * B:(h + 1) * B] for h in range(n_heads)], axis=1)

    return _call(
        body, comm, name="attn_fwd", grid=(T // B,),
        in_specs=_attn_specs(HD, W) + [pl.BlockSpec((1, n_heads), lambda n: (0, 0))],
        out_specs=[pl.BlockSpec((B, HD), lambda n: (n, 0)), pl.BlockSpec((B, n_heads), lambda n: (n, 0))],
        out_shape=[_sds((T, HD), BF16), _sds((T, n_heads), F32)],
        sem=("parallel",), operands=[q, kvx, kvx, sinks])


def attn_bwd(q, kvx, do, lse, sinks, rope, n_heads, n_kv, comm=None):
    T, HD = q.shape
    KVD = n_kv * HEAD_DIM
    W = n_kv * LANES
    B = ATT_BLOCK
    chunks_per_group = n_heads // n_kv // 2
    scale = 1.0 / math.sqrt(HEAD_DIM)
    nb = T // B

    def body(q_ref, kvc_ref, kvp_ref, do_ref, lse_ref, sink_ref, c_ref, su_ref, sd_ref, dq_ref, dkv_ref, dsink_ref):
        n = pl.program_id(0)

        @pl.when(n == 0)
        def _():
            dkv_ref[...] = jnp.zeros_like(dkv_ref)
            dsink_ref[...] = jnp.zeros_like(dsink_ref)

        def chunk(ref, h):
            return ref[:, (h // 2) * LANES:(h // 2 + 1) * LANES]

        def band(part, h):
            return _band(kvc_ref, kvp_ref, part, h // 2 // chunks_per_group, h % 2, W)

        def stack(per_head):
            return jnp.concatenate([per_head(h) for h in range(n_heads)], axis=0)

        mask = jnp.tile(_band_mask(n), (n_heads, 1))
        s = stack(lambda h: _dot(chunk(q_ref, h), band(0, h), ((1,), (1,))))
        dp = stack(lambda h: _dot(chunk(do_ref, h), band(1, h), ((1,), (1,))))
        lse = stack(lambda h: lse_ref[:, h:h + 1])
        sink = stack(lambda h: jnp.broadcast_to(sink_ref[:, h:h + 1], (B, 1)))
        probs = jnp.exp(jnp.where(mask, s * scale, MASK_VALUE) - lse)
        delta = jnp.sum(probs * dp, axis=-1, keepdims=True)
        ds = (probs * (dp - delta) * scale).astype(BF16)
        probs = probs.astype(BF16)
        sink_term = jnp.exp(sink - lse) * delta
        dsk = [-jnp.sum(sink_term[h * B:(h + 1) * B], axis=0, keepdims=True) for h in range(n_heads)]

        dk_wide, dv_wide = [None] * n_kv, [None] * n_kv
        for c in range(n_heads // 2):
            g = c // chunks_per_group
            dq2 = None
            for h in (2 * c, 2 * c + 1):
                half = _half_mask(h % 2)
                q2, do2 = chunk(q_ref, h), chunk(do_ref, h)
                ds_h, p_h = ds[h * B:(h + 1) * B], probs[h * B:(h + 1) * B]
                part = _dot(ds_h, band(0, h), ((1,), (0,)))
                dq2 = part if dq2 is None else dq2 + part
                dk_h = _dot(ds_h, jnp.where(half, q2, jnp.zeros_like(q2)), ((0,), (0,)))
                dv_h = _dot(p_h, jnp.where(half, do2, jnp.zeros_like(do2)), ((0,), (0,)))
                dk_wide[g] = dk_h if dk_wide[g] is None else dk_wide[g] + dk_h
                dv_wide[g] = dv_h if dv_wide[g] is None else dv_wide[g] + dv_h
            dq_ref[:, c * LANES:(c + 1) * LANES] = _rope_chunk_bwd(
                dq2, c_ref[...], su_ref[...], sd_ref[...]).astype(dq_ref.dtype)

        def fold(wide):
            low = _half_mask(0)
            both = [w + pltpu.roll(w, HEAD_DIM, 1) for w in wide]
            return jnp.concatenate([jnp.where(low, both[2 * cg], both[2 * cg + 1]) for cg in range(n_kv // 2)], axis=1)

        dkv = jnp.concatenate([fold(dk_wide), fold(dv_wide)], axis=1)
        prev = pl.ds(pl.multiple_of(jnp.maximum(n - 1, 0) * B, B), B)
        cur = pl.ds(pl.multiple_of(n * B, B), B)
        dkv_ref[prev, :] += dkv[0:B, :]
        dkv_ref[cur, :] += dkv[B:, :]
        dsink_ref[...] += jnp.concatenate(dsk, axis=1)

    tab = pl.BlockSpec((B, LANES), lambda n: (n, 0))
    return _call(
        body, comm, name="attn_bwd", grid=(nb,),
        in_specs=_attn_specs(HD, W) + [pl.BlockSpec((B, HD), lambda n: (n, 0)),
                                       pl.BlockSpec((B, n_heads), lambda n: (n, 0)),
                                       pl.BlockSpec((1, n_heads), lambda n: (0, 0)), tab, tab, tab],
        out_specs=[pl.BlockSpec((B, HD), lambda n: (n, 0)), pl.BlockSpec((T, 2 * KVD), lambda n: (0, 0)),
                   pl.BlockSpec((1, n_heads), lambda n: (0, 0))],
        out_shape=[_sds((T, HD + 2 * KVD), BF16), _sds((T, 2 * KVD), F32), _sds((1, n_heads), F32)],
        sem=("arbitrary",), operands=[q, kvx, kvx, do, lse, sinks, *rope])


def dkv_finish(d_qkv, dkv, rope, HD, KVD, *, tm):
    T = dkv.shape[0]
    kv_col = HD // (2 * KVD)

    def body(alias_ref, dkv_ref, c_ref, su_ref, sd_ref, o_ref):
        del alias_ref
        dk = _rope_wide(dkv_ref[:, 0:KVD], c_ref[...], su_ref[...], sd_ref[...], _rope_chunk_bwd)
        o_ref[:, 0:KVD] = dk.astype(o_ref.dtype)
        o_ref[:, KVD:] = dkv_ref[:, KVD:].astype(o_ref.dtype)

    tab = pl.BlockSpec((tm, LANES), lambda i: (i, 0))
    return pl.pallas_call(
        body, name="dkv_finish", grid=(T // tm,),
        in_specs=[pl.BlockSpec(memory_space=pl.ANY), pl.BlockSpec((tm, 2 * KVD), lambda i: (i, 0)), tab, tab, tab],
        out_specs=pl.BlockSpec((tm, 2 * KVD), lambda i: (i, kv_col)),
        out_shape=_sds(d_qkv.shape, d_qkv.dtype), input_output_aliases={0: 0},
        compiler_params=_params(("parallel",)))(d_qkv, dkv, *rope)


def ple_bwd_elem(d_out, pp, gg, *, tm):
    T, D = d_out.shape

    def body(d_ref, pp_ref, gg_ref, dpp_ref, dgg_ref):
        d = d_ref[...]
        sg = _sigmoid(gg_ref[...].astype(F32))
        dpp_ref[...] = (d * sg).astype(dpp_ref.dtype)
        dgg_ref[...] = (d * pp_ref[...].astype(F32) * sg * (1.0 - sg)).astype(dgg_ref.dtype)

    tile = pl.BlockSpec((tm, D), lambda i: (i, 0))
    return pl.pallas_call(
        body, name="ple_bwd_elem", grid=(T // tm,), in_specs=[tile, tile, tile], out_specs=[tile, tile],
        out_shape=[_sds((T, D), BF16), _sds((T, D), BF16)], compiler_params=_params(("parallel",)))(d_out, pp, gg)


def chip_sum(name, g, p_sib, *, ta):
    _, a, b = g.shape
    assert a % ta == 0

    def body(core_ref, g_ref, p_ref, o_ref):
        del core_ref
        o_ref[...] = (g_ref[...].astype(F32) + p_ref[...].astype(F32)).astype(o_ref.dtype)

    blk = pl.BlockSpec((None, ta, b), lambda q, i, core: (q, i, 0))
    my_core = lax.axis_index("c").astype(jnp.int32).reshape(1)
    return pl.pallas_call(
        body, name=name, out_shape=_sds((N_CHIPS, a, b), g.dtype),
        grid_spec=pltpu.PrefetchScalarGridSpec(
            num_scalar_prefetch=1, grid=(N_CHIPS, a // ta),
            in_specs=[pl.BlockSpec((None, None, ta, b), lambda q, i, core: (q, core[0], i, 0)), blk],
            out_specs=blk),
        compiler_params=_params(("arbitrary", "arbitrary")))(my_core, g.reshape(N_CHIPS, 2, a, b), p_sib)


def adamw(name, recvs, w, m, v, *, ta):
    L, a, b = w.shape
    n_terms = recvs[0].shape[0]
    assert a % ta == 0 and len(recvs) == L
    c1 = 1.0 - ADAM_B1 ** ADAM_STEP
    c2 = 1.0 - ADAM_B2 ** ADAM_STEP

    def body(*refs):
        r_refs = refs[:L]
        w_ref, m_ref, v_ref, g_ref, d_ref, nm_ref, nv_ref = refs[L:]
        layer = pl.program_id(0)
        for l in range(L):
            @pl.when(layer == l)
            def _(r_ref=r_refs[l]):
                g = r_ref[0].astype(F32)
                for s in range(1, n_terms):
                    g = g + r_ref[s].astype(F32)
                nm = ADAM_B1 * m_ref[...] + (1.0 - ADAM_B1) * g
                nv = ADAM_B2 * v_ref[...] + (1.0 - ADAM_B2) * jnp.square(g)
                m_hat = nm / c1
                v_hat = nv / c2
                g_ref[...] = g
                d_ref[...] = -ADAM_LR * (m_hat / (jnp.sqrt(v_hat) + ADAM_EPS) + ADAM_WD * w_ref[...])
                nm_ref[...] = nm
                nv_ref[...] = nv

    blk = pl.BlockSpec((None, ta, b), lambda l, i: (l, i, 0))
    out = _sds((L, a, b), F32)
    r_specs = [pl.BlockSpec((n_terms, ta, b), lambda l, i, ll=ll: (0, jnp.where(l == ll, i, 0), 0)) for ll in range(L)]
    return pl.pallas_call(
        body, name=name, grid=(L, a // ta), in_specs=r_specs + [blk, blk, blk],
        out_specs=[blk, blk, blk, blk], out_shape=[out, out, out, out],
        compiler_params=_params(("arbitrary", "arbitrary")))(*recvs, w, m, v)


def _pack_rows(parts):
    out = []
    for arr, rows in parts:
        arr = arr.reshape(-1, LANES).astype(F32)
        out.append(jnp.pad(arr, ((0, rows - arr.shape[0]), (0, 0))))
    return jnp.concatenate(out, axis=0)


def _small_rows(a, prefix):
    return _pack_rows([(a[prefix + name], rows) for name, rows in SMALL_SHARDED])


def _unpack_small(packed, a):
    out, r0 = {}, 0
    for name, rows in SMALL_SHARDED:
        shape = a[name].shape
        used = math.prod(shape) // LANES
        out[name] = packed[r0:r0 + used].reshape(shape)
        r0 += rows
    return out


def _rep_rows(a, prefix):
    parts = []
    for name in REPLICATED:
        arr = a[prefix + name]
        if arr.size % LANES:
            arr = jnp.pad(arr.reshape(1, -1), ((0, 0), (0, LANES - arr.size % LANES)))
        rows = -(-arr.size // LANES)
        parts.append((arr, -(-rows // SUBLANES) * SUBLANES))
    return _pack_rows(parts)


def _unpack_rep(packed, a):
    out, r0 = {}, 0
    for name in REPLICATED:
        shape = a[name].shape
        size = math.prod(shape)
        rows = -(-size // LANES)
        out[name] = packed[r0:r0 + rows].reshape(-1)[:size].reshape(shape)
        r0 += -(-rows // SUBLANES) * SUBLANES
    return out


def _step(a):
    x = a['x'][0]
    T, D = x.shape
    tgt = a['loss_target'][0]
    p_in = [a['p'][i, 0] for i in range(DEPTH)]
    PLE = p_in[0].shape[1]
    n_heads = a['attn_sinks'].shape[1]
    HD = n_heads * HEAD_DIM
    KVD = a['kv_w_k'].shape[1]
    n_kv = KVD // HEAD_DIM
    F = a['mlp_w_down'].shape[1] * N_DEV
    tm = min(TOKEN_TILE, T)
    tm2 = min(2 * TOKEN_TILE, T)
    tmc = min(TOKEN_TILE, T)
    alpha = DEEPNORM_ALPHA
    xb = x.astype(BF16)
    p_b = [p.astype(BF16) for p in p_in]

    def shard3(w):
        return w.reshape((1,) + w.shape) if w.ndim == 2 else w

    def gather(*specs):
        return _GatherJob([shard3(a[nm])[l].astype(BF16) for nm, l in specs])

    (W_in, small_full), = comm_only("gather_first",
                                    [_GatherJob([a['conv_w_in'][0].astype(BF16), _small_rows(a, '')])])
    r0, small = 0, {}
    for name, rows in SMALL_SHARDED:
        small[name] = small_full[:, r0:r0 + rows]
        r0 += rows
    b_in = small['conv_b_in'][:, 0:2 * D // N_DEV // LANES].reshape(1, 2 * D)
    w_dw = jnp.transpose(small['conv_w_dw'], (1, 0, 2)).reshape(CONV_HALO, D)
    b_dw, cln_g, cln_b, b_out = (small[nm][:, 0].reshape(1, D) for nm in
                                 ('conv_b_dw', 'conv_ln_g', 'conv_ln_b', 'conv_b_out'))
    W_up, W_down, W_proj, W_gate = {}, {}, {}, {}
    mix_g, mix_b, mlp_g, mlp_b = a['mix_ln_g'], a['mix_ln_b'], a['mlp_ln_g'], a['mlp_ln_b']
    rope = _rope_tables(T)

    def set_ple_weights(li, g_proj, g_gate):
        W_proj[li] = jnp.transpose(g_proj, (1, 0, 2)).reshape(1, PLE, D)
        W_gate[li] = g_gate.reshape(1, D, D)

    def row(v, i):
        return v[i:i + 1]

    def res_ln_epi(coef):
        def epi(accs, ex, out, i):
            acc = accs[0] if isinstance(accs, list) else accs
            n_ex = len(ex)
            res_ref, g_ref, b_ref = ex[n_ex - 3], ex[n_ex - 2], ex[n_ex - 1]
            z = coef * res_ref[...] + acc
            if n_ex == 4:
                z = z + ex[0][...]
            out[0][...] = z
            xo = _ln_fwd(z, g_ref[...], b_ref[...])
            out[1][...] = xo
            out[2][...] = xo.astype(BF16)
        return epi

    res_ln_outs = [(_sds((T, D), F32), 'tile'), (_sds((T, D), F32), 'tile'), (_sds((T, D), BF16), 'tile')]

    def mlp_fwd(li, xin, xin_b, up_comm=None, down_comm=None):
        def up_epi(accs, ex, out, i):
            u = accs[0]
            out[0][...] = u.astype(BF16)
            out[1][...] = jnp.square(jnp.maximum(u, 0.0)).astype(BF16)
        res = mm_nn(f"mlp_up_{li}", [(xin_b, W_up[li], 0)], [], [(_sds((T, F), BF16), 'tile')] * 2, up_epi,
                    tm=tm2, tn=min(1024, F), comm=up_comm)
        (u, act), got_up = res if up_comm is not None else (res, ())
        if li not in W_down:
            W_down[li] = got_up[0].reshape(1, F, D)
        res = mm_nn(f"mlp_down_{li}", [(act, W_down[li], 0)],
                    [(xin, 'tile'), (row(mlp_g, li), 'row'), (row(mlp_b, li), 'row')],
                    res_ln_outs, res_ln_epi(alpha), tm=tm, tn=D, tk=min(1024, F), comm=down_comm)
        (z, xo, xo_b), got_down = res if down_comm is not None else (res, ())
        return u, act, z, xo, xo_b, got_up, got_down

    def ple_fwd(li, xin, xin_b, with_loss):
        def epi(accs, ex, out, i):
            pp, gg = accs
            xo = ex[0][...] + pp * _sigmoid(gg)
            out[1][...] = pp.astype(BF16)
            out[2][...] = gg.astype(BF16)
            if with_loss:
                err = xo - ex[1][...]
                out[0][...] = err * (1.0 / D)
                _init_or_add(out[3], i, jnp.sum(err * err, axis=0, keepdims=True) * (0.5 / D))
            else:
                out[0][...] = xo
                out[3][...] = xo.astype(BF16)
        extras = [(xin, 'tile')] + ([(tgt, 'tile')] if with_loss else [])
        outs = [(_sds((T, D), F32), 'tile'), (_sds((T, D), BF16), 'tile'), (_sds((T, D), BF16), 'tile')]
        outs.append((_sds((1, D), F32), 'rowacc') if with_loss else (_sds((T, D), BF16), 'tile'))
        return mm_nn(f"ple_{li}", [(p_b[li], W_proj[li], 0), (xin_b, W_gate[li], 0)], extras, outs, epi, tm=tm, tn=D)

    assert D // N_DEV == LANES
    (g0, ha0, hg0), (g_wout,) = _glu(xb, W_in, b_in, T, D, tm, gather(('conv_w_out', 0)))
    W_out = g_wout.reshape(1, D, D)
    (c0, s0), (W_up[0],) = dwconv_fwd(g0, w_dw, b_dw, cln_g, cln_b, tm=tmc, comm=gather(('mlp_w_up', 0)))
    (z1, x1, x1b), got = mm_nn("conv_out", [(s0, W_out, 0)],
                               [(b_out, 'row'), (x, 'tile'), (row(mix_g, 0), 'row'), (row(mix_b, 0), 'row')],
                               res_ln_outs, res_ln_epi(alpha), tm=tm, tn=D,
                               comm=gather(('ple_w_proj', 0), ('ple_w_gate', 0)))
    set_ple_weights(0, *got)
    u0, act0, z2, x2, x2b, _, got = mlp_fwd(
        0, x1, x1b, up_comm=gather(('mlp_w_down', 0)),
        down_comm=gather(('attn_w_q', 0), ('kv_w_k', 0), ('kv_w_v', 0), ('attn_w_o', 0)))
    W_qkv = jnp.concatenate([got[0].reshape(D, HD), got[1].reshape(D, KVD), got[2].reshape(D, KVD)], axis=1)[None]
    W_o = got[3].reshape(1, HD, D)
    x3, pp0, gg0, x3b = ple_fwd(0, x2, x2b, False)

    def qkv_epi(accs, ex, out, i):
        t = accs[0]
        c, su, sd = ex[0][...], ex[1][...], ex[2][...]
        out[0][...] = _rope_wide(t[:, 0:HD], c, su, sd, _rope_chunk).astype(BF16)
        widen_kv(_rope_wide(t[:, HD:HD + KVD], c, su, sd, _rope_chunk), t[:, HD + KVD:], out[1], n_kv)
    NQ = HD + 2 * KVD
    q1, kvx1 = mm_nn("qkv_rope", [(x3b, W_qkv, 0)], [(t, 'tab') for t in rope],
                     [(_sds((T, HD), BF16), 'rows'), (_sds((T, 4 * n_kv * LANES), BF16), 'rows')], qkv_epi,
                     tm=tm, tn=NQ)
    (o1, lse1), (W_up[1], g_down1) = attn_fwd(q1, kvx1, a['attn_sinks'], n_heads, n_kv,
                                              comm=gather(('mlp_w_up', 1), ('mlp_w_down', 1)))
    W_down[1] = g_down1.reshape(1, F, D)
    (z3, x4, x4b), got = mm_nn("attn_out", [(o1, W_o, 0)],
                               [(x3, 'tile'), (row(mix_g, 1), 'row'), (row(mix_b, 1), 'row')],
                               res_ln_outs, res_ln_epi(alpha), tm=tm, tn=D,
                               comm=gather(('ple_w_proj', 1), ('ple_w_gate', 1)))
    set_ple_weights(1, *got)
    u1, act1, z4, x5, x5b, _, _ = mlp_fwd(1, x4, x4b)
    dy, pp1, gg1, loss_row = ple_fwd(1, x5, x5b, True)
    loss_local = jnp.sum(loss_row)

    grads = {}

    def ln_bwd_epi(coef, with_colsum):
        def epi(acc, ex, out, i):
            d_x = acc + coef * ex[0][...]
            dz, dg, db = _ln_bwd(d_x, ex[1][...], ex[2][...])
            out[0][...] = dz
            out[1][...] = dz.astype(BF16)
            _init_or_add(out[2], i, dg)
            _init_or_add(out[3], i, db)
            if with_colsum:
                _init_or_add(out[4], i, jnp.sum(dz, axis=0, keepdims=True))
        return epi

    def ln_bwd_outs(with_colsum):
        outs = [(_sds((T, D), F32), 'tile'), (_sds((T, D), BF16), 'tile'), (_sds((1, D), F32), 'rowacc'),
                (_sds((1, D), F32), 'rowacc')]
        return outs + ([(_sds((1, D), F32), 'rowacc')] if with_colsum else [])

    def ple_bwd(li, d_out, xin, pp, gg, z_mlp, pair_specs=None):
        d_pp, d_gg = ple_bwd_elem(d_out, pp, gg, tm=tm)
        grads[('ple_w_proj', li)] = mm_tn(f"d_ple_proj_{li}", p_b[li], d_pp, D, BF16, tm=tm, tk=PLE, tn=D)
        grads[('ple_w_gate', li)] = mm_tn(f"d_ple_gate_{li}", xin, d_gg, D, BF16, tm=tm, tk=D, tn=D)
        job = pair_stage(*pair_specs) if pair_specs else None
        res = mm_nt(f"ple_dx_{li}", d_gg, W_gate[li], [(d_out, 'tile'), (z_mlp, 'tile'), (row(mlp_g, li), 'row')],
                    ln_bwd_outs(False), ln_bwd_epi(1.0, False), tm=tm, tko=D, tc=D, comm=job)
        (dz, dzb, dg, db), got = res if job is not None else (res, None)
        grads[('mlp_ln_g', li)], grads[('mlp_ln_b', li)] = dg, db
        return dz, dzb, (chip_stage(job, got) if job is not None else None)

    recv = {}
    wqkv_cols = {'attn_w_q': (0, HD), 'kv_w_k': (HD, HD + KVD), 'kv_w_v': (HD + KVD, NQ)}

    def piece(name, li):
        if name == 'conv_w_in':
            return grads['conv_w_in']
        if name == 'mlp_w_up':
            return grads[('mlp_w_up', li)]
        if name == 'ple_w_proj':
            return jnp.transpose(grads[('ple_w_proj', li)][0].reshape(PLE, N_DEV, D // N_DEV), (1, 0, 2))
        if name in wqkv_cols:
            g = grads['w_qkv'][:, wqkv_cols[name][0]:wqkv_cols[name][1]]
        else:
            g = grads[name] if name in grads else grads[(name, li)]
            g = g[0]
        return g.reshape(N_DEV, g.shape[0] // N_DEV, g.shape[1])

    def pair_stage(*specs):
        job = _PairJob([piece(nm, li) for nm, li in specs])
        job.specs = specs
        return job

    def chip_stage(pair_job, got):
        sums = [chip_sum(f"chip_sum_{nm}_{li}", mine, theirs, ta=min(512, mine.shape[1]))
                for (nm, li), mine, theirs in zip(pair_job.specs, pair_job.sources, got)]
        job = _ChipJob(sums)
        job.specs = pair_job.specs
        return job

    def keep(job, got):
        for spec, r in zip(job.specs, got):
            recv[spec] = r

    def mlp_bwd(li, dz, dzb, xin, u, act, z_mix, with_colsum, pair_specs, chip_job, down_job=None):
        res = mm_tn(f"d_mlp_down_{li}", act, dzb, D, BF16, tm=tm, tk=min(1024, F), tn=D, comm=down_job)
        if down_job is not None:
            res, got = res
            keep(down_job, got)
        grads[('mlp_w_down', li)] = res

        def du_epi(acc, ex, out, i):
            out[0][...] = (acc * (2.0 * jnp.maximum(ex[0][...].astype(F32), 0.0))).astype(BF16)
        pair = pair_stage(*pair_specs)
        (du,), got = mm_nt(f"mlp_du_{li}", dzb, W_down[li], [(u, 'tile')], [(_sds((T, F), BF16), 'tile')], du_epi,
                           tm=tm2, tko=min(1024, F), tc=D, comm=pair)
        pending = chip_stage(pair, got)
        grads[('mlp_w_up', li)] = mm_tn(f"d_mlp_up_{li}", xin, du, F // N_DEV, BF16, tm=tm, tk=D, tn=min(1024, F))
        if chip_job is None:
            chip_job, pending = pending, None
        res, got = mm_nt(f"mlp_dx_{li}", du, W_up[li], [(dz, 'tile'), (z_mix, 'tile'), (row(mix_g, li), 'row')],
                         ln_bwd_outs(with_colsum), ln_bwd_epi(alpha, with_colsum), tm=tm, tko=D, tc=min(1024, F),
                         comm=chip_job)
        keep(chip_job, got)
        grads[('mix_ln_g', li)], grads[('mix_ln_b', li)] = res[2], res[3]
        return res, pending

    dz4, dz4b, _ = ple_bwd(1, dy, x5b, pp1, gg1, z4)
    (dz3, dz3b, _, _), _ = mlp_bwd(1, dz4, dz4b, x4b, u1, act1, z3, False,
                                   [('mlp_w_down', 1), ('ple_w_gate', 1), ('ple_w_proj', 1)], None)
    grads['attn_w_o'] = mm_tn("d_attn_wo", o1, dz3b, D, BF16, tm=tm, tk=HD, tn=D)

    def do_epi(acc, ex, out, i):
        out[0][...] = acc.astype(BF16)
    pair = pair_stage(('mlp_w_up', 1), ('attn_w_o', 0))
    (do1,), got = mm_nt("attn_do", dz3b, W_o, [], [(_sds((T, HD), BF16), 'tile')], do_epi, tm=tm, tko=HD, tc=D,
                        comm=pair)
    job = chip_stage(pair, got)
    (d_qkv, dkv, d_sinks), got = attn_bwd(q1, kvx1, do1, lse1, a['attn_sinks'], rope, n_heads, n_kv, comm=job)
    keep(job, got)
    d_qkv = dkv_finish(d_qkv, dkv, rope, HD, KVD, tm=tm)
    grads['w_qkv'] = mm_tn("d_wqkv", x3b, d_qkv, NQ, BF16, tm=tm, tk=D, tn=NQ)[0]

    def dx3_epi(acc, ex, out, i):
        out[0][...] = acc + alpha * ex[0][...]
    dx3, = mm_nt("attn_dx", d_qkv, W_qkv, [(dz3, 'tile')], [(_sds((T, D), F32), 'tile')], dx3_epi,
                 tm=tm, tko=D, tc=NQ)

    dz2, dz2b, job = ple_bwd(0, dx3, x2b, pp0, gg0, z2,
                             [('attn_w_q', 0), ('kv_w_k', 0), ('kv_w_v', 0), ('ple_w_gate', 0), ('ple_w_proj', 0)])
    (dz1, dz1b, _, _, db_out), _ = mlp_bwd(0, dz2, dz2b, x1b, u0, act0, z1, True, [('mlp_w_down', 0)], None, job)
    grads['conv_w_out'] = mm_tn("d_conv_wout", s0, dz1b, D, BF16, tm=tm, tk=D, tn=D)

    def ds_epi(acc, ex, out, i):
        n = _ln_fwd(ex[0][...], ex[1][...], ex[2][...])
        sg = _sigmoid(n)
        dn = acc * (sg * (1.0 + n * (1.0 - sg)))
        dc, dg, db = _ln_bwd(dn, ex[0][...], ex[1][...])
        out[0][...] = dc
        _init_or_add(out[1], i, dg)
        _init_or_add(out[2], i, db)
    pair = pair_stage(('mlp_w_up', 0), ('conv_w_out', 0))
    (dc0, d_cln_g, d_cln_b), got = mm_nt("conv_ds", dz1b, W_out, [(c0, 'tile'), (cln_g, 'row'), (cln_b, 'row')],
                                         [(_sds((T, D), F32), 'tile'), (_sds((1, D), F32), 'rowacc'),
                                          (_sds((1, D), F32), 'rowacc')], ds_epi, tm=tm, tko=D, tc=D, comm=pair)
    job = chip_stage(pair, got)
    (dh0, d_wdw, d_bdw, d_bin), got = dwconv_bwd(dc0, g0, ha0, hg0, w_dw, tm=tmc, comm=job)
    keep(job, got)
    grads['conv_w_in'] = mm_tn("d_conv_win", xb, dh0, 2 * D // N_DEV, BF16, tm=tm, tk=D, tn=D)

    def dx_epi(acc, ex, out, i):
        out[0][...] = acc + alpha * ex[0][...]
    pair = pair_stage(('conv_w_in', 0))
    (grad_x,), got = mm_nt("conv_dx", dh0, W_in, [(dz1, 'tile')], [(_sds((T, D), F32), 'tile')], dx_epi,
                           tm=tm, tko=D, tc=D, comm=pair)
    last_chip = chip_stage(pair, got)

    def own_rows(vec, rows_used, rows):
        arr = vec.reshape(N_DEV, rows_used, LANES)
        return jnp.pad(arr, ((0, 0), (0, rows - rows_used), (0, 0)))
    dwdw_dev = jnp.transpose(d_wdw.reshape(CONV_HALO, N_DEV, D // N_DEV), (1, 0, 2))
    lane_rows = D // N_DEV // LANES
    small_grad = jnp.concatenate([
        own_rows(d_bin, 2 * lane_rows, 8), dwdw_dev if lane_rows == 1 else dwdw_dev.reshape(N_DEV, -1, LANES),
        own_rows(d_bdw, lane_rows, 8), own_rows(d_cln_g, lane_rows, 8), own_rows(d_cln_b, lane_rows, 8),
        own_rows(db_out, lane_rows, 8)], axis=1)
    n_small = small_grad.shape[1]

    rep_local = {'mix_ln_g': jnp.concatenate([grads[('mix_ln_g', li)] for li in range(DEPTH)], axis=0),
                 'mix_ln_b': jnp.concatenate([grads[('mix_ln_b', li)] for li in range(DEPTH)], axis=0),
                 'mlp_ln_g': jnp.concatenate([grads[('mlp_ln_g', li)] for li in range(DEPTH)], axis=0),
                 'mlp_ln_b': jnp.concatenate([grads[('mlp_ln_b', li)] for li in range(DEPTH)], axis=0),
                 'attn_sinks': d_sinks}
    rep_grad = _rep_rows(rep_local, '')
    n_rep = rep_grad.shape[0]
    got, (recv_small, recv_rep) = comm_only("exchange_last", [last_chip, _DirectJob(
        [small_grad, jnp.broadcast_to(rep_grad[None], (N_DEV, n_rep, LANES))])])
    keep(last_chip, got)

    result = {}
    kinds = ('grad', 'delta', 'new_m', 'new_v')
    w, m, v = (_small_rows(a, pre)[None] for pre in ('', 'm_', 'v_'))
    for kind, arr in zip(kinds, adamw("adamw_small", [recv_small], w, m, v, ta=n_small)):
        for pname, val in _unpack_small(arr[0], a).items():
            result[(kind, pname)] = val
    w, m, v = (_rep_rows(a, pre)[None] for pre in ('', 'm_', 'v_'))
    for kind, arr in zip(kinds, adamw("adamw_rep", [recv_rep], w, m, v, ta=n_rep)):
        for pname, val in _unpack_rep(arr[0], a).items():
            result[(kind, pname)] = val
    for name in BIG_WEIGHTS:
        w, m, v = (shard3(a[pre + name]) for pre in ('', 'm_', 'v_'))
        recvs = [recv[(name, li)] for li in range(w.shape[0])]
        for kind, arr in zip(kinds, adamw("adamw_" + name, recvs, w, m, v, ta=min(256, w.shape[1]))):
            result[(kind, name)] = arr.reshape(a[name].shape)

    loss = lax.psum(loss_local, ("x", "y", "c"))
    out = [loss, grad_x[None]]
    for kind in ('grad', 'delta', 'new_m', 'new_v'):
        out += [result[(kind, name)] for name in WEIGHT_NAMES]
    return tuple(out)


def _glu(x, W_in, b_in, T, D, tm, comm=None):
    n = W_in.shape[2]
    q = 2 if D // n % 2 == 0 else 1
    nt = D // (q * n)
    tn = q * n

    def body(x_ref, wa_ref, wg_ref, ba_ref, bg_ref, g_ref, ha_ref, hg_ref):
        xb = x_ref[...]
        ha = jnp.concatenate([_dot(xb, wa_ref[s], ((1,), (0,))) for s in range(q)], axis=1) + ba_ref[...]
        hg = jnp.concatenate([_dot(xb, wg_ref[s], ((1,), (0,))) for s in range(q)], axis=1) + bg_ref[...]
        g_ref[...] = ha * _sigmoid(hg)
        ha_ref[...] = ha.astype(ha_ref.dtype)
        hg_ref[...] = hg.astype(hg_ref.dtype)

    return _call(
        body, comm, name="conv_in_glu", grid=(T // tm, nt),
        in_specs=[pl.BlockSpec((tm, D), lambda i, j: (i, 0)),
                  pl.BlockSpec((q, D, n), lambda i, j: (j, 0, 0)),
                  pl.BlockSpec((q, D, n), lambda i, j: (j + nt, 0, 0)),
                  pl.BlockSpec((1, tn), lambda i, j: (0, j)), pl.BlockSpec((1, tn), lambda i, j: (0, j + nt))],
        out_specs=[pl.BlockSpec((tm, tn), lambda i, j: (i, j))] * 3,
        out_shape=[_sds((T, D), F32), _sds((T, D), BF16), _sds((T, D), BF16)],
        sem=("parallel", "parallel"), operands=[x, W_in, W_in, b_in, b_in])


def kernel(x, p, conv_w_in, conv_b_in, conv_w_dw, conv_b_dw, conv_ln_g, conv_ln_b, conv_w_out, conv_b_out, kv_w_k, kv_w_v, attn_w_q, attn_sinks, attn_w_o, mix_ln_g, mix_ln_b, mlp_w_up, mlp_w_down, mlp_ln_g, mlp_ln_b, ple_w_proj, ple_w_gate, loss_target, m_conv_w_in, m_conv_b_in, m_conv_w_dw, m_conv_b_dw, m_conv_ln_g, m_conv_ln_b, m_conv_w_out, m_conv_b_out, m_kv_w_k, m_kv_w_v, m_attn_w_q, m_attn_sinks, m_attn_w_o, m_mix_ln_g, m_mix_ln_b, m_mlp_w_up, m_mlp_w_down, m_mlp_ln_g, m_mlp_ln_b, m_ple_w_proj, m_ple_w_gate, v_conv_w_in, v_conv_b_in, v_conv_w_dw, v_conv_b_dw, v_conv_ln_g, v_conv_ln_b, v_conv_w_out, v_conv_b_out, v_kv_w_k, v_kv_w_v, v_attn_w_q, v_attn_sinks, v_attn_w_o, v_mix_ln_g, v_mix_ln_b, v_mlp_w_up, v_mlp_w_down, v_mlp_ln_g, v_mlp_ln_b, v_ple_w_proj, v_ple_w_gate):
    return _step(dict(locals()))
```

```python
import functools
import math

import jax
import jax.numpy as jnp
from jax import lax
from jax.experimental import pallas as pl
from jax.experimental.pallas import tpu as pltpu

F32 = jnp.float32
BF16 = jnp.bfloat16

N_DEV = 8
HEAD_DIM = 64
ROPE_DIM = HEAD_DIM // 4
ROPE_HALF = ROPE_DIM // 2
ROPE_THETA = 500000.0
ATT_BLOCK = 128
CONV_WIDTH = 31
CONV_HALO = 32
CONV_ROWS = 128
LN_EPS = 1e-5
DEPTH = 2
DEEPNORM_ALPHA = (2 * DEPTH) ** 0.25
MASK_VALUE = -1e30

ADAM_LR = 0.001
ADAM_B1 = 0.9
ADAM_B2 = 0.999
ADAM_EPS = 1e-08
ADAM_WD = 0.01
ADAM_STEP = 10

LANES = 128
SUBLANES = 8
VMEM_LIMIT_BYTES = 52 * 1024 * 1024
TOKEN_TILE = 512
MESH_ID = pl.DeviceIdType.MESH
FORWARD_AT_TENTHS = 6

WEIGHT_NAMES = ['conv_w_in', 'conv_b_in', 'conv_w_dw', 'conv_b_dw', 'conv_ln_g', 'conv_ln_b', 'conv_w_out',
                'conv_b_out', 'kv_w_k', 'kv_w_v', 'attn_w_q', 'attn_sinks', 'attn_w_o', 'mix_ln_g', 'mix_ln_b',
                'mlp_w_up', 'mlp_w_down', 'mlp_ln_g', 'mlp_ln_b', 'ple_w_proj', 'ple_w_gate']
BIG_WEIGHTS = ['conv_w_in', 'conv_w_out', 'kv_w_k', 'kv_w_v', 'attn_w_q', 'attn_w_o', 'mlp_w_up', 'mlp_w_down',
               'ple_w_proj', 'ple_w_gate']
SMALL_SHARDED = [('conv_b_in', 8), ('conv_w_dw', 32), ('conv_b_dw', 8), ('conv_ln_g', 8), ('conv_ln_b', 8),
                 ('conv_b_out', 8)]
REPLICATED = ['mix_ln_g', 'mix_ln_b', 'mlp_ln_g', 'mlp_ln_b', 'attn_sinks']


def _params(sem):
    return pltpu.CompilerParams(dimension_semantics=sem, vmem_limit_bytes=VMEM_LIMIT_BYTES)


def _sds(shape, dtype):
    return jax.ShapeDtypeStruct(shape, dtype)


def _my_place():
    x, y, c = lax.axis_index("x"), lax.axis_index("y"), lax.axis_index("c")
    return x, y, c, 4 * x + 2 * y + c


def _peers(x, y, c):
    out = []
    for dx in (0, 1):
        for dy in (0, 1):
            for dc in (0, 1):
                if dx or dy or dc:
                    px, py, pc = x ^ dx, y ^ dy, c ^ dc
                    out.append(((px, py, pc), 4 * px + 2 * py + pc))
    return out


N_CHIPS = 4


def _other_chips(x, y):
    return [((x ^ dx, y ^ dy), 2 * (x ^ dx) + (y ^ dy)) for dx, dy in ((1, 0), (0, 1), (1, 1))]


def _remote(src, dst, send, recv, to):
    return pltpu.make_async_remote_copy(src_ref=src, dst_ref=dst, send_sem=send, recv_sem=recv, device_id=to,
                                        device_id_type=MESH_ID)


def _wait_slabs(buf, count, send, recv, me, sent=True, received=True):
    part = buf.at[pl.ds(0, count)]
    cp = _remote(part, part, send, recv, me)
    if sent:
        cp.wait_send()
    if received:
        cp.wait_recv()


class _DirectJob:
    n_sems = 3

    def __init__(self, items):
        self.sources = list(items)
        self.dests = [_sds(it.shape, it.dtype) for it in items]
        self.n = len(items)

    def start(self, src, dst, send, recv, loc):
        x, y, c, me = _my_place()
        for k in range(self.n):
            here = dst[k].at[pl.ds(me, 1)]
            pltpu.make_async_copy(src[k].at[pl.ds(me, 1)], here, loc.at[k]).start()
            for peer, idx in _peers(x, y, c):
                _remote(src[k].at[pl.ds(idx, 1)], here, send.at[k], recv.at[k], peer).start()

    def forward(self, src, dst, send, recv, loc):
        pass

    def finish(self, src, dst, send, recv, loc):
        x, y, c, me = _my_place()
        for k in range(self.n):
            pltpu.make_async_copy(src[k].at[pl.ds(me, 1)], dst[k].at[pl.ds(me, 1)], loc.at[k]).wait()
            _wait_slabs(dst[k], N_DEV - 1, send.at[k], recv.at[k], (x, y, c))


class _GatherJob:
    n_sems = 5

    def __init__(self, items):
        self.sources = [it.reshape((1,) + it.shape) for it in items]
        self.dests = [_sds((N_DEV,) + it.shape, it.dtype) for it in items]
        self.n = len(items)

    def start(self, src, dst, send_ici, recv_ici, send_d2d, recv_d2d, loc):
        x, y, c, me = _my_place()
        for k in range(self.n):
            here = dst[k].at[pl.ds(me, 1)]
            pltpu.make_async_copy(src[k], here, loc.at[k]).start()
            _remote(src[k], here, send_d2d.at[k], recv_d2d.at[k], (x, y, 1 - c)).start()
            for (px, py), _ in _other_chips(x, y):
                _remote(src[k], here, send_ici.at[k], recv_ici.at[k], (px, py, c)).start()

    def forward(self, src, dst, send_ici, recv_ici, send_d2d, recv_d2d, loc):
        x, y, c, me = _my_place()
        for k in range(self.n):
            _wait_slabs(dst[k], N_CHIPS - 1, send_ici.at[k], recv_ici.at[k], (x, y, c), sent=False)
            for _, q in _other_chips(x, y):
                rows = dst[k].at[pl.ds(2 * q + c, 1)]
                _remote(rows, rows, send_d2d.at[k], recv_d2d.at[k], (x, y, 1 - c)).start()

    def finish(self, src, dst, send_ici, recv_ici, send_d2d, recv_d2d, loc):
        x, y, c, me = _my_place()
        for k in range(self.n):
            pltpu.make_async_copy(src[k], dst[k].at[pl.ds(me, 1)], loc.at[k]).wait()
            _wait_slabs(dst[k], N_CHIPS - 1, send_ici.at[k], recv_ici.at[k], (x, y, c), received=False)
            _wait_slabs(dst[k], N_CHIPS, send_d2d.at[k], recv_d2d.at[k], (x, y, c))


class _PairJob:
    n_sems = 2

    def __init__(self, items):
        self.sources = list(items)
        self.dests = [_sds((N_CHIPS,) + it.shape[1:], it.dtype) for it in items]
        self.n = len(items)

    def start(self, src, dst, send, recv):
        x, y, c, me = _my_place()
        for k in range(self.n):
            for q in range(N_CHIPS):
                _remote(src[k].at[pl.ds(2 * q + 1 - c, 1)], dst[k].at[pl.ds(q, 1)], send.at[k], recv.at[k],
                        (x, y, 1 - c)).start()

    def forward(self, src, dst, send, recv):
        pass

    def finish(self, src, dst, send, recv):
        x, y, c, me = _my_place()
        for k in range(self.n):
            _wait_slabs(dst[k], N_CHIPS, send.at[k], recv.at[k], (x, y, c))


class _ChipJob:
    n_sems = 3

    def __init__(self, items):
        self.sources = list(items)
        self.dests = [_sds(it.shape, it.dtype) for it in items]
        self.n = len(items)

    def start(self, src, dst, send, recv, loc):
        x, y, c, me = _my_place()
        mine = 2 * x + y
        for k in range(self.n):
            here = dst[k].at[pl.ds(mine, 1)]
            pltpu.make_async_copy(src[k].at[pl.ds(mine, 1)], here, loc.at[k]).start()
            for (px, py), q in _other_chips(x, y):
                _remote(src[k].at[pl.ds(q, 1)], here, send.at[k], recv.at[k], (px, py, c)).start()

    def forward(self, src, dst, send, recv, loc):
        pass

    def finish(self, src, dst, send, recv, loc):
        x, y, c, me = _my_place()
        mine = 2 * x + y
        for k in range(self.n):
            pltpu.make_async_copy(src[k].at[pl.ds(mine, 1)], dst[k].at[pl.ds(mine, 1)], loc.at[k]).wait()
            _wait_slabs(dst[k], N_CHIPS - 1, send.at[k], recv.at[k], (x, y, c))


def _job_sems(job):
    return [pltpu.SemaphoreType.DMA((job.n,))] * job.n_sems


def _call(body, comm, *, name, grid, in_specs, out_specs, out_shape, operands, sem, scratch_shapes=(), aliases=None):
    single = not isinstance(out_shape, (list, tuple))
    out_shape = [out_shape] if single else list(out_shape)
    out_specs = [out_specs] if single else list(out_specs)
    if comm is None:
        res = pl.pallas_call(body, name=name, grid=grid, in_specs=list(in_specs), out_specs=out_specs,
                             out_shape=out_shape, scratch_shapes=list(scratch_shapes),
                             input_output_aliases=aliases or {}, compiler_params=_params(sem))(*operands)
        return res[0] if single else res
    n_in, n_out, n_scr, n_c = len(in_specs), len(out_shape), len(scratch_shapes), comm.n
    any_spec = pl.BlockSpec(memory_space=pl.ANY)
    steps = math.prod(grid)
    mid = min(steps - 1, (steps * FORWARD_AT_TENTHS) // 10)

    def hosted(*refs):
        ins, c_src = refs[:n_in], refs[n_in:n_in + n_c]
        outs = refs[n_in + n_c:n_in + n_c + n_out]
        c_dst = refs[n_in + n_c + n_out:n_in + 2 * n_c + n_out]
        scr = refs[n_in + 2 * n_c + n_out:n_in + 2 * n_c + n_out + n_scr]
        sems = refs[n_in + 2 * n_c + n_out + n_scr:]
        step = pl.program_id(0)
        for d in range(1, len(grid)):
            step = step * grid[d] + pl.program_id(d)

        @pl.when(step == 0)
        def _():
            comm.start(c_src, c_dst, *sems)

        @pl.when(step == mid)
        def _():
            comm.forward(c_src, c_dst, *sems)

        body(*ins, *outs, *scr)

        @pl.when(step == steps - 1)
        def _():
            comm.finish(c_src, c_dst, *sems)

    res = pl.pallas_call(hosted, name=name, grid=grid, in_specs=list(in_specs) + [any_spec] * n_c,
                         out_specs=out_specs + [any_spec] * n_c, out_shape=out_shape + comm.dests,
                         scratch_shapes=list(scratch_shapes) + _job_sems(comm), input_output_aliases=aliases or {},
                         compiler_params=_params(("arbitrary",) * len(grid)))(*operands, *comm.sources)
    main = res[:n_out]
    return (main[0] if single else main), res[n_out:]


def comm_only(name, jobs):
    any_spec = pl.BlockSpec(memory_space=pl.ANY)
    n_all = sum(job.n for job in jobs)

    def body(*refs):
        srcs, dsts, sems = refs[:n_all], refs[n_all:2 * n_all], refs[2 * n_all:]
        parts, k0, s0 = [], 0, 0
        for job in jobs:
            parts.append((job, srcs[k0:k0 + job.n], dsts[k0:k0 + job.n], sems[s0:s0 + job.n_sems]))
            k0 += job.n
            s0 += job.n_sems
        for stage in ('start', 'forward', 'finish'):
            for job, src, dst, sem in parts:
                getattr(job, stage)(src, dst, *sem)

    res = pl.pallas_call(body, name=name, in_specs=[any_spec] * n_all, out_specs=[any_spec] * n_all,
                         out_shape=[d for job in jobs for d in job.dests],
                         scratch_shapes=[s for job in jobs for s in _job_sems(job)],
                         )(*[s for job in jobs for s in job.sources])
    out, k0 = [], 0
    for job in jobs:
        out.append(res[k0:k0 + job.n])
        k0 += job.n
    return out


def _dot(a, b, dims):
    if a.dtype != BF16:
        a = a.astype(BF16)
    if b.dtype != BF16:
        b = b.astype(BF16)
    return lax.dot_general(a, b, (dims, ((), ())), preferred_element_type=F32)


def _sigmoid(v):
    return 1.0 / (1.0 + jnp.exp(-v))


def _ln_stats(z):
    mu = jnp.mean(z, axis=-1, keepdims=True)
    zc = z - mu
    var = jnp.mean(zc * zc, axis=-1, keepdims=True)
    return zc * lax.rsqrt(var + LN_EPS)


def _ln_fwd(z, g, b):
    return _ln_stats(z) * g + b


def _ln_bwd(dy, z, g):
    xhat = _ln_stats(z)
    mu = jnp.mean(z, axis=-1, keepdims=True)
    zc = z - mu
    rstd = lax.rsqrt(jnp.mean(zc * zc, axis=-1, keepdims=True) + LN_EPS)
    dxh = dy * g
    m1 = jnp.mean(dxh, axis=-1, keepdims=True)
    m2 = jnp.mean(dxh * xhat, axis=-1, keepdims=True)
    dz = rstd * (dxh - m1 - xhat * m2)
    return dz, jnp.sum(dy * xhat, axis=0, keepdims=True), jnp.sum(dy, axis=0, keepdims=True)


def _extra_spec(shape, kind, tm, tn, ij):
    if kind == 'tile':
        return pl.BlockSpec((tm, tn), lambda *g: (ij(g)[0], ij(g)[1]))
    if kind in ('row', 'rowacc'):
        return pl.BlockSpec((1, tn), lambda *g: (0, ij(g)[1]))
    if kind == 'tab':
        return pl.BlockSpec((tm, LANES), lambda *g: (ij(g)[0], 0))
    if kind == 'rows':
        return pl.BlockSpec((tm, shape[1]), lambda *g: (ij(g)[0], 0))
    raise ValueError(kind)


def mm_nn(name, pairs, extras, outs, epi, *, tm, tn, tk=None, comm=None):
    M = pairs[0][0].shape[0]
    N = pairs[0][1].shape[0] * pairs[0][1].shape[2]
    n_pairs = len(pairs)
    K0 = pairs[0][0].shape[1]
    tk = K0 if tk is None else tk
    nk = K0 // tk
    assert nk == 1 or n_pairs == 1
    assert M % tm == 0 and N % tn == 0 and K0 % tk == 0
    has_rowacc = any(kind == 'rowacc' for _, kind in outs)
    assert not has_rowacc or (N == tn and nk == 1)
    in_specs, operands, slabs = [], [], []
    for a, b, off in pairs:
        K = a.shape[1]
        ktile = K if n_pairs > 1 else tk
        n = b.shape[2]
        assert b.shape[1] == K
        in_specs.append(pl.BlockSpec((tm, ktile), lambda i, j, k: (i, k)))
        if tn <= n:
            assert n % tn == 0
            r = n // tn
            in_specs.append(pl.BlockSpec((None, ktile, tn),
                                         lambda i, j, k, r=r, off=off: ((j + off) // r, k, (j + off) % r)))
            slabs.append(0)
        else:
            assert tn % n == 0
            in_specs.append(pl.BlockSpec((tn // n, ktile, n), lambda i, j, k, off=off: (j + off, k, 0)))
            slabs.append(tn // n)
        operands += [a, b]
    ij = lambda g: (g[0], g[1])
    for arr, kind in extras:
        in_specs.append(_extra_spec(arr.shape, kind, tm, tn, ij))
        operands.append(arr)
    out_specs = [_extra_spec(o.shape, kind, tm, tn, ij) for o, kind in outs]
    n_ex, n_out = len(extras), len(outs)

    def pair_dot(ab, q):
        a = ab[2 * q][...]
        if not slabs[q]:
            return _dot(a, ab[2 * q + 1][...], ((1,), (0,)))
        return jnp.concatenate([_dot(a, ab[2 * q + 1][s], ((1,), (0,))) for s in range(slabs[q])], axis=1)

    def body(*refs):
        ab = refs[:2 * n_pairs]
        ex = refs[2 * n_pairs:2 * n_pairs + n_ex]
        out = refs[2 * n_pairs + n_ex:2 * n_pairs + n_ex + n_out]
        i = pl.program_id(0)
        if nk == 1:
            accs = [pair_dot(ab, q) for q in range(n_pairs)]
            epi(accs, ex, out, i)
        else:
            acc_ref = refs[-1]
            k = pl.program_id(2)

            @pl.when(k == 0)
            def _():
                acc_ref[...] = jnp.zeros_like(acc_ref)

            acc_ref[...] += pair_dot(ab, 0)

            @pl.when(k == nk - 1)
            def _():
                epi([acc_ref[...]], ex, out, i)

    scratch = [pltpu.VMEM((tm, tn), F32)] if nk > 1 else []
    sem = ("arbitrary",) * 3 if has_rowacc else ("parallel", "parallel", "arbitrary")
    return _call(body, comm, name=name, grid=(M // tm, N // tn, nk), in_specs=in_specs, out_specs=out_specs,
                 out_shape=[o for o, _ in outs], scratch_shapes=scratch, sem=sem, operands=operands)


def mm_nt(name, a, b, extras, outs, epi, *, tm, tko, tc, comm=None):
    M, N = a.shape
    J, K, n = b.shape
    assert J * n == N and M % tm == 0 and K % tko == 0 and N % tc == 0
    nc = N // tc
    has_rowacc = any(kind == 'rowacc' for _, kind in outs)
    assert not has_rowacc or K == tko
    if tc <= n:
        assert n % tc == 0
        r = n // tc
        slabs = 0
        b_spec = pl.BlockSpec((None, tko, tc), lambda i, j, c: (c // r, j, c % r))
    else:
        assert tc % n == 0
        slabs = tc // n
        b_spec = pl.BlockSpec((slabs, tko, n), lambda i, j, c: (c, j, 0))
    in_specs = [pl.BlockSpec((tm, tc), lambda i, j, c: (i, c)), b_spec]

    def nt_dot(a_ref, b_ref):
        if not slabs:
            return _dot(a_ref[...], b_ref[...], ((1,), (1,)))
        acc = _dot(a_ref[:, 0:n], b_ref[0], ((1,), (1,)))
        for s in range(1, slabs):
            acc = acc + _dot(a_ref[:, s * n:(s + 1) * n], b_ref[s], ((1,), (1,)))
        return acc

    ij = lambda g: (g[0], g[1])
    operands = [a, b]
    for arr, kind in extras:
        in_specs.append(_extra_spec(arr.shape, kind, tm, tko, ij))
        operands.append(arr)
    out_specs = [_extra_spec(o.shape, kind, tm, tko, ij) for o, kind in outs]
    n_ex, n_out = len(extras), len(outs)

    def body(*refs):
        a_ref, b_ref = refs[:2]
        ex = refs[2:2 + n_ex]
        out = refs[2 + n_ex:2 + n_ex + n_out]
        i = pl.program_id(0)
        if nc == 1:
            epi(nt_dot(a_ref, b_ref), ex, out, i)
        else:
            acc_ref = refs[-1]
            c = pl.program_id(2)

            @pl.when(c == 0)
            def _():
                acc_ref[...] = jnp.zeros_like(acc_ref)

            acc_ref[...] += nt_dot(a_ref, b_ref)

            @pl.when(c == nc - 1)
            def _():
                epi(acc_ref[...], ex, out, i)

    scratch = [pltpu.VMEM((tm, tko), F32)] if nc > 1 else []
    sem = ("arbitrary",) * 3 if has_rowacc else ("parallel", "parallel", "arbitrary")
    return _call(body, comm, name=name, grid=(M // tm, K // tko, nc), in_specs=in_specs, out_specs=out_specs,
                 out_shape=[o for o, _ in outs], scratch_shapes=scratch, sem=sem, operands=operands)


def mm_tn(name, a, d, n, out_dtype, *, tm, tk, tn, comm=None):
    M, K = a.shape
    N = d.shape[1]
    assert d.shape[0] == M and N % n == 0 and N % tn == 0 and K % tk == 0 and M % tm == 0
    nm = M // tm
    if tn <= n:
        assert n % tn == 0
        r = n // tn
        slabs = 0
        o_spec = pl.BlockSpec((None, tk, tn), lambda kk, j, m: (j // r, kk, j % r))
    else:
        assert tn % n == 0
        slabs = tn // n
        o_spec = pl.BlockSpec((slabs, tk, n), lambda kk, j, m: (j, kk, 0))

    def write(o_ref, acc):
        if not slabs:
            o_ref[...] = acc.astype(o_ref.dtype)
        else:
            for s in range(slabs):
                o_ref[s] = acc[:, s * n:(s + 1) * n].astype(o_ref.dtype)

    def body(a_ref, d_ref, o_ref, *scratch):
        if nm == 1:
            write(o_ref, _dot(a_ref[...], d_ref[...], ((0,), (0,))))
            return
        acc_ref, = scratch
        m = pl.program_id(2)

        @pl.when(m == 0)
        def _():
            acc_ref[...] = jnp.zeros_like(acc_ref)

        acc_ref[...] += _dot(a_ref[...], d_ref[...], ((0,), (0,)))

        @pl.when(m == nm - 1)
        def _():
            write(o_ref, acc_ref[...])

    return _call(
        body, comm, name=name, grid=(K // tk, N // tn, nm),
        in_specs=[pl.BlockSpec((tm, tk), lambda kk, j, m: (m, kk)), pl.BlockSpec((tm, tn), lambda kk, j, m: (m, j))],
        out_specs=o_spec, out_shape=_sds((N // n, K, n), out_dtype),
        scratch_shapes=[pltpu.VMEM((tk, tn), F32)] if nm > 1 else [],
        sem=("parallel", "parallel", "arbitrary"), operands=[a, d])


def _init_or_add(ref, i, value):
    @pl.when(i == 0)
    def _():
        ref[...] = value

    @pl.when(i > 0)
    def _():
        ref[...] += value


def _rope_tables(T):
    pos = jnp.arange(T, dtype=F32)
    inv_freq = ROPE_THETA ** (-jnp.arange(0, ROPE_DIM, 2, dtype=F32) / ROPE_DIM)
    ang = pos[:, None] * inv_freq[None, :]
    cos, sin = jnp.cos(ang), jnp.sin(ang)
    ones = jnp.ones((T, HEAD_DIM - ROPE_DIM), F32)
    zeros = jnp.zeros((T, HEAD_DIM - ROPE_DIM), F32)
    zh = jnp.zeros((T, ROPE_HALF), F32)
    c_head = jnp.concatenate([cos, cos, ones], axis=1)
    s_up = jnp.concatenate([-sin, zh, zeros], axis=1)
    s_dn = jnp.concatenate([zh, sin, zeros], axis=1)
    rep = LANES // HEAD_DIM
    return tuple(jnp.tile(t, (1, rep)) for t in (c_head, s_up, s_dn))


def _rope_chunk(t, c, s_up, s_dn):
    return t * c + pltpu.roll(t, LANES - ROPE_HALF, 1) * s_up + pltpu.roll(t, ROPE_HALF, 1) * s_dn


def _rope_chunk_bwd(d, c, s_up, s_dn):
    return d * c + pltpu.roll(d * s_up, ROPE_HALF, 1) + pltpu.roll(d * s_dn, LANES - ROPE_HALF, 1)


def _rope_wide(t, c, s_up, s_dn, fn):
    chunks = [fn(t[:, q * LANES:(q + 1) * LANES], c, s_up, s_dn) for q in range(t.shape[1] // LANES)]
    return chunks[0] if len(chunks) == 1 else jnp.concatenate(chunks, axis=1)


def _taps_by_residue(first):
    groups = {}
    for o in range(first, first + CONV_WIDTH):
        groups.setdefault(o % SUBLANES, []).append(o)
    return sorted(groups.items())


def _shifted_taps(win_ref, span_ref, r0, res, offs, rb, ls):
    if res == 0:
        return [functools.partial(lambda o: win_ref[pl.ds(r0 + o, rb), ls], o) for o in offs]
    n = rb + offs[-1] - res
    span_ref[0:n, :] = win_ref[pl.ds(r0 + res, n), ls]
    return [functools.partial(lambda o: span_ref[pl.ds(o - res, rb), :], o) for o in offs]


def _halo_before_spec(tm, D):
    return pl.BlockSpec((CONV_HALO, D), lambda i: (jnp.maximum(i * (tm // CONV_HALO) - 1, 0), 0))


def dwconv_fwd(g, w_dw, b_dw, ln_g, ln_b, *, tm, comm=None):
    T, D = g.shape
    nl = D // LANES

    rb = min(CONV_ROWS, tm)

    def body(g_ref, gh_ref, w_ref, b_ref, lg_ref, lb_ref, c_ref, s_ref, win_ref, span_ref):
        i = pl.program_id(0)
        win_ref[0:CONV_HALO, :] = jnp.where(i > 0, gh_ref[...], 0.0)
        win_ref[CONV_HALO:, :] = g_ref[...]

        def lane_chunk(q, carry):
            ls = pl.ds(pl.multiple_of(q * LANES, LANES), LANES)
            for r0 in range(0, tm, rb):
                acc = jnp.broadcast_to(b_ref[:, ls], (rb, LANES))
                for res, offs in _taps_by_residue(CONV_HALO - (CONV_WIDTH - 1)):
                    taps = _shifted_taps(win_ref, span_ref, r0, res, offs, rb, ls)
                    for o, tap in zip(offs, taps):
                        k = o - (CONV_HALO - (CONV_WIDTH - 1))
                        acc = acc + tap() * w_ref[k:k + 1, ls]
                c_ref[pl.ds(r0, rb), ls] = acc
            return carry

        lax.fori_loop(0, nl, lane_chunk, 0)
        n = _ln_fwd(c_ref[...], lg_ref[...], lb_ref[...])
        s_ref[...] = (n * _sigmoid(n)).astype(s_ref.dtype)

    row = pl.BlockSpec((1, D), lambda i: (0, 0))
    return _call(
        body, comm, name="dwconv_fwd", grid=(T // tm,),
        in_specs=[pl.BlockSpec((tm, D), lambda i: (i, 0)), _halo_before_spec(tm, D),
                  pl.BlockSpec((CONV_HALO, D), lambda i: (0, 0)), row, row, row],
        out_specs=[pl.BlockSpec((tm, D), lambda i: (i, 0)), pl.BlockSpec((tm, D), lambda i: (i, 0))],
        out_shape=[_sds((T, D), F32), _sds((T, D), BF16)],
        scratch_shapes=[pltpu.VMEM((tm + CONV_HALO, D), F32), pltpu.VMEM((rb + CONV_HALO, LANES), F32)],
        sem=("parallel",), operands=[g, g, w_dw, b_dw, ln_g, ln_b])


def dwconv_bwd(dc, g, ha, hg, w_dw, *, tm, comm=None):
    T, D = g.shape
    nl = D // LANES
    last = T // CONV_HALO - 1
    nt = T // tm

    rb = min(CONV_ROWS, tm)

    def body(dc_ref, dcn_ref, g_ref, gh_ref, ha_ref, hg_ref, w_ref, dh_ref, dw_ref, dbdw_ref, dbin_ref,
             win_ref, dwin_ref, dg_ref, dwp_ref, span_ref):
        i = pl.program_id(0)
        win_ref[0:CONV_HALO, :] = jnp.where(i > 0, gh_ref[...], 0.0)
        win_ref[CONV_HALO:, :] = g_ref[...]
        dwin_ref[0:tm, :] = dc_ref[...]
        dwin_ref[tm:, :] = jnp.where(i < nt - 1, dcn_ref[...], 0.0)

        @pl.when(i == 0)
        def _():
            dwp_ref[...] = jnp.zeros_like(dwp_ref)

        first = CONV_HALO - (CONV_WIDTH - 1)

        def lane_chunk(q, carry):
            ls = pl.ds(pl.multiple_of(q * LANES, LANES), LANES)
            for r0 in range(0, tm, rb):
                acc = jnp.zeros((rb, LANES), F32)
                for res, offs in _taps_by_residue(0):
                    taps = _shifted_taps(dwin_ref, span_ref, r0, res, offs, rb, ls)
                    for o, tap in zip(offs, taps):
                        k = CONV_WIDTH - 1 - o
                        acc = acc + tap() * w_ref[k:k + 1, ls]
                dg_ref[pl.ds(r0, rb), ls] = acc
                dcv = dwin_ref[pl.ds(r0, rb), ls]
                for res, offs in _taps_by_residue(first):
                    taps = _shifted_taps(win_ref, span_ref, r0, res, offs, rb, ls)
                    for o, tap in zip(offs, taps):
                        k = o - first
                        prod = dcv * tap()
                        dwp_ref[k, :, ls] += jnp.sum(prod.reshape(rb // SUBLANES, SUBLANES, LANES), axis=0)
            return carry

        lax.fori_loop(0, nl, lane_chunk, 0)

        @pl.when(i == nt - 1)
        def _():
            for k in range(CONV_WIDTH):
                dw_ref[k:k + 1, :] = jnp.sum(dwp_ref[k], axis=0, keepdims=True)
            dw_ref[CONV_WIDTH:, :] = jnp.zeros((CONV_HALO - CONV_WIDTH, D), F32)
        dg = dg_ref[...]
        ha = ha_ref[...].astype(F32)
        sg = _sigmoid(hg_ref[...].astype(F32))
        d_ha = dg * sg
        d_hg = dg * ha * sg * (1.0 - sg)
        dh_ref[:, 0:D] = d_ha.astype(dh_ref.dtype)
        dh_ref[:, D:] = d_hg.astype(dh_ref.dtype)
        _init_or_add(dbdw_ref, i, jnp.sum(dc_ref[...], axis=0, keepdims=True))
        _init_or_add(dbin_ref, i, jnp.concatenate([jnp.sum(d_ha, axis=0, keepdims=True),
                                                   jnp.sum(d_hg, axis=0, keepdims=True)], axis=1))

    tile = pl.BlockSpec((tm, D), lambda i: (i, 0))
    return _call(
        body, comm, name="dwconv_bwd", grid=(nt,),
        in_specs=[tile,
                  pl.BlockSpec((CONV_HALO, D), lambda i: (jnp.minimum((i + 1) * (tm // CONV_HALO), last), 0)),
                  tile, _halo_before_spec(tm, D),
                  tile, tile, pl.BlockSpec((CONV_HALO, D), lambda i: (0, 0))],
        out_specs=[pl.BlockSpec((tm, 2 * D), lambda i: (i, 0)), pl.BlockSpec((CONV_HALO, D), lambda i: (0, 0)),
                   pl.BlockSpec((1, D), lambda i: (0, 0)), pl.BlockSpec((1, 2 * D), lambda i: (0, 0))],
        out_shape=[_sds((T, 2 * D), BF16), _sds((CONV_HALO, D), F32), _sds((1, D), F32), _sds((1, 2 * D), F32)],
        scratch_shapes=[pltpu.VMEM((tm + CONV_HALO, D), F32), pltpu.VMEM((tm + CONV_HALO, D), F32),
                        pltpu.VMEM((tm, D), F32), pltpu.VMEM((CONV_WIDTH, SUBLANES, D), F32),
                        pltpu.VMEM((rb + CONV_HALO, LANES), F32)],
        sem=("arbitrary",), operands=[dc, dc, g, g, ha, hg, w_dw])


def _attn_specs(HD, W):
    B = ATT_BLOCK
    return [pl.BlockSpec((B, HD), lambda n: (n, 0)),
            pl.BlockSpec((B, 4 * W), lambda n: (n, 0)),
            pl.BlockSpec((B, 4 * W), lambda n: (jnp.maximum(n - 1, 0), 0))]


def _band_mask(n):
    r = lax.broadcasted_iota(jnp.int32, (ATT_BLOCK, 2 * ATT_BLOCK), 0)
    j = lax.broadcasted_iota(jnp.int32, (ATT_BLOCK, 2 * ATT_BLOCK), 1)
    return (j > r) & (j <= r + ATT_BLOCK) & ((n > 0) | (j >= ATT_BLOCK))


def _band(kvc_ref, kvp_ref, part, g, parity, W):
    lanes = slice((2 * part + parity) * W + g * LANES, (2 * part + parity) * W + (g + 1) * LANES)
    return jnp.concatenate([kvp_ref[:, lanes], kvc_ref[:, lanes]], axis=0)


def _half_mask(parity):
    lane = lax.broadcasted_iota(jnp.int32, (1, LANES), 1)
    return (lane < HEAD_DIM) if parity == 0 else (lane >= HEAD_DIM)


def widen_kv(k, v, out_ref, n_kv):
    W = n_kv * LANES
    low = _half_mask(0)
    for part, src in enumerate((k, v)):
        for cg in range(n_kv * HEAD_DIM // LANES):
            chunk = src[:, cg * LANES:(cg + 1) * LANES]
            swapped = pltpu.roll(chunk, HEAD_DIM, 1)
            for g, lo, hi in ((2 * cg, chunk, swapped), (2 * cg + 1, swapped, chunk)):
                base = 2 * part * W + g * LANES
                out_ref[:, base:base + LANES] = jnp.where(low, lo, 0.0).astype(out_ref.dtype)
                out_ref[:, base + W:base + W + LANES] = jnp.where(low, 0.0, hi).astype(out_ref.dtype)


def attn_fwd(q, kvx, sinks, n_heads, n_kv, comm=None):
    T, HD = q.shape
    W = n_kv * LANES
    B = ATT_BLOCK
    chunks_per_group = n_heads // n_kv // 2
    scale = 1.0 / math.sqrt(HEAD_DIM)

    def body(q_ref, kvc_ref, kvp_ref, sink_ref, o_ref, lse_ref):
        n = pl.program_id(0)
        mask = jnp.tile(_band_mask(n), (n_heads, 1))
        s = jnp.concatenate(
            [_dot(q_ref[:, (h // 2) * LANES:(h // 2 + 1) * LANES],
                  _band(kvc_ref, kvp_ref, 0, h // 2 // chunks_per_group, h % 2, W), ((1,), (1,)))
             for h in range(n_heads)], axis=0)
        sink = jnp.concatenate([jnp.broadcast_to(sink_ref[:, h:h + 1], (B, 1)) for h in range(n_heads)], axis=0)
        s = jnp.where(mask, s * scale, MASK_VALUE)
        m = jnp.maximum(jnp.max(s, axis=-1, keepdims=True), sink)
        e = jnp.exp(s - m)
        total = _dot(e, jnp.ones((2 * B, LANES), BF16), ((1,), (0,))) + jnp.exp(sink - m)
        lse = m + jnp.log(total[:, 0:1])
        inv = 1.0 / total
        probs = (e * jnp.concatenate([inv, inv], axis=1)).astype(BF16)
        for c in range(n_heads // 2):
            g = c // chunks_per_group
            out = (_dot(probs[2 * c * B:(2 * c + 1) * B], _band(kvc_ref, kvp_ref, 1, g, 0, W), ((1,), (0,)))
                   + _dot(probs[(2 * c + 1) * B:(2 * c + 2) * B], _band(kvc_ref, kvp_ref, 1, g, 1, W), ((1,), (0,))))
            o_ref[:, c * LANES:(c + 1) * LANES] = out.astype(o_ref.dtype)
        lse_ref[...] = jnp.concatenate([lse[h * B:(h + 1) * B] for h in range(n_heads)], axis=1)

    return _call(
        body, comm, name="attn_fwd", grid=(T // B,),
        in_specs=_attn_specs(HD, W) + [pl.BlockSpec((1, n_heads), lambda n: (0, 0))],
        out_specs=[pl.BlockSpec((B, HD), lambda n: (n, 0)), pl.BlockSpec((B, n_heads), lambda n: (n, 0))],
        out_shape=[_sds((T, HD), BF16), _sds((T, n_heads), F32)],
        sem=("parallel",), operands=[q, kvx, kvx, sinks])


def attn_bwd(q, kvx, do, lse, sinks, rope, n_heads, n_kv, comm=None):
    T, HD = q.shape
    KVD = n_kv * HEAD_DIM
    W = n_kv * LANES
    B = ATT_BLOCK
    chunks_per_group = n_heads // n_kv // 2
    scale = 1.0 / math.sqrt(HEAD_DIM)
    nb = T // B

    def body(q_ref, kvc_ref, kvp_ref, do_ref, lse_ref, sink_ref, c_ref, su_ref, sd_ref, dq_ref, dkv_ref, dsink_ref):
        n = pl.program_id(0)

        @pl.when(n == 0)
        def _():
            dkv_ref[...] = jnp.zeros_like(dkv_ref)
            dsink_ref[...] = jnp.zeros_like(dsink_ref)

        def chunk(ref, h):
            return ref[:, (h // 2) * LANES:(h // 2 + 1) * LANES]

        def band(part, h):
            return _band(kvc_ref, kvp_ref, part, h // 2 // chunks_per_group, h % 2, W)

        def stack(per_head):
            return jnp.concatenate([per_head(h) for h in range(n_heads)], axis=0)

        mask = jnp.tile(_band_mask(n), (n_heads, 1))
        s = stack(lambda h: _dot(chunk(q_ref, h), band(0, h), ((1,), (1,))))
        dp = stack(lambda h: _dot(chunk(do_ref, h), band(1, h), ((1,), (1,))))
        lse = stack(lambda h: lse_ref[:, h:h + 1])
        sink = stack(lambda h: jnp.broadcast_to(sink_ref[:, h:h + 1], (B, 1)))
        probs = jnp.exp(jnp.where(mask, s * scale, MASK_VALUE) - lse)
        delta = jnp.sum(probs * dp, axis=-1, keepdims=True)
        ds = (probs * (dp - delta) * scale).astype(BF16)
        probs = probs.astype(BF16)
        sink_term = jnp.exp(sink - lse) * delta
        dsk = [-jnp.sum(sink_term[h * B:(h + 1) * B], axis=0, keepdims=True) for h in range(n_heads)]

        dk_wide, dv_wide = [None] * n_kv, [None] * n_kv
        for c in range(n_heads // 2):
            g = c // chunks_per_group
            dq2 = None
            for h in (2 * c, 2 * c + 1):
                half = _half_mask(h % 2)
                q2, do2 = chunk(q_ref, h), chunk(do_ref, h)
                ds_h, p_h = ds[h * B:(h + 1) * B], probs[h * B:(h + 1) * B]
                part = _dot(ds_h, band(0, h), ((1,), (0,)))
                dq2 = part if dq2 is None else dq2 + part
                dk_h = _dot(ds_h, jnp.where(half, q2, jnp.zeros_like(q2)), ((0,), (0,)))
                dv_h = _dot(p_h, jnp.where(half, do2, jnp.zeros_like(do2)), ((0,), (0,)))
                dk_wide[g] = dk_h if dk_wide[g] is None else dk_wide[g] + dk_h
                dv_wide[g] = dv_h if dv_wide[g] is None else dv_wide[g] + dv_h
            dq_ref[:, c * LANES:(c + 1) * LANES] = _rope_chunk_bwd(
                dq2, c_ref[...], su_ref[...], sd_ref[...]).astype(dq_ref.dtype)

        def fold(wide):
            low = _half_mask(0)
            both = [w + pltpu.roll(w, HEAD_DIM, 1) for w in wide]
            return jnp.concatenate([jnp.where(low, both[2 * cg], both[2 * cg + 1]) for cg in range(n_kv // 2)], axis=1)

        dkv = jnp.concatenate([fold(dk_wide), fold(dv_wide)], axis=1)
        prev = pl.ds(pl.multiple_of(jnp.maximum(n - 1, 0) * B, B), B)
        cur = pl.ds(pl.multiple_of(n * B, B), B)
        dkv_ref[prev, :] += dkv[0:B, :]
        dkv_ref[cur, :] += dkv[B:, :]
        dsink_ref[...] += jnp.concatenate(dsk, axis=1)

    tab = pl.BlockSpec((B, LANES), lambda n: (n, 0))
    return _call(
        body, comm, name="attn_bwd", grid=(nb,),
        in_specs=_attn_specs(HD, W) + [pl.BlockSpec((B, HD), lambda n: (n, 0)),
                                       pl.BlockSpec((B, n_heads), lambda n: (n, 0)),
                                       pl.BlockSpec((1, n_heads), lambda n: (0, 0)), tab, tab, tab],
        out_specs=[pl.BlockSpec((B, HD), lambda n: (n, 0)), pl.BlockSpec((T, 2 * KVD), lambda n: (0, 0)),
                   pl.BlockSpec((1, n_heads), lambda n: (0, 0))],
        out_shape=[_sds((T, HD + 2 * KVD), BF16), _sds((T, 2 * KVD), F32), _sds((1, n_heads), F32)],
        sem=("arbitrary",), operands=[q, kvx, kvx, do, lse, sinks, *rope])


def dkv_finish(d_qkv, dkv, rope, HD, KVD, *, tm):
    T = dkv.shape[0]
    kv_col = HD // (2 * KVD)

    def body(alias_ref, dkv_ref, c_ref, su_ref, sd_ref, o_ref):
        del alias_ref
        dk = _rope_wide(dkv_ref[:, 0:KVD], c_ref[...], su_ref[...], sd_ref[...], _rope_chunk_bwd)
        o_ref[:, 0:KVD] = dk.astype(o_ref.dtype)
        o_ref[:, KVD:] = dkv_ref[:, KVD:].astype(o_ref.dtype)

    tab = pl.BlockSpec((tm, LANES), lambda i: (i, 0))
    return pl.pallas_call(
        body, name="dkv_finish", grid=(T // tm,),
        in_specs=[pl.BlockSpec(memory_space=pl.ANY), pl.BlockSpec((tm, 2 * KVD), lambda i: (i, 0)), tab, tab, tab],
        out_specs=pl.BlockSpec((tm, 2 * KVD), lambda i: (i, kv_col)),
        out_shape=_sds(d_qkv.shape, d_qkv.dtype), input_output_aliases={0: 0},
        compiler_params=_params(("parallel",)))(d_qkv, dkv, *rope)


def ple_bwd_elem(d_out, pp, gg, *, tm):
    T, D = d_out.shape

    def body(d_ref, pp_ref, gg_ref, dpp_ref, dgg_ref):
        d = d_ref[...]
        sg = _sigmoid(gg_ref[...].astype(F32))
        dpp_ref[...] = (d * sg).astype(dpp_ref.dtype)
        dgg_ref[...] = (d * pp_ref[...].astype(F32) * sg * (1.0 - sg)).astype(dgg_ref.dtype)

    tile = pl.BlockSpec((tm, D), lambda i: (i, 0))
    return pl.pallas_call(
        body, name="ple_bwd_elem", grid=(T // tm,), in_specs=[tile, tile, tile], out_specs=[tile, tile],
        out_shape=[_sds((T, D), BF16), _sds((T, D), BF16)], compiler_params=_params(("parallel",)))(d_out, pp, gg)


def chip_sum(name, g, p_sib, *, ta):
    _, a, b = g.shape
    assert a % ta == 0

    def body(core_ref, g_ref, p_ref, o_ref):
        del core_ref
        o_ref[...] = (g_ref[...].astype(F32) + p_ref[...].astype(F32)).astype(o_ref.dtype)

    blk = pl.BlockSpec((None, ta, b), lambda q, i, core: (q, i, 0))
    my_core = lax.axis_index("c").astype(jnp.int32).reshape(1)
    return pl.pallas_call(
        body, name=name, out_shape=_sds((N_CHIPS, a, b), g.dtype),
        grid_spec=pltpu.PrefetchScalarGridSpec(
            num_scalar_prefetch=1, grid=(N_CHIPS, a // ta),
            in_specs=[pl.BlockSpec((None, None, ta, b), lambda q, i, core: (q, core[0], i, 0)), blk],
            out_specs=blk),
        compiler_params=_params(("arbitrary", "arbitrary")))(my_core, g.reshape(N_CHIPS, 2, a, b), p_sib)


def adamw(name, recvs, w, m, v, *, ta):
    L, a, b = w.shape
    n_terms = recvs[0].shape[0]
    assert a % ta == 0 and len(recvs) == L
    c1 = 1.0 - ADAM_B1 ** ADAM_STEP
    c2 = 1.0 - ADAM_B2 ** ADAM_STEP

    def body(*refs):
        r_refs = refs[:L]
        w_ref, m_ref, v_ref, g_ref, d_ref, nm_ref, nv_ref = refs[L:]
        layer = pl.program_id(0)
        for l in range(L):
            @pl.when(layer == l)
            def _(r_ref=r_refs[l]):
                g = r_ref[0].astype(F32)
                for s in range(1, n_terms):
                    g = g + r_ref[s].astype(F32)
                nm = ADAM_B1 * m_ref[...] + (1.0 - ADAM_B1) * g
                nv = ADAM_B2 * v_ref[...] + (1.0 - ADAM_B2) * jnp.square(g)
                m_hat = nm / c1
                v_hat = nv / c2
                g_ref[...] = g
                d_ref[...] = -ADAM_LR * (m_hat / (jnp.sqrt(v_hat) + ADAM_EPS) + ADAM_WD * w_ref[...])
                nm_ref[...] = nm
                nv_ref[...] = nv

    blk = pl.BlockSpec((None, ta, b), lambda l, i: (l, i, 0))
    out = _sds((L, a, b), F32)
    r_specs = [pl.BlockSpec((n_terms, ta, b), lambda l, i, ll=ll: (0, jnp.where(l == ll, i, 0), 0)) for ll in range(L)]
    return pl.pallas_call(
        body, name=name, grid=(L, a // ta), in_specs=r_specs + [blk, blk, blk],
        out_specs=[blk, blk, blk, blk], out_shape=[out, out, out, out],
        compiler_params=_params(("arbitrary", "arbitrary")))(*recvs, w, m, v)


def _pack_rows(parts):
    out = []
    for arr, rows in parts:
        arr = arr.reshape(-1, LANES).astype(F32)
        out.append(jnp.pad(arr, ((0, rows - arr.shape[0]), (0, 0))))
    return jnp.concatenate(out, axis=0)


def _small_rows(a, prefix):
    return _pack_rows([(a[prefix + name], rows) for name, rows in SMALL_SHARDED])


def _unpack_small(packed, a):
    out, r0 = {}, 0
    for name, rows in SMALL_SHARDED:
        shape = a[name].shape
        used = math.prod(shape) // LANES
        out[name] = packed[r0:r0 + used].reshape(shape)
        r0 += rows
    return out


def _rep_rows(a, prefix):
    parts = []
    for name in REPLICATED:
        arr = a[prefix + name]
        if arr.size % LANES:
            arr = jnp.pad(arr.reshape(1, -1), ((0, 0), (0, LANES - arr.size % LANES)))
        rows = -(-arr.size // LANES)
        parts.append((arr, -(-rows // SUBLANES) * SUBLANES))
    return _pack_rows(parts)


def _unpack_rep(packed, a):
    out, r0 = {}, 0
    for name in REPLICATED:
        shape = a[name].shape
        size = math.prod(shape)
        rows = -(-size // LANES)
        out[name] = packed[r0:r0 + rows].reshape(-1)[:size].reshape(shape)
        r0 += -(-rows // SUBLANES) * SUBLANES
    return out


def _step(a):
    x = a['x'][0]
    T, D = x.shape
    tgt = a['loss_target'][0]
    p_in = [a['p'][i, 0] for i in range(DEPTH)]
    PLE = p_in[0].shape[1]
    n_heads = a['attn_sinks'].shape[1]
    HD = n_heads * HEAD_DIM
    KVD = a['kv_w_k'].shape[1]
    n_kv = KVD // HEAD_DIM
    F = a['mlp_w_down'].shape[1] * N_DEV
    tm = min(TOKEN_TILE, T)
    tm2 = min(2 * TOKEN_TILE, T)
    tmc = min(TOKEN_TILE, T)
    tw = 512
    alpha = DEEPNORM_ALPHA
    xb = x.astype(BF16)
    p_b = [p.astype(BF16) for p in p_in]

    def shard3(w):
        return w.reshape((1,) + w.shape) if w.ndim == 2 else w

    def gather(*specs):
        return _GatherJob([shard3(a[nm])[l].astype(BF16) for nm, l in specs])

    (W_in, small_full), = comm_only("gather_first",
                                    [_GatherJob([a['conv_w_in'][0].astype(BF16), _small_rows(a, '')])])
    r0, small = 0, {}
    for name, rows in SMALL_SHARDED:
        small[name] = small_full[:, r0:r0 + rows]
        r0 += rows
    b_in = small['conv_b_in'][:, 0:2 * D // N_DEV // LANES].reshape(1, 2 * D)
    w_dw = jnp.transpose(small['conv_w_dw'], (1, 0, 2)).reshape(CONV_HALO, D)
    b_dw, cln_g, cln_b, b_out = (small[nm][:, 0].reshape(1, D) for nm in
                                 ('conv_b_dw', 'conv_ln_g', 'conv_ln_b', 'conv_b_out'))
    W_up, W_down, W_proj, W_gate = {}, {}, {}, {}
    mix_g, mix_b, mlp_g, mlp_b = a['mix_ln_g'], a['mix_ln_b'], a['mlp_ln_g'], a['mlp_ln_b']
    rope = _rope_tables(T)

    def set_ple_weights(li, g_proj, g_gate):
        W_proj[li] = jnp.transpose(g_proj, (1, 0, 2)).reshape(1, PLE, D)
        W_gate[li] = g_gate.reshape(1, D, D)

    def row(v, i):
        return v[i:i + 1]

    def res_ln_epi(coef):
        def epi(accs, ex, out, i):
            acc = accs[0] if isinstance(accs, list) else accs
            n_ex = len(ex)
            res_ref, g_ref, b_ref = ex[n_ex - 3], ex[n_ex - 2], ex[n_ex - 1]
            z = coef * res_ref[...] + acc
            if n_ex == 4:
                z = z + ex[0][...]
            out[0][...] = z
            xo = _ln_fwd(z, g_ref[...], b_ref[...])
            out[1][...] = xo
            out[2][...] = xo.astype(BF16)
        return epi

    res_ln_outs = [(_sds((T, D), F32), 'tile'), (_sds((T, D), F32), 'tile'), (_sds((T, D), BF16), 'tile')]

    def mlp_fwd(li, xin, xin_b, up_comm=None, down_comm=None):
        def up_epi(accs, ex, out, i):
            u = accs[0]
            out[0][...] = u.astype(BF16)
            out[1][...] = jnp.square(jnp.maximum(u, 0.0)).astype(BF16)
        res = mm_nn(f"mlp_up_{li}", [(xin_b, W_up[li], 0)], [], [(_sds((T, F), BF16), 'tile')] * 2, up_epi,
                    tm=tm2, tn=min(1024, F), comm=up_comm)
        (u, act), got_up = res if up_comm is not None else (res, ())
        if li not in W_down:
            W_down[li] = got_up[0].reshape(1, F, D)
        res = mm_nn(f"mlp_down_{li}", [(act, W_down[li], 0)],
                    [(xin, 'tile'), (row(mlp_g, li), 'row'), (row(mlp_b, li), 'row')],
                    res_ln_outs, res_ln_epi(alpha), tm=tm, tn=D, tk=F, comm=down_comm)
        (z, xo, xo_b), got_down = res if down_comm is not None else (res, ())
        return u, act, z, xo, xo_b, got_up, got_down

    def ple_fwd(li, xin, xin_b, with_loss):
        def epi(accs, ex, out, i):
            pp, gg = accs
            xo = ex[0][...] + pp * _sigmoid(gg)
            out[1][...] = pp.astype(BF16)
            out[2][...] = gg.astype(BF16)
            if with_loss:
                err = xo - ex[1][...]
                out[0][...] = err * (1.0 / D)
                _init_or_add(out[3], i, jnp.sum(err * err, axis=0, keepdims=True) * (0.5 / D))
            else:
                out[0][...] = xo
                out[3][...] = xo.astype(BF16)
        extras = [(xin, 'tile')] + ([(tgt, 'tile')] if with_loss else [])
        outs = [(_sds((T, D), F32), 'tile'), (_sds((T, D), BF16), 'tile'), (_sds((T, D), BF16), 'tile')]
        outs.append((_sds((1, D), F32), 'rowacc') if with_loss else (_sds((T, D), BF16), 'tile'))
        return mm_nn(f"ple_{li}", [(p_b[li], W_proj[li], 0), (xin_b, W_gate[li], 0)], extras, outs, epi, tm=tm, tn=D)

    assert D // N_DEV == LANES
    (g0, ha0, hg0), got = _glu(xb, W_in, b_in, T, D, tm,
                               gather(('conv_w_out', 0), ('ple_w_proj', 0), ('ple_w_gate', 0)))
    W_out = got[0].reshape(1, D, D)
    set_ple_weights(0, got[1], got[2])
    (c0, s0), (W_up[0],) = dwconv_fwd(g0, w_dw, b_dw, cln_g, cln_b, tm=tmc, comm=gather(('mlp_w_up', 0)))
    z1, x1, x1b = mm_nn("conv_out", [(s0, W_out, 0)],
                        [(b_out, 'row'), (x, 'tile'), (row(mix_g, 0), 'row'), (row(mix_b, 0), 'row')],
                        res_ln_outs, res_ln_epi(alpha), tm=tm, tn=D)
    u0, act0, z2, x2, x2b, _, got = mlp_fwd(
        0, x1, x1b, up_comm=gather(('mlp_w_down', 0)),
        down_comm=gather(('attn_w_q', 0), ('kv_w_k', 0), ('kv_w_v', 0), ('attn_w_o', 0),
                         ('ple_w_proj', 1), ('ple_w_gate', 1)))
    W_qkv = jnp.concatenate([got[0].reshape(D, HD), got[1].reshape(D, KVD), got[2].reshape(D, KVD)], axis=1)[None]
    W_o = got[3].reshape(1, HD, D)
    set_ple_weights(1, got[4], got[5])
    x3, pp0, gg0, x3b = ple_fwd(0, x2, x2b, False)

    def qkv_epi(accs, ex, out, i):
        t = accs[0]
        c, su, sd = ex[0][...], ex[1][...], ex[2][...]
        out[0][...] = _rope_wide(t[:, 0:HD], c, su, sd, _rope_chunk).astype(BF16)
        widen_kv(_rope_wide(t[:, HD:HD + KVD], c, su, sd, _rope_chunk), t[:, HD + KVD:], out[1], n_kv)
    NQ = HD + 2 * KVD
    q1, kvx1 = mm_nn("qkv_rope", [(x3b, W_qkv, 0)], [(t, 'tab') for t in rope],
                     [(_sds((T, HD), BF16), 'rows'), (_sds((T, 4 * n_kv * LANES), BF16), 'rows')], qkv_epi,
                     tm=tm, tn=NQ)
    (o1, lse1), (W_up[1],) = attn_fwd(q1, kvx1, a['attn_sinks'], n_heads, n_kv, comm=gather(('mlp_w_up', 1)))
    z3, x4, x4b = mm_nn("attn_out", [(o1, W_o, 0)], [(x3, 'tile'), (row(mix_g, 1), 'row'), (row(mix_b, 1), 'row')],
                        res_ln_outs, res_ln_epi(alpha), tm=tm, tn=D)
    u1, act1, z4, x5, x5b, _, _ = mlp_fwd(1, x4, x4b, up_comm=gather(('mlp_w_down', 1)))
    dy, pp1, gg1, loss_row = ple_fwd(1, x5, x5b, True)
    loss_local = jnp.sum(loss_row)

    grads = {}

    def ln_bwd_epi(coef, with_colsum):
        def epi(acc, ex, out, i):
            d_x = acc + coef * ex[0][...]
            dz, dg, db = _ln_bwd(d_x, ex[1][...], ex[2][...])
            out[0][...] = dz
            out[1][...] = dz.astype(BF16)
            _init_or_add(out[2], i, dg)
            _init_or_add(out[3], i, db)
            if with_colsum:
                _init_or_add(out[4], i, jnp.sum(dz, axis=0, keepdims=True))
        return epi

    def ln_bwd_outs(with_colsum):
        outs = [(_sds((T, D), F32), 'tile'), (_sds((T, D), BF16), 'tile'), (_sds((1, D), F32), 'rowacc'),
                (_sds((1, D), F32), 'rowacc')]
        return outs + ([(_sds((1, D), F32), 'rowacc')] if with_colsum else [])

    def ple_bwd(li, d_out, xin, pp, gg, z_mlp, pair_specs=None):
        d_pp, d_gg = ple_bwd_elem(d_out, pp, gg, tm=tm)
        grads[('ple_w_proj', li)] = mm_tn(f"d_ple_proj_{li}", p_b[li], d_pp, D, BF16, tm=T, tk=PLE, tn=tw)
        grads[('ple_w_gate', li)] = mm_tn(f"d_ple_gate_{li}", xin, d_gg, D, BF16, tm=T, tk=tw, tn=tw)
        job = pair_stage(*pair_specs) if pair_specs else None
        res = mm_nt(f"ple_dx_{li}", d_gg, W_gate[li], [(d_out, 'tile'), (z_mlp, 'tile'), (row(mlp_g, li), 'row')],
                    ln_bwd_outs(False), ln_bwd_epi(1.0, False), tm=tm, tko=D, tc=D, comm=job)
        (dz, dzb, dg, db), got = res if job is not None else (res, None)
        grads[('mlp_ln_g', li)], grads[('mlp_ln_b', li)] = dg, db
        return dz, dzb, (chip_stage(job, got) if job is not None else None)

    recv = {}
    wqkv_cols = {'attn_w_q': (0, HD), 'kv_w_k': (HD, HD + KVD), 'kv_w_v': (HD + KVD, NQ)}

    def piece(name, li):
        if name == 'conv_w_in':
            return grads['conv_w_in']
        if name == 'mlp_w_up':
            return grads[('mlp_w_up', li)]
        if name == 'ple_w_proj':
            return jnp.transpose(grads[('ple_w_proj', li)][0].reshape(PLE, N_DEV, D // N_DEV), (1, 0, 2))
        if name in wqkv_cols:
            g = grads['w_qkv'][:, wqkv_cols[name][0]:wqkv_cols[name][1]]
        else:
            g = grads[name] if name in grads else grads[(name, li)]
            g = g[0]
        return g.reshape(N_DEV, g.shape[0] // N_DEV, g.shape[1])

    def pair_stage(*specs):
        job = _PairJob([piece(nm, li) for nm, li in specs])
        job.specs = specs
        return job

    def chip_stage(pair_job, got):
        sums = [chip_sum(f"chip_sum_{nm}_{li}", mine, theirs, ta=min(512, mine.shape[1]))
                for (nm, li), mine, theirs in zip(pair_job.specs, pair_job.sources, got)]
        job = _ChipJob(sums)
        job.specs = pair_job.specs
        return job

    def keep(job, got):
        for spec, r in zip(job.specs, got):
            recv[spec] = r

    def mlp_bwd(li, dz, dzb, xin, u, act, z_mix, with_colsum, pair_specs, chip_job, down_job=None):
        res = mm_tn(f"d_mlp_down_{li}", act, dzb, D, BF16, tm=T, tk=tw, tn=tw, comm=down_job)
        if down_job is not None:
            res, got = res
            keep(down_job, got)
        grads[('mlp_w_down', li)] = res

        def du_epi(acc, ex, out, i):
            out[0][...] = (acc * (2.0 * jnp.maximum(ex[0][...].astype(F32), 0.0))).astype(BF16)
        pair = pair_stage(*pair_specs)
        (du,), got = mm_nt(f"mlp_du_{li}", dzb, W_down[li], [(u, 'tile')], [(_sds((T, F), BF16), 'tile')], du_epi,
                           tm=tm2, tko=min(1024, F), tc=D, comm=pair)
        pending = chip_stage(pair, got)
        grads[('mlp_w_up', li)] = mm_tn(f"d_mlp_up_{li}", xin, du, F // N_DEV, BF16, tm=T, tk=tw, tn=tw)
        if chip_job is None:
            chip_job, pending = pending, None
        res, got = mm_nt(f"mlp_dx_{li}", du, W_up[li], [(dz, 'tile'), (z_mix, 'tile'), (row(mix_g, li), 'row')],
                         ln_bwd_outs(with_colsum), ln_bwd_epi(alpha, with_colsum), tm=tm, tko=D, tc=F,
                         comm=chip_job)
        keep(chip_job, got)
        grads[('mix_ln_g', li)], grads[('mix_ln_b', li)] = res[2], res[3]
        return res, pending

    dz4, dz4b, _ = ple_bwd(1, dy, x5b, pp1, gg1, z4)
    (dz3, dz3b, _, _), _ = mlp_bwd(1, dz4, dz4b, x4b, u1, act1, z3, False,
                                   [('mlp_w_down', 1), ('ple_w_gate', 1), ('ple_w_proj', 1)], None)
    grads['attn_w_o'] = mm_tn("d_attn_wo", o1, dz3b, D, BF16, tm=T, tk=tw, tn=tw)

    def do_epi(acc, ex, out, i):
        out[0][...] = acc.astype(BF16)
    pair = pair_stage(('mlp_w_up', 1), ('attn_w_o', 0))
    (do1,), got = mm_nt("attn_do", dz3b, W_o, [], [(_sds((T, HD), BF16), 'tile')], do_epi, tm=tm, tko=HD, tc=D,
                        comm=pair)
    job = chip_stage(pair, got)
    (d_qkv, dkv, d_sinks), got = attn_bwd(q1, kvx1, do1, lse1, a['attn_sinks'], rope, n_heads, n_kv, comm=job)
    keep(job, got)
    d_qkv = dkv_finish(d_qkv, dkv, rope, HD, KVD, tm=tm)
    grads['w_qkv'] = mm_tn("d_wqkv", x3b, d_qkv, NQ, BF16, tm=T, tk=tw, tn=2 * KVD)[0]

    def dx3_epi(acc, ex, out, i):
        out[0][...] = acc + alpha * ex[0][...]
    dx3, = mm_nt("attn_dx", d_qkv, W_qkv, [(dz3, 'tile')], [(_sds((T, D), F32), 'tile')], dx3_epi,
                 tm=tm, tko=D, tc=NQ)

    dz2, dz2b, job = ple_bwd(0, dx3, x2b, pp0, gg0, z2,
                             [('attn_w_q', 0), ('kv_w_k', 0), ('kv_w_v', 0), ('ple_w_gate', 0), ('ple_w_proj', 0)])
    (dz1, dz1b, _, _, db_out), _ = mlp_bwd(0, dz2, dz2b, x1b, u0, act0, z1, True, [('mlp_w_down', 0)], None, job)
    grads['conv_w_out'] = mm_tn("d_conv_wout", s0, dz1b, D, BF16, tm=T, tk=tw, tn=tw)

    def ds_epi(acc, ex, out, i):
        n = _ln_fwd(ex[0][...], ex[1][...], ex[2][...])
        sg = _sigmoid(n)
        dn = acc * (sg * (1.0 + n * (1.0 - sg)))
        dc, dg, db = _ln_bwd(dn, ex[0][...], ex[1][...])
        out[0][...] = dc
        _init_or_add(out[1], i, dg)
        _init_or_add(out[2], i, db)
    pair = pair_stage(('mlp_w_up', 0), ('conv_w_out', 0))
    (dc0, d_cln_g, d_cln_b), got = mm_nt("conv_ds", dz1b, W_out, [(c0, 'tile'), (cln_g, 'row'), (cln_b, 'row')],
                                         [(_sds((T, D), F32), 'tile'), (_sds((1, D), F32), 'rowacc'),
                                          (_sds((1, D), F32), 'rowacc')], ds_epi, tm=tm, tko=D, tc=D, comm=pair)
    job = chip_stage(pair, got)
    (dh0, d_wdw, d_bdw, d_bin), got = dwconv_bwd(dc0, g0, ha0, hg0, w_dw, tm=tmc, comm=job)
    keep(job, got)
    grads['conv_w_in'] = mm_tn("d_conv_win", xb, dh0, 2 * D // N_DEV, BF16, tm=T, tk=tw, tn=tw)

    def dx_epi(acc, ex, out, i):
        out[0][...] = acc + alpha * ex[0][...]
    pair = pair_stage(('conv_w_in', 0))
    (grad_x,), got = mm_nt("conv_dx", dh0, W_in, [(dz1, 'tile')], [(_sds((T, D), F32), 'tile')], dx_epi,
                           tm=tm, tko=D, tc=D, comm=pair)
    last_chip = chip_stage(pair, got)

    def own_rows(vec, rows_used, rows):
        arr = vec.reshape(N_DEV, rows_used, LANES)
        return jnp.pad(arr, ((0, 0), (0, rows - rows_used), (0, 0)))
    dwdw_dev = jnp.transpose(d_wdw.reshape(CONV_HALO, N_DEV, D // N_DEV), (1, 0, 2))
    lane_rows = D // N_DEV // LANES
    small_grad = jnp.concatenate([
        own_rows(d_bin, 2 * lane_rows, 8), dwdw_dev if lane_rows == 1 else dwdw_dev.reshape(N_DEV, -1, LANES),
        own_rows(d_bdw, lane_rows, 8), own_rows(d_cln_g, lane_rows, 8), own_rows(d_cln_b, lane_rows, 8),
        own_rows(db_out, lane_rows, 8)], axis=1)
    n_small = small_grad.shape[1]

    rep_local = {'mix_ln_g': jnp.concatenate([grads[('mix_ln_g', li)] for li in range(DEPTH)], axis=0),
                 'mix_ln_b': jnp.concatenate([grads[('mix_ln_b', li)] for li in range(DEPTH)], axis=0),
                 'mlp_ln_g': jnp.concatenate([grads[('mlp_ln_g', li)] for li in range(DEPTH)], axis=0),
                 'mlp_ln_b': jnp.concatenate([grads[('mlp_ln_b', li)] for li in range(DEPTH)], axis=0),
                 'attn_sinks': d_sinks}
    rep_grad = _rep_rows(rep_local, '')
    n_rep = rep_grad.shape[0]
    got, (recv_small, recv_rep) = comm_only("exchange_last", [last_chip, _DirectJob(
        [small_grad, jnp.broadcast_to(rep_grad[None], (N_DEV, n_rep, LANES))])])
    keep(last_chip, got)

    result = {}
    kinds = ('grad', 'delta', 'new_m', 'new_v')
    w, m, v = (_small_rows(a, pre)[None] for pre in ('', 'm_', 'v_'))
    for kind, arr in zip(kinds, adamw("adamw_small", [recv_small], w, m, v, ta=n_small)):
        for pname, val in _unpack_small(arr[0], a).items():
            result[(kind, pname)] = val
    w, m, v = (_rep_rows(a, pre)[None] for pre in ('', 'm_', 'v_'))
    for kind, arr in zip(kinds, adamw("adamw_rep", [recv_rep], w, m, v, ta=n_rep)):
        for pname, val in _unpack_rep(arr[0], a).items():
            result[(kind, pname)] = val
    for name in BIG_WEIGHTS:
        w, m, v = (shard3(a[pre + name]) for pre in ('', 'm_', 'v_'))
        recvs = [recv[(name, li)] for li in range(w.shape[0])]
        for kind, arr in zip(kinds, adamw("adamw_" + name, recvs, w, m, v, ta=min(256, w.shape[1]))):
            result[(kind, name)] = arr.reshape(a[name].shape)

    loss = lax.psum(loss_local, ("x", "y", "c"))
    out = [loss, grad_x[None]]
    for kind in ('grad', 'delta', 'new_m', 'new_v'):
        out += [result[(kind, name)] for name in WEIGHT_NAMES]
    return tuple(out)


def _glu(x, W_in, b_in, T, D, tm, comm=None):
    n = W_in.shape[2]
    q = 2 if D // n % 2 == 0 else 1
    nt = D // (q * n)
    tn = q * n

    def body(x_ref, wa_ref, wg_ref, ba_ref, bg_ref, g_ref, ha_ref, hg_ref):
        xb = x_ref[...]
        ha = jnp.concatenate([_dot(xb, wa_ref[s], ((1,), (0,))) for s in range(q)], axis=1) + ba_ref[...]
        hg = jnp.concatenate([_dot(xb, wg_ref[s], ((1,), (0,))) for s in range(q)], axis=1) + bg_ref[...]
        g_ref[...] = ha * _sigmoid(hg)
        ha_ref[...] = ha.astype(ha_ref.dtype)
        hg_ref[...] = hg.astype(hg_ref.dtype)

    return _call(
        body, comm, name="conv_in_glu", grid=(T // tm, nt),
        in_specs=[pl.BlockSpec((tm, D), lambda i, j: (i, 0)),
                  pl.BlockSpec((q, D, n), lambda i, j: (j, 0, 0)),
                  pl.BlockSpec((q, D, n), lambda i, j: (j + nt, 0, 0)),
                  pl.BlockSpec((1, tn), lambda i, j: (0, j)), pl.BlockSpec((1, tn), lambda i, j: (0, j + nt))],
        out_specs=[pl.BlockSpec((tm, tn), lambda i, j: (i, j))] * 3,
        out_shape=[_sds((T, D), F32), _sds((T, D), BF16), _sds((T, D), BF16)],
        sem=("parallel", "parallel"), operands=[x, W_in, W_in, b_in, b_in])


def kernel(x, p, conv_w_in, conv_b_in, conv_w_dw, conv_b_dw, conv_ln_g, conv_ln_b, conv_w_out, conv_b_out, kv_w_k, kv_w_v, attn_w_q, attn_sinks, attn_w_o, mix_ln_g, mix_ln_b, mlp_w_up, mlp_w_down, mlp_ln_g, mlp_ln_b, ple_w_proj, ple_w_gate, loss_target, m_conv_w_in, m_conv_b_in, m_conv_w_dw, m_conv_b_dw, m_conv_ln_g, m_conv_ln_b, m_conv_w_out, m_conv_b_out, m_kv_w_k, m_kv_w_v, m_attn_w_q, m_attn_sinks, m_attn_w_o, m_mix_ln_g, m_mix_ln_b, m_mlp_w_up, m_mlp_w_down, m_mlp_ln_g, m_mlp_ln_b, m_ple_w_proj, m_ple_w_gate, v_conv_w_in, v_conv_b_in, v_conv_w_dw, v_conv_b_dw, v_conv_ln_g, v_conv_ln_b, v_conv_w_out, v_conv_b_out, v_kv_w_k, v_kv_w_v, v_attn_w_q, v_attn_sinks, v_attn_w_o, v_mix_ln_g, v_mix_ln_b, v_mlp_w_up, v_mlp_w_down, v_mlp_ln_g, v_mlp_ln_b, v_ple_w_proj, v_ple_w_gate):
    return _step(dict(locals()))
```

```python
import functools
import math

import jax
import jax.numpy as jnp
from jax import lax
from jax.experimental import pallas as pl
from jax.experimental.pallas import tpu as pltpu

F32 = jnp.float32
BF16 = jnp.bfloat16

N_DEV = 8
HEAD_DIM = 64
ROPE_DIM = HEAD_DIM // 4
ROPE_HALF = ROPE_DIM // 2
ROPE_THETA = 500000.0
ATT_BLOCK = 128
CONV_WIDTH = 31
CONV_HALO = 32
CONV_ROWS = 128
LN_EPS = 1e-5
DEPTH = 2
DEEPNORM_ALPHA = (2 * DEPTH) ** 0.25
MASK_VALUE = -1e30

ADAM_LR = 0.001
ADAM_B1 = 0.9
ADAM_B2 = 0.999
ADAM_EPS = 1e-08
ADAM_WD = 0.01
ADAM_STEP = 10

LANES = 128
SUBLANES = 8
VMEM_LIMIT_BYTES = 52 * 1024 * 1024
TOKEN_TILE = 512
MESH_ID = pl.DeviceIdType.MESH
FORWARD_AT_TENTHS = 6

WEIGHT_NAMES = ['conv_w_in', 'conv_b_in', 'conv_w_dw', 'conv_b_dw', 'conv_ln_g', 'conv_ln_b', 'conv_w_out',
                'conv_b_out', 'kv_w_k', 'kv_w_v', 'attn_w_q', 'attn_sinks', 'attn_w_o', 'mix_ln_g', 'mix_ln_b',
                'mlp_w_up', 'mlp_w_down', 'mlp_ln_g', 'mlp_ln_b', 'ple_w_proj', 'ple_w_gate']
BIG_WEIGHTS = ['conv_w_in', 'conv_w_out', 'kv_w_k', 'kv_w_v', 'attn_w_q', 'attn_w_o', 'mlp_w_up', 'mlp_w_down',
               'ple_w_proj', 'ple_w_gate']
SMALL_SHARDED = [('conv_b_in', 8), ('conv_w_dw', 32), ('conv_b_dw', 8), ('conv_ln_g', 8), ('conv_ln_b', 8),
                 ('conv_b_out', 8)]
REPLICATED = ['mix_ln_g', 'mix_ln_b', 'mlp_ln_g', 'mlp_ln_b', 'attn_sinks']


def _params(sem):
    return pltpu.CompilerParams(dimension_semantics=sem, vmem_limit_bytes=VMEM_LIMIT_BYTES)


def _sds(shape, dtype):
    return jax.ShapeDtypeStruct(shape, dtype)


def _my_place():
    x, y, c = lax.axis_index("x"), lax.axis_index("y"), lax.axis_index("c")
    return x, y, c, 4 * x + 2 * y + c


def _peers(x, y, c):
    out = []
    for dx in (0, 1):
        for dy in (0, 1):
            for dc in (0, 1):
                if dx or dy or dc:
                    px, py, pc = x ^ dx, y ^ dy, c ^ dc
                    out.append(((px, py, pc), 4 * px + 2 * py + pc))
    return out


N_CHIPS = 4


def _other_chips(x, y):
    return [((x ^ dx, y ^ dy), 2 * (x ^ dx) + (y ^ dy)) for dx, dy in ((1, 0), (0, 1), (1, 1))]


def _remote(src, dst, send, recv, to):
    return pltpu.make_async_remote_copy(src_ref=src, dst_ref=dst, send_sem=send, recv_sem=recv, device_id=to,
                                        device_id_type=MESH_ID)


def _wait_slabs(buf, count, send, recv, me, sent=True, received=True):
    part = buf.at[pl.ds(0, count)]
    cp = _remote(part, part, send, recv, me)
    if sent:
        cp.wait_send()
    if received:
        cp.wait_recv()


class _DirectJob:
    n_sems = 3

    def __init__(self, items):
        self.sources = list(items)
        self.dests = [_sds(it.shape, it.dtype) for it in items]
        self.n = len(items)

    def start(self, src, dst, send, recv, loc):
        x, y, c, me = _my_place()
        for k in range(self.n):
            here = dst[k].at[pl.ds(me, 1)]
            pltpu.make_async_copy(src[k].at[pl.ds(me, 1)], here, loc.at[k]).start()
            for peer, idx in _peers(x, y, c):
                _remote(src[k].at[pl.ds(idx, 1)], here, send.at[k], recv.at[k], peer).start()

    def forward(self, src, dst, send, recv, loc):
        pass

    def finish(self, src, dst, send, recv, loc):
        x, y, c, me = _my_place()
        for k in range(self.n):
            pltpu.make_async_copy(src[k].at[pl.ds(me, 1)], dst[k].at[pl.ds(me, 1)], loc.at[k]).wait()
            _wait_slabs(dst[k], N_DEV - 1, send.at[k], recv.at[k], (x, y, c))


class _GatherJob:
    n_sems = 5

    def __init__(self, items):
        self.sources = [it.reshape((1,) + it.shape) for it in items]
        self.dests = [_sds((N_DEV,) + it.shape, it.dtype) for it in items]
        self.n = len(items)

    def start(self, src, dst, send_ici, recv_ici, send_d2d, recv_d2d, loc):
        x, y, c, me = _my_place()
        for k in range(self.n):
            here = dst[k].at[pl.ds(me, 1)]
            pltpu.make_async_copy(src[k], here, loc.at[k]).start()
            _remote(src[k], here, send_d2d.at[k], recv_d2d.at[k], (x, y, 1 - c)).start()
            for (px, py), _ in _other_chips(x, y):
                _remote(src[k], here, send_ici.at[k], recv_ici.at[k], (px, py, c)).start()

    def forward(self, src, dst, send_ici, recv_ici, send_d2d, recv_d2d, loc):
        x, y, c, me = _my_place()
        for k in range(self.n):
            _wait_slabs(dst[k], N_CHIPS - 1, send_ici.at[k], recv_ici.at[k], (x, y, c), sent=False)
            for _, q in _other_chips(x, y):
                rows = dst[k].at[pl.ds(2 * q + c, 1)]
                _remote(rows, rows, send_d2d.at[k], recv_d2d.at[k], (x, y, 1 - c)).start()

    def finish(self, src, dst, send_ici, recv_ici, send_d2d, recv_d2d, loc):
        x, y, c, me = _my_place()
        for k in range(self.n):
            pltpu.make_async_copy(src[k], dst[k].at[pl.ds(me, 1)], loc.at[k]).wait()
            _wait_slabs(dst[k], N_CHIPS - 1, send_ici.at[k], recv_ici.at[k], (x, y, c), received=False)
            _wait_slabs(dst[k], N_CHIPS, send_d2d.at[k], recv_d2d.at[k], (x, y, c))


class _PairJob:
    n_sems = 2

    def __init__(self, items):
        self.sources = list(items)
        self.dests = [_sds((N_CHIPS,) + it.shape[1:], it.dtype) for it in items]
        self.n = len(items)

    def start(self, src, dst, send, recv):
        x, y, c, me = _my_place()
        for k in range(self.n):
            for q in range(N_CHIPS):
                _remote(src[k].at[pl.ds(2 * q + 1 - c, 1)], dst[k].at[pl.ds(q, 1)], send.at[k], recv.at[k],
                        (x, y, 1 - c)).start()

    def forward(self, src, dst, send, recv):
        pass

    def finish(self, src, dst, send, recv):
        x, y, c, me = _my_place()
        for k in range(self.n):
            _wait_slabs(dst[k], N_CHIPS, send.at[k], recv.at[k], (x, y, c))


class _ChipJob:
    n_sems = 3

    def __init__(self, items):
        self.sources = list(items)
        self.dests = [_sds(it.shape, it.dtype) for it in items]
        self.n = len(items)

    def start(self, src, dst, send, recv, loc):
        x, y, c, me = _my_place()
        mine = 2 * x + y
        for k in range(self.n):
            here = dst[k].at[pl.ds(mine, 1)]
            pltpu.make_async_copy(src[k].at[pl.ds(mine, 1)], here, loc.at[k]).start()
            for (px, py), q in _other_chips(x, y):
                _remote(src[k].at[pl.ds(q, 1)], here, send.at[k], recv.at[k], (px, py, c)).start()

    def forward(self, src, dst, send, recv, loc):
        pass

    def finish(self, src, dst, send, recv, loc):
        x, y, c, me = _my_place()
        mine = 2 * x + y
        for k in range(self.n):
            pltpu.make_async_copy(src[k].at[pl.ds(mine, 1)], dst[k].at[pl.ds(mine, 1)], loc.at[k]).wait()
            _wait_slabs(dst[k], N_CHIPS - 1, send.at[k], recv.at[k], (x, y, c))


def _job_sems(job):
    return [pltpu.SemaphoreType.DMA((job.n,))] * job.n_sems


def _call(body, comm, *, name, grid, in_specs, out_specs, out_shape, operands, sem, scratch_shapes=(), aliases=None):
    single = not isinstance(out_shape, (list, tuple))
    out_shape = [out_shape] if single else list(out_shape)
    out_specs = [out_specs] if single else list(out_specs)
    if comm is None:
        res = pl.pallas_call(body, name=name, grid=grid, in_specs=list(in_specs), out_specs=out_specs,
                             out_shape=out_shape, scratch_shapes=list(scratch_shapes),
                             input_output_aliases=aliases or {}, compiler_params=_params(sem))(*operands)
        return res[0] if single else res
    n_in, n_out, n_scr, n_c = len(in_specs), len(out_shape), len(scratch_shapes), comm.n
    any_spec = pl.BlockSpec(memory_space=pl.ANY)
    steps = math.prod(grid)
    mid = min(steps - 1, (steps * FORWARD_AT_TENTHS) // 10)

    def hosted(*refs):
        ins, c_src = refs[:n_in], refs[n_in:n_in + n_c]
        outs = refs[n_in + n_c:n_in + n_c + n_out]
        c_dst = refs[n_in + n_c + n_out:n_in + 2 * n_c + n_out]
        scr = refs[n_in + 2 * n_c + n_out:n_in + 2 * n_c + n_out + n_scr]
        sems = refs[n_in + 2 * n_c + n_out + n_scr:]
        step = pl.program_id(0)
        for d in range(1, len(grid)):
            step = step * grid[d] + pl.program_id(d)

        @pl.when(step == 0)
        def _():
            comm.start(c_src, c_dst, *sems)

        @pl.when(step == mid)
        def _():
            comm.forward(c_src, c_dst, *sems)

        body(*ins, *outs, *scr)

        @pl.when(step == steps - 1)
        def _():
            comm.finish(c_src, c_dst, *sems)

    res = pl.pallas_call(hosted, name=name, grid=grid, in_specs=list(in_specs) + [any_spec] * n_c,
                         out_specs=out_specs + [any_spec] * n_c, out_shape=out_shape + comm.dests,
                         scratch_shapes=list(scratch_shapes) + _job_sems(comm), input_output_aliases=aliases or {},
                         compiler_params=_params(("arbitrary",) * len(grid)))(*operands, *comm.sources)
    main = res[:n_out]
    return (main[0] if single else main), res[n_out:]


def comm_only(name, jobs):
    any_spec = pl.BlockSpec(memory_space=pl.ANY)
    n_all = sum(job.n for job in jobs)

    def body(*refs):
        srcs, dsts, sems = refs[:n_all], refs[n_all:2 * n_all], refs[2 * n_all:]
        parts, k0, s0 = [], 0, 0
        for job in jobs:
            parts.append((job, srcs[k0:k0 + job.n], dsts[k0:k0 + job.n], sems[s0:s0 + job.n_sems]))
            k0 += job.n
            s0 += job.n_sems
        for stage in ('start', 'forward', 'finish'):
            for job, src, dst, sem in parts:
                getattr(job, stage)(src, dst, *sem)

    res = pl.pallas_call(body, name=name, in_specs=[any_spec] * n_all, out_specs=[any_spec] * n_all,
                         out_shape=[d for job in jobs for d in job.dests],
                         scratch_shapes=[s for job in jobs for s in _job_sems(job)],
                         )(*[s for job in jobs for s in job.sources])
    out, k0 = [], 0
    for job in jobs:
        out.append(res[k0:k0 + job.n])
        k0 += job.n
    return out


def _dot(a, b, dims):
    if a.dtype != BF16:
        a = a.astype(BF16)
    if b.dtype != BF16:
        b = b.astype(BF16)
    return lax.dot_general(a, b, (dims, ((), ())), preferred_element_type=F32)


def _sigmoid(v):
    return 1.0 / (1.0 + jnp.exp(-v))


def _ln_stats(z):
    mu = jnp.mean(z, axis=-1, keepdims=True)
    zc = z - mu
    var = jnp.mean(zc * zc, axis=-1, keepdims=True)
    return zc * lax.rsqrt(var + LN_EPS)


def _ln_fwd(z, g, b):
    return _ln_stats(z) * g + b


def _ln_bwd(dy, z, g):
    xhat = _ln_stats(z)
    mu = jnp.mean(z, axis=-1, keepdims=True)
    zc = z - mu
    rstd = lax.rsqrt(jnp.mean(zc * zc, axis=-1, keepdims=True) + LN_EPS)
    dxh = dy * g
    m1 = jnp.mean(dxh, axis=-1, keepdims=True)
    m2 = jnp.mean(dxh * xhat, axis=-1, keepdims=True)
    dz = rstd * (dxh - m1 - xhat * m2)
    return dz, jnp.sum(dy * xhat, axis=0, keepdims=True), jnp.sum(dy, axis=0, keepdims=True)


def _extra_spec(shape, kind, tm, tn, ij):
    if kind == 'tile':
        return pl.BlockSpec((tm, tn), lambda *g: (ij(g)[0], ij(g)[1]))
    if kind in ('row', 'rowacc'):
        return pl.BlockSpec((1, tn), lambda *g: (0, ij(g)[1]))
    if kind == 'tab':
        return pl.BlockSpec((tm, LANES), lambda *g: (ij(g)[0], 0))
    if kind == 'rows':
        return pl.BlockSpec((tm, shape[1]), lambda *g: (ij(g)[0], 0))
    raise ValueError(kind)


def mm_nn(name, pairs, extras, outs, epi, *, tm, tn, tk=None, comm=None):
    M = pairs[0][0].shape[0]
    N = pairs[0][1].shape[0] * pairs[0][1].shape[2]
    n_pairs = len(pairs)
    K0 = pairs[0][0].shape[1]
    tk = K0 if tk is None else tk
    nk = K0 // tk
    assert nk == 1 or n_pairs == 1
    assert M % tm == 0 and N % tn == 0 and K0 % tk == 0
    has_rowacc = any(kind == 'rowacc' for _, kind in outs)
    assert not has_rowacc or (N == tn and nk == 1)
    in_specs, operands, slabs = [], [], []
    for a, b, off in pairs:
        K = a.shape[1]
        ktile = K if n_pairs > 1 else tk
        n = b.shape[2]
        assert b.shape[1] == K
        in_specs.append(pl.BlockSpec((tm, ktile), lambda i, j, k: (i, k)))
        if tn <= n:
            assert n % tn == 0
            r = n // tn
            in_specs.append(pl.BlockSpec((None, ktile, tn),
                                         lambda i, j, k, r=r, off=off: ((j + off) // r, k, (j + off) % r)))
            slabs.append(0)
        else:
            assert tn % n == 0
            in_specs.append(pl.BlockSpec((tn // n, ktile, n), lambda i, j, k, off=off: (j + off, k, 0)))
            slabs.append(tn // n)
        operands += [a, b]
    ij = lambda g: (g[0], g[1])
    for arr, kind in extras:
        in_specs.append(_extra_spec(arr.shape, kind, tm, tn, ij))
        operands.append(arr)
    out_specs = [_extra_spec(o.shape, kind, tm, tn, ij) for o, kind in outs]
    n_ex, n_out = len(extras), len(outs)

    def pair_dot(ab, q):
        a = ab[2 * q][...]
        if not slabs[q]:
            return _dot(a, ab[2 * q + 1][...], ((1,), (0,)))
        return jnp.concatenate([_dot(a, ab[2 * q + 1][s], ((1,), (0,))) for s in range(slabs[q])], axis=1)

    def body(*refs):
        ab = refs[:2 * n_pairs]
        ex = refs[2 * n_pairs:2 * n_pairs + n_ex]
        out = refs[2 * n_pairs + n_ex:2 * n_pairs + n_ex + n_out]
        i = pl.program_id(0)
        if nk == 1:
            accs = [pair_dot(ab, q) for q in range(n_pairs)]
            epi(accs, ex, out, i)
        else:
            acc_ref = refs[-1]
            k = pl.program_id(2)

            @pl.when(k == 0)
            def _():
                acc_ref[...] = jnp.zeros_like(acc_ref)

            acc_ref[...] += pair_dot(ab, 0)

            @pl.when(k == nk - 1)
            def _():
                epi([acc_ref[...]], ex, out, i)

    scratch = [pltpu.VMEM((tm, tn), F32)] if nk > 1 else []
    sem = ("arbitrary",) * 3 if has_rowacc else ("parallel", "parallel", "arbitrary")
    return _call(body, comm, name=name, grid=(M // tm, N // tn, nk), in_specs=in_specs, out_specs=out_specs,
                 out_shape=[o for o, _ in outs], scratch_shapes=scratch, sem=sem, operands=operands)


def mm_nt(name, a, b, extras, outs, epi, *, tm, tko, tc, comm=None):
    M, N = a.shape
    J, K, n = b.shape
    assert J * n == N and M % tm == 0 and K % tko == 0 and N % tc == 0
    nc = N // tc
    has_rowacc = any(kind == 'rowacc' for _, kind in outs)
    assert not has_rowacc or K == tko
    if tc <= n:
        assert n % tc == 0
        r = n // tc
        slabs = 0
        b_spec = pl.BlockSpec((None, tko, tc), lambda i, j, c: (c // r, j, c % r))
    else:
        assert tc % n == 0
        slabs = tc // n
        b_spec = pl.BlockSpec((slabs, tko, n), lambda i, j, c: (c, j, 0))
    in_specs = [pl.BlockSpec((tm, tc), lambda i, j, c: (i, c)), b_spec]

    def nt_dot(a_ref, b_ref):
        if not slabs:
            return _dot(a_ref[...], b_ref[...], ((1,), (1,)))
        acc = _dot(a_ref[:, 0:n], b_ref[0], ((1,), (1,)))
        for s in range(1, slabs):
            acc = acc + _dot(a_ref[:, s * n:(s + 1) * n], b_ref[s], ((1,), (1,)))
        return acc

    ij = lambda g: (g[0], g[1])
    operands = [a, b]
    for arr, kind in extras:
        in_specs.append(_extra_spec(arr.shape, kind, tm, tko, ij))
        operands.append(arr)
    out_specs = [_extra_spec(o.shape, kind, tm, tko, ij) for o, kind in outs]
    n_ex, n_out = len(extras), len(outs)

    def body(*refs):
        a_ref, b_ref = refs[:2]
        ex = refs[2:2 + n_ex]
        out = refs[2 + n_ex:2 + n_ex + n_out]
        i = pl.program_id(0)
        if nc == 1:
            epi(nt_dot(a_ref, b_ref), ex, out, i)
        else:
            acc_ref = refs[-1]
            c = pl.program_id(2)

            @pl.when(c == 0)
            def _():
                acc_ref[...] = jnp.zeros_like(acc_ref)

            acc_ref[...] += nt_dot(a_ref, b_ref)

            @pl.when(c == nc - 1)
            def _():
                epi(acc_ref[...], ex, out, i)

    scratch = [pltpu.VMEM((tm, tko), F32)] if nc > 1 else []
    sem = ("arbitrary",) * 3 if has_rowacc else ("parallel", "parallel", "arbitrary")
    return _call(body, comm, name=name, grid=(M // tm, K // tko, nc), in_specs=in_specs, out_specs=out_specs,
                 out_shape=[o for o, _ in outs], scratch_shapes=scratch, sem=sem, operands=operands)


def mm_tn(name, a, d, n, out_dtype, *, tm, tk, tn, comm=None):
    M, K = a.shape
    N = d.shape[1]
    assert d.shape[0] == M and N % n == 0 and N % tn == 0 and K % tk == 0 and M % tm == 0
    nm = M // tm
    if tn <= n:
        assert n % tn == 0
        r = n // tn
        slabs = 0
        o_spec = pl.BlockSpec((None, tk, tn), lambda kk, j, m: (j // r, kk, j % r))
    else:
        assert tn % n == 0
        slabs = tn // n
        o_spec = pl.BlockSpec((slabs, tk, n), lambda kk, j, m: (j, kk, 0))

    def write(o_ref, acc):
        if not slabs:
            o_ref[...] = acc.astype(o_ref.dtype)
        else:
            for s in range(slabs):
                o_ref[s] = acc[:, s * n:(s + 1) * n].astype(o_ref.dtype)

    def body(a_ref, d_ref, o_ref, *scratch):
        if nm == 1:
            write(o_ref, _dot(a_ref[...], d_ref[...], ((0,), (0,))))
            return
        acc_ref, = scratch
        m = pl.program_id(2)

        @pl.when(m == 0)
        def _():
            acc_ref[...] = jnp.zeros_like(acc_ref)

        acc_ref[...] += _dot(a_ref[...], d_ref[...], ((0,), (0,)))

        @pl.when(m == nm - 1)
        def _():
            write(o_ref, acc_ref[...])

    return _call(
        body, comm, name=name, grid=(K // tk, N // tn, nm),
        in_specs=[pl.BlockSpec((tm, tk), lambda kk, j, m: (m, kk)), pl.BlockSpec((tm, tn), lambda kk, j, m: (m, j))],
        out_specs=o_spec, out_shape=_sds((N // n, K, n), out_dtype),
        scratch_shapes=[pltpu.VMEM((tk, tn), F32)] if nm > 1 else [],
        sem=("parallel", "parallel", "arbitrary"), operands=[a, d])


def _init_or_add(ref, i, value):
    @pl.when(i == 0)
    def _():
        ref[...] = value

    @pl.when(i > 0)
    def _():
        ref[...] += value


def _rope_tables(T):
    pos = jnp.arange(T, dtype=F32)
    inv_freq = ROPE_THETA ** (-jnp.arange(0, ROPE_DIM, 2, dtype=F32) / ROPE_DIM)
    ang = pos[:, None] * inv_freq[None, :]
    cos, sin = jnp.cos(ang), jnp.sin(ang)
    ones = jnp.ones((T, HEAD_DIM - ROPE_DIM), F32)
    zeros = jnp.zeros((T, HEAD_DIM - ROPE_DIM), F32)
    zh = jnp.zeros((T, ROPE_HALF), F32)
    c_head = jnp.concatenate([cos, cos, ones], axis=1)
    s_up = jnp.concatenate([-sin, zh, zeros], axis=1)
    s_dn = jnp.concatenate([zh, sin, zeros], axis=1)
    rep = LANES // HEAD_DIM
    return tuple(jnp.tile(t, (1, rep)) for t in (c_head, s_up, s_dn))


def _rope_chunk(t, c, s_up, s_dn):
    return t * c + pltpu.roll(t, LANES - ROPE_HALF, 1) * s_up + pltpu.roll(t, ROPE_HALF, 1) * s_dn


def _rope_chunk_bwd(d, c, s_up, s_dn):
    return d * c + pltpu.roll(d * s_up, ROPE_HALF, 1) + pltpu.roll(d * s_dn, LANES - ROPE_HALF, 1)


def _rope_wide(t, c, s_up, s_dn, fn):
    chunks = [fn(t[:, q * LANES:(q + 1) * LANES], c, s_up, s_dn) for q in range(t.shape[1] // LANES)]
    return chunks[0] if len(chunks) == 1 else jnp.concatenate(chunks, axis=1)


def _taps_by_residue(first):
    groups = {}
    for o in range(first, first + CONV_WIDTH):
        groups.setdefault(o % SUBLANES, []).append(o)
    return sorted(groups.items())


def _shifted_taps(win_ref, span_ref, r0, res, offs, rb, ls):
    if res == 0:
        return [functools.partial(lambda o: win_ref[pl.ds(r0 + o, rb), ls], o) for o in offs]
    n = rb + offs[-1] - res
    span_ref[0:n, :] = win_ref[pl.ds(r0 + res, n), ls]
    return [functools.partial(lambda o: span_ref[pl.ds(o - res, rb), :], o) for o in offs]


def _halo_before_spec(tm, D):
    return pl.BlockSpec((CONV_HALO, D), lambda i: (jnp.maximum(i * (tm // CONV_HALO) - 1, 0), 0))


def dwconv_fwd(g, w_dw, b_dw, ln_g, ln_b, *, tm, comm=None):
    T, D = g.shape
    nl = D // LANES

    rb = min(CONV_ROWS, tm)

    def body(g_ref, gh_ref, w_ref, b_ref, lg_ref, lb_ref, c_ref, s_ref, win_ref, span_ref):
        i = pl.program_id(0)
        win_ref[0:CONV_HALO, :] = jnp.where(i > 0, gh_ref[...], 0.0)
        win_ref[CONV_HALO:, :] = g_ref[...]

        def lane_chunk(q, carry):
            ls = pl.ds(pl.multiple_of(q * LANES, LANES), LANES)
            for r0 in range(0, tm, rb):
                acc = jnp.broadcast_to(b_ref[:, ls], (rb, LANES))
                for res, offs in _taps_by_residue(CONV_HALO - (CONV_WIDTH - 1)):
                    taps = _shifted_taps(win_ref, span_ref, r0, res, offs, rb, ls)
                    for o, tap in zip(offs, taps):
                        k = o - (CONV_HALO - (CONV_WIDTH - 1))
                        acc = acc + tap() * w_ref[k:k + 1, ls]
                c_ref[pl.ds(r0, rb), ls] = acc
            return carry

        lax.fori_loop(0, nl, lane_chunk, 0)
        n = _ln_fwd(c_ref[...], lg_ref[...], lb_ref[...])
        s_ref[...] = (n * _sigmoid(n)).astype(s_ref.dtype)

    row = pl.BlockSpec((1, D), lambda i: (0, 0))
    return _call(
        body, comm, name="dwconv_fwd", grid=(T // tm,),
        in_specs=[pl.BlockSpec((tm, D), lambda i: (i, 0)), _halo_before_spec(tm, D),
                  pl.BlockSpec((CONV_HALO, D), lambda i: (0, 0)), row, row, row],
        out_specs=[pl.BlockSpec((tm, D), lambda i: (i, 0)), pl.BlockSpec((tm, D), lambda i: (i, 0))],
        out_shape=[_sds((T, D), F32), _sds((T, D), BF16)],
        scratch_shapes=[pltpu.VMEM((tm + CONV_HALO, D), F32), pltpu.VMEM((rb + CONV_HALO, LANES), F32)],
        sem=("parallel",), operands=[g, g, w_dw, b_dw, ln_g, ln_b])


def dwconv_bwd(dc, g, ha, hg, w_dw, *, tm, comm=None):
    T, D = g.shape
    nl = D // LANES
    last = T // CONV_HALO - 1
    nt = T // tm

    rb = min(CONV_ROWS, tm)

    def body(dc_ref, dcn_ref, g_ref, gh_ref, ha_ref, hg_ref, w_ref, dh_ref, dw_ref, dbdw_ref, dbin_ref,
             win_ref, dwin_ref, dg_ref, dwp_ref, span_ref):
        i = pl.program_id(0)
        win_ref[0:CONV_HALO, :] = jnp.where(i > 0, gh_ref[...], 0.0)
        win_ref[CONV_HALO:, :] = g_ref[...]
        dwin_ref[0:tm, :] = dc_ref[...]
        dwin_ref[tm:, :] = jnp.where(i < nt - 1, dcn_ref[...], 0.0)

        @pl.when(i == 0)
        def _():
            dwp_ref[...] = jnp.zeros_like(dwp_ref)

        first = CONV_HALO - (CONV_WIDTH - 1)

        def lane_chunk(q, carry):
            ls = pl.ds(pl.multiple_of(q * LANES, LANES), LANES)
            for r0 in range(0, tm, rb):
                acc = jnp.zeros((rb, LANES), F32)
                for res, offs in _taps_by_residue(0):
                    taps = _shifted_taps(dwin_ref, span_ref, r0, res, offs, rb, ls)
                    for o, tap in zip(offs, taps):
                        k = CONV_WIDTH - 1 - o
                        acc = acc + tap() * w_ref[k:k + 1, ls]
                dg_ref[pl.ds(r0, rb), ls] = acc
                dcv = dwin_ref[pl.ds(r0, rb), ls]
                for res, offs in _taps_by_residue(first):
                    taps = _shifted_taps(win_ref, span_ref, r0, res, offs, rb, ls)
                    for o, tap in zip(offs, taps):
                        k = o - first
                        prod = dcv * tap()
                        dwp_ref[k, :, ls] += jnp.sum(prod.reshape(rb // SUBLANES, SUBLANES, LANES), axis=0)
            return carry

        lax.fori_loop(0, nl, lane_chunk, 0)

        @pl.when(i == nt - 1)
        def _():
            for k in range(CONV_WIDTH):
                dw_ref[k:k + 1, :] = jnp.sum(dwp_ref[k], axis=0, keepdims=True)
            dw_ref[CONV_WIDTH:, :] = jnp.zeros((CONV_HALO - CONV_WIDTH, D), F32)
        dg = dg_ref[...]
        ha = ha_ref[...].astype(F32)
        sg = _sigmoid(hg_ref[...].astype(F32))
        d_ha = dg * sg
        d_hg = dg * ha * sg * (1.0 - sg)
        dh_ref[:, 0:D] = d_ha.astype(dh_ref.dtype)
        dh_ref[:, D:] = d_hg.astype(dh_ref.dtype)
        _init_or_add(dbdw_ref, i, jnp.sum(dc_ref[...], axis=0, keepdims=True))
        _init_or_add(dbin_ref, i, jnp.concatenate([jnp.sum(d_ha, axis=0, keepdims=True),
                                                   jnp.sum(d_hg, axis=0, keepdims=True)], axis=1))

    tile = pl.BlockSpec((tm, D), lambda i: (i, 0))
    return _call(
        body, comm, name="dwconv_bwd", grid=(nt,),
        in_specs=[tile,
                  pl.BlockSpec((CONV_HALO, D), lambda i: (jnp.minimum((i + 1) * (tm // CONV_HALO), last), 0)),
                  tile, _halo_before_spec(tm, D),
                  tile, tile, pl.BlockSpec((CONV_HALO, D), lambda i: (0, 0))],
        out_specs=[pl.BlockSpec((tm, 2 * D), lambda i: (i, 0)), pl.BlockSpec((CONV_HALO, D), lambda i: (0, 0)),
                   pl.BlockSpec((1, D), lambda i: (0, 0)), pl.BlockSpec((1, 2 * D), lambda i: (0, 0))],
        out_shape=[_sds((T, 2 * D), BF16), _sds((CONV_HALO, D), F32), _sds((1, D), F32), _sds((1, 2 * D), F32)],
        scratch_shapes=[pltpu.VMEM((tm + CONV_HALO, D), F32), pltpu.VMEM((tm + CONV_HALO, D), F32),
                        pltpu.VMEM((tm, D), F32), pltpu.VMEM((CONV_WIDTH, SUBLANES, D), F32),
                        pltpu.VMEM((rb + CONV_HALO, LANES), F32)],
        sem=("arbitrary",), operands=[dc, dc, g, g, ha, hg, w_dw])


def _attn_specs(HD, W):
    B = ATT_BLOCK
    return [pl.BlockSpec((B, HD), lambda n: (n, 0)),
            pl.BlockSpec((B, 4 * W), lambda n: (n, 0)),
            pl.BlockSpec((B, 4 * W), lambda n: (jnp.maximum(n - 1, 0), 0))]


def _band_mask(n):
    r = lax.broadcasted_iota(jnp.int32, (ATT_BLOCK, 2 * ATT_BLOCK), 0)
    j = lax.broadcasted_iota(jnp.int32, (ATT_BLOCK, 2 * ATT_BLOCK), 1)
    return (j > r) & (j <= r + ATT_BLOCK) & ((n > 0) | (j >= ATT_BLOCK))


def _band(kvc_ref, kvp_ref, part, g, parity, W):
    lanes = slice((2 * part + parity) * W + g * LANES, (2 * part + parity) * W + (g + 1) * LANES)
    return jnp.concatenate([kvp_ref[:, lanes], kvc_ref[:, lanes]], axis=0)


def _half_mask(parity):
    lane = lax.broadcasted_iota(jnp.int32, (1, LANES), 1)
    return (lane < HEAD_DIM) if parity == 0 else (lane >= HEAD_DIM)


def widen_kv(k, v, out_ref, n_kv):
    W = n_kv * LANES
    low = _half_mask(0)
    for part, src in enumerate((k, v)):
        for cg in range(n_kv * HEAD_DIM // LANES):
            chunk = src[:, cg * LANES:(cg + 1) * LANES]
            swapped = pltpu.roll(chunk, HEAD_DIM, 1)
            for g, lo, hi in ((2 * cg, chunk, swapped), (2 * cg + 1, swapped, chunk)):
                base = 2 * part * W + g * LANES
                out_ref[:, base:base + LANES] = jnp.where(low, lo, 0.0).astype(out_ref.dtype)
                out_ref[:, base + W:base + W + LANES] = jnp.where(low, 0.0, hi).astype(out_ref.dtype)


def attn_fwd(q, kvx, sinks, n_heads, n_kv, comm=None):
    T, HD = q.shape
    W = n_kv * LANES
    B = ATT_BLOCK
    chunks_per_group = n_heads // n_kv // 2
    scale = 1.0 / math.sqrt(HEAD_DIM)

    def body(q_ref, kvc_ref, kvp_ref, sink_ref, o_ref, lse_ref):
        n = pl.program_id(0)
        mask = jnp.tile(_band_mask(n), (n_heads, 1))
        s = jnp.concatenate(
            [_dot(q_ref[:, (h // 2) * LANES:(h // 2 + 1) * LANES],
                  _band(kvc_ref, kvp_ref, 0, h // 2 // chunks_per_group, h % 2, W), ((1,), (1,)))
             for h in range(n_heads)], axis=0)
        sink = jnp.concatenate([jnp.broadcast_to(sink_ref[:, h:h + 1], (B, 1)) for h in range(n_heads)], axis=0)
        s = jnp.where(mask, s * scale, MASK_VALUE)
        m = jnp.maximum(jnp.max(s, axis=-1, keepdims=True), sink)
        e = jnp.exp(s - m)
        total = _dot(e, jnp.ones((2 * B, LANES), BF16), ((1,), (0,))) + jnp.exp(sink - m)
        lse = m + jnp.log(total[:, 0:1])
        inv = 1.0 / total
        probs = (e * jnp.concatenate([inv, inv], axis=1)).astype(BF16)
        for c in range(n_heads // 2):
            g = c // chunks_per_group
            out = (_dot(probs[2 * c * B:(2 * c + 1) * B], _band(kvc_ref, kvp_ref, 1, g, 0, W), ((1,), (0,)))
                   + _dot(probs[(2 * c + 1) * B:(2 * c + 2) * B], _band(kvc_ref, kvp_ref, 1, g, 1, W), ((1,), (0,))))
            o_ref[:, c * LANES:(c + 1) * LANES] = out.astype(o_ref.dtype)
        lse_ref[...] = jnp.concatenate([lse[h * B:(h + 1) * B] for h in range(n_heads)], axis=1)

    return _call(
        body, comm, name="attn_fwd", grid=(T // B,),
        in_specs=_attn_specs(HD, W) + [pl.BlockSpec((1, n_heads), lambda n: (0, 0))],
        out_specs=[pl.BlockSpec((B, HD), lambda n: (n, 0)), pl.BlockSpec((B, n_heads), lambda n: (n, 0))],
        out_shape=[_sds((T, HD), BF16), _sds((T, n_heads), F32)],
        sem=("parallel",), operands=[q, kvx, kvx, sinks])


def attn_bwd(q, kvx, do, lse, sinks, rope, n_heads, n_kv, comm=None):
    T, HD = q.shape
    KVD = n_kv * HEAD_DIM
    W = n_kv * LANES
    B = ATT_BLOCK
    chunks_per_group = n_heads // n_kv // 2
    scale = 1.0 / math.sqrt(HEAD_DIM)
    nb = T // B

    def body(q_ref, kvc_ref, kvp_ref, do_ref, lse_ref, sink_ref, c_ref, su_ref, sd_ref, dq_ref, dkv_ref, dsink_ref):
        n = pl.program_id(0)

        @pl.when(n == 0)
        def _():
            dkv_ref[...] = jnp.zeros_like(dkv_ref)
            dsink_ref[...] = jnp.zeros_like(dsink_ref)

        def chunk(ref, h):
            return ref[:, (h // 2) * LANES:(h // 2 + 1) * LANES]

        def band(part, h):
            return _band(kvc_ref, kvp_ref, part, h // 2 // chunks_per_group, h % 2, W)

        def stack(per_head):
            return jnp.concatenate([per_head(h) for h in range(n_heads)], axis=0)

        mask = jnp.tile(_band_mask(n), (n_heads, 1))
        s = stack(lambda h: _dot(chunk(q_ref, h), band(0, h), ((1,), (1,))))
        dp = stack(lambda h: _dot(chunk(do_ref, h), band(1, h), ((1,), (1,))))
        lse = stack(lambda h: lse_ref[:, h:h + 1])
        sink = stack(lambda h: jnp.broadcast_to(sink_ref[:, h:h + 1], (B, 1)))
        probs = jnp.exp(jnp.where(mask, s * scale, MASK_VALUE) - lse)
        delta = jnp.sum(probs * dp, axis=-1, keepdims=True)
        ds = (probs * (dp - delta) * scale).astype(BF16)
        probs = probs.astype(BF16)
        sink_term = jnp.exp(sink - lse) * delta
        dsk = [-jnp.sum(sink_term[h * B:(h + 1) * B], axis=0, keepdims=True) for h in range(n_heads)]

        dk_wide, dv_wide = [None] * n_kv, [None] * n_kv
        for c in range(n_heads // 2):
            g = c // chunks_per_group
            dq2 = None
            for h in (2 * c, 2 * c + 1):
                half = _half_mask(h % 2)
                q2, do2 = chunk(q_ref, h), chunk(do_ref, h)
                ds_h, p_h = ds[h * B:(h + 1) * B], probs[h * B:(h + 1) * B]
                part = _dot(ds_h, band(0, h), ((1,), (0,)))
                dq2 = part if dq2 is None else dq2 + part
                dk_h = _dot(ds_h, jnp.where(half, q2, jnp.zeros_like(q2)), ((0,), (0,)))
                dv_h = _dot(p_h, jnp.where(half, do2, jnp.zeros_like(do2)), ((0,), (0,)))
                dk_wide[g] = dk_h if dk_wide[g] is None else dk_wide[g] + dk_h
                dv_wide[g] = dv_h if dv_wide[g] is None else dv_wide[g] + dv_h
            dq_ref[:, c * LANES:(c + 1) * LANES] = _rope_chunk_bwd(
                dq2, c_ref[...], su_ref[...], sd_ref[...]).astype(dq_ref.dtype)

        def fold(wide):
            low = _half_mask(0)
            both = [w + pltpu.roll(w, HEAD_DIM, 1) for w in wide]
            return jnp.concatenate([jnp.where(low, both[2 * cg], both[2 * cg + 1]) for cg in range(n_kv // 2)], axis=1)

        dkv = jnp.concatenate([fold(dk_wide), fold(dv_wide)], axis=1)
        prev = pl.ds(pl.multiple_of(jnp.maximum(n - 1, 0) * B, B), B)
        cur = pl.ds(pl.multiple_of(n * B, B), B)
        dkv_ref[prev, :] += dkv[0:B, :]
        dkv_ref[cur, :] += dkv[B:, :]
        dsink_ref[...] += jnp.concatenate(dsk, axis=1)

    tab = pl.BlockSpec((B, LANES), lambda n: (n, 0))
    return _call(
        body, comm, name="attn_bwd", grid=(nb,),
        in_specs=_attn_specs(HD, W) + [pl.BlockSpec((B, HD), lambda n: (n, 0)),
                                       pl.BlockSpec((B, n_heads), lambda n: (n, 0)),
                                       pl.BlockSpec((1, n_heads), lambda n: (0, 0)), tab, tab, tab],
        out_specs=[pl.BlockSpec((B, HD), lambda n: (n, 0)), pl.BlockSpec((T, 2 * KVD), lambda n: (0, 0)),
                   pl.BlockSpec((1, n_heads), lambda n: (0, 0))],
        out_shape=[_sds((T, HD + 2 * KVD), BF16), _sds((T, 2 * KVD), F32), _sds((1, n_heads), F32)],
        sem=("arbitrary",), operands=[q, kvx, kvx, do, lse, sinks, *rope])


def dkv_finish(d_qkv, dkv, rope, HD, KVD, *, tm):
    T = dkv.shape[0]
    kv_col = HD // (2 * KVD)

    def body(alias_ref, dkv_ref, c_ref, su_ref, sd_ref, o_ref):
        del alias_ref
        dk = _rope_wide(dkv_ref[:, 0:KVD], c_ref[...], su_ref[...], sd_ref[...], _rope_chunk_bwd)
        o_ref[:, 0:KVD] = dk.astype(o_ref.dtype)
        o_ref[:, KVD:] = dkv_ref[:, KVD:].astype(o_ref.dtype)

    tab = pl.BlockSpec((tm, LANES), lambda i: (i, 0))
    return pl.pallas_call(
        body, name="dkv_finish", grid=(T // tm,),
        in_specs=[pl.BlockSpec(memory_space=pl.ANY), pl.BlockSpec((tm, 2 * KVD), lambda i: (i, 0)), tab, tab, tab],
        out_specs=pl.BlockSpec((tm, 2 * KVD), lambda i: (i, kv_col)),
        out_shape=_sds(d_qkv.shape, d_qkv.dtype), input_output_aliases={0: 0},
        compiler_params=_params(("parallel",)))(d_qkv, dkv, *rope)


def ple_bwd_elem(d_out, pp, gg, *, tm):
    T, D = d_out.shape

    def body(d_ref, pp_ref, gg_ref, dpp_ref, dgg_ref):
        d = d_ref[...]
        sg = _sigmoid(gg_ref[...].astype(F32))
        dpp_ref[...] = (d * sg).astype(dpp_ref.dtype)
        dgg_ref[...] = (d * pp_ref[...].astype(F32) * sg * (1.0 - sg)).astype(dgg_ref.dtype)

    tile = pl.BlockSpec((tm, D), lambda i: (i, 0))
    return pl.pallas_call(
        body, name="ple_bwd_elem", grid=(T // tm,), in_specs=[tile, tile, tile], out_specs=[tile, tile],
        out_shape=[_sds((T, D), BF16), _sds((T, D), BF16)], compiler_params=_params(("parallel",)))(d_out, pp, gg)


def chip_sum(name, g, p_sib, *, ta, part=0, parts=1):
    _, a, b = g.shape
    rows = a // parts
    assert a % parts == 0 and rows % ta == 0
    first = part * (rows // ta)

    def body(core_ref, g_ref, p_ref, o_ref):
        del core_ref
        o_ref[...] = (g_ref[...].astype(F32) + p_ref[...].astype(F32)).astype(o_ref.dtype)

    my_core = lax.axis_index("c").astype(jnp.int32).reshape(1)
    return pl.pallas_call(
        body, name=name, out_shape=_sds((N_CHIPS, rows, b), g.dtype),
        grid_spec=pltpu.PrefetchScalarGridSpec(
            num_scalar_prefetch=1, grid=(N_CHIPS, rows // ta),
            in_specs=[pl.BlockSpec((None, None, ta, b), lambda q, i, core: (q, core[0], first + i, 0)),
                      pl.BlockSpec((None, ta, b), lambda q, i, core: (q, first + i, 0))],
            out_specs=pl.BlockSpec((None, ta, b), lambda q, i, core: (q, i, 0))),
        compiler_params=_params(("arbitrary", "arbitrary")))(my_core, g.reshape(N_CHIPS, 2, a, b), p_sib)


def adamw(name, recvs, w, m, v, *, ta):
    L, a, b = w.shape
    n_terms = recvs[0].shape[0]
    assert a % ta == 0 and len(recvs) == L
    c1 = 1.0 - ADAM_B1 ** ADAM_STEP
    c2 = 1.0 - ADAM_B2 ** ADAM_STEP

    def body(*refs):
        r_refs = refs[:L]
        w_ref, m_ref, v_ref, g_ref, d_ref, nm_ref, nv_ref = refs[L:]
        layer = pl.program_id(0)
        for l in range(L):
            @pl.when(layer == l)
            def _(r_ref=r_refs[l]):
                g = r_ref[0].astype(F32)
                for s in range(1, n_terms):
                    g = g + r_ref[s].astype(F32)
                nm = ADAM_B1 * m_ref[...] + (1.0 - ADAM_B1) * g
                nv = ADAM_B2 * v_ref[...] + (1.0 - ADAM_B2) * jnp.square(g)
                m_hat = nm / c1
                v_hat = nv / c2
                g_ref[...] = g
                d_ref[...] = -ADAM_LR * (m_hat / (jnp.sqrt(v_hat) + ADAM_EPS) + ADAM_WD * w_ref[...])
                nm_ref[...] = nm
                nv_ref[...] = nv

    blk = pl.BlockSpec((None, ta, b), lambda l, i: (l, i, 0))
    out = _sds((L, a, b), F32)
    r_specs = [pl.BlockSpec((n_terms, ta, b), lambda l, i, ll=ll: (0, jnp.where(l == ll, i, 0), 0)) for ll in range(L)]
    return pl.pallas_call(
        body, name=name, grid=(L, a // ta), in_specs=r_specs + [blk, blk, blk],
        out_specs=[blk, blk, blk, blk], out_shape=[out, out, out, out],
        compiler_params=_params(("arbitrary", "arbitrary")))(*recvs, w, m, v)


def _pack_rows(parts):
    out = []
    for arr, rows in parts:
        arr = arr.reshape(-1, LANES).astype(F32)
        out.append(jnp.pad(arr, ((0, rows - arr.shape[0]), (0, 0))))
    return jnp.concatenate(out, axis=0)


def _small_rows(a, prefix):
    return _pack_rows([(a[prefix + name], rows) for name, rows in SMALL_SHARDED])


def _unpack_small(packed, a):
    out, r0 = {}, 0
    for name, rows in SMALL_SHARDED:
        shape = a[name].shape
        used = math.prod(shape) // LANES
        out[name] = packed[r0:r0 + used].reshape(shape)
        r0 += rows
    return out


def _rep_rows(a, prefix):
    parts = []
    for name in REPLICATED:
        arr = a[prefix + name]
        if arr.size % LANES:
            arr = jnp.pad(arr.reshape(1, -1), ((0, 0), (0, LANES - arr.size % LANES)))
        rows = -(-arr.size // LANES)
        parts.append((arr, -(-rows // SUBLANES) * SUBLANES))
    return _pack_rows(parts)


def _unpack_rep(packed, a):
    out, r0 = {}, 0
    for name in REPLICATED:
        shape = a[name].shape
        size = math.prod(shape)
        rows = -(-size // LANES)
        out[name] = packed[r0:r0 + rows].reshape(-1)[:size].reshape(shape)
        r0 += -(-rows // SUBLANES) * SUBLANES
    return out


def _step(a):
    x = a['x'][0]
    T, D = x.shape
    tgt = a['loss_target'][0]
    p_in = [a['p'][i, 0] for i in range(DEPTH)]
    PLE = p_in[0].shape[1]
    n_heads = a['attn_sinks'].shape[1]
    HD = n_heads * HEAD_DIM
    KVD = a['kv_w_k'].shape[1]
    n_kv = KVD // HEAD_DIM
    F = a['mlp_w_down'].shape[1] * N_DEV
    tm = min(TOKEN_TILE, T)
    tm2 = min(2 * TOKEN_TILE, T)
    tmc = min(TOKEN_TILE, T)
    tw = 512
    alpha = DEEPNORM_ALPHA
    xb = x.astype(BF16)
    p_b = [p.astype(BF16) for p in p_in]

    def shard3(w):
        return w.reshape((1,) + w.shape) if w.ndim == 2 else w

    def gather(*specs):
        items = []
        for spec in specs:
            w = shard3(a[spec[0]])[spec[1]]
            if len(spec) == 3:
                rows = w.shape[0] // 2
                w = w[spec[2] * rows:(spec[2] + 1) * rows]
            items.append(w.astype(BF16))
        return _GatherJob(items)

    def join_halves(lo, hi):
        return jnp.concatenate([lo, hi], axis=1)

    (W_in, small_full), = comm_only("gather_first",
                                    [_GatherJob([a['conv_w_in'][0].astype(BF16), _small_rows(a, '')])])
    r0, small = 0, {}
    for name, rows in SMALL_SHARDED:
        small[name] = small_full[:, r0:r0 + rows]
        r0 += rows
    b_in = small['conv_b_in'][:, 0:2 * D // N_DEV // LANES].reshape(1, 2 * D)
    w_dw = jnp.transpose(small['conv_w_dw'], (1, 0, 2)).reshape(CONV_HALO, D)
    b_dw, cln_g, cln_b, b_out = (small[nm][:, 0].reshape(1, D) for nm in
                                 ('conv_b_dw', 'conv_ln_g', 'conv_ln_b', 'conv_b_out'))
    W_up, W_down, W_proj, W_gate = {}, {}, {}, {}
    mix_g, mix_b, mlp_g, mlp_b = a['mix_ln_g'], a['mix_ln_b'], a['mlp_ln_g'], a['mlp_ln_b']
    rope = _rope_tables(T)

    def set_ple_weights(li, g_proj, g_gate):
        W_proj[li] = jnp.transpose(g_proj, (1, 0, 2)).reshape(1, PLE, D)
        W_gate[li] = g_gate.reshape(1, D, D)

    def row(v, i):
        return v[i:i + 1]

    def res_ln_epi(coef):
        def epi(accs, ex, out, i):
            acc = accs[0] if isinstance(accs, list) else accs
            n_ex = len(ex)
            res_ref, g_ref, b_ref = ex[n_ex - 3], ex[n_ex - 2], ex[n_ex - 1]
            z = coef * res_ref[...] + acc
            if n_ex == 4:
                z = z + ex[0][...]
            out[0][...] = z
            xo = _ln_fwd(z, g_ref[...], b_ref[...])
            out[1][...] = xo
            out[2][...] = xo.astype(BF16)
        return epi

    res_ln_outs = [(_sds((T, D), F32), 'tile'), (_sds((T, D), F32), 'tile'), (_sds((T, D), BF16), 'tile')]

    def mlp_fwd(li, xin, xin_b, w_down_lo, down_comm=None):
        def up_epi(accs, ex, out, i):
            u = accs[0]
            out[0][...] = u.astype(BF16)
            out[1][...] = jnp.square(jnp.maximum(u, 0.0)).astype(BF16)
        (u, act), got_up = mm_nn(f"mlp_up_{li}", [(xin_b, W_up[li], 0)], [], [(_sds((T, F), BF16), 'tile')] * 2,
                                 up_epi, tm=tm2, tn=min(1024, F), comm=gather(('mlp_w_down', li, 1)))
        W_down[li] = join_halves(w_down_lo, got_up[0]).reshape(1, F, D)
        res = mm_nn(f"mlp_down_{li}", [(act, W_down[li], 0)],
                    [(xin, 'tile'), (row(mlp_g, li), 'row'), (row(mlp_b, li), 'row')],
                    res_ln_outs, res_ln_epi(alpha), tm=tm, tn=D, tk=F, comm=down_comm)
        (z, xo, xo_b), got_down = res if down_comm is not None else (res, ())
        return u, act, z, xo, xo_b, got_down

    def ple_fwd(li, xin, xin_b, with_loss, comm=None):
        def epi(accs, ex, out, i):
            pp, gg = accs
            xo = ex[0][...] + pp * _sigmoid(gg)
            out[1][...] = pp.astype(BF16)
            out[2][...] = gg.astype(BF16)
            if with_loss:
                err = xo - ex[1][...]
                out[0][...] = err * (1.0 / D)
                _init_or_add(out[3], i, jnp.sum(err * err, axis=0, keepdims=True) * (0.5 / D))
            else:
                out[0][...] = xo
                out[3][...] = xo.astype(BF16)
        extras = [(xin, 'tile')] + ([(tgt, 'tile')] if with_loss else [])
        outs = [(_sds((T, D), F32), 'tile'), (_sds((T, D), BF16), 'tile'), (_sds((T, D), BF16), 'tile')]
        outs.append((_sds((1, D), F32), 'rowacc') if with_loss else (_sds((T, D), BF16), 'tile'))
        return mm_nn(f"ple_{li}", [(p_b[li], W_proj[li], 0), (xin_b, W_gate[li], 0)], extras, outs, epi, tm=tm, tn=D,
                     comm=comm)

    assert D // N_DEV == LANES
    (g0, ha0, hg0), got = _glu(xb, W_in, b_in, T, D, tm, gather(('conv_w_out', 0), ('mlp_w_up', 0, 0)))
    W_out, up_lo = got[0].reshape(1, D, D), got[1]
    (c0, s0), got = dwconv_fwd(g0, w_dw, b_dw, cln_g, cln_b, tm=tmc,
                               comm=gather(('mlp_w_up', 0, 1), ('ple_w_proj', 0), ('ple_w_gate', 0)))
    W_up[0] = join_halves(up_lo, got[0])
    set_ple_weights(0, got[1], got[2])
    (z1, x1, x1b), (down_lo,) = mm_nn("conv_out", [(s0, W_out, 0)],
                                      [(b_out, 'row'), (x, 'tile'), (row(mix_g, 0), 'row'), (row(mix_b, 0), 'row')],
                                      res_ln_outs, res_ln_epi(alpha), tm=tm, tn=D, comm=gather(('mlp_w_down', 0, 0)))
    u0, act0, z2, x2, x2b, got = mlp_fwd(
        0, x1, x1b, down_lo, down_comm=gather(('attn_w_q', 0), ('kv_w_k', 0), ('kv_w_v', 0), ('attn_w_o', 0)))
    W_qkv = jnp.concatenate([got[0].reshape(D, HD), got[1].reshape(D, KVD), got[2].reshape(D, KVD)], axis=1)[None]
    W_o = got[3].reshape(1, HD, D)
    (x3, pp0, gg0, x3b), got = ple_fwd(0, x2, x2b, False, comm=gather(('ple_w_proj', 1), ('ple_w_gate', 1)))
    set_ple_weights(1, got[0], got[1])

    def qkv_epi(accs, ex, out, i):
        t = accs[0]
        c, su, sd = ex[0][...], ex[1][...], ex[2][...]
        out[0][...] = _rope_wide(t[:, 0:HD], c, su, sd, _rope_chunk).astype(BF16)
        widen_kv(_rope_wide(t[:, HD:HD + KVD], c, su, sd, _rope_chunk), t[:, HD + KVD:], out[1], n_kv)
    NQ = HD + 2 * KVD
    q1, kvx1 = mm_nn("qkv_rope", [(x3b, W_qkv, 0)], [(t, 'tab') for t in rope],
                     [(_sds((T, HD), BF16), 'rows'), (_sds((T, 4 * n_kv * LANES), BF16), 'rows')], qkv_epi,
                     tm=tm, tn=NQ)
    (o1, lse1), got = attn_fwd(q1, kvx1, a['attn_sinks'], n_heads, n_kv,
                               comm=gather(('mlp_w_up', 1, 0), ('mlp_w_up', 1, 1)))
    W_up[1] = join_halves(got[0], got[1])
    (z3, x4, x4b), (down_lo,) = mm_nn("attn_out", [(o1, W_o, 0)],
                                      [(x3, 'tile'), (row(mix_g, 1), 'row'), (row(mix_b, 1), 'row')],
                                      res_ln_outs, res_ln_epi(alpha), tm=tm, tn=D, comm=gather(('mlp_w_down', 1, 0)))
    u1, act1, z4, x5, x5b, _ = mlp_fwd(1, x4, x4b, down_lo)
    dy, pp1, gg1, loss_row = ple_fwd(1, x5, x5b, True)
    loss_local = jnp.sum(loss_row)

    grads = {}

    def ln_bwd_epi(coef, with_colsum):
        def epi(acc, ex, out, i):
            d_x = acc + coef * ex[0][...]
            dz, dg, db = _ln_bwd(d_x, ex[1][...], ex[2][...])
            out[0][...] = dz
            out[1][...] = dz.astype(BF16)
            _init_or_add(out[2], i, dg)
            _init_or_add(out[3], i, db)
            if with_colsum:
                _init_or_add(out[4], i, jnp.sum(dz, axis=0, keepdims=True))
        return epi

    def ln_bwd_outs(with_colsum):
        outs = [(_sds((T, D), F32), 'tile'), (_sds((T, D), BF16), 'tile'), (_sds((1, D), F32), 'rowacc'),
                (_sds((1, D), F32), 'rowacc')]
        return outs + ([(_sds((1, D), F32), 'rowacc')] if with_colsum else [])

    def ple_bwd(li, d_out, xin, pp, gg, z_mlp, pair_specs=None):
        d_pp, d_gg = ple_bwd_elem(d_out, pp, gg, tm=tm)
        grads[('ple_w_proj', li)] = mm_tn(f"d_ple_proj_{li}", p_b[li], d_pp, D, BF16, tm=T, tk=PLE, tn=tw)
        grads[('ple_w_gate', li)] = mm_tn(f"d_ple_gate_{li}", xin, d_gg, D, BF16, tm=T, tk=tw, tn=tw)
        job = pair_stage(*pair_specs) if pair_specs else None
        res = mm_nt(f"ple_dx_{li}", d_gg, W_gate[li], [(d_out, 'tile'), (z_mlp, 'tile'), (row(mlp_g, li), 'row')],
                    ln_bwd_outs(False), ln_bwd_epi(1.0, False), tm=tm, tko=D, tc=D, comm=job)
        (dz, dzb, dg, db), got = res if job is not None else (res, None)
        grads[('mlp_ln_g', li)], grads[('mlp_ln_b', li)] = dg, db
        if job is not None:
            pair_done(job, got)
        return dz, dzb

    recv = {}
    wqkv_cols = {'attn_w_q': (0, HD), 'kv_w_k': (HD, HD + KVD), 'kv_w_v': (HD + KVD, NQ)}

    def piece(name, li):
        if name == 'conv_w_in':
            return grads['conv_w_in']
        if name == 'mlp_w_up':
            return grads[('mlp_w_up', li)]
        if name == 'ple_w_proj':
            return jnp.transpose(grads[('ple_w_proj', li)][0].reshape(PLE, N_DEV, D // N_DEV), (1, 0, 2))
        if name in wqkv_cols:
            g = grads['w_qkv'][:, wqkv_cols[name][0]:wqkv_cols[name][1]]
        else:
            g = grads[name] if name in grads else grads[(name, li)]
            g = g[0]
        return g.reshape(N_DEV, g.shape[0] // N_DEV, g.shape[1])

    def pair_stage(*specs):
        job = _PairJob([piece(nm, li) for nm, li in specs])
        job.specs = specs
        return job

    sums = {}
    halved = ('mlp_w_up', 'mlp_w_down')

    def pair_done(pair_job, got):
        for (nm, li), mine, theirs in zip(pair_job.specs, pair_job.sources, got):
            if nm in halved:
                for h in (0, 1):
                    sums[(nm, li, h)] = chip_sum(f"chip_sum_{nm}_{li}_{h}", mine, theirs,
                                                 ta=min(256, mine.shape[1] // 2), part=h, parts=2)
            else:
                sums[(nm, li)] = chip_sum(f"chip_sum_{nm}_{li}", mine, theirs, ta=min(512, mine.shape[1]))

    def chip_stage(*keys):
        job = _ChipJob([sums[k] for k in keys])
        job.specs = keys
        return job

    def hosted(res, job):
        main, got = res
        for spec, r in zip(job.specs, got):
            recv[spec] = r
        return main

    def mlp_bwd(li, dz, dzb, xin, u, act, z_mix, with_colsum, pair_specs, down_keys=None):
        if down_keys:
            job = chip_stage(*down_keys)
            grads[('mlp_w_down', li)] = hosted(
                mm_tn(f"d_mlp_down_{li}", act, dzb, D, BF16, tm=T, tk=tw, tn=tw, comm=job), job)
        else:
            grads[('mlp_w_down', li)] = mm_tn(f"d_mlp_down_{li}", act, dzb, D, BF16, tm=T, tk=tw, tn=tw)

        def du_epi(acc, ex, out, i):
            out[0][...] = (acc * (2.0 * jnp.maximum(ex[0][...].astype(F32), 0.0))).astype(BF16)
        pair = pair_stage(*pair_specs)
        (du,), got = mm_nt(f"mlp_du_{li}", dzb, W_down[li], [(u, 'tile')], [(_sds((T, F), BF16), 'tile')], du_epi,
                           tm=tm2, tko=min(1024, F), tc=D, comm=pair)
        pair_done(pair, got)
        job = chip_stage(('mlp_w_down', li, 0))
        grads[('mlp_w_up', li)] = hosted(
            mm_tn(f"d_mlp_up_{li}", xin, du, F // N_DEV, BF16, tm=T, tk=tw, tn=tw, comm=job), job)
        job = chip_stage(('mlp_w_down', li, 1))
        res = hosted(mm_nt(f"mlp_dx_{li}", du, W_up[li], [(dz, 'tile'), (z_mix, 'tile'), (row(mix_g, li), 'row')],
                           ln_bwd_outs(with_colsum), ln_bwd_epi(alpha, with_colsum), tm=tm, tko=D, tc=F, comm=job), job)
        grads[('mix_ln_g', li)], grads[('mix_ln_b', li)] = res[2], res[3]
        return res

    dz4, dz4b = ple_bwd(1, dy, x5b, pp1, gg1, z4)
    dz3, dz3b, _, _ = mlp_bwd(1, dz4, dz4b, x4b, u1, act1, z3, False,
                              [('mlp_w_down', 1), ('ple_w_gate', 1), ('ple_w_proj', 1)])
    grads['attn_w_o'] = mm_tn("d_attn_wo", o1, dz3b, D, BF16, tm=T, tk=tw, tn=tw)

    def do_epi(acc, ex, out, i):
        out[0][...] = acc.astype(BF16)
    pair = pair_stage(('mlp_w_up', 1), ('attn_w_o', 0))
    (do1,), got = mm_nt("attn_do", dz3b, W_o, [], [(_sds((T, HD), BF16), 'tile')], do_epi, tm=tm, tko=HD, tc=D,
                        comm=pair)
    pair_done(pair, got)
    job = chip_stage(('mlp_w_up', 1, 0), ('mlp_w_up', 1, 1), ('attn_w_o', 0))
    d_qkv, dkv, d_sinks = hosted(attn_bwd(q1, kvx1, do1, lse1, a['attn_sinks'], rope, n_heads, n_kv, comm=job), job)
    d_qkv = dkv_finish(d_qkv, dkv, rope, HD, KVD, tm=tm)
    job = chip_stage(('ple_w_gate', 1), ('ple_w_proj', 1))
    grads['w_qkv'] = hosted(mm_tn("d_wqkv", x3b, d_qkv, NQ, BF16, tm=T, tk=tw, tn=2 * KVD, comm=job), job)[0]

    def dx3_epi(acc, ex, out, i):
        out[0][...] = acc + alpha * ex[0][...]
    dx3, = mm_nt("attn_dx", d_qkv, W_qkv, [(dz3, 'tile')], [(_sds((T, D), F32), 'tile')], dx3_epi,
                 tm=tm, tko=D, tc=NQ)

    dz2, dz2b = ple_bwd(0, dx3, x2b, pp0, gg0, z2,
                        [('attn_w_q', 0), ('kv_w_k', 0), ('kv_w_v', 0), ('ple_w_gate', 0), ('ple_w_proj', 0)])
    dz1, dz1b, _, _, db_out = mlp_bwd(
        0, dz2, dz2b, x1b, u0, act0, z1, True, [('mlp_w_down', 0)],
        down_keys=[('attn_w_q', 0), ('kv_w_k', 0), ('kv_w_v', 0), ('ple_w_gate', 0), ('ple_w_proj', 0)])
    grads['conv_w_out'] = mm_tn("d_conv_wout", s0, dz1b, D, BF16, tm=T, tk=tw, tn=tw)

    def ds_epi(acc, ex, out, i):
        n = _ln_fwd(ex[0][...], ex[1][...], ex[2][...])
        sg = _sigmoid(n)
        dn = acc * (sg * (1.0 + n * (1.0 - sg)))
        dc, dg, db = _ln_bwd(dn, ex[0][...], ex[1][...])
        out[0][...] = dc
        _init_or_add(out[1], i, dg)
        _init_or_add(out[2], i, db)
    pair = pair_stage(('mlp_w_up', 0), ('conv_w_out', 0))
    (dc0, d_cln_g, d_cln_b), got = mm_nt("conv_ds", dz1b, W_out, [(c0, 'tile'), (cln_g, 'row'), (cln_b, 'row')],
                                         [(_sds((T, D), F32), 'tile'), (_sds((1, D), F32), 'rowacc'),
                                          (_sds((1, D), F32), 'rowacc')], ds_epi, tm=tm, tko=D, tc=D, comm=pair)
    pair_done(pair, got)
    job = chip_stage(('mlp_w_up', 0, 0), ('mlp_w_up', 0, 1), ('conv_w_out', 0))
    dh0, d_wdw, d_bdw, d_bin = hosted(dwconv_bwd(dc0, g0, ha0, hg0, w_dw, tm=tmc, comm=job), job)
    grads['conv_w_in'] = mm_tn("d_conv_win", xb, dh0, 2 * D // N_DEV, BF16, tm=T, tk=tw, tn=tw)

    def dx_epi(acc, ex, out, i):
        out[0][...] = acc + alpha * ex[0][...]
    pair = pair_stage(('conv_w_in', 0))
    (grad_x,), got = mm_nt("conv_dx", dh0, W_in, [(dz1, 'tile')], [(_sds((T, D), F32), 'tile')], dx_epi,
                           tm=tm, tko=D, tc=D, comm=pair)
    pair_done(pair, got)
    last_chip = chip_stage(('conv_w_in', 0))

    def own_rows(vec, rows_used, rows):
        arr = vec.reshape(N_DEV, rows_used, LANES)
        return jnp.pad(arr, ((0, 0), (0, rows - rows_used), (0, 0)))
    dwdw_dev = jnp.transpose(d_wdw.reshape(CONV_HALO, N_DEV, D // N_DEV), (1, 0, 2))
    lane_rows = D // N_DEV // LANES
    small_grad = jnp.concatenate([
        own_rows(d_bin, 2 * lane_rows, 8), dwdw_dev if lane_rows == 1 else dwdw_dev.reshape(N_DEV, -1, LANES),
        own_rows(d_bdw, lane_rows, 8), own_rows(d_cln_g, lane_rows, 8), own_rows(d_cln_b, lane_rows, 8),
        own_rows(db_out, lane_rows, 8)], axis=1)
    n_small = small_grad.shape[1]

    rep_local = {'mix_ln_g': jnp.concatenate([grads[('mix_ln_g', li)] for li in range(DEPTH)], axis=0),
                 'mix_ln_b': jnp.concatenate([grads[('mix_ln_b', li)] for li in range(DEPTH)], axis=0),
                 'mlp_ln_g': jnp.concatenate([grads[('mlp_ln_g', li)] for li in range(DEPTH)], axis=0),
                 'mlp_ln_b': jnp.concatenate([grads[('mlp_ln_b', li)] for li in range(DEPTH)], axis=0),
                 'attn_sinks': d_sinks}
    rep_grad = _rep_rows(rep_local, '')
    n_rep = rep_grad.shape[0]
    got, (recv_small, recv_rep) = comm_only("exchange_last", [last_chip, _DirectJob(
        [small_grad, jnp.broadcast_to(rep_grad[None], (N_DEV, n_rep, LANES))])])
    hosted((None, got), last_chip)

    result = {}
    kinds = ('grad', 'delta', 'new_m', 'new_v')
    w, m, v = (_small_rows(a, pre)[None] for pre in ('', 'm_', 'v_'))
    for kind, arr in zip(kinds, adamw("adamw_small", [recv_small], w, m, v, ta=n_small)):
        for pname, val in _unpack_small(arr[0], a).items():
            result[(kind, pname)] = val
    w, m, v = (_rep_rows(a, pre)[None] for pre in ('', 'm_', 'v_'))
    for kind, arr in zip(kinds, adamw("adamw_rep", [recv_rep], w, m, v, ta=n_rep)):
        for pname, val in _unpack_rep(arr[0], a).items():
            result[(kind, pname)] = val
    for name in BIG_WEIGHTS:
        w, m, v = (shard3(a[pre + name]) for pre in ('', 'm_', 'v_'))
        recvs = [join_halves(recv[(name, li, 0)], recv[(name, li, 1)]) if name in halved else recv[(name, li)]
                 for li in range(w.shape[0])]
        for kind, arr in zip(kinds, adamw("adamw_" + name, recvs, w, m, v, ta=min(256, w.shape[1]))):
            result[(kind, name)] = arr.reshape(a[name].shape)

    loss = lax.psum(loss_local, ("x", "y", "c"))
    out = [loss, grad_x[None]]
    for kind in ('grad', 'delta', 'new_m', 'new_v'):
        out += [result[(kind, name)] for name in WEIGHT_NAMES]
    return tuple(out)


def _glu(x, W_in, b_in, T, D, tm, comm=None):
    n = W_in.shape[2]
    q = 2 if D // n % 2 == 0 else 1
    nt = D // (q * n)
    tn = q * n

    def body(x_ref, wa_ref, wg_ref, ba_ref, bg_ref, g_ref, ha_ref, hg_ref):
        xb = x_ref[...]
        ha = jnp.concatenate([_dot(xb, wa_ref[s], ((1,), (0,))) for s in range(q)], axis=1) + ba_ref[...]
        hg = jnp.concatenate([_dot(xb, wg_ref[s], ((1,), (0,))) for s in range(q)], axis=1) + bg_ref[...]
        g_ref[...] = ha * _sigmoid(hg)
        ha_ref[...] = ha.astype(ha_ref.dtype)
        hg_ref[...] = hg.astype(hg_ref.dtype)

    return _call(
        body, comm, name="conv_in_glu", grid=(T // tm, nt),
        in_specs=[pl.BlockSpec((tm, D), lambda i, j: (i, 0)),
                  pl.BlockSpec((q, D, n), lambda i, j: (j, 0, 0)),
                  pl.BlockSpec((q, D, n), lambda i, j: (j + nt, 0, 0)),
                  pl.BlockSpec((1, tn), lambda i, j: (0, j)), pl.BlockSpec((1, tn), lambda i, j: (0, j + nt))],
        out_specs=[pl.BlockSpec((tm, tn), lambda i, j: (i, j))] * 3,
        out_shape=[_sds((T, D), F32), _sds((T, D), BF16), _sds((T, D), BF16)],
        sem=("parallel", "parallel"), operands=[x, W_in, W_in, b_in, b_in])


def kernel(x, p, conv_w_in, conv_b_in, conv_w_dw, conv_b_dw, conv_ln_g, conv_ln_b, conv_w_out, conv_b_out, kv_w_k, kv_w_v, attn_w_q, attn_sinks, attn_w_o, mix_ln_g, mix_ln_b, mlp_w_up, mlp_w_down, mlp_ln_g, mlp_ln_b, ple_w_proj, ple_w_gate, loss_target, m_conv_w_in, m_conv_b_in, m_conv_w_dw, m_conv_b_dw, m_conv_ln_g, m_conv_ln_b, m_conv_w_out, m_conv_b_out, m_kv_w_k, m_kv_w_v, m_attn_w_q, m_attn_sinks, m_attn_w_o, m_mix_ln_g, m_mix_ln_b, m_mlp_w_up, m_mlp_w_down, m_mlp_ln_g, m_mlp_ln_b, m_ple_w_proj, m_ple_w_gate, v_conv_w_in, v_conv_b_in, v_conv_w_dw, v_conv_b_dw, v_conv_ln_g, v_conv_ln_b, v_conv_w_out, v_conv_b_out, v_kv_w_k, v_kv_w_v, v_attn_w_q, v_attn_sinks, v_attn_w_o, v_mix_ln_g, v_mix_ln_b, v_mlp_w_up, v_mlp_w_down, v_mlp_ln_g, v_mlp_ln_b, v_ple_w_proj, v_ple_w_gate):
    return _step(dict(locals()))
```

```python
import functools
import math

import jax
import jax.numpy as jnp
from jax import lax
from jax.experimental import pallas as pl
from jax.experimental.pallas import tpu as pltpu

F32 = jnp.float32
BF16 = jnp.bfloat16

N_DEV = 8
HEAD_DIM = 64
ROPE_DIM = HEAD_DIM // 4
ROPE_HALF = ROPE_DIM // 2
ROPE_THETA = 500000.0
ATT_BLOCK = 128
CONV_WIDTH = 31
CONV_HALO = 32
CONV_ROWS = 128
LN_EPS = 1e-5
DEPTH = 2
DEEPNORM_ALPHA = (2 * DEPTH) ** 0.25
MASK_VALUE = -1e30

ADAM_LR = 0.001
ADAM_B1 = 0.9
ADAM_B2 = 0.999
ADAM_EPS = 1e-08
ADAM_WD = 0.01
ADAM_STEP = 10

LANES = 128
SUBLANES = 8
VMEM_LIMIT_BYTES = 52 * 1024 * 1024
TOKEN_TILE = 512
MESH_ID = pl.DeviceIdType.MESH
RELAY_AT_TENTHS = 5
FORWARD_AT_TENTHS = 8

WEIGHT_NAMES = ['conv_w_in', 'conv_b_in', 'conv_w_dw', 'conv_b_dw', 'conv_ln_g', 'conv_ln_b', 'conv_w_out',
                'conv_b_out', 'kv_w_k', 'kv_w_v', 'attn_w_q', 'attn_sinks', 'attn_w_o', 'mix_ln_g', 'mix_ln_b',
                'mlp_w_up', 'mlp_w_down', 'mlp_ln_g', 'mlp_ln_b', 'ple_w_proj', 'ple_w_gate']
BIG_WEIGHTS = ['conv_w_in', 'conv_w_out', 'kv_w_k', 'kv_w_v', 'attn_w_q', 'attn_w_o', 'mlp_w_up', 'mlp_w_down',
               'ple_w_proj', 'ple_w_gate']
SMALL_SHARDED = [('conv_b_in', 8), ('conv_w_dw', 32), ('conv_b_dw', 8), ('conv_ln_g', 8), ('conv_ln_b', 8),
                 ('conv_b_out', 8)]
REPLICATED = ['mix_ln_g', 'mix_ln_b', 'mlp_ln_g', 'mlp_ln_b', 'attn_sinks']


def _params(sem):
    return pltpu.CompilerParams(dimension_semantics=sem, vmem_limit_bytes=VMEM_LIMIT_BYTES)


def _sds(shape, dtype):
    return jax.ShapeDtypeStruct(shape, dtype)


def _my_place():
    x, y, c = lax.axis_index("x"), lax.axis_index("y"), lax.axis_index("c")
    return x, y, c, 4 * x + 2 * y + c


def _peers(x, y, c):
    out = []
    for dx in (0, 1):
        for dy in (0, 1):
            for dc in (0, 1):
                if dx or dy or dc:
                    px, py, pc = x ^ dx, y ^ dy, c ^ dc
                    out.append(((px, py, pc), 4 * px + 2 * py + pc))
    return out


N_CHIPS = 4


def _other_chips(x, y):
    return [((x ^ dx, y ^ dy), 2 * (x ^ dx) + (y ^ dy)) for dx, dy in ((1, 0), (0, 1), (1, 1))]


def _remote(src, dst, send, recv, to):
    return pltpu.make_async_remote_copy(src_ref=src, dst_ref=dst, send_sem=send, recv_sem=recv, device_id=to,
                                        device_id_type=MESH_ID)


def _wait_slabs(buf, count, send, recv, me, sent=True, received=True):
    part = buf.at[pl.ds(0, count)]
    cp = _remote(part, part, send, recv, me)
    if sent:
        cp.wait_send()
    if received:
        cp.wait_recv()


class _DirectJob:
    n_sems = 3

    def __init__(self, items):
        self.sources = list(items)
        self.dests = [_sds(it.shape, it.dtype) for it in items]
        self.n = len(items)

    def start(self, src, dst, send, recv, loc):
        x, y, c, me = _my_place()
        for k in range(self.n):
            here = dst[k].at[pl.ds(me, 1)]
            pltpu.make_async_copy(src[k].at[pl.ds(me, 1)], here, loc.at[k]).start()
            for peer, idx in _peers(x, y, c):
                _remote(src[k].at[pl.ds(idx, 1)], here, send.at[k], recv.at[k], peer).start()

    def relay(self, *refs):
        pass

    def forward(self, *refs):
        pass

    def finish(self, src, dst, send, recv, loc):
        x, y, c, me = _my_place()
        for k in range(self.n):
            pltpu.make_async_copy(src[k].at[pl.ds(me, 1)], dst[k].at[pl.ds(me, 1)], loc.at[k]).wait()
            _wait_slabs(dst[k], N_DEV - 1, send.at[k], recv.at[k], (x, y, c))


class _GatherJob:
    n_sems = 7

    def __init__(self, items):
        self.sources = [it.reshape((1,) + it.shape) for it in items]
        self.dests = [_sds((N_DEV,) + it.shape, it.dtype) for it in items]
        self.n = len(items)

    @staticmethod
    def _neighbours(x, y, c):
        out = []
        for dx, dy in ((c, 1 - c), (1 - c, c), (1, 1)):
            px, py = x ^ dx, y ^ dy
            out.append(((px, py, c), 4 * px + 2 * py + c))
        return out

    def start(self, src, dst, send_a, recv_a, send_b, recv_b, send_d, recv_d, loc):
        x, y, c, me = _my_place()
        (first, _), (second, _), _ = self._neighbours(x, y, c)
        for k in range(self.n):
            here = dst[k].at[pl.ds(me, 1)]
            pltpu.make_async_copy(src[k], here, loc.at[k]).start()
            _remote(src[k], here, send_d.at[k], recv_d.at[k], (x, y, 1 - c)).start()
            _remote(src[k], here, send_a.at[k], recv_a.at[k], first).start()
            _remote(src[k], here, send_b.at[k], recv_b.at[k], second).start()

    def relay(self, src, dst, send_a, recv_a, send_b, recv_b, send_d, recv_d, loc):
        x, y, c, me = _my_place()
        (_, first_idx), (second, _), _ = self._neighbours(x, y, c)
        for k in range(self.n):
            _wait_slabs(dst[k], 1, send_a.at[k], recv_a.at[k], (x, y, c), sent=False)
            rows = dst[k].at[pl.ds(first_idx, 1)]
            _remote(rows, rows, send_b.at[k], recv_b.at[k], second).start()

    def forward(self, src, dst, send_a, recv_a, send_b, recv_b, send_d, recv_d, loc):
        x, y, c, me = _my_place()
        for k in range(self.n):
            _wait_slabs(dst[k], 2, send_b.at[k], recv_b.at[k], (x, y, c), sent=False)
            for _, idx in self._neighbours(x, y, c):
                rows = dst[k].at[pl.ds(idx, 1)]
                _remote(rows, rows, send_d.at[k], recv_d.at[k], (x, y, 1 - c)).start()

    def finish(self, src, dst, send_a, recv_a, send_b, recv_b, send_d, recv_d, loc):
        x, y, c, me = _my_place()
        for k in range(self.n):
            pltpu.make_async_copy(src[k], dst[k].at[pl.ds(me, 1)], loc.at[k]).wait()
            _wait_slabs(dst[k], 1, send_a.at[k], recv_a.at[k], (x, y, c), received=False)
            _wait_slabs(dst[k], 2, send_b.at[k], recv_b.at[k], (x, y, c), received=False)
            _wait_slabs(dst[k], N_CHIPS, send_d.at[k], recv_d.at[k], (x, y, c))


class _PairJob:
    n_sems = 2

    def __init__(self, items):
        self.sources = list(items)
        self.dests = [_sds((N_CHIPS,) + it.shape[1:], it.dtype) for it in items]
        self.n = len(items)

    def start(self, src, dst, send, recv):
        x, y, c, me = _my_place()
        for k in range(self.n):
            for q in range(N_CHIPS):
                _remote(src[k].at[pl.ds(2 * q + 1 - c, 1)], dst[k].at[pl.ds(q, 1)], send.at[k], recv.at[k],
                        (x, y, 1 - c)).start()

    def relay(self, *refs):
        pass

    def forward(self, *refs):
        pass

    def finish(self, src, dst, send, recv):
        x, y, c, me = _my_place()
        for k in range(self.n):
            _wait_slabs(dst[k], N_CHIPS, send.at[k], recv.at[k], (x, y, c))


class _ChipJob:
    n_sems = 3

    def __init__(self, items):
        self.sources = list(items)
        self.dests = [_sds(it.shape, it.dtype) for it in items]
        self.n = len(items)

    def start(self, src, dst, send, recv, loc):
        x, y, c, me = _my_place()
        mine = 2 * x + y
        for k in range(self.n):
            here = dst[k].at[pl.ds(mine, 1)]
            pltpu.make_async_copy(src[k].at[pl.ds(mine, 1)], here, loc.at[k]).start()
            for (px, py), q in _other_chips(x, y):
                _remote(src[k].at[pl.ds(q, 1)], here, send.at[k], recv.at[k], (px, py, c)).start()

    def relay(self, *refs):
        pass

    def forward(self, *refs):
        pass

    def finish(self, src, dst, send, recv, loc):
        x, y, c, me = _my_place()
        mine = 2 * x + y
        for k in range(self.n):
            pltpu.make_async_copy(src[k].at[pl.ds(mine, 1)], dst[k].at[pl.ds(mine, 1)], loc.at[k]).wait()
            _wait_slabs(dst[k], N_CHIPS - 1, send.at[k], recv.at[k], (x, y, c))


def _job_sems(job):
    return [pltpu.SemaphoreType.DMA((job.n,))] * job.n_sems


def _call(body, comm, *, name, grid, in_specs, out_specs, out_shape, operands, sem, scratch_shapes=(), aliases=None):
    single = not isinstance(out_shape, (list, tuple))
    out_shape = [out_shape] if single else list(out_shape)
    out_specs = [out_specs] if single else list(out_specs)
    if comm is None:
        res = pl.pallas_call(body, name=name, grid=grid, in_specs=list(in_specs), out_specs=out_specs,
                             out_shape=out_shape, scratch_shapes=list(scratch_shapes),
                             input_output_aliases=aliases or {}, compiler_params=_params(sem))(*operands)
        return res[0] if single else res
    n_in, n_out, n_scr, n_c = len(in_specs), len(out_shape), len(scratch_shapes), comm.n
    any_spec = pl.BlockSpec(memory_space=pl.ANY)
    steps = math.prod(grid)
    early = min(steps - 1, (steps * RELAY_AT_TENTHS) // 10)
    mid = min(steps - 1, (steps * FORWARD_AT_TENTHS) // 10)

    def hosted(*refs):
        ins, c_src = refs[:n_in], refs[n_in:n_in + n_c]
        outs = refs[n_in + n_c:n_in + n_c + n_out]
        c_dst = refs[n_in + n_c + n_out:n_in + 2 * n_c + n_out]
        scr = refs[n_in + 2 * n_c + n_out:n_in + 2 * n_c + n_out + n_scr]
        sems = refs[n_in + 2 * n_c + n_out + n_scr:]
        step = pl.program_id(0)
        for d in range(1, len(grid)):
            step = step * grid[d] + pl.program_id(d)

        @pl.when(step == 0)
        def _():
            comm.start(c_src, c_dst, *sems)

        @pl.when(step == early)
        def _():
            comm.relay(c_src, c_dst, *sems)

        @pl.when(step == mid)
        def _():
            comm.forward(c_src, c_dst, *sems)

        body(*ins, *outs, *scr)

        @pl.when(step == steps - 1)
        def _():
            comm.finish(c_src, c_dst, *sems)

    res = pl.pallas_call(hosted, name=name, grid=grid, in_specs=list(in_specs) + [any_spec] * n_c,
                         out_specs=out_specs + [any_spec] * n_c, out_shape=out_shape + comm.dests,
                         scratch_shapes=list(scratch_shapes) + _job_sems(comm), input_output_aliases=aliases or {},
                         compiler_params=_params(("arbitrary",) * len(grid)))(*operands, *comm.sources)
    main = res[:n_out]
    return (main[0] if single else main), res[n_out:]


def comm_only(name, jobs):
    any_spec = pl.BlockSpec(memory_space=pl.ANY)
    n_all = sum(job.n for job in jobs)

    def body(*refs):
        srcs, dsts, sems = refs[:n_all], refs[n_all:2 * n_all], refs[2 * n_all:]
        parts, k0, s0 = [], 0, 0
        for job in jobs:
            parts.append((job, srcs[k0:k0 + job.n], dsts[k0:k0 + job.n], sems[s0:s0 + job.n_sems]))
            k0 += job.n
            s0 += job.n_sems
        for stage in ('start', 'relay', 'forward', 'finish'):
            for job, src, dst, sem in parts:
                getattr(job, stage)(src, dst, *sem)

    res = pl.pallas_call(body, name=name, in_specs=[any_spec] * n_all, out_specs=[any_spec] * n_all,
                         out_shape=[d for job in jobs for d in job.dests],
                         scratch_shapes=[s for job in jobs for s in _job_sems(job)],
                         )(*[s for job in jobs for s in job.sources])
    out, k0 = [], 0
    for job in jobs:
        out.append(res[k0:k0 + job.n])
        k0 += job.n
    return out


def _dot(a, b, dims):
    if a.dtype != BF16:
        a = a.astype(BF16)
    if b.dtype != BF16:
        b = b.astype(BF16)
    return lax.dot_general(a, b, (dims, ((), ())), preferred_element_type=F32)


def _sigmoid(v):
    return 1.0 / (1.0 + jnp.exp(-v))


def _ln_stats(z):
    mu = jnp.mean(z, axis=-1, keepdims=True)
    zc = z - mu
    var = jnp.mean(zc * zc, axis=-1, keepdims=True)
    return zc * lax.rsqrt(var + LN_EPS)


def _ln_fwd(z, g, b):
    return _ln_stats(z) * g + b


def _ln_bwd(dy, z, g):
    xhat = _ln_stats(z)
    mu = jnp.mean(z, axis=-1, keepdims=True)
    zc = z - mu
    rstd = lax.rsqrt(jnp.mean(zc * zc, axis=-1, keepdims=True) + LN_EPS)
    dxh = dy * g
    m1 = jnp.mean(dxh, axis=-1, keepdims=True)
    m2 = jnp.mean(dxh * xhat, axis=-1, keepdims=True)
    dz = rstd * (dxh - m1 - xhat * m2)
    return dz, jnp.sum(dy * xhat, axis=0, keepdims=True), jnp.sum(dy, axis=0, keepdims=True)


def _extra_spec(shape, kind, tm, tn, ij):
    if kind == 'tile':
        return pl.BlockSpec((tm, tn), lambda *g: (ij(g)[0], ij(g)[1]))
    if kind in ('row', 'rowacc'):
        return pl.BlockSpec((1, tn), lambda *g: (0, ij(g)[1]))
    if kind == 'tab':
        return pl.BlockSpec((tm, LANES), lambda *g: (ij(g)[0], 0))
    if kind == 'rows':
        return pl.BlockSpec((tm, shape[1]), lambda *g: (ij(g)[0], 0))
    raise ValueError(kind)


def mm_nn(name, pairs, extras, outs, epi, *, tm, tn, tk=None, comm=None):
    M = pairs[0][0].shape[0]
    N = pairs[0][1].shape[0] * pairs[0][1].shape[2]
    n_pairs = len(pairs)
    K0 = pairs[0][0].shape[1]
    tk = K0 if tk is None else tk
    nk = K0 // tk
    assert nk == 1 or n_pairs == 1
    assert M % tm == 0 and N % tn == 0 and K0 % tk == 0
    has_rowacc = any(kind == 'rowacc' for _, kind in outs)
    assert not has_rowacc or (N == tn and nk == 1)
    in_specs, operands, slabs = [], [], []
    for a, b, off in pairs:
        K = a.shape[1]
        ktile = K if n_pairs > 1 else tk
        n = b.shape[2]
        assert b.shape[1] == K
        in_specs.append(pl.BlockSpec((tm, ktile), lambda i, j, k: (i, k)))
        if tn <= n:
            assert n % tn == 0
            r = n // tn
            in_specs.append(pl.BlockSpec((None, ktile, tn),
                                         lambda i, j, k, r=r, off=off: ((j + off) // r, k, (j + off) % r)))
            slabs.append(0)
        else:
            assert tn % n == 0
            in_specs.append(pl.BlockSpec((tn // n, ktile, n), lambda i, j, k, off=off: (j + off, k, 0)))
            slabs.append(tn // n)
        operands += [a, b]
    ij = lambda g: (g[0], g[1])
    for arr, kind in extras:
        in_specs.append(_extra_spec(arr.shape, kind, tm, tn, ij))
        operands.append(arr)
    out_specs = [_extra_spec(o.shape, kind, tm, tn, ij) for o, kind in outs]
    n_ex, n_out = len(extras), len(outs)

    def pair_dot(ab, q):
        a = ab[2 * q][...]
        if not slabs[q]:
            return _dot(a, ab[2 * q + 1][...], ((1,), (0,)))
        return jnp.concatenate([_dot(a, ab[2 * q + 1][s], ((1,), (0,))) for s in range(slabs[q])], axis=1)

    def body(*refs):
        ab = refs[:2 * n_pairs]
        ex = refs[2 * n_pairs:2 * n_pairs + n_ex]
        out = refs[2 * n_pairs + n_ex:2 * n_pairs + n_ex + n_out]
        i = pl.program_id(0)
        if nk == 1:
            accs = [pair_dot(ab, q) for q in range(n_pairs)]
            epi(accs, ex, out, i)
        else:
            acc_ref = refs[-1]
            k = pl.program_id(2)

            @pl.when(k == 0)
            def _():
                acc_ref[...] = jnp.zeros_like(acc_ref)

            acc_ref[...] += pair_dot(ab, 0)

            @pl.when(k == nk - 1)
            def _():
                epi([acc_ref[...]], ex, out, i)

    scratch = [pltpu.VMEM((tm, tn), F32)] if nk > 1 else []
    sem = ("arbitrary",) * 3 if has_rowacc else ("parallel", "parallel", "arbitrary")
    return _call(body, comm, name=name, grid=(M // tm, N // tn, nk), in_specs=in_specs, out_specs=out_specs,
                 out_shape=[o for o, _ in outs], scratch_shapes=scratch, sem=sem, operands=operands)


def mm_nt(name, a, b, extras, outs, epi, *, tm, tko, tc, comm=None, a_pro=None):
    M, N = a.shape
    J, K, n = b.shape
    assert J * n == N and M % tm == 0 and K % tko == 0 and N % tc == 0
    nc = N // tc
    assert a_pro is None or (nc == 1 and tc <= n and K == tko == N)
    has_rowacc = any(kind == 'rowacc' for _, kind in outs)
    assert not has_rowacc or K == tko
    if tc <= n:
        assert n % tc == 0
        r = n // tc
        slabs = 0
        b_spec = pl.BlockSpec((None, tko, tc), lambda i, j, c: (c // r, j, c % r))
    else:
        assert tc % n == 0
        slabs = tc // n
        b_spec = pl.BlockSpec((slabs, tko, n), lambda i, j, c: (c, j, 0))
    n_ab = 2 if a_pro is None else 1
    in_specs = ([pl.BlockSpec((tm, tc), lambda i, j, c: (i, c))] if a_pro is None else []) + [b_spec]

    def nt_dot(a_ref, b_ref):
        if not slabs:
            return _dot(a_ref[...], b_ref[...], ((1,), (1,)))
        acc = _dot(a_ref[:, 0:n], b_ref[0], ((1,), (1,)))
        for s in range(1, slabs):
            acc = acc + _dot(a_ref[:, s * n:(s + 1) * n], b_ref[s], ((1,), (1,)))
        return acc

    ij = lambda g: (g[0], g[1])
    operands = [a, b] if a_pro is None else [b]
    for arr, kind in extras:
        in_specs.append(_extra_spec(arr.shape, kind, tm, tko, ij))
        operands.append(arr)
    out_specs = [_extra_spec(o.shape, kind, tm, tko, ij) for o, kind in outs]
    n_ex, n_out = len(extras), len(outs)

    def body(*refs):
        ex = refs[n_ab:n_ab + n_ex]
        out = refs[n_ab + n_ex:n_ab + n_ex + n_out]
        i = pl.program_id(0)
        if a_pro is not None:
            epi(_dot(a_pro(ex), refs[0][...], ((1,), (1,))), ex, out, i)
            return
        a_ref, b_ref = refs[:2]
        if nc == 1:
            epi(nt_dot(a_ref, b_ref), ex, out, i)
        else:
            acc_ref = refs[-1]
            c = pl.program_id(2)

            @pl.when(c == 0)
            def _():
                acc_ref[...] = jnp.zeros_like(acc_ref)

            acc_ref[...] += nt_dot(a_ref, b_ref)

            @pl.when(c == nc - 1)
            def _():
                epi(acc_ref[...], ex, out, i)

    scratch = [pltpu.VMEM((tm, tko), F32)] if nc > 1 else []
    sem = ("arbitrary",) * 3 if has_rowacc else ("parallel", "parallel", "arbitrary")
    return _call(body, comm, name=name, grid=(M // tm, K // tko, nc), in_specs=in_specs, out_specs=out_specs,
                 out_shape=[o for o, _ in outs], scratch_shapes=scratch, sem=sem, operands=operands)


def mm_tn(name, a, d, n, out_dtype, *, tm, tk, tn, comm=None):
    M, K = a.shape
    N = d.shape[1]
    assert d.shape[0] == M and N % n == 0 and N % tn == 0 and K % tk == 0 and M % tm == 0
    nm = M // tm
    if tn <= n:
        assert n % tn == 0
        r = n // tn
        slabs = 0
        o_spec = pl.BlockSpec((None, tk, tn), lambda kk, j, m: (j // r, kk, j % r))
    else:
        assert tn % n == 0
        slabs = tn // n
        o_spec = pl.BlockSpec((slabs, tk, n), lambda kk, j, m: (j, kk, 0))

    def write(o_ref, acc):
        if not slabs:
            o_ref[...] = acc.astype(o_ref.dtype)
        else:
            for s in range(slabs):
                o_ref[s] = acc[:, s * n:(s + 1) * n].astype(o_ref.dtype)

    def body(a_ref, d_ref, o_ref, *scratch):
        if nm == 1:
            write(o_ref, _dot(a_ref[...], d_ref[...], ((0,), (0,))))
            return
        acc_ref, = scratch
        m = pl.program_id(2)

        @pl.when(m == 0)
        def _():
            acc_ref[...] = jnp.zeros_like(acc_ref)

        acc_ref[...] += _dot(a_ref[...], d_ref[...], ((0,), (0,)))

        @pl.when(m == nm - 1)
        def _():
            write(o_ref, acc_ref[...])

    return _call(
        body, comm, name=name, grid=(K // tk, N // tn, nm),
        in_specs=[pl.BlockSpec((tm, tk), lambda kk, j, m: (m, kk)), pl.BlockSpec((tm, tn), lambda kk, j, m: (m, j))],
        out_specs=o_spec, out_shape=_sds((N // n, K, n), out_dtype),
        scratch_shapes=[pltpu.VMEM((tk, tn), F32)] if nm > 1 else [],
        sem=("parallel", "parallel", "arbitrary"), operands=[a, d])


def _init_or_add(ref, i, value):
    @pl.when(i == 0)
    def _():
        ref[...] = value

    @pl.when(i > 0)
    def _():
        ref[...] += value


def _rope_tables(T):
    pos = jnp.arange(T, dtype=F32)
    inv_freq = ROPE_THETA ** (-jnp.arange(0, ROPE_DIM, 2, dtype=F32) / ROPE_DIM)
    ang = pos[:, None] * inv_freq[None, :]
    cos, sin = jnp.cos(ang), jnp.sin(ang)
    ones = jnp.ones((T, HEAD_DIM - ROPE_DIM), F32)
    zeros = jnp.zeros((T, HEAD_DIM - ROPE_DIM), F32)
    zh = jnp.zeros((T, ROPE_HALF), F32)
    c_head = jnp.concatenate([cos, cos, ones], axis=1)
    s_up = jnp.concatenate([-sin, zh, zeros], axis=1)
    s_dn = jnp.concatenate([zh, sin, zeros], axis=1)
    rep = LANES // HEAD_DIM
    return tuple(jnp.tile(t, (1, rep)) for t in (c_head, s_up, s_dn))


def _rope_chunk(t, c, s_up, s_dn):
    return t * c + pltpu.roll(t, LANES - ROPE_HALF, 1) * s_up + pltpu.roll(t, ROPE_HALF, 1) * s_dn


def _rope_chunk_bwd(d, c, s_up, s_dn):
    return d * c + pltpu.roll(d * s_up, ROPE_HALF, 1) + pltpu.roll(d * s_dn, LANES - ROPE_HALF, 1)


def _rope_wide(t, c, s_up, s_dn, fn):
    chunks = [fn(t[:, q * LANES:(q + 1) * LANES], c, s_up, s_dn) for q in range(t.shape[1] // LANES)]
    return chunks[0] if len(chunks) == 1 else jnp.concatenate(chunks, axis=1)


def _taps_by_residue(first):
    groups = {}
    for o in range(first, first + CONV_WIDTH):
        groups.setdefault(o % SUBLANES, []).append(o)
    return sorted(groups.items())


def _shifted_taps(win_ref, span_ref, r0, res, offs, rb, ls):
    if res == 0:
        return [functools.partial(lambda o: win_ref[pl.ds(r0 + o, rb), ls], o) for o in offs]
    n = rb + offs[-1] - res
    span_ref[0:n, :] = win_ref[pl.ds(r0 + res, n), ls]
    return [functools.partial(lambda o: span_ref[pl.ds(o - res, rb), :], o) for o in offs]


def _halo_before_spec(tm, D):
    return pl.BlockSpec((CONV_HALO, D), lambda i: (jnp.maximum(i * (tm // CONV_HALO) - 1, 0), 0))


def dwconv_fwd(g, w_dw, b_dw, ln_g, ln_b, *, tm, comm=None):
    T, D = g.shape
    nl = D // LANES

    rb = min(CONV_ROWS, tm)

    def body(g_ref, gh_ref, w_ref, b_ref, lg_ref, lb_ref, c_ref, s_ref, win_ref, span_ref):
        i = pl.program_id(0)
        win_ref[0:CONV_HALO, :] = jnp.where(i > 0, gh_ref[...], 0.0)
        win_ref[CONV_HALO:, :] = g_ref[...]

        def lane_chunk(q, carry):
            ls = pl.ds(pl.multiple_of(q * LANES, LANES), LANES)
            for r0 in range(0, tm, rb):
                acc = jnp.broadcast_to(b_ref[:, ls], (rb, LANES))
                for res, offs in _taps_by_residue(CONV_HALO - (CONV_WIDTH - 1)):
                    taps = _shifted_taps(win_ref, span_ref, r0, res, offs, rb, ls)
                    for o, tap in zip(offs, taps):
                        k = o - (CONV_HALO - (CONV_WIDTH - 1))
                        acc = acc + tap() * w_ref[k:k + 1, ls]
                c_ref[pl.ds(r0, rb), ls] = acc
            return carry

        lax.fori_loop(0, nl, lane_chunk, 0)
        n = _ln_fwd(c_ref[...], lg_ref[...], lb_ref[...])
        s_ref[...] = (n * _sigmoid(n)).astype(s_ref.dtype)

    row = pl.BlockSpec((1, D), lambda i: (0, 0))
    return _call(
        body, comm, name="dwconv_fwd", grid=(T // tm,),
        in_specs=[pl.BlockSpec((tm, D), lambda i: (i, 0)), _halo_before_spec(tm, D),
                  pl.BlockSpec((CONV_HALO, D), lambda i: (0, 0)), row, row, row],
        out_specs=[pl.BlockSpec((tm, D), lambda i: (i, 0)), pl.BlockSpec((tm, D), lambda i: (i, 0))],
        out_shape=[_sds((T, D), F32), _sds((T, D), BF16)],
        scratch_shapes=[pltpu.VMEM((tm + CONV_HALO, D), F32), pltpu.VMEM((rb + CONV_HALO, LANES), F32)],
        sem=("parallel",), operands=[g, g, w_dw, b_dw, ln_g, ln_b])


def dwconv_bwd(dc, g, ha, hg, w_dw, *, tm, comm=None):
    T, D = g.shape
    nl = D // LANES
    last = T // CONV_HALO - 1
    nt = T // tm

    rb = min(CONV_ROWS, tm)

    def body(dc_ref, dcn_ref, g_ref, gh_ref, ha_ref, hg_ref, w_ref, dh_ref, dw_ref, dbdw_ref, dbin_ref,
             win_ref, dwin_ref, dg_ref, dwp_ref, span_ref):
        i = pl.program_id(0)
        win_ref[0:CONV_HALO, :] = jnp.where(i > 0, gh_ref[...], 0.0)
        win_ref[CONV_HALO:, :] = g_ref[...]
        dwin_ref[0:tm, :] = dc_ref[...]
        dwin_ref[tm:, :] = jnp.where(i < nt - 1, dcn_ref[...], 0.0)

        @pl.when(i == 0)
        def _():
            dwp_ref[...] = jnp.zeros_like(dwp_ref)

        first = CONV_HALO - (CONV_WIDTH - 1)

        def lane_chunk(q, carry):
            ls = pl.ds(pl.multiple_of(q * LANES, LANES), LANES)
            for r0 in range(0, tm, rb):
                acc = jnp.zeros((rb, LANES), F32)
                for res, offs in _taps_by_residue(0):
                    taps = _shifted_taps(dwin_ref, span_ref, r0, res, offs, rb, ls)
                    for o, tap in zip(offs, taps):
                        k = CONV_WIDTH - 1 - o
                        acc = acc + tap() * w_ref[k:k + 1, ls]
                dg_ref[pl.ds(r0, rb), ls] = acc
                dcv = dwin_ref[pl.ds(r0, rb), ls]
                for res, offs in _taps_by_residue(first):
                    taps = _shifted_taps(win_ref, span_ref, r0, res, offs, rb, ls)
                    for o, tap in zip(offs, taps):
                        k = o - first
                        prod = dcv * tap()
                        dwp_ref[k, :, ls] += jnp.sum(prod.reshape(rb // SUBLANES, SUBLANES, LANES), axis=0)
            return carry

        lax.fori_loop(0, nl, lane_chunk, 0)

        @pl.when(i == nt - 1)
        def _():
            for k in range(CONV_WIDTH):
                dw_ref[k:k + 1, :] = jnp.sum(dwp_ref[k], axis=0, keepdims=True)
            dw_ref[CONV_WIDTH:, :] = jnp.zeros((CONV_HALO - CONV_WIDTH, D), F32)
        dg = dg_ref[...]
        ha = ha_ref[...].astype(F32)
        sg = _sigmoid(hg_ref[...].astype(F32))
        d_ha = dg * sg
        d_hg = dg * ha * sg * (1.0 - sg)
        dh_ref[:, 0:D] = d_ha.astype(dh_ref.dtype)
        dh_ref[:, D:] = d_hg.astype(dh_ref.dtype)
        _init_or_add(dbdw_ref, i, jnp.sum(dc_ref[...], axis=0, keepdims=True))
        _init_or_add(dbin_ref, i, jnp.concatenate([jnp.sum(d_ha, axis=0, keepdims=True),
                                                   jnp.sum(d_hg, axis=0, keepdims=True)], axis=1))

    tile = pl.BlockSpec((tm, D), lambda i: (i, 0))
    return _call(
        body, comm, name="dwconv_bwd", grid=(nt,),
        in_specs=[tile,
                  pl.BlockSpec((CONV_HALO, D), lambda i: (jnp.minimum((i + 1) * (tm // CONV_HALO), last), 0)),
                  tile, _halo_before_spec(tm, D),
                  tile, tile, pl.BlockSpec((CONV_HALO, D), lambda i: (0, 0))],
        out_specs=[pl.BlockSpec((tm, 2 * D), lambda i: (i, 0)), pl.BlockSpec((CONV_HALO, D), lambda i: (0, 0)),
                   pl.BlockSpec((1, D), lambda i: (0, 0)), pl.BlockSpec((1, 2 * D), lambda i: (0, 0))],
        out_shape=[_sds((T, 2 * D), BF16), _sds((CONV_HALO, D), F32), _sds((1, D), F32), _sds((1, 2 * D), F32)],
        scratch_shapes=[pltpu.VMEM((tm + CONV_HALO, D), F32), pltpu.VMEM((tm + CONV_HALO, D), F32),
                        pltpu.VMEM((tm, D), F32), pltpu.VMEM((CONV_WIDTH, SUBLANES, D), F32),
                        pltpu.VMEM((rb + CONV_HALO, LANES), F32)],
        sem=("arbitrary",), operands=[dc, dc, g, g, ha, hg, w_dw])


def _attn_specs(HD, W):
    B = ATT_BLOCK
    return [pl.BlockSpec((B, HD), lambda n: (n, 0)),
            pl.BlockSpec((B, 4 * W), lambda n: (n, 0)),
            pl.BlockSpec((B, 4 * W), lambda n: (jnp.maximum(n - 1, 0), 0))]


def _band_mask(n):
    r = lax.broadcasted_iota(jnp.int32, (ATT_BLOCK, 2 * ATT_BLOCK), 0)
    j = lax.broadcasted_iota(jnp.int32, (ATT_BLOCK, 2 * ATT_BLOCK), 1)
    return (j > r) & (j <= r + ATT_BLOCK) & ((n > 0) | (j >= ATT_BLOCK))


def _band(kvc_ref, kvp_ref, part, g, parity, W):
    lanes = slice((2 * part + parity) * W + g * LANES, (2 * part + parity) * W + (g + 1) * LANES)
    return jnp.concatenate([kvp_ref[:, lanes], kvc_ref[:, lanes]], axis=0)


def _half_mask(parity):
    lane = lax.broadcasted_iota(jnp.int32, (1, LANES), 1)
    return (lane < HEAD_DIM) if parity == 0 else (lane >= HEAD_DIM)


def widen_kv(k, v, out_ref, n_kv):
    W = n_kv * LANES
    low = _half_mask(0)
    for part, src in enumerate((k, v)):
        for cg in range(n_kv * HEAD_DIM // LANES):
            chunk = src[:, cg * LANES:(cg + 1) * LANES]
            swapped = pltpu.roll(chunk, HEAD_DIM, 1)
            for g, lo, hi in ((2 * cg, chunk, swapped), (2 * cg + 1, swapped, chunk)):
                base = 2 * part * W + g * LANES
                out_ref[:, base:base + LANES] = jnp.where(low, lo, 0.0).astype(out_ref.dtype)
                out_ref[:, base + W:base + W + LANES] = jnp.where(low, 0.0, hi).astype(out_ref.dtype)


def attn_fwd(q, kvx, sinks, n_heads, n_kv, comm=None):
    T, HD = q.shape
    W = n_kv * LANES
    B = ATT_BLOCK
    chunks_per_group = n_heads // n_kv // 2
    scale = 1.0 / math.sqrt(HEAD_DIM)

    def body(q_ref, kvc_ref, kvp_ref, sink_ref, o_ref, lse_ref):
        n = pl.program_id(0)
        mask = jnp.tile(_band_mask(n), (n_heads, 1))
        s = jnp.concatenate(
            [_dot(q_ref[:, (h // 2) * LANES:(h // 2 + 1) * LANES],
                  _band(kvc_ref, kvp_ref, 0, h // 2 // chunks_per_group, h % 2, W), ((1,), (1,)))
             for h in range(n_heads)], axis=0)
        sink = jnp.concatenate([jnp.broadcast_to(sink_ref[:, h:h + 1], (B, 1)) for h in range(n_heads)], axis=0)
        s = jnp.where(mask, s * scale, MASK_VALUE)
        m = jnp.maximum(jnp.max(s, axis=-1, keepdims=True), sink)
        e = jnp.exp(s - m)
        total = _dot(e, jnp.ones((2 * B, LANES), BF16), ((1,), (0,))) + jnp.exp(sink - m)
        lse = m + jnp.log(total[:, 0:1])
        inv = 1.0 / total
        probs = (e * jnp.concatenate([inv, inv], axis=1)).astype(BF16)
        for c in range(n_heads // 2):
            g = c // chunks_per_group
            out = (_dot(probs[2 * c * B:(2 * c + 1) * B], _band(kvc_ref, kvp_ref, 1, g, 0, W), ((1,), (0,)))
                   + _dot(probs[(2 * c + 1) * B:(2 * c + 2) * B], _band(kvc_ref, kvp_ref, 1, g, 1, W), ((1,), (0,))))
            o_ref[:, c * LANES:(c + 1) * LANES] = out.astype(o_ref.dtype)
        lse_ref[...] = jnp.concatenate([lse[h * B:(h + 1) * B] for h in range(n_heads)], axis=1)

    return _call(
        body, comm, name="attn_fwd", grid=(T // B,),
        in_specs=_attn_specs(HD, W) + [pl.BlockSpec((1, n_heads), lambda n: (0, 0))],
        out_specs=[pl.BlockSpec((B, HD), lambda n: (n, 0)), pl.BlockSpec((B, n_heads), lambda n: (n, 0))],
        out_shape=[_sds((T, HD), BF16), _sds((T, n_heads), F32)],
        sem=("parallel",), operands=[q, kvx, kvx, sinks])


def attn_bwd(q, kvx, do, lse, sinks, rope, n_heads, n_kv, comm=None):
    T, HD = q.shape
    KVD = n_kv * HEAD_DIM
    W = n_kv * LANES
    B = ATT_BLOCK
    chunks_per_group = n_heads // n_kv // 2
    scale = 1.0 / math.sqrt(HEAD_DIM)
    nb = T // B

    def body(q_ref, kvc_ref, kvp_ref, do_ref, lse_ref, sink_ref, c_ref, su_ref, sd_ref, dq_ref, dkv_ref, dsink_ref):
        n = pl.program_id(0)

        @pl.when(n == 0)
        def _():
            dkv_ref[...] = jnp.zeros_like(dkv_ref)
            dsink_ref[...] = jnp.zeros_like(dsink_ref)

        def chunk(ref, h):
            return ref[:, (h // 2) * LANES:(h // 2 + 1) * LANES]

        def band(part, h):
            return _band(kvc_ref, kvp_ref, part, h // 2 // chunks_per_group, h % 2, W)

        def stack(per_head):
            return jnp.concatenate([per_head(h) for h in range(n_heads)], axis=0)

        mask = jnp.tile(_band_mask(n), (n_heads, 1))
        s = stack(lambda h: _dot(chunk(q_ref, h), band(0, h), ((1,), (1,))))
        dp = stack(lambda h: _dot(chunk(do_ref, h), band(1, h), ((1,), (1,))))
        lse = stack(lambda h: lse_ref[:, h:h + 1])
        sink = stack(lambda h: jnp.broadcast_to(sink_ref[:, h:h + 1], (B, 1)))
        probs = jnp.exp(jnp.where(mask, s * scale, MASK_VALUE) - lse)
        delta = jnp.sum(probs * dp, axis=-1, keepdims=True)
        ds = (probs * (dp - delta) * scale).astype(BF16)
        probs = probs.astype(BF16)
        sink_term = jnp.exp(sink - lse) * delta
        dsk = [-jnp.sum(sink_term[h * B:(h + 1) * B], axis=0, keepdims=True) for h in range(n_heads)]

        dk_wide, dv_wide = [None] * n_kv, [None] * n_kv
        for c in range(n_heads // 2):
            g = c // chunks_per_group
            dq2 = None
            for h in (2 * c, 2 * c + 1):
                half = _half_mask(h % 2)
                q2, do2 = chunk(q_ref, h), chunk(do_ref, h)
                ds_h, p_h = ds[h * B:(h + 1) * B], probs[h * B:(h + 1) * B]
                part = _dot(ds_h, band(0, h), ((1,), (0,)))
                dq2 = part if dq2 is None else dq2 + part
                dk_h = _dot(ds_h, jnp.where(half, q2, jnp.zeros_like(q2)), ((0,), (0,)))
                dv_h = _dot(p_h, jnp.where(half, do2, jnp.zeros_like(do2)), ((0,), (0,)))
                dk_wide[g] = dk_h if dk_wide[g] is None else dk_wide[g] + dk_h
                dv_wide[g] = dv_h if dv_wide[g] is None else dv_wide[g] + dv_h
            dq_ref[:, c * LANES:(c + 1) * LANES] = _rope_chunk_bwd(
                dq2, c_ref[...], su_ref[...], sd_ref[...]).astype(dq_ref.dtype)

        def fold(wide):
            low = _half_mask(0)
            both = [w + pltpu.roll(w, HEAD_DIM, 1) for w in wide]
            return jnp.concatenate([jnp.where(low, both[2 * cg], both[2 * cg + 1]) for cg in range(n_kv // 2)], axis=1)

        dkv = jnp.concatenate([fold(dk_wide), fold(dv_wide)], axis=1)
        prev = pl.ds(pl.multiple_of(jnp.maximum(n - 1, 0) * B, B), B)
        cur = pl.ds(pl.multiple_of(n * B, B), B)
        dkv_ref[prev, :] += dkv[0:B, :]
        dkv_ref[cur, :] += dkv[B:, :]
        dsink_ref[...] += jnp.concatenate(dsk, axis=1)

    tab = pl.BlockSpec((B, LANES), lambda n: (n, 0))
    return _call(
        body, comm, name="attn_bwd", grid=(nb,),
        in_specs=_attn_specs(HD, W) + [pl.BlockSpec((B, HD), lambda n: (n, 0)),
                                       pl.BlockSpec((B, n_heads), lambda n: (n, 0)),
                                       pl.BlockSpec((1, n_heads), lambda n: (0, 0)), tab, tab, tab],
        out_specs=[pl.BlockSpec((B, HD), lambda n: (n, 0)), pl.BlockSpec((T, 2 * KVD), lambda n: (0, 0)),
                   pl.BlockSpec((1, n_heads), lambda n: (0, 0))],
        out_shape=[_sds((T, HD + 2 * KVD), BF16), _sds((T, 2 * KVD), F32), _sds((1, n_heads), F32)],
        sem=("arbitrary",), operands=[q, kvx, kvx, do, lse, sinks, *rope])


def dkv_finish(d_qkv, dkv, rope, HD, KVD, *, tm):
    T = dkv.shape[0]
    kv_col = HD // (2 * KVD)

    def body(alias_ref, dkv_ref, c_ref, su_ref, sd_ref, o_ref):
        del alias_ref
        dk = _rope_wide(dkv_ref[:, 0:KVD], c_ref[...], su_ref[...], sd_ref[...], _rope_chunk_bwd)
        o_ref[:, 0:KVD] = dk.astype(o_ref.dtype)
        o_ref[:, KVD:] = dkv_ref[:, KVD:].astype(o_ref.dtype)

    tab = pl.BlockSpec((tm, LANES), lambda i: (i, 0))
    return pl.pallas_call(
        body, name="dkv_finish", grid=(T // tm,),
        in_specs=[pl.BlockSpec(memory_space=pl.ANY), pl.BlockSpec((tm, 2 * KVD), lambda i: (i, 0)), tab, tab, tab],
        out_specs=pl.BlockSpec((tm, 2 * KVD), lambda i: (i, kv_col)),
        out_shape=_sds(d_qkv.shape, d_qkv.dtype), input_output_aliases={0: 0},
        compiler_params=_params(("parallel",)))(d_qkv, dkv, *rope)


def ple_bwd_elem(d_out, pp, gg, *, tm):
    T, D = d_out.shape

    def body(d_ref, pp_ref, gg_ref, dpp_ref, dgg_ref):
        d = d_ref[...]
        sg = _sigmoid(gg_ref[...].astype(F32))
        dpp_ref[...] = (d * sg).astype(dpp_ref.dtype)
        dgg_ref[...] = (d * pp_ref[...].astype(F32) * sg * (1.0 - sg)).astype(dgg_ref.dtype)

    tile = pl.BlockSpec((tm, D), lambda i: (i, 0))
    return pl.pallas_call(
        body, name="ple_bwd_elem", grid=(T // tm,), in_specs=[tile, tile, tile], out_specs=[tile, tile],
        out_shape=[_sds((T, D), BF16), _sds((T, D), BF16)], compiler_params=_params(("parallel",)))(d_out, pp, gg)


def chip_sum(name, g, p_sib, *, ta, part=0, parts=1):
    _, a, b = g.shape
    rows = a // parts
    assert a % parts == 0 and rows % ta == 0
    first = part * (rows // ta)

    def body(core_ref, g_ref, p_ref, o_ref):
        del core_ref
        o_ref[...] = (g_ref[...].astype(F32) + p_ref[...].astype(F32)).astype(o_ref.dtype)

    my_core = lax.axis_index("c").astype(jnp.int32).reshape(1)
    return pl.pallas_call(
        body, name=name, out_shape=_sds((N_CHIPS, rows, b), g.dtype),
        grid_spec=pltpu.PrefetchScalarGridSpec(
            num_scalar_prefetch=1, grid=(N_CHIPS, rows // ta),
            in_specs=[pl.BlockSpec((None, None, ta, b), lambda q, i, core: (q, core[0], first + i, 0)),
                      pl.BlockSpec((None, ta, b), lambda q, i, core: (q, first + i, 0))],
            out_specs=pl.BlockSpec((None, ta, b), lambda q, i, core: (q, i, 0))),
        compiler_params=_params(("arbitrary", "arbitrary")))(my_core, g.reshape(N_CHIPS, 2, a, b), p_sib)


def adamw(name, recvs, w, m, v, *, ta):
    L, a, b = w.shape
    n_terms = recvs[0].shape[0]
    assert a % ta == 0 and len(recvs) == L
    c1 = 1.0 - ADAM_B1 ** ADAM_STEP
    c2 = 1.0 - ADAM_B2 ** ADAM_STEP

    def body(*refs):
        r_refs = refs[:L]
        w_ref, m_ref, v_ref, g_ref, d_ref, nm_ref, nv_ref = refs[L:]
        layer = pl.program_id(0)
        for l in range(L):
            @pl.when(layer == l)
            def _(r_ref=r_refs[l]):
                g = r_ref[0].astype(F32)
                for s in range(1, n_terms):
                    g = g + r_ref[s].astype(F32)
                nm = ADAM_B1 * m_ref[...] + (1.0 - ADAM_B1) * g
                nv = ADAM_B2 * v_ref[...] + (1.0 - ADAM_B2) * jnp.square(g)
                m_hat = nm / c1
                v_hat = nv / c2
                g_ref[...] = g
                d_ref[...] = -ADAM_LR * (m_hat / (jnp.sqrt(v_hat) + ADAM_EPS) + ADAM_WD * w_ref[...])
                nm_ref[...] = nm
                nv_ref[...] = nv

    blk = pl.BlockSpec((None, ta, b), lambda l, i: (l, i, 0))
    out = _sds((L, a, b), F32)
    r_specs = [pl.BlockSpec((n_terms, ta, b), lambda l, i, ll=ll: (0, jnp.where(l == ll, i, 0), 0)) for ll in range(L)]
    return pl.pallas_call(
        body, name=name, grid=(L, a // ta), in_specs=r_specs + [blk, blk, blk],
        out_specs=[blk, blk, blk, blk], out_shape=[out, out, out, out],
        compiler_params=_params(("arbitrary", "arbitrary")))(*recvs, w, m, v)


def _pack_rows(parts):
    out = []
    for arr, rows in parts:
        arr = arr.reshape(-1, LANES).astype(F32)
        out.append(jnp.pad(arr, ((0, rows - arr.shape[0]), (0, 0))))
    return jnp.concatenate(out, axis=0)


def _small_rows(a, prefix):
    return _pack_rows([(a[prefix + name], rows) for name, rows in SMALL_SHARDED])


def _unpack_small(packed, a):
    out, r0 = {}, 0
    for name, rows in SMALL_SHARDED:
        shape = a[name].shape
        used = math.prod(shape) // LANES
        out[name] = packed[r0:r0 + used].reshape(shape)
        r0 += rows
    return out


def _rep_rows(a, prefix):
    parts = []
    for name in REPLICATED:
        arr = a[prefix + name]
        if arr.size % LANES:
            arr = jnp.pad(arr.reshape(1, -1), ((0, 0), (0, LANES - arr.size % LANES)))
        rows = -(-arr.size // LANES)
        parts.append((arr, -(-rows // SUBLANES) * SUBLANES))
    return _pack_rows(parts)


def _unpack_rep(packed, a):
    out, r0 = {}, 0
    for name in REPLICATED:
        shape = a[name].shape
        size = math.prod(shape)
        rows = -(-size // LANES)
        out[name] = packed[r0:r0 + rows].reshape(-1)[:size].reshape(shape)
        r0 += -(-rows // SUBLANES) * SUBLANES
    return out


def _step(a):
    x = a['x'][0]
    T, D = x.shape
    tgt = a['loss_target'][0]
    p_in = [a['p'][i, 0] for i in range(DEPTH)]
    PLE = p_in[0].shape[1]
    n_heads = a['attn_sinks'].shape[1]
    HD = n_heads * HEAD_DIM
    KVD = a['kv_w_k'].shape[1]
    n_kv = KVD // HEAD_DIM
    F = a['mlp_w_down'].shape[1] * N_DEV
    tm = min(TOKEN_TILE, T)
    tm2 = min(2 * TOKEN_TILE, T)
    tmc = min(TOKEN_TILE, T)
    tw = 512
    alpha = DEEPNORM_ALPHA
    xb = x.astype(BF16)
    p_b = [p.astype(BF16) for p in p_in]

    def shard3(w):
        return w.reshape((1,) + w.shape) if w.ndim == 2 else w

    def gather(*specs):
        items = []
        for spec in specs:
            w = shard3(a[spec[0]])[spec[1]]
            if len(spec) == 3:
                rows = w.shape[0] // 2
                w = w[spec[2] * rows:(spec[2] + 1) * rows]
            items.append(w.astype(BF16))
        return _GatherJob(items)

    def join_halves(lo, hi):
        return jnp.concatenate([lo, hi], axis=1)

    (W_in, small_full), = comm_only("gather_first",
                                    [_GatherJob([a['conv_w_in'][0].astype(BF16), _small_rows(a, '')])])
    r0, small = 0, {}
    for name, rows in SMALL_SHARDED:
        small[name] = small_full[:, r0:r0 + rows]
        r0 += rows
    b_in = small['conv_b_in'][:, 0:2 * D // N_DEV // LANES].reshape(1, 2 * D)
    w_dw = jnp.transpose(small['conv_w_dw'], (1, 0, 2)).reshape(CONV_HALO, D)
    b_dw, cln_g, cln_b, b_out = (small[nm][:, 0].reshape(1, D) for nm in
                                 ('conv_b_dw', 'conv_ln_g', 'conv_ln_b', 'conv_b_out'))
    W_up, W_down, W_proj, W_gate = {}, {}, {}, {}
    mix_g, mix_b, mlp_g, mlp_b = a['mix_ln_g'], a['mix_ln_b'], a['mlp_ln_g'], a['mlp_ln_b']
    rope = _rope_tables(T)

    def set_ple_weights(li, g_proj, g_gate):
        W_proj[li] = jnp.transpose(g_proj, (1, 0, 2)).reshape(1, PLE, D)
        W_gate[li] = g_gate.reshape(1, D, D)

    def row(v, i):
        return v[i:i + 1]

    def res_ln_epi(coef):
        def epi(accs, ex, out, i):
            acc = accs[0] if isinstance(accs, list) else accs
            n_ex = len(ex)
            res_ref, g_ref, b_ref = ex[n_ex - 3], ex[n_ex - 2], ex[n_ex - 1]
            z = coef * res_ref[...] + acc
            if n_ex == 4:
                z = z + ex[0][...]
            out[0][...] = z
            xo = _ln_fwd(z, g_ref[...], b_ref[...])
            out[1][...] = xo
            out[2][...] = xo.astype(BF16)
        return epi

    res_ln_outs = [(_sds((T, D), F32), 'tile'), (_sds((T, D), F32), 'tile'), (_sds((T, D), BF16), 'tile')]

    def mlp_fwd(li, xin, xin_b, down_comm=None):
        def up_epi(accs, ex, out, i):
            u = accs[0]
            out[0][...] = u.astype(BF16)
            out[1][...] = jnp.square(jnp.maximum(u, 0.0)).astype(BF16)
        (u, act), got_up = mm_nn(f"mlp_up_{li}", [(xin_b, W_up[li], 0)], [], [(_sds((T, F), BF16), 'tile')] * 2,
                                 up_epi, tm=tm2, tn=min(1024, F), comm=gather(('mlp_w_down', li)))
        W_down[li] = got_up[0].reshape(1, F, D)
        res = mm_nn(f"mlp_down_{li}", [(act, W_down[li], 0)],
                    [(xin, 'tile'), (row(mlp_g, li), 'row'), (row(mlp_b, li), 'row')],
                    res_ln_outs, res_ln_epi(alpha), tm=tm, tn=D, tk=F, comm=down_comm)
        (z, xo, xo_b), got_down = res if down_comm is not None else (res, ())
        return u, act, z, xo, xo_b, got_down

    def ple_fwd(li, xin, xin_b, with_loss, comm=None):
        def epi(accs, ex, out, i):
            pp, gg = accs
            xo = ex[0][...] + pp * _sigmoid(gg)
            out[1][...] = pp.astype(BF16)
            out[2][...] = gg.astype(BF16)
            if with_loss:
                err = xo - ex[1][...]
                out[0][...] = err * (1.0 / D)
                _init_or_add(out[3], i, jnp.sum(err * err, axis=0, keepdims=True) * (0.5 / D))
            else:
                out[0][...] = xo
                out[3][...] = xo.astype(BF16)
        extras = [(xin, 'tile')] + ([(tgt, 'tile')] if with_loss else [])
        outs = [(_sds((T, D), F32), 'tile'), (_sds((T, D), BF16), 'tile'), (_sds((T, D), BF16), 'tile')]
        outs.append((_sds((1, D), F32), 'rowacc') if with_loss else (_sds((T, D), BF16), 'tile'))
        return mm_nn(f"ple_{li}", [(p_b[li], W_proj[li], 0), (xin_b, W_gate[li], 0)], extras, outs, epi, tm=tm, tn=D,
                     comm=comm)

    assert D // N_DEV == LANES
    (g0, ha0, hg0), got = _glu(xb, W_in, b_in, T, D, tm,
                               gather(('conv_w_out', 0), ('ple_w_proj', 0), ('ple_w_gate', 0)))
    W_out = got[0].reshape(1, D, D)
    set_ple_weights(0, got[1], got[2])
    (c0, s0), (W_up[0],) = dwconv_fwd(g0, w_dw, b_dw, cln_g, cln_b, tm=tmc, comm=gather(('mlp_w_up', 0)))
    z1, x1, x1b = mm_nn("conv_out", [(s0, W_out, 0)],
                        [(b_out, 'row'), (x, 'tile'), (row(mix_g, 0), 'row'), (row(mix_b, 0), 'row')],
                        res_ln_outs, res_ln_epi(alpha), tm=tm, tn=D)
    u0, act0, z2, x2, x2b, got = mlp_fwd(
        0, x1, x1b, down_comm=gather(('attn_w_q', 0), ('kv_w_k', 0), ('kv_w_v', 0), ('attn_w_o', 0),
                                     ('ple_w_proj', 1), ('ple_w_gate', 1)))
    W_qkv = jnp.concatenate([got[0].reshape(D, HD), got[1].reshape(D, KVD), got[2].reshape(D, KVD)], axis=1)[None]
    W_o = got[3].reshape(1, HD, D)
    set_ple_weights(1, got[4], got[5])
    x3, pp0, gg0, x3b = ple_fwd(0, x2, x2b, False)

    def qkv_epi(accs, ex, out, i):
        t = accs[0]
        c, su, sd = ex[0][...], ex[1][...], ex[2][...]
        out[0][...] = _rope_wide(t[:, 0:HD], c, su, sd, _rope_chunk).astype(BF16)
        widen_kv(_rope_wide(t[:, HD:HD + KVD], c, su, sd, _rope_chunk), t[:, HD + KVD:], out[1], n_kv)
    NQ = HD + 2 * KVD
    q1, kvx1 = mm_nn("qkv_rope", [(x3b, W_qkv, 0)], [(t, 'tab') for t in rope],
                     [(_sds((T, HD), BF16), 'rows'), (_sds((T, 4 * n_kv * LANES), BF16), 'rows')], qkv_epi,
                     tm=tm, tn=NQ)
    (o1, lse1), (W_up[1],) = attn_fwd(q1, kvx1, a['attn_sinks'], n_heads, n_kv, comm=gather(('mlp_w_up', 1)))
    z3, x4, x4b = mm_nn("attn_out", [(o1, W_o, 0)], [(x3, 'tile'), (row(mix_g, 1), 'row'), (row(mix_b, 1), 'row')],
                        res_ln_outs, res_ln_epi(alpha), tm=tm, tn=D)
    u1, act1, z4, x5, x5b, _ = mlp_fwd(1, x4, x4b)
    dy, pp1, gg1, loss_row = ple_fwd(1, x5, x5b, True)
    loss_local = jnp.sum(loss_row)

    grads = {}

    def ln_bwd_epi(coef, with_colsum):
        def epi(acc, ex, out, i):
            d_x = acc + coef * ex[0][...]
            dz, dg, db = _ln_bwd(d_x, ex[1][...], ex[2][...])
            out[0][...] = dz
            out[1][...] = dz.astype(BF16)
            _init_or_add(out[2], i, dg)
            _init_or_add(out[3], i, db)
            if with_colsum:
                _init_or_add(out[4], i, jnp.sum(dz, axis=0, keepdims=True))
        return epi

    def ln_bwd_outs(with_colsum):
        outs = [(_sds((T, D), F32), 'tile'), (_sds((T, D), BF16), 'tile'), (_sds((1, D), F32), 'rowacc'),
                (_sds((1, D), F32), 'rowacc')]
        return outs + ([(_sds((1, D), F32), 'rowacc')] if with_colsum else [])

    def ple_bwd(li, d_out, xin, pp, gg, z_mlp, pair_specs=None):
        side = {}

        def gate_grads(ex):
            d = ex[0][...]
            sg = _sigmoid(ex[4][...].astype(F32))
            side['d_pp'] = (d * sg).astype(BF16)
            side['d_gg'] = (d * ex[3][...].astype(F32) * sg * (1.0 - sg)).astype(BF16)
            return side['d_gg']

        inner = ln_bwd_epi(1.0, False)

        def epi(acc, ex, out, i):
            inner(acc, ex, out, i)
            out[4][...] = side['d_pp']
            out[5][...] = side['d_gg']

        job = pair_stage(*pair_specs) if pair_specs else None
        res = mm_nt(f"ple_dx_{li}", _sds((T, D), BF16), W_gate[li],
                    [(d_out, 'tile'), (z_mlp, 'tile'), (row(mlp_g, li), 'row'), (pp, 'tile'), (gg, 'tile')],
                    ln_bwd_outs(False) + [(_sds((T, D), BF16), 'tile')] * 2, epi, tm=tm, tko=D, tc=D, comm=job,
                    a_pro=gate_grads)
        (dz, dzb, dg, db, d_pp, d_gg), got = res if job is not None else (res, None)
        grads[('mlp_ln_g', li)], grads[('mlp_ln_b', li)] = dg, db
        if job is not None:
            pair_done(job, got)
        grads[('ple_w_proj', li)] = mm_tn(f"d_ple_proj_{li}", p_b[li], d_pp, D, BF16, tm=T, tk=PLE, tn=tw)
        grads[('ple_w_gate', li)] = mm_tn(f"d_ple_gate_{li}", xin, d_gg, D, BF16, tm=T, tk=tw, tn=tw)
        return dz, dzb

    recv = {}
    wqkv_cols = {'attn_w_q': (0, HD), 'kv_w_k': (HD, HD + KVD), 'kv_w_v': (HD + KVD, NQ)}

    def piece(name, li):
        if name == 'conv_w_in':
            return grads['conv_w_in']
        if name == 'mlp_w_up':
            return grads[('mlp_w_up', li)]
        if name == 'ple_w_proj':
            return jnp.transpose(grads[('ple_w_proj', li)][0].reshape(PLE, N_DEV, D // N_DEV), (1, 0, 2))
        if name in wqkv_cols:
            g = grads['w_qkv'][:, wqkv_cols[name][0]:wqkv_cols[name][1]]
        else:
            g = grads[name] if name in grads else grads[(name, li)]
            g = g[0]
        return g.reshape(N_DEV, g.shape[0] // N_DEV, g.shape[1])

    def pair_stage(*specs):
        job = _PairJob([piece(nm, li) for nm, li in specs])
        job.specs = specs
        return job

    sums = {}
    halved = ()

    def pair_done(pair_job, got):
        for (nm, li), mine, theirs in zip(pair_job.specs, pair_job.sources, got):
            if nm in halved:
                for h in (0, 1):
                    sums[(nm, li, h)] = chip_sum(f"chip_sum_{nm}_{li}_{h}", mine, theirs,
                                                 ta=min(256, mine.shape[1] // 2), part=h, parts=2)
            else:
                sums[(nm, li)] = chip_sum(f"chip_sum_{nm}_{li}", mine, theirs, ta=min(512, mine.shape[1]))

    def chip_stage(*keys):
        job = _ChipJob([sums[k] for k in keys])
        job.specs = keys
        return job

    def hosted(res, job):
        main, got = res
        for spec, r in zip(job.specs, got):
            recv[spec] = r
        return main

    def mlp_bwd(li, dz, dzb, xin, u, act, z_mix, with_colsum, pair_specs, down_keys=None):
        if down_keys:
            job = chip_stage(*down_keys)
            grads[('mlp_w_down', li)] = hosted(
                mm_tn(f"d_mlp_down_{li}", act, dzb, D, BF16, tm=T, tk=tw, tn=tw, comm=job), job)
        else:
            grads[('mlp_w_down', li)] = mm_tn(f"d_mlp_down_{li}", act, dzb, D, BF16, tm=T, tk=tw, tn=tw)

        def du_epi(acc, ex, out, i):
            out[0][...] = (acc * (2.0 * jnp.maximum(ex[0][...].astype(F32), 0.0))).astype(BF16)
        pair = pair_stage(*pair_specs)
        (du,), got = mm_nt(f"mlp_du_{li}", dzb, W_down[li], [(u, 'tile')], [(_sds((T, F), BF16), 'tile')], du_epi,
                           tm=tm2, tko=min(1024, F), tc=D, comm=pair)
        pair_done(pair, got)
        grads[('mlp_w_up', li)] = mm_tn(f"d_mlp_up_{li}", xin, du, F // N_DEV, BF16, tm=T, tk=tw, tn=tw)
        job = chip_stage(*pair_specs)
        res = hosted(mm_nt(f"mlp_dx_{li}", du, W_up[li], [(dz, 'tile'), (z_mix, 'tile'), (row(mix_g, li), 'row')],
                           ln_bwd_outs(with_colsum), ln_bwd_epi(alpha, with_colsum), tm=tm, tko=D, tc=F, comm=job), job)
        grads[('mix_ln_g', li)], grads[('mix_ln_b', li)] = res[2], res[3]
        return res

    dz4, dz4b = ple_bwd(1, dy, x5b, pp1, gg1, z4)
    dz3, dz3b, _, _ = mlp_bwd(1, dz4, dz4b, x4b, u1, act1, z3, False,
                              [('mlp_w_down', 1), ('ple_w_gate', 1), ('ple_w_proj', 1)])
    grads['attn_w_o'] = mm_tn("d_attn_wo", o1, dz3b, D, BF16, tm=T, tk=tw, tn=tw)

    def do_epi(acc, ex, out, i):
        out[0][...] = acc.astype(BF16)
    pair = pair_stage(('mlp_w_up', 1), ('attn_w_o', 0))
    (do1,), got = mm_nt("attn_do", dz3b, W_o, [], [(_sds((T, HD), BF16), 'tile')], do_epi, tm=tm, tko=HD, tc=D,
                        comm=pair)
    pair_done(pair, got)
    job = chip_stage(('mlp_w_up', 1), ('attn_w_o', 0))
    d_qkv, dkv, d_sinks = hosted(attn_bwd(q1, kvx1, do1, lse1, a['attn_sinks'], rope, n_heads, n_kv, comm=job), job)
    d_qkv = dkv_finish(d_qkv, dkv, rope, HD, KVD, tm=tm)
    grads['w_qkv'] = mm_tn("d_wqkv", x3b, d_qkv, NQ, BF16, tm=T, tk=tw, tn=2 * KVD)[0]

    def dx3_epi(acc, ex, out, i):
        out[0][...] = acc + alpha * ex[0][...]
    dx3, = mm_nt("attn_dx", d_qkv, W_qkv, [(dz3, 'tile')], [(_sds((T, D), F32), 'tile')], dx3_epi,
                 tm=tm, tko=D, tc=NQ)

    dz2, dz2b = ple_bwd(0, dx3, x2b, pp0, gg0, z2, [('attn_w_q', 0), ('kv_w_k', 0), ('kv_w_v', 0)])
    dz1, dz1b, _, _, db_out = mlp_bwd(
        0, dz2, dz2b, x1b, u0, act0, z1, True, [('mlp_w_down', 0), ('ple_w_gate', 0), ('ple_w_proj', 0)],
        down_keys=[('attn_w_q', 0), ('kv_w_k', 0), ('kv_w_v', 0)])
    grads['conv_w_out'] = mm_tn("d_conv_wout", s0, dz1b, D, BF16, tm=T, tk=tw, tn=tw)

    def ds_epi(acc, ex, out, i):
        n = _ln_fwd(ex[0][...], ex[1][...], ex[2][...])
        sg = _sigmoid(n)
        dn = acc * (sg * (1.0 + n * (1.0 - sg)))
        dc, dg, db = _ln_bwd(dn, ex[0][...], ex[1][...])
        out[0][...] = dc
        _init_or_add(out[1], i, dg)
        _init_or_add(out[2], i, db)
    pair = pair_stage(('mlp_w_up', 0), ('conv_w_out', 0))
    (dc0, d_cln_g, d_cln_b), got = mm_nt("conv_ds", dz1b, W_out, [(c0, 'tile'), (cln_g, 'row'), (cln_b, 'row')],
                                         [(_sds((T, D), F32), 'tile'), (_sds((1, D), F32), 'rowacc'),
                                          (_sds((1, D), F32), 'rowacc')], ds_epi, tm=tm, tko=D, tc=D, comm=pair)
    pair_done(pair, got)
    job = chip_stage(('mlp_w_up', 0), ('conv_w_out', 0))
    dh0, d_wdw, d_bdw, d_bin = hosted(dwconv_bwd(dc0, g0, ha0, hg0, w_dw, tm=tmc, comm=job), job)
    grads['conv_w_in'] = mm_tn("d_conv_win", xb, dh0, 2 * D // N_DEV, BF16, tm=T, tk=tw, tn=tw)

    def dx_epi(acc, ex, out, i):
        out[0][...] = acc + alpha * ex[0][...]
    pair = pair_stage(('conv_w_in', 0))
    (grad_x,), got = mm_nt("conv_dx", dh0, W_in, [(dz1, 'tile')], [(_sds((T, D), F32), 'tile')], dx_epi,
                           tm=tm, tko=D, tc=D, comm=pair)
    pair_done(pair, got)
    last_chip = chip_stage(('conv_w_in', 0))

    def own_rows(vec, rows_used, rows):
        arr = vec.reshape(N_DEV, rows_used, LANES)
        return jnp.pad(arr, ((0, 0), (0, rows - rows_used), (0, 0)))
    dwdw_dev = jnp.transpose(d_wdw.reshape(CONV_HALO, N_DEV, D // N_DEV), (1, 0, 2))
    lane_rows = D // N_DEV // LANES
    small_grad = jnp.concatenate([
        own_rows(d_bin, 2 * lane_rows, 8), dwdw_dev if lane_rows == 1 else dwdw_dev.reshape(N_DEV, -1, LANES),
        own_rows(d_bdw, lane_rows, 8), own_rows(d_cln_g, lane_rows, 8), own_rows(d_cln_b, lane_rows, 8),
        own_rows(db_out, lane_rows, 8)], axis=1)
    n_small = small_grad.shape[1]

    rep_local = {'mix_ln_g': jnp.concatenate([grads[('mix_ln_g', li)] for li in range(DEPTH)], axis=0),
                 'mix_ln_b': jnp.concatenate([grads[('mix_ln_b', li)] for li in range(DEPTH)], axis=0),
                 'mlp_ln_g': jnp.concatenate([grads[('mlp_ln_g', li)] for li in range(DEPTH)], axis=0),
                 'mlp_ln_b': jnp.concatenate([grads[('mlp_ln_b', li)] for li in range(DEPTH)], axis=0),
                 'attn_sinks': d_sinks}
    rep_grad = _rep_rows(rep_local, '')
    n_rep = rep_grad.shape[0]
    got, (recv_small, recv_rep) = comm_only("exchange_last", [last_chip, _DirectJob(
        [small_grad, jnp.broadcast_to(rep_grad[None], (N_DEV, n_rep, LANES))])])
    hosted((None, got), last_chip)

    result = {}
    kinds = ('grad', 'delta', 'new_m', 'new_v')
    w, m, v = (_small_rows(a, pre)[None] for pre in ('', 'm_', 'v_'))
    for kind, arr in zip(kinds, adamw("adamw_small", [recv_small], w, m, v, ta=n_small)):
        for pname, val in _unpack_small(arr[0], a).items():
            result[(kind, pname)] = val
    w, m, v = (_rep_rows(a, pre)[None] for pre in ('', 'm_', 'v_'))
    for kind, arr in zip(kinds, adamw("adamw_rep", [recv_rep], w, m, v, ta=n_rep)):
        for pname, val in _unpack_rep(arr[0], a).items():
            result[(kind, pname)] = val
    for name in BIG_WEIGHTS:
        w, m, v = (shard3(a[pre + name]) for pre in ('', 'm_', 'v_'))
        recvs = [join_halves(recv[(name, li, 0)], recv[(name, li, 1)]) if name in halved else recv[(name, li)]
                 for li in range(w.shape[0])]
        for kind, arr in zip(kinds, adamw("adamw_" + name, recvs, w, m, v, ta=min(256, w.shape[1]))):
            result[(kind, name)] = arr.reshape(a[name].shape)

    loss = lax.psum(loss_local, ("x", "y", "c"))
    out = [loss, grad_x[None]]
    for kind in ('grad', 'delta', 'new_m', 'new_v'):
        out += [result[(kind, name)] for name in WEIGHT_NAMES]
    return tuple(out)


def _glu(x, W_in, b_in, T, D, tm, comm=None):
    n = W_in.shape[2]
    q = 2 if D // n % 2 == 0 else 1
    nt = D // (q * n)
    tn = q * n

    def body(x_ref, wa_ref, wg_ref, ba_ref, bg_ref, g_ref, ha_ref, hg_ref):
        xb = x_ref[...]
        ha = jnp.concatenate([_dot(xb, wa_ref[s], ((1,), (0,))) for s in range(q)], axis=1) + ba_ref[...]
        hg = jnp.concatenate([_dot(xb, wg_ref[s], ((1,), (0,))) for s in range(q)], axis=1) + bg_ref[...]
        g_ref[...] = ha * _sigmoid(hg)
        ha_ref[...] = ha.astype(ha_ref.dtype)
        hg_ref[...] = hg.astype(hg_ref.dtype)

    return _call(
        body, comm, name="conv_in_glu", grid=(T // tm, nt),
        in_specs=[pl.BlockSpec((tm, D), lambda i, j: (i, 0)),
                  pl.BlockSpec((q, D, n), lambda i, j: (j, 0, 0)),
                  pl.BlockSpec((q, D, n), lambda i, j: (j + nt, 0, 0)),
                  pl.BlockSpec((1, tn), lambda i, j: (0, j)), pl.BlockSpec((1, tn), lambda i, j: (0, j + nt))],
        out_specs=[pl.BlockSpec((tm, tn), lambda i, j: (i, j))] * 3,
        out_shape=[_sds((T, D), F32), _sds((T, D), BF16), _sds((T, D), BF16)],
        sem=("parallel", "parallel"), operands=[x, W_in, W_in, b_in, b_in])


def kernel(x, p, conv_w_in, conv_b_in, conv_w_dw, conv_b_dw, conv_ln_g, conv_ln_b, conv_w_out, conv_b_out, kv_w_k, kv_w_v, attn_w_q, attn_sinks, attn_w_o, mix_ln_g, mix_ln_b, mlp_w_up, mlp_w_down, mlp_ln_g, mlp_ln_b, ple_w_proj, ple_w_gate, loss_target, m_conv_w_in, m_conv_b_in, m_conv_w_dw, m_conv_b_dw, m_conv_ln_g, m_conv_ln_b, m_conv_w_out, m_conv_b_out, m_kv_w_k, m_kv_w_v, m_attn_w_q, m_attn_sinks, m_attn_w_o, m_mix_ln_g, m_mix_ln_b, m_mlp_w_up, m_mlp_w_down, m_mlp_ln_g, m_mlp_ln_b, m_ple_w_proj, m_ple_w_gate, v_conv_w_in, v_conv_b_in, v_conv_w_dw, v_conv_b_dw, v_conv_ln_g, v_conv_ln_b, v_conv_w_out, v_conv_b_out, v_kv_w_k, v_kv_w_v, v_attn_w_q, v_attn_sinks, v_attn_w_o, v_mix_ln_g, v_mix_ln_b, v_mlp_w_up, v_mlp_w_down, v_mlp_ln_g, v_mlp_ln_b, v_ple_w_proj, v_ple_w_gate):
    return _step(dict(locals()))
```

```python
import functools
import math

import jax
import jax.numpy as jnp
from jax import lax
from jax.experimental import pallas as pl
from jax.experimental.pallas import tpu as pltpu

F32 = jnp.float32
BF16 = jnp.bfloat16

N_DEV = 8
HEAD_DIM = 64
ROPE_DIM = HEAD_DIM // 4
ROPE_HALF = ROPE_DIM // 2
ROPE_THETA = 500000.0
ATT_BLOCK = 128
CONV_WIDTH = 31
CONV_HALO = 32
CONV_ROWS = 128
LN_EPS = 1e-5
DEPTH = 2
DEEPNORM_ALPHA = (2 * DEPTH) ** 0.25
MASK_VALUE = -1e30

ADAM_LR = 0.001
ADAM_B1 = 0.9
ADAM_B2 = 0.999
ADAM_EPS = 1e-08
ADAM_WD = 0.01
ADAM_STEP = 10

LANES = 128
SUBLANES = 8
VMEM_LIMIT_BYTES = 52 * 1024 * 1024
TOKEN_TILE = 512
MESH_ID = pl.DeviceIdType.MESH
RELAY_AT_TENTHS = 5
FORWARD_AT_TENTHS = 8

WEIGHT_NAMES = ['conv_w_in', 'conv_b_in', 'conv_w_dw', 'conv_b_dw', 'conv_ln_g', 'conv_ln_b', 'conv_w_out',
                'conv_b_out', 'kv_w_k', 'kv_w_v', 'attn_w_q', 'attn_sinks', 'attn_w_o', 'mix_ln_g', 'mix_ln_b',
                'mlp_w_up', 'mlp_w_down', 'mlp_ln_g', 'mlp_ln_b', 'ple_w_proj', 'ple_w_gate']
BIG_WEIGHTS = ['conv_w_in', 'conv_w_out', 'kv_w_k', 'kv_w_v', 'attn_w_q', 'attn_w_o', 'mlp_w_up', 'mlp_w_down',
               'ple_w_proj', 'ple_w_gate']
SMALL_SHARDED = [('conv_b_in', 8), ('conv_w_dw', 32), ('conv_b_dw', 8), ('conv_ln_g', 8), ('conv_ln_b', 8),
                 ('conv_b_out', 8)]
REPLICATED = ['mix_ln_g', 'mix_ln_b', 'mlp_ln_g', 'mlp_ln_b', 'attn_sinks']


def _params(sem):
    return pltpu.CompilerParams(dimension_semantics=sem, vmem_limit_bytes=VMEM_LIMIT_BYTES)


def _sds(shape, dtype):
    return jax.ShapeDtypeStruct(shape, dtype)


def _my_place():
    x, y, c = lax.axis_index("x"), lax.axis_index("y"), lax.axis_index("c")
    return x, y, c, 4 * x + 2 * y + c


def _peers(x, y, c):
    out = []
    for dx in (0, 1):
        for dy in (0, 1):
            for dc in (0, 1):
                if dx or dy or dc:
                    px, py, pc = x ^ dx, y ^ dy, c ^ dc
                    out.append(((px, py, pc), 4 * px + 2 * py + pc))
    return out


N_CHIPS = 4


def _other_chips(x, y):
    return [((x ^ dx, y ^ dy), 2 * (x ^ dx) + (y ^ dy)) for dx, dy in ((1, 0), (0, 1), (1, 1))]


def _remote(src, dst, send, recv, to):
    return pltpu.make_async_remote_copy(src_ref=src, dst_ref=dst, send_sem=send, recv_sem=recv, device_id=to,
                                        device_id_type=MESH_ID)


def _wait_slabs(buf, count, send, recv, me, sent=True, received=True):
    part = buf.at[pl.ds(0, count)]
    cp = _remote(part, part, send, recv, me)
    if sent:
        cp.wait_send()
    if received:
        cp.wait_recv()


class _DirectJob:
    n_sems = 3

    def __init__(self, items):
        self.sources = list(items)
        self.dests = [_sds(it.shape, it.dtype) for it in items]
        self.n = len(items)

    def start(self, src, dst, send, recv, loc):
        x, y, c, me = _my_place()
        for k in range(self.n):
            here = dst[k].at[pl.ds(me, 1)]
            pltpu.make_async_copy(src[k].at[pl.ds(me, 1)], here, loc.at[k]).start()
            for peer, idx in _peers(x, y, c):
                _remote(src[k].at[pl.ds(idx, 1)], here, send.at[k], recv.at[k], peer).start()

    def relay(self, *refs):
        pass

    def forward(self, *refs):
        pass

    def finish(self, src, dst, send, recv, loc):
        x, y, c, me = _my_place()
        for k in range(self.n):
            pltpu.make_async_copy(src[k].at[pl.ds(me, 1)], dst[k].at[pl.ds(me, 1)], loc.at[k]).wait()
            _wait_slabs(dst[k], N_DEV - 1, send.at[k], recv.at[k], (x, y, c))


class _GatherJob:
    n_sems = 7

    def __init__(self, items):
        self.sources = [it.reshape((1,) + it.shape) for it in items]
        self.dests = [_sds((N_DEV,) + it.shape, it.dtype) for it in items]
        self.n = len(items)

    @staticmethod
    def _neighbours(x, y, c):
        out = []
        for dx, dy in ((c, 1 - c), (1 - c, c), (1, 1)):
            px, py = x ^ dx, y ^ dy
            out.append(((px, py, c), 4 * px + 2 * py + c))
        return out

    def start(self, src, dst, send_a, recv_a, send_b, recv_b, send_d, recv_d, loc):
        x, y, c, me = _my_place()
        (first, _), (second, _), _ = self._neighbours(x, y, c)
        for k in range(self.n):
            here = dst[k].at[pl.ds(me, 1)]
            pltpu.make_async_copy(src[k], here, loc.at[k]).start()
            _remote(src[k], here, send_d.at[k], recv_d.at[k], (x, y, 1 - c)).start()
            _remote(src[k], here, send_a.at[k], recv_a.at[k], first).start()
            _remote(src[k], here, send_b.at[k], recv_b.at[k], second).start()

    def relay(self, src, dst, send_a, recv_a, send_b, recv_b, send_d, recv_d, loc):
        x, y, c, me = _my_place()
        (_, first_idx), (second, _), _ = self._neighbours(x, y, c)
        for k in range(self.n):
            _wait_slabs(dst[k], 1, send_a.at[k], recv_a.at[k], (x, y, c), sent=False)
            rows = dst[k].at[pl.ds(first_idx, 1)]
            _remote(rows, rows, send_b.at[k], recv_b.at[k], second).start()

    def forward(self, src, dst, send_a, recv_a, send_b, recv_b, send_d, recv_d, loc):
        x, y, c, me = _my_place()
        for k in range(self.n):
            _wait_slabs(dst[k], 2, send_b.at[k], recv_b.at[k], (x, y, c), sent=False)
            for _, idx in self._neighbours(x, y, c):
                rows = dst[k].at[pl.ds(idx, 1)]
                _remote(rows, rows, send_d.at[k], recv_d.at[k], (x, y, 1 - c)).start()

    def finish(self, src, dst, send_a, recv_a, send_b, recv_b, send_d, recv_d, loc):
        x, y, c, me = _my_place()
        for k in range(self.n):
            pltpu.make_async_copy(src[k], dst[k].at[pl.ds(me, 1)], loc.at[k]).wait()
            _wait_slabs(dst[k], 1, send_a.at[k], recv_a.at[k], (x, y, c), received=False)
            _wait_slabs(dst[k], 2, send_b.at[k], recv_b.at[k], (x, y, c), received=False)
            _wait_slabs(dst[k], N_CHIPS, send_d.at[k], recv_d.at[k], (x, y, c))


class _PairJob:
    n_sems = 2

    def __init__(self, items):
        self.sources = list(items)
        self.dests = [_sds((N_CHIPS,) + it.shape[1:], it.dtype) for it in items]
        self.n = len(items)

    def start(self, src, dst, send, recv):
        x, y, c, me = _my_place()
        for k in range(self.n):
            for q in range(N_CHIPS):
                _remote(src[k].at[pl.ds(2 * q + 1 - c, 1)], dst[k].at[pl.ds(q, 1)], send.at[k], recv.at[k],
                        (x, y, 1 - c)).start()

    def relay(self, *refs):
        pass

    def forward(self, *refs):
        pass

    def finish(self, src, dst, send, recv):
        x, y, c, me = _my_place()
        for k in range(self.n):
            _wait_slabs(dst[k], N_CHIPS, send.at[k], recv.at[k], (x, y, c))


class _ChipJob:
    n_sems = 3

    def __init__(self, items):
        self.sources = list(items)
        self.dests = [_sds(it.shape, it.dtype) for it in items]
        self.n = len(items)

    def start(self, src, dst, send, recv, loc):
        x, y, c, me = _my_place()
        mine = 2 * x + y
        for k in range(self.n):
            here = dst[k].at[pl.ds(mine, 1)]
            pltpu.make_async_copy(src[k].at[pl.ds(mine, 1)], here, loc.at[k]).start()
            for (px, py), q in _other_chips(x, y):
                _remote(src[k].at[pl.ds(q, 1)], here, send.at[k], recv.at[k], (px, py, c)).start()

    def relay(self, *refs):
        pass

    def forward(self, *refs):
        pass

    def finish(self, src, dst, send, recv, loc):
        x, y, c, me = _my_place()
        mine = 2 * x + y
        for k in range(self.n):
            pltpu.make_async_copy(src[k].at[pl.ds(mine, 1)], dst[k].at[pl.ds(mine, 1)], loc.at[k]).wait()
            _wait_slabs(dst[k], N_CHIPS - 1, send.at[k], recv.at[k], (x, y, c))


class _Jobs:
    def __init__(self, jobs):
        self.jobs = list(jobs)
        self.sources = [s for job in self.jobs for s in job.sources]
        self.dests = [d for job in self.jobs for d in job.dests]
        self.n = len(self.sources)

    def _stage(self, stage, src, dst, sems):
        k0 = s0 = 0
        for job in self.jobs:
            getattr(job, stage)(src[k0:k0 + job.n], dst[k0:k0 + job.n], *sems[s0:s0 + job.n_sems])
            k0 += job.n
            s0 += job.n_sems

    def start(self, src, dst, *sems):
        self._stage('start', src, dst, sems)

    def relay(self, src, dst, *sems):
        self._stage('relay', src, dst, sems)

    def forward(self, src, dst, *sems):
        self._stage('forward', src, dst, sems)

    def finish(self, src, dst, *sems):
        self._stage('finish', src, dst, sems)

    def split(self, got):
        out, k0 = [], 0
        for job in self.jobs:
            out.append(got[k0:k0 + job.n])
            k0 += job.n
        return out


def _job_sems(job):
    if isinstance(job, _Jobs):
        return [s for part in job.jobs for s in _job_sems(part)]
    return [pltpu.SemaphoreType.DMA((job.n,))] * job.n_sems


def _call(body, comm, *, name, grid, in_specs, out_specs, out_shape, operands, sem, scratch_shapes=(), aliases=None):
    single = not isinstance(out_shape, (list, tuple))
    out_shape = [out_shape] if single else list(out_shape)
    out_specs = [out_specs] if single else list(out_specs)
    if comm is None:
        res = pl.pallas_call(body, name=name, grid=grid, in_specs=list(in_specs), out_specs=out_specs,
                             out_shape=out_shape, scratch_shapes=list(scratch_shapes),
                             input_output_aliases=aliases or {}, compiler_params=_params(sem))(*operands)
        return res[0] if single else res
    n_in, n_out, n_scr, n_c = len(in_specs), len(out_shape), len(scratch_shapes), comm.n
    any_spec = pl.BlockSpec(memory_space=pl.ANY)
    steps = math.prod(grid)
    early = min(steps - 1, (steps * RELAY_AT_TENTHS) // 10)
    mid = min(steps - 1, (steps * FORWARD_AT_TENTHS) // 10)

    def hosted(*refs):
        ins, c_src = refs[:n_in], refs[n_in:n_in + n_c]
        outs = refs[n_in + n_c:n_in + n_c + n_out]
        c_dst = refs[n_in + n_c + n_out:n_in + 2 * n_c + n_out]
        scr = refs[n_in + 2 * n_c + n_out:n_in + 2 * n_c + n_out + n_scr]
        sems = refs[n_in + 2 * n_c + n_out + n_scr:]
        step = pl.program_id(0)
        for d in range(1, len(grid)):
            step = step * grid[d] + pl.program_id(d)

        @pl.when(step == 0)
        def _():
            comm.start(c_src, c_dst, *sems)

        @pl.when(step == early)
        def _():
            comm.relay(c_src, c_dst, *sems)

        @pl.when(step == mid)
        def _():
            comm.forward(c_src, c_dst, *sems)

        body(*ins, *outs, *scr)

        @pl.when(step == steps - 1)
        def _():
            comm.finish(c_src, c_dst, *sems)

    res = pl.pallas_call(hosted, name=name, grid=grid, in_specs=list(in_specs) + [any_spec] * n_c,
                         out_specs=out_specs + [any_spec] * n_c, out_shape=out_shape + comm.dests,
                         scratch_shapes=list(scratch_shapes) + _job_sems(comm), input_output_aliases=aliases or {},
                         compiler_params=_params(("arbitrary",) * len(grid)))(*operands, *comm.sources)
    main = res[:n_out]
    return (main[0] if single else main), res[n_out:]


def comm_only(name, jobs):
    any_spec = pl.BlockSpec(memory_space=pl.ANY)
    n_all = sum(job.n for job in jobs)

    def body(*refs):
        srcs, dsts, sems = refs[:n_all], refs[n_all:2 * n_all], refs[2 * n_all:]
        parts, k0, s0 = [], 0, 0
        for job in jobs:
            parts.append((job, srcs[k0:k0 + job.n], dsts[k0:k0 + job.n], sems[s0:s0 + job.n_sems]))
            k0 += job.n
            s0 += job.n_sems
        for stage in ('start', 'relay', 'forward', 'finish'):
            for job, src, dst, sem in parts:
                getattr(job, stage)(src, dst, *sem)

    res = pl.pallas_call(body, name=name, in_specs=[any_spec] * n_all, out_specs=[any_spec] * n_all,
                         out_shape=[d for job in jobs for d in job.dests],
                         scratch_shapes=[s for job in jobs for s in _job_sems(job)],
                         )(*[s for job in jobs for s in job.sources])
    out, k0 = [], 0
    for job in jobs:
        out.append(res[k0:k0 + job.n])
        k0 += job.n
    return out


def _dot(a, b, dims):
    if a.dtype != BF16:
        a = a.astype(BF16)
    if b.dtype != BF16:
        b = b.astype(BF16)
    return lax.dot_general(a, b, (dims, ((), ())), preferred_element_type=F32)


def _sigmoid(v):
    return 1.0 / (1.0 + jnp.exp(-v))


def _ln_stats(z):
    mu = jnp.mean(z, axis=-1, keepdims=True)
    zc = z - mu
    var = jnp.mean(zc * zc, axis=-1, keepdims=True)
    return zc * lax.rsqrt(var + LN_EPS)


def _ln_fwd(z, g, b):
    return _ln_stats(z) * g + b


def _ln_bwd(dy, z, g):
    xhat = _ln_stats(z)
    mu = jnp.mean(z, axis=-1, keepdims=True)
    zc = z - mu
    rstd = lax.rsqrt(jnp.mean(zc * zc, axis=-1, keepdims=True) + LN_EPS)
    dxh = dy * g
    m1 = jnp.mean(dxh, axis=-1, keepdims=True)
    m2 = jnp.mean(dxh * xhat, axis=-1, keepdims=True)
    dz = rstd * (dxh - m1 - xhat * m2)
    return dz, jnp.sum(dy * xhat, axis=0, keepdims=True), jnp.sum(dy, axis=0, keepdims=True)


def _extra_spec(shape, kind, tm, tn, ij):
    if kind == 'tile':
        return pl.BlockSpec((tm, tn), lambda *g: (ij(g)[0], ij(g)[1]))
    if kind in ('row', 'rowacc'):
        return pl.BlockSpec((1, tn), lambda *g: (0, ij(g)[1]))
    if kind == 'tab':
        return pl.BlockSpec((tm, LANES), lambda *g: (ij(g)[0], 0))
    if kind == 'rows':
        return pl.BlockSpec((tm, shape[1]), lambda *g: (ij(g)[0], 0))
    raise ValueError(kind)


def mm_nn(name, pairs, extras, outs, epi, *, tm, tn, tk=None, comm=None):
    M = pairs[0][0].shape[0]
    N = pairs[0][1].shape[0] * pairs[0][1].shape[2]
    n_pairs = len(pairs)
    K0 = pairs[0][0].shape[1]
    tk = K0 if tk is None else tk
    nk = K0 // tk
    assert nk == 1 or n_pairs == 1
    assert M % tm == 0 and N % tn == 0 and K0 % tk == 0
    has_rowacc = any(kind == 'rowacc' for _, kind in outs)
    assert not has_rowacc or (N == tn and nk == 1)
    in_specs, operands, slabs = [], [], []
    for a, b, off in pairs:
        K = a.shape[1]
        ktile = K if n_pairs > 1 else tk
        n = b.shape[2]
        assert b.shape[1] == K
        in_specs.append(pl.BlockSpec((tm, ktile), lambda i, j, k: (i, k)))
        if tn <= n:
            assert n % tn == 0
            r = n // tn
            in_specs.append(pl.BlockSpec((None, ktile, tn),
                                         lambda i, j, k, r=r, off=off: ((j + off) // r, k, (j + off) % r)))
            slabs.append(0)
        else:
            assert tn % n == 0
            in_specs.append(pl.BlockSpec((tn // n, ktile, n), lambda i, j, k, off=off: (j + off, k, 0)))
            slabs.append(tn // n)
        operands += [a, b]
    ij = lambda g: (g[0], g[1])
    for arr, kind in extras:
        in_specs.append(_extra_spec(arr.shape, kind, tm, tn, ij))
        operands.append(arr)
    out_specs = [_extra_spec(o.shape, kind, tm, tn, ij) for o, kind in outs]
    n_ex, n_out = len(extras), len(outs)

    def pair_dot(ab, q):
        a = ab[2 * q][...]
        if not slabs[q]:
            return _dot(a, ab[2 * q + 1][...], ((1,), (0,)))
        return jnp.concatenate([_dot(a, ab[2 * q + 1][s], ((1,), (0,))) for s in range(slabs[q])], axis=1)

    def body(*refs):
        ab = refs[:2 * n_pairs]
        ex = refs[2 * n_pairs:2 * n_pairs + n_ex]
        out = refs[2 * n_pairs + n_ex:2 * n_pairs + n_ex + n_out]
        i = pl.program_id(0)
        if nk == 1:
            accs = [pair_dot(ab, q) for q in range(n_pairs)]
            epi(accs, ex, out, i)
        else:
            acc_ref = refs[-1]
            k = pl.program_id(2)

            @pl.when(k == 0)
            def _():
                acc_ref[...] = jnp.zeros_like(acc_ref)

            acc_ref[...] += pair_dot(ab, 0)

            @pl.when(k == nk - 1)
            def _():
                epi([acc_ref[...]], ex, out, i)

    scratch = [pltpu.VMEM((tm, tn), F32)] if nk > 1 else []
    sem = ("arbitrary",) * 3 if has_rowacc else ("parallel", "parallel", "arbitrary")
    return _call(body, comm, name=name, grid=(M // tm, N // tn, nk), in_specs=in_specs, out_specs=out_specs,
                 out_shape=[o for o, _ in outs], scratch_shapes=scratch, sem=sem, operands=operands)


def mm_nt(name, a, b, extras, outs, epi, *, tm, tko, tc, comm=None, a_pro=None):
    M, N = a.shape
    J, K, n = b.shape
    assert J * n == N and M % tm == 0 and K % tko == 0 and N % tc == 0
    nc = N // tc
    assert a_pro is None or (nc == 1 and tc <= n and K == tko == N)
    has_rowacc = any(kind == 'rowacc' for _, kind in outs)
    assert not has_rowacc or K == tko
    if tc <= n:
        assert n % tc == 0
        r = n // tc
        slabs = 0
        b_spec = pl.BlockSpec((None, tko, tc), lambda i, j, c: (c // r, j, c % r))
    else:
        assert tc % n == 0
        slabs = tc // n
        b_spec = pl.BlockSpec((slabs, tko, n), lambda i, j, c: (c, j, 0))
    n_ab = 2 if a_pro is None else 1
    in_specs = ([pl.BlockSpec((tm, tc), lambda i, j, c: (i, c))] if a_pro is None else []) + [b_spec]

    def nt_dot(a_ref, b_ref):
        if not slabs:
            return _dot(a_ref[...], b_ref[...], ((1,), (1,)))
        acc = _dot(a_ref[:, 0:n], b_ref[0], ((1,), (1,)))
        for s in range(1, slabs):
            acc = acc + _dot(a_ref[:, s * n:(s + 1) * n], b_ref[s], ((1,), (1,)))
        return acc

    ij = lambda g: (g[0], g[1])
    operands = [a, b] if a_pro is None else [b]
    for arr, kind in extras:
        in_specs.append(_extra_spec(arr.shape, kind, tm, tko, ij))
        operands.append(arr)
    out_specs = [_extra_spec(o.shape, kind, tm, tko, ij) for o, kind in outs]
    n_ex, n_out = len(extras), len(outs)

    def body(*refs):
        ex = refs[n_ab:n_ab + n_ex]
        out = refs[n_ab + n_ex:n_ab + n_ex + n_out]
        i = pl.program_id(0)
        if a_pro is not None:
            epi(_dot(a_pro(ex), refs[0][...], ((1,), (1,))), ex, out, i)
            return
        a_ref, b_ref = refs[:2]
        if nc == 1:
            epi(nt_dot(a_ref, b_ref), ex, out, i)
        else:
            acc_ref = refs[-1]
            c = pl.program_id(2)

            @pl.when(c == 0)
            def _():
                acc_ref[...] = jnp.zeros_like(acc_ref)

            acc_ref[...] += nt_dot(a_ref, b_ref)

            @pl.when(c == nc - 1)
            def _():
                epi(acc_ref[...], ex, out, i)

    scratch = [pltpu.VMEM((tm, tko), F32)] if nc > 1 else []
    sem = ("arbitrary",) * 3 if has_rowacc else ("parallel", "parallel", "arbitrary")
    return _call(body, comm, name=name, grid=(M // tm, K // tko, nc), in_specs=in_specs, out_specs=out_specs,
                 out_shape=[o for o, _ in outs], scratch_shapes=scratch, sem=sem, operands=operands)


def mm_tn(name, a, d, n, out_dtype, *, tm, tk, tn, comm=None):
    M, K = a.shape
    N = d.shape[1]
    assert d.shape[0] == M and N % n == 0 and N % tn == 0 and K % tk == 0 and M % tm == 0
    nm = M // tm
    if tn <= n:
        assert n % tn == 0
        r = n // tn
        slabs = 0
        o_spec = pl.BlockSpec((None, tk, tn), lambda kk, j, m: (j // r, kk, j % r))
    else:
        assert tn % n == 0
        slabs = tn // n
        o_spec = pl.BlockSpec((slabs, tk, n), lambda kk, j, m: (j, kk, 0))

    def write(o_ref, acc):
        if not slabs:
            o_ref[...] = acc.astype(o_ref.dtype)
        else:
            for s in range(slabs):
                o_ref[s] = acc[:, s * n:(s + 1) * n].astype(o_ref.dtype)

    def body(a_ref, d_ref, o_ref, *scratch):
        if nm == 1:
            write(o_ref, _dot(a_ref[...], d_ref[...], ((0,), (0,))))
            return
        acc_ref, = scratch
        m = pl.program_id(2)

        @pl.when(m == 0)
        def _():
            acc_ref[...] = jnp.zeros_like(acc_ref)

        acc_ref[...] += _dot(a_ref[...], d_ref[...], ((0,), (0,)))

        @pl.when(m == nm - 1)
        def _():
            write(o_ref, acc_ref[...])

    return _call(
        body, comm, name=name, grid=(K // tk, N // tn, nm),
        in_specs=[pl.BlockSpec((tm, tk), lambda kk, j, m: (m, kk)), pl.BlockSpec((tm, tn), lambda kk, j, m: (m, j))],
        out_specs=o_spec, out_shape=_sds((N // n, K, n), out_dtype),
        scratch_shapes=[pltpu.VMEM((tk, tn), F32)] if nm > 1 else [],
        sem=("parallel", "parallel", "arbitrary"), operands=[a, d])


def _init_or_add(ref, i, value):
    @pl.when(i == 0)
    def _():
        ref[...] = value

    @pl.when(i > 0)
    def _():
        ref[...] += value


def _rope_tables(T):
    pos = jnp.arange(T, dtype=F32)
    inv_freq = ROPE_THETA ** (-jnp.arange(0, ROPE_DIM, 2, dtype=F32) / ROPE_DIM)
    ang = pos[:, None] * inv_freq[None, :]
    cos, sin = jnp.cos(ang), jnp.sin(ang)
    ones = jnp.ones((T, HEAD_DIM - ROPE_DIM), F32)
    zeros = jnp.zeros((T, HEAD_DIM - ROPE_DIM), F32)
    zh = jnp.zeros((T, ROPE_HALF), F32)
    c_head = jnp.concatenate([cos, cos, ones], axis=1)
    s_up = jnp.concatenate([-sin, zh, zeros], axis=1)
    s_dn = jnp.concatenate([zh, sin, zeros], axis=1)
    rep = LANES // HEAD_DIM
    return tuple(jnp.tile(t, (1, rep)) for t in (c_head, s_up, s_dn))


def _rope_chunk(t, c, s_up, s_dn):
    return t * c + pltpu.roll(t, LANES - ROPE_HALF, 1) * s_up + pltpu.roll(t, ROPE_HALF, 1) * s_dn


def _rope_chunk_bwd(d, c, s_up, s_dn):
    return d * c + pltpu.roll(d * s_up, ROPE_HALF, 1) + pltpu.roll(d * s_dn, LANES - ROPE_HALF, 1)


def _rope_wide(t, c, s_up, s_dn, fn):
    chunks = [fn(t[:, q * LANES:(q + 1) * LANES], c, s_up, s_dn) for q in range(t.shape[1] // LANES)]
    return chunks[0] if len(chunks) == 1 else jnp.concatenate(chunks, axis=1)


def _taps_by_residue(first):
    groups = {}
    for o in range(first, first + CONV_WIDTH):
        groups.setdefault(o % SUBLANES, []).append(o)
    return sorted(groups.items())


def _shifted_taps(win_ref, span_ref, r0, res, offs, rb, ls):
    if res == 0:
        return [functools.partial(lambda o: win_ref[pl.ds(r0 + o, rb), ls], o) for o in offs]
    n = rb + offs[-1] - res
    span_ref[0:n, :] = win_ref[pl.ds(r0 + res, n), ls]
    return [functools.partial(lambda o: span_ref[pl.ds(o - res, rb), :], o) for o in offs]


def _halo_before_spec(tm, D):
    return pl.BlockSpec((CONV_HALO, D), lambda i: (jnp.maximum(i * (tm // CONV_HALO) - 1, 0), 0))


def dwconv_fwd(g, w_dw, b_dw, ln_g, ln_b, *, tm, comm=None):
    T, D = g.shape
    nl = D // LANES

    rb = min(CONV_ROWS, tm)

    def body(g_ref, gh_ref, w_ref, b_ref, lg_ref, lb_ref, c_ref, s_ref, win_ref, span_ref):
        i = pl.program_id(0)
        win_ref[0:CONV_HALO, :] = jnp.where(i > 0, gh_ref[...], 0.0)
        win_ref[CONV_HALO:, :] = g_ref[...]

        def lane_chunk(q, carry):
            ls = pl.ds(pl.multiple_of(q * LANES, LANES), LANES)
            for r0 in range(0, tm, rb):
                acc = jnp.broadcast_to(b_ref[:, ls], (rb, LANES))
                for res, offs in _taps_by_residue(CONV_HALO - (CONV_WIDTH - 1)):
                    taps = _shifted_taps(win_ref, span_ref, r0, res, offs, rb, ls)
                    for o, tap in zip(offs, taps):
                        k = o - (CONV_HALO - (CONV_WIDTH - 1))
                        acc = acc + tap() * w_ref[k:k + 1, ls]
                c_ref[pl.ds(r0, rb), ls] = acc
            return carry

        lax.fori_loop(0, nl, lane_chunk, 0)
        n = _ln_fwd(c_ref[...], lg_ref[...], lb_ref[...])
        s_ref[...] = (n * _sigmoid(n)).astype(s_ref.dtype)

    row = pl.BlockSpec((1, D), lambda i: (0, 0))
    return _call(
        body, comm, name="dwconv_fwd", grid=(T // tm,),
        in_specs=[pl.BlockSpec((tm, D), lambda i: (i, 0)), _halo_before_spec(tm, D),
                  pl.BlockSpec((CONV_HALO, D), lambda i: (0, 0)), row, row, row],
        out_specs=[pl.BlockSpec((tm, D), lambda i: (i, 0)), pl.BlockSpec((tm, D), lambda i: (i, 0))],
        out_shape=[_sds((T, D), F32), _sds((T, D), BF16)],
        scratch_shapes=[pltpu.VMEM((tm + CONV_HALO, D), F32), pltpu.VMEM((rb + CONV_HALO, LANES), F32)],
        sem=("parallel",), operands=[g, g, w_dw, b_dw, ln_g, ln_b])


def dwconv_bwd(dc, g, ha, hg, w_dw, *, tm, comm=None):
    T, D = g.shape
    nl = D // LANES
    last = T // CONV_HALO - 1
    nt = T // tm

    rb = min(CONV_ROWS, tm)

    def body(dc_ref, dcn_ref, g_ref, gh_ref, ha_ref, hg_ref, w_ref, dh_ref, dw_ref, dbdw_ref, dbin_ref,
             win_ref, dwin_ref, dg_ref, dwp_ref, span_ref):
        i = pl.program_id(0)
        win_ref[0:CONV_HALO, :] = jnp.where(i > 0, gh_ref[...], 0.0)
        win_ref[CONV_HALO:, :] = g_ref[...]
        dwin_ref[0:tm, :] = dc_ref[...]
        dwin_ref[tm:, :] = jnp.where(i < nt - 1, dcn_ref[...], 0.0)

        @pl.when(i == 0)
        def _():
            dwp_ref[...] = jnp.zeros_like(dwp_ref)

        first = CONV_HALO - (CONV_WIDTH - 1)

        def lane_chunk(q, carry):
            ls = pl.ds(pl.multiple_of(q * LANES, LANES), LANES)
            for r0 in range(0, tm, rb):
                acc = jnp.zeros((rb, LANES), F32)
                for res, offs in _taps_by_residue(0):
                    taps = _shifted_taps(dwin_ref, span_ref, r0, res, offs, rb, ls)
                    for o, tap in zip(offs, taps):
                        k = CONV_WIDTH - 1 - o
                        acc = acc + tap() * w_ref[k:k + 1, ls]
                dg_ref[pl.ds(r0, rb), ls] = acc
                dcv = dwin_ref[pl.ds(r0, rb), ls]
                for res, offs in _taps_by_residue(first):
                    taps = _shifted_taps(win_ref, span_ref, r0, res, offs, rb, ls)
                    for o, tap in zip(offs, taps):
                        k = o - first
                        prod = dcv * tap()
                        dwp_ref[k, :, ls] += jnp.sum(prod.reshape(rb // SUBLANES, SUBLANES, LANES), axis=0)
            return carry

        lax.fori_loop(0, nl, lane_chunk, 0)

        @pl.when(i == nt - 1)
        def _():
            for k in range(CONV_WIDTH):
                dw_ref[k:k + 1, :] = jnp.sum(dwp_ref[k], axis=0, keepdims=True)
            dw_ref[CONV_WIDTH:, :] = jnp.zeros((CONV_HALO - CONV_WIDTH, D), F32)
        dg = dg_ref[...]
        ha = ha_ref[...].astype(F32)
        sg = _sigmoid(hg_ref[...].astype(F32))
        d_ha = dg * sg
        d_hg = dg * ha * sg * (1.0 - sg)
        dh_ref[:, 0:D] = d_ha.astype(dh_ref.dtype)
        dh_ref[:, D:] = d_hg.astype(dh_ref.dtype)
        _init_or_add(dbdw_ref, i, jnp.sum(dc_ref[...], axis=0, keepdims=True))
        _init_or_add(dbin_ref, i, jnp.concatenate([jnp.sum(d_ha, axis=0, keepdims=True),
                                                   jnp.sum(d_hg, axis=0, keepdims=True)], axis=1))

    tile = pl.BlockSpec((tm, D), lambda i: (i, 0))
    return _call(
        body, comm, name="dwconv_bwd", grid=(nt,),
        in_specs=[tile,
                  pl.BlockSpec((CONV_HALO, D), lambda i: (jnp.minimum((i + 1) * (tm // CONV_HALO), last), 0)),
                  tile, _halo_before_spec(tm, D),
                  tile, tile, pl.BlockSpec((CONV_HALO, D), lambda i: (0, 0))],
        out_specs=[pl.BlockSpec((tm, 2 * D), lambda i: (i, 0)), pl.BlockSpec((CONV_HALO, D), lambda i: (0, 0)),
                   pl.BlockSpec((1, D), lambda i: (0, 0)), pl.BlockSpec((1, 2 * D), lambda i: (0, 0))],
        out_shape=[_sds((T, 2 * D), BF16), _sds((CONV_HALO, D), F32), _sds((1, D), F32), _sds((1, 2 * D), F32)],
        scratch_shapes=[pltpu.VMEM((tm + CONV_HALO, D), F32), pltpu.VMEM((tm + CONV_HALO, D), F32),
                        pltpu.VMEM((tm, D), F32), pltpu.VMEM((CONV_WIDTH, SUBLANES, D), F32),
                        pltpu.VMEM((rb + CONV_HALO, LANES), F32)],
        sem=("arbitrary",), operands=[dc, dc, g, g, ha, hg, w_dw])


def _attn_specs(HD, W):
    B = ATT_BLOCK
    return [pl.BlockSpec((B, HD), lambda n: (n, 0)),
            pl.BlockSpec((B, 4 * W), lambda n: (n, 0)),
            pl.BlockSpec((B, 4 * W), lambda n: (jnp.maximum(n - 1, 0), 0))]


def _band_mask(n):
    r = lax.broadcasted_iota(jnp.int32, (ATT_BLOCK, 2 * ATT_BLOCK), 0)
    j = lax.broadcasted_iota(jnp.int32, (ATT_BLOCK, 2 * ATT_BLOCK), 1)
    return (j > r) & (j <= r + ATT_BLOCK) & ((n > 0) | (j >= ATT_BLOCK))


def _band(kvc_ref, kvp_ref, part, g, parity, W):
    lanes = slice((2 * part + parity) * W + g * LANES, (2 * part + parity) * W + (g + 1) * LANES)
    return jnp.concatenate([kvp_ref[:, lanes], kvc_ref[:, lanes]], axis=0)


def _half_mask(parity):
    lane = lax.broadcasted_iota(jnp.int32, (1, LANES), 1)
    return (lane < HEAD_DIM) if parity == 0 else (lane >= HEAD_DIM)


def widen_kv(k, v, out_ref, n_kv):
    W = n_kv * LANES
    low = _half_mask(0)
    for part, src in enumerate((k, v)):
        for cg in range(n_kv * HEAD_DIM // LANES):
            chunk = src[:, cg * LANES:(cg + 1) * LANES]
            swapped = pltpu.roll(chunk, HEAD_DIM, 1)
            for g, lo, hi in ((2 * cg, chunk, swapped), (2 * cg + 1, swapped, chunk)):
                base = 2 * part * W + g * LANES
                out_ref[:, base:base + LANES] = jnp.where(low, lo, 0.0).astype(out_ref.dtype)
                out_ref[:, base + W:base + W + LANES] = jnp.where(low, 0.0, hi).astype(out_ref.dtype)


def attn_fwd(q, kvx, sinks, n_heads, n_kv, comm=None):
    T, HD = q.shape
    W = n_kv * LANES
    B = ATT_BLOCK
    chunks_per_group = n_heads // n_kv // 2
    scale = 1.0 / math.sqrt(HEAD_DIM)

    def body(q_ref, kvc_ref, kvp_ref, sink_ref, o_ref, lse_ref):
        n = pl.program_id(0)
        mask = jnp.tile(_band_mask(n), (n_heads, 1))
        s = jnp.concatenate(
            [_dot(q_ref[:, (h // 2) * LANES:(h // 2 + 1) * LANES],
                  _band(kvc_ref, kvp_ref, 0, h // 2 // chunks_per_group, h % 2, W), ((1,), (1,)))
             for h in range(n_heads)], axis=0)
        sink = jnp.concatenate([jnp.broadcast_to(sink_ref[:, h:h + 1], (B, 1)) for h in range(n_heads)], axis=0)
        s = jnp.where(mask, s * scale, MASK_VALUE)
        m = jnp.maximum(jnp.max(s, axis=-1, keepdims=True), sink)
        e = jnp.exp(s - m)
        total = _dot(e, jnp.ones((2 * B, LANES), BF16), ((1,), (0,))) + jnp.exp(sink - m)
        lse = m + jnp.log(total[:, 0:1])
        inv = 1.0 / total
        probs = (e * jnp.concatenate([inv, inv], axis=1)).astype(BF16)
        for c in range(n_heads // 2):
            g = c // chunks_per_group
            out = (_dot(probs[2 * c * B:(2 * c + 1) * B], _band(kvc_ref, kvp_ref, 1, g, 0, W), ((1,), (0,)))
                   + _dot(probs[(2 * c + 1) * B:(2 * c + 2) * B], _band(kvc_ref, kvp_ref, 1, g, 1, W), ((1,), (0,))))
            o_ref[:, c * LANES:(c + 1) * LANES] = out.astype(o_ref.dtype)
        lse_ref[...] = jnp.concatenate([lse[h * B:(h + 1) * B] for h in range(n_heads)], axis=1)

    return _call(
        body, comm, name="attn_fwd", grid=(T // B,),
        in_specs=_attn_specs(HD, W) + [pl.BlockSpec((1, n_heads), lambda n: (0, 0))],
        out_specs=[pl.BlockSpec((B, HD), lambda n: (n, 0)), pl.BlockSpec((B, n_heads), lambda n: (n, 0))],
        out_shape=[_sds((T, HD), BF16), _sds((T, n_heads), F32)],
        sem=("parallel",), operands=[q, kvx, kvx, sinks])


def attn_bwd(q, kvx, do, lse, sinks, rope, n_heads, n_kv, comm=None):
    T, HD = q.shape
    KVD = n_kv * HEAD_DIM
    W = n_kv * LANES
    B = ATT_BLOCK
    chunks_per_group = n_heads // n_kv // 2
    scale = 1.0 / math.sqrt(HEAD_DIM)
    nb = T // B

    def body(q_ref, kvc_ref, kvp_ref, do_ref, lse_ref, sink_ref, c_ref, su_ref, sd_ref, dq_ref, dkv_ref, dsink_ref):
        n = pl.program_id(0)

        @pl.when(n == 0)
        def _():
            dkv_ref[...] = jnp.zeros_like(dkv_ref)
            dsink_ref[...] = jnp.zeros_like(dsink_ref)

        def chunk(ref, h):
            return ref[:, (h // 2) * LANES:(h // 2 + 1) * LANES]

        def band(part, h):
            return _band(kvc_ref, kvp_ref, part, h // 2 // chunks_per_group, h % 2, W)

        def stack(per_head):
            return jnp.concatenate([per_head(h) for h in range(n_heads)], axis=0)

        mask = jnp.tile(_band_mask(n), (n_heads, 1))
        s = stack(lambda h: _dot(chunk(q_ref, h), band(0, h), ((1,), (1,))))
        dp = stack(lambda h: _dot(chunk(do_ref, h), band(1, h), ((1,), (1,))))
        lse = stack(lambda h: lse_ref[:, h:h + 1])
        sink = stack(lambda h: jnp.broadcast_to(sink_ref[:, h:h + 1], (B, 1)))
        probs = jnp.exp(jnp.where(mask, s * scale, MASK_VALUE) - lse)
        delta = jnp.sum(probs * dp, axis=-1, keepdims=True)
        ds = (probs * (dp - delta) * scale).astype(BF16)
        probs = probs.astype(BF16)
        sink_term = jnp.exp(sink - lse) * delta
        dsk = [-jnp.sum(sink_term[h * B:(h + 1) * B], axis=0, keepdims=True) for h in range(n_heads)]

        dk_wide, dv_wide = [None] * n_kv, [None] * n_kv
        for c in range(n_heads // 2):
            g = c // chunks_per_group
            dq2 = None
            for h in (2 * c, 2 * c + 1):
                half = _half_mask(h % 2)
                q2, do2 = chunk(q_ref, h), chunk(do_ref, h)
                ds_h, p_h = ds[h * B:(h + 1) * B], probs[h * B:(h + 1) * B]
                part = _dot(ds_h, band(0, h), ((1,), (0,)))
                dq2 = part if dq2 is None else dq2 + part
                dk_h = _dot(ds_h, jnp.where(half, q2, jnp.zeros_like(q2)), ((0,), (0,)))
                dv_h = _dot(p_h, jnp.where(half, do2, jnp.zeros_like(do2)), ((0,), (0,)))
                dk_wide[g] = dk_h if dk_wide[g] is None else dk_wide[g] + dk_h
                dv_wide[g] = dv_h if dv_wide[g] is None else dv_wide[g] + dv_h
            dq_ref[:, c * LANES:(c + 1) * LANES] = _rope_chunk_bwd(
                dq2, c_ref[...], su_ref[...], sd_ref[...]).astype(dq_ref.dtype)

        def fold(wide):
            low = _half_mask(0)
            both = [w + pltpu.roll(w, HEAD_DIM, 1) for w in wide]
            return jnp.concatenate([jnp.where(low, both[2 * cg], both[2 * cg + 1]) for cg in range(n_kv // 2)], axis=1)

        dkv = jnp.concatenate([fold(dk_wide), fold(dv_wide)], axis=1)
        prev = pl.ds(pl.multiple_of(jnp.maximum(n - 1, 0) * B, B), B)
        cur = pl.ds(pl.multiple_of(n * B, B), B)
        dkv_ref[prev, :] += dkv[0:B, :]
        dkv_ref[cur, :] += dkv[B:, :]
        dsink_ref[...] += jnp.concatenate(dsk, axis=1)

    tab = pl.BlockSpec((B, LANES), lambda n: (n, 0))
    return _call(
        body, comm, name="attn_bwd", grid=(nb,),
        in_specs=_attn_specs(HD, W) + [pl.BlockSpec((B, HD), lambda n: (n, 0)),
                                       pl.BlockSpec((B, n_heads), lambda n: (n, 0)),
                                       pl.BlockSpec((1, n_heads), lambda n: (0, 0)), tab, tab, tab],
        out_specs=[pl.BlockSpec((B, HD), lambda n: (n, 0)), pl.BlockSpec((T, 2 * KVD), lambda n: (0, 0)),
                   pl.BlockSpec((1, n_heads), lambda n: (0, 0))],
        out_shape=[_sds((T, HD + 2 * KVD), BF16), _sds((T, 2 * KVD), F32), _sds((1, n_heads), F32)],
        sem=("arbitrary",), operands=[q, kvx, kvx, do, lse, sinks, *rope])


def dkv_finish(d_qkv, dkv, rope, HD, KVD, *, tm):
    T = dkv.shape[0]
    kv_col = HD // (2 * KVD)

    def body(alias_ref, dkv_ref, c_ref, su_ref, sd_ref, o_ref):
        del alias_ref
        dk = _rope_wide(dkv_ref[:, 0:KVD], c_ref[...], su_ref[...], sd_ref[...], _rope_chunk_bwd)
        o_ref[:, 0:KVD] = dk.astype(o_ref.dtype)
        o_ref[:, KVD:] = dkv_ref[:, KVD:].astype(o_ref.dtype)

    tab = pl.BlockSpec((tm, LANES), lambda i: (i, 0))
    return pl.pallas_call(
        body, name="dkv_finish", grid=(T // tm,),
        in_specs=[pl.BlockSpec(memory_space=pl.ANY), pl.BlockSpec((tm, 2 * KVD), lambda i: (i, 0)), tab, tab, tab],
        out_specs=pl.BlockSpec((tm, 2 * KVD), lambda i: (i, kv_col)),
        out_shape=_sds(d_qkv.shape, d_qkv.dtype), input_output_aliases={0: 0},
        compiler_params=_params(("parallel",)))(d_qkv, dkv, *rope)


def ple_bwd_elem(d_out, pp, gg, *, tm):
    T, D = d_out.shape

    def body(d_ref, pp_ref, gg_ref, dpp_ref, dgg_ref):
        d = d_ref[...]
        sg = _sigmoid(gg_ref[...].astype(F32))
        dpp_ref[...] = (d * sg).astype(dpp_ref.dtype)
        dgg_ref[...] = (d * pp_ref[...].astype(F32) * sg * (1.0 - sg)).astype(dgg_ref.dtype)

    tile = pl.BlockSpec((tm, D), lambda i: (i, 0))
    return pl.pallas_call(
        body, name="ple_bwd_elem", grid=(T // tm,), in_specs=[tile, tile, tile], out_specs=[tile, tile],
        out_shape=[_sds((T, D), BF16), _sds((T, D), BF16)], compiler_params=_params(("parallel",)))(d_out, pp, gg)


def chip_sum(name, g, p_sib, *, ta, part=0, parts=1):
    _, a, b = g.shape
    rows = a // parts
    assert a % parts == 0 and rows % ta == 0
    first = part * (rows // ta)

    def body(core_ref, g_ref, p_ref, o_ref):
        del core_ref
        o_ref[...] = (g_ref[...].astype(F32) + p_ref[...].astype(F32)).astype(o_ref.dtype)

    my_core = lax.axis_index("c").astype(jnp.int32).reshape(1)
    return pl.pallas_call(
        body, name=name, out_shape=_sds((N_CHIPS, rows, b), g.dtype),
        grid_spec=pltpu.PrefetchScalarGridSpec(
            num_scalar_prefetch=1, grid=(N_CHIPS, rows // ta),
            in_specs=[pl.BlockSpec((None, None, ta, b), lambda q, i, core: (q, core[0], first + i, 0)),
                      pl.BlockSpec((None, ta, b), lambda q, i, core: (q, first + i, 0))],
            out_specs=pl.BlockSpec((None, ta, b), lambda q, i, core: (q, i, 0))),
        compiler_params=_params(("arbitrary", "arbitrary")))(my_core, g.reshape(N_CHIPS, 2, a, b), p_sib)


def adamw(name, recvs, w, m, v, *, ta):
    L, a, b = w.shape
    n_terms = recvs[0].shape[0]
    assert a % ta == 0 and len(recvs) == L
    c1 = 1.0 - ADAM_B1 ** ADAM_STEP
    c2 = 1.0 - ADAM_B2 ** ADAM_STEP

    def body(*refs):
        r_refs = refs[:L]
        w_ref, m_ref, v_ref, g_ref, d_ref, nm_ref, nv_ref = refs[L:]
        layer = pl.program_id(0)
        for l in range(L):
            @pl.when(layer == l)
            def _(r_ref=r_refs[l]):
                g = r_ref[0].astype(F32)
                for s in range(1, n_terms):
                    g = g + r_ref[s].astype(F32)
                nm = ADAM_B1 * m_ref[...] + (1.0 - ADAM_B1) * g
                nv = ADAM_B2 * v_ref[...] + (1.0 - ADAM_B2) * jnp.square(g)
                m_hat = nm / c1
                v_hat = nv / c2
                g_ref[...] = g
                d_ref[...] = -ADAM_LR * (m_hat / (jnp.sqrt(v_hat) + ADAM_EPS) + ADAM_WD * w_ref[...])
                nm_ref[...] = nm
                nv_ref[...] = nv

    blk = pl.BlockSpec((None, ta, b), lambda l, i: (l, i, 0))
    out = _sds((L, a, b), F32)
    r_specs = [pl.BlockSpec((n_terms, ta, b), lambda l, i, ll=ll: (0, jnp.where(l == ll, i, 0), 0)) for ll in range(L)]
    return pl.pallas_call(
        body, name=name, grid=(L, a // ta), in_specs=r_specs + [blk, blk, blk],
        out_specs=[blk, blk, blk, blk], out_shape=[out, out, out, out],
        compiler_params=_params(("arbitrary", "arbitrary")))(*recvs, w, m, v)


def _pack_rows(parts):
    out = []
    for arr, rows in parts:
        arr = arr.reshape(-1, LANES).astype(F32)
        out.append(jnp.pad(arr, ((0, rows - arr.shape[0]), (0, 0))))
    return jnp.concatenate(out, axis=0)


def _small_rows(a, prefix):
    return _pack_rows([(a[prefix + name], rows) for name, rows in SMALL_SHARDED])


def _unpack_small(packed, a):
    out, r0 = {}, 0
    for name, rows in SMALL_SHARDED:
        shape = a[name].shape
        used = math.prod(shape) // LANES
        out[name] = packed[r0:r0 + used].reshape(shape)
        r0 += rows
    return out


def _rep_rows(a, prefix):
    parts = []
    for name in REPLICATED:
        arr = a[prefix + name]
        if arr.size % LANES:
            arr = jnp.pad(arr.reshape(1, -1), ((0, 0), (0, LANES - arr.size % LANES)))
        rows = -(-arr.size // LANES)
        parts.append((arr, -(-rows // SUBLANES) * SUBLANES))
    return _pack_rows(parts)


def _unpack_rep(packed, a):
    out, r0 = {}, 0
    for name in REPLICATED:
        shape = a[name].shape
        size = math.prod(shape)
        rows = -(-size // LANES)
        out[name] = packed[r0:r0 + rows].reshape(-1)[:size].reshape(shape)
        r0 += -(-rows // SUBLANES) * SUBLANES
    return out


def _step(a):
    x = a['x'][0]
    T, D = x.shape
    tgt = a['loss_target'][0]
    p_in = [a['p'][i, 0] for i in range(DEPTH)]
    PLE = p_in[0].shape[1]
    n_heads = a['attn_sinks'].shape[1]
    HD = n_heads * HEAD_DIM
    KVD = a['kv_w_k'].shape[1]
    n_kv = KVD // HEAD_DIM
    F = a['mlp_w_down'].shape[1] * N_DEV
    tm = min(TOKEN_TILE, T)
    tm2 = min(2 * TOKEN_TILE, T)
    tmc = min(TOKEN_TILE, T)
    tw = 512
    alpha = DEEPNORM_ALPHA
    xb = x.astype(BF16)
    p_b = [p.astype(BF16) for p in p_in]

    def shard3(w):
        return w.reshape((1,) + w.shape) if w.ndim == 2 else w

    def gather(*specs):
        items = []
        for spec in specs:
            w = shard3(a[spec[0]])[spec[1]]
            if len(spec) == 3:
                rows = w.shape[0] // 2
                w = w[spec[2] * rows:(spec[2] + 1) * rows]
            items.append(w.astype(BF16))
        return _GatherJob(items)

    def join_halves(lo, hi):
        return jnp.concatenate([lo, hi], axis=1)

    (W_in, small_full), = comm_only("gather_first",
                                    [_GatherJob([a['conv_w_in'][0].astype(BF16), _small_rows(a, '')])])
    r0, small = 0, {}
    for name, rows in SMALL_SHARDED:
        small[name] = small_full[:, r0:r0 + rows]
        r0 += rows
    b_in = small['conv_b_in'][:, 0:2 * D // N_DEV // LANES].reshape(1, 2 * D)
    w_dw = jnp.transpose(small['conv_w_dw'], (1, 0, 2)).reshape(CONV_HALO, D)
    b_dw, cln_g, cln_b, b_out = (small[nm][:, 0].reshape(1, D) for nm in
                                 ('conv_b_dw', 'conv_ln_g', 'conv_ln_b', 'conv_b_out'))
    W_up, W_down, W_proj, W_gate = {}, {}, {}, {}
    mix_g, mix_b, mlp_g, mlp_b = a['mix_ln_g'], a['mix_ln_b'], a['mlp_ln_g'], a['mlp_ln_b']
    rope = _rope_tables(T)

    def set_ple_weights(li, g_proj, g_gate):
        W_proj[li] = jnp.transpose(g_proj, (1, 0, 2)).reshape(1, PLE, D)
        W_gate[li] = g_gate.reshape(1, D, D)

    def row(v, i):
        return v[i:i + 1]

    def res_ln_epi(coef):
        def epi(accs, ex, out, i):
            acc = accs[0] if isinstance(accs, list) else accs
            n_ex = len(ex)
            res_ref, g_ref, b_ref = ex[n_ex - 3], ex[n_ex - 2], ex[n_ex - 1]
            z = coef * res_ref[...] + acc
            if n_ex == 4:
                z = z + ex[0][...]
            out[0][...] = z
            xo = _ln_fwd(z, g_ref[...], b_ref[...])
            out[1][...] = xo
            out[2][...] = xo.astype(BF16)
        return epi

    res_ln_outs = [(_sds((T, D), F32), 'tile'), (_sds((T, D), F32), 'tile'), (_sds((T, D), BF16), 'tile')]

    def mlp_fwd(li, xin, xin_b, down_comm=None):
        def up_epi(accs, ex, out, i):
            u = accs[0]
            out[0][...] = u.astype(BF16)
            out[1][...] = jnp.square(jnp.maximum(u, 0.0)).astype(BF16)
        (u, act), got_up = mm_nn(f"mlp_up_{li}", [(xin_b, W_up[li], 0)], [], [(_sds((T, F), BF16), 'tile')] * 2,
                                 up_epi, tm=tm2, tn=min(1024, F), comm=gather(('mlp_w_down', li)))
        W_down[li] = got_up[0].reshape(1, F, D)
        res = mm_nn(f"mlp_down_{li}", [(act, W_down[li], 0)],
                    [(xin, 'tile'), (row(mlp_g, li), 'row'), (row(mlp_b, li), 'row')],
                    res_ln_outs, res_ln_epi(alpha), tm=tm, tn=D, tk=F, comm=down_comm)
        (z, xo, xo_b), got_down = res if down_comm is not None else (res, ())
        return u, act, z, xo, xo_b, got_down

    def ple_fwd(li, xin, xin_b, with_loss, comm=None):
        def epi(accs, ex, out, i):
            pp, gg = accs
            xo = ex[0][...] + pp * _sigmoid(gg)
            out[1][...] = pp.astype(BF16)
            out[2][...] = gg.astype(BF16)
            if with_loss:
                err = xo - ex[1][...]
                out[0][...] = err * (1.0 / D)
                _init_or_add(out[3], i, jnp.sum(err * err, axis=0, keepdims=True) * (0.5 / D))
            else:
                out[0][...] = xo
                out[3][...] = xo.astype(BF16)
        extras = [(xin, 'tile')] + ([(tgt, 'tile')] if with_loss else [])
        outs = [(_sds((T, D), F32), 'tile'), (_sds((T, D), BF16), 'tile'), (_sds((T, D), BF16), 'tile')]
        outs.append((_sds((1, D), F32), 'rowacc') if with_loss else (_sds((T, D), BF16), 'tile'))
        return mm_nn(f"ple_{li}", [(p_b[li], W_proj[li], 0), (xin_b, W_gate[li], 0)], extras, outs, epi, tm=tm, tn=D,
                     comm=comm)

    assert D // N_DEV == LANES
    (g0, ha0, hg0), got = _glu(xb, W_in, b_in, T, D, tm,
                               gather(('conv_w_out', 0), ('ple_w_proj', 0), ('ple_w_gate', 0)))
    W_out = got[0].reshape(1, D, D)
    set_ple_weights(0, got[1], got[2])
    (c0, s0), (W_up[0],) = dwconv_fwd(g0, w_dw, b_dw, cln_g, cln_b, tm=tmc, comm=gather(('mlp_w_up', 0)))
    z1, x1, x1b = mm_nn("conv_out", [(s0, W_out, 0)],
                        [(b_out, 'row'), (x, 'tile'), (row(mix_g, 0), 'row'), (row(mix_b, 0), 'row')],
                        res_ln_outs, res_ln_epi(alpha), tm=tm, tn=D)
    u0, act0, z2, x2, x2b, got = mlp_fwd(
        0, x1, x1b, down_comm=gather(('attn_w_q', 0), ('kv_w_k', 0), ('kv_w_v', 0), ('attn_w_o', 0),
                                     ('ple_w_proj', 1), ('ple_w_gate', 1)))
    W_qkv = jnp.concatenate([got[0].reshape(D, HD), got[1].reshape(D, KVD), got[2].reshape(D, KVD)], axis=1)[None]
    W_o = got[3].reshape(1, HD, D)
    set_ple_weights(1, got[4], got[5])
    x3, pp0, gg0, x3b = ple_fwd(0, x2, x2b, False)

    def qkv_epi(accs, ex, out, i):
        t = accs[0]
        c, su, sd = ex[0][...], ex[1][...], ex[2][...]
        out[0][...] = _rope_wide(t[:, 0:HD], c, su, sd, _rope_chunk).astype(BF16)
        widen_kv(_rope_wide(t[:, HD:HD + KVD], c, su, sd, _rope_chunk), t[:, HD + KVD:], out[1], n_kv)
    NQ = HD + 2 * KVD
    q1, kvx1 = mm_nn("qkv_rope", [(x3b, W_qkv, 0)], [(t, 'tab') for t in rope],
                     [(_sds((T, HD), BF16), 'rows'), (_sds((T, 4 * n_kv * LANES), BF16), 'rows')], qkv_epi,
                     tm=tm, tn=NQ)
    (o1, lse1), (W_up[1],) = attn_fwd(q1, kvx1, a['attn_sinks'], n_heads, n_kv, comm=gather(('mlp_w_up', 1)))
    z3, x4, x4b = mm_nn("attn_out", [(o1, W_o, 0)], [(x3, 'tile'), (row(mix_g, 1), 'row'), (row(mix_b, 1), 'row')],
                        res_ln_outs, res_ln_epi(alpha), tm=tm, tn=D)
    u1, act1, z4, x5, x5b, _ = mlp_fwd(1, x4, x4b)
    dy, pp1, gg1, loss_row = ple_fwd(1, x5, x5b, True)
    loss_local = jnp.sum(loss_row)

    grads = {}

    def ln_bwd_epi(coef, with_colsum):
        def epi(acc, ex, out, i):
            d_x = acc + coef * ex[0][...]
            dz, dg, db = _ln_bwd(d_x, ex[1][...], ex[2][...])
            out[0][...] = dz
            out[1][...] = dz.astype(BF16)
            _init_or_add(out[2], i, dg)
            _init_or_add(out[3], i, db)
            if with_colsum:
                _init_or_add(out[4], i, jnp.sum(dz, axis=0, keepdims=True))
        return epi

    def ln_bwd_outs(with_colsum):
        outs = [(_sds((T, D), F32), 'tile'), (_sds((T, D), BF16), 'tile'), (_sds((1, D), F32), 'rowacc'),
                (_sds((1, D), F32), 'rowacc')]
        return outs + ([(_sds((1, D), F32), 'rowacc')] if with_colsum else [])

    def ple_bwd(li, d_out, xin, pp, gg, z_mlp, pair_specs=None):
        side = {}

        def gate_grads(ex):
            d = ex[0][...]
            sg = _sigmoid(ex[4][...].astype(F32))
            side['d_pp'] = (d * sg).astype(BF16)
            side['d_gg'] = (d * ex[3][...].astype(F32) * sg * (1.0 - sg)).astype(BF16)
            return side['d_gg']

        inner = ln_bwd_epi(1.0, False)

        def epi(acc, ex, out, i):
            inner(acc, ex, out, i)
            out[4][...] = side['d_pp']
            out[5][...] = side['d_gg']

        job = pair_stage(*pair_specs) if pair_specs else None
        res = mm_nt(f"ple_dx_{li}", _sds((T, D), BF16), W_gate[li],
                    [(d_out, 'tile'), (z_mlp, 'tile'), (row(mlp_g, li), 'row'), (pp, 'tile'), (gg, 'tile')],
                    ln_bwd_outs(False) + [(_sds((T, D), BF16), 'tile')] * 2, epi, tm=tm, tko=D, tc=D, comm=job,
                    a_pro=gate_grads)
        (dz, dzb, dg, db, d_pp, d_gg), got = res if job is not None else (res, None)
        grads[('mlp_ln_g', li)], grads[('mlp_ln_b', li)] = dg, db
        if job is not None:
            pair_done(job, got)
        grads[('ple_w_proj', li)] = mm_tn(f"d_ple_proj_{li}", p_b[li], d_pp, D, BF16, tm=T, tk=PLE, tn=tw)
        grads[('ple_w_gate', li)] = mm_tn(f"d_ple_gate_{li}", xin, d_gg, D, BF16, tm=T, tk=tw, tn=tw)
        return dz, dzb

    recv = {}
    wqkv_cols = {'attn_w_q': (0, HD), 'kv_w_k': (HD, HD + KVD), 'kv_w_v': (HD + KVD, NQ)}

    def piece(name, li):
        if name == 'conv_w_in':
            return grads['conv_w_in']
        if name == 'mlp_w_up':
            return grads[('mlp_w_up', li)]
        if name == 'ple_w_proj':
            return jnp.transpose(grads[('ple_w_proj', li)][0].reshape(PLE, N_DEV, D // N_DEV), (1, 0, 2))
        if name in wqkv_cols:
            g = grads['w_qkv'][:, wqkv_cols[name][0]:wqkv_cols[name][1]]
        else:
            g = grads[name] if name in grads else grads[(name, li)]
            g = g[0]
        return g.reshape(N_DEV, g.shape[0] // N_DEV, g.shape[1])

    def pair_stage(*specs):
        job = _PairJob([piece(nm, li) for nm, li in specs])
        job.specs = specs
        return job

    sums = {}
    halved = ()

    def pair_done(pair_job, got):
        for (nm, li), mine, theirs in zip(pair_job.specs, pair_job.sources, got):
            if nm in halved:
                for h in (0, 1):
                    sums[(nm, li, h)] = chip_sum(f"chip_sum_{nm}_{li}_{h}", mine, theirs,
                                                 ta=min(256, mine.shape[1] // 2), part=h, parts=2)
            else:
                sums[(nm, li)] = chip_sum(f"chip_sum_{nm}_{li}", mine, theirs, ta=min(512, mine.shape[1]))

    def chip_stage(*keys):
        job = _ChipJob([sums[k] for k in keys])
        job.specs = keys
        return job

    def hosted(res, job):
        main, got = res
        for spec, r in zip(job.specs, got):
            recv[spec] = r
        return main

    def mlp_bwd(li, dz, dzb, xin, u, act, z_mix, with_colsum, pair_specs, down_keys=None):
        if down_keys:
            job = chip_stage(*down_keys)
            grads[('mlp_w_down', li)] = hosted(
                mm_tn(f"d_mlp_down_{li}", act, dzb, D, BF16, tm=T, tk=tw, tn=tw, comm=job), job)
        else:
            grads[('mlp_w_down', li)] = mm_tn(f"d_mlp_down_{li}", act, dzb, D, BF16, tm=T, tk=tw, tn=tw)

        def du_epi(acc, ex, out, i):
            out[0][...] = (acc * (2.0 * jnp.maximum(ex[0][...].astype(F32), 0.0))).astype(BF16)
        pair = pair_stage(*pair_specs)
        (du,), got = mm_nt(f"mlp_du_{li}", dzb, W_down[li], [(u, 'tile')], [(_sds((T, F), BF16), 'tile')], du_epi,
                           tm=tm2, tko=min(1024, F), tc=D, comm=pair)
        pair_done(pair, got)
        grads[('mlp_w_up', li)] = mm_tn(f"d_mlp_up_{li}", xin, du, F // N_DEV, BF16, tm=T, tk=tw, tn=tw)
        job = chip_stage(('mlp_w_down', li))
        res = hosted(mm_nt(f"mlp_dx_{li}", du, W_up[li], [(dz, 'tile'), (z_mix, 'tile'), (row(mix_g, li), 'row')],
                           ln_bwd_outs(with_colsum), ln_bwd_epi(alpha, with_colsum), tm=tm, tko=D, tc=F, comm=job), job)
        grads[('mix_ln_g', li)], grads[('mix_ln_b', li)] = res[2], res[3]
        return res

    dz4, dz4b = ple_bwd(1, dy, x5b, pp1, gg1, z4)
    dz3, dz3b, _, _ = mlp_bwd(1, dz4, dz4b, x4b, u1, act1, z3, False,
                              [('mlp_w_down', 1), ('ple_w_gate', 1), ('ple_w_proj', 1)])
    job = chip_stage(('ple_w_gate', 1), ('ple_w_proj', 1))
    grads['attn_w_o'] = hosted(mm_tn("d_attn_wo", o1, dz3b, D, BF16, tm=T, tk=tw, tn=tw, comm=job), job)

    def do_epi(acc, ex, out, i):
        out[0][...] = acc.astype(BF16)
    pair = pair_stage(('mlp_w_up', 1), ('attn_w_o', 0))
    (do1,), got = mm_nt("attn_do", dz3b, W_o, [], [(_sds((T, HD), BF16), 'tile')], do_epi, tm=tm, tko=HD, tc=D,
                        comm=pair)
    pair_done(pair, got)
    job = chip_stage(('mlp_w_up', 1), ('attn_w_o', 0))
    d_qkv, dkv, d_sinks = hosted(attn_bwd(q1, kvx1, do1, lse1, a['attn_sinks'], rope, n_heads, n_kv, comm=job), job)
    d_qkv = dkv_finish(d_qkv, dkv, rope, HD, KVD, tm=tm)
    grads['w_qkv'] = mm_tn("d_wqkv", x3b, d_qkv, NQ, BF16, tm=T, tk=tw, tn=2 * KVD)[0]

    def dx3_epi(acc, ex, out, i):
        out[0][...] = acc + alpha * ex[0][...]
    dx3, = mm_nt("attn_dx", d_qkv, W_qkv, [(dz3, 'tile')], [(_sds((T, D), F32), 'tile')], dx3_epi,
                 tm=tm, tko=D, tc=NQ)

    dz2, dz2b = ple_bwd(0, dx3, x2b, pp0, gg0, z2, [('attn_w_q', 0), ('kv_w_k', 0), ('kv_w_v', 0)])
    dz1, dz1b, _, _, db_out = mlp_bwd(
        0, dz2, dz2b, x1b, u0, act0, z1, True, [('mlp_w_down', 0), ('ple_w_gate', 0), ('ple_w_proj', 0)],
        down_keys=[('attn_w_q', 0), ('kv_w_k', 0), ('kv_w_v', 0)])
    job = chip_stage(('ple_w_gate', 0), ('ple_w_proj', 0))
    grads['conv_w_out'] = hosted(mm_tn("d_conv_wout", s0, dz1b, D, BF16, tm=T, tk=tw, tn=tw, comm=job), job)

    def ds_epi(acc, ex, out, i):
        n = _ln_fwd(ex[0][...], ex[1][...], ex[2][...])
        sg = _sigmoid(n)
        dn = acc * (sg * (1.0 + n * (1.0 - sg)))
        dc, dg, db = _ln_bwd(dn, ex[0][...], ex[1][...])
        out[0][...] = dc
        _init_or_add(out[1], i, dg)
        _init_or_add(out[2], i, db)
    pair = pair_stage(('mlp_w_up', 0), ('conv_w_out', 0))
    (dc0, d_cln_g, d_cln_b), got = mm_nt("conv_ds", dz1b, W_out, [(c0, 'tile'), (cln_g, 'row'), (cln_b, 'row')],
                                         [(_sds((T, D), F32), 'tile'), (_sds((1, D), F32), 'rowacc'),
                                          (_sds((1, D), F32), 'rowacc')], ds_epi, tm=tm, tko=D, tc=D, comm=pair)
    pair_done(pair, got)
    job = chip_stage(('mlp_w_up', 0), ('conv_w_out', 0))
    dh0, d_wdw, d_bdw, d_bin = hosted(dwconv_bwd(dc0, g0, ha0, hg0, w_dw, tm=tmc, comm=job), job)
    grads['conv_w_in'] = mm_tn("d_conv_win", xb, dh0, 2 * D // N_DEV, BF16, tm=T, tk=tw, tn=tw)

    pair = pair_stage(('conv_w_in', 0))
    pair_done(pair, comm_only("pair_last", [pair])[0])
    last_chip = chip_stage(('conv_w_in', 0))

    def own_rows(vec, rows_used, rows):
        arr = vec.reshape(N_DEV, rows_used, LANES)
        return jnp.pad(arr, ((0, 0), (0, rows - rows_used), (0, 0)))
    dwdw_dev = jnp.transpose(d_wdw.reshape(CONV_HALO, N_DEV, D // N_DEV), (1, 0, 2))
    lane_rows = D // N_DEV // LANES
    small_grad = jnp.concatenate([
        own_rows(d_bin, 2 * lane_rows, 8), dwdw_dev if lane_rows == 1 else dwdw_dev.reshape(N_DEV, -1, LANES),
        own_rows(d_bdw, lane_rows, 8), own_rows(d_cln_g, lane_rows, 8), own_rows(d_cln_b, lane_rows, 8),
        own_rows(db_out, lane_rows, 8)], axis=1)
    n_small = small_grad.shape[1]

    rep_local = {'mix_ln_g': jnp.concatenate([grads[('mix_ln_g', li)] for li in range(DEPTH)], axis=0),
                 'mix_ln_b': jnp.concatenate([grads[('mix_ln_b', li)] for li in range(DEPTH)], axis=0),
                 'mlp_ln_g': jnp.concatenate([grads[('mlp_ln_g', li)] for li in range(DEPTH)], axis=0),
                 'mlp_ln_b': jnp.concatenate([grads[('mlp_ln_b', li)] for li in range(DEPTH)], axis=0),
                 'attn_sinks': d_sinks}
    rep_grad = _rep_rows(rep_local, '')
    n_rep = rep_grad.shape[0]
    last = _Jobs([last_chip, _DirectJob([small_grad, jnp.broadcast_to(rep_grad[None], (N_DEV, n_rep, LANES))])])

    def dx_epi(acc, ex, out, i):
        out[0][...] = acc + alpha * ex[0][...]
    (grad_x,), got = mm_nt("conv_dx", dh0, W_in, [(dz1, 'tile')], [(_sds((T, D), F32), 'tile')], dx_epi,
                           tm=tm, tko=D, tc=D, comm=last)
    got, (recv_small, recv_rep) = last.split(got)
    hosted((None, got), last_chip)

    result = {}
    kinds = ('grad', 'delta', 'new_m', 'new_v')
    w, m, v = (_small_rows(a, pre)[None] for pre in ('', 'm_', 'v_'))
    for kind, arr in zip(kinds, adamw("adamw_small", [recv_small], w, m, v, ta=n_small)):
        for pname, val in _unpack_small(arr[0], a).items():
            result[(kind, pname)] = val
    w, m, v = (_rep_rows(a, pre)[None] for pre in ('', 'm_', 'v_'))
    for kind, arr in zip(kinds, adamw("adamw_rep", [recv_rep], w, m, v, ta=n_rep)):
        for pname, val in _unpack_rep(arr[0], a).items():
            result[(kind, pname)] = val
    for name in BIG_WEIGHTS:
        w, m, v = (shard3(a[pre + name]) for pre in ('', 'm_', 'v_'))
        recvs = [join_halves(recv[(name, li, 0)], recv[(name, li, 1)]) if name in halved else recv[(name, li)]
                 for li in range(w.shape[0])]
        for kind, arr in zip(kinds, adamw("adamw_" + name, recvs, w, m, v, ta=min(256, w.shape[1]))):
            result[(kind, name)] = arr.reshape(a[name].shape)

    loss = lax.psum(loss_local, ("x", "y", "c"))
    out = [loss, grad_x[None]]
    for kind in ('grad', 'delta', 'new_m', 'new_v'):
        out += [result[(kind, name)] for name in WEIGHT_NAMES]
    return tuple(out)


def _glu(x, W_in, b_in, T, D, tm, comm=None):
    n = W_in.shape[2]
    q = 2 if D // n % 2 == 0 else 1
    nt = D // (q * n)
    tn = q * n

    def body(x_ref, wa_ref, wg_ref, ba_ref, bg_ref, g_ref, ha_ref, hg_ref):
        xb = x_ref[...]
        ha = jnp.concatenate([_dot(xb, wa_ref[s], ((1,), (0,))) for s in range(q)], axis=1) + ba_ref[...]
        hg = jnp.concatenate([_dot(xb, wg_ref[s], ((1,), (0,))) for s in range(q)], axis=1) + bg_ref[...]
        g_ref[...] = ha * _sigmoid(hg)
        ha_ref[...] = ha.astype(ha_ref.dtype)
        hg_ref[...] = hg.astype(hg_ref.dtype)

    return _call(
        body, comm, name="conv_in_glu", grid=(T // tm, nt),
        in_specs=[pl.BlockSpec((tm, D), lambda i, j: (i, 0)),
                  pl.BlockSpec((q, D, n), lambda i, j: (j, 0, 0)),
                  pl.BlockSpec((q, D, n), lambda i, j: (j + nt, 0, 0)),
                  pl.BlockSpec((1, tn), lambda i, j: (0, j)), pl.BlockSpec((1, tn), lambda i, j: (0, j + nt))],
        out_specs=[pl.BlockSpec((tm, tn), lambda i, j: (i, j))] * 3,
        out_shape=[_sds((T, D), F32), _sds((T, D), BF16), _sds((T, D), BF16)],
        sem=("parallel", "parallel"), operands=[x, W_in, W_in, b_in, b_in])


def kernel(x, p, conv_w_in, conv_b_in, conv_w_dw, conv_b_dw, conv_ln_g, conv_ln_b, conv_w_out, conv_b_out, kv_w_k, kv_w_v, attn_w_q, attn_sinks, attn_w_o, mix_ln_g, mix_ln_b, mlp_w_up, mlp_w_down, mlp_ln_g, mlp_ln_b, ple_w_proj, ple_w_gate, loss_target, m_conv_w_in, m_conv_b_in, m_conv_w_dw, m_conv_b_dw, m_conv_ln_g, m_conv_ln_b, m_conv_w_out, m_conv_b_out, m_kv_w_k, m_kv_w_v, m_attn_w_q, m_attn_sinks, m_attn_w_o, m_mix_ln_g, m_mix_ln_b, m_mlp_w_up, m_mlp_w_down, m_mlp_ln_g, m_mlp_ln_b, m_ple_w_proj, m_ple_w_gate, v_conv_w_in, v_conv_b_in, v_conv_w_dw, v_conv_b_dw, v_conv_ln_g, v_conv_ln_b, v_conv_w_out, v_conv_b_out, v_kv_w_k, v_kv_w_v, v_attn_w_q, v_attn_sinks, v_attn_w_o, v_mix_ln_g, v_mix_ln_b, v_mlp_w_up, v_mlp_w_down, v_mlp_ln_g, v_mlp_ln_b, v_ple_w_proj, v_ple_w_gate):
    return _step(dict(locals()))
```

```python
import functools
import math

import jax
import jax.numpy as jnp
from jax import lax
from jax.experimental import pallas as pl
from jax.experimental.pallas import tpu as pltpu

F32 = jnp.float32
BF16 = jnp.bfloat16

N_DEV = 8
HEAD_DIM = 64
ROPE_DIM = HEAD_DIM // 4
ROPE_HALF = ROPE_DIM // 2
ROPE_THETA = 500000.0
ATT_BLOCK = 128
CONV_WIDTH = 31
CONV_HALO = 32
CONV_ROWS = 128
LN_EPS = 1e-5
DEPTH = 2
DEEPNORM_ALPHA = (2 * DEPTH) ** 0.25
MASK_VALUE = -1e30

ADAM_LR = 0.001
ADAM_B1 = 0.9
ADAM_B2 = 0.999
ADAM_EPS = 1e-08
ADAM_WD = 0.01
ADAM_STEP = 10

LANES = 128
SUBLANES = 8
VMEM_LIMIT_BYTES = 52 * 1024 * 1024
TOKEN_TILE = 512
MESH_ID = pl.DeviceIdType.MESH
RELAY_AT_TENTHS = 5
FORWARD_AT_TENTHS = 8

WEIGHT_NAMES = ['conv_w_in', 'conv_b_in', 'conv_w_dw', 'conv_b_dw', 'conv_ln_g', 'conv_ln_b', 'conv_w_out',
                'conv_b_out', 'kv_w_k', 'kv_w_v', 'attn_w_q', 'attn_sinks', 'attn_w_o', 'mix_ln_g', 'mix_ln_b',
                'mlp_w_up', 'mlp_w_down', 'mlp_ln_g', 'mlp_ln_b', 'ple_w_proj', 'ple_w_gate']
BIG_WEIGHTS = ['conv_w_in', 'conv_w_out', 'kv_w_k', 'kv_w_v', 'attn_w_q', 'attn_w_o', 'mlp_w_up', 'mlp_w_down',
               'ple_w_proj', 'ple_w_gate']
SMALL_SHARDED = [('conv_b_in', 8), ('conv_w_dw', 32), ('conv_b_dw', 8), ('conv_ln_g', 8), ('conv_ln_b', 8),
                 ('conv_b_out', 8)]
REPLICATED = ['mix_ln_g', 'mix_ln_b', 'mlp_ln_g', 'mlp_ln_b', 'attn_sinks']


def _params(sem):
    return pltpu.CompilerParams(dimension_semantics=sem, vmem_limit_bytes=VMEM_LIMIT_BYTES)


def _sds(shape, dtype):
    return jax.ShapeDtypeStruct(shape, dtype)


def _my_place():
    x, y, c = lax.axis_index("x"), lax.axis_index("y"), lax.axis_index("c")
    return x, y, c, 4 * x + 2 * y + c


def _peers(x, y, c):
    out = []
    for dx in (0, 1):
        for dy in (0, 1):
            for dc in (0, 1):
                if dx or dy or dc:
                    px, py, pc = x ^ dx, y ^ dy, c ^ dc
                    out.append(((px, py, pc), 4 * px + 2 * py + pc))
    return out


N_CHIPS = 4


def _other_chips(x, y):
    return [((x ^ dx, y ^ dy), 2 * (x ^ dx) + (y ^ dy)) for dx, dy in ((1, 0), (0, 1), (1, 1))]


def _remote(src, dst, send, recv, to):
    return pltpu.make_async_remote_copy(src_ref=src, dst_ref=dst, send_sem=send, recv_sem=recv, device_id=to,
                                        device_id_type=MESH_ID)


def _wait_slabs(buf, count, send, recv, me, sent=True, received=True):
    part = buf.at[pl.ds(0, count)]
    cp = _remote(part, part, send, recv, me)
    if sent:
        cp.wait_send()
    if received:
        cp.wait_recv()


class _DirectJob:
    n_sems = 3

    def __init__(self, items):
        self.sources = list(items)
        self.dests = [_sds(it.shape, it.dtype) for it in items]
        self.n = len(items)

    def start(self, src, dst, send, recv, loc):
        x, y, c, me = _my_place()
        for k in range(self.n):
            here = dst[k].at[pl.ds(me, 1)]
            pltpu.make_async_copy(src[k].at[pl.ds(me, 1)], here, loc.at[k]).start()
            for peer, idx in _peers(x, y, c):
                _remote(src[k].at[pl.ds(idx, 1)], here, send.at[k], recv.at[k], peer).start()

    def relay(self, *refs):
        pass

    def forward(self, *refs):
        pass

    def finish(self, src, dst, send, recv, loc):
        x, y, c, me = _my_place()
        for k in range(self.n):
            pltpu.make_async_copy(src[k].at[pl.ds(me, 1)], dst[k].at[pl.ds(me, 1)], loc.at[k]).wait()
            _wait_slabs(dst[k], N_DEV - 1, send.at[k], recv.at[k], (x, y, c))


class _GatherJob:
    n_sems = 7

    def __init__(self, items):
        self.sources = [it.reshape((1,) + it.shape) for it in items]
        self.dests = [_sds((N_DEV,) + it.shape, it.dtype) for it in items]
        self.n = len(items)

    @staticmethod
    def _neighbours(x, y, c):
        out = []
        for dx, dy in ((c, 1 - c), (1 - c, c), (1, 1)):
            px, py = x ^ dx, y ^ dy
            out.append(((px, py, c), 4 * px + 2 * py + c))
        return out

    def start(self, src, dst, send_a, recv_a, send_b, recv_b, send_d, recv_d, loc):
        x, y, c, me = _my_place()
        (first, _), (second, _), _ = self._neighbours(x, y, c)
        for k in range(self.n):
            here = dst[k].at[pl.ds(me, 1)]
            pltpu.make_async_copy(src[k], here, loc.at[k]).start()
            _remote(src[k], here, send_d.at[k], recv_d.at[k], (x, y, 1 - c)).start()
            _remote(src[k], here, send_a.at[k], recv_a.at[k], first).start()
            _remote(src[k], here, send_b.at[k], recv_b.at[k], second).start()

    def relay(self, src, dst, send_a, recv_a, send_b, recv_b, send_d, recv_d, loc):
        x, y, c, me = _my_place()
        (_, first_idx), (second, _), _ = self._neighbours(x, y, c)
        for k in range(self.n):
            _wait_slabs(dst[k], 1, send_a.at[k], recv_a.at[k], (x, y, c), sent=False)
            rows = dst[k].at[pl.ds(first_idx, 1)]
            _remote(rows, rows, send_b.at[k], recv_b.at[k], second).start()

    def forward(self, src, dst, send_a, recv_a, send_b, recv_b, send_d, recv_d, loc):
        x, y, c, me = _my_place()
        for k in range(self.n):
            _wait_slabs(dst[k], 2, send_b.at[k], recv_b.at[k], (x, y, c), sent=False)
            for _, idx in self._neighbours(x, y, c):
                rows = dst[k].at[pl.ds(idx, 1)]
                _remote(rows, rows, send_d.at[k], recv_d.at[k], (x, y, 1 - c)).start()

    def finish(self, src, dst, send_a, recv_a, send_b, recv_b, send_d, recv_d, loc):
        x, y, c, me = _my_place()
        for k in range(self.n):
            pltpu.make_async_copy(src[k], dst[k].at[pl.ds(me, 1)], loc.at[k]).wait()
            _wait_slabs(dst[k], 1, send_a.at[k], recv_a.at[k], (x, y, c), received=False)
            _wait_slabs(dst[k], 2, send_b.at[k], recv_b.at[k], (x, y, c), received=False)
            _wait_slabs(dst[k], N_CHIPS, send_d.at[k], recv_d.at[k], (x, y, c))


class _PairJob:
    n_sems = 2

    def __init__(self, items):
        self.sources = list(items)
        self.dests = [_sds((N_CHIPS,) + it.shape[1:], it.dtype) for it in items]
        self.n = len(items)

    def start(self, src, dst, send, recv):
        x, y, c, me = _my_place()
        for k in range(self.n):
            for q in range(N_CHIPS):
                _remote(src[k].at[pl.ds(2 * q + 1 - c, 1)], dst[k].at[pl.ds(q, 1)], send.at[k], recv.at[k],
                        (x, y, 1 - c)).start()

    def relay(self, *refs):
        pass

    def forward(self, *refs):
        pass

    def finish(self, src, dst, send, recv):
        x, y, c, me = _my_place()
        for k in range(self.n):
            _wait_slabs(dst[k], N_CHIPS, send.at[k], recv.at[k], (x, y, c))


class _ChipJob:
    n_sems = 3

    def __init__(self, items):
        self.sources = list(items)
        self.dests = [_sds(it.shape, it.dtype) for it in items]
        self.n = len(items)

    def start(self, src, dst, send, recv, loc):
        x, y, c, me = _my_place()
        mine = 2 * x + y
        for k in range(self.n):
            here = dst[k].at[pl.ds(mine, 1)]
            pltpu.make_async_copy(src[k].at[pl.ds(mine, 1)], here, loc.at[k]).start()
            for (px, py), q in _other_chips(x, y):
                _remote(src[k].at[pl.ds(q, 1)], here, send.at[k], recv.at[k], (px, py, c)).start()

    def relay(self, *refs):
        pass

    def forward(self, *refs):
        pass

    def finish(self, src, dst, send, recv, loc):
        x, y, c, me = _my_place()
        mine = 2 * x + y
        for k in range(self.n):
            pltpu.make_async_copy(src[k].at[pl.ds(mine, 1)], dst[k].at[pl.ds(mine, 1)], loc.at[k]).wait()
            _wait_slabs(dst[k], N_CHIPS - 1, send.at[k], recv.at[k], (x, y, c))


class _Jobs:
    def __init__(self, jobs):
        self.jobs = list(jobs)
        self.sources = [s for job in self.jobs for s in job.sources]
        self.dests = [d for job in self.jobs for d in job.dests]
        self.n = len(self.sources)

    def _stage(self, stage, src, dst, sems):
        k0 = s0 = 0
        for job in self.jobs:
            getattr(job, stage)(src[k0:k0 + job.n], dst[k0:k0 + job.n], *sems[s0:s0 + job.n_sems])
            k0 += job.n
            s0 += job.n_sems

    def start(self, src, dst, *sems):
        self._stage('start', src, dst, sems)

    def relay(self, src, dst, *sems):
        self._stage('relay', src, dst, sems)

    def forward(self, src, dst, *sems):
        self._stage('forward', src, dst, sems)

    def finish(self, src, dst, *sems):
        self._stage('finish', src, dst, sems)

    def split(self, got):
        out, k0 = [], 0
        for job in self.jobs:
            out.append(got[k0:k0 + job.n])
            k0 += job.n
        return out


def _job_sems(job):
    if isinstance(job, _Jobs):
        return [s for part in job.jobs for s in _job_sems(part)]
    return [pltpu.SemaphoreType.DMA((job.n,))] * job.n_sems


def _call(body, comm, *, name, grid, in_specs, out_specs, out_shape, operands, sem, scratch_shapes=(), aliases=None):
    single = not isinstance(out_shape, (list, tuple))
    out_shape = [out_shape] if single else list(out_shape)
    out_specs = [out_specs] if single else list(out_specs)
    if comm is None:
        res = pl.pallas_call(body, name=name, grid=grid, in_specs=list(in_specs), out_specs=out_specs,
                             out_shape=out_shape, scratch_shapes=list(scratch_shapes),
                             input_output_aliases=aliases or {}, compiler_params=_params(sem))(*operands)
        return res[0] if single else res
    n_in, n_out, n_scr, n_c = len(in_specs), len(out_shape), len(scratch_shapes), comm.n
    any_spec = pl.BlockSpec(memory_space=pl.ANY)
    steps = math.prod(grid)
    early = min(steps - 1, (steps * RELAY_AT_TENTHS) // 10)
    mid = min(steps - 1, (steps * FORWARD_AT_TENTHS) // 10)

    def hosted(*refs):
        ins, c_src = refs[:n_in], refs[n_in:n_in + n_c]
        outs = refs[n_in + n_c:n_in + n_c + n_out]
        c_dst = refs[n_in + n_c + n_out:n_in + 2 * n_c + n_out]
        scr = refs[n_in + 2 * n_c + n_out:n_in + 2 * n_c + n_out + n_scr]
        sems = refs[n_in + 2 * n_c + n_out + n_scr:]
        step = pl.program_id(0)
        for d in range(1, len(grid)):
            step = step * grid[d] + pl.program_id(d)

        @pl.when(step == 0)
        def _():
            comm.start(c_src, c_dst, *sems)

        @pl.when(step == early)
        def _():
            comm.relay(c_src, c_dst, *sems)

        @pl.when(step == mid)
        def _():
            comm.forward(c_src, c_dst, *sems)

        body(*ins, *outs, *scr)

        @pl.when(step == steps - 1)
        def _():
            comm.finish(c_src, c_dst, *sems)

    res = pl.pallas_call(hosted, name=name, grid=grid, in_specs=list(in_specs) + [any_spec] * n_c,
                         out_specs=out_specs + [any_spec] * n_c, out_shape=out_shape + comm.dests,
                         scratch_shapes=list(scratch_shapes) + _job_sems(comm), input_output_aliases=aliases or {},
                         compiler_params=_params(("arbitrary",) * len(grid)))(*operands, *comm.sources)
    main = res[:n_out]
    return (main[0] if single else main), res[n_out:]


def comm_only(name, jobs):
    any_spec = pl.BlockSpec(memory_space=pl.ANY)
    n_all = sum(job.n for job in jobs)

    def body(*refs):
        srcs, dsts, sems = refs[:n_all], refs[n_all:2 * n_all], refs[2 * n_all:]
        parts, k0, s0 = [], 0, 0
        for job in jobs:
            parts.append((job, srcs[k0:k0 + job.n], dsts[k0:k0 + job.n], sems[s0:s0 + job.n_sems]))
            k0 += job.n
            s0 += job.n_sems
        for stage in ('start', 'relay', 'forward', 'finish'):
            for job, src, dst, sem in parts:
                getattr(job, stage)(src, dst, *sem)

    res = pl.pallas_call(body, name=name, in_specs=[any_spec] * n_all, out_specs=[any_spec] * n_all,
                         out_shape=[d for job in jobs for d in job.dests],
                         scratch_shapes=[s for job in jobs for s in _job_sems(job)],
                         )(*[s for job in jobs for s in job.sources])
    out, k0 = [], 0
    for job in jobs:
        out.append(res[k0:k0 + job.n])
        k0 += job.n
    return out


def _dot(a, b, dims):
    if a.dtype != BF16:
        a = a.astype(BF16)
    if b.dtype != BF16:
        b = b.astype(BF16)
    return lax.dot_general(a, b, (dims, ((), ())), preferred_element_type=F32)


def _sigmoid(v):
    return 1.0 / (1.0 + jnp.exp(-v))


def _ln_stats(z):
    mu = jnp.mean(z, axis=-1, keepdims=True)
    zc = z - mu
    var = jnp.mean(zc * zc, axis=-1, keepdims=True)
    return zc * lax.rsqrt(var + LN_EPS)


def _ln_fwd(z, g, b):
    return _ln_stats(z) * g + b


def _ln_bwd(dy, z, g):
    xhat = _ln_stats(z)
    mu = jnp.mean(z, axis=-1, keepdims=True)
    zc = z - mu
    rstd = lax.rsqrt(jnp.mean(zc * zc, axis=-1, keepdims=True) + LN_EPS)
    dxh = dy * g
    m1 = jnp.mean(dxh, axis=-1, keepdims=True)
    m2 = jnp.mean(dxh * xhat, axis=-1, keepdims=True)
    dz = rstd * (dxh - m1 - xhat * m2)
    return dz, jnp.sum(dy * xhat, axis=0, keepdims=True), jnp.sum(dy, axis=0, keepdims=True)


def _extra_spec(shape, kind, tm, tn, ij):
    if kind == 'tile':
        return pl.BlockSpec((tm, tn), lambda *g: (ij(g)[0], ij(g)[1]))
    if kind in ('row', 'rowacc'):
        return pl.BlockSpec((1, tn), lambda *g: (0, ij(g)[1]))
    if kind == 'tab':
        return pl.BlockSpec((tm, LANES), lambda *g: (ij(g)[0], 0))
    if kind == 'rows':
        return pl.BlockSpec((tm, shape[1]), lambda *g: (ij(g)[0], 0))
    raise ValueError(kind)


def mm_nn(name, pairs, extras, outs, epi, *, tm, tn, tk=None, comm=None):
    M = pairs[0][0].shape[0]
    N = pairs[0][1].shape[0] * pairs[0][1].shape[2]
    n_pairs = len(pairs)
    K0 = pairs[0][0].shape[1]
    tk = K0 if tk is None else tk
    nk = K0 // tk
    assert nk == 1 or n_pairs == 1
    assert M % tm == 0 and N % tn == 0 and K0 % tk == 0
    has_rowacc = any(kind == 'rowacc' for _, kind in outs)
    assert not has_rowacc or (N == tn and nk == 1)
    in_specs, operands, slabs = [], [], []
    for a, b, off in pairs:
        K = a.shape[1]
        ktile = K if n_pairs > 1 else tk
        n = b.shape[2]
        assert b.shape[1] == K
        in_specs.append(pl.BlockSpec((tm, ktile), lambda i, j, k: (i, k)))
        if tn <= n:
            assert n % tn == 0
            r = n // tn
            in_specs.append(pl.BlockSpec((None, ktile, tn),
                                         lambda i, j, k, r=r, off=off: ((j + off) // r, k, (j + off) % r)))
            slabs.append(0)
        else:
            assert tn % n == 0
            in_specs.append(pl.BlockSpec((tn // n, ktile, n), lambda i, j, k, off=off: (j + off, k, 0)))
            slabs.append(tn // n)
        operands += [a, b]
    ij = lambda g: (g[0], g[1])
    for arr, kind in extras:
        in_specs.append(_extra_spec(arr.shape, kind, tm, tn, ij))
        operands.append(arr)
    out_specs = [_extra_spec(o.shape, kind, tm, tn, ij) for o, kind in outs]
    n_ex, n_out = len(extras), len(outs)

    def pair_dot(ab, q):
        a = ab[2 * q][...]
        if not slabs[q]:
            return _dot(a, ab[2 * q + 1][...], ((1,), (0,)))
        return jnp.concatenate([_dot(a, ab[2 * q + 1][s], ((1,), (0,))) for s in range(slabs[q])], axis=1)

    def body(*refs):
        ab = refs[:2 * n_pairs]
        ex = refs[2 * n_pairs:2 * n_pairs + n_ex]
        out = refs[2 * n_pairs + n_ex:2 * n_pairs + n_ex + n_out]
        i = pl.program_id(0)
        if nk == 1:
            accs = [pair_dot(ab, q) for q in range(n_pairs)]
            epi(accs, ex, out, i)
        else:
            acc_ref = refs[-1]
            k = pl.program_id(2)

            @pl.when(k == 0)
            def _():
                acc_ref[...] = jnp.zeros_like(acc_ref)

            acc_ref[...] += pair_dot(ab, 0)

            @pl.when(k == nk - 1)
            def _():
                epi([acc_ref[...]], ex, out, i)

    scratch = [pltpu.VMEM((tm, tn), F32)] if nk > 1 else []
    sem = ("arbitrary",) * 3 if has_rowacc else ("parallel", "parallel", "arbitrary")
    return _call(body, comm, name=name, grid=(M // tm, N // tn, nk), in_specs=in_specs, out_specs=out_specs,
                 out_shape=[o for o, _ in outs], scratch_shapes=scratch, sem=sem, operands=operands)


def mm_nt(name, a, b, extras, outs, epi, *, tm, tko, tc, comm=None, a_pro=None, dws=()):
    M, N = a.shape
    J, K, n = b.shape
    assert J * n == N and M % tm == 0 and K % tko == 0 and N % tc == 0
    nc = N // tc
    assert a_pro is None or (nc == 1 and tc <= n and K == tko == N)
    assert not dws or (nc == 1 and K == tko and tc <= n)
    n_dw, nt = len(dws), M // tm
    has_rowacc = any(kind == 'rowacc' for _, kind in outs)
    assert not has_rowacc or K == tko
    if tc <= n:
        assert n % tc == 0
        r = n // tc
        slabs = 0
        b_spec = pl.BlockSpec((None, tko, tc), lambda i, j, c: (c // r, j, c % r))
    else:
        assert tc % n == 0
        slabs = tc // n
        b_spec = pl.BlockSpec((slabs, tko, n), lambda i, j, c: (c, j, 0))
    n_ab = 2 if a_pro is None else 1
    in_specs = ([pl.BlockSpec((tm, tc), lambda i, j, c: (i, c))] if a_pro is None else []) + [b_spec]

    def nt_dot(a_ref, b_ref):
        if not slabs:
            return _dot(a_ref[...], b_ref[...], ((1,), (1,)))
        acc = _dot(a_ref[:, 0:n], b_ref[0], ((1,), (1,)))
        for s in range(1, slabs):
            acc = acc + _dot(a_ref[:, s * n:(s + 1) * n], b_ref[s], ((1,), (1,)))
        return acc

    ij = lambda g: (g[0], g[1])
    operands = [a, b] if a_pro is None else [b]
    for arr, kind in extras:
        in_specs.append(_extra_spec(arr.shape, kind, tm, tko, ij))
        operands.append(arr)
    out_specs = [_extra_spec(o.shape, kind, tm, tko, ij) for o, kind in outs]
    n_ex, n_out = len(extras), len(outs)
    out_shape = [o for o, _ in outs]
    for lhs, _ in dws:
        in_specs.append(pl.BlockSpec((tm, lhs.shape[1]), lambda i, j, c: (i, 0)))
        operands.append(lhs)
        out_specs.append(pl.BlockSpec((None, lhs.shape[1], N), lambda i, j, c: (0, 0, 0)))
        out_shape.append(_sds((1, lhs.shape[1], N), BF16))

    def weight_grads(refs, a_tile, i):
        lhs_refs = refs[n_ab + n_ex:n_ab + n_ex + n_dw]
        dw_refs = refs[n_ab + n_ex + n_dw + n_out:n_ab + n_ex + n_dw + n_out + n_dw]
        acc_refs = refs[len(refs) - n_dw:]
        for (_, rhs_fn), lhs_ref, dw_ref, acc_ref in zip(dws, lhs_refs, dw_refs, acc_refs):
            part = _dot(lhs_ref[...], a_tile if rhs_fn is None else rhs_fn(), ((0,), (0,)))
            _init_or_add(acc_ref, i, part)

            @pl.when(i == nt - 1)
            def _(dw_ref=dw_ref, acc_ref=acc_ref):
                dw_ref[...] = acc_ref[...].astype(dw_ref.dtype)

    def body(*refs):
        ex = refs[n_ab:n_ab + n_ex]
        out = refs[n_ab + n_ex + n_dw:n_ab + n_ex + n_dw + n_out]
        i = pl.program_id(0)
        if a_pro is not None:
            a_tile = a_pro(ex)
            epi(_dot(a_tile, refs[0][...], ((1,), (1,))), ex, out, i)
            weight_grads(refs, a_tile, i)
            return
        a_ref, b_ref = refs[:2]
        if nc == 1:
            epi(nt_dot(a_ref, b_ref), ex, out, i)
            if dws:
                weight_grads(refs, a_ref[...], i)
        else:
            acc_ref = refs[-1]
            c = pl.program_id(2)

            @pl.when(c == 0)
            def _():
                acc_ref[...] = jnp.zeros_like(acc_ref)

            acc_ref[...] += nt_dot(a_ref, b_ref)

            @pl.when(c == nc - 1)
            def _():
                epi(acc_ref[...], ex, out, i)

    scratch = ([pltpu.VMEM((tm, tko), F32)] if nc > 1 else []) + [pltpu.VMEM((lhs.shape[1], N), F32) for lhs, _ in dws]
    sem = ("arbitrary",) * 3 if (has_rowacc or dws) else ("parallel", "parallel", "arbitrary")
    return _call(body, comm, name=name, grid=(M // tm, K // tko, nc), in_specs=in_specs, out_specs=out_specs,
                 out_shape=out_shape, scratch_shapes=scratch, sem=sem, operands=operands)


def mm_tn(name, a, d, n, out_dtype, *, tm, tk, tn, comm=None):
    M, K = a.shape
    N = d.shape[1]
    assert d.shape[0] == M and N % n == 0 and N % tn == 0 and K % tk == 0 and M % tm == 0
    nm = M // tm
    if tn <= n:
        assert n % tn == 0
        r = n // tn
        slabs = 0
        o_spec = pl.BlockSpec((None, tk, tn), lambda kk, j, m: (j // r, kk, j % r))
    else:
        assert tn % n == 0
        slabs = tn // n
        o_spec = pl.BlockSpec((slabs, tk, n), lambda kk, j, m: (j, kk, 0))

    def write(o_ref, acc):
        if not slabs:
            o_ref[...] = acc.astype(o_ref.dtype)
        else:
            for s in range(slabs):
                o_ref[s] = acc[:, s * n:(s + 1) * n].astype(o_ref.dtype)

    def body(a_ref, d_ref, o_ref, *scratch):
        if nm == 1:
            write(o_ref, _dot(a_ref[...], d_ref[...], ((0,), (0,))))
            return
        acc_ref, = scratch
        m = pl.program_id(2)

        @pl.when(m == 0)
        def _():
            acc_ref[...] = jnp.zeros_like(acc_ref)

        acc_ref[...] += _dot(a_ref[...], d_ref[...], ((0,), (0,)))

        @pl.when(m == nm - 1)
        def _():
            write(o_ref, acc_ref[...])

    return _call(
        body, comm, name=name, grid=(K // tk, N // tn, nm),
        in_specs=[pl.BlockSpec((tm, tk), lambda kk, j, m: (m, kk)), pl.BlockSpec((tm, tn), lambda kk, j, m: (m, j))],
        out_specs=o_spec, out_shape=_sds((N // n, K, n), out_dtype),
        scratch_shapes=[pltpu.VMEM((tk, tn), F32)] if nm > 1 else [],
        sem=("parallel", "parallel", "arbitrary"), operands=[a, d])


def mlp_du_dwup(name, dzb, w_down, u, xin, n, *, tm, tf, comm=None):
    T, D = dzb.shape
    F = u.shape[1]
    assert T % tm == 0 and F % tf == 0 and tf % n == 0
    slabs, nt = tf // n, T // tm

    def body(dz_ref, w_ref, u_ref, x_ref, du_ref, dw_ref, acc_ref):
        i = pl.program_id(1)
        da = _dot(dz_ref[...], w_ref[...], ((1,), (1,)))
        du = (da * (2.0 * jnp.maximum(u_ref[...].astype(F32), 0.0))).astype(BF16)
        du_ref[...] = du
        part = _dot(x_ref[...], du, ((0,), (0,)))

        @pl.when(i == 0)
        def _():
            acc_ref[...] = part

        @pl.when(i > 0)
        def _():
            acc_ref[...] += part

        @pl.when(i == nt - 1)
        def _():
            for s in range(slabs):
                dw_ref[s] = acc_ref[:, s * n:(s + 1) * n].astype(dw_ref.dtype)

    return _call(
        body, comm, name=name, grid=(F // tf, nt),
        in_specs=[pl.BlockSpec((tm, D), lambda j, i: (i, 0)), pl.BlockSpec((None, tf, D), lambda j, i: (0, j, 0)),
                  pl.BlockSpec((tm, tf), lambda j, i: (i, j)), pl.BlockSpec((tm, D), lambda j, i: (i, 0))],
        out_specs=[pl.BlockSpec((tm, tf), lambda j, i: (i, j)), pl.BlockSpec((slabs, D, n), lambda j, i: (j, 0, 0))],
        out_shape=[_sds((T, F), BF16), _sds((F // n, D, n), BF16)],
        scratch_shapes=[pltpu.VMEM((D, tf), F32)], sem=("arbitrary", "arbitrary"), operands=[dzb, w_down, u, xin])


def _init_or_add(ref, i, value):
    @pl.when(i == 0)
    def _():
        ref[...] = value

    @pl.when(i > 0)
    def _():
        ref[...] += value


def _rope_tables(T):
    pos = jnp.arange(T, dtype=F32)
    inv_freq = ROPE_THETA ** (-jnp.arange(0, ROPE_DIM, 2, dtype=F32) / ROPE_DIM)
    ang = pos[:, None] * inv_freq[None, :]
    cos, sin = jnp.cos(ang), jnp.sin(ang)
    ones = jnp.ones((T, HEAD_DIM - ROPE_DIM), F32)
    zeros = jnp.zeros((T, HEAD_DIM - ROPE_DIM), F32)
    zh = jnp.zeros((T, ROPE_HALF), F32)
    c_head = jnp.concatenate([cos, cos, ones], axis=1)
    s_up = jnp.concatenate([-sin, zh, zeros], axis=1)
    s_dn = jnp.concatenate([zh, sin, zeros], axis=1)
    rep = LANES // HEAD_DIM
    return tuple(jnp.tile(t, (1, rep)) for t in (c_head, s_up, s_dn))


def _rope_chunk(t, c, s_up, s_dn):
    return t * c + pltpu.roll(t, LANES - ROPE_HALF, 1) * s_up + pltpu.roll(t, ROPE_HALF, 1) * s_dn


def _rope_chunk_bwd(d, c, s_up, s_dn):
    return d * c + pltpu.roll(d * s_up, ROPE_HALF, 1) + pltpu.roll(d * s_dn, LANES - ROPE_HALF, 1)


def _rope_wide(t, c, s_up, s_dn, fn):
    chunks = [fn(t[:, q * LANES:(q + 1) * LANES], c, s_up, s_dn) for q in range(t.shape[1] // LANES)]
    return chunks[0] if len(chunks) == 1 else jnp.concatenate(chunks, axis=1)


def _taps_by_residue(first):
    groups = {}
    for o in range(first, first + CONV_WIDTH):
        groups.setdefault(o % SUBLANES, []).append(o)
    return sorted(groups.items())


def _shifted_taps(win_ref, span_ref, r0, res, offs, rb, ls):
    if res == 0:
        return [functools.partial(lambda o: win_ref[pl.ds(r0 + o, rb), ls], o) for o in offs]
    n = rb + offs[-1] - res
    span_ref[0:n, :] = win_ref[pl.ds(r0 + res, n), ls]
    return [functools.partial(lambda o: span_ref[pl.ds(o - res, rb), :], o) for o in offs]


def _halo_before_spec(tm, D):
    return pl.BlockSpec((CONV_HALO, D), lambda i: (jnp.maximum(i * (tm // CONV_HALO) - 1, 0), 0))


def dwconv_fwd(g, w_dw, b_dw, ln_g, ln_b, *, tm, comm=None):
    T, D = g.shape
    nl = D // LANES

    rb = min(CONV_ROWS, tm)

    def body(g_ref, gh_ref, w_ref, b_ref, lg_ref, lb_ref, c_ref, s_ref, win_ref, span_ref):
        i = pl.program_id(0)
        win_ref[0:CONV_HALO, :] = jnp.where(i > 0, gh_ref[...], 0.0)
        win_ref[CONV_HALO:, :] = g_ref[...]

        def lane_chunk(q, carry):
            ls = pl.ds(pl.multiple_of(q * LANES, LANES), LANES)
            for r0 in range(0, tm, rb):
                acc = jnp.broadcast_to(b_ref[:, ls], (rb, LANES))
                for res, offs in _taps_by_residue(CONV_HALO - (CONV_WIDTH - 1)):
                    taps = _shifted_taps(win_ref, span_ref, r0, res, offs, rb, ls)
                    for o, tap in zip(offs, taps):
                        k = o - (CONV_HALO - (CONV_WIDTH - 1))
                        acc = acc + tap() * w_ref[k:k + 1, ls]
                c_ref[pl.ds(r0, rb), ls] = acc
            return carry

        lax.fori_loop(0, nl, lane_chunk, 0)
        n = _ln_fwd(c_ref[...], lg_ref[...], lb_ref[...])
        s_ref[...] = (n * _sigmoid(n)).astype(s_ref.dtype)

    row = pl.BlockSpec((1, D), lambda i: (0, 0))
    return _call(
        body, comm, name="dwconv_fwd", grid=(T // tm,),
        in_specs=[pl.BlockSpec((tm, D), lambda i: (i, 0)), _halo_before_spec(tm, D),
                  pl.BlockSpec((CONV_HALO, D), lambda i: (0, 0)), row, row, row],
        out_specs=[pl.BlockSpec((tm, D), lambda i: (i, 0)), pl.BlockSpec((tm, D), lambda i: (i, 0))],
        out_shape=[_sds((T, D), F32), _sds((T, D), BF16)],
        scratch_shapes=[pltpu.VMEM((tm + CONV_HALO, D), F32), pltpu.VMEM((rb + CONV_HALO, LANES), F32)],
        sem=("parallel",), operands=[g, g, w_dw, b_dw, ln_g, ln_b])


def dwconv_bwd(dc, g, ha, hg, w_dw, *, tm, comm=None):
    T, D = g.shape
    nl = D // LANES
    last = T // CONV_HALO - 1
    nt = T // tm

    rb = min(CONV_ROWS, tm)

    def body(dc_ref, dcn_ref, g_ref, gh_ref, ha_ref, hg_ref, w_ref, dh_ref, dw_ref, dbdw_ref, dbin_ref,
             win_ref, dwin_ref, dg_ref, dwp_ref, span_ref):
        i = pl.program_id(0)
        win_ref[0:CONV_HALO, :] = jnp.where(i > 0, gh_ref[...], 0.0)
        win_ref[CONV_HALO:, :] = g_ref[...]
        dwin_ref[0:tm, :] = dc_ref[...]
        dwin_ref[tm:, :] = jnp.where(i < nt - 1, dcn_ref[...], 0.0)

        @pl.when(i == 0)
        def _():
            dwp_ref[...] = jnp.zeros_like(dwp_ref)

        first = CONV_HALO - (CONV_WIDTH - 1)

        def lane_chunk(q, carry):
            ls = pl.ds(pl.multiple_of(q * LANES, LANES), LANES)
            for r0 in range(0, tm, rb):
                acc = jnp.zeros((rb, LANES), F32)
                for res, offs in _taps_by_residue(0):
                    taps = _shifted_taps(dwin_ref, span_ref, r0, res, offs, rb, ls)
                    for o, tap in zip(offs, taps):
                        k = CONV_WIDTH - 1 - o
                        acc = acc + tap() * w_ref[k:k + 1, ls]
                dg_ref[pl.ds(r0, rb), ls] = acc
                dcv = dwin_ref[pl.ds(r0, rb), ls]
                for res, offs in _taps_by_residue(first):
                    taps = _shifted_taps(win_ref, span_ref, r0, res, offs, rb, ls)
                    for o, tap in zip(offs, taps):
                        k = o - first
                        prod = dcv * tap()
                        dwp_ref[k, :, ls] += jnp.sum(prod.reshape(rb // SUBLANES, SUBLANES, LANES), axis=0)
            return carry

        lax.fori_loop(0, nl, lane_chunk, 0)

        @pl.when(i == nt - 1)
        def _():
            for k in range(CONV_WIDTH):
                dw_ref[k:k + 1, :] = jnp.sum(dwp_ref[k], axis=0, keepdims=True)
            dw_ref[CONV_WIDTH:, :] = jnp.zeros((CONV_HALO - CONV_WIDTH, D), F32)
        dg = dg_ref[...]
        ha = ha_ref[...].astype(F32)
        sg = _sigmoid(hg_ref[...].astype(F32))
        d_ha = dg * sg
        d_hg = dg * ha * sg * (1.0 - sg)
        dh_ref[:, 0:D] = d_ha.astype(dh_ref.dtype)
        dh_ref[:, D:] = d_hg.astype(dh_ref.dtype)
        _init_or_add(dbdw_ref, i, jnp.sum(dc_ref[...], axis=0, keepdims=True))
        _init_or_add(dbin_ref, i, jnp.concatenate([jnp.sum(d_ha, axis=0, keepdims=True),
                                                   jnp.sum(d_hg, axis=0, keepdims=True)], axis=1))

    tile = pl.BlockSpec((tm, D), lambda i: (i, 0))
    return _call(
        body, comm, name="dwconv_bwd", grid=(nt,),
        in_specs=[tile,
                  pl.BlockSpec((CONV_HALO, D), lambda i: (jnp.minimum((i + 1) * (tm // CONV_HALO), last), 0)),
                  tile, _halo_before_spec(tm, D),
                  tile, tile, pl.BlockSpec((CONV_HALO, D), lambda i: (0, 0))],
        out_specs=[pl.BlockSpec((tm, 2 * D), lambda i: (i, 0)), pl.BlockSpec((CONV_HALO, D), lambda i: (0, 0)),
                   pl.BlockSpec((1, D), lambda i: (0, 0)), pl.BlockSpec((1, 2 * D), lambda i: (0, 0))],
        out_shape=[_sds((T, 2 * D), BF16), _sds((CONV_HALO, D), F32), _sds((1, D), F32), _sds((1, 2 * D), F32)],
        scratch_shapes=[pltpu.VMEM((tm + CONV_HALO, D), F32), pltpu.VMEM((tm + CONV_HALO, D), F32),
                        pltpu.VMEM((tm, D), F32), pltpu.VMEM((CONV_WIDTH, SUBLANES, D), F32),
                        pltpu.VMEM((rb + CONV_HALO, LANES), F32)],
        sem=("arbitrary",), operands=[dc, dc, g, g, ha, hg, w_dw])


def _attn_specs(HD, W):
    B = ATT_BLOCK
    return [pl.BlockSpec((B, HD), lambda n: (n, 0)),
            pl.BlockSpec((B, 4 * W), lambda n: (n, 0)),
            pl.BlockSpec((B, 4 * W), lambda n: (jnp.maximum(n - 1, 0), 0))]


def _band_mask(n):
    r = lax.broadcasted_iota(jnp.int32, (ATT_BLOCK, 2 * ATT_BLOCK), 0)
    j = lax.broadcasted_iota(jnp.int32, (ATT_BLOCK, 2 * ATT_BLOCK), 1)
    return (j > r) & (j <= r + ATT_BLOCK) & ((n > 0) | (j >= ATT_BLOCK))


def _band(kvc_ref, kvp_ref, part, g, parity, W):
    lanes = slice((2 * part + parity) * W + g * LANES, (2 * part + parity) * W + (g + 1) * LANES)
    return jnp.concatenate([kvp_ref[:, lanes], kvc_ref[:, lanes]], axis=0)


def _half_mask(parity):
    lane = lax.broadcasted_iota(jnp.int32, (1, LANES), 1)
    return (lane < HEAD_DIM) if parity == 0 else (lane >= HEAD_DIM)


def widen_kv(k, v, out_ref, n_kv):
    W = n_kv * LANES
    low = _half_mask(0)
    for part, src in enumerate((k, v)):
        for cg in range(n_kv * HEAD_DIM // LANES):
            chunk = src[:, cg * LANES:(cg + 1) * LANES]
            swapped = pltpu.roll(chunk, HEAD_DIM, 1)
            for g, lo, hi in ((2 * cg, chunk, swapped), (2 * cg + 1, swapped, chunk)):
                base = 2 * part * W + g * LANES
                out_ref[:, base:base + LANES] = jnp.where(low, lo, 0.0).astype(out_ref.dtype)
                out_ref[:, base + W:base + W + LANES] = jnp.where(low, 0.0, hi).astype(out_ref.dtype)


def attn_fwd(q, kvx, sinks, n_heads, n_kv, comm=None):
    T, HD = q.shape
    W = n_kv * LANES
    B = ATT_BLOCK
    chunks_per_group = n_heads // n_kv // 2
    scale = 1.0 / math.sqrt(HEAD_DIM)

    def body(q_ref, kvc_ref, kvp_ref, sink_ref, o_ref, lse_ref):
        n = pl.program_id(0)
        mask = jnp.tile(_band_mask(n), (n_heads, 1))
        s = jnp.concatenate(
            [_dot(q_ref[:, (h // 2) * LANES:(h // 2 + 1) * LANES],
                  _band(kvc_ref, kvp_ref, 0, h // 2 // chunks_per_group, h % 2, W), ((1,), (1,)))
             for h in range(n_heads)], axis=0)
        sink = jnp.concatenate([jnp.broadcast_to(sink_ref[:, h:h + 1], (B, 1)) for h in range(n_heads)], axis=0)
        s = jnp.where(mask, s * scale, MASK_VALUE)
        m = jnp.maximum(jnp.max(s, axis=-1, keepdims=True), sink)
        e = jnp.exp(s - m)
        total = _dot(e, jnp.ones((2 * B, LANES), BF16), ((1,), (0,))) + jnp.exp(sink - m)
        lse = m + jnp.log(total[:, 0:1])
        inv = 1.0 / total
        probs = (e * jnp.concatenate([inv, inv], axis=1)).astype(BF16)
        for c in range(n_heads // 2):
            g = c // chunks_per_group
            out = (_dot(probs[2 * c * B:(2 * c + 1) * B], _band(kvc_ref, kvp_ref, 1, g, 0, W), ((1,), (0,)))
                   + _dot(probs[(2 * c + 1) * B:(2 * c + 2) * B], _band(kvc_ref, kvp_ref, 1, g, 1, W), ((1,), (0,))))
            o_ref[:, c * LANES:(c + 1) * LANES] = out.astype(o_ref.dtype)
        lse_ref[...] = jnp.concatenate([lse[h * B:(h + 1) * B] for h in range(n_heads)], axis=1)

    return _call(
        body, comm, name="attn_fwd", grid=(T // B,),
        in_specs=_attn_specs(HD, W) + [pl.BlockSpec((1, n_heads), lambda n: (0, 0))],
        out_specs=[pl.BlockSpec((B, HD), lambda n: (n, 0)), pl.BlockSpec((B, n_heads), lambda n: (n, 0))],
        out_shape=[_sds((T, HD), BF16), _sds((T, n_heads), F32)],
        sem=("parallel",), operands=[q, kvx, kvx, sinks])


def attn_bwd(q, kvx, do, lse, sinks, rope, n_heads, n_kv, comm=None):
    T, HD = q.shape
    KVD = n_kv * HEAD_DIM
    W = n_kv * LANES
    B = ATT_BLOCK
    chunks_per_group = n_heads // n_kv // 2
    scale = 1.0 / math.sqrt(HEAD_DIM)
    nb = T // B

    def body(q_ref, kvc_ref, kvp_ref, do_ref, lse_ref, sink_ref, c_ref, su_ref, sd_ref, dq_ref, dkv_ref, dsink_ref):
        n = pl.program_id(0)

        @pl.when(n == 0)
        def _():
            dkv_ref[...] = jnp.zeros_like(dkv_ref)
            dsink_ref[...] = jnp.zeros_like(dsink_ref)

        def chunk(ref, h):
            return ref[:, (h // 2) * LANES:(h // 2 + 1) * LANES]

        def band(part, h):
            return _band(kvc_ref, kvp_ref, part, h // 2 // chunks_per_group, h % 2, W)

        def stack(per_head):
            return jnp.concatenate([per_head(h) for h in range(n_heads)], axis=0)

        mask = jnp.tile(_band_mask(n), (n_heads, 1))
        s = stack(lambda h: _dot(chunk(q_ref, h), band(0, h), ((1,), (1,))))
        dp = stack(lambda h: _dot(chunk(do_ref, h), band(1, h), ((1,), (1,))))
        lse = stack(lambda h: lse_ref[:, h:h + 1])
        sink = stack(lambda h: jnp.broadcast_to(sink_ref[:, h:h + 1], (B, 1)))
        probs = jnp.exp(jnp.where(mask, s * scale, MASK_VALUE) - lse)
        delta = jnp.sum(probs * dp, axis=-1, keepdims=True)
        ds = (probs * (dp - delta) * scale).astype(BF16)
        probs = probs.astype(BF16)
        sink_term = jnp.exp(sink - lse) * delta
        dsk = [-jnp.sum(sink_term[h * B:(h + 1) * B], axis=0, keepdims=True) for h in range(n_heads)]

        dk_wide, dv_wide = [None] * n_kv, [None] * n_kv
        for c in range(n_heads // 2):
            g = c // chunks_per_group
            dq2 = None
            for h in (2 * c, 2 * c + 1):
                half = _half_mask(h % 2)
                q2, do2 = chunk(q_ref, h), chunk(do_ref, h)
                ds_h, p_h = ds[h * B:(h + 1) * B], probs[h * B:(h + 1) * B]
                part = _dot(ds_h, band(0, h), ((1,), (0,)))
                dq2 = part if dq2 is None else dq2 + part
                dk_h = _dot(ds_h, jnp.where(half, q2, jnp.zeros_like(q2)), ((0,), (0,)))
                dv_h = _dot(p_h, jnp.where(half, do2, jnp.zeros_like(do2)), ((0,), (0,)))
                dk_wide[g] = dk_h if dk_wide[g] is None else dk_wide[g] + dk_h
                dv_wide[g] = dv_h if dv_wide[g] is None else dv_wide[g] + dv_h
            dq_ref[:, c * LANES:(c + 1) * LANES] = _rope_chunk_bwd(
                dq2, c_ref[...], su_ref[...], sd_ref[...]).astype(dq_ref.dtype)

        def fold(wide):
            low = _half_mask(0)
            both = [w + pltpu.roll(w, HEAD_DIM, 1) for w in wide]
            return jnp.concatenate([jnp.where(low, both[2 * cg], both[2 * cg + 1]) for cg in range(n_kv // 2)], axis=1)

        dkv = jnp.concatenate([fold(dk_wide), fold(dv_wide)], axis=1)
        prev = pl.ds(pl.multiple_of(jnp.maximum(n - 1, 0) * B, B), B)
        cur = pl.ds(pl.multiple_of(n * B, B), B)
        dkv_ref[prev, :] += dkv[0:B, :]
        dkv_ref[cur, :] += dkv[B:, :]
        dsink_ref[...] += jnp.concatenate(dsk, axis=1)

    tab = pl.BlockSpec((B, LANES), lambda n: (n, 0))
    return _call(
        body, comm, name="attn_bwd", grid=(nb,),
        in_specs=_attn_specs(HD, W) + [pl.BlockSpec((B, HD), lambda n: (n, 0)),
                                       pl.BlockSpec((B, n_heads), lambda n: (n, 0)),
                                       pl.BlockSpec((1, n_heads), lambda n: (0, 0)), tab, tab, tab],
        out_specs=[pl.BlockSpec((B, HD), lambda n: (n, 0)), pl.BlockSpec((T, 2 * KVD), lambda n: (0, 0)),
                   pl.BlockSpec((1, n_heads), lambda n: (0, 0))],
        out_shape=[_sds((T, HD + 2 * KVD), BF16), _sds((T, 2 * KVD), F32), _sds((1, n_heads), F32)],
        sem=("arbitrary",), operands=[q, kvx, kvx, do, lse, sinks, *rope])


def dkv_finish(d_qkv, dkv, rope, HD, KVD, *, tm):
    T = dkv.shape[0]
    kv_col = HD // (2 * KVD)

    def body(alias_ref, dkv_ref, c_ref, su_ref, sd_ref, o_ref):
        del alias_ref
        dk = _rope_wide(dkv_ref[:, 0:KVD], c_ref[...], su_ref[...], sd_ref[...], _rope_chunk_bwd)
        o_ref[:, 0:KVD] = dk.astype(o_ref.dtype)
        o_ref[:, KVD:] = dkv_ref[:, KVD:].astype(o_ref.dtype)

    tab = pl.BlockSpec((tm, LANES), lambda i: (i, 0))
    return pl.pallas_call(
        body, name="dkv_finish", grid=(T // tm,),
        in_specs=[pl.BlockSpec(memory_space=pl.ANY), pl.BlockSpec((tm, 2 * KVD), lambda i: (i, 0)), tab, tab, tab],
        out_specs=pl.BlockSpec((tm, 2 * KVD), lambda i: (i, kv_col)),
        out_shape=_sds(d_qkv.shape, d_qkv.dtype), input_output_aliases={0: 0},
        compiler_params=_params(("parallel",)))(d_qkv, dkv, *rope)


def ple_bwd_elem(d_out, pp, gg, *, tm):
    T, D = d_out.shape

    def body(d_ref, pp_ref, gg_ref, dpp_ref, dgg_ref):
        d = d_ref[...]
        sg = _sigmoid(gg_ref[...].astype(F32))
        dpp_ref[...] = (d * sg).astype(dpp_ref.dtype)
        dgg_ref[...] = (d * pp_ref[...].astype(F32) * sg * (1.0 - sg)).astype(dgg_ref.dtype)

    tile = pl.BlockSpec((tm, D), lambda i: (i, 0))
    return pl.pallas_call(
        body, name="ple_bwd_elem", grid=(T // tm,), in_specs=[tile, tile, tile], out_specs=[tile, tile],
        out_shape=[_sds((T, D), BF16), _sds((T, D), BF16)], compiler_params=_params(("parallel",)))(d_out, pp, gg)


def chip_sum(name, g, p_sib, *, ta, part=0, parts=1):
    _, a, b = g.shape
    rows = a // parts
    assert a % parts == 0 and rows % ta == 0
    first = part * (rows // ta)

    def body(core_ref, g_ref, p_ref, o_ref):
        del core_ref
        o_ref[...] = (g_ref[...].astype(F32) + p_ref[...].astype(F32)).astype(o_ref.dtype)

    my_core = lax.axis_index("c").astype(jnp.int32).reshape(1)
    return pl.pallas_call(
        body, name=name, out_shape=_sds((N_CHIPS, rows, b), g.dtype),
        grid_spec=pltpu.PrefetchScalarGridSpec(
            num_scalar_prefetch=1, grid=(N_CHIPS, rows // ta),
            in_specs=[pl.BlockSpec((None, None, ta, b), lambda q, i, core: (q, core[0], first + i, 0)),
                      pl.BlockSpec((None, ta, b), lambda q, i, core: (q, first + i, 0))],
            out_specs=pl.BlockSpec((None, ta, b), lambda q, i, core: (q, i, 0))),
        compiler_params=_params(("arbitrary", "arbitrary")))(my_core, g.reshape(N_CHIPS, 2, a, b), p_sib)


def adamw(name, recvs, w, m, v, *, ta):
    L, a, b = w.shape
    n_terms = recvs[0].shape[0]
    assert a % ta == 0 and len(recvs) == L
    c1 = 1.0 - ADAM_B1 ** ADAM_STEP
    c2 = 1.0 - ADAM_B2 ** ADAM_STEP

    def body(*refs):
        r_refs = refs[:L]
        w_ref, m_ref, v_ref, g_ref, d_ref, nm_ref, nv_ref = refs[L:]
        layer = pl.program_id(0)
        for l in range(L):
            @pl.when(layer == l)
            def _(r_ref=r_refs[l]):
                g = r_ref[0].astype(F32)
                for s in range(1, n_terms):
                    g = g + r_ref[s].astype(F32)
                nm = ADAM_B1 * m_ref[...] + (1.0 - ADAM_B1) * g
                nv = ADAM_B2 * v_ref[...] + (1.0 - ADAM_B2) * jnp.square(g)
                m_hat = nm / c1
                v_hat = nv / c2
                g_ref[...] = g
                d_ref[...] = -ADAM_LR * (m_hat / (jnp.sqrt(v_hat) + ADAM_EPS) + ADAM_WD * w_ref[...])
                nm_ref[...] = nm
                nv_ref[...] = nv

    blk = pl.BlockSpec((None, ta, b), lambda l, i: (l, i, 0))
    out = _sds((L, a, b), F32)
    r_specs = [pl.BlockSpec((n_terms, ta, b), lambda l, i, ll=ll: (0, jnp.where(l == ll, i, 0), 0)) for ll in range(L)]
    return pl.pallas_call(
        body, name=name, grid=(L, a // ta), in_specs=r_specs + [blk, blk, blk],
        out_specs=[blk, blk, blk, blk], out_shape=[out, out, out, out],
        compiler_params=_params(("arbitrary", "arbitrary")))(*recvs, w, m, v)


def _pack_rows(parts):
    out = []
    for arr, rows in parts:
        arr = arr.reshape(-1, LANES).astype(F32)
        out.append(jnp.pad(arr, ((0, rows - arr.shape[0]), (0, 0))))
    return jnp.concatenate(out, axis=0)


def _small_rows(a, prefix):
    return _pack_rows([(a[prefix + name], rows) for name, rows in SMALL_SHARDED])


def _unpack_small(packed, a):
    out, r0 = {}, 0
    for name, rows in SMALL_SHARDED:
        shape = a[name].shape
        used = math.prod(shape) // LANES
        out[name] = packed[r0:r0 + used].reshape(shape)
        r0 += rows
    return out


def _rep_rows(a, prefix):
    parts = []
    for name in REPLICATED:
        arr = a[prefix + name]
        if arr.size % LANES:
            arr = jnp.pad(arr.reshape(1, -1), ((0, 0), (0, LANES - arr.size % LANES)))
        rows = -(-arr.size // LANES)
        parts.append((arr, -(-rows // SUBLANES) * SUBLANES))
    return _pack_rows(parts)


def _unpack_rep(packed, a):
    out, r0 = {}, 0
    for name in REPLICATED:
        shape = a[name].shape
        size = math.prod(shape)
        rows = -(-size // LANES)
        out[name] = packed[r0:r0 + rows].reshape(-1)[:size].reshape(shape)
        r0 += -(-rows // SUBLANES) * SUBLANES
    return out


def _step(a):
    x = a['x'][0]
    T, D = x.shape
    tgt = a['loss_target'][0]
    p_in = [a['p'][i, 0] for i in range(DEPTH)]
    PLE = p_in[0].shape[1]
    n_heads = a['attn_sinks'].shape[1]
    HD = n_heads * HEAD_DIM
    KVD = a['kv_w_k'].shape[1]
    n_kv = KVD // HEAD_DIM
    F = a['mlp_w_down'].shape[1] * N_DEV
    tm = min(TOKEN_TILE, T)
    tm2 = min(2 * TOKEN_TILE, T)
    tmc = min(TOKEN_TILE, T)
    tw = 512
    alpha = DEEPNORM_ALPHA
    xb = x.astype(BF16)
    p_b = [p.astype(BF16) for p in p_in]

    def shard3(w):
        return w.reshape((1,) + w.shape) if w.ndim == 2 else w

    def gather(*specs):
        items = []
        for spec in specs:
            w = shard3(a[spec[0]])[spec[1]]
            if len(spec) == 3:
                rows = w.shape[0] // 2
                w = w[spec[2] * rows:(spec[2] + 1) * rows]
            items.append(w.astype(BF16))
        return _GatherJob(items)

    def join_halves(lo, hi):
        return jnp.concatenate([lo, hi], axis=1)

    (W_in, small_full), = comm_only("gather_first",
                                    [_GatherJob([a['conv_w_in'][0].astype(BF16), _small_rows(a, '')])])
    r0, small = 0, {}
    for name, rows in SMALL_SHARDED:
        small[name] = small_full[:, r0:r0 + rows]
        r0 += rows
    b_in = small['conv_b_in'][:, 0:2 * D // N_DEV // LANES].reshape(1, 2 * D)
    w_dw = jnp.transpose(small['conv_w_dw'], (1, 0, 2)).reshape(CONV_HALO, D)
    b_dw, cln_g, cln_b, b_out = (small[nm][:, 0].reshape(1, D) for nm in
                                 ('conv_b_dw', 'conv_ln_g', 'conv_ln_b', 'conv_b_out'))
    W_up, W_down, W_proj, W_gate = {}, {}, {}, {}
    mix_g, mix_b, mlp_g, mlp_b = a['mix_ln_g'], a['mix_ln_b'], a['mlp_ln_g'], a['mlp_ln_b']
    rope = _rope_tables(T)

    def set_ple_weights(li, g_proj, g_gate):
        W_proj[li] = jnp.transpose(g_proj, (1, 0, 2)).reshape(1, PLE, D)
        W_gate[li] = g_gate.reshape(1, D, D)

    def row(v, i):
        return v[i:i + 1]

    def res_ln_epi(coef):
        def epi(accs, ex, out, i):
            acc = accs[0] if isinstance(accs, list) else accs
            n_ex = len(ex)
            res_ref, g_ref, b_ref = ex[n_ex - 3], ex[n_ex - 2], ex[n_ex - 1]
            z = coef * res_ref[...] + acc
            if n_ex == 4:
                z = z + ex[0][...]
            out[0][...] = z
            xo = _ln_fwd(z, g_ref[...], b_ref[...])
            out[1][...] = xo
            out[2][...] = xo.astype(BF16)
        return epi

    res_ln_outs = [(_sds((T, D), F32), 'tile'), (_sds((T, D), F32), 'tile'), (_sds((T, D), BF16), 'tile')]

    def mlp_fwd(li, xin, xin_b, down_comm=None):
        def up_epi(accs, ex, out, i):
            u = accs[0]
            out[0][...] = u.astype(BF16)
            out[1][...] = jnp.square(jnp.maximum(u, 0.0)).astype(BF16)
        (u, act), got_up = mm_nn(f"mlp_up_{li}", [(xin_b, W_up[li], 0)], [], [(_sds((T, F), BF16), 'tile')] * 2,
                                 up_epi, tm=tm2, tn=min(1024, F), comm=gather(('mlp_w_down', li)))
        W_down[li] = got_up[0].reshape(1, F, D)
        res = mm_nn(f"mlp_down_{li}", [(act, W_down[li], 0)],
                    [(xin, 'tile'), (row(mlp_g, li), 'row'), (row(mlp_b, li), 'row')],
                    res_ln_outs, res_ln_epi(alpha), tm=tm, tn=D, tk=F, comm=down_comm)
        (z, xo, xo_b), got_down = res if down_comm is not None else (res, ())
        return u, act, z, xo, xo_b, got_down

    def ple_fwd(li, xin, xin_b, with_loss, comm=None):
        def epi(accs, ex, out, i):
            pp, gg = accs
            xo = ex[0][...] + pp * _sigmoid(gg)
            out[1][...] = pp.astype(BF16)
            out[2][...] = gg.astype(BF16)
            if with_loss:
                err = xo - ex[1][...]
                out[0][...] = err * (1.0 / D)
                _init_or_add(out[3], i, jnp.sum(err * err, axis=0, keepdims=True) * (0.5 / D))
            else:
                out[0][...] = xo
                out[3][...] = xo.astype(BF16)
        extras = [(xin, 'tile')] + ([(tgt, 'tile')] if with_loss else [])
        outs = [(_sds((T, D), F32), 'tile'), (_sds((T, D), BF16), 'tile'), (_sds((T, D), BF16), 'tile')]
        outs.append((_sds((1, D), F32), 'rowacc') if with_loss else (_sds((T, D), BF16), 'tile'))
        return mm_nn(f"ple_{li}", [(p_b[li], W_proj[li], 0), (xin_b, W_gate[li], 0)], extras, outs, epi, tm=tm, tn=D,
                     comm=comm)

    assert D // N_DEV == LANES
    (g0, ha0, hg0), got = _glu(xb, W_in, b_in, T, D, tm,
                               gather(('conv_w_out', 0), ('ple_w_proj', 0), ('ple_w_gate', 0)))
    W_out = got[0].reshape(1, D, D)
    set_ple_weights(0, got[1], got[2])
    (c0, s0), (W_up[0],) = dwconv_fwd(g0, w_dw, b_dw, cln_g, cln_b, tm=tmc, comm=gather(('mlp_w_up', 0)))
    z1, x1, x1b = mm_nn("conv_out", [(s0, W_out, 0)],
                        [(b_out, 'row'), (x, 'tile'), (row(mix_g, 0), 'row'), (row(mix_b, 0), 'row')],
                        res_ln_outs, res_ln_epi(alpha), tm=tm, tn=D)
    u0, act0, z2, x2, x2b, got = mlp_fwd(
        0, x1, x1b, down_comm=gather(('attn_w_q', 0), ('kv_w_k', 0), ('kv_w_v', 0), ('attn_w_o', 0),
                                     ('ple_w_proj', 1), ('ple_w_gate', 1)))
    W_qkv = jnp.concatenate([got[0].reshape(D, HD), got[1].reshape(D, KVD), got[2].reshape(D, KVD)], axis=1)[None]
    W_o = got[3].reshape(1, HD, D)
    set_ple_weights(1, got[4], got[5])
    x3, pp0, gg0, x3b = ple_fwd(0, x2, x2b, False)

    def qkv_epi(accs, ex, out, i):
        t = accs[0]
        c, su, sd = ex[0][...], ex[1][...], ex[2][...]
        out[0][...] = _rope_wide(t[:, 0:HD], c, su, sd, _rope_chunk).astype(BF16)
        widen_kv(_rope_wide(t[:, HD:HD + KVD], c, su, sd, _rope_chunk), t[:, HD + KVD:], out[1], n_kv)
    NQ = HD + 2 * KVD
    q1, kvx1 = mm_nn("qkv_rope", [(x3b, W_qkv, 0)], [(t, 'tab') for t in rope],
                     [(_sds((T, HD), BF16), 'rows'), (_sds((T, 4 * n_kv * LANES), BF16), 'rows')], qkv_epi,
                     tm=tm, tn=NQ)
    (o1, lse1), (W_up[1],) = attn_fwd(q1, kvx1, a['attn_sinks'], n_heads, n_kv, comm=gather(('mlp_w_up', 1)))
    z3, x4, x4b = mm_nn("attn_out", [(o1, W_o, 0)], [(x3, 'tile'), (row(mix_g, 1), 'row'), (row(mix_b, 1), 'row')],
                        res_ln_outs, res_ln_epi(alpha), tm=tm, tn=D)
    u1, act1, z4, x5, x5b, _ = mlp_fwd(1, x4, x4b)
    dy, pp1, gg1, loss_row = ple_fwd(1, x5, x5b, True)
    loss_local = jnp.sum(loss_row)

    grads = {}

    def ln_bwd_epi(coef, with_colsum):
        def epi(acc, ex, out, i):
            d_x = acc + coef * ex[0][...]
            dz, dg, db = _ln_bwd(d_x, ex[1][...], ex[2][...])
            out[0][...] = dz
            out[1][...] = dz.astype(BF16)
            _init_or_add(out[2], i, dg)
            _init_or_add(out[3], i, db)
            if with_colsum:
                _init_or_add(out[4], i, jnp.sum(dz, axis=0, keepdims=True))
        return epi

    def ln_bwd_outs(with_colsum):
        outs = [(_sds((T, D), F32), 'tile'), (_sds((T, D), BF16), 'tile'), (_sds((1, D), F32), 'rowacc'),
                (_sds((1, D), F32), 'rowacc')]
        return outs + ([(_sds((1, D), F32), 'rowacc')] if with_colsum else [])

    def ple_bwd(li, d_out, xin, pp, gg, z_mlp, pair_specs=None):
        side = {}

        def gate_grads(ex):
            d = ex[0][...]
            sg = _sigmoid(ex[4][...].astype(F32))
            side['d_pp'] = (d * sg).astype(BF16)
            side['d_gg'] = (d * ex[3][...].astype(F32) * sg * (1.0 - sg)).astype(BF16)
            return side['d_gg']

        job = pair_stage(*pair_specs) if pair_specs else None
        res = mm_nt(f"ple_dx_{li}", _sds((T, D), BF16), W_gate[li],
                    [(d_out, 'tile'), (z_mlp, 'tile'), (row(mlp_g, li), 'row'), (pp, 'tile'), (gg, 'tile')],
                    ln_bwd_outs(False), ln_bwd_epi(1.0, False), tm=tm, tko=D, tc=D, comm=job, a_pro=gate_grads,
                    dws=[(xin, lambda: side['d_gg']), (p_b[li], lambda: side['d_pp'])])
        (dz, dzb, dg, db, dw_gate, dw_proj), got = res if job is not None else (res, None)
        grads[('mlp_ln_g', li)], grads[('mlp_ln_b', li)] = dg, db
        grads[('ple_w_gate', li)], grads[('ple_w_proj', li)] = dw_gate, dw_proj
        if job is not None:
            pair_done(job, got)
        return dz, dzb

    recv = {}
    wqkv_cols = {'attn_w_q': (0, HD), 'kv_w_k': (HD, HD + KVD), 'kv_w_v': (HD + KVD, NQ)}

    def piece(name, li):
        if name == 'conv_w_in':
            return grads['conv_w_in']
        if name == 'mlp_w_up':
            return grads[('mlp_w_up', li)]
        if name == 'ple_w_proj':
            return jnp.transpose(grads[('ple_w_proj', li)][0].reshape(PLE, N_DEV, D // N_DEV), (1, 0, 2))
        if name in wqkv_cols:
            g = grads['w_qkv'][:, wqkv_cols[name][0]:wqkv_cols[name][1]]
        else:
            g = grads[name] if name in grads else grads[(name, li)]
            g = g[0]
        return g.reshape(N_DEV, g.shape[0] // N_DEV, g.shape[1])

    def pair_stage(*specs):
        job = _PairJob([piece(nm, li) for nm, li in specs])
        job.specs = specs
        return job

    sums = {}
    halved = ()

    def pair_done(pair_job, got):
        for (nm, li), mine, theirs in zip(pair_job.specs, pair_job.sources, got):
            if nm in halved:
                for h in (0, 1):
                    sums[(nm, li, h)] = chip_sum(f"chip_sum_{nm}_{li}_{h}", mine, theirs,
                                                 ta=min(256, mine.shape[1] // 2), part=h, parts=2)
            else:
                sums[(nm, li)] = chip_sum(f"chip_sum_{nm}_{li}", mine, theirs, ta=min(512, mine.shape[1]))

    def chip_stage(*keys):
        job = _ChipJob([sums[k] for k in keys])
        job.specs = keys
        return job

    def hosted(res, job):
        main, got = res
        for spec, r in zip(job.specs, got):
            recv[spec] = r
        return main

    def mlp_bwd(li, dz, dzb, xin, u, act, z_mix, with_colsum, pair_specs, down_keys=None):
        if down_keys:
            job = chip_stage(*down_keys)
            grads[('mlp_w_down', li)] = hosted(
                mm_tn(f"d_mlp_down_{li}", act, dzb, D, BF16, tm=T, tk=tw, tn=tw, comm=job), job)
        else:
            grads[('mlp_w_down', li)] = mm_tn(f"d_mlp_down_{li}", act, dzb, D, BF16, tm=T, tk=tw, tn=tw)

        pair = pair_stage(*pair_specs)
        (du, grads[('mlp_w_up', li)]), got = mlp_du_dwup(f"mlp_du_{li}", dzb, W_down[li], u, xin, F // N_DEV,
                                                        tm=tm2, tf=min(1024, F), comm=pair)
        pair_done(pair, got)
        job = chip_stage(('mlp_w_down', li))
        res = hosted(mm_nt(f"mlp_dx_{li}", du, W_up[li], [(dz, 'tile'), (z_mix, 'tile'), (row(mix_g, li), 'row')],
                           ln_bwd_outs(with_colsum), ln_bwd_epi(alpha, with_colsum), tm=tm, tko=D, tc=F, comm=job), job)
        grads[('mix_ln_g', li)], grads[('mix_ln_b', li)] = res[2], res[3]
        return res

    dz4, dz4b = ple_bwd(1, dy, x5b, pp1, gg1, z4)
    dz3, dz3b, _, _ = mlp_bwd(1, dz4, dz4b, x4b, u1, act1, z3, False,
                              [('mlp_w_down', 1), ('ple_w_gate', 1), ('ple_w_proj', 1)])
    def split_hosted(res, jobs, chips, pairs):
        main, got = res
        parts = jobs.split(got)
        for k in chips:
            hosted((None, parts[k]), jobs.jobs[k])
        for k in pairs:
            pair_done(jobs.jobs[k], parts[k])
        return main

    def do_epi(acc, ex, out, i):
        out[0][...] = acc.astype(BF16)
    jobs = _Jobs([pair_stage(('mlp_w_up', 1)), chip_stage(('ple_w_gate', 1), ('ple_w_proj', 1))])
    do1, grads['attn_w_o'] = split_hosted(
        mm_nt("attn_do", dz3b, W_o, [], [(_sds((T, HD), BF16), 'tile')], do_epi, tm=tm, tko=HD, tc=D, comm=jobs,
              dws=[(o1, None)]), jobs, chips=[1], pairs=[0])
    jobs = _Jobs([chip_stage(('mlp_w_up', 1)), pair_stage(('attn_w_o', 0))])
    d_qkv, dkv, d_sinks = split_hosted(
        attn_bwd(q1, kvx1, do1, lse1, a['attn_sinks'], rope, n_heads, n_kv, comm=jobs), jobs, chips=[0], pairs=[1])
    d_qkv = dkv_finish(d_qkv, dkv, rope, HD, KVD, tm=tm)

    def dx3_epi(acc, ex, out, i):
        out[0][...] = acc + alpha * ex[0][...]
    dx3, dw_qkv = mm_nt("attn_dx", d_qkv, W_qkv, [(dz3, 'tile')], [(_sds((T, D), F32), 'tile')], dx3_epi,
                        tm=tm, tko=D, tc=NQ, dws=[(x3b, None)])
    grads['w_qkv'] = dw_qkv[0]

    dz2, dz2b = ple_bwd(0, dx3, x2b, pp0, gg0, z2, [('attn_w_q', 0), ('kv_w_k', 0), ('kv_w_v', 0)])
    dz1, dz1b, _, _, db_out = mlp_bwd(
        0, dz2, dz2b, x1b, u0, act0, z1, True, [('mlp_w_down', 0), ('ple_w_gate', 0), ('ple_w_proj', 0)],
        down_keys=[('attn_w_q', 0), ('kv_w_k', 0), ('kv_w_v', 0), ('attn_w_o', 0)])

    def ds_epi(acc, ex, out, i):
        n = _ln_fwd(ex[0][...], ex[1][...], ex[2][...])
        sg = _sigmoid(n)
        dn = acc * (sg * (1.0 + n * (1.0 - sg)))
        dc, dg, db = _ln_bwd(dn, ex[0][...], ex[1][...])
        out[0][...] = dc
        _init_or_add(out[1], i, dg)
        _init_or_add(out[2], i, db)
    jobs = _Jobs([pair_stage(('mlp_w_up', 0)), chip_stage(('ple_w_gate', 0), ('ple_w_proj', 0))])
    dc0, d_cln_g, d_cln_b, grads['conv_w_out'] = split_hosted(
        mm_nt("conv_ds", dz1b, W_out, [(c0, 'tile'), (cln_g, 'row'), (cln_b, 'row')],
              [(_sds((T, D), F32), 'tile'), (_sds((1, D), F32), 'rowacc'), (_sds((1, D), F32), 'rowacc')], ds_epi,
              tm=tm, tko=D, tc=D, comm=jobs, dws=[(s0, None)]), jobs, chips=[1], pairs=[0])
    job = chip_stage(('mlp_w_up', 0))
    dh0, d_wdw, d_bdw, d_bin = hosted(dwconv_bwd(dc0, g0, ha0, hg0, w_dw, tm=tmc, comm=job), job)
    grads['conv_w_in'] = mm_tn("d_conv_win", xb, dh0, 2 * D // N_DEV, BF16, tm=T, tk=tw, tn=tw)

    pair = pair_stage(('conv_w_in', 0), ('conv_w_out', 0))
    pair_done(pair, comm_only("pair_last", [pair])[0])
    last_chip = chip_stage(('conv_w_in', 0), ('conv_w_out', 0))

    def own_rows(vec, rows_used, rows):
        arr = vec.reshape(N_DEV, rows_used, LANES)
        return jnp.pad(arr, ((0, 0), (0, rows - rows_used), (0, 0)))
    dwdw_dev = jnp.transpose(d_wdw.reshape(CONV_HALO, N_DEV, D // N_DEV), (1, 0, 2))
    lane_rows = D // N_DEV // LANES
    small_grad = jnp.concatenate([
        own_rows(d_bin, 2 * lane_rows, 8), dwdw_dev if lane_rows == 1 else dwdw_dev.reshape(N_DEV, -1, LANES),
        own_rows(d_bdw, lane_rows, 8), own_rows(d_cln_g, lane_rows, 8), own_rows(d_cln_b, lane_rows, 8),
        own_rows(db_out, lane_rows, 8)], axis=1)
    n_small = small_grad.shape[1]

    rep_local = {'mix_ln_g': jnp.concatenate([grads[('mix_ln_g', li)] for li in range(DEPTH)], axis=0),
                 'mix_ln_b': jnp.concatenate([grads[('mix_ln_b', li)] for li in range(DEPTH)], axis=0),
                 'mlp_ln_g': jnp.concatenate([grads[('mlp_ln_g', li)] for li in range(DEPTH)], axis=0),
                 'mlp_ln_b': jnp.concatenate([grads[('mlp_ln_b', li)] for li in range(DEPTH)], axis=0),
                 'attn_sinks': d_sinks}
    rep_grad = _rep_rows(rep_local, '')
    n_rep = rep_grad.shape[0]
    last = _Jobs([last_chip, _DirectJob([small_grad, jnp.broadcast_to(rep_grad[None], (N_DEV, n_rep, LANES))])])

    def dx_epi(acc, ex, out, i):
        out[0][...] = acc + alpha * ex[0][...]
    (grad_x,), got = mm_nt("conv_dx", dh0, W_in, [(dz1, 'tile')], [(_sds((T, D), F32), 'tile')], dx_epi,
                           tm=tm, tko=D, tc=D, comm=last)
    got, (recv_small, recv_rep) = last.split(got)
    hosted((None, got), last_chip)

    result = {}
    kinds = ('grad', 'delta', 'new_m', 'new_v')
    w, m, v = (_small_rows(a, pre)[None] for pre in ('', 'm_', 'v_'))
    for kind, arr in zip(kinds, adamw("adamw_small", [recv_small], w, m, v, ta=n_small)):
        for pname, val in _unpack_small(arr[0], a).items():
            result[(kind, pname)] = val
    w, m, v = (_rep_rows(a, pre)[None] for pre in ('', 'm_', 'v_'))
    for kind, arr in zip(kinds, adamw("adamw_rep", [recv_rep], w, m, v, ta=n_rep)):
        for pname, val in _unpack_rep(arr[0], a).items():
            result[(kind, pname)] = val
    for name in BIG_WEIGHTS:
        w, m, v = (shard3(a[pre + name]) for pre in ('', 'm_', 'v_'))
        recvs = [join_halves(recv[(name, li, 0)], recv[(name, li, 1)]) if name in halved else recv[(name, li)]
                 for li in range(w.shape[0])]
        for kind, arr in zip(kinds, adamw("adamw_" + name, recvs, w, m, v, ta=min(256, w.shape[1]))):
            result[(kind, name)] = arr.reshape(a[name].shape)

    loss = lax.psum(loss_local, ("x", "y", "c"))
    out = [loss, grad_x[None]]
    for kind in ('grad', 'delta', 'new_m', 'new_v'):
        out += [result[(kind, name)] for name in WEIGHT_NAMES]
    return tuple(out)


def _glu(x, W_in, b_in, T, D, tm, comm=None):
    n = W_in.shape[2]
    q = 2 if D // n % 2 == 0 else 1
    nt = D // (q * n)
    tn = q * n

    def body(x_ref, wa_ref, wg_ref, ba_ref, bg_ref, g_ref, ha_ref, hg_ref):
        xb = x_ref[...]
        ha = jnp.concatenate([_dot(xb, wa_ref[s], ((1,), (0,))) for s in range(q)], axis=1) + ba_ref[...]
        hg = jnp.concatenate([_dot(xb, wg_ref[s], ((1,), (0,))) for s in range(q)], axis=1) + bg_ref[...]
        g_ref[...] = ha * _sigmoid(hg)
        ha_ref[...] = ha.astype(ha_ref.dtype)
        hg_ref[...] = hg.astype(hg_ref.dtype)

    return _call(
        body, comm, name="conv_in_glu", grid=(T // tm, nt),
        in_specs=[pl.BlockSpec((tm, D), lambda i, j: (i, 0)),
                  pl.BlockSpec((q, D, n), lambda i, j: (j, 0, 0)),
                  pl.BlockSpec((q, D, n), lambda i, j: (j + nt, 0, 0)),
                  pl.BlockSpec((1, tn), lambda i, j: (0, j)), pl.BlockSpec((1, tn), lambda i, j: (0, j + nt))],
        out_specs=[pl.BlockSpec((tm, tn), lambda i, j: (i, j))] * 3,
        out_shape=[_sds((T, D), F32), _sds((T, D), BF16), _sds((T, D), BF16)],
        sem=("parallel", "parallel"), operands=[x, W_in, W_in, b_in, b_in])


def kernel(x, p, conv_w_in, conv_b_in, conv_w_dw, conv_b_dw, conv_ln_g, conv_ln_b, conv_w_out, conv_b_out, kv_w_k, kv_w_v, attn_w_q, attn_sinks, attn_w_o, mix_ln_g, mix_ln_b, mlp_w_up, mlp_w_down, mlp_ln_g, mlp_ln_b, ple_w_proj, ple_w_gate, loss_target, m_conv_w_in, m_conv_b_in, m_conv_w_dw, m_conv_b_dw, m_conv_ln_g, m_conv_ln_b, m_conv_w_out, m_conv_b_out, m_kv_w_k, m_kv_w_v, m_attn_w_q, m_attn_sinks, m_attn_w_o, m_mix_ln_g, m_mix_ln_b, m_mlp_w_up, m_mlp_w_down, m_mlp_ln_g, m_mlp_ln_b, m_ple_w_proj, m_ple_w_gate, v_conv_w_in, v_conv_b_in, v_conv_w_dw, v_conv_b_dw, v_conv_ln_g, v_conv_ln_b, v_conv_w_out, v_conv_b_out, v_kv_w_k, v_kv_w_v, v_attn_w_q, v_attn_sinks, v_attn_w_o, v_mix_ln_g, v_mix_ln_b, v_mlp_w_up, v_mlp_w_down, v_mlp_ln_g, v_mlp_ln_b, v_ple_w_proj, v_ple_w_gate):
    return _step(dict(locals()))
```

```python
import functools
import math

import jax
import jax.numpy as jnp
from jax import lax
from jax.experimental import pallas as pl
from jax.experimental.pallas import tpu as pltpu

F32 = jnp.float32
BF16 = jnp.bfloat16

N_DEV = 8
HEAD_DIM = 64
ROPE_DIM = HEAD_DIM // 4
ROPE_HALF = ROPE_DIM // 2
ROPE_THETA = 500000.0
ATT_BLOCK = 128
CONV_WIDTH = 31
CONV_HALO = 32
CONV_ROWS = 128
LN_EPS = 1e-5
DEPTH = 2
DEEPNORM_ALPHA = (2 * DEPTH) ** 0.25
MASK_VALUE = -1e30

ADAM_LR = 0.001
ADAM_B1 = 0.9
ADAM_B2 = 0.999
ADAM_EPS = 1e-08
ADAM_WD = 0.01
ADAM_STEP = 10

LANES = 128
SUBLANES = 8
VMEM_LIMIT_BYTES = 52 * 1024 * 1024
TOKEN_TILE = 512
MESH_ID = pl.DeviceIdType.MESH
RELAY_AT_TENTHS = 5
FORWARD_AT_TENTHS = 8

WEIGHT_NAMES = ['conv_w_in', 'conv_b_in', 'conv_w_dw', 'conv_b_dw', 'conv_ln_g', 'conv_ln_b', 'conv_w_out',
                'conv_b_out', 'kv_w_k', 'kv_w_v', 'attn_w_q', 'attn_sinks', 'attn_w_o', 'mix_ln_g', 'mix_ln_b',
                'mlp_w_up', 'mlp_w_down', 'mlp_ln_g', 'mlp_ln_b', 'ple_w_proj', 'ple_w_gate']
BIG_WEIGHTS = ['conv_w_in', 'conv_w_out', 'kv_w_k', 'kv_w_v', 'attn_w_q', 'attn_w_o', 'mlp_w_up', 'mlp_w_down',
               'ple_w_proj', 'ple_w_gate']
SMALL_SHARDED = [('conv_b_in', 8), ('conv_w_dw', 32), ('conv_b_dw', 8), ('conv_ln_g', 8), ('conv_ln_b', 8),
                 ('conv_b_out', 8)]
REPLICATED = ['mix_ln_g', 'mix_ln_b', 'mlp_ln_g', 'mlp_ln_b', 'attn_sinks']


def _params(sem):
    return pltpu.CompilerParams(dimension_semantics=sem, vmem_limit_bytes=VMEM_LIMIT_BYTES)


def _sds(shape, dtype):
    return jax.ShapeDtypeStruct(shape, dtype)


def _my_place():
    x, y, c = lax.axis_index("x"), lax.axis_index("y"), lax.axis_index("c")
    return x, y, c, 4 * x + 2 * y + c


def _peers(x, y, c):
    out = []
    for dx in (0, 1):
        for dy in (0, 1):
            for dc in (0, 1):
                if dx or dy or dc:
                    px, py, pc = x ^ dx, y ^ dy, c ^ dc
                    out.append(((px, py, pc), 4 * px + 2 * py + pc))
    return out


N_CHIPS = 4


def _other_chips(x, y):
    return [((x ^ dx, y ^ dy), 2 * (x ^ dx) + (y ^ dy)) for dx, dy in ((1, 0), (0, 1), (1, 1))]


def _remote(src, dst, send, recv, to):
    return pltpu.make_async_remote_copy(src_ref=src, dst_ref=dst, send_sem=send, recv_sem=recv, device_id=to,
                                        device_id_type=MESH_ID)


def _wait_slabs(buf, count, send, recv, me, sent=True, received=True):
    part = buf.at[pl.ds(0, count)]
    cp = _remote(part, part, send, recv, me)
    if sent:
        cp.wait_send()
    if received:
        cp.wait_recv()


class _DirectJob:
    n_sems = 3

    def __init__(self, items):
        self.sources = list(items)
        self.dests = [_sds(it.shape, it.dtype) for it in items]
        self.n = len(items)

    def start(self, src, dst, send, recv, loc):
        x, y, c, me = _my_place()
        for k in range(self.n):
            here = dst[k].at[pl.ds(me, 1)]
            pltpu.make_async_copy(src[k].at[pl.ds(me, 1)], here, loc.at[k]).start()
            for peer, idx in _peers(x, y, c):
                _remote(src[k].at[pl.ds(idx, 1)], here, send.at[k], recv.at[k], peer).start()

    def relay(self, *refs):
        pass

    def forward(self, *refs):
        pass

    def finish(self, src, dst, send, recv, loc):
        x, y, c, me = _my_place()
        for k in range(self.n):
            pltpu.make_async_copy(src[k].at[pl.ds(me, 1)], dst[k].at[pl.ds(me, 1)], loc.at[k]).wait()
            _wait_slabs(dst[k], N_DEV - 1, send.at[k], recv.at[k], (x, y, c))


class _GatherJob:
    n_sems = 7

    def __init__(self, items):
        self.sources = [it.reshape((1,) + it.shape) for it in items]
        self.dests = [_sds((N_DEV,) + it.shape, it.dtype) for it in items]
        self.n = len(items)

    @staticmethod
    def _neighbours(x, y, c):
        out = []
        for dx, dy in ((c, 1 - c), (1 - c, c), (1, 1)):
            px, py = x ^ dx, y ^ dy
            out.append(((px, py, c), 4 * px + 2 * py + c))
        return out

    def start(self, src, dst, send_a, recv_a, send_b, recv_b, send_d, recv_d, loc):
        x, y, c, me = _my_place()
        (first, _), (second, _), _ = self._neighbours(x, y, c)
        for k in range(self.n):
            here = dst[k].at[pl.ds(me, 1)]
            pltpu.make_async_copy(src[k], here, loc.at[k]).start()
            _remote(src[k], here, send_d.at[k], recv_d.at[k], (x, y, 1 - c)).start()
            _remote(src[k], here, send_a.at[k], recv_a.at[k], first).start()
            _remote(src[k], here, send_b.at[k], recv_b.at[k], second).start()

    def relay(self, src, dst, send_a, recv_a, send_b, recv_b, send_d, recv_d, loc):
        x, y, c, me = _my_place()
        (_, first_idx), (second, _), _ = self._neighbours(x, y, c)
        for k in range(self.n):
            _wait_slabs(dst[k], 1, send_a.at[k], recv_a.at[k], (x, y, c), sent=False)
            rows = dst[k].at[pl.ds(first_idx, 1)]
            _remote(rows, rows, send_b.at[k], recv_b.at[k], second).start()

    def forward(self, src, dst, send_a, recv_a, send_b, recv_b, send_d, recv_d, loc):
        x, y, c, me = _my_place()
        for k in range(self.n):
            _wait_slabs(dst[k], 2, send_b.at[k], recv_b.at[k], (x, y, c), sent=False)
            for _, idx in self._neighbours(x, y, c):
                rows = dst[k].at[pl.ds(idx, 1)]
                _remote(rows, rows, send_d.at[k], recv_d.at[k], (x, y, 1 - c)).start()

    def finish(self, src, dst, send_a, recv_a, send_b, recv_b, send_d, recv_d, loc):
        x, y, c, me = _my_place()
        for k in range(self.n):
            pltpu.make_async_copy(src[k], dst[k].at[pl.ds(me, 1)], loc.at[k]).wait()
            _wait_slabs(dst[k], 1, send_a.at[k], recv_a.at[k], (x, y, c), received=False)
            _wait_slabs(dst[k], 2, send_b.at[k], recv_b.at[k], (x, y, c), received=False)
            _wait_slabs(dst[k], N_CHIPS, send_d.at[k], recv_d.at[k], (x, y, c))


class _PairJob:
    n_sems = 2

    def __init__(self, items):
        self.sources = list(items)
        self.dests = [_sds((N_CHIPS,) + it.shape[1:], it.dtype) for it in items]
        self.n = len(items)

    def start(self, src, dst, send, recv):
        x, y, c, me = _my_place()
        for k in range(self.n):
            for q in range(N_CHIPS):
                _remote(src[k].at[pl.ds(2 * q + 1 - c, 1)], dst[k].at[pl.ds(q, 1)], send.at[k], recv.at[k],
                        (x, y, 1 - c)).start()

    def relay(self, *refs):
        pass

    def forward(self, *refs):
        pass

    def finish(self, src, dst, send, recv):
        x, y, c, me = _my_place()
        for k in range(self.n):
            _wait_slabs(dst[k], N_CHIPS, send.at[k], recv.at[k], (x, y, c))


class _ChipJob:
    n_sems = 3

    def __init__(self, items):
        self.sources = list(items)
        self.dests = [_sds(it.shape, it.dtype) for it in items]
        self.n = len(items)

    def start(self, src, dst, send, recv, loc):
        x, y, c, me = _my_place()
        mine = 2 * x + y
        for k in range(self.n):
            here = dst[k].at[pl.ds(mine, 1)]
            pltpu.make_async_copy(src[k].at[pl.ds(mine, 1)], here, loc.at[k]).start()
            for (px, py), q in _other_chips(x, y):
                _remote(src[k].at[pl.ds(q, 1)], here, send.at[k], recv.at[k], (px, py, c)).start()

    def relay(self, *refs):
        pass

    def forward(self, *refs):
        pass

    def finish(self, src, dst, send, recv, loc):
        x, y, c, me = _my_place()
        mine = 2 * x + y
        for k in range(self.n):
            pltpu.make_async_copy(src[k].at[pl.ds(mine, 1)], dst[k].at[pl.ds(mine, 1)], loc.at[k]).wait()
            _wait_slabs(dst[k], N_CHIPS - 1, send.at[k], recv.at[k], (x, y, c))


class _Jobs:
    def __init__(self, jobs):
        self.jobs = list(jobs)
        self.sources = [s for job in self.jobs for s in job.sources]
        self.dests = [d for job in self.jobs for d in job.dests]
        self.n = len(self.sources)

    def _stage(self, stage, src, dst, sems):
        k0 = s0 = 0
        for job in self.jobs:
            getattr(job, stage)(src[k0:k0 + job.n], dst[k0:k0 + job.n], *sems[s0:s0 + job.n_sems])
            k0 += job.n
            s0 += job.n_sems

    def start(self, src, dst, *sems):
        self._stage('start', src, dst, sems)

    def relay(self, src, dst, *sems):
        self._stage('relay', src, dst, sems)

    def forward(self, src, dst, *sems):
        self._stage('forward', src, dst, sems)

    def finish(self, src, dst, *sems):
        self._stage('finish', src, dst, sems)

    def split(self, got):
        out, k0 = [], 0
        for job in self.jobs:
            out.append(got[k0:k0 + job.n])
            k0 += job.n
        return out


def _job_sems(job):
    if isinstance(job, _Jobs):
        return [s for part in job.jobs for s in _job_sems(part)]
    return [pltpu.SemaphoreType.DMA((job.n,))] * job.n_sems


def _call(body, comm, *, name, grid, in_specs, out_specs, out_shape, operands, sem, scratch_shapes=(), aliases=None):
    single = not isinstance(out_shape, (list, tuple))
    out_shape = [out_shape] if single else list(out_shape)
    out_specs = [out_specs] if single else list(out_specs)
    if comm is None:
        res = pl.pallas_call(body, name=name, grid=grid, in_specs=list(in_specs), out_specs=out_specs,
                             out_shape=out_shape, scratch_shapes=list(scratch_shapes),
                             input_output_aliases=aliases or {}, compiler_params=_params(sem))(*operands)
        return res[0] if single else res
    n_in, n_out, n_scr, n_c = len(in_specs), len(out_shape), len(scratch_shapes), comm.n
    any_spec = pl.BlockSpec(memory_space=pl.ANY)
    steps = math.prod(grid)
    early = min(steps - 1, (steps * RELAY_AT_TENTHS) // 10)
    mid = min(steps - 1, (steps * FORWARD_AT_TENTHS) // 10)

    def hosted(*refs):
        ins, c_src = refs[:n_in], refs[n_in:n_in + n_c]
        outs = refs[n_in + n_c:n_in + n_c + n_out]
        c_dst = refs[n_in + n_c + n_out:n_in + 2 * n_c + n_out]
        scr = refs[n_in + 2 * n_c + n_out:n_in + 2 * n_c + n_out + n_scr]
        sems = refs[n_in + 2 * n_c + n_out + n_scr:]
        step = pl.program_id(0)
        for d in range(1, len(grid)):
            step = step * grid[d] + pl.program_id(d)

        @pl.when(step == 0)
        def _():
            comm.start(c_src, c_dst, *sems)

        @pl.when(step == early)
        def _():
            comm.relay(c_src, c_dst, *sems)

        @pl.when(step == mid)
        def _():
            comm.forward(c_src, c_dst, *sems)

        body(*ins, *outs, *scr)

        @pl.when(step == steps - 1)
        def _():
            comm.finish(c_src, c_dst, *sems)

    res = pl.pallas_call(hosted, name=name, grid=grid, in_specs=list(in_specs) + [any_spec] * n_c,
                         out_specs=out_specs + [any_spec] * n_c, out_shape=out_shape + comm.dests,
                         scratch_shapes=list(scratch_shapes) + _job_sems(comm), input_output_aliases=aliases or {},
                         compiler_params=_params(("arbitrary",) * len(grid)))(*operands, *comm.sources)
    main = res[:n_out]
    return (main[0] if single else main), res[n_out:]


def comm_only(name, jobs):
    any_spec = pl.BlockSpec(memory_space=pl.ANY)
    n_all = sum(job.n for job in jobs)

    def body(*refs):
        srcs, dsts, sems = refs[:n_all], refs[n_all:2 * n_all], refs[2 * n_all:]
        parts, k0, s0 = [], 0, 0
        for job in jobs:
            parts.append((job, srcs[k0:k0 + job.n], dsts[k0:k0 + job.n], sems[s0:s0 + job.n_sems]))
            k0 += job.n
            s0 += job.n_sems
        for stage in ('start', 'relay', 'forward', 'finish'):
            for job, src, dst, sem in parts:
                getattr(job, stage)(src, dst, *sem)

    res = pl.pallas_call(body, name=name, in_specs=[any_spec] * n_all, out_specs=[any_spec] * n_all,
                         out_shape=[d for job in jobs for d in job.dests],
                         scratch_shapes=[s for job in jobs for s in _job_sems(job)],
                         )(*[s for job in jobs for s in job.sources])
    out, k0 = [], 0
    for job in jobs:
        out.append(res[k0:k0 + job.n])
        k0 += job.n
    return out


def _dot(a, b, dims):
    if a.dtype != BF16:
        a = a.astype(BF16)
    if b.dtype != BF16:
        b = b.astype(BF16)
    return lax.dot_general(a, b, (dims, ((), ())), preferred_element_type=F32)


def _sigmoid(v):
    return 1.0 / (1.0 + jnp.exp(-v))


def _ln_stats(z):
    mu = jnp.mean(z, axis=-1, keepdims=True)
    zc = z - mu
    var = jnp.mean(zc * zc, axis=-1, keepdims=True)
    return zc * lax.rsqrt(var + LN_EPS)


def _ln_fwd(z, g, b):
    return _ln_stats(z) * g + b


def _ln_bwd(dy, z, g):
    xhat = _ln_stats(z)
    mu = jnp.mean(z, axis=-1, keepdims=True)
    zc = z - mu
    rstd = lax.rsqrt(jnp.mean(zc * zc, axis=-1, keepdims=True) + LN_EPS)
    dxh = dy * g
    m1 = jnp.mean(dxh, axis=-1, keepdims=True)
    m2 = jnp.mean(dxh * xhat, axis=-1, keepdims=True)
    dz = rstd * (dxh - m1 - xhat * m2)
    return dz, jnp.sum(dy * xhat, axis=0, keepdims=True), jnp.sum(dy, axis=0, keepdims=True)


def _extra_spec(shape, kind, tm, tn, ij):
    if kind == 'tile':
        return pl.BlockSpec((tm, tn), lambda *g: (ij(g)[0], ij(g)[1]))
    if kind in ('row', 'rowacc'):
        return pl.BlockSpec((1, tn), lambda *g: (0, ij(g)[1]))
    if kind == 'tab':
        return pl.BlockSpec((tm, LANES), lambda *g: (ij(g)[0], 0))
    if kind == 'rows':
        return pl.BlockSpec((tm, shape[1]), lambda *g: (ij(g)[0], 0))
    raise ValueError(kind)


def mm_nn(name, pairs, extras, outs, epi, *, tm, tn, tk=None, comm=None):
    M = pairs[0][0].shape[0]
    N = pairs[0][1].shape[0] * pairs[0][1].shape[2]
    n_pairs = len(pairs)
    K0 = pairs[0][0].shape[1]
    tk = K0 if tk is None else tk
    nk = K0 // tk
    assert nk == 1 or n_pairs == 1
    assert M % tm == 0 and N % tn == 0 and K0 % tk == 0
    has_rowacc = any(kind == 'rowacc' for _, kind in outs)
    assert not has_rowacc or (N == tn and nk == 1)
    in_specs, operands, slabs = [], [], []
    for a, b, off in pairs:
        K = a.shape[1]
        ktile = K if n_pairs > 1 else tk
        n = b.shape[2]
        assert b.shape[1] == K
        in_specs.append(pl.BlockSpec((tm, ktile), lambda i, j, k: (i, k)))
        if tn <= n:
            assert n % tn == 0
            r = n // tn
            in_specs.append(pl.BlockSpec((None, ktile, tn),
                                         lambda i, j, k, r=r, off=off: ((j + off) // r, k, (j + off) % r)))
            slabs.append(0)
        else:
            assert tn % n == 0
            in_specs.append(pl.BlockSpec((tn // n, ktile, n), lambda i, j, k, off=off: (j + off, k, 0)))
            slabs.append(tn // n)
        operands += [a, b]
    ij = lambda g: (g[0], g[1])
    for arr, kind in extras:
        in_specs.append(_extra_spec(arr.shape, kind, tm, tn, ij))
        operands.append(arr)
    out_specs = [_extra_spec(o.shape, kind, tm, tn, ij) for o, kind in outs]
    n_ex, n_out = len(extras), len(outs)

    def pair_dot(ab, q):
        a = ab[2 * q][...]
        if not slabs[q]:
            return _dot(a, ab[2 * q + 1][...], ((1,), (0,)))
        return jnp.concatenate([_dot(a, ab[2 * q + 1][s], ((1,), (0,))) for s in range(slabs[q])], axis=1)

    def body(*refs):
        ab = refs[:2 * n_pairs]
        ex = refs[2 * n_pairs:2 * n_pairs + n_ex]
        out = refs[2 * n_pairs + n_ex:2 * n_pairs + n_ex + n_out]
        i = pl.program_id(0)
        if nk == 1:
            accs = [pair_dot(ab, q) for q in range(n_pairs)]
            epi(accs, ex, out, i)
        else:
            acc_ref = refs[-1]
            k = pl.program_id(2)

            @pl.when(k == 0)
            def _():
                acc_ref[...] = jnp.zeros_like(acc_ref)

            acc_ref[...] += pair_dot(ab, 0)

            @pl.when(k == nk - 1)
            def _():
                epi([acc_ref[...]], ex, out, i)

    scratch = [pltpu.VMEM((tm, tn), F32)] if nk > 1 else []
    sem = ("arbitrary",) * 3 if has_rowacc else ("parallel", "parallel", "arbitrary")
    return _call(body, comm, name=name, grid=(M // tm, N // tn, nk), in_specs=in_specs, out_specs=out_specs,
                 out_shape=[o for o, _ in outs], scratch_shapes=scratch, sem=sem, operands=operands)


def mm_nt(name, a, b, extras, outs, epi, *, tm, tko, tc, comm=None, a_pro=None, dws=()):
    M, N = a.shape
    J, K, n = b.shape
    assert J * n == N and M % tm == 0 and K % tko == 0 and N % tc == 0
    nc = N // tc
    assert a_pro is None or (nc == 1 and tc <= n and K == tko == N)
    assert not dws or (nc == 1 and K == tko and tc <= n)
    n_dw, nt = len(dws), M // tm
    has_rowacc = any(kind == 'rowacc' for _, kind in outs)
    assert not has_rowacc or K == tko
    if tc <= n:
        assert n % tc == 0
        r = n // tc
        slabs = 0
        b_spec = pl.BlockSpec((None, tko, tc), lambda i, j, c: (c // r, j, c % r))
    else:
        assert tc % n == 0
        slabs = tc // n
        b_spec = pl.BlockSpec((slabs, tko, n), lambda i, j, c: (c, j, 0))
    n_ab = 2 if a_pro is None else 1
    in_specs = ([pl.BlockSpec((tm, tc), lambda i, j, c: (i, c))] if a_pro is None else []) + [b_spec]

    def nt_dot(a_ref, b_ref):
        if not slabs:
            return _dot(a_ref[...], b_ref[...], ((1,), (1,)))
        acc = _dot(a_ref[:, 0:n], b_ref[0], ((1,), (1,)))
        for s in range(1, slabs):
            acc = acc + _dot(a_ref[:, s * n:(s + 1) * n], b_ref[s], ((1,), (1,)))
        return acc

    ij = lambda g: (g[0], g[1])
    operands = [a, b] if a_pro is None else [b]
    for arr, kind in extras:
        in_specs.append(_extra_spec(arr.shape, kind, tm, tko, ij))
        operands.append(arr)
    out_specs = [_extra_spec(o.shape, kind, tm, tko, ij) for o, kind in outs]
    n_ex, n_out = len(extras), len(outs)
    out_shape = [o for o, _ in outs]
    for lhs, _ in dws:
        in_specs.append(pl.BlockSpec((tm, lhs.shape[1]), lambda i, j, c: (i, 0)))
        operands.append(lhs)
        out_specs.append(pl.BlockSpec((None, lhs.shape[1], N), lambda i, j, c: (0, 0, 0)))
        out_shape.append(_sds((1, lhs.shape[1], N), BF16))

    def weight_grads(refs, a_tile, i):
        lhs_refs = refs[n_ab + n_ex:n_ab + n_ex + n_dw]
        dw_refs = refs[n_ab + n_ex + n_dw + n_out:n_ab + n_ex + n_dw + n_out + n_dw]
        acc_refs = refs[len(refs) - n_dw:]
        for (_, rhs_fn), lhs_ref, dw_ref, acc_ref in zip(dws, lhs_refs, dw_refs, acc_refs):
            part = _dot(lhs_ref[...], a_tile if rhs_fn is None else rhs_fn(), ((0,), (0,)))
            _init_or_add(acc_ref, i, part)

            @pl.when(i == nt - 1)
            def _(dw_ref=dw_ref, acc_ref=acc_ref):
                dw_ref[...] = acc_ref[...].astype(dw_ref.dtype)

    def body(*refs):
        ex = refs[n_ab:n_ab + n_ex]
        out = refs[n_ab + n_ex + n_dw:n_ab + n_ex + n_dw + n_out]
        i = pl.program_id(0)
        if a_pro is not None:
            a_tile = a_pro(ex)
            epi(_dot(a_tile, refs[0][...], ((1,), (1,))), ex, out, i)
            weight_grads(refs, a_tile, i)
            return
        a_ref, b_ref = refs[:2]
        if nc == 1:
            epi(nt_dot(a_ref, b_ref), ex, out, i)
            if dws:
                weight_grads(refs, a_ref[...], i)
        else:
            acc_ref = refs[-1]
            c = pl.program_id(2)

            @pl.when(c == 0)
            def _():
                acc_ref[...] = jnp.zeros_like(acc_ref)

            acc_ref[...] += nt_dot(a_ref, b_ref)

            @pl.when(c == nc - 1)
            def _():
                epi(acc_ref[...], ex, out, i)

    scratch = ([pltpu.VMEM((tm, tko), F32)] if nc > 1 else []) + [pltpu.VMEM((lhs.shape[1], N), F32) for lhs, _ in dws]
    sem = ("arbitrary",) * 3 if (has_rowacc or dws) else ("parallel", "parallel", "arbitrary")
    return _call(body, comm, name=name, grid=(M // tm, K // tko, nc), in_specs=in_specs, out_specs=out_specs,
                 out_shape=out_shape, scratch_shapes=scratch, sem=sem, operands=operands)


def mm_tn(name, a, d, n, out_dtype, *, tm, tk, tn, comm=None):
    M, K = a.shape
    N = d.shape[1]
    assert d.shape[0] == M and N % n == 0 and N % tn == 0 and K % tk == 0 and M % tm == 0
    nm = M // tm
    if tn <= n:
        assert n % tn == 0
        r = n // tn
        slabs = 0
        o_spec = pl.BlockSpec((None, tk, tn), lambda kk, j, m: (j // r, kk, j % r))
    else:
        assert tn % n == 0
        slabs = tn // n
        o_spec = pl.BlockSpec((slabs, tk, n), lambda kk, j, m: (j, kk, 0))

    def write(o_ref, acc):
        if not slabs:
            o_ref[...] = acc.astype(o_ref.dtype)
        else:
            for s in range(slabs):
                o_ref[s] = acc[:, s * n:(s + 1) * n].astype(o_ref.dtype)

    def body(a_ref, d_ref, o_ref, *scratch):
        if nm == 1:
            write(o_ref, _dot(a_ref[...], d_ref[...], ((0,), (0,))))
            return
        acc_ref, = scratch
        m = pl.program_id(2)

        @pl.when(m == 0)
        def _():
            acc_ref[...] = jnp.zeros_like(acc_ref)

        acc_ref[...] += _dot(a_ref[...], d_ref[...], ((0,), (0,)))

        @pl.when(m == nm - 1)
        def _():
            write(o_ref, acc_ref[...])

    return _call(
        body, comm, name=name, grid=(K // tk, N // tn, nm),
        in_specs=[pl.BlockSpec((tm, tk), lambda kk, j, m: (m, kk)), pl.BlockSpec((tm, tn), lambda kk, j, m: (m, j))],
        out_specs=o_spec, out_shape=_sds((N // n, K, n), out_dtype),
        scratch_shapes=[pltpu.VMEM((tk, tn), F32)] if nm > 1 else [],
        sem=("parallel", "parallel", "arbitrary"), operands=[a, d])


def mlp_du_dw(name, dzb, w_down, u, act, xin, n, *, tm, tf, comm=None):
    T, D = dzb.shape
    F = u.shape[1]
    assert T % tm == 0 and F % tf == 0 and tf % n == 0
    slabs, nt = tf // n, T // tm

    def body(dz_ref, w_ref, u_ref, a_ref, x_ref, du_ref, dwu_ref, dwd_ref, up_acc, down_acc):
        i = pl.program_id(1)
        dz = dz_ref[...]
        da = _dot(dz, w_ref[...], ((1,), (1,)))
        du = (da * (2.0 * jnp.maximum(u_ref[...].astype(F32), 0.0))).astype(BF16)
        du_ref[...] = du
        _init_or_add(up_acc, i, _dot(x_ref[...], du, ((0,), (0,))))
        _init_or_add(down_acc, i, _dot(a_ref[...], dz, ((0,), (0,))))

        @pl.when(i == nt - 1)
        def _():
            for s in range(slabs):
                dwu_ref[s] = up_acc[:, s * n:(s + 1) * n].astype(dwu_ref.dtype)
            dwd_ref[...] = down_acc[...].astype(dwd_ref.dtype)

    hidden = pl.BlockSpec((tm, tf), lambda j, i: (i, j))
    tokens = pl.BlockSpec((tm, D), lambda j, i: (i, 0))
    return _call(
        body, comm, name=name, grid=(F // tf, nt),
        in_specs=[tokens, pl.BlockSpec((None, tf, D), lambda j, i: (0, j, 0)), hidden, hidden, tokens],
        out_specs=[hidden, pl.BlockSpec((slabs, D, n), lambda j, i: (j, 0, 0)),
                   pl.BlockSpec((None, tf, D), lambda j, i: (0, j, 0))],
        out_shape=[_sds((T, F), BF16), _sds((F // n, D, n), BF16), _sds((1, F, D), BF16)],
        scratch_shapes=[pltpu.VMEM((D, tf), F32), pltpu.VMEM((tf, D), F32)], sem=("arbitrary", "arbitrary"),
        operands=[dzb, w_down, u, act, xin])


def _init_or_add(ref, i, value):
    @pl.when(i == 0)
    def _():
        ref[...] = value

    @pl.when(i > 0)
    def _():
        ref[...] += value


def _rope_tables(T):
    pos = jnp.arange(T, dtype=F32)
    inv_freq = ROPE_THETA ** (-jnp.arange(0, ROPE_DIM, 2, dtype=F32) / ROPE_DIM)
    ang = pos[:, None] * inv_freq[None, :]
    cos, sin = jnp.cos(ang), jnp.sin(ang)
    ones = jnp.ones((T, HEAD_DIM - ROPE_DIM), F32)
    zeros = jnp.zeros((T, HEAD_DIM - ROPE_DIM), F32)
    zh = jnp.zeros((T, ROPE_HALF), F32)
    c_head = jnp.concatenate([cos, cos, ones], axis=1)
    s_up = jnp.concatenate([-sin, zh, zeros], axis=1)
    s_dn = jnp.concatenate([zh, sin, zeros], axis=1)
    rep = LANES // HEAD_DIM
    return tuple(jnp.tile(t, (1, rep)) for t in (c_head, s_up, s_dn))


def _rope_chunk(t, c, s_up, s_dn):
    return t * c + pltpu.roll(t, LANES - ROPE_HALF, 1) * s_up + pltpu.roll(t, ROPE_HALF, 1) * s_dn


def _rope_chunk_bwd(d, c, s_up, s_dn):
    return d * c + pltpu.roll(d * s_up, ROPE_HALF, 1) + pltpu.roll(d * s_dn, LANES - ROPE_HALF, 1)


def _rope_wide(t, c, s_up, s_dn, fn):
    chunks = [fn(t[:, q * LANES:(q + 1) * LANES], c, s_up, s_dn) for q in range(t.shape[1] // LANES)]
    return chunks[0] if len(chunks) == 1 else jnp.concatenate(chunks, axis=1)


def _taps_by_residue(first):
    groups = {}
    for o in range(first, first + CONV_WIDTH):
        groups.setdefault(o % SUBLANES, []).append(o)
    return sorted(groups.items())


def _shifted_taps(win_ref, span_ref, r0, res, offs, rb, ls):
    if res == 0:
        return [functools.partial(lambda o: win_ref[pl.ds(r0 + o, rb), ls], o) for o in offs]
    n = rb + offs[-1] - res
    span_ref[0:n, :] = win_ref[pl.ds(r0 + res, n), ls]
    return [functools.partial(lambda o: span_ref[pl.ds(o - res, rb), :], o) for o in offs]


def _halo_before_spec(tm, D):
    return pl.BlockSpec((CONV_HALO, D), lambda i: (jnp.maximum(i * (tm // CONV_HALO) - 1, 0), 0))


def dwconv_fwd(g, w_dw, b_dw, ln_g, ln_b, *, tm, comm=None):
    T, D = g.shape
    nl = D // LANES

    rb = min(CONV_ROWS, tm)

    def body(g_ref, gh_ref, w_ref, b_ref, lg_ref, lb_ref, c_ref, s_ref, win_ref, span_ref):
        i = pl.program_id(0)
        win_ref[0:CONV_HALO, :] = jnp.where(i > 0, gh_ref[...], 0.0)
        win_ref[CONV_HALO:, :] = g_ref[...]

        def lane_chunk(q, carry):
            ls = pl.ds(pl.multiple_of(q * LANES, LANES), LANES)
            for r0 in range(0, tm, rb):
                acc = jnp.broadcast_to(b_ref[:, ls], (rb, LANES))
                for res, offs in _taps_by_residue(CONV_HALO - (CONV_WIDTH - 1)):
                    taps = _shifted_taps(win_ref, span_ref, r0, res, offs, rb, ls)
                    for o, tap in zip(offs, taps):
                        k = o - (CONV_HALO - (CONV_WIDTH - 1))
                        acc = acc + tap() * w_ref[k:k + 1, ls]
                c_ref[pl.ds(r0, rb), ls] = acc
            return carry

        lax.fori_loop(0, nl, lane_chunk, 0)
        n = _ln_fwd(c_ref[...], lg_ref[...], lb_ref[...])
        s_ref[...] = (n * _sigmoid(n)).astype(s_ref.dtype)

    row = pl.BlockSpec((1, D), lambda i: (0, 0))
    return _call(
        body, comm, name="dwconv_fwd", grid=(T // tm,),
        in_specs=[pl.BlockSpec((tm, D), lambda i: (i, 0)), _halo_before_spec(tm, D),
                  pl.BlockSpec((CONV_HALO, D), lambda i: (0, 0)), row, row, row],
        out_specs=[pl.BlockSpec((tm, D), lambda i: (i, 0)), pl.BlockSpec((tm, D), lambda i: (i, 0))],
        out_shape=[_sds((T, D), F32), _sds((T, D), BF16)],
        scratch_shapes=[pltpu.VMEM((tm + CONV_HALO, D), F32), pltpu.VMEM((rb + CONV_HALO, LANES), F32)],
        sem=("parallel",), operands=[g, g, w_dw, b_dw, ln_g, ln_b])


def dwconv_bwd(dc, g, ha, hg, w_dw, *, tm, comm=None):
    T, D = g.shape
    nl = D // LANES
    last = T // CONV_HALO - 1
    nt = T // tm

    rb = min(CONV_ROWS, tm)

    def body(dc_ref, dcn_ref, g_ref, gh_ref, ha_ref, hg_ref, w_ref, dh_ref, dw_ref, dbdw_ref, dbin_ref,
             win_ref, dwin_ref, dg_ref, dwp_ref, span_ref):
        i = pl.program_id(0)
        win_ref[0:CONV_HALO, :] = jnp.where(i > 0, gh_ref[...], 0.0)
        win_ref[CONV_HALO:, :] = g_ref[...]
        dwin_ref[0:tm, :] = dc_ref[...]
        dwin_ref[tm:, :] = jnp.where(i < nt - 1, dcn_ref[...], 0.0)

        @pl.when(i == 0)
        def _():
            dwp_ref[...] = jnp.zeros_like(dwp_ref)

        first = CONV_HALO - (CONV_WIDTH - 1)

        def lane_chunk(q, carry):
            ls = pl.ds(pl.multiple_of(q * LANES, LANES), LANES)
            for r0 in range(0, tm, rb):
                acc = jnp.zeros((rb, LANES), F32)
                for res, offs in _taps_by_residue(0):
                    taps = _shifted_taps(dwin_ref, span_ref, r0, res, offs, rb, ls)
                    for o, tap in zip(offs, taps):
                        k = CONV_WIDTH - 1 - o
                        acc = acc + tap() * w_ref[k:k + 1, ls]
                dg_ref[pl.ds(r0, rb), ls] = acc
                dcv = dwin_ref[pl.ds(r0, rb), ls]
                for res, offs in _taps_by_residue(first):
                    taps = _shifted_taps(win_ref, span_ref, r0, res, offs, rb, ls)
                    for o, tap in zip(offs, taps):
                        k = o - first
                        prod = dcv * tap()
                        dwp_ref[k, :, ls] += jnp.sum(prod.reshape(rb // SUBLANES, SUBLANES, LANES), axis=0)
            return carry

        lax.fori_loop(0, nl, lane_chunk, 0)

        @pl.when(i == nt - 1)
        def _():
            for k in range(CONV_WIDTH):
                dw_ref[k:k + 1, :] = jnp.sum(dwp_ref[k], axis=0, keepdims=True)
            dw_ref[CONV_WIDTH:, :] = jnp.zeros((CONV_HALO - CONV_WIDTH, D), F32)
        dg = dg_ref[...]
        ha = ha_ref[...].astype(F32)
        sg = _sigmoid(hg_ref[...].astype(F32))
        d_ha = dg * sg
        d_hg = dg * ha * sg * (1.0 - sg)
        dh_ref[:, 0:D] = d_ha.astype(dh_ref.dtype)
        dh_ref[:, D:] = d_hg.astype(dh_ref.dtype)
        _init_or_add(dbdw_ref, i, jnp.sum(dc_ref[...], axis=0, keepdims=True))
        _init_or_add(dbin_ref, i, jnp.concatenate([jnp.sum(d_ha, axis=0, keepdims=True),
                                                   jnp.sum(d_hg, axis=0, keepdims=True)], axis=1))

    tile = pl.BlockSpec((tm, D), lambda i: (i, 0))
    return _call(
        body, comm, name="dwconv_bwd", grid=(nt,),
        in_specs=[tile,
                  pl.BlockSpec((CONV_HALO, D), lambda i: (jnp.minimum((i + 1) * (tm // CONV_HALO), last), 0)),
                  tile, _halo_before_spec(tm, D),
                  tile, tile, pl.BlockSpec((CONV_HALO, D), lambda i: (0, 0))],
        out_specs=[pl.BlockSpec((tm, 2 * D), lambda i: (i, 0)), pl.BlockSpec((CONV_HALO, D), lambda i: (0, 0)),
                   pl.BlockSpec((1, D), lambda i: (0, 0)), pl.BlockSpec((1, 2 * D), lambda i: (0, 0))],
        out_shape=[_sds((T, 2 * D), BF16), _sds((CONV_HALO, D), F32), _sds((1, D), F32), _sds((1, 2 * D), F32)],
        scratch_shapes=[pltpu.VMEM((tm + CONV_HALO, D), F32), pltpu.VMEM((tm + CONV_HALO, D), F32),
                        pltpu.VMEM((tm, D), F32), pltpu.VMEM((CONV_WIDTH, SUBLANES, D), F32),
                        pltpu.VMEM((rb + CONV_HALO, LANES), F32)],
        sem=("arbitrary",), operands=[dc, dc, g, g, ha, hg, w_dw])


def _attn_specs(HD, W):
    B = ATT_BLOCK
    return [pl.BlockSpec((B, HD), lambda n: (n, 0)),
            pl.BlockSpec((B, 4 * W), lambda n: (n, 0)),
            pl.BlockSpec((B, 4 * W), lambda n: (jnp.maximum(n - 1, 0), 0))]


def _band_mask(n):
    r = lax.broadcasted_iota(jnp.int32, (ATT_BLOCK, 2 * ATT_BLOCK), 0)
    j = lax.broadcasted_iota(jnp.int32, (ATT_BLOCK, 2 * ATT_BLOCK), 1)
    return (j > r) & (j <= r + ATT_BLOCK) & ((n > 0) | (j >= ATT_BLOCK))


def _band(kvc_ref, kvp_ref, part, g, parity, W):
    lanes = slice((2 * part + parity) * W + g * LANES, (2 * part + parity) * W + (g + 1) * LANES)
    return jnp.concatenate([kvp_ref[:, lanes], kvc_ref[:, lanes]], axis=0)


def _half_mask(parity):
    lane = lax.broadcasted_iota(jnp.int32, (1, LANES), 1)
    return (lane < HEAD_DIM) if parity == 0 else (lane >= HEAD_DIM)


def widen_kv(k, v, out_ref, n_kv):
    W = n_kv * LANES
    low = _half_mask(0)
    for part, src in enumerate((k, v)):
        for cg in range(n_kv * HEAD_DIM // LANES):
            chunk = src[:, cg * LANES:(cg + 1) * LANES]
            swapped = pltpu.roll(chunk, HEAD_DIM, 1)
            for g, lo, hi in ((2 * cg, chunk, swapped), (2 * cg + 1, swapped, chunk)):
                base = 2 * part * W + g * LANES
                out_ref[:, base:base + LANES] = jnp.where(low, lo, 0.0).astype(out_ref.dtype)
                out_ref[:, base + W:base + W + LANES] = jnp.where(low, 0.0, hi).astype(out_ref.dtype)


def attn_fwd(q, kvx, sinks, n_heads, n_kv, comm=None):
    T, HD = q.shape
    W = n_kv * LANES
    B = ATT_BLOCK
    chunks_per_group = n_heads // n_kv // 2
    scale = 1.0 / math.sqrt(HEAD_DIM)

    def body(q_ref, kvc_ref, kvp_ref, sink_ref, o_ref, lse_ref):
        n = pl.program_id(0)
        mask = jnp.tile(_band_mask(n), (n_heads, 1))
        s = jnp.concatenate(
            [_dot(q_ref[:, (h // 2) * LANES:(h // 2 + 1) * LANES],
                  _band(kvc_ref, kvp_ref, 0, h // 2 // chunks_per_group, h % 2, W), ((1,), (1,)))
             for h in range(n_heads)], axis=0)
        sink = jnp.concatenate([jnp.broadcast_to(sink_ref[:, h:h + 1], (B, 1)) for h in range(n_heads)], axis=0)
        s = jnp.where(mask, s * scale, MASK_VALUE)
        m = jnp.maximum(jnp.max(s, axis=-1, keepdims=True), sink)
        e = jnp.exp(s - m)
        total = _dot(e, jnp.ones((2 * B, LANES), BF16), ((1,), (0,))) + jnp.exp(sink - m)
        lse = m + jnp.log(total[:, 0:1])
        inv = 1.0 / total
        probs = (e * jnp.concatenate([inv, inv], axis=1)).astype(BF16)
        for c in range(n_heads // 2):
            g = c // chunks_per_group
            out = (_dot(probs[2 * c * B:(2 * c + 1) * B], _band(kvc_ref, kvp_ref, 1, g, 0, W), ((1,), (0,)))
                   + _dot(probs[(2 * c + 1) * B:(2 * c + 2) * B], _band(kvc_ref, kvp_ref, 1, g, 1, W), ((1,), (0,))))
            o_ref[:, c * LANES:(c + 1) * LANES] = out.astype(o_ref.dtype)
        lse_ref[...] = jnp.concatenate([lse[h * B:(h + 1) * B] for h in range(n_heads)], axis=1)

    return _call(
        body, comm, name="attn_fwd", grid=(T // B,),
        in_specs=_attn_specs(HD, W) + [pl.BlockSpec((1, n_heads), lambda n: (0, 0))],
        out_specs=[pl.BlockSpec((B, HD), lambda n: (n, 0)), pl.BlockSpec((B, n_heads), lambda n: (n, 0))],
        out_shape=[_sds((T, HD), BF16), _sds((T, n_heads), F32)],
        sem=("parallel",), operands=[q, kvx, kvx, sinks])


def attn_bwd(q, kvx, do, lse, sinks, rope, n_heads, n_kv, comm=None):
    T, HD = q.shape
    KVD = n_kv * HEAD_DIM
    W = n_kv * LANES
    B = ATT_BLOCK
    chunks_per_group = n_heads // n_kv // 2
    scale = 1.0 / math.sqrt(HEAD_DIM)
    nb = T // B

    def body(q_ref, kvc_ref, kvp_ref, do_ref, lse_ref, sink_ref, c_ref, su_ref, sd_ref, dq_ref, dkv_ref, dsink_ref):
        n = pl.program_id(0)

        @pl.when(n == 0)
        def _():
            dkv_ref[...] = jnp.zeros_like(dkv_ref)
            dsink_ref[...] = jnp.zeros_like(dsink_ref)

        def chunk(ref, h):
            return ref[:, (h // 2) * LANES:(h // 2 + 1) * LANES]

        def band(part, h):
            return _band(kvc_ref, kvp_ref, part, h // 2 // chunks_per_group, h % 2, W)

        def stack(per_head):
            return jnp.concatenate([per_head(h) for h in range(n_heads)], axis=0)

        mask = jnp.tile(_band_mask(n), (n_heads, 1))
        s = stack(lambda h: _dot(chunk(q_ref, h), band(0, h), ((1,), (1,))))
        dp = stack(lambda h: _dot(chunk(do_ref, h), band(1, h), ((1,), (1,))))
        lse = stack(lambda h: lse_ref[:, h:h + 1])
        sink = stack(lambda h: jnp.broadcast_to(sink_ref[:, h:h + 1], (B, 1)))
        probs = jnp.exp(jnp.where(mask, s * scale, MASK_VALUE) - lse)
        delta = jnp.sum(probs * dp, axis=-1, keepdims=True)
        ds = (probs * (dp - delta) * scale).astype(BF16)
        probs = probs.astype(BF16)
        sink_term = jnp.exp(sink - lse) * delta
        dsk = [-jnp.sum(sink_term[h * B:(h + 1) * B], axis=0, keepdims=True) for h in range(n_heads)]

        dk_wide, dv_wide = [None] * n_kv, [None] * n_kv
        for c in range(n_heads // 2):
            g = c // chunks_per_group
            dq2 = None
            for h in (2 * c, 2 * c + 1):
                half = _half_mask(h % 2)
                q2, do2 = chunk(q_ref, h), chunk(do_ref, h)
                ds_h, p_h = ds[h * B:(h + 1) * B], probs[h * B:(h + 1) * B]
                part = _dot(ds_h, band(0, h), ((1,), (0,)))
                dq2 = part if dq2 is None else dq2 + part
                dk_h = _dot(ds_h, jnp.where(half, q2, jnp.zeros_like(q2)), ((0,), (0,)))
                dv_h = _dot(p_h, jnp.where(half, do2, jnp.zeros_like(do2)), ((0,), (0,)))
                dk_wide[g] = dk_h if dk_wide[g] is None else dk_wide[g] + dk_h
                dv_wide[g] = dv_h if dv_wide[g] is None else dv_wide[g] + dv_h
            dq_ref[:, c * LANES:(c + 1) * LANES] = _rope_chunk_bwd(
                dq2, c_ref[...], su_ref[...], sd_ref[...]).astype(dq_ref.dtype)

        def fold(wide):
            low = _half_mask(0)
            both = [w + pltpu.roll(w, HEAD_DIM, 1) for w in wide]
            return jnp.concatenate([jnp.where(low, both[2 * cg], both[2 * cg + 1]) for cg in range(n_kv // 2)], axis=1)

        dkv = jnp.concatenate([fold(dk_wide), fold(dv_wide)], axis=1)
        prev = pl.ds(pl.multiple_of(jnp.maximum(n - 1, 0) * B, B), B)
        cur = pl.ds(pl.multiple_of(n * B, B), B)
        dkv_ref[prev, :] += dkv[0:B, :]
        dkv_ref[cur, :] += dkv[B:, :]
        dsink_ref[...] += jnp.concatenate(dsk, axis=1)

    tab = pl.BlockSpec((B, LANES), lambda n: (n, 0))
    return _call(
        body, comm, name="attn_bwd", grid=(nb,),
        in_specs=_attn_specs(HD, W) + [pl.BlockSpec((B, HD), lambda n: (n, 0)),
                                       pl.BlockSpec((B, n_heads), lambda n: (n, 0)),
                                       pl.BlockSpec((1, n_heads), lambda n: (0, 0)), tab, tab, tab],
        out_specs=[pl.BlockSpec((B, HD), lambda n: (n, 0)), pl.BlockSpec((T, 2 * KVD), lambda n: (0, 0)),
                   pl.BlockSpec((1, n_heads), lambda n: (0, 0))],
        out_shape=[_sds((T, HD + 2 * KVD), BF16), _sds((T, 2 * KVD), F32), _sds((1, n_heads), F32)],
        sem=("arbitrary",), operands=[q, kvx, kvx, do, lse, sinks, *rope])


def dkv_finish(d_qkv, dkv, rope, HD, KVD, *, tm):
    T = dkv.shape[0]
    kv_col = HD // (2 * KVD)

    def body(alias_ref, dkv_ref, c_ref, su_ref, sd_ref, o_ref):
        del alias_ref
        dk = _rope_wide(dkv_ref[:, 0:KVD], c_ref[...], su_ref[...], sd_ref[...], _rope_chunk_bwd)
        o_ref[:, 0:KVD] = dk.astype(o_ref.dtype)
        o_ref[:, KVD:] = dkv_ref[:, KVD:].astype(o_ref.dtype)

    tab = pl.BlockSpec((tm, LANES), lambda i: (i, 0))
    return pl.pallas_call(
        body, name="dkv_finish", grid=(T // tm,),
        in_specs=[pl.BlockSpec(memory_space=pl.ANY), pl.BlockSpec((tm, 2 * KVD), lambda i: (i, 0)), tab, tab, tab],
        out_specs=pl.BlockSpec((tm, 2 * KVD), lambda i: (i, kv_col)),
        out_shape=_sds(d_qkv.shape, d_qkv.dtype), input_output_aliases={0: 0},
        compiler_params=_params(("parallel",)))(d_qkv, dkv, *rope)


def ple_bwd_elem(d_out, pp, gg, *, tm):
    T, D = d_out.shape

    def body(d_ref, pp_ref, gg_ref, dpp_ref, dgg_ref):
        d = d_ref[...]
        sg = _sigmoid(gg_ref[...].astype(F32))
        dpp_ref[...] = (d * sg).astype(dpp_ref.dtype)
        dgg_ref[...] = (d * pp_ref[...].astype(F32) * sg * (1.0 - sg)).astype(dgg_ref.dtype)

    tile = pl.BlockSpec((tm, D), lambda i: (i, 0))
    return pl.pallas_call(
        body, name="ple_bwd_elem", grid=(T // tm,), in_specs=[tile, tile, tile], out_specs=[tile, tile],
        out_shape=[_sds((T, D), BF16), _sds((T, D), BF16)], compiler_params=_params(("parallel",)))(d_out, pp, gg)


def chip_sum(name, g, p_sib, *, ta, part=0, parts=1):
    _, a, b = g.shape
    rows = a // parts
    assert a % parts == 0 and rows % ta == 0
    first = part * (rows // ta)

    def body(core_ref, g_ref, p_ref, o_ref):
        del core_ref
        o_ref[...] = (g_ref[...].astype(F32) + p_ref[...].astype(F32)).astype(o_ref.dtype)

    my_core = lax.axis_index("c").astype(jnp.int32).reshape(1)
    return pl.pallas_call(
        body, name=name, out_shape=_sds((N_CHIPS, rows, b), g.dtype),
        grid_spec=pltpu.PrefetchScalarGridSpec(
            num_scalar_prefetch=1, grid=(N_CHIPS, rows // ta),
            in_specs=[pl.BlockSpec((None, None, ta, b), lambda q, i, core: (q, core[0], first + i, 0)),
                      pl.BlockSpec((None, ta, b), lambda q, i, core: (q, first + i, 0))],
            out_specs=pl.BlockSpec((None, ta, b), lambda q, i, core: (q, i, 0))),
        compiler_params=_params(("arbitrary", "arbitrary")))(my_core, g.reshape(N_CHIPS, 2, a, b), p_sib)


def adamw(name, recvs, w, m, v, *, ta):
    L, a, b = w.shape
    n_terms = recvs[0].shape[0]
    assert a % ta == 0 and len(recvs) == L
    c1 = 1.0 - ADAM_B1 ** ADAM_STEP
    c2 = 1.0 - ADAM_B2 ** ADAM_STEP

    def body(*refs):
        r_refs = refs[:L]
        w_ref, m_ref, v_ref, g_ref, d_ref, nm_ref, nv_ref = refs[L:]
        layer = pl.program_id(0)
        for l in range(L):
            @pl.when(layer == l)
            def _(r_ref=r_refs[l]):
                g = r_ref[0].astype(F32)
                for s in range(1, n_terms):
                    g = g + r_ref[s].astype(F32)
                nm = ADAM_B1 * m_ref[...] + (1.0 - ADAM_B1) * g
                nv = ADAM_B2 * v_ref[...] + (1.0 - ADAM_B2) * jnp.square(g)
                m_hat = nm / c1
                v_hat = nv / c2
                g_ref[...] = g
                d_ref[...] = -ADAM_LR * (m_hat / (jnp.sqrt(v_hat) + ADAM_EPS) + ADAM_WD * w_ref[...])
                nm_ref[...] = nm
                nv_ref[...] = nv

    blk = pl.BlockSpec((None, ta, b), lambda l, i: (l, i, 0))
    out = _sds((L, a, b), F32)
    r_specs = [pl.BlockSpec((n_terms, ta, b), lambda l, i, ll=ll: (0, jnp.where(l == ll, i, 0), 0)) for ll in range(L)]
    return pl.pallas_call(
        body, name=name, grid=(L, a // ta), in_specs=r_specs + [blk, blk, blk],
        out_specs=[blk, blk, blk, blk], out_shape=[out, out, out, out],
        compiler_params=_params(("arbitrary", "arbitrary")))(*recvs, w, m, v)


def _pack_rows(parts):
    out = []
    for arr, rows in parts:
        arr = arr.reshape(-1, LANES).astype(F32)
        out.append(jnp.pad(arr, ((0, rows - arr.shape[0]), (0, 0))))
    return jnp.concatenate(out, axis=0)


def _small_rows(a, prefix):
    return _pack_rows([(a[prefix + name], rows) for name, rows in SMALL_SHARDED])


def _unpack_small(packed, a):
    out, r0 = {}, 0
    for name, rows in SMALL_SHARDED:
        shape = a[name].shape
        used = math.prod(shape) // LANES
        out[name] = packed[r0:r0 + used].reshape(shape)
        r0 += rows
    return out


def _rep_rows(a, prefix):
    parts = []
    for name in REPLICATED:
        arr = a[prefix + name]
        if arr.size % LANES:
            arr = jnp.pad(arr.reshape(1, -1), ((0, 0), (0, LANES - arr.size % LANES)))
        rows = -(-arr.size // LANES)
        parts.append((arr, -(-rows // SUBLANES) * SUBLANES))
    return _pack_rows(parts)


def _unpack_rep(packed, a):
    out, r0 = {}, 0
    for name in REPLICATED:
        shape = a[name].shape
        size = math.prod(shape)
        rows = -(-size // LANES)
        out[name] = packed[r0:r0 + rows].reshape(-1)[:size].reshape(shape)
        r0 += -(-rows // SUBLANES) * SUBLANES
    return out


def _step(a):
    x = a['x'][0]
    T, D = x.shape
    tgt = a['loss_target'][0]
    p_in = [a['p'][i, 0] for i in range(DEPTH)]
    PLE = p_in[0].shape[1]
    n_heads = a['attn_sinks'].shape[1]
    HD = n_heads * HEAD_DIM
    KVD = a['kv_w_k'].shape[1]
    n_kv = KVD // HEAD_DIM
    F = a['mlp_w_down'].shape[1] * N_DEV
    tm = min(TOKEN_TILE, T)
    tm2 = min(2 * TOKEN_TILE, T)
    tmc = min(TOKEN_TILE, T)
    tw = 512
    alpha = DEEPNORM_ALPHA
    xb = x.astype(BF16)
    p_b = [p.astype(BF16) for p in p_in]

    def shard3(w):
        return w.reshape((1,) + w.shape) if w.ndim == 2 else w

    def gather(*specs):
        items = []
        for spec in specs:
            w = shard3(a[spec[0]])[spec[1]]
            if len(spec) == 3:
                rows = w.shape[0] // 2
                w = w[spec[2] * rows:(spec[2] + 1) * rows]
            items.append(w.astype(BF16))
        return _GatherJob(items)

    def join_halves(lo, hi):
        return jnp.concatenate([lo, hi], axis=1)

    (W_in, small_full), = comm_only("gather_first",
                                    [_GatherJob([a['conv_w_in'][0].astype(BF16), _small_rows(a, '')])])
    r0, small = 0, {}
    for name, rows in SMALL_SHARDED:
        small[name] = small_full[:, r0:r0 + rows]
        r0 += rows
    b_in = small['conv_b_in'][:, 0:2 * D // N_DEV // LANES].reshape(1, 2 * D)
    w_dw = jnp.transpose(small['conv_w_dw'], (1, 0, 2)).reshape(CONV_HALO, D)
    b_dw, cln_g, cln_b, b_out = (small[nm][:, 0].reshape(1, D) for nm in
                                 ('conv_b_dw', 'conv_ln_g', 'conv_ln_b', 'conv_b_out'))
    W_up, W_down, W_proj, W_gate = {}, {}, {}, {}
    mix_g, mix_b, mlp_g, mlp_b = a['mix_ln_g'], a['mix_ln_b'], a['mlp_ln_g'], a['mlp_ln_b']
    rope = _rope_tables(T)

    def set_ple_weights(li, g_proj, g_gate):
        W_proj[li] = jnp.transpose(g_proj, (1, 0, 2)).reshape(1, PLE, D)
        W_gate[li] = g_gate.reshape(1, D, D)

    def row(v, i):
        return v[i:i + 1]

    def res_ln_epi(coef):
        def epi(accs, ex, out, i):
            acc = accs[0] if isinstance(accs, list) else accs
            n_ex = len(ex)
            res_ref, g_ref, b_ref = ex[n_ex - 3], ex[n_ex - 2], ex[n_ex - 1]
            z = coef * res_ref[...] + acc
            if n_ex == 4:
                z = z + ex[0][...]
            out[0][...] = z
            xo = _ln_fwd(z, g_ref[...], b_ref[...])
            out[1][...] = xo
            out[2][...] = xo.astype(BF16)
        return epi

    res_ln_outs = [(_sds((T, D), F32), 'tile'), (_sds((T, D), F32), 'tile'), (_sds((T, D), BF16), 'tile')]

    def mlp_fwd(li, xin, xin_b, down_comm=None):
        def up_epi(accs, ex, out, i):
            u = accs[0]
            out[0][...] = u.astype(BF16)
            out[1][...] = jnp.square(jnp.maximum(u, 0.0)).astype(BF16)
        (u, act), got_up = mm_nn(f"mlp_up_{li}", [(xin_b, W_up[li], 0)], [], [(_sds((T, F), BF16), 'tile')] * 2,
                                 up_epi, tm=tm2, tn=min(1024, F), comm=gather(('mlp_w_down', li)))
        W_down[li] = got_up[0].reshape(1, F, D)
        res = mm_nn(f"mlp_down_{li}", [(act, W_down[li], 0)],
                    [(xin, 'tile'), (row(mlp_g, li), 'row'), (row(mlp_b, li), 'row')],
                    res_ln_outs, res_ln_epi(alpha), tm=tm, tn=D, tk=F, comm=down_comm)
        (z, xo, xo_b), got_down = res if down_comm is not None else (res, ())
        return u, act, z, xo, xo_b, got_down

    def ple_fwd(li, xin, xin_b, with_loss, comm=None):
        def epi(accs, ex, out, i):
            pp, gg = accs
            xo = ex[0][...] + pp * _sigmoid(gg)
            out[1][...] = pp.astype(BF16)
            out[2][...] = gg.astype(BF16)
            if with_loss:
                err = xo - ex[1][...]
                out[0][...] = err * (1.0 / D)
                _init_or_add(out[3], i, jnp.sum(err * err, axis=0, keepdims=True) * (0.5 / D))
            else:
                out[0][...] = xo
                out[3][...] = xo.astype(BF16)
        extras = [(xin, 'tile')] + ([(tgt, 'tile')] if with_loss else [])
        outs = [(_sds((T, D), F32), 'tile'), (_sds((T, D), BF16), 'tile'), (_sds((T, D), BF16), 'tile')]
        outs.append((_sds((1, D), F32), 'rowacc') if with_loss else (_sds((T, D), BF16), 'tile'))
        return mm_nn(f"ple_{li}", [(p_b[li], W_proj[li], 0), (xin_b, W_gate[li], 0)], extras, outs, epi, tm=tm, tn=D,
                     comm=comm)

    assert D // N_DEV == LANES
    (g0, ha0, hg0), got = _glu(xb, W_in, b_in, T, D, tm,
                               gather(('conv_w_out', 0), ('ple_w_proj', 0), ('ple_w_gate', 0)))
    W_out = got[0].reshape(1, D, D)
    set_ple_weights(0, got[1], got[2])
    (c0, s0), (W_up[0],) = dwconv_fwd(g0, w_dw, b_dw, cln_g, cln_b, tm=tmc, comm=gather(('mlp_w_up', 0)))
    z1, x1, x1b = mm_nn("conv_out", [(s0, W_out, 0)],
                        [(b_out, 'row'), (x, 'tile'), (row(mix_g, 0), 'row'), (row(mix_b, 0), 'row')],
                        res_ln_outs, res_ln_epi(alpha), tm=tm, tn=D)
    u0, act0, z2, x2, x2b, got = mlp_fwd(
        0, x1, x1b, down_comm=gather(('attn_w_q', 0), ('kv_w_k', 0), ('kv_w_v', 0), ('attn_w_o', 0),
                                     ('ple_w_proj', 1), ('ple_w_gate', 1)))
    W_qkv = jnp.concatenate([got[0].reshape(D, HD), got[1].reshape(D, KVD), got[2].reshape(D, KVD)], axis=1)[None]
    W_o = got[3].reshape(1, HD, D)
    set_ple_weights(1, got[4], got[5])
    x3, pp0, gg0, x3b = ple_fwd(0, x2, x2b, False)

    def qkv_epi(accs, ex, out, i):
        t = accs[0]
        c, su, sd = ex[0][...], ex[1][...], ex[2][...]
        out[0][...] = _rope_wide(t[:, 0:HD], c, su, sd, _rope_chunk).astype(BF16)
        widen_kv(_rope_wide(t[:, HD:HD + KVD], c, su, sd, _rope_chunk), t[:, HD + KVD:], out[1], n_kv)
    NQ = HD + 2 * KVD
    q1, kvx1 = mm_nn("qkv_rope", [(x3b, W_qkv, 0)], [(t, 'tab') for t in rope],
                     [(_sds((T, HD), BF16), 'rows'), (_sds((T, 4 * n_kv * LANES), BF16), 'rows')], qkv_epi,
                     tm=tm, tn=NQ)
    (o1, lse1), (W_up[1],) = attn_fwd(q1, kvx1, a['attn_sinks'], n_heads, n_kv, comm=gather(('mlp_w_up', 1)))
    z3, x4, x4b = mm_nn("attn_out", [(o1, W_o, 0)], [(x3, 'tile'), (row(mix_g, 1), 'row'), (row(mix_b, 1), 'row')],
                        res_ln_outs, res_ln_epi(alpha), tm=tm, tn=D)
    u1, act1, z4, x5, x5b, _ = mlp_fwd(1, x4, x4b)
    dy, pp1, gg1, loss_row = ple_fwd(1, x5, x5b, True)
    loss_local = jnp.sum(loss_row)

    grads = {}

    def ln_bwd_epi(coef, with_colsum):
        def epi(acc, ex, out, i):
            d_x = acc + coef * ex[0][...]
            dz, dg, db = _ln_bwd(d_x, ex[1][...], ex[2][...])
            out[0][...] = dz
            out[1][...] = dz.astype(BF16)
            _init_or_add(out[2], i, dg)
            _init_or_add(out[3], i, db)
            if with_colsum:
                _init_or_add(out[4], i, jnp.sum(dz, axis=0, keepdims=True))
        return epi

    def ln_bwd_outs(with_colsum):
        outs = [(_sds((T, D), F32), 'tile'), (_sds((T, D), BF16), 'tile'), (_sds((1, D), F32), 'rowacc'),
                (_sds((1, D), F32), 'rowacc')]
        return outs + ([(_sds((1, D), F32), 'rowacc')] if with_colsum else [])

    def ple_bwd(li, d_out, xin, pp, gg, z_mlp, pair_specs=None, chip_keys=None):
        side = {}

        def gate_grads(ex):
            d = ex[0][...]
            sg = _sigmoid(ex[4][...].astype(F32))
            side['d_pp'] = (d * sg).astype(BF16)
            side['d_gg'] = (d * ex[3][...].astype(F32) * sg * (1.0 - sg)).astype(BF16)
            return side['d_gg']

        jobs = _Jobs([pair_stage(*pair_specs), chip_stage(*chip_keys)]) if pair_specs else None
        res = mm_nt(f"ple_dx_{li}", _sds((T, D), BF16), W_gate[li],
                    [(d_out, 'tile'), (z_mlp, 'tile'), (row(mlp_g, li), 'row'), (pp, 'tile'), (gg, 'tile')],
                    ln_bwd_outs(False), ln_bwd_epi(1.0, False), tm=tm, tko=D, tc=D, comm=jobs, a_pro=gate_grads,
                    dws=[(xin, lambda: side['d_gg']), (p_b[li], lambda: side['d_pp'])])
        if jobs is not None:
            res = split_hosted(res, jobs, chips=[1], pairs=[0])
        dz, dzb, dg, db, dw_gate, dw_proj = res
        grads[('mlp_ln_g', li)], grads[('mlp_ln_b', li)] = dg, db
        grads[('ple_w_gate', li)], grads[('ple_w_proj', li)] = dw_gate, dw_proj
        return dz, dzb

    recv = {}
    wqkv_cols = {'attn_w_q': (0, HD), 'kv_w_k': (HD, HD + KVD), 'kv_w_v': (HD + KVD, NQ)}

    def piece(name, li):
        if name == 'conv_w_in':
            return grads['conv_w_in']
        if name == 'mlp_w_up':
            return grads[('mlp_w_up', li)]
        if name == 'ple_w_proj':
            return jnp.transpose(grads[('ple_w_proj', li)][0].reshape(PLE, N_DEV, D // N_DEV), (1, 0, 2))
        if name in wqkv_cols:
            g = grads['w_qkv'][:, wqkv_cols[name][0]:wqkv_cols[name][1]]
        else:
            g = grads[name] if name in grads else grads[(name, li)]
            g = g[0]
        return g.reshape(N_DEV, g.shape[0] // N_DEV, g.shape[1])

    def pair_stage(*specs):
        job = _PairJob([piece(nm, li) for nm, li in specs])
        job.specs = specs
        return job

    sums = {}
    halved = ()

    def pair_done(pair_job, got):
        for (nm, li), mine, theirs in zip(pair_job.specs, pair_job.sources, got):
            if nm in halved:
                for h in (0, 1):
                    sums[(nm, li, h)] = chip_sum(f"chip_sum_{nm}_{li}_{h}", mine, theirs,
                                                 ta=min(256, mine.shape[1] // 2), part=h, parts=2)
            else:
                sums[(nm, li)] = chip_sum(f"chip_sum_{nm}_{li}", mine, theirs, ta=min(512, mine.shape[1]))

    def chip_stage(*keys):
        job = _ChipJob([sums[k] for k in keys])
        job.specs = keys
        return job

    def hosted(res, job):
        main, got = res
        for spec, r in zip(job.specs, got):
            recv[spec] = r
        return main

    def split_hosted(res, jobs, chips, pairs):
        main, got = res
        parts = jobs.split(got)
        for k in chips:
            hosted((None, parts[k]), jobs.jobs[k])
        for k in pairs:
            pair_done(jobs.jobs[k], parts[k])
        return main

    def mlp_bwd(li, dz, dzb, xin, u, act, z_mix, with_colsum, du_pairs, du_chips):
        jobs = _Jobs([pair_stage(*du_pairs)] + ([chip_stage(*du_chips)] if du_chips else []))
        du, grads[('mlp_w_up', li)], grads[('mlp_w_down', li)] = split_hosted(
            mlp_du_dw(f"mlp_du_{li}", dzb, W_down[li], u, act, xin, F // N_DEV, tm=tm2, tf=min(1024, F), comm=jobs),
            jobs, chips=[1] if du_chips else [], pairs=[0])
        pair = pair_stage(('mlp_w_up', li), ('mlp_w_down', li))
        res, got = mm_nt(f"mlp_dx_{li}", du, W_up[li], [(dz, 'tile'), (z_mix, 'tile'), (row(mix_g, li), 'row')],
                         ln_bwd_outs(with_colsum), ln_bwd_epi(alpha, with_colsum), tm=tm, tko=D, tc=F, comm=pair)
        pair_done(pair, got)
        grads[('mix_ln_g', li)], grads[('mix_ln_b', li)] = res[2], res[3]
        return res

    dz4, dz4b = ple_bwd(1, dy, x5b, pp1, gg1, z4)
    dz3, dz3b, _, _ = mlp_bwd(1, dz4, dz4b, x4b, u1, act1, z3, False, [('ple_w_gate', 1), ('ple_w_proj', 1)], None)

    def do_epi(acc, ex, out, i):
        out[0][...] = acc.astype(BF16)
    job = chip_stage(('ple_w_gate', 1), ('ple_w_proj', 1))
    do1, grads['attn_w_o'] = hosted(
        mm_nt("attn_do", dz3b, W_o, [], [(_sds((T, HD), BF16), 'tile')], do_epi, tm=tm, tko=HD, tc=D, comm=job,
              dws=[(o1, None)]), job)
    jobs = _Jobs([chip_stage(('mlp_w_up', 1)), pair_stage(('attn_w_o', 0))])
    d_qkv, dkv, d_sinks = split_hosted(
        attn_bwd(q1, kvx1, do1, lse1, a['attn_sinks'], rope, n_heads, n_kv, comm=jobs), jobs, chips=[0], pairs=[1])
    d_qkv = dkv_finish(d_qkv, dkv, rope, HD, KVD, tm=tm)

    def dx3_epi(acc, ex, out, i):
        out[0][...] = acc + alpha * ex[0][...]
    job = chip_stage(('attn_w_o', 0))
    dx3, dw_qkv = hosted(mm_nt("attn_dx", d_qkv, W_qkv, [(dz3, 'tile')], [(_sds((T, D), F32), 'tile')], dx3_epi,
                               tm=tm, tko=D, tc=NQ, comm=job, dws=[(x3b, None)]), job)
    grads['w_qkv'] = dw_qkv[0]

    dz2, dz2b = ple_bwd(0, dx3, x2b, pp0, gg0, z2, [('attn_w_q', 0), ('kv_w_k', 0), ('kv_w_v', 0)],
                        [('mlp_w_down', 1)])
    dz1, dz1b, _, _, db_out = mlp_bwd(0, dz2, dz2b, x1b, u0, act0, z1, True, [('ple_w_gate', 0), ('ple_w_proj', 0)],
                                      [('attn_w_q', 0), ('kv_w_k', 0), ('kv_w_v', 0)])

    def ds_epi(acc, ex, out, i):
        n = _ln_fwd(ex[0][...], ex[1][...], ex[2][...])
        sg = _sigmoid(n)
        dn = acc * (sg * (1.0 + n * (1.0 - sg)))
        dc, dg, db = _ln_bwd(dn, ex[0][...], ex[1][...])
        out[0][...] = dc
        _init_or_add(out[1], i, dg)
        _init_or_add(out[2], i, db)
    job = chip_stage(('ple_w_gate', 0), ('ple_w_proj', 0))
    dc0, d_cln_g, d_cln_b, grads['conv_w_out'] = hosted(
        mm_nt("conv_ds", dz1b, W_out, [(c0, 'tile'), (cln_g, 'row'), (cln_b, 'row')],
              [(_sds((T, D), F32), 'tile'), (_sds((1, D), F32), 'rowacc'), (_sds((1, D), F32), 'rowacc')], ds_epi,
              tm=tm, tko=D, tc=D, comm=job, dws=[(s0, None)]), job)
    job = chip_stage(('mlp_w_up', 0), ('mlp_w_down', 0))
    dh0, d_wdw, d_bdw, d_bin = hosted(dwconv_bwd(dc0, g0, ha0, hg0, w_dw, tm=tmc, comm=job), job)
    grads['conv_w_in'] = mm_tn("d_conv_win", xb, dh0, 2 * D // N_DEV, BF16, tm=T, tk=tw, tn=tw)

    pair = pair_stage(('conv_w_in', 0), ('conv_w_out', 0))
    pair_done(pair, comm_only("pair_last", [pair])[0])
    last_chip = chip_stage(('conv_w_in', 0), ('conv_w_out', 0))

    def own_rows(vec, rows_used, rows):
        arr = vec.reshape(N_DEV, rows_used, LANES)
        return jnp.pad(arr, ((0, 0), (0, rows - rows_used), (0, 0)))
    dwdw_dev = jnp.transpose(d_wdw.reshape(CONV_HALO, N_DEV, D // N_DEV), (1, 0, 2))
    lane_rows = D // N_DEV // LANES
    small_grad = jnp.concatenate([
        own_rows(d_bin, 2 * lane_rows, 8), dwdw_dev if lane_rows == 1 else dwdw_dev.reshape(N_DEV, -1, LANES),
        own_rows(d_bdw, lane_rows, 8), own_rows(d_cln_g, lane_rows, 8), own_rows(d_cln_b, lane_rows, 8),
        own_rows(db_out, lane_rows, 8)], axis=1)
    n_small = small_grad.shape[1]

    rep_local = {'mix_ln_g': jnp.concatenate([grads[('mix_ln_g', li)] for li in range(DEPTH)], axis=0),
                 'mix_ln_b': jnp.concatenate([grads[('mix_ln_b', li)] for li in range(DEPTH)], axis=0),
                 'mlp_ln_g': jnp.concatenate([grads[('mlp_ln_g', li)] for li in range(DEPTH)], axis=0),
                 'mlp_ln_b': jnp.concatenate([grads[('mlp_ln_b', li)] for li in range(DEPTH)], axis=0),
                 'attn_sinks': d_sinks}
    rep_grad = _rep_rows(rep_local, '')
    n_rep = rep_grad.shape[0]
    last = _Jobs([last_chip, _DirectJob([small_grad, jnp.broadcast_to(rep_grad[None], (N_DEV, n_rep, LANES))])])

    def dx_epi(acc, ex, out, i):
        out[0][...] = acc + alpha * ex[0][...]
    (grad_x,), got = mm_nt("conv_dx", dh0, W_in, [(dz1, 'tile')], [(_sds((T, D), F32), 'tile')], dx_epi,
                           tm=tm, tko=D, tc=D, comm=last)
    got, (recv_small, recv_rep) = last.split(got)
    hosted((None, got), last_chip)

    result = {}
    kinds = ('grad', 'delta', 'new_m', 'new_v')
    w, m, v = (_small_rows(a, pre)[None] for pre in ('', 'm_', 'v_'))
    for kind, arr in zip(kinds, adamw("adamw_small", [recv_small], w, m, v, ta=n_small)):
        for pname, val in _unpack_small(arr[0], a).items():
            result[(kind, pname)] = val
    w, m, v = (_rep_rows(a, pre)[None] for pre in ('', 'm_', 'v_'))
    for kind, arr in zip(kinds, adamw("adamw_rep", [recv_rep], w, m, v, ta=n_rep)):
        for pname, val in _unpack_rep(arr[0], a).items():
            result[(kind, pname)] = val
    for name in BIG_WEIGHTS:
        w, m, v = (shard3(a[pre + name]) for pre in ('', 'm_', 'v_'))
        recvs = [join_halves(recv[(name, li, 0)], recv[(name, li, 1)]) if name in halved else recv[(name, li)]
                 for li in range(w.shape[0])]
        for kind, arr in zip(kinds, adamw("adamw_" + name, recvs, w, m, v, ta=min(256, w.shape[1]))):
            result[(kind, name)] = arr.reshape(a[name].shape)

    loss = lax.psum(loss_local, ("x", "y", "c"))
    out = [loss, grad_x[None]]
    for kind in ('grad', 'delta', 'new_m', 'new_v'):
        out += [result[(kind, name)] for name in WEIGHT_NAMES]
    return tuple(out)


def _glu(x, W_in, b_in, T, D, tm, comm=None):
    n = W_in.shape[2]
    q = 2 if D // n % 2 == 0 else 1
    nt = D // (q * n)
    tn = q * n

    def body(x_ref, wa_ref, wg_ref, ba_ref, bg_ref, g_ref, ha_ref, hg_ref):
        xb = x_ref[...]
        ha = jnp.concatenate([_dot(xb, wa_ref[s], ((1,), (0,))) for s in range(q)], axis=1) + ba_ref[...]
        hg = jnp.concatenate([_dot(xb, wg_ref[s], ((1,), (0,))) for s in range(q)], axis=1) + bg_ref[...]
        g_ref[...] = ha * _sigmoid(hg)
        ha_ref[...] = ha.astype(ha_ref.dtype)
        hg_ref[...] = hg.astype(hg_ref.dtype)

    return _call(
        body, comm, name="conv_in_glu", grid=(T // tm, nt),
        in_specs=[pl.BlockSpec((tm, D), lambda i, j: (i, 0)),
                  pl.BlockSpec((q, D, n), lambda i, j: (j, 0, 0)),
                  pl.BlockSpec((q, D, n), lambda i, j: (j + nt, 0, 0)),
                  pl.BlockSpec((1, tn), lambda i, j: (0, j)), pl.BlockSpec((1, tn), lambda i, j: (0, j + nt))],
        out_specs=[pl.BlockSpec((tm, tn), lambda i, j: (i, j))] * 3,
        out_shape=[_sds((T, D), F32), _sds((T, D), BF16), _sds((T, D), BF16)],
        sem=("parallel", "parallel"), operands=[x, W_in, W_in, b_in, b_in])


def kernel(x, p, conv_w_in, conv_b_in, conv_w_dw, conv_b_dw, conv_ln_g, conv_ln_b, conv_w_out, conv_b_out, kv_w_k, kv_w_v, attn_w_q, attn_sinks, attn_w_o, mix_ln_g, mix_ln_b, mlp_w_up, mlp_w_down, mlp_ln_g, mlp_ln_b, ple_w_proj, ple_w_gate, loss_target, m_conv_w_in, m_conv_b_in, m_conv_w_dw, m_conv_b_dw, m_conv_ln_g, m_conv_ln_b, m_conv_w_out, m_conv_b_out, m_kv_w_k, m_kv_w_v, m_attn_w_q, m_attn_sinks, m_attn_w_o, m_mix_ln_g, m_mix_ln_b, m_mlp_w_up, m_mlp_w_down, m_mlp_ln_g, m_mlp_ln_b, m_ple_w_proj, m_ple_w_gate, v_conv_w_in, v_conv_b_in, v_conv_w_dw, v_conv_b_dw, v_conv_ln_g, v_conv_ln_b, v_conv_w_out, v_conv_b_out, v_kv_w_k, v_kv_w_v, v_attn_w_q, v_attn_sinks, v_attn_w_o, v_mix_ln_g, v_mix_ln_b, v_mlp_w_up, v_mlp_w_down, v_mlp_ln_g, v_mlp_ln_b, v_ple_w_proj, v_ple_w_gate):
    return _step(dict(locals()))
```

```python
import functools
import math

import jax
import jax.numpy as jnp
from jax import lax
from jax.experimental import pallas as pl
from jax.experimental.pallas import tpu as pltpu

F32 = jnp.float32
BF16 = jnp.bfloat16

N_DEV = 8
HEAD_DIM = 64
ROPE_DIM = HEAD_DIM // 4
ROPE_HALF = ROPE_DIM // 2
ROPE_THETA = 500000.0
ATT_BLOCK = 128
CONV_WIDTH = 31
CONV_HALO = 32
CONV_ROWS = 64
LN_EPS = 1e-5
DEPTH = 2
DEEPNORM_ALPHA = (2 * DEPTH) ** 0.25
MASK_VALUE = -1e30

ADAM_LR = 0.001
ADAM_B1 = 0.9
ADAM_B2 = 0.999
ADAM_EPS = 1e-08
ADAM_WD = 0.01
ADAM_STEP = 10

LANES = 128
SUBLANES = 8
VMEM_LIMIT_BYTES = 52 * 1024 * 1024
TOKEN_TILE = 512
MESH_ID = pl.DeviceIdType.MESH
RELAY_AT_TENTHS = 5
FORWARD_AT_TENTHS = 8

WEIGHT_NAMES = ['conv_w_in', 'conv_b_in', 'conv_w_dw', 'conv_b_dw', 'conv_ln_g', 'conv_ln_b', 'conv_w_out',
                'conv_b_out', 'kv_w_k', 'kv_w_v', 'attn_w_q', 'attn_sinks', 'attn_w_o', 'mix_ln_g', 'mix_ln_b',
                'mlp_w_up', 'mlp_w_down', 'mlp_ln_g', 'mlp_ln_b', 'ple_w_proj', 'ple_w_gate']
BIG_WEIGHTS = ['conv_w_in', 'conv_w_out', 'kv_w_k', 'kv_w_v', 'attn_w_q', 'attn_w_o', 'mlp_w_up', 'mlp_w_down',
               'ple_w_proj', 'ple_w_gate']
SMALL_SHARDED = [('conv_b_in', 8), ('conv_w_dw', 32), ('conv_b_dw', 8), ('conv_ln_g', 8), ('conv_ln_b', 8),
                 ('conv_b_out', 8)]
REPLICATED = ['mix_ln_g', 'mix_ln_b', 'mlp_ln_g', 'mlp_ln_b', 'attn_sinks']


def _params(sem):
    return pltpu.CompilerParams(dimension_semantics=sem, vmem_limit_bytes=VMEM_LIMIT_BYTES)


def _sds(shape, dtype):
    return jax.ShapeDtypeStruct(shape, dtype)


def _my_place():
    x, y, c = lax.axis_index("x"), lax.axis_index("y"), lax.axis_index("c")
    return x, y, c, 4 * x + 2 * y + c


def _peers(x, y, c):
    out = []
    for dx in (0, 1):
        for dy in (0, 1):
            for dc in (0, 1):
                if dx or dy or dc:
                    px, py, pc = x ^ dx, y ^ dy, c ^ dc
                    out.append(((px, py, pc), 4 * px + 2 * py + pc))
    return out


N_CHIPS = 4


def _other_chips(x, y):
    return [((x ^ dx, y ^ dy), 2 * (x ^ dx) + (y ^ dy)) for dx, dy in ((1, 0), (0, 1), (1, 1))]


def _remote(src, dst, send, recv, to):
    return pltpu.make_async_remote_copy(src_ref=src, dst_ref=dst, send_sem=send, recv_sem=recv, device_id=to,
                                        device_id_type=MESH_ID)


def _wait_slabs(buf, count, send, recv, me, sent=True, received=True):
    part = buf.at[pl.ds(0, count)]
    cp = _remote(part, part, send, recv, me)
    if sent:
        cp.wait_send()
    if received:
        cp.wait_recv()


class _DirectJob:
    n_sems = 3

    def __init__(self, items):
        self.sources = list(items)
        self.dests = [_sds(it.shape, it.dtype) for it in items]
        self.n = len(items)

    def start(self, src, dst, send, recv, loc):
        x, y, c, me = _my_place()
        for k in range(self.n):
            here = dst[k].at[pl.ds(me, 1)]
            pltpu.make_async_copy(src[k].at[pl.ds(me, 1)], here, loc.at[k]).start()
            for peer, idx in _peers(x, y, c):
                _remote(src[k].at[pl.ds(idx, 1)], here, send.at[k], recv.at[k], peer).start()

    def relay(self, *refs):
        pass

    def forward(self, *refs):
        pass

    def finish(self, src, dst, send, recv, loc):
        x, y, c, me = _my_place()
        for k in range(self.n):
            pltpu.make_async_copy(src[k].at[pl.ds(me, 1)], dst[k].at[pl.ds(me, 1)], loc.at[k]).wait()
            _wait_slabs(dst[k], N_DEV - 1, send.at[k], recv.at[k], (x, y, c))


class _GatherJob:
    n_sems = 7

    def __init__(self, items):
        self.sources = [it.reshape((1,) + it.shape) for it in items]
        self.dests = [_sds((N_DEV,) + it.shape, it.dtype) for it in items]
        self.n = len(items)

    @staticmethod
    def _neighbours(x, y, c):
        out = []
        for dx, dy in ((c, 1 - c), (1 - c, c), (1, 1)):
            px, py = x ^ dx, y ^ dy
            out.append(((px, py, c), 4 * px + 2 * py + c))
        return out

    def start(self, src, dst, send_a, recv_a, send_b, recv_b, send_d, recv_d, loc):
        x, y, c, me = _my_place()
        (first, _), (second, _), _ = self._neighbours(x, y, c)
        for k in range(self.n):
            here = dst[k].at[pl.ds(me, 1)]
            pltpu.make_async_copy(src[k], here, loc.at[k]).start()
            _remote(src[k], here, send_d.at[k], recv_d.at[k], (x, y, 1 - c)).start()
            _remote(src[k], here, send_a.at[k], recv_a.at[k], first).start()
            _remote(src[k], here, send_b.at[k], recv_b.at[k], second).start()

    def relay(self, src, dst, send_a, recv_a, send_b, recv_b, send_d, recv_d, loc):
        x, y, c, me = _my_place()
        (_, first_idx), (second, _), _ = self._neighbours(x, y, c)
        for k in range(self.n):
            _wait_slabs(dst[k], 1, send_a.at[k], recv_a.at[k], (x, y, c), sent=False)
            rows = dst[k].at[pl.ds(first_idx, 1)]
            _remote(rows, rows, send_b.at[k], recv_b.at[k], second).start()

    def forward(self, src, dst, send_a, recv_a, send_b, recv_b, send_d, recv_d, loc):
        x, y, c, me = _my_place()
        for k in range(self.n):
            _wait_slabs(dst[k], 2, send_b.at[k], recv_b.at[k], (x, y, c), sent=False)
            for _, idx in self._neighbours(x, y, c):
                rows = dst[k].at[pl.ds(idx, 1)]
                _remote(rows, rows, send_d.at[k], recv_d.at[k], (x, y, 1 - c)).start()

    def finish(self, src, dst, send_a, recv_a, send_b, recv_b, send_d, recv_d, loc):
        x, y, c, me = _my_place()
        for k in range(self.n):
            pltpu.make_async_copy(src[k], dst[k].at[pl.ds(me, 1)], loc.at[k]).wait()
            _wait_slabs(dst[k], 1, send_a.at[k], recv_a.at[k], (x, y, c), received=False)
            _wait_slabs(dst[k], 2, send_b.at[k], recv_b.at[k], (x, y, c), received=False)
            _wait_slabs(dst[k], N_CHIPS, send_d.at[k], recv_d.at[k], (x, y, c))


class _PairJob:
    n_sems = 2

    def __init__(self, items):
        self.sources = list(items)
        self.dests = [_sds((N_CHIPS,) + it.shape[1:], it.dtype) for it in items]
        self.n = len(items)

    def start(self, src, dst, send, recv):
        x, y, c, me = _my_place()
        for k in range(self.n):
            for q in range(N_CHIPS):
                _remote(src[k].at[pl.ds(2 * q + 1 - c, 1)], dst[k].at[pl.ds(q, 1)], send.at[k], recv.at[k],
                        (x, y, 1 - c)).start()

    def relay(self, *refs):
        pass

    def forward(self, *refs):
        pass

    def finish(self, src, dst, send, recv):
        x, y, c, me = _my_place()
        for k in range(self.n):
            _wait_slabs(dst[k], N_CHIPS, send.at[k], recv.at[k], (x, y, c))


class _ChipJob:
    n_sems = 3

    def __init__(self, items):
        self.sources = list(items)
        self.dests = [_sds(it.shape, it.dtype) for it in items]
        self.n = len(items)

    def start(self, src, dst, send, recv, loc):
        x, y, c, me = _my_place()
        mine = 2 * x + y
        for k in range(self.n):
            here = dst[k].at[pl.ds(mine, 1)]
            pltpu.make_async_copy(src[k].at[pl.ds(mine, 1)], here, loc.at[k]).start()
            for (px, py), q in _other_chips(x, y):
                _remote(src[k].at[pl.ds(q, 1)], here, send.at[k], recv.at[k], (px, py, c)).start()

    def relay(self, *refs):
        pass

    def forward(self, *refs):
        pass

    def finish(self, src, dst, send, recv, loc):
        x, y, c, me = _my_place()
        mine = 2 * x + y
        for k in range(self.n):
            pltpu.make_async_copy(src[k].at[pl.ds(mine, 1)], dst[k].at[pl.ds(mine, 1)], loc.at[k]).wait()
            _wait_slabs(dst[k], N_CHIPS - 1, send.at[k], recv.at[k], (x, y, c))


class _Jobs:
    def __init__(self, jobs):
        self.jobs = list(jobs)
        self.sources = [s for job in self.jobs for s in job.sources]
        self.dests = [d for job in self.jobs for d in job.dests]
        self.n = len(self.sources)

    def _stage(self, stage, src, dst, sems):
        k0 = s0 = 0
        for job in self.jobs:
            getattr(job, stage)(src[k0:k0 + job.n], dst[k0:k0 + job.n], *sems[s0:s0 + job.n_sems])
            k0 += job.n
            s0 += job.n_sems

    def start(self, src, dst, *sems):
        self._stage('start', src, dst, sems)

    def relay(self, src, dst, *sems):
        self._stage('relay', src, dst, sems)

    def forward(self, src, dst, *sems):
        self._stage('forward', src, dst, sems)

    def finish(self, src, dst, *sems):
        self._stage('finish', src, dst, sems)

    def split(self, got):
        out, k0 = [], 0
        for job in self.jobs:
            out.append(got[k0:k0 + job.n])
            k0 += job.n
        return out


def _job_sems(job):
    if isinstance(job, _Jobs):
        return [s for part in job.jobs for s in _job_sems(part)]
    return [pltpu.SemaphoreType.DMA((job.n,))] * job.n_sems


def _call(body, comm, *, name, grid, in_specs, out_specs, out_shape, operands, sem, scratch_shapes=(), aliases=None):
    single = not isinstance(out_shape, (list, tuple))
    out_shape = [out_shape] if single else list(out_shape)
    out_specs = [out_specs] if single else list(out_specs)
    if comm is None:
        res = pl.pallas_call(body, name=name, grid=grid, in_specs=list(in_specs), out_specs=out_specs,
                             out_shape=out_shape, scratch_shapes=list(scratch_shapes),
                             input_output_aliases=aliases or {}, compiler_params=_params(sem))(*operands)
        return res[0] if single else res
    n_in, n_out, n_scr, n_c = len(in_specs), len(out_shape), len(scratch_shapes), comm.n
    any_spec = pl.BlockSpec(memory_space=pl.ANY)
    steps = math.prod(grid)
    early = min(steps - 1, (steps * RELAY_AT_TENTHS) // 10)
    mid = min(steps - 1, (steps * FORWARD_AT_TENTHS) // 10)

    def hosted(*refs):
        ins, c_src = refs[:n_in], refs[n_in:n_in + n_c]
        outs = refs[n_in + n_c:n_in + n_c + n_out]
        c_dst = refs[n_in + n_c + n_out:n_in + 2 * n_c + n_out]
        scr = refs[n_in + 2 * n_c + n_out:n_in + 2 * n_c + n_out + n_scr]
        sems = refs[n_in + 2 * n_c + n_out + n_scr:]
        step = pl.program_id(0)
        for d in range(1, len(grid)):
            step = step * grid[d] + pl.program_id(d)

        @pl.when(step == 0)
        def _():
            comm.start(c_src, c_dst, *sems)

        @pl.when(step == early)
        def _():
            comm.relay(c_src, c_dst, *sems)

        @pl.when(step == mid)
        def _():
            comm.forward(c_src, c_dst, *sems)

        body(*ins, *outs, *scr)

        @pl.when(step == steps - 1)
        def _():
            comm.finish(c_src, c_dst, *sems)

    res = pl.pallas_call(hosted, name=name, grid=grid, in_specs=list(in_specs) + [any_spec] * n_c,
                         out_specs=out_specs + [any_spec] * n_c, out_shape=out_shape + comm.dests,
                         scratch_shapes=list(scratch_shapes) + _job_sems(comm), input_output_aliases=aliases or {},
                         compiler_params=_params(("arbitrary",) * len(grid)))(*operands, *comm.sources)
    main = res[:n_out]
    return (main[0] if single else main), res[n_out:]


def comm_only(name, jobs):
    any_spec = pl.BlockSpec(memory_space=pl.ANY)
    n_all = sum(job.n for job in jobs)

    def body(*refs):
        srcs, dsts, sems = refs[:n_all], refs[n_all:2 * n_all], refs[2 * n_all:]
        parts, k0, s0 = [], 0, 0
        for job in jobs:
            parts.append((job, srcs[k0:k0 + job.n], dsts[k0:k0 + job.n], sems[s0:s0 + job.n_sems]))
            k0 += job.n
            s0 += job.n_sems
        for stage in ('start', 'relay', 'forward', 'finish'):
            for job, src, dst, sem in parts:
                getattr(job, stage)(src, dst, *sem)

    res = pl.pallas_call(body, name=name, in_specs=[any_spec] * n_all, out_specs=[any_spec] * n_all,
                         out_shape=[d for job in jobs for d in job.dests],
                         scratch_shapes=[s for job in jobs for s in _job_sems(job)],
                         )(*[s for job in jobs for s in job.sources])
    out, k0 = [], 0
    for job in jobs:
        out.append(res[k0:k0 + job.n])
        k0 += job.n
    return out


def _dot(a, b, dims):
    if a.dtype != BF16:
        a = a.astype(BF16)
    if b.dtype != BF16:
        b = b.astype(BF16)
    return lax.dot_general(a, b, (dims, ((), ())), preferred_element_type=F32)


def _sigmoid(v):
    return 1.0 / (1.0 + jnp.exp(-v))


def _sq_relu(u):
    return jnp.square(jnp.maximum(u.astype(F32), 0.0)).astype(BF16)


def _ln_stats(z):
    mu = jnp.mean(z, axis=-1, keepdims=True)
    zc = z - mu
    var = jnp.mean(zc * zc, axis=-1, keepdims=True)
    return zc * lax.rsqrt(var + LN_EPS)


def _ln_fwd(z, g, b):
    return _ln_stats(z) * g + b


def _ln_bwd(dy, z, g):
    xhat = _ln_stats(z)
    mu = jnp.mean(z, axis=-1, keepdims=True)
    zc = z - mu
    rstd = lax.rsqrt(jnp.mean(zc * zc, axis=-1, keepdims=True) + LN_EPS)
    dxh = dy * g
    m1 = jnp.mean(dxh, axis=-1, keepdims=True)
    m2 = jnp.mean(dxh * xhat, axis=-1, keepdims=True)
    dz = rstd * (dxh - m1 - xhat * m2)
    return dz, jnp.sum(dy * xhat, axis=0, keepdims=True), jnp.sum(dy, axis=0, keepdims=True)


def _extra_spec(shape, kind, tm, tn, ij):
    if kind == 'tile':
        return pl.BlockSpec((tm, tn), lambda *g: (ij(g)[0], ij(g)[1]))
    if kind in ('row', 'rowacc'):
        return pl.BlockSpec((1, tn), lambda *g: (0, ij(g)[1]))
    if kind == 'tab':
        return pl.BlockSpec((tm, LANES), lambda *g: (ij(g)[0], 0))
    if kind == 'rows':
        return pl.BlockSpec((tm, shape[1]), lambda *g: (ij(g)[0], 0))
    raise ValueError(kind)


def mm_nn(name, pairs, extras, outs, epi, *, tm, tn, tk=None, comm=None, a_pro=None):
    M = pairs[0][0].shape[0]
    N = pairs[0][1].shape[0] * pairs[0][1].shape[2]
    n_pairs = len(pairs)
    K0 = pairs[0][0].shape[1]
    tk = K0 if tk is None else tk
    nk = K0 // tk
    assert nk == 1 or n_pairs == 1
    assert M % tm == 0 and N % tn == 0 and K0 % tk == 0
    has_rowacc = any(kind == 'rowacc' for _, kind in outs)
    assert not has_rowacc or (N == tn and nk == 1)
    in_specs, operands, slabs = [], [], []
    for a, b, off in pairs:
        K = a.shape[1]
        ktile = K if n_pairs > 1 else tk
        n = b.shape[2]
        assert b.shape[1] == K
        in_specs.append(pl.BlockSpec((tm, ktile), lambda i, j, k: (i, k)))
        if tn <= n:
            assert n % tn == 0
            r = n // tn
            in_specs.append(pl.BlockSpec((None, ktile, tn),
                                         lambda i, j, k, r=r, off=off: ((j + off) // r, k, (j + off) % r)))
            slabs.append(0)
        else:
            assert tn % n == 0
            in_specs.append(pl.BlockSpec((tn // n, ktile, n), lambda i, j, k, off=off: (j + off, k, 0)))
            slabs.append(tn // n)
        operands += [a, b]
    ij = lambda g: (g[0], g[1])
    for arr, kind in extras:
        in_specs.append(_extra_spec(arr.shape, kind, tm, tn, ij))
        operands.append(arr)
    out_specs = [_extra_spec(o.shape, kind, tm, tn, ij) for o, kind in outs]
    n_ex, n_out = len(extras), len(outs)

    def pair_dot(ab, q):
        a = ab[2 * q][...]
        if a_pro is not None:
            a = a_pro(a)
        if not slabs[q]:
            return _dot(a, ab[2 * q + 1][...], ((1,), (0,)))
        return jnp.concatenate([_dot(a, ab[2 * q + 1][s], ((1,), (0,))) for s in range(slabs[q])], axis=1)

    def body(*refs):
        ab = refs[:2 * n_pairs]
        ex = refs[2 * n_pairs:2 * n_pairs + n_ex]
        out = refs[2 * n_pairs + n_ex:2 * n_pairs + n_ex + n_out]
        i = pl.program_id(0)
        if nk == 1:
            accs = [pair_dot(ab, q) for q in range(n_pairs)]
            epi(accs, ex, out, i)
        else:
            acc_ref = refs[-1]
            k = pl.program_id(2)

            @pl.when(k == 0)
            def _():
                acc_ref[...] = jnp.zeros_like(acc_ref)

            acc_ref[...] += pair_dot(ab, 0)

            @pl.when(k == nk - 1)
            def _():
                epi([acc_ref[...]], ex, out, i)

    scratch = [pltpu.VMEM((tm, tn), F32)] if nk > 1 else []
    sem = ("arbitrary",) * 3 if has_rowacc else ("parallel", "parallel", "arbitrary")
    return _call(body, comm, name=name, grid=(M // tm, N // tn, nk), in_specs=in_specs, out_specs=out_specs,
                 out_shape=[o for o, _ in outs], scratch_shapes=scratch, sem=sem, operands=operands)


def mm_nt(name, a, b, extras, outs, epi, *, tm, tko, tc, comm=None, a_pro=None, dws=()):
    M, N = a.shape
    J, K, n = b.shape
    assert J * n == N and M % tm == 0 and K % tko == 0 and N % tc == 0
    nc = N // tc
    assert a_pro is None or (nc == 1 and tc <= n and K == tko == N)
    assert not dws or (nc == 1 and K == tko and tc <= n)
    n_dw, nt = len(dws), M // tm
    has_rowacc = any(kind == 'rowacc' for _, kind in outs)
    assert not has_rowacc or K == tko
    if tc <= n:
        assert n % tc == 0
        r = n // tc
        slabs = 0
        b_spec = pl.BlockSpec((None, tko, tc), lambda i, j, c: (c // r, j, c % r))
    else:
        assert tc % n == 0
        slabs = tc // n
        b_spec = pl.BlockSpec((slabs, tko, n), lambda i, j, c: (c, j, 0))
    n_ab = 2 if a_pro is None else 1
    in_specs = ([pl.BlockSpec((tm, tc), lambda i, j, c: (i, c))] if a_pro is None else []) + [b_spec]

    def nt_dot(a_ref, b_ref):
        if not slabs:
            return _dot(a_ref[...], b_ref[...], ((1,), (1,)))
        acc = _dot(a_ref[:, 0:n], b_ref[0], ((1,), (1,)))
        for s in range(1, slabs):
            acc = acc + _dot(a_ref[:, s * n:(s + 1) * n], b_ref[s], ((1,), (1,)))
        return acc

    ij = lambda g: (g[0], g[1])
    operands = [a, b] if a_pro is None else [b]
    for arr, kind in extras:
        in_specs.append(_extra_spec(arr.shape, kind, tm, tko, ij))
        operands.append(arr)
    out_specs = [_extra_spec(o.shape, kind, tm, tko, ij) for o, kind in outs]
    n_ex, n_out = len(extras), len(outs)
    out_shape = [o for o, _ in outs]
    for lhs, _ in dws:
        in_specs.append(pl.BlockSpec((tm, lhs.shape[1]), lambda i, j, c: (i, 0)))
        operands.append(lhs)
        out_specs.append(pl.BlockSpec((None, lhs.shape[1], N), lambda i, j, c: (0, 0, 0)))
        out_shape.append(_sds((1, lhs.shape[1], N), BF16))

    def weight_grads(refs, a_tile, i):
        lhs_refs = refs[n_ab + n_ex:n_ab + n_ex + n_dw]
        dw_refs = refs[n_ab + n_ex + n_dw + n_out:n_ab + n_ex + n_dw + n_out + n_dw]
        acc_refs = refs[len(refs) - n_dw:]
        for (_, rhs_fn), lhs_ref, dw_ref, acc_ref in zip(dws, lhs_refs, dw_refs, acc_refs):
            part = _dot(lhs_ref[...], a_tile if rhs_fn is None else rhs_fn(), ((0,), (0,)))
            _init_or_add(acc_ref, i, part)

            @pl.when(i == nt - 1)
            def _(dw_ref=dw_ref, acc_ref=acc_ref):
                dw_ref[...] = acc_ref[...].astype(dw_ref.dtype)

    def body(*refs):
        ex = refs[n_ab:n_ab + n_ex]
        out = refs[n_ab + n_ex + n_dw:n_ab + n_ex + n_dw + n_out]
        i = pl.program_id(0)
        if a_pro is not None:
            a_tile = a_pro(ex)
            epi(_dot(a_tile, refs[0][...], ((1,), (1,))), ex, out, i)
            weight_grads(refs, a_tile, i)
            return
        a_ref, b_ref = refs[:2]
        if nc == 1:
            epi(nt_dot(a_ref, b_ref), ex, out, i)
            if dws:
                weight_grads(refs, a_ref[...], i)
        else:
            acc_ref = refs[-1]
            c = pl.program_id(2)

            @pl.when(c == 0)
            def _():
                acc_ref[...] = jnp.zeros_like(acc_ref)

            acc_ref[...] += nt_dot(a_ref, b_ref)

            @pl.when(c == nc - 1)
            def _():
                epi(acc_ref[...], ex, out, i)

    scratch = ([pltpu.VMEM((tm, tko), F32)] if nc > 1 else []) + [pltpu.VMEM((lhs.shape[1], N), F32) for lhs, _ in dws]
    sem = ("arbitrary",) * 3 if (has_rowacc or dws) else ("parallel", "parallel", "arbitrary")
    return _call(body, comm, name=name, grid=(M // tm, K // tko, nc), in_specs=in_specs, out_specs=out_specs,
                 out_shape=out_shape, scratch_shapes=scratch, sem=sem, operands=operands)


def mm_tn(name, a, d, n, out_dtype, *, tm, tk, tn, comm=None):
    M, K = a.shape
    N = d.shape[1]
    assert d.shape[0] == M and N % n == 0 and N % tn == 0 and K % tk == 0 and M % tm == 0
    nm = M // tm
    if tn <= n:
        assert n % tn == 0
        r = n // tn
        slabs = 0
        o_spec = pl.BlockSpec((None, tk, tn), lambda kk, j, m: (j // r, kk, j % r))
    else:
        assert tn % n == 0
        slabs = tn // n
        o_spec = pl.BlockSpec((slabs, tk, n), lambda kk, j, m: (j, kk, 0))

    def write(o_ref, acc):
        if not slabs:
            o_ref[...] = acc.astype(o_ref.dtype)
        else:
            for s in range(slabs):
                o_ref[s] = acc[:, s * n:(s + 1) * n].astype(o_ref.dtype)

    def body(a_ref, d_ref, o_ref, *scratch):
        if nm == 1:
            write(o_ref, _dot(a_ref[...], d_ref[...], ((0,), (0,))))
            return
        acc_ref, = scratch
        m = pl.program_id(2)

        @pl.when(m == 0)
        def _():
            acc_ref[...] = jnp.zeros_like(acc_ref)

        acc_ref[...] += _dot(a_ref[...], d_ref[...], ((0,), (0,)))

        @pl.when(m == nm - 1)
        def _():
            write(o_ref, acc_ref[...])

    return _call(
        body, comm, name=name, grid=(K // tk, N // tn, nm),
        in_specs=[pl.BlockSpec((tm, tk), lambda kk, j, m: (m, kk)), pl.BlockSpec((tm, tn), lambda kk, j, m: (m, j))],
        out_specs=o_spec, out_shape=_sds((N // n, K, n), out_dtype),
        scratch_shapes=[pltpu.VMEM((tk, tn), F32)] if nm > 1 else [],
        sem=("parallel", "parallel", "arbitrary"), operands=[a, d])


def mlp_du_dw(name, dzb, w_down, u, xin, n, *, tm, tf, comm=None):
    T, D = dzb.shape
    F = u.shape[1]
    assert T % tm == 0 and F % tf == 0 and tf % n == 0
    slabs, nt = tf // n, T // tm

    def body(dz_ref, w_ref, u_ref, x_ref, du_ref, dwu_ref, dwd_ref, up_acc, down_acc):
        i = pl.program_id(1)
        dz = dz_ref[...]
        da = _dot(dz, w_ref[...], ((1,), (1,)))
        relu = jnp.maximum(u_ref[...].astype(F32), 0.0)
        du = (da * (2.0 * relu)).astype(BF16)
        du_ref[...] = du
        _init_or_add(up_acc, i, _dot(x_ref[...], du, ((0,), (0,))))
        _init_or_add(down_acc, i, _dot(jnp.square(relu).astype(BF16), dz, ((0,), (0,))))

        @pl.when(i == nt - 1)
        def _():
            for s in range(slabs):
                dwu_ref[s] = up_acc[:, s * n:(s + 1) * n].astype(dwu_ref.dtype)
            dwd_ref[...] = down_acc[...].astype(dwd_ref.dtype)

    hidden = pl.BlockSpec((tm, tf), lambda j, i: (i, j))
    tokens = pl.BlockSpec((tm, D), lambda j, i: (i, 0))
    return _call(
        body, comm, name=name, grid=(F // tf, nt),
        in_specs=[tokens, pl.BlockSpec((None, tf, D), lambda j, i: (0, j, 0)), hidden, tokens],
        out_specs=[hidden, pl.BlockSpec((slabs, D, n), lambda j, i: (j, 0, 0)),
                   pl.BlockSpec((None, tf, D), lambda j, i: (0, j, 0))],
        out_shape=[_sds((T, F), BF16), _sds((F // n, D, n), BF16), _sds((1, F, D), BF16)],
        scratch_shapes=[pltpu.VMEM((D, tf), F32), pltpu.VMEM((tf, D), F32)], sem=("arbitrary", "arbitrary"),
        operands=[dzb, w_down, u, xin])


def _init_or_add(ref, i, value):
    @pl.when(i == 0)
    def _():
        ref[...] = value

    @pl.when(i > 0)
    def _():
        ref[...] += value


def _rope_tables(T):
    pos = jnp.arange(T, dtype=F32)
    inv_freq = ROPE_THETA ** (-jnp.arange(0, ROPE_DIM, 2, dtype=F32) / ROPE_DIM)
    ang = pos[:, None] * inv_freq[None, :]
    cos, sin = jnp.cos(ang), jnp.sin(ang)
    ones = jnp.ones((T, HEAD_DIM - ROPE_DIM), F32)
    zeros = jnp.zeros((T, HEAD_DIM - ROPE_DIM), F32)
    zh = jnp.zeros((T, ROPE_HALF), F32)
    c_head = jnp.concatenate([cos, cos, ones], axis=1)
    s_up = jnp.concatenate([-sin, zh, zeros], axis=1)
    s_dn = jnp.concatenate([zh, sin, zeros], axis=1)
    rep = LANES // HEAD_DIM
    return tuple(jnp.tile(t, (1, rep)) for t in (c_head, s_up, s_dn))


def _rope_chunk(t, c, s_up, s_dn):
    return t * c + pltpu.roll(t, LANES - ROPE_HALF, 1) * s_up + pltpu.roll(t, ROPE_HALF, 1) * s_dn


def _rope_chunk_bwd(d, c, s_up, s_dn):
    return d * c + pltpu.roll(d * s_up, ROPE_HALF, 1) + pltpu.roll(d * s_dn, LANES - ROPE_HALF, 1)


def _rope_wide(t, c, s_up, s_dn, fn):
    chunks = [fn(t[:, q * LANES:(q + 1) * LANES], c, s_up, s_dn) for q in range(t.shape[1] // LANES)]
    return chunks[0] if len(chunks) == 1 else jnp.concatenate(chunks, axis=1)


def _taps_by_residue(first):
    groups = {}
    for o in range(first, first + CONV_WIDTH):
        groups.setdefault(o % SUBLANES, []).append(o)
    return sorted(groups.items())


def _shifted_taps(win_ref, span_ref, r0, res, offs, rb, ls):
    if res == 0:
        return [functools.partial(lambda o: win_ref[pl.ds(r0 + o, rb), ls], o) for o in offs]
    n = rb + offs[-1] - res
    span_ref[0:n, :] = win_ref[pl.ds(r0 + res, n), ls]
    return [functools.partial(lambda o: span_ref[pl.ds(o - res, rb), :], o) for o in offs]


def _halo_before_spec(tm, D):
    return pl.BlockSpec((CONV_HALO, D), lambda i: (jnp.maximum(i * (tm // CONV_HALO) - 1, 0), 0))


def dwconv_fwd(g, w_dw, b_dw, ln_g, ln_b, *, tm, comm=None):
    T, D = g.shape
    nl = D // LANES

    rb = min(CONV_ROWS, tm)

    def body(g_ref, gh_ref, w_ref, b_ref, lg_ref, lb_ref, c_ref, s_ref, win_ref, span_ref):
        i = pl.program_id(0)
        win_ref[0:CONV_HALO, :] = jnp.where(i > 0, gh_ref[...], 0.0)
        win_ref[CONV_HALO:, :] = g_ref[...]

        def lane_chunk(q, carry):
            ls = pl.ds(pl.multiple_of(q * LANES, LANES), LANES)
            for r0 in range(0, tm, rb):
                acc = jnp.broadcast_to(b_ref[:, ls], (rb, LANES))
                for res, offs in _taps_by_residue(CONV_HALO - (CONV_WIDTH - 1)):
                    taps = _shifted_taps(win_ref, span_ref, r0, res, offs, rb, ls)
                    for o, tap in zip(offs, taps):
                        k = o - (CONV_HALO - (CONV_WIDTH - 1))
                        acc = acc + tap() * w_ref[k:k + 1, ls]
                c_ref[pl.ds(r0, rb), ls] = acc
            return carry

        lax.fori_loop(0, nl, lane_chunk, 0)
        n = _ln_fwd(c_ref[...], lg_ref[...], lb_ref[...])
        s_ref[...] = (n * _sigmoid(n)).astype(s_ref.dtype)

    row = pl.BlockSpec((1, D), lambda i: (0, 0))
    return _call(
        body, comm, name="dwconv_fwd", grid=(T // tm,),
        in_specs=[pl.BlockSpec((tm, D), lambda i: (i, 0)), _halo_before_spec(tm, D),
                  pl.BlockSpec((CONV_HALO, D), lambda i: (0, 0)), row, row, row],
        out_specs=[pl.BlockSpec((tm, D), lambda i: (i, 0)), pl.BlockSpec((tm, D), lambda i: (i, 0))],
        out_shape=[_sds((T, D), F32), _sds((T, D), BF16)],
        scratch_shapes=[pltpu.VMEM((tm + CONV_HALO, D), F32), pltpu.VMEM((rb + CONV_HALO, LANES), F32)],
        sem=("parallel",), operands=[g, g, w_dw, b_dw, ln_g, ln_b])


def dwconv_bwd(dc, g, ha, hg, w_dw, *, tm, comm=None):
    T, D = g.shape
    nl = D // LANES
    last = T // CONV_HALO - 1
    nt = T // tm

    rb = min(CONV_ROWS, tm)

    def body(dc_ref, dcn_ref, g_ref, gh_ref, ha_ref, hg_ref, w_ref, dh_ref, dw_ref, dbdw_ref, dbin_ref,
             win_ref, dwin_ref, dg_ref, dwp_ref, span_ref):
        i = pl.program_id(0)
        win_ref[0:CONV_HALO, :] = jnp.where(i > 0, gh_ref[...], 0.0)
        win_ref[CONV_HALO:, :] = g_ref[...]
        dwin_ref[0:tm, :] = dc_ref[...]
        dwin_ref[tm:, :] = jnp.where(i < nt - 1, dcn_ref[...], 0.0)

        @pl.when(i == 0)
        def _():
            dwp_ref[...] = jnp.zeros_like(dwp_ref)

        first = CONV_HALO - (CONV_WIDTH - 1)

        def lane_chunk(q, carry):
            ls = pl.ds(pl.multiple_of(q * LANES, LANES), LANES)
            for r0 in range(0, tm, rb):
                acc = jnp.zeros((rb, LANES), F32)
                for res, offs in _taps_by_residue(0):
                    taps = _shifted_taps(dwin_ref, span_ref, r0, res, offs, rb, ls)
                    for o, tap in zip(offs, taps):
                        k = CONV_WIDTH - 1 - o
                        acc = acc + tap() * w_ref[k:k + 1, ls]
                dg_ref[pl.ds(r0, rb), ls] = acc
                dcv = dwin_ref[pl.ds(r0, rb), ls]
                for res, offs in _taps_by_residue(first):
                    taps = _shifted_taps(win_ref, span_ref, r0, res, offs, rb, ls)
                    for o, tap in zip(offs, taps):
                        k = o - first
                        prod = dcv * tap()
                        dwp_ref[k, :, ls] += jnp.sum(prod.reshape(rb // SUBLANES, SUBLANES, LANES), axis=0)
            return carry

        lax.fori_loop(0, nl, lane_chunk, 0)

        @pl.when(i == nt - 1)
        def _():
            for k in range(CONV_WIDTH):
                dw_ref[k:k + 1, :] = jnp.sum(dwp_ref[k], axis=0, keepdims=True)
            dw_ref[CONV_WIDTH:, :] = jnp.zeros((CONV_HALO - CONV_WIDTH, D), F32)
        dg = dg_ref[...]
        ha = ha_ref[...].astype(F32)
        sg = _sigmoid(hg_ref[...].astype(F32))
        d_ha = dg * sg
        d_hg = dg * ha * sg * (1.0 - sg)
        dh_ref[:, 0:D] = d_ha.astype(dh_ref.dtype)
        dh_ref[:, D:] = d_hg.astype(dh_ref.dtype)
        _init_or_add(dbdw_ref, i, jnp.sum(dc_ref[...], axis=0, keepdims=True))
        _init_or_add(dbin_ref, i, jnp.concatenate([jnp.sum(d_ha, axis=0, keepdims=True),
                                                   jnp.sum(d_hg, axis=0, keepdims=True)], axis=1))

    tile = pl.BlockSpec((tm, D), lambda i: (i, 0))
    return _call(
        body, comm, name="dwconv_bwd", grid=(nt,),
        in_specs=[tile,
                  pl.BlockSpec((CONV_HALO, D), lambda i: (jnp.minimum((i + 1) * (tm // CONV_HALO), last), 0)),
                  tile, _halo_before_spec(tm, D),
                  tile, tile, pl.BlockSpec((CONV_HALO, D), lambda i: (0, 0))],
        out_specs=[pl.BlockSpec((tm, 2 * D), lambda i: (i, 0)), pl.BlockSpec((CONV_HALO, D), lambda i: (0, 0)),
                   pl.BlockSpec((1, D), lambda i: (0, 0)), pl.BlockSpec((1, 2 * D), lambda i: (0, 0))],
        out_shape=[_sds((T, 2 * D), BF16), _sds((CONV_HALO, D), F32), _sds((1, D), F32), _sds((1, 2 * D), F32)],
        scratch_shapes=[pltpu.VMEM((tm + CONV_HALO, D), F32), pltpu.VMEM((tm + CONV_HALO, D), F32),
                        pltpu.VMEM((tm, D), F32), pltpu.VMEM((CONV_WIDTH, SUBLANES, D), F32),
                        pltpu.VMEM((rb + CONV_HALO, LANES), F32)],
        sem=("arbitrary",), operands=[dc, dc, g, g, ha, hg, w_dw])


def _attn_specs(HD, W):
    B = ATT_BLOCK
    return [pl.BlockSpec((B, HD), lambda n: (n, 0)),
            pl.BlockSpec((B, 4 * W), lambda n: (n, 0)),
            pl.BlockSpec((B, 4 * W), lambda n: (jnp.maximum(n - 1, 0), 0))]


def _band_mask(n):
    r = lax.broadcasted_iota(jnp.int32, (ATT_BLOCK, 2 * ATT_BLOCK), 0)
    j = lax.broadcasted_iota(jnp.int32, (ATT_BLOCK, 2 * ATT_BLOCK), 1)
    return (j > r) & (j <= r + ATT_BLOCK) & ((n > 0) | (j >= ATT_BLOCK))


def _band(kvc_ref, kvp_ref, part, g, parity, W):
    lanes = slice((2 * part + parity) * W + g * LANES, (2 * part + parity) * W + (g + 1) * LANES)
    return jnp.concatenate([kvp_ref[:, lanes], kvc_ref[:, lanes]], axis=0)


def _half_mask(parity):
    lane = lax.broadcasted_iota(jnp.int32, (1, LANES), 1)
    return (lane < HEAD_DIM) if parity == 0 else (lane >= HEAD_DIM)


def widen_kv(k, v, out_ref, n_kv):
    W = n_kv * LANES
    low = _half_mask(0)
    for part, src in enumerate((k, v)):
        for cg in range(n_kv * HEAD_DIM // LANES):
            chunk = src[:, cg * LANES:(cg + 1) * LANES]
            swapped = pltpu.roll(chunk, HEAD_DIM, 1)
            for g, lo, hi in ((2 * cg, chunk, swapped), (2 * cg + 1, swapped, chunk)):
                base = 2 * part * W + g * LANES
                out_ref[:, base:base + LANES] = jnp.where(low, lo, 0.0).astype(out_ref.dtype)
                out_ref[:, base + W:base + W + LANES] = jnp.where(low, 0.0, hi).astype(out_ref.dtype)


def attn_fwd(q, kvx, sinks, n_heads, n_kv, comm=None):
    T, HD = q.shape
    W = n_kv * LANES
    B = ATT_BLOCK
    chunks_per_group = n_heads // n_kv // 2
    scale = 1.0 / math.sqrt(HEAD_DIM)

    def body(q_ref, kvc_ref, kvp_ref, sink_ref, o_ref, lse_ref):
        n = pl.program_id(0)
        mask = jnp.tile(_band_mask(n), (n_heads, 1))
        s = jnp.concatenate(
            [_dot(q_ref[:, (h // 2) * LANES:(h // 2 + 1) * LANES],
                  _band(kvc_ref, kvp_ref, 0, h // 2 // chunks_per_group, h % 2, W), ((1,), (1,)))
             for h in range(n_heads)], axis=0)
        sink = jnp.concatenate([jnp.broadcast_to(sink_ref[:, h:h + 1], (B, 1)) for h in range(n_heads)], axis=0)
        s = jnp.where(mask, s * scale, MASK_VALUE)
        m = jnp.maximum(jnp.max(s, axis=-1, keepdims=True), sink)
        e = jnp.exp(s - m)
        total = _dot(e, jnp.ones((2 * B, LANES), BF16), ((1,), (0,))) + jnp.exp(sink - m)
        lse = m + jnp.log(total[:, 0:1])
        inv = 1.0 / total
        probs = (e * jnp.concatenate([inv, inv], axis=1)).astype(BF16)
        for c in range(n_heads // 2):
            g = c // chunks_per_group
            out = (_dot(probs[2 * c * B:(2 * c + 1) * B], _band(kvc_ref, kvp_ref, 1, g, 0, W), ((1,), (0,)))
                   + _dot(probs[(2 * c + 1) * B:(2 * c + 2) * B], _band(kvc_ref, kvp_ref, 1, g, 1, W), ((1,), (0,))))
            o_ref[:, c * LANES:(c + 1) * LANES] = out.astype(o_ref.dtype)
        lse_ref[...] = jnp.concatenate([lse[h * B:(h + 1) * B] for h in range(n_heads)], axis=1)

    return _call(
        body, comm, name="attn_fwd", grid=(T // B,),
        in_specs=_attn_specs(HD, W) + [pl.BlockSpec((1, n_heads), lambda n: (0, 0))],
        out_specs=[pl.BlockSpec((B, HD), lambda n: (n, 0)), pl.BlockSpec((B, n_heads), lambda n: (n, 0))],
        out_shape=[_sds((T, HD), BF16), _sds((T, n_heads), F32)],
        sem=("parallel",), operands=[q, kvx, kvx, sinks])


def attn_bwd(q, kvx, do, lse, sinks, rope, n_heads, n_kv, comm=None):
    T, HD = q.shape
    KVD = n_kv * HEAD_DIM
    W = n_kv * LANES
    B = ATT_BLOCK
    chunks_per_group = n_heads // n_kv // 2
    scale = 1.0 / math.sqrt(HEAD_DIM)
    nb = T // B

    def body(q_ref, kvc_ref, kvp_ref, do_ref, lse_ref, sink_ref, c_ref, su_ref, sd_ref, dq_ref, dkv_ref, dsink_ref):
        n = pl.program_id(0)

        @pl.when(n == 0)
        def _():
            dkv_ref[...] = jnp.zeros_like(dkv_ref)
            dsink_ref[...] = jnp.zeros_like(dsink_ref)

        def chunk(ref, h):
            return ref[:, (h // 2) * LANES:(h // 2 + 1) * LANES]

        def band(part, h):
            return _band(kvc_ref, kvp_ref, part, h // 2 // chunks_per_group, h % 2, W)

        def stack(per_head):
            return jnp.concatenate([per_head(h) for h in range(n_heads)], axis=0)

        mask = jnp.tile(_band_mask(n), (n_heads, 1))
        s = stack(lambda h: _dot(chunk(q_ref, h), band(0, h), ((1,), (1,))))
        dp = stack(lambda h: _dot(chunk(do_ref, h), band(1, h), ((1,), (1,))))
        lse = stack(lambda h: lse_ref[:, h:h + 1])
        sink = stack(lambda h: jnp.broadcast_to(sink_ref[:, h:h + 1], (B, 1)))
        probs = jnp.exp(jnp.where(mask, s * scale, MASK_VALUE) - lse)
        delta = jnp.sum(probs * dp, axis=-1, keepdims=True)
        ds = (probs * (dp - delta) * scale).astype(BF16)
        probs = probs.astype(BF16)
        sink_term = jnp.exp(sink - lse) * delta
        dsk = [-jnp.sum(sink_term[h * B:(h + 1) * B], axis=0, keepdims=True) for h in range(n_heads)]

        dk_wide, dv_wide = [None] * n_kv, [None] * n_kv
        for c in range(n_heads // 2):
            g = c // chunks_per_group
            dq2 = None
            for h in (2 * c, 2 * c + 1):
                half = _half_mask(h % 2)
                q2, do2 = chunk(q_ref, h), chunk(do_ref, h)
                ds_h, p_h = ds[h * B:(h + 1) * B], probs[h * B:(h + 1) * B]
                part = _dot(ds_h, band(0, h), ((1,), (0,)))
                dq2 = part if dq2 is None else dq2 + part
                dk_h = _dot(ds_h, jnp.where(half, q2, jnp.zeros_like(q2)), ((0,), (0,)))
                dv_h = _dot(p_h, jnp.where(half, do2, jnp.zeros_like(do2)), ((0,), (0,)))
                dk_wide[g] = dk_h if dk_wide[g] is None else dk_wide[g] + dk_h
                dv_wide[g] = dv_h if dv_wide[g] is None else dv_wide[g] + dv_h
            dq_ref[:, c * LANES:(c + 1) * LANES] = _rope_chunk_bwd(
                dq2, c_ref[...], su_ref[...], sd_ref[...]).astype(dq_ref.dtype)

        def fold(wide):
            low = _half_mask(0)
            both = [w + pltpu.roll(w, HEAD_DIM, 1) for w in wide]
            return jnp.concatenate([jnp.where(low, both[2 * cg], both[2 * cg + 1]) for cg in range(n_kv // 2)], axis=1)

        dkv = jnp.concatenate([fold(dk_wide), fold(dv_wide)], axis=1)
        prev = pl.ds(pl.multiple_of(jnp.maximum(n - 1, 0) * B, B), B)
        cur = pl.ds(pl.multiple_of(n * B, B), B)
        dkv_ref[prev, :] += dkv[0:B, :]
        dkv_ref[cur, :] += dkv[B:, :]
        dsink_ref[...] += jnp.concatenate(dsk, axis=1)

    tab = pl.BlockSpec((B, LANES), lambda n: (n, 0))
    return _call(
        body, comm, name="attn_bwd", grid=(nb,),
        in_specs=_attn_specs(HD, W) + [pl.BlockSpec((B, HD), lambda n: (n, 0)),
                                       pl.BlockSpec((B, n_heads), lambda n: (n, 0)),
                                       pl.BlockSpec((1, n_heads), lambda n: (0, 0)), tab, tab, tab],
        out_specs=[pl.BlockSpec((B, HD), lambda n: (n, 0)), pl.BlockSpec((T, 2 * KVD), lambda n: (0, 0)),
                   pl.BlockSpec((1, n_heads), lambda n: (0, 0))],
        out_shape=[_sds((T, HD + 2 * KVD), BF16), _sds((T, 2 * KVD), F32), _sds((1, n_heads), F32)],
        sem=("arbitrary",), operands=[q, kvx, kvx, do, lse, sinks, *rope])


def dkv_finish(d_qkv, dkv, rope, HD, KVD, *, tm):
    T = dkv.shape[0]
    kv_col = HD // (2 * KVD)

    def body(alias_ref, dkv_ref, c_ref, su_ref, sd_ref, o_ref):
        del alias_ref
        dk = _rope_wide(dkv_ref[:, 0:KVD], c_ref[...], su_ref[...], sd_ref[...], _rope_chunk_bwd)
        o_ref[:, 0:KVD] = dk.astype(o_ref.dtype)
        o_ref[:, KVD:] = dkv_ref[:, KVD:].astype(o_ref.dtype)

    tab = pl.BlockSpec((tm, LANES), lambda i: (i, 0))
    return pl.pallas_call(
        body, name="dkv_finish", grid=(T // tm,),
        in_specs=[pl.BlockSpec(memory_space=pl.ANY), pl.BlockSpec((tm, 2 * KVD), lambda i: (i, 0)), tab, tab, tab],
        out_specs=pl.BlockSpec((tm, 2 * KVD), lambda i: (i, kv_col)),
        out_shape=_sds(d_qkv.shape, d_qkv.dtype), input_output_aliases={0: 0},
        compiler_params=_params(("parallel",)))(d_qkv, dkv, *rope)


def ple_bwd_elem(d_out, pp, gg, *, tm):
    T, D = d_out.shape

    def body(d_ref, pp_ref, gg_ref, dpp_ref, dgg_ref):
        d = d_ref[...]
        sg = _sigmoid(gg_ref[...].astype(F32))
        dpp_ref[...] = (d * sg).astype(dpp_ref.dtype)
        dgg_ref[...] = (d * pp_ref[...].astype(F32) * sg * (1.0 - sg)).astype(dgg_ref.dtype)

    tile = pl.BlockSpec((tm, D), lambda i: (i, 0))
    return pl.pallas_call(
        body, name="ple_bwd_elem", grid=(T // tm,), in_specs=[tile, tile, tile], out_specs=[tile, tile],
        out_shape=[_sds((T, D), BF16), _sds((T, D), BF16)], compiler_params=_params(("parallel",)))(d_out, pp, gg)


def chip_sum(name, g, p_sib, *, ta, part=0, parts=1):
    _, a, b = g.shape
    rows = a // parts
    assert a % parts == 0 and rows % ta == 0
    first = part * (rows // ta)

    def body(core_ref, g_ref, p_ref, o_ref):
        del core_ref
        o_ref[...] = (g_ref[...].astype(F32) + p_ref[...].astype(F32)).astype(o_ref.dtype)

    my_core = lax.axis_index("c").astype(jnp.int32).reshape(1)
    return pl.pallas_call(
        body, name=name, out_shape=_sds((N_CHIPS, rows, b), g.dtype),
        grid_spec=pltpu.PrefetchScalarGridSpec(
            num_scalar_prefetch=1, grid=(N_CHIPS, rows // ta),
            in_specs=[pl.BlockSpec((None, None, ta, b), lambda q, i, core: (q, core[0], first + i, 0)),
                      pl.BlockSpec((None, ta, b), lambda q, i, core: (q, first + i, 0))],
            out_specs=pl.BlockSpec((None, ta, b), lambda q, i, core: (q, i, 0))),
        compiler_params=_params(("arbitrary", "arbitrary")))(my_core, g.reshape(N_CHIPS, 2, a, b), p_sib)


def adamw(name, recvs, w, m, v, *, ta):
    L, a, b = w.shape
    n_terms = recvs[0].shape[0]
    assert a % ta == 0 and len(recvs) == L
    c1 = 1.0 - ADAM_B1 ** ADAM_STEP
    c2 = 1.0 - ADAM_B2 ** ADAM_STEP

    def body(*refs):
        r_refs = refs[:L]
        w_ref, m_ref, v_ref, g_ref, d_ref, nm_ref, nv_ref = refs[L:]
        layer = pl.program_id(0)
        for l in range(L):
            @pl.when(layer == l)
            def _(r_ref=r_refs[l]):
                g = r_ref[0].astype(F32)
                for s in range(1, n_terms):
                    g = g + r_ref[s].astype(F32)
                nm = ADAM_B1 * m_ref[...] + (1.0 - ADAM_B1) * g
                nv = ADAM_B2 * v_ref[...] + (1.0 - ADAM_B2) * jnp.square(g)
                m_hat = nm / c1
                v_hat = nv / c2
                g_ref[...] = g
                d_ref[...] = -ADAM_LR * (m_hat / (jnp.sqrt(v_hat) + ADAM_EPS) + ADAM_WD * w_ref[...])
                nm_ref[...] = nm
                nv_ref[...] = nv

    blk = pl.BlockSpec((None, ta, b), lambda l, i: (l, i, 0))
    out = _sds((L, a, b), F32)
    r_specs = [pl.BlockSpec((n_terms, ta, b), lambda l, i, ll=ll: (0, jnp.where(l == ll, i, 0), 0)) for ll in range(L)]
    return pl.pallas_call(
        body, name=name, grid=(L, a // ta), in_specs=r_specs + [blk, blk, blk],
        out_specs=[blk, blk, blk, blk], out_shape=[out, out, out, out],
        compiler_params=_params(("arbitrary", "arbitrary")))(*recvs, w, m, v)


def _pack_rows(parts):
    out = []
    for arr, rows in parts:
        arr = arr.reshape(-1, LANES).astype(F32)
        out.append(jnp.pad(arr, ((0, rows - arr.shape[0]), (0, 0))))
    return jnp.concatenate(out, axis=0)


def _small_rows(a, prefix):
    return _pack_rows([(a[prefix + name], rows) for name, rows in SMALL_SHARDED])


def _unpack_small(packed, a):
    out, r0 = {}, 0
    for name, rows in SMALL_SHARDED:
        shape = a[name].shape
        used = math.prod(shape) // LANES
        out[name] = packed[r0:r0 + used].reshape(shape)
        r0 += rows
    return out


def _rep_rows(a, prefix):
    parts = []
    for name in REPLICATED:
        arr = a[prefix + name]
        if arr.size % LANES:
            arr = jnp.pad(arr.reshape(1, -1), ((0, 0), (0, LANES - arr.size % LANES)))
        rows = -(-arr.size // LANES)
        parts.append((arr, -(-rows // SUBLANES) * SUBLANES))
    return _pack_rows(parts)


def _unpack_rep(packed, a):
    out, r0 = {}, 0
    for name in REPLICATED:
        shape = a[name].shape
        size = math.prod(shape)
        rows = -(-size // LANES)
        out[name] = packed[r0:r0 + rows].reshape(-1)[:size].reshape(shape)
        r0 += -(-rows // SUBLANES) * SUBLANES
    return out


def _step(a):
    x = a['x'][0]
    T, D = x.shape
    tgt = a['loss_target'][0]
    p_in = [a['p'][i, 0] for i in range(DEPTH)]
    PLE = p_in[0].shape[1]
    n_heads = a['attn_sinks'].shape[1]
    HD = n_heads * HEAD_DIM
    KVD = a['kv_w_k'].shape[1]
    n_kv = KVD // HEAD_DIM
    F = a['mlp_w_down'].shape[1] * N_DEV
    tm = min(TOKEN_TILE, T)
    tm2 = min(2 * TOKEN_TILE, T)
    tmc = min(TOKEN_TILE, T)
    tw = 512
    alpha = DEEPNORM_ALPHA
    xb = x.astype(BF16)
    p_b = [p.astype(BF16) for p in p_in]

    def shard3(w):
        return w.reshape((1,) + w.shape) if w.ndim == 2 else w

    def gather(*specs):
        items = []
        for spec in specs:
            w = shard3(a[spec[0]])[spec[1]]
            if len(spec) == 3:
                rows = w.shape[0] // 2
                w = w[spec[2] * rows:(spec[2] + 1) * rows]
            items.append(w.astype(BF16))
        return _GatherJob(items)

    def join_halves(lo, hi):
        return jnp.concatenate([lo, hi], axis=1)

    (W_in, small_full), = comm_only("gather_first",
                                    [_GatherJob([a['conv_w_in'][0].astype(BF16), _small_rows(a, '')])])
    r0, small = 0, {}
    for name, rows in SMALL_SHARDED:
        small[name] = small_full[:, r0:r0 + rows]
        r0 += rows
    b_in = small['conv_b_in'][:, 0:2 * D // N_DEV // LANES].reshape(1, 2 * D)
    w_dw = jnp.transpose(small['conv_w_dw'], (1, 0, 2)).reshape(CONV_HALO, D)
    b_dw, cln_g, cln_b, b_out = (small[nm][:, 0].reshape(1, D) for nm in
                                 ('conv_b_dw', 'conv_ln_g', 'conv_ln_b', 'conv_b_out'))
    W_up, W_down, W_proj, W_gate = {}, {}, {}, {}
    mix_g, mix_b, mlp_g, mlp_b = a['mix_ln_g'], a['mix_ln_b'], a['mlp_ln_g'], a['mlp_ln_b']
    rope = _rope_tables(T)

    def set_ple_weights(li, g_proj, g_gate):
        W_proj[li] = jnp.transpose(g_proj, (1, 0, 2)).reshape(1, PLE, D)
        W_gate[li] = g_gate.reshape(1, D, D)

    def row(v, i):
        return v[i:i + 1]

    def res_ln_epi(coef):
        def epi(accs, ex, out, i):
            acc = accs[0] if isinstance(accs, list) else accs
            n_ex = len(ex)
            res_ref, g_ref, b_ref = ex[n_ex - 3], ex[n_ex - 2], ex[n_ex - 1]
            z = coef * res_ref[...] + acc
            if n_ex == 4:
                z = z + ex[0][...]
            out[0][...] = z
            xo = _ln_fwd(z, g_ref[...], b_ref[...])
            out[1][...] = xo
            out[2][...] = xo.astype(BF16)
        return epi

    res_ln_outs = [(_sds((T, D), F32), 'tile'), (_sds((T, D), F32), 'tile'), (_sds((T, D), BF16), 'tile')]

    def mlp_fwd(li, xin, xin_b, down_comm=None):
        def up_epi(accs, ex, out, i):
            out[0][...] = accs[0].astype(BF16)
        (u,), got_up = mm_nn(f"mlp_up_{li}", [(xin_b, W_up[li], 0)], [], [(_sds((T, F), BF16), 'tile')],
                             up_epi, tm=tm2, tn=min(1024, F), comm=gather(('mlp_w_down', li)))
        W_down[li] = got_up[0].reshape(1, F, D)
        res = mm_nn(f"mlp_down_{li}", [(u, W_down[li], 0)],
                    [(xin, 'tile'), (row(mlp_g, li), 'row'), (row(mlp_b, li), 'row')],
                    res_ln_outs, res_ln_epi(alpha), tm=tm, tn=D, tk=F, comm=down_comm, a_pro=_sq_relu)
        (z, xo, xo_b), got_down = res if down_comm is not None else (res, ())
        return u, z, xo, xo_b, got_down

    def ple_fwd(li, xin, xin_b, with_loss, comm=None):
        def epi(accs, ex, out, i):
            pp, gg = accs
            xo = ex[0][...] + pp * _sigmoid(gg)
            out[1][...] = pp.astype(BF16)
            out[2][...] = gg.astype(BF16)
            if with_loss:
                err = xo - ex[1][...]
                out[0][...] = err * (1.0 / D)
                _init_or_add(out[3], i, jnp.sum(err * err, axis=0, keepdims=True) * (0.5 / D))
            else:
                out[0][...] = xo
                out[3][...] = xo.astype(BF16)
        extras = [(xin, 'tile')] + ([(tgt, 'tile')] if with_loss else [])
        outs = [(_sds((T, D), F32), 'tile'), (_sds((T, D), BF16), 'tile'), (_sds((T, D), BF16), 'tile')]
        outs.append((_sds((1, D), F32), 'rowacc') if with_loss else (_sds((T, D), BF16), 'tile'))
        return mm_nn(f"ple_{li}", [(p_b[li], W_proj[li], 0), (xin_b, W_gate[li], 0)], extras, outs, epi, tm=tm, tn=D,
                     comm=comm)

    assert D // N_DEV == LANES
    (g0, ha0, hg0), got = _glu(xb, W_in, b_in, T, D, tm,
                               gather(('conv_w_out', 0), ('ple_w_proj', 0), ('ple_w_gate', 0)))
    W_out = got[0].reshape(1, D, D)
    set_ple_weights(0, got[1], got[2])
    (c0, s0), (W_up[0],) = dwconv_fwd(g0, w_dw, b_dw, cln_g, cln_b, tm=tmc, comm=gather(('mlp_w_up', 0)))
    z1, x1, x1b = mm_nn("conv_out", [(s0, W_out, 0)],
                        [(b_out, 'row'), (x, 'tile'), (row(mix_g, 0), 'row'), (row(mix_b, 0), 'row')],
                        res_ln_outs, res_ln_epi(alpha), tm=tm, tn=D)
    u0, z2, x2, x2b, got = mlp_fwd(
        0, x1, x1b, down_comm=gather(('attn_w_q', 0), ('kv_w_k', 0), ('kv_w_v', 0), ('attn_w_o', 0),
                                     ('ple_w_proj', 1), ('ple_w_gate', 1)))
    W_qkv = jnp.concatenate([got[0].reshape(D, HD), got[1].reshape(D, KVD), got[2].reshape(D, KVD)], axis=1)[None]
    W_o = got[3].reshape(1, HD, D)
    set_ple_weights(1, got[4], got[5])
    x3, pp0, gg0, x3b = ple_fwd(0, x2, x2b, False)

    def qkv_epi(accs, ex, out, i):
        t = accs[0]
        c, su, sd = ex[0][...], ex[1][...], ex[2][...]
        out[0][...] = _rope_wide(t[:, 0:HD], c, su, sd, _rope_chunk).astype(BF16)
        widen_kv(_rope_wide(t[:, HD:HD + KVD], c, su, sd, _rope_chunk), t[:, HD + KVD:], out[1], n_kv)
    NQ = HD + 2 * KVD
    q1, kvx1 = mm_nn("qkv_rope", [(x3b, W_qkv, 0)], [(t, 'tab') for t in rope],
                     [(_sds((T, HD), BF16), 'rows'), (_sds((T, 4 * n_kv * LANES), BF16), 'rows')], qkv_epi,
                     tm=tm, tn=NQ)
    (o1, lse1), (W_up[1],) = attn_fwd(q1, kvx1, a['attn_sinks'], n_heads, n_kv, comm=gather(('mlp_w_up', 1)))
    z3, x4, x4b = mm_nn("attn_out", [(o1, W_o, 0)], [(x3, 'tile'), (row(mix_g, 1), 'row'), (row(mix_b, 1), 'row')],
                        res_ln_outs, res_ln_epi(alpha), tm=tm, tn=D)
    u1, z4, x5, x5b, _ = mlp_fwd(1, x4, x4b)
    dy, pp1, gg1, loss_row = ple_fwd(1, x5, x5b, True)
    loss_local = jnp.sum(loss_row)

    grads = {}

    def ln_bwd_epi(coef, with_colsum):
        def epi(acc, ex, out, i):
            d_x = acc + coef * ex[0][...]
            dz, dg, db = _ln_bwd(d_x, ex[1][...], ex[2][...])
            out[0][...] = dz
            out[1][...] = dz.astype(BF16)
            _init_or_add(out[2], i, dg)
            _init_or_add(out[3], i, db)
            if with_colsum:
                _init_or_add(out[4], i, jnp.sum(dz, axis=0, keepdims=True))
        return epi

    def ln_bwd_outs(with_colsum):
        outs = [(_sds((T, D), F32), 'tile'), (_sds((T, D), BF16), 'tile'), (_sds((1, D), F32), 'rowacc'),
                (_sds((1, D), F32), 'rowacc')]
        return outs + ([(_sds((1, D), F32), 'rowacc')] if with_colsum else [])

    def ple_bwd(li, d_out, xin, pp, gg, z_mlp, pair_specs=None, chip_keys=None):
        side = {}

        def gate_grads(ex):
            d = ex[0][...]
            sg = _sigmoid(ex[4][...].astype(F32))
            side['d_pp'] = (d * sg).astype(BF16)
            side['d_gg'] = (d * ex[3][...].astype(F32) * sg * (1.0 - sg)).astype(BF16)
            return side['d_gg']

        jobs = _Jobs([pair_stage(*pair_specs), chip_stage(*chip_keys)]) if pair_specs else None
        res = mm_nt(f"ple_dx_{li}", _sds((T, D), BF16), W_gate[li],
                    [(d_out, 'tile'), (z_mlp, 'tile'), (row(mlp_g, li), 'row'), (pp, 'tile'), (gg, 'tile')],
                    ln_bwd_outs(False), ln_bwd_epi(1.0, False), tm=tm, tko=D, tc=D, comm=jobs, a_pro=gate_grads,
                    dws=[(xin, lambda: side['d_gg']), (p_b[li], lambda: side['d_pp'])])
        if jobs is not None:
            res = split_hosted(res, jobs, chips=[1], pairs=[0])
        dz, dzb, dg, db, dw_gate, dw_proj = res
        grads[('mlp_ln_g', li)], grads[('mlp_ln_b', li)] = dg, db
        grads[('ple_w_gate', li)], grads[('ple_w_proj', li)] = dw_gate, dw_proj
        return dz, dzb

    recv = {}
    wqkv_cols = {'attn_w_q': (0, HD), 'kv_w_k': (HD, HD + KVD), 'kv_w_v': (HD + KVD, NQ)}

    def piece(name, li):
        if name == 'conv_w_in':
            return grads['conv_w_in']
        if name == 'mlp_w_up':
            return grads[('mlp_w_up', li)]
        if name == 'ple_w_proj':
            return jnp.transpose(grads[('ple_w_proj', li)][0].reshape(PLE, N_DEV, D // N_DEV), (1, 0, 2))
        if name in wqkv_cols:
            g = grads['w_qkv'][:, wqkv_cols[name][0]:wqkv_cols[name][1]]
        else:
            g = grads[name] if name in grads else grads[(name, li)]
            g = g[0]
        return g.reshape(N_DEV, g.shape[0] // N_DEV, g.shape[1])

    def pair_stage(*specs):
        job = _PairJob([piece(nm, li) for nm, li in specs])
        job.specs = specs
        return job

    sums = {}
    halved = ()

    def pair_done(pair_job, got):
        for (nm, li), mine, theirs in zip(pair_job.specs, pair_job.sources, got):
            if nm in halved:
                for h in (0, 1):
                    sums[(nm, li, h)] = chip_sum(f"chip_sum_{nm}_{li}_{h}", mine, theirs,
                                                 ta=min(256, mine.shape[1] // 2), part=h, parts=2)
            else:
                sums[(nm, li)] = chip_sum(f"chip_sum_{nm}_{li}", mine, theirs, ta=min(512, mine.shape[1]))

    def chip_stage(*keys):
        job = _ChipJob([sums[k] for k in keys])
        job.specs = keys
        return job

    def hosted(res, job):
        main, got = res
        for spec, r in zip(job.specs, got):
            recv[spec] = r
        return main

    def split_hosted(res, jobs, chips, pairs):
        main, got = res
        parts = jobs.split(got)
        for k in chips:
            hosted((None, parts[k]), jobs.jobs[k])
        for k in pairs:
            pair_done(jobs.jobs[k], parts[k])
        return main

    def mlp_bwd(li, dz, dzb, xin, u, z_mix, with_colsum, du_pairs, du_chips):
        jobs = _Jobs([pair_stage(*du_pairs)] + ([chip_stage(*du_chips)] if du_chips else []))
        du, grads[('mlp_w_up', li)], grads[('mlp_w_down', li)] = split_hosted(
            mlp_du_dw(f"mlp_du_{li}", dzb, W_down[li], u, xin, F // N_DEV, tm=tm2, tf=min(1024, F), comm=jobs),
            jobs, chips=[1] if du_chips else [], pairs=[0])
        pair = pair_stage(('mlp_w_up', li), ('mlp_w_down', li))
        res, got = mm_nt(f"mlp_dx_{li}", du, W_up[li], [(dz, 'tile'), (z_mix, 'tile'), (row(mix_g, li), 'row')],
                         ln_bwd_outs(with_colsum), ln_bwd_epi(alpha, with_colsum), tm=tm, tko=D, tc=F, comm=pair)
        pair_done(pair, got)
        grads[('mix_ln_g', li)], grads[('mix_ln_b', li)] = res[2], res[3]
        return res

    dz4, dz4b = ple_bwd(1, dy, x5b, pp1, gg1, z4)
    dz3, dz3b, _, _ = mlp_bwd(1, dz4, dz4b, x4b, u1, z3, False, [('ple_w_gate', 1), ('ple_w_proj', 1)], None)

    def do_epi(acc, ex, out, i):
        out[0][...] = acc.astype(BF16)
    job = chip_stage(('ple_w_gate', 1), ('ple_w_proj', 1))
    do1, grads['attn_w_o'] = hosted(
        mm_nt("attn_do", dz3b, W_o, [], [(_sds((T, HD), BF16), 'tile')], do_epi, tm=tm, tko=HD, tc=D, comm=job,
              dws=[(o1, None)]), job)
    jobs = _Jobs([chip_stage(('mlp_w_up', 1)), pair_stage(('attn_w_o', 0))])
    d_qkv, dkv, d_sinks = split_hosted(
        attn_bwd(q1, kvx1, do1, lse1, a['attn_sinks'], rope, n_heads, n_kv, comm=jobs), jobs, chips=[0], pairs=[1])
    d_qkv = dkv_finish(d_qkv, dkv, rope, HD, KVD, tm=tm)

    def dx3_epi(acc, ex, out, i):
        out[0][...] = acc + alpha * ex[0][...]
    job = chip_stage(('attn_w_o', 0))
    dx3, dw_qkv = hosted(mm_nt("attn_dx", d_qkv, W_qkv, [(dz3, 'tile')], [(_sds((T, D), F32), 'tile')], dx3_epi,
                               tm=tm, tko=D, tc=NQ, comm=job, dws=[(x3b, None)]), job)
    grads['w_qkv'] = dw_qkv[0]

    dz2, dz2b = ple_bwd(0, dx3, x2b, pp0, gg0, z2, [('attn_w_q', 0), ('kv_w_k', 0), ('kv_w_v', 0)],
                        [('mlp_w_down', 1)])
    dz1, dz1b, _, _, db_out = mlp_bwd(0, dz2, dz2b, x1b, u0, z1, True, [('ple_w_gate', 0), ('ple_w_proj', 0)],
                                      [('attn_w_q', 0), ('kv_w_k', 0), ('kv_w_v', 0)])

    def ds_epi(acc, ex, out, i):
        n = _ln_fwd(ex[0][...], ex[1][...], ex[2][...])
        sg = _sigmoid(n)
        dn = acc * (sg * (1.0 + n * (1.0 - sg)))
        dc, dg, db = _ln_bwd(dn, ex[0][...], ex[1][...])
        out[0][...] = dc
        _init_or_add(out[1], i, dg)
        _init_or_add(out[2], i, db)
    job = chip_stage(('ple_w_gate', 0), ('ple_w_proj', 0))
    dc0, d_cln_g, d_cln_b, grads['conv_w_out'] = hosted(
        mm_nt("conv_ds", dz1b, W_out, [(c0, 'tile'), (cln_g, 'row'), (cln_b, 'row')],
              [(_sds((T, D), F32), 'tile'), (_sds((1, D), F32), 'rowacc'), (_sds((1, D), F32), 'rowacc')], ds_epi,
              tm=tm, tko=D, tc=D, comm=job, dws=[(s0, None)]), job)
    job = chip_stage(('mlp_w_up', 0), ('mlp_w_down', 0))
    dh0, d_wdw, d_bdw, d_bin = hosted(dwconv_bwd(dc0, g0, ha0, hg0, w_dw, tm=tmc, comm=job), job)
    grads['conv_w_in'] = mm_tn("d_conv_win", xb, dh0, 2 * D // N_DEV, BF16, tm=T, tk=tw, tn=tw)

    pair = pair_stage(('conv_w_in', 0), ('conv_w_out', 0))
    pair_done(pair, comm_only("pair_last", [pair])[0])
    last_chip = chip_stage(('conv_w_in', 0), ('conv_w_out', 0))

    def own_rows(vec, rows_used, rows):
        arr = vec.reshape(N_DEV, rows_used, LANES)
        return jnp.pad(arr, ((0, 0), (0, rows - rows_used), (0, 0)))
    dwdw_dev = jnp.transpose(d_wdw.reshape(CONV_HALO, N_DEV, D // N_DEV), (1, 0, 2))
    lane_rows = D // N_DEV // LANES
    small_grad = jnp.concatenate([
        own_rows(d_bin, 2 * lane_rows, 8), dwdw_dev if lane_rows == 1 else dwdw_dev.reshape(N_DEV, -1, LANES),
        own_rows(d_bdw, lane_rows, 8), own_rows(d_cln_g, lane_rows, 8), own_rows(d_cln_b, lane_rows, 8),
        own_rows(db_out, lane_rows, 8)], axis=1)
    n_small = small_grad.shape[1]

    rep_local = {'mix_ln_g': jnp.concatenate([grads[('mix_ln_g', li)] for li in range(DEPTH)], axis=0),
                 'mix_ln_b': jnp.concatenate([grads[('mix_ln_b', li)] for li in range(DEPTH)], axis=0),
                 'mlp_ln_g': jnp.concatenate([grads[('mlp_ln_g', li)] for li in range(DEPTH)], axis=0),
                 'mlp_ln_b': jnp.concatenate([grads[('mlp_ln_b', li)] for li in range(DEPTH)], axis=0),
                 'attn_sinks': d_sinks}
    rep_grad = _rep_rows(rep_local, '')
    n_rep = rep_grad.shape[0]
    last = _Jobs([last_chip, _DirectJob([small_grad, jnp.broadcast_to(rep_grad[None], (N_DEV, n_rep, LANES))])])

    def dx_epi(acc, ex, out, i):
        out[0][...] = acc + alpha * ex[0][...]
    (grad_x,), got = mm_nt("conv_dx", dh0, W_in, [(dz1, 'tile')], [(_sds((T, D), F32), 'tile')], dx_epi,
                           tm=tm, tko=D, tc=D, comm=last)
    got, (recv_small, recv_rep) = last.split(got)
    hosted((None, got), last_chip)

    result = {}
    kinds = ('grad', 'delta', 'new_m', 'new_v')
    w, m, v = (_small_rows(a, pre)[None] for pre in ('', 'm_', 'v_'))
    for kind, arr in zip(kinds, adamw("adamw_small", [recv_small], w, m, v, ta=n_small)):
        for pname, val in _unpack_small(arr[0], a).items():
            result[(kind, pname)] = val
    w, m, v = (_rep_rows(a, pre)[None] for pre in ('', 'm_', 'v_'))
    for kind, arr in zip(kinds, adamw("adamw_rep", [recv_rep], w, m, v, ta=n_rep)):
        for pname, val in _unpack_rep(arr[0], a).items():
            result[(kind, pname)] = val
    for name in BIG_WEIGHTS:
        w, m, v = (shard3(a[pre + name]) for pre in ('', 'm_', 'v_'))
        recvs = [join_halves(recv[(name, li, 0)], recv[(name, li, 1)]) if name in halved else recv[(name, li)]
                 for li in range(w.shape[0])]
        for kind, arr in zip(kinds, adamw("adamw_" + name, recvs, w, m, v, ta=min(256, w.shape[1]))):
            result[(kind, name)] = arr.reshape(a[name].shape)

    loss = lax.psum(loss_local, ("x", "y", "c"))
    out = [loss, grad_x[None]]
    for kind in ('grad', 'delta', 'new_m', 'new_v'):
        out += [result[(kind, name)] for name in WEIGHT_NAMES]
    return tuple(out)


def _glu(x, W_in, b_in, T, D, tm, comm=None):
    n = W_in.shape[2]
    q = 2 if D // n % 2 == 0 else 1
    nt = D // (q * n)
    tn = q * n

    def body(x_ref, wa_ref, wg_ref, ba_ref, bg_ref, g_ref, ha_ref, hg_ref):
        xb = x_ref[...]
        ha = jnp.concatenate([_dot(xb, wa_ref[s], ((1,), (0,))) for s in range(q)], axis=1) + ba_ref[...]
        hg = jnp.concatenate([_dot(xb, wg_ref[s], ((1,), (0,))) for s in range(q)], axis=1) + bg_ref[...]
        g_ref[...] = ha * _sigmoid(hg)
        ha_ref[...] = ha.astype(ha_ref.dtype)
        hg_ref[...] = hg.astype(hg_ref.dtype)

    return _call(
        body, comm, name="conv_in_glu", grid=(T // tm, nt),
        in_specs=[pl.BlockSpec((tm, D), lambda i, j: (i, 0)),
                  pl.BlockSpec((q, D, n), lambda i, j: (j, 0, 0)),
                  pl.BlockSpec((q, D, n), lambda i, j: (j + nt, 0, 0)),
                  pl.BlockSpec((1, tn), lambda i, j: (0, j)), pl.BlockSpec((1, tn), lambda i, j: (0, j + nt))],
        out_specs=[pl.BlockSpec((tm, tn), lambda i, j: (i, j))] * 3,
        out_shape=[_sds((T, D), F32), _sds((T, D), BF16), _sds((T, D), BF16)],
        sem=("parallel", "parallel"), operands=[x, W_in, W_in, b_in, b_in])


def kernel(x, p, conv_w_in, conv_b_in, conv_w_dw, conv_b_dw, conv_ln_g, conv_ln_b, conv_w_out, conv_b_out, kv_w_k, kv_w_v, attn_w_q, attn_sinks, attn_w_o, mix_ln_g, mix_ln_b, mlp_w_up, mlp_w_down, mlp_ln_g, mlp_ln_b, ple_w_proj, ple_w_gate, loss_target, m_conv_w_in, m_conv_b_in, m_conv_w_dw, m_conv_b_dw, m_conv_ln_g, m_conv_ln_b, m_conv_w_out, m_conv_b_out, m_kv_w_k, m_kv_w_v, m_attn_w_q, m_attn_sinks, m_attn_w_o, m_mix_ln_g, m_mix_ln_b, m_mlp_w_up, m_mlp_w_down, m_mlp_ln_g, m_mlp_ln_b, m_ple_w_proj, m_ple_w_gate, v_conv_w_in, v_conv_b_in, v_conv_w_dw, v_conv_b_dw, v_conv_ln_g, v_conv_ln_b, v_conv_w_out, v_conv_b_out, v_kv_w_k, v_kv_w_v, v_attn_w_q, v_attn_sinks, v_attn_w_o, v_mix_ln_g, v_mix_ln_b, v_mlp_w_up, v_mlp_w_down, v_mlp_ln_g, v_mlp_ln_b, v_ple_w_proj, v_ple_w_gate):
    return _step(dict(locals()))
```

```python
import functools
import math

import jax
import jax.numpy as jnp
from jax import lax
from jax.experimental import pallas as pl
from jax.experimental.pallas import tpu as pltpu

F32 = jnp.float32
BF16 = jnp.bfloat16

N_DEV = 8
HEAD_DIM = 64
ROPE_DIM = HEAD_DIM // 4
ROPE_HALF = ROPE_DIM // 2
ROPE_THETA = 500000.0
ATT_BLOCK = 128
CONV_WIDTH = 31
CONV_HALO = 32
CONV_ROWS = 64
LN_EPS = 1e-5
DEPTH = 2
DEEPNORM_ALPHA = (2 * DEPTH) ** 0.25
MASK_VALUE = -1e30

ADAM_LR = 0.001
ADAM_B1 = 0.9
ADAM_B2 = 0.999
ADAM_EPS = 1e-08
ADAM_WD = 0.01
ADAM_STEP = 10

LANES = 128
SUBLANES = 8
VMEM_LIMIT_BYTES = 52 * 1024 * 1024
TOKEN_TILE = 512
MESH_ID = pl.DeviceIdType.MESH
RELAY_AT_TENTHS = 5
FORWARD_AT_TENTHS = 8

WEIGHT_NAMES = ['conv_w_in', 'conv_b_in', 'conv_w_dw', 'conv_b_dw', 'conv_ln_g', 'conv_ln_b', 'conv_w_out',
                'conv_b_out', 'kv_w_k', 'kv_w_v', 'attn_w_q', 'attn_sinks', 'attn_w_o', 'mix_ln_g', 'mix_ln_b',
                'mlp_w_up', 'mlp_w_down', 'mlp_ln_g', 'mlp_ln_b', 'ple_w_proj', 'ple_w_gate']
BIG_WEIGHTS = ['conv_w_in', 'conv_w_out', 'kv_w_k', 'kv_w_v', 'attn_w_q', 'attn_w_o', 'mlp_w_up', 'mlp_w_down',
               'ple_w_proj', 'ple_w_gate']
SMALL_SHARDED = [('conv_b_in', 8), ('conv_w_dw', 32), ('conv_b_dw', 8), ('conv_ln_g', 8), ('conv_ln_b', 8),
                 ('conv_b_out', 8)]
REPLICATED = ['mix_ln_g', 'mix_ln_b', 'mlp_ln_g', 'mlp_ln_b', 'attn_sinks']


def _params(sem):
    return pltpu.CompilerParams(dimension_semantics=sem, vmem_limit_bytes=VMEM_LIMIT_BYTES)


def _sds(shape, dtype):
    return jax.ShapeDtypeStruct(shape, dtype)


def _my_place():
    x, y, c = lax.axis_index("x"), lax.axis_index("y"), lax.axis_index("c")
    return x, y, c, 4 * x + 2 * y + c


def _peers(x, y, c):
    out = []
    for dx in (0, 1):
        for dy in (0, 1):
            for dc in (0, 1):
                if dx or dy or dc:
                    px, py, pc = x ^ dx, y ^ dy, c ^ dc
                    out.append(((px, py, pc), 4 * px + 2 * py + pc))
    return out


N_CHIPS = 4


def _other_chips(x, y):
    return [((x ^ dx, y ^ dy), 2 * (x ^ dx) + (y ^ dy)) for dx, dy in ((1, 0), (0, 1), (1, 1))]


def _remote(src, dst, send, recv, to):
    return pltpu.make_async_remote_copy(src_ref=src, dst_ref=dst, send_sem=send, recv_sem=recv, device_id=to,
                                        device_id_type=MESH_ID)


def _wait_slabs(buf, count, send, recv, me, sent=True, received=True):
    part = buf.at[pl.ds(0, count)]
    cp = _remote(part, part, send, recv, me)
    if sent:
        cp.wait_send()
    if received:
        cp.wait_recv()


class _DirectJob:
    n_sems = 3

    def __init__(self, items):
        self.sources = list(items)
        self.dests = [_sds(it.shape, it.dtype) for it in items]
        self.n = len(items)

    def start(self, src, dst, send, recv, loc):
        x, y, c, me = _my_place()
        for k in range(self.n):
            here = dst[k].at[pl.ds(me, 1)]
            pltpu.make_async_copy(src[k].at[pl.ds(me, 1)], here, loc.at[k]).start()
            for peer, idx in _peers(x, y, c):
                _remote(src[k].at[pl.ds(idx, 1)], here, send.at[k], recv.at[k], peer).start()

    def relay(self, *refs):
        pass

    def forward(self, *refs):
        pass

    def finish(self, src, dst, send, recv, loc):
        x, y, c, me = _my_place()
        for k in range(self.n):
            pltpu.make_async_copy(src[k].at[pl.ds(me, 1)], dst[k].at[pl.ds(me, 1)], loc.at[k]).wait()
            _wait_slabs(dst[k], N_DEV - 1, send.at[k], recv.at[k], (x, y, c))


class _GatherJob:
    n_sems = 7

    def __init__(self, items):
        self.sources = [it.reshape((1,) + it.shape) for it in items]
        self.dests = [_sds((N_DEV,) + it.shape, it.dtype) for it in items]
        self.n = len(items)

    @staticmethod
    def _neighbours(x, y, c):
        out = []
        for dx, dy in ((c, 1 - c), (1 - c, c), (1, 1)):
            px, py = x ^ dx, y ^ dy
            out.append(((px, py, c), 4 * px + 2 * py + c))
        return out

    def start(self, src, dst, send_a, recv_a, send_b, recv_b, send_d, recv_d, loc):
        x, y, c, me = _my_place()
        (first, _), (second, _), _ = self._neighbours(x, y, c)
        for k in range(self.n):
            here = dst[k].at[pl.ds(me, 1)]
            pltpu.make_async_copy(src[k], here, loc.at[k]).start()
            _remote(src[k], here, send_d.at[k], recv_d.at[k], (x, y, 1 - c)).start()
            _remote(src[k], here, send_a.at[k], recv_a.at[k], first).start()
            _remote(src[k], here, send_b.at[k], recv_b.at[k], second).start()

    def relay(self, src, dst, send_a, recv_a, send_b, recv_b, send_d, recv_d, loc):
        x, y, c, me = _my_place()
        (_, first_idx), (second, _), _ = self._neighbours(x, y, c)
        for k in range(self.n):
            _wait_slabs(dst[k], 1, send_a.at[k], recv_a.at[k], (x, y, c), sent=False)
            rows = dst[k].at[pl.ds(first_idx, 1)]
            _remote(rows, rows, send_b.at[k], recv_b.at[k], second).start()

    def forward(self, src, dst, send_a, recv_a, send_b, recv_b, send_d, recv_d, loc):
        x, y, c, me = _my_place()
        for k in range(self.n):
            _wait_slabs(dst[k], 2, send_b.at[k], recv_b.at[k], (x, y, c), sent=False)
            for _, idx in self._neighbours(x, y, c):
                rows = dst[k].at[pl.ds(idx, 1)]
                _remote(rows, rows, send_d.at[k], recv_d.at[k], (x, y, 1 - c)).start()

    def finish(self, src, dst, send_a, recv_a, send_b, recv_b, send_d, recv_d, loc):
        x, y, c, me = _my_place()
        for k in range(self.n):
            pltpu.make_async_copy(src[k], dst[k].at[pl.ds(me, 1)], loc.at[k]).wait()
            _wait_slabs(dst[k], 1, send_a.at[k], recv_a.at[k], (x, y, c), received=False)
            _wait_slabs(dst[k], 2, send_b.at[k], recv_b.at[k], (x, y, c), received=False)
            _wait_slabs(dst[k], N_CHIPS, send_d.at[k], recv_d.at[k], (x, y, c))


class _PairJob:
    n_sems = 2

    def __init__(self, items):
        self.sources = list(items)
        self.dests = [_sds((N_CHIPS,) + it.shape[1:], it.dtype) for it in items]
        self.n = len(items)

    def start(self, src, dst, send, recv):
        x, y, c, me = _my_place()
        for k in range(self.n):
            for q in range(N_CHIPS):
                _remote(src[k].at[pl.ds(2 * q + 1 - c, 1)], dst[k].at[pl.ds(q, 1)], send.at[k], recv.at[k],
                        (x, y, 1 - c)).start()

    def relay(self, *refs):
        pass

    def forward(self, *refs):
        pass

    def finish(self, src, dst, send, recv):
        x, y, c, me = _my_place()
        for k in range(self.n):
            _wait_slabs(dst[k], N_CHIPS, send.at[k], recv.at[k], (x, y, c))


class _ChipJob:
    n_sems = 3

    def __init__(self, items):
        self.sources = list(items)
        self.dests = [_sds(it.shape, it.dtype) for it in items]
        self.n = len(items)

    def start(self, src, dst, send, recv, loc):
        x, y, c, me = _my_place()
        mine = 2 * x + y
        for k in range(self.n):
            here = dst[k].at[pl.ds(mine, 1)]
            pltpu.make_async_copy(src[k].at[pl.ds(mine, 1)], here, loc.at[k]).start()
            for (px, py), q in _other_chips(x, y):
                _remote(src[k].at[pl.ds(q, 1)], here, send.at[k], recv.at[k], (px, py, c)).start()

    def relay(self, *refs):
        pass

    def forward(self, *refs):
        pass

    def finish(self, src, dst, send, recv, loc):
        x, y, c, me = _my_place()
        mine = 2 * x + y
        for k in range(self.n):
            pltpu.make_async_copy(src[k].at[pl.ds(mine, 1)], dst[k].at[pl.ds(mine, 1)], loc.at[k]).wait()
            _wait_slabs(dst[k], N_CHIPS - 1, send.at[k], recv.at[k], (x, y, c))


class _Jobs:
    def __init__(self, jobs):
        self.jobs = list(jobs)
        self.sources = [s for job in self.jobs for s in job.sources]
        self.dests = [d for job in self.jobs for d in job.dests]
        self.n = len(self.sources)

    def _stage(self, stage, src, dst, sems):
        k0 = s0 = 0
        for job in self.jobs:
            getattr(job, stage)(src[k0:k0 + job.n], dst[k0:k0 + job.n], *sems[s0:s0 + job.n_sems])
            k0 += job.n
            s0 += job.n_sems

    def start(self, src, dst, *sems):
        self._stage('start', src, dst, sems)

    def relay(self, src, dst, *sems):
        self._stage('relay', src, dst, sems)

    def forward(self, src, dst, *sems):
        self._stage('forward', src, dst, sems)

    def finish(self, src, dst, *sems):
        self._stage('finish', src, dst, sems)

    def split(self, got):
        out, k0 = [], 0
        for job in self.jobs:
            out.append(got[k0:k0 + job.n])
            k0 += job.n
        return out


def _job_sems(job):
    if isinstance(job, _Jobs):
        return [s for part in job.jobs for s in _job_sems(part)]
    return [pltpu.SemaphoreType.DMA((job.n,))] * job.n_sems


def _call(body, comm, *, name, grid, in_specs, out_specs, out_shape, operands, sem, scratch_shapes=(), aliases=None):
    single = not isinstance(out_shape, (list, tuple))
    out_shape = [out_shape] if single else list(out_shape)
    out_specs = [out_specs] if single else list(out_specs)
    if comm is None:
        res = pl.pallas_call(body, name=name, grid=grid, in_specs=list(in_specs), out_specs=out_specs,
                             out_shape=out_shape, scratch_shapes=list(scratch_shapes),
                             input_output_aliases=aliases or {}, compiler_params=_params(sem))(*operands)
        return res[0] if single else res
    n_in, n_out, n_scr, n_c = len(in_specs), len(out_shape), len(scratch_shapes), comm.n
    any_spec = pl.BlockSpec(memory_space=pl.ANY)
    steps = math.prod(grid)
    early = min(steps - 1, (steps * RELAY_AT_TENTHS) // 10)
    mid = min(steps - 1, (steps * FORWARD_AT_TENTHS) // 10)

    def hosted(*refs):
        ins, c_src = refs[:n_in], refs[n_in:n_in + n_c]
        outs = refs[n_in + n_c:n_in + n_c + n_out]
        c_dst = refs[n_in + n_c + n_out:n_in + 2 * n_c + n_out]
        scr = refs[n_in + 2 * n_c + n_out:n_in + 2 * n_c + n_out + n_scr]
        sems = refs[n_in + 2 * n_c + n_out + n_scr:]
        step = pl.program_id(0)
        for d in range(1, len(grid)):
            step = step * grid[d] + pl.program_id(d)

        @pl.when(step == 0)
        def _():
            comm.start(c_src, c_dst, *sems)

        @pl.when(step == early)
        def _():
            comm.relay(c_src, c_dst, *sems)

        @pl.when(step == mid)
        def _():
            comm.forward(c_src, c_dst, *sems)

        body(*ins, *outs, *scr)

        @pl.when(step == steps - 1)
        def _():
            comm.finish(c_src, c_dst, *sems)

    res = pl.pallas_call(hosted, name=name, grid=grid, in_specs=list(in_specs) + [any_spec] * n_c,
                         out_specs=out_specs + [any_spec] * n_c, out_shape=out_shape + comm.dests,
                         scratch_shapes=list(scratch_shapes) + _job_sems(comm), input_output_aliases=aliases or {},
                         compiler_params=_params(("arbitrary",) * len(grid)))(*operands, *comm.sources)
    main = res[:n_out]
    return (main[0] if single else main), res[n_out:]


def comm_only(name, jobs):
    any_spec = pl.BlockSpec(memory_space=pl.ANY)
    n_all = sum(job.n for job in jobs)

    def body(*refs):
        srcs, dsts, sems = refs[:n_all], refs[n_all:2 * n_all], refs[2 * n_all:]
        parts, k0, s0 = [], 0, 0
        for job in jobs:
            parts.append((job, srcs[k0:k0 + job.n], dsts[k0:k0 + job.n], sems[s0:s0 + job.n_sems]))
            k0 += job.n
            s0 += job.n_sems
        for stage in ('start', 'relay', 'forward', 'finish'):
            for job, src, dst, sem in parts:
                getattr(job, stage)(src, dst, *sem)

    res = pl.pallas_call(body, name=name, in_specs=[any_spec] * n_all, out_specs=[any_spec] * n_all,
                         out_shape=[d for job in jobs for d in job.dests],
                         scratch_shapes=[s for job in jobs for s in _job_sems(job)],
                         )(*[s for job in jobs for s in job.sources])
    out, k0 = [], 0
    for job in jobs:
        out.append(res[k0:k0 + job.n])
        k0 += job.n
    return out


def _dot(a, b, dims):
    if a.dtype != BF16:
        a = a.astype(BF16)
    if b.dtype != BF16:
        b = b.astype(BF16)
    return lax.dot_general(a, b, (dims, ((), ())), preferred_element_type=F32)


def _sigmoid(v):
    return 1.0 / (1.0 + jnp.exp(-v))


def _sq_relu(u):
    return jnp.square(jnp.maximum(u.astype(F32), 0.0)).astype(BF16)


def _ln_stats(z):
    mu = jnp.mean(z, axis=-1, keepdims=True)
    zc = z - mu
    var = jnp.mean(zc * zc, axis=-1, keepdims=True)
    return zc * lax.rsqrt(var + LN_EPS)


def _ln_fwd(z, g, b):
    return _ln_stats(z) * g + b


def _ln_bwd(dy, z, g):
    xhat = _ln_stats(z)
    mu = jnp.mean(z, axis=-1, keepdims=True)
    zc = z - mu
    rstd = lax.rsqrt(jnp.mean(zc * zc, axis=-1, keepdims=True) + LN_EPS)
    dxh = dy * g
    m1 = jnp.mean(dxh, axis=-1, keepdims=True)
    m2 = jnp.mean(dxh * xhat, axis=-1, keepdims=True)
    dz = rstd * (dxh - m1 - xhat * m2)
    return dz, jnp.sum(dy * xhat, axis=0, keepdims=True), jnp.sum(dy, axis=0, keepdims=True)


def _extra_spec(shape, kind, tm, tn, ij):
    if kind == 'tile':
        return pl.BlockSpec((tm, tn), lambda *g: (ij(g)[0], ij(g)[1]))
    if kind in ('row', 'rowacc'):
        return pl.BlockSpec((1, tn), lambda *g: (0, ij(g)[1]))
    if kind == 'tab':
        return pl.BlockSpec((tm, LANES), lambda *g: (ij(g)[0], 0))
    if kind == 'rows':
        return pl.BlockSpec((tm, shape[1]), lambda *g: (ij(g)[0], 0))
    raise ValueError(kind)


def mm_nn(name, pairs, extras, outs, epi, *, tm, tn, tk=None, comm=None, a_pro=None):
    M = pairs[0][0].shape[0]
    N = pairs[0][1].shape[0] * pairs[0][1].shape[2]
    n_pairs = len(pairs)
    K0 = pairs[0][0].shape[1]
    tk = K0 if tk is None else tk
    nk = K0 // tk
    assert nk == 1 or n_pairs == 1
    assert M % tm == 0 and N % tn == 0 and K0 % tk == 0
    has_rowacc = any(kind == 'rowacc' for _, kind in outs)
    assert not has_rowacc or (N == tn and nk == 1)
    in_specs, operands, slabs = [], [], []
    for a, b, off in pairs:
        K = a.shape[1]
        ktile = K if n_pairs > 1 else tk
        n = b.shape[2]
        assert b.shape[1] == K
        in_specs.append(pl.BlockSpec((tm, ktile), lambda i, j, k: (i, k)))
        if tn <= n:
            assert n % tn == 0
            r = n // tn
            in_specs.append(pl.BlockSpec((None, ktile, tn),
                                         lambda i, j, k, r=r, off=off: ((j + off) // r, k, (j + off) % r)))
            slabs.append(0)
        else:
            assert tn % n == 0
            in_specs.append(pl.BlockSpec((tn // n, ktile, n), lambda i, j, k, off=off: (j + off, k, 0)))
            slabs.append(tn // n)
        operands += [a, b]
    ij = lambda g: (g[0], g[1])
    for arr, kind in extras:
        in_specs.append(_extra_spec(arr.shape, kind, tm, tn, ij))
        operands.append(arr)
    out_specs = [_extra_spec(o.shape, kind, tm, tn, ij) for o, kind in outs]
    n_ex, n_out = len(extras), len(outs)

    def pair_dot(ab, q):
        a = ab[2 * q][...]
        if a_pro is not None:
            a = a_pro(a)
        if not slabs[q]:
            return _dot(a, ab[2 * q + 1][...], ((1,), (0,)))
        return jnp.concatenate([_dot(a, ab[2 * q + 1][s], ((1,), (0,))) for s in range(slabs[q])], axis=1)

    def body(*refs):
        ab = refs[:2 * n_pairs]
        ex = refs[2 * n_pairs:2 * n_pairs + n_ex]
        out = refs[2 * n_pairs + n_ex:2 * n_pairs + n_ex + n_out]
        i = pl.program_id(0)
        if nk == 1:
            accs = [pair_dot(ab, q) for q in range(n_pairs)]
            epi(accs, ex, out, i)
        else:
            acc_ref = refs[-1]
            k = pl.program_id(2)

            @pl.when(k == 0)
            def _():
                acc_ref[...] = jnp.zeros_like(acc_ref)

            acc_ref[...] += pair_dot(ab, 0)

            @pl.when(k == nk - 1)
            def _():
                epi([acc_ref[...]], ex, out, i)

    scratch = [pltpu.VMEM((tm, tn), F32)] if nk > 1 else []
    sem = ("arbitrary",) * 3 if has_rowacc else ("parallel", "parallel", "arbitrary")
    return _call(body, comm, name=name, grid=(M // tm, N // tn, nk), in_specs=in_specs, out_specs=out_specs,
                 out_shape=[o for o, _ in outs], scratch_shapes=scratch, sem=sem, operands=operands)


def mm_nt(name, a, b, extras, outs, epi, *, tm, tko, tc, comm=None, a_pro=None, dws=()):
    M, N = a.shape
    J, K, n = b.shape
    assert J * n == N and M % tm == 0 and K % tko == 0 and N % tc == 0
    nc = N // tc
    assert a_pro is None or (nc == 1 and tc <= n and K == tko == N)
    assert not dws or (nc == 1 and K == tko and tc <= n)
    n_dw, nt = len(dws), M // tm
    has_rowacc = any(kind == 'rowacc' for _, kind in outs)
    assert not has_rowacc or K == tko
    if tc <= n:
        assert n % tc == 0
        r = n // tc
        slabs = 0
        b_spec = pl.BlockSpec((None, tko, tc), lambda i, j, c: (c // r, j, c % r))
    else:
        assert tc % n == 0
        slabs = tc // n
        b_spec = pl.BlockSpec((slabs, tko, n), lambda i, j, c: (c, j, 0))
    n_ab = 2 if a_pro is None else 1
    in_specs = ([pl.BlockSpec((tm, tc), lambda i, j, c: (i, c))] if a_pro is None else []) + [b_spec]

    def nt_dot(a_ref, b_ref):
        if not slabs:
            return _dot(a_ref[...], b_ref[...], ((1,), (1,)))
        acc = _dot(a_ref[:, 0:n], b_ref[0], ((1,), (1,)))
        for s in range(1, slabs):
            acc = acc + _dot(a_ref[:, s * n:(s + 1) * n], b_ref[s], ((1,), (1,)))
        return acc

    ij = lambda g: (g[0], g[1])
    operands = [a, b] if a_pro is None else [b]
    for arr, kind in extras:
        in_specs.append(_extra_spec(arr.shape, kind, tm, tko, ij))
        operands.append(arr)
    out_specs = [_extra_spec(o.shape, kind, tm, tko, ij) for o, kind in outs]
    n_ex, n_out = len(extras), len(outs)
    out_shape = [o for o, _ in outs]
    for lhs, _ in dws:
        in_specs.append(pl.BlockSpec((tm, lhs.shape[1]), lambda i, j, c: (i, 0)))
        operands.append(lhs)
        out_specs.append(pl.BlockSpec((None, lhs.shape[1], N), lambda i, j, c: (0, 0, 0)))
        out_shape.append(_sds((1, lhs.shape[1], N), BF16))

    def weight_grads(refs, a_tile, i):
        lhs_refs = refs[n_ab + n_ex:n_ab + n_ex + n_dw]
        dw_refs = refs[n_ab + n_ex + n_dw + n_out:n_ab + n_ex + n_dw + n_out + n_dw]
        acc_refs = refs[len(refs) - n_dw:]
        for (_, rhs_fn), lhs_ref, dw_ref, acc_ref in zip(dws, lhs_refs, dw_refs, acc_refs):
            part = _dot(lhs_ref[...], a_tile if rhs_fn is None else rhs_fn(), ((0,), (0,)))
            _init_or_add(acc_ref, i, part)

            @pl.when(i == nt - 1)
            def _(dw_ref=dw_ref, acc_ref=acc_ref):
                dw_ref[...] = acc_ref[...].astype(dw_ref.dtype)

    def body(*refs):
        ex = refs[n_ab:n_ab + n_ex]
        out = refs[n_ab + n_ex + n_dw:n_ab + n_ex + n_dw + n_out]
        i = pl.program_id(0)
        if a_pro is not None:
            a_tile = a_pro(ex)
            epi(_dot(a_tile, refs[0][...], ((1,), (1,))), ex, out, i)
            weight_grads(refs, a_tile, i)
            return
        a_ref, b_ref = refs[:2]
        if nc == 1:
            epi(nt_dot(a_ref, b_ref), ex, out, i)
            if dws:
                weight_grads(refs, a_ref[...], i)
        else:
            acc_ref = refs[-1]
            c = pl.program_id(2)

            @pl.when(c == 0)
            def _():
                acc_ref[...] = jnp.zeros_like(acc_ref)

            acc_ref[...] += nt_dot(a_ref, b_ref)

            @pl.when(c == nc - 1)
            def _():
                epi(acc_ref[...], ex, out, i)

    scratch = ([pltpu.VMEM((tm, tko), F32)] if nc > 1 else []) + [pltpu.VMEM((lhs.shape[1], N), F32) for lhs, _ in dws]
    sem = ("arbitrary",) * 3 if (has_rowacc or dws) else ("parallel", "parallel", "arbitrary")
    return _call(body, comm, name=name, grid=(M // tm, K // tko, nc), in_specs=in_specs, out_specs=out_specs,
                 out_shape=out_shape, scratch_shapes=scratch, sem=sem, operands=operands)


def mm_tn(name, a, d, n, out_dtype, *, tm, tk, tn, comm=None):
    M, K = a.shape
    N = d.shape[1]
    assert d.shape[0] == M and N % n == 0 and N % tn == 0 and K % tk == 0 and M % tm == 0
    nm = M // tm
    if tn <= n:
        assert n % tn == 0
        r = n // tn
        slabs = 0
        o_spec = pl.BlockSpec((None, tk, tn), lambda kk, j, m: (j // r, kk, j % r))
    else:
        assert tn % n == 0
        slabs = tn // n
        o_spec = pl.BlockSpec((slabs, tk, n), lambda kk, j, m: (j, kk, 0))

    def write(o_ref, acc):
        if not slabs:
            o_ref[...] = acc.astype(o_ref.dtype)
        else:
            for s in range(slabs):
                o_ref[s] = acc[:, s * n:(s + 1) * n].astype(o_ref.dtype)

    def body(a_ref, d_ref, o_ref, *scratch):
        if nm == 1:
            write(o_ref, _dot(a_ref[...], d_ref[...], ((0,), (0,))))
            return
        acc_ref, = scratch
        m = pl.program_id(2)

        @pl.when(m == 0)
        def _():
            acc_ref[...] = jnp.zeros_like(acc_ref)

        acc_ref[...] += _dot(a_ref[...], d_ref[...], ((0,), (0,)))

        @pl.when(m == nm - 1)
        def _():
            write(o_ref, acc_ref[...])

    return _call(
        body, comm, name=name, grid=(K // tk, N // tn, nm),
        in_specs=[pl.BlockSpec((tm, tk), lambda kk, j, m: (m, kk)), pl.BlockSpec((tm, tn), lambda kk, j, m: (m, j))],
        out_specs=o_spec, out_shape=_sds((N // n, K, n), out_dtype),
        scratch_shapes=[pltpu.VMEM((tk, tn), F32)] if nm > 1 else [],
        sem=("parallel", "parallel", "arbitrary"), operands=[a, d])


def mlp_du_dw(name, dzb, w_down, u, xin, n, *, tm, tf, comm=None):
    T, D = dzb.shape
    F = u.shape[1]
    assert T % tm == 0 and F % tf == 0 and tf % n == 0
    slabs, nt = tf // n, T // tm

    def body(dz_ref, w_ref, u_ref, x_ref, du_ref, dwu_ref, dwd_ref, up_acc, down_acc):
        i = pl.program_id(1)
        dz = dz_ref[...]
        da = _dot(dz, w_ref[...], ((1,), (1,)))
        relu = jnp.maximum(u_ref[...].astype(F32), 0.0)
        du = (da * (2.0 * relu)).astype(BF16)
        du_ref[...] = du
        _init_or_add(up_acc, i, _dot(x_ref[...], du, ((0,), (0,))))
        _init_or_add(down_acc, i, _dot(jnp.square(relu).astype(BF16), dz, ((0,), (0,))))

        @pl.when(i == nt - 1)
        def _():
            for s in range(slabs):
                dwu_ref[s] = up_acc[:, s * n:(s + 1) * n].astype(dwu_ref.dtype)
            dwd_ref[...] = down_acc[...].astype(dwd_ref.dtype)

    hidden = pl.BlockSpec((tm, tf), lambda j, i: (i, j))
    tokens = pl.BlockSpec((tm, D), lambda j, i: (i, 0))
    return _call(
        body, comm, name=name, grid=(F // tf, nt),
        in_specs=[tokens, pl.BlockSpec((None, tf, D), lambda j, i: (0, j, 0)), hidden, tokens],
        out_specs=[hidden, pl.BlockSpec((slabs, D, n), lambda j, i: (j, 0, 0)),
                   pl.BlockSpec((None, tf, D), lambda j, i: (0, j, 0))],
        out_shape=[_sds((T, F), BF16), _sds((F // n, D, n), BF16), _sds((1, F, D), BF16)],
        scratch_shapes=[pltpu.VMEM((D, tf), F32), pltpu.VMEM((tf, D), F32)], sem=("arbitrary", "arbitrary"),
        operands=[dzb, w_down, u, xin])


def _init_or_add(ref, i, value):
    @pl.when(i == 0)
    def _():
        ref[...] = value

    @pl.when(i > 0)
    def _():
        ref[...] += value


def _rope_tables(T):
    pos = jnp.arange(T, dtype=F32)
    inv_freq = ROPE_THETA ** (-jnp.arange(0, ROPE_DIM, 2, dtype=F32) / ROPE_DIM)
    ang = pos[:, None] * inv_freq[None, :]
    cos, sin = jnp.cos(ang), jnp.sin(ang)
    ones = jnp.ones((T, HEAD_DIM - ROPE_DIM), F32)
    zeros = jnp.zeros((T, HEAD_DIM - ROPE_DIM), F32)
    zh = jnp.zeros((T, ROPE_HALF), F32)
    c_head = jnp.concatenate([cos, cos, ones], axis=1)
    s_up = jnp.concatenate([-sin, zh, zeros], axis=1)
    s_dn = jnp.concatenate([zh, sin, zeros], axis=1)
    rep = LANES // HEAD_DIM
    return tuple(jnp.tile(t, (1, rep)) for t in (c_head, s_up, s_dn))


def _rope_chunk(t, c, s_up, s_dn):
    return t * c + pltpu.roll(t, LANES - ROPE_HALF, 1) * s_up + pltpu.roll(t, ROPE_HALF, 1) * s_dn


def _rope_chunk_bwd(d, c, s_up, s_dn):
    return d * c + pltpu.roll(d * s_up, ROPE_HALF, 1) + pltpu.roll(d * s_dn, LANES - ROPE_HALF, 1)


def _rope_wide(t, c, s_up, s_dn, fn):
    chunks = [fn(t[:, q * LANES:(q + 1) * LANES], c, s_up, s_dn) for q in range(t.shape[1] // LANES)]
    return chunks[0] if len(chunks) == 1 else jnp.concatenate(chunks, axis=1)


def _taps_by_residue(first):
    groups = {}
    for o in range(first, first + CONV_WIDTH):
        groups.setdefault(o % SUBLANES, []).append(o)
    return sorted(groups.items())


def _shifted_taps(win_ref, span_ref, r0, res, offs, rb, ls):
    if res == 0:
        return [functools.partial(lambda o: win_ref[pl.ds(r0 + o, rb), ls], o) for o in offs]
    n = rb + offs[-1] - res
    span_ref[0:n, :] = win_ref[pl.ds(r0 + res, n), ls]
    return [functools.partial(lambda o: span_ref[pl.ds(o - res, rb), :], o) for o in offs]


def _halo_before_spec(tm, D):
    return pl.BlockSpec((CONV_HALO, D), lambda i: (jnp.maximum(i * (tm // CONV_HALO) - 1, 0), 0))


def dwconv_fwd(g, w_dw, b_dw, ln_g, ln_b, *, tm, comm=None):
    T, D = g.shape
    nl = D // LANES

    rb = min(CONV_ROWS, tm)

    def body(g_ref, gh_ref, w_ref, b_ref, lg_ref, lb_ref, c_ref, s_ref, win_ref, span_ref):
        i = pl.program_id(0)
        win_ref[0:CONV_HALO, :] = jnp.where(i > 0, gh_ref[...], 0.0)
        win_ref[CONV_HALO:, :] = g_ref[...]

        def lane_chunk(q, carry):
            ls = pl.ds(pl.multiple_of(q * LANES, LANES), LANES)
            for r0 in range(0, tm, rb):
                acc = jnp.broadcast_to(b_ref[:, ls], (rb, LANES))
                for res, offs in _taps_by_residue(CONV_HALO - (CONV_WIDTH - 1)):
                    taps = _shifted_taps(win_ref, span_ref, r0, res, offs, rb, ls)
                    for o, tap in zip(offs, taps):
                        k = o - (CONV_HALO - (CONV_WIDTH - 1))
                        acc = acc + tap() * w_ref[k:k + 1, ls]
                c_ref[pl.ds(r0, rb), ls] = acc
            return carry

        lax.fori_loop(0, nl, lane_chunk, 0)
        n = _ln_fwd(c_ref[...], lg_ref[...], lb_ref[...])
        s_ref[...] = (n * _sigmoid(n)).astype(s_ref.dtype)

    row = pl.BlockSpec((1, D), lambda i: (0, 0))
    return _call(
        body, comm, name="dwconv_fwd", grid=(T // tm,),
        in_specs=[pl.BlockSpec((tm, D), lambda i: (i, 0)), _halo_before_spec(tm, D),
                  pl.BlockSpec((CONV_HALO, D), lambda i: (0, 0)), row, row, row],
        out_specs=[pl.BlockSpec((tm, D), lambda i: (i, 0)), pl.BlockSpec((tm, D), lambda i: (i, 0))],
        out_shape=[_sds((T, D), F32), _sds((T, D), BF16)],
        scratch_shapes=[pltpu.VMEM((tm + CONV_HALO, D), F32), pltpu.VMEM((rb + CONV_HALO, LANES), F32)],
        sem=("parallel",), operands=[g, g, w_dw, b_dw, ln_g, ln_b])


def dwconv_bwd(dc, g, ha, hg, w_dw, *, tm, comm=None):
    T, D = g.shape
    nl = D // LANES
    last = T // CONV_HALO - 1
    nt = T // tm

    rb = min(CONV_ROWS, tm)

    def body(dc_ref, dcn_ref, g_ref, gh_ref, ha_ref, hg_ref, w_ref, dh_ref, dw_ref, dbdw_ref, dbin_ref,
             win_ref, dwin_ref, dg_ref, dwp_ref, span_ref):
        i = pl.program_id(0)
        win_ref[0:CONV_HALO, :] = jnp.where(i > 0, gh_ref[...], 0.0)
        win_ref[CONV_HALO:, :] = g_ref[...]
        dwin_ref[0:tm, :] = dc_ref[...]
        dwin_ref[tm:, :] = jnp.where(i < nt - 1, dcn_ref[...], 0.0)

        @pl.when(i == 0)
        def _():
            dwp_ref[...] = jnp.zeros_like(dwp_ref)

        first = CONV_HALO - (CONV_WIDTH - 1)

        def lane_chunk(q, carry):
            ls = pl.ds(pl.multiple_of(q * LANES, LANES), LANES)
            for r0 in range(0, tm, rb):
                acc = jnp.zeros((rb, LANES), F32)
                for res, offs in _taps_by_residue(0):
                    taps = _shifted_taps(dwin_ref, span_ref, r0, res, offs, rb, ls)
                    for o, tap in zip(offs, taps):
                        k = CONV_WIDTH - 1 - o
                        acc = acc + tap() * w_ref[k:k + 1, ls]
                dg_ref[pl.ds(r0, rb), ls] = acc
                dcv = dwin_ref[pl.ds(r0, rb), ls]
                for res, offs in _taps_by_residue(first):
                    taps = _shifted_taps(win_ref, span_ref, r0, res, offs, rb, ls)
                    for o, tap in zip(offs, taps):
                        k = o - first
                        prod = dcv * tap()
                        dwp_ref[k, :, ls] += jnp.sum(prod.reshape(rb // SUBLANES, SUBLANES, LANES), axis=0)
            return carry

        lax.fori_loop(0, nl, lane_chunk, 0)

        @pl.when(i == nt - 1)
        def _():
            for k in range(CONV_WIDTH):
                dw_ref[k:k + 1, :] = jnp.sum(dwp_ref[k], axis=0, keepdims=True)
            dw_ref[CONV_WIDTH:, :] = jnp.zeros((CONV_HALO - CONV_WIDTH, D), F32)
        dg = dg_ref[...]
        ha = ha_ref[...].astype(F32)
        sg = _sigmoid(hg_ref[...].astype(F32))
        d_ha = dg * sg
        d_hg = dg * ha * sg * (1.0 - sg)
        dh_ref[:, 0:D] = d_ha.astype(dh_ref.dtype)
        dh_ref[:, D:] = d_hg.astype(dh_ref.dtype)
        _init_or_add(dbdw_ref, i, jnp.sum(dc_ref[...], axis=0, keepdims=True))
        _init_or_add(dbin_ref, i, jnp.concatenate([jnp.sum(d_ha, axis=0, keepdims=True),
                                                   jnp.sum(d_hg, axis=0, keepdims=True)], axis=1))

    tile = pl.BlockSpec((tm, D), lambda i: (i, 0))
    return _call(
        body, comm, name="dwconv_bwd", grid=(nt,),
        in_specs=[tile,
                  pl.BlockSpec((CONV_HALO, D), lambda i: (jnp.minimum((i + 1) * (tm // CONV_HALO), last), 0)),
                  tile, _halo_before_spec(tm, D),
                  tile, tile, pl.BlockSpec((CONV_HALO, D), lambda i: (0, 0))],
        out_specs=[pl.BlockSpec((tm, 2 * D), lambda i: (i, 0)), pl.BlockSpec((CONV_HALO, D), lambda i: (0, 0)),
                   pl.BlockSpec((1, D), lambda i: (0, 0)), pl.BlockSpec((1, 2 * D), lambda i: (0, 0))],
        out_shape=[_sds((T, 2 * D), BF16), _sds((CONV_HALO, D), F32), _sds((1, D), F32), _sds((1, 2 * D), F32)],
        scratch_shapes=[pltpu.VMEM((tm + CONV_HALO, D), F32), pltpu.VMEM((tm + CONV_HALO, D), F32),
                        pltpu.VMEM((tm, D), F32), pltpu.VMEM((CONV_WIDTH, SUBLANES, D), F32),
                        pltpu.VMEM((rb + CONV_HALO, LANES), F32)],
        sem=("arbitrary",), operands=[dc, dc, g, g, ha, hg, w_dw])


def _attn_specs(HD, W):
    B = ATT_BLOCK
    return [pl.BlockSpec((B, HD), lambda n: (n, 0)),
            pl.BlockSpec((B, 4 * W), lambda n: (n, 0)),
            pl.BlockSpec((B, 4 * W), lambda n: (jnp.maximum(n - 1, 0), 0))]


def _band_mask(n):
    r = lax.broadcasted_iota(jnp.int32, (ATT_BLOCK, 2 * ATT_BLOCK), 0)
    j = lax.broadcasted_iota(jnp.int32, (ATT_BLOCK, 2 * ATT_BLOCK), 1)
    return (j > r) & (j <= r + ATT_BLOCK) & ((n > 0) | (j >= ATT_BLOCK))


def _band(kvc_ref, kvp_ref, part, g, parity, W):
    lanes = slice((2 * part + parity) * W + g * LANES, (2 * part + parity) * W + (g + 1) * LANES)
    return jnp.concatenate([kvp_ref[:, lanes], kvc_ref[:, lanes]], axis=0)


def _half_mask(parity):
    lane = lax.broadcasted_iota(jnp.int32, (1, LANES), 1)
    return (lane < HEAD_DIM) if parity == 0 else (lane >= HEAD_DIM)


def widen_kv(k, v, out_ref, n_kv):
    W = n_kv * LANES
    low = _half_mask(0)
    for part, src in enumerate((k, v)):
        for cg in range(n_kv * HEAD_DIM // LANES):
            chunk = src[:, cg * LANES:(cg + 1) * LANES]
            swapped = pltpu.roll(chunk, HEAD_DIM, 1)
            for g, lo, hi in ((2 * cg, chunk, swapped), (2 * cg + 1, swapped, chunk)):
                base = 2 * part * W + g * LANES
                out_ref[:, base:base + LANES] = jnp.where(low, lo, 0.0).astype(out_ref.dtype)
                out_ref[:, base + W:base + W + LANES] = jnp.where(low, 0.0, hi).astype(out_ref.dtype)


def attn_fwd(q, kvx, sinks, n_heads, n_kv, comm=None):
    T, HD = q.shape
    W = n_kv * LANES
    B = ATT_BLOCK
    chunks_per_group = n_heads // n_kv // 2
    scale = 1.0 / math.sqrt(HEAD_DIM)

    def body(q_ref, kvc_ref, kvp_ref, sink_ref, o_ref, lse_ref):
        n = pl.program_id(0)
        mask = jnp.tile(_band_mask(n), (n_heads, 1))
        s = jnp.concatenate(
            [_dot(q_ref[:, (h // 2) * LANES:(h // 2 + 1) * LANES],
                  _band(kvc_ref, kvp_ref, 0, h // 2 // chunks_per_group, h % 2, W), ((1,), (1,)))
             for h in range(n_heads)], axis=0)
        sink = jnp.concatenate([jnp.broadcast_to(sink_ref[:, h:h + 1], (B, 1)) for h in range(n_heads)], axis=0)
        s = jnp.where(mask, s * scale, MASK_VALUE)
        m = jnp.maximum(jnp.max(s, axis=-1, keepdims=True), sink)
        e = jnp.exp(s - m)
        total = _dot(e, jnp.ones((2 * B, LANES), BF16), ((1,), (0,))) + jnp.exp(sink - m)
        lse = m + jnp.log(total[:, 0:1])
        inv = 1.0 / total
        probs = (e * jnp.concatenate([inv, inv], axis=1)).astype(BF16)
        for c in range(n_heads // 2):
            g = c // chunks_per_group
            out = (_dot(probs[2 * c * B:(2 * c + 1) * B], _band(kvc_ref, kvp_ref, 1, g, 0, W), ((1,), (0,)))
                   + _dot(probs[(2 * c + 1) * B:(2 * c + 2) * B], _band(kvc_ref, kvp_ref, 1, g, 1, W), ((1,), (0,))))
            o_ref[:, c * LANES:(c + 1) * LANES] = out.astype(o_ref.dtype)
        lse_ref[...] = jnp.concatenate([lse[h * B:(h + 1) * B] for h in range(n_heads)], axis=1)

    return _call(
        body, comm, name="attn_fwd", grid=(T // B,),
        in_specs=_attn_specs(HD, W) + [pl.BlockSpec((1, n_heads), lambda n: (0, 0))],
        out_specs=[pl.BlockSpec((B, HD), lambda n: (n, 0)), pl.BlockSpec((B, n_heads), lambda n: (n, 0))],
        out_shape=[_sds((T, HD), BF16), _sds((T, n_heads), F32)],
        sem=("parallel",), operands=[q, kvx, kvx, sinks])


def attn_bwd(q, kvx, do, lse, sinks, rope, n_heads, n_kv, comm=None):
    T, HD = q.shape
    KVD = n_kv * HEAD_DIM
    W = n_kv * LANES
    B = ATT_BLOCK
    chunks_per_group = n_heads // n_kv // 2
    scale = 1.0 / math.sqrt(HEAD_DIM)
    nb = T // B

    def body(q_ref, kvc_ref, kvp_ref, do_ref, lse_ref, sink_ref, c_ref, su_ref, sd_ref, dq_ref, dkv_ref, dsink_ref):
        n = pl.program_id(0)

        @pl.when(n == 0)
        def _():
            dkv_ref[...] = jnp.zeros_like(dkv_ref)
            dsink_ref[...] = jnp.zeros_like(dsink_ref)

        def chunk(ref, h):
            return ref[:, (h // 2) * LANES:(h // 2 + 1) * LANES]

        def band(part, h):
            return _band(kvc_ref, kvp_ref, part, h // 2 // chunks_per_group, h % 2, W)

        def stack(per_head):
            return jnp.concatenate([per_head(h) for h in range(n_heads)], axis=0)

        mask = jnp.tile(_band_mask(n), (n_heads, 1))
        s = stack(lambda h: _dot(chunk(q_ref, h), band(0, h), ((1,), (1,))))
        dp = stack(lambda h: _dot(chunk(do_ref, h), band(1, h), ((1,), (1,))))
        lse = stack(lambda h: lse_ref[:, h:h + 1])
        sink = stack(lambda h: jnp.broadcast_to(sink_ref[:, h:h + 1], (B, 1)))
        probs = jnp.exp(jnp.where(mask, s * scale, MASK_VALUE) - lse)
        delta = jnp.sum(probs * dp, axis=-1, keepdims=True)
        ds = (probs * (dp - delta) * scale).astype(BF16)
        probs = probs.astype(BF16)
        sink_term = jnp.exp(sink - lse) * delta
        dsk = [-jnp.sum(sink_term[h * B:(h + 1) * B], axis=0, keepdims=True) for h in range(n_heads)]

        dk_wide, dv_wide = [None] * n_kv, [None] * n_kv
        for c in range(n_heads // 2):
            g = c // chunks_per_group
            dq2 = None
            for h in (2 * c, 2 * c + 1):
                half = _half_mask(h % 2)
                q2, do2 = chunk(q_ref, h), chunk(do_ref, h)
                ds_h, p_h = ds[h * B:(h + 1) * B], probs[h * B:(h + 1) * B]
                part = _dot(ds_h, band(0, h), ((1,), (0,)))
                dq2 = part if dq2 is None else dq2 + part
                dk_h = _dot(ds_h, jnp.where(half, q2, jnp.zeros_like(q2)), ((0,), (0,)))
                dv_h = _dot(p_h, jnp.where(half, do2, jnp.zeros_like(do2)), ((0,), (0,)))
                dk_wide[g] = dk_h if dk_wide[g] is None else dk_wide[g] + dk_h
                dv_wide[g] = dv_h if dv_wide[g] is None else dv_wide[g] + dv_h
            dq_ref[:, c * LANES:(c + 1) * LANES] = _rope_chunk_bwd(
                dq2, c_ref[...], su_ref[...], sd_ref[...]).astype(dq_ref.dtype)

        def fold(wide):
            low = _half_mask(0)
            both = [w + pltpu.roll(w, HEAD_DIM, 1) for w in wide]
            return jnp.concatenate([jnp.where(low, both[2 * cg], both[2 * cg + 1]) for cg in range(n_kv // 2)], axis=1)

        dkv = jnp.concatenate([fold(dk_wide), fold(dv_wide)], axis=1)
        prev = pl.ds(pl.multiple_of(jnp.maximum(n - 1, 0) * B, B), B)
        cur = pl.ds(pl.multiple_of(n * B, B), B)
        dkv_ref[prev, :] += dkv[0:B, :]
        dkv_ref[cur, :] += dkv[B:, :]
        dsink_ref[...] += jnp.concatenate(dsk, axis=1)

    tab = pl.BlockSpec((B, LANES), lambda n: (n, 0))
    return _call(
        body, comm, name="attn_bwd", grid=(nb,),
        in_specs=_attn_specs(HD, W) + [pl.BlockSpec((B, HD), lambda n: (n, 0)),
                                       pl.BlockSpec((B, n_heads), lambda n: (n, 0)),
                                       pl.BlockSpec((1, n_heads), lambda n: (0, 0)), tab, tab, tab],
        out_specs=[pl.BlockSpec((B, HD), lambda n: (n, 0)), pl.BlockSpec((T, 2 * KVD), lambda n: (0, 0)),
                   pl.BlockSpec((1, n_heads), lambda n: (0, 0))],
        out_shape=[_sds((T, HD + 2 * KVD), BF16), _sds((T, 2 * KVD), F32), _sds((1, n_heads), F32)],
        sem=("arbitrary",), operands=[q, kvx, kvx, do, lse, sinks, *rope])


def dkv_finish(d_qkv, dkv, rope, HD, KVD, *, tm):
    T = dkv.shape[0]
    kv_col = HD // (2 * KVD)

    def body(alias_ref, dkv_ref, c_ref, su_ref, sd_ref, o_ref):
        del alias_ref
        dk = _rope_wide(dkv_ref[:, 0:KVD], c_ref[...], su_ref[...], sd_ref[...], _rope_chunk_bwd)
        o_ref[:, 0:KVD] = dk.astype(o_ref.dtype)
        o_ref[:, KVD:] = dkv_ref[:, KVD:].astype(o_ref.dtype)

    tab = pl.BlockSpec((tm, LANES), lambda i: (i, 0))
    return pl.pallas_call(
        body, name="dkv_finish", grid=(T // tm,),
        in_specs=[pl.BlockSpec(memory_space=pl.ANY), pl.BlockSpec((tm, 2 * KVD), lambda i: (i, 0)), tab, tab, tab],
        out_specs=pl.BlockSpec((tm, 2 * KVD), lambda i: (i, kv_col)),
        out_shape=_sds(d_qkv.shape, d_qkv.dtype), input_output_aliases={0: 0},
        compiler_params=_params(("parallel",)))(d_qkv, dkv, *rope)


def ple_bwd_elem(d_out, pp, gg, *, tm):
    T, D = d_out.shape

    def body(d_ref, pp_ref, gg_ref, dpp_ref, dgg_ref):
        d = d_ref[...]
        sg = _sigmoid(gg_ref[...].astype(F32))
        dpp_ref[...] = (d * sg).astype(dpp_ref.dtype)
        dgg_ref[...] = (d * pp_ref[...].astype(F32) * sg * (1.0 - sg)).astype(dgg_ref.dtype)

    tile = pl.BlockSpec((tm, D), lambda i: (i, 0))
    return pl.pallas_call(
        body, name="ple_bwd_elem", grid=(T // tm,), in_specs=[tile, tile, tile], out_specs=[tile, tile],
        out_shape=[_sds((T, D), BF16), _sds((T, D), BF16)], compiler_params=_params(("parallel",)))(d_out, pp, gg)


def chip_sum(name, g, p_sib, *, ta, part=0, parts=1):
    _, a, b = g.shape
    rows = a // parts
    assert a % parts == 0 and rows % ta == 0
    first = part * (rows // ta)

    def body(core_ref, g_ref, p_ref, o_ref):
        del core_ref
        o_ref[...] = (g_ref[...].astype(F32) + p_ref[...].astype(F32)).astype(o_ref.dtype)

    my_core = lax.axis_index("c").astype(jnp.int32).reshape(1)
    return pl.pallas_call(
        body, name=name, out_shape=_sds((N_CHIPS, rows, b), g.dtype),
        grid_spec=pltpu.PrefetchScalarGridSpec(
            num_scalar_prefetch=1, grid=(N_CHIPS, rows // ta),
            in_specs=[pl.BlockSpec((None, None, ta, b), lambda q, i, core: (q, core[0], first + i, 0)),
                      pl.BlockSpec((None, ta, b), lambda q, i, core: (q, first + i, 0))],
            out_specs=pl.BlockSpec((None, ta, b), lambda q, i, core: (q, i, 0))),
        compiler_params=_params(("arbitrary", "arbitrary")))(my_core, g.reshape(N_CHIPS, 2, a, b), p_sib)


def chip_sums(name, pairs):
    n = len(pairs)

    def body(core_ref, *refs):
        del core_ref
        for k in range(n):
            refs[2 * n + k][...] = (refs[2 * k][...].astype(F32) + refs[2 * k + 1][...].astype(F32)
                                    ).astype(refs[2 * n + k].dtype)

    in_specs, out_specs, out_shape, operands = [], [], [], []
    for g, p_sib in pairs:
        _, a, b = g.shape
        in_specs += [pl.BlockSpec((None, None, a, b), lambda q, core: (q, core[0], 0, 0)),
                     pl.BlockSpec((None, a, b), lambda q, core: (q, 0, 0))]
        out_specs.append(pl.BlockSpec((None, a, b), lambda q, core: (q, 0, 0)))
        out_shape.append(_sds((N_CHIPS, a, b), g.dtype))
        operands += [g.reshape(N_CHIPS, 2, a, b), p_sib]
    my_core = lax.axis_index("c").astype(jnp.int32).reshape(1)
    return pl.pallas_call(
        body, name=name, out_shape=out_shape,
        grid_spec=pltpu.PrefetchScalarGridSpec(num_scalar_prefetch=1, grid=(N_CHIPS,), in_specs=in_specs,
                                               out_specs=out_specs),
        compiler_params=_params(("arbitrary",)))(my_core, *operands)


def adamw(name, recvs, w, m, v, *, ta):
    L, a, b = w.shape
    n_terms = recvs[0].shape[0]
    assert a % ta == 0 and len(recvs) == L
    c1 = 1.0 - ADAM_B1 ** ADAM_STEP
    c2 = 1.0 - ADAM_B2 ** ADAM_STEP

    def body(*refs):
        r_refs = refs[:L]
        w_ref, m_ref, v_ref, g_ref, d_ref, nm_ref, nv_ref = refs[L:]
        layer = pl.program_id(0)
        for l in range(L):
            @pl.when(layer == l)
            def _(r_ref=r_refs[l]):
                g = r_ref[0].astype(F32)
                for s in range(1, n_terms):
                    g = g + r_ref[s].astype(F32)
                nm = ADAM_B1 * m_ref[...] + (1.0 - ADAM_B1) * g
                nv = ADAM_B2 * v_ref[...] + (1.0 - ADAM_B2) * jnp.square(g)
                m_hat = nm / c1
                v_hat = nv / c2
                g_ref[...] = g
                d_ref[...] = -ADAM_LR * (m_hat / (jnp.sqrt(v_hat) + ADAM_EPS) + ADAM_WD * w_ref[...])
                nm_ref[...] = nm
                nv_ref[...] = nv

    blk = pl.BlockSpec((None, ta, b), lambda l, i: (l, i, 0))
    out = _sds((L, a, b), F32)
    r_specs = [pl.BlockSpec((n_terms, ta, b), lambda l, i, ll=ll: (0, jnp.where(l == ll, i, 0), 0)) for ll in range(L)]
    return pl.pallas_call(
        body, name=name, grid=(L, a // ta), in_specs=r_specs + [blk, blk, blk],
        out_specs=[blk, blk, blk, blk], out_shape=[out, out, out, out],
        compiler_params=_params(("arbitrary", "arbitrary")))(*recvs, w, m, v)


def _pack_rows(parts):
    out = []
    for arr, rows in parts:
        arr = arr.reshape(-1, LANES).astype(F32)
        out.append(jnp.pad(arr, ((0, rows - arr.shape[0]), (0, 0))))
    return jnp.concatenate(out, axis=0)


def _small_rows(a, prefix):
    return _pack_rows([(a[prefix + name], rows) for name, rows in SMALL_SHARDED])


def _unpack_small(packed, a):
    out, r0 = {}, 0
    for name, rows in SMALL_SHARDED:
        shape = a[name].shape
        used = math.prod(shape) // LANES
        out[name] = packed[r0:r0 + used].reshape(shape)
        r0 += rows
    return out


def _rep_rows(a, prefix):
    parts = []
    for name in REPLICATED:
        arr = a[prefix + name]
        if arr.size % LANES:
            arr = jnp.pad(arr.reshape(1, -1), ((0, 0), (0, LANES - arr.size % LANES)))
        rows = -(-arr.size // LANES)
        parts.append((arr, -(-rows // SUBLANES) * SUBLANES))
    return _pack_rows(parts)


def _unpack_rep(packed, a):
    out, r0 = {}, 0
    for name in REPLICATED:
        shape = a[name].shape
        size = math.prod(shape)
        rows = -(-size // LANES)
        out[name] = packed[r0:r0 + rows].reshape(-1)[:size].reshape(shape)
        r0 += -(-rows // SUBLANES) * SUBLANES
    return out


def _step(a):
    x = a['x'][0]
    T, D = x.shape
    tgt = a['loss_target'][0]
    p_in = [a['p'][i, 0] for i in range(DEPTH)]
    PLE = p_in[0].shape[1]
    n_heads = a['attn_sinks'].shape[1]
    HD = n_heads * HEAD_DIM
    KVD = a['kv_w_k'].shape[1]
    n_kv = KVD // HEAD_DIM
    F = a['mlp_w_down'].shape[1] * N_DEV
    tm = min(TOKEN_TILE, T)
    tm2 = min(2 * TOKEN_TILE, T)
    tmc = min(TOKEN_TILE, T)
    tw = 512
    alpha = DEEPNORM_ALPHA
    xb, p_b = x, p_in

    def shard3(w):
        return w.reshape((1,) + w.shape) if w.ndim == 2 else w

    def gather(*specs):
        items = []
        for spec in specs:
            w = shard3(a[spec[0]])[spec[1]]
            if len(spec) == 3:
                rows = w.shape[0] // 2
                w = w[spec[2] * rows:(spec[2] + 1) * rows]
            items.append(w.astype(BF16))
        return _GatherJob(items)

    def join_halves(lo, hi):
        return jnp.concatenate([lo, hi], axis=1)

    (W_in, small_full), = comm_only("gather_first",
                                    [_GatherJob([a['conv_w_in'][0].astype(BF16), _small_rows(a, '')])])
    r0, small = 0, {}
    for name, rows in SMALL_SHARDED:
        small[name] = small_full[:, r0:r0 + rows]
        r0 += rows
    b_in = small['conv_b_in'][:, 0:2 * D // N_DEV // LANES].reshape(1, 2 * D)
    w_dw = jnp.transpose(small['conv_w_dw'], (1, 0, 2)).reshape(CONV_HALO, D)
    b_dw, cln_g, cln_b, b_out = (small[nm][:, 0].reshape(1, D) for nm in
                                 ('conv_b_dw', 'conv_ln_g', 'conv_ln_b', 'conv_b_out'))
    W_up, W_down, W_proj, W_gate = {}, {}, {}, {}
    mix_g, mix_b, mlp_g, mlp_b = a['mix_ln_g'], a['mix_ln_b'], a['mlp_ln_g'], a['mlp_ln_b']
    rope = _rope_tables(T)

    def set_ple_weights(li, g_proj, g_gate):
        W_proj[li] = jnp.transpose(g_proj, (1, 0, 2)).reshape(1, PLE, D)
        W_gate[li] = g_gate.reshape(1, D, D)

    def row(v, i):
        return v[i:i + 1]

    def res_ln_epi(coef):
        def epi(accs, ex, out, i):
            acc = accs[0] if isinstance(accs, list) else accs
            n_ex = len(ex)
            res_ref, g_ref, b_ref = ex[n_ex - 3], ex[n_ex - 2], ex[n_ex - 1]
            z = coef * res_ref[...] + acc
            if n_ex == 4:
                z = z + ex[0][...]
            out[0][...] = z
            xo = _ln_fwd(z, g_ref[...], b_ref[...])
            out[1][...] = xo
            out[2][...] = xo.astype(BF16)
        return epi

    res_ln_outs = [(_sds((T, D), F32), 'tile'), (_sds((T, D), F32), 'tile'), (_sds((T, D), BF16), 'tile')]

    def mlp_fwd(li, xin, xin_b, down_comm=None):
        def up_epi(accs, ex, out, i):
            out[0][...] = accs[0].astype(BF16)
        (u,), got_up = mm_nn(f"mlp_up_{li}", [(xin_b, W_up[li], 0)], [], [(_sds((T, F), BF16), 'tile')],
                             up_epi, tm=tm2, tn=min(1024, F), comm=gather(('mlp_w_down', li)))
        W_down[li] = got_up[0].reshape(1, F, D)
        res = mm_nn(f"mlp_down_{li}", [(u, W_down[li], 0)],
                    [(xin, 'tile'), (row(mlp_g, li), 'row'), (row(mlp_b, li), 'row')],
                    res_ln_outs, res_ln_epi(alpha), tm=tm, tn=D, tk=F, comm=down_comm, a_pro=_sq_relu)
        (z, xo, xo_b), got_down = res if down_comm is not None else (res, ())
        return u, z, xo, xo_b, got_down

    def ple_fwd(li, xin, xin_b, with_loss, comm=None):
        def epi(accs, ex, out, i):
            pp, gg = accs
            xo = ex[0][...] + pp * _sigmoid(gg)
            out[1][...] = pp.astype(BF16)
            out[2][...] = gg.astype(BF16)
            if with_loss:
                err = xo - ex[1][...]
                out[0][...] = err * (1.0 / D)
                _init_or_add(out[3], i, jnp.sum(err * err, axis=0, keepdims=True) * (0.5 / D))
            else:
                out[0][...] = xo
                out[3][...] = xo.astype(BF16)
        extras = [(xin, 'tile')] + ([(tgt, 'tile')] if with_loss else [])
        outs = [(_sds((T, D), F32), 'tile'), (_sds((T, D), BF16), 'tile'), (_sds((T, D), BF16), 'tile')]
        outs.append((_sds((1, D), F32), 'rowacc') if with_loss else (_sds((T, D), BF16), 'tile'))
        return mm_nn(f"ple_{li}", [(p_b[li], W_proj[li], 0), (xin_b, W_gate[li], 0)], extras, outs, epi, tm=tm, tn=D,
                     comm=comm)

    assert D // N_DEV == LANES
    (g0, ha0, hg0), got = _glu(xb, W_in, b_in, T, D, tm,
                               gather(('conv_w_out', 0), ('ple_w_proj', 0), ('ple_w_gate', 0)))
    W_out = got[0].reshape(1, D, D)
    set_ple_weights(0, got[1], got[2])
    (c0, s0), (W_up[0],) = dwconv_fwd(g0, w_dw, b_dw, cln_g, cln_b, tm=tmc, comm=gather(('mlp_w_up', 0)))
    z1, x1, x1b = mm_nn("conv_out", [(s0, W_out, 0)],
                        [(b_out, 'row'), (x, 'tile'), (row(mix_g, 0), 'row'), (row(mix_b, 0), 'row')],
                        res_ln_outs, res_ln_epi(alpha), tm=tm, tn=D)
    u0, z2, x2, x2b, got = mlp_fwd(
        0, x1, x1b, down_comm=gather(('attn_w_q', 0), ('kv_w_k', 0), ('kv_w_v', 0), ('attn_w_o', 0),
                                     ('ple_w_proj', 1), ('ple_w_gate', 1)))
    W_qkv = jnp.concatenate([got[0].reshape(D, HD), got[1].reshape(D, KVD), got[2].reshape(D, KVD)], axis=1)[None]
    W_o = got[3].reshape(1, HD, D)
    set_ple_weights(1, got[4], got[5])
    x3, pp0, gg0, x3b = ple_fwd(0, x2, x2b, False)

    def qkv_epi(accs, ex, out, i):
        t = accs[0]
        c, su, sd = ex[0][...], ex[1][...], ex[2][...]
        out[0][...] = _rope_wide(t[:, 0:HD], c, su, sd, _rope_chunk).astype(BF16)
        widen_kv(_rope_wide(t[:, HD:HD + KVD], c, su, sd, _rope_chunk), t[:, HD + KVD:], out[1], n_kv)
    NQ = HD + 2 * KVD
    q1, kvx1 = mm_nn("qkv_rope", [(x3b, W_qkv, 0)], [(t, 'tab') for t in rope],
                     [(_sds((T, HD), BF16), 'rows'), (_sds((T, 4 * n_kv * LANES), BF16), 'rows')], qkv_epi,
                     tm=tm, tn=NQ)
    (o1, lse1), (W_up[1],) = attn_fwd(q1, kvx1, a['attn_sinks'], n_heads, n_kv, comm=gather(('mlp_w_up', 1)))
    z3, x4, x4b = mm_nn("attn_out", [(o1, W_o, 0)], [(x3, 'tile'), (row(mix_g, 1), 'row'), (row(mix_b, 1), 'row')],
                        res_ln_outs, res_ln_epi(alpha), tm=tm, tn=D)
    u1, z4, x5, x5b, _ = mlp_fwd(1, x4, x4b)
    dy, pp1, gg1, loss_row = ple_fwd(1, x5, x5b, True)
    loss_local = jnp.sum(loss_row)

    grads = {}

    def ln_bwd_epi(coef, with_colsum):
        def epi(acc, ex, out, i):
            d_x = acc + coef * ex[0][...]
            dz, dg, db = _ln_bwd(d_x, ex[1][...], ex[2][...])
            out[0][...] = dz
            out[1][...] = dz.astype(BF16)
            _init_or_add(out[2], i, dg)
            _init_or_add(out[3], i, db)
            if with_colsum:
                _init_or_add(out[4], i, jnp.sum(dz, axis=0, keepdims=True))
        return epi

    def ln_bwd_outs(with_colsum):
        outs = [(_sds((T, D), F32), 'tile'), (_sds((T, D), BF16), 'tile'), (_sds((1, D), F32), 'rowacc'),
                (_sds((1, D), F32), 'rowacc')]
        return outs + ([(_sds((1, D), F32), 'rowacc')] if with_colsum else [])

    def ple_bwd(li, d_out, xin, pp, gg, z_mlp, pair_specs=None, chip_keys=None):
        side = {}

        def gate_grads(ex):
            d = ex[0][...]
            sg = _sigmoid(ex[4][...].astype(F32))
            side['d_pp'] = (d * sg).astype(BF16)
            side['d_gg'] = (d * ex[3][...].astype(F32) * sg * (1.0 - sg)).astype(BF16)
            return side['d_gg']

        jobs = _Jobs([pair_stage(*pair_specs), chip_stage(*chip_keys)]) if pair_specs else None
        res = mm_nt(f"ple_dx_{li}", _sds((T, D), BF16), W_gate[li],
                    [(d_out, 'tile'), (z_mlp, 'tile'), (row(mlp_g, li), 'row'), (pp, 'tile'), (gg, 'tile')],
                    ln_bwd_outs(False), ln_bwd_epi(1.0, False), tm=tm, tko=D, tc=D, comm=jobs, a_pro=gate_grads,
                    dws=[(xin, lambda: side['d_gg']), (p_b[li], lambda: side['d_pp'])])
        if jobs is not None:
            res = split_hosted(res, jobs, chips=[1], pairs=[0])
        dz, dzb, dg, db, dw_gate, dw_proj = res
        grads[('mlp_ln_g', li)], grads[('mlp_ln_b', li)] = dg, db
        grads[('ple_w_gate', li)], grads[('ple_w_proj', li)] = dw_gate, dw_proj
        return dz, dzb

    recv = {}
    wqkv_cols = {'attn_w_q': (0, HD), 'kv_w_k': (HD, HD + KVD), 'kv_w_v': (HD + KVD, NQ)}

    def piece(name, li):
        if name == 'conv_w_in':
            return grads['conv_w_in']
        if name == 'mlp_w_up':
            return grads[('mlp_w_up', li)]
        if name == 'ple_w_proj':
            return jnp.transpose(grads[('ple_w_proj', li)][0].reshape(PLE, N_DEV, D // N_DEV), (1, 0, 2))
        if name in wqkv_cols:
            g = grads['w_qkv'][:, wqkv_cols[name][0]:wqkv_cols[name][1]]
        else:
            g = grads[name] if name in grads else grads[(name, li)]
            g = g[0]
        return g.reshape(N_DEV, g.shape[0] // N_DEV, g.shape[1])

    def pair_stage(*specs):
        job = _PairJob([piece(nm, li) for nm, li in specs])
        job.specs = specs
        return job

    sums = {}
    halved = ()

    def pair_done(pair_job, got):
        name = "chip_sum_" + "_".join(f"{nm}_{li}" for nm, li in pair_job.specs)
        for spec, total in zip(pair_job.specs, chip_sums(name, list(zip(pair_job.sources, got)))):
            sums[spec] = total

    def chip_stage(*keys):
        job = _ChipJob([sums[k] for k in keys])
        job.specs = keys
        return job

    def hosted(res, job):
        main, got = res
        for spec, r in zip(job.specs, got):
            recv[spec] = r
        return main

    def split_hosted(res, jobs, chips, pairs):
        main, got = res
        parts = jobs.split(got)
        for k in chips:
            hosted((None, parts[k]), jobs.jobs[k])
        for k in pairs:
            pair_done(jobs.jobs[k], parts[k])
        return main

    def mlp_bwd(li, dz, dzb, xin, u, z_mix, with_colsum, du_pairs, du_chips):
        jobs = _Jobs([pair_stage(*du_pairs)] + ([chip_stage(*du_chips)] if du_chips else []))
        du, grads[('mlp_w_up', li)], grads[('mlp_w_down', li)] = split_hosted(
            mlp_du_dw(f"mlp_du_{li}", dzb, W_down[li], u, xin, F // N_DEV, tm=tm2, tf=min(1024, F), comm=jobs),
            jobs, chips=[1] if du_chips else [], pairs=[0])
        pair = pair_stage(('mlp_w_up', li), ('mlp_w_down', li))
        res, got = mm_nt(f"mlp_dx_{li}", du, W_up[li], [(dz, 'tile'), (z_mix, 'tile'), (row(mix_g, li), 'row')],
                         ln_bwd_outs(with_colsum), ln_bwd_epi(alpha, with_colsum), tm=tm, tko=D, tc=F, comm=pair)
        pair_done(pair, got)
        grads[('mix_ln_g', li)], grads[('mix_ln_b', li)] = res[2], res[3]
        return res

    dz4, dz4b = ple_bwd(1, dy, x5b, pp1, gg1, z4)
    dz3, dz3b, _, _ = mlp_bwd(1, dz4, dz4b, x4b, u1, z3, False, [('ple_w_gate', 1), ('ple_w_proj', 1)], None)

    def do_epi(acc, ex, out, i):
        out[0][...] = acc.astype(BF16)
    job = chip_stage(('ple_w_gate', 1), ('ple_w_proj', 1))
    do1, grads['attn_w_o'] = hosted(
        mm_nt("attn_do", dz3b, W_o, [], [(_sds((T, HD), BF16), 'tile')], do_epi, tm=tm, tko=HD, tc=D, comm=job,
              dws=[(o1, None)]), job)
    jobs = _Jobs([chip_stage(('mlp_w_up', 1)), pair_stage(('attn_w_o', 0))])
    d_qkv, dkv, d_sinks = split_hosted(
        attn_bwd(q1, kvx1, do1, lse1, a['attn_sinks'], rope, n_heads, n_kv, comm=jobs), jobs, chips=[0], pairs=[1])
    d_qkv = dkv_finish(d_qkv, dkv, rope, HD, KVD, tm=tm)

    def dx3_epi(acc, ex, out, i):
        out[0][...] = acc + alpha * ex[0][...]
    job = chip_stage(('attn_w_o', 0))
    dx3, dw_qkv = hosted(mm_nt("attn_dx", d_qkv, W_qkv, [(dz3, 'tile')], [(_sds((T, D), F32), 'tile')], dx3_epi,
                               tm=tm, tko=D, tc=NQ, comm=job, dws=[(x3b, None)]), job)
    grads['w_qkv'] = dw_qkv[0]

    dz2, dz2b = ple_bwd(0, dx3, x2b, pp0, gg0, z2, [('attn_w_q', 0), ('kv_w_k', 0), ('kv_w_v', 0)],
                        [('mlp_w_down', 1)])
    dz1, dz1b, _, _, db_out = mlp_bwd(0, dz2, dz2b, x1b, u0, z1, True, [('ple_w_gate', 0), ('ple_w_proj', 0)],
                                      [('attn_w_q', 0), ('kv_w_k', 0), ('kv_w_v', 0)])

    def ds_epi(acc, ex, out, i):
        n = _ln_fwd(ex[0][...], ex[1][...], ex[2][...])
        sg = _sigmoid(n)
        dn = acc * (sg * (1.0 + n * (1.0 - sg)))
        dc, dg, db = _ln_bwd(dn, ex[0][...], ex[1][...])
        out[0][...] = dc
        _init_or_add(out[1], i, dg)
        _init_or_add(out[2], i, db)
    job = chip_stage(('ple_w_gate', 0), ('ple_w_proj', 0))
    dc0, d_cln_g, d_cln_b, grads['conv_w_out'] = hosted(
        mm_nt("conv_ds", dz1b, W_out, [(c0, 'tile'), (cln_g, 'row'), (cln_b, 'row')],
              [(_sds((T, D), F32), 'tile'), (_sds((1, D), F32), 'rowacc'), (_sds((1, D), F32), 'rowacc')], ds_epi,
              tm=tm, tko=D, tc=D, comm=job, dws=[(s0, None)]), job)
    job = chip_stage(('mlp_w_up', 0), ('mlp_w_down', 0))
    dh0, d_wdw, d_bdw, d_bin = hosted(dwconv_bwd(dc0, g0, ha0, hg0, w_dw, tm=tmc, comm=job), job)
    grads['conv_w_in'] = mm_tn("d_conv_win", xb, dh0, 2 * D // N_DEV, BF16, tm=T, tk=tw, tn=tw)

    pair = pair_stage(('conv_w_in', 0), ('conv_w_out', 0))
    pair_done(pair, comm_only("pair_last", [pair])[0])
    last_chip = chip_stage(('conv_w_in', 0), ('conv_w_out', 0))

    def own_rows(vec, rows_used, rows):
        arr = vec.reshape(N_DEV, rows_used, LANES)
        return jnp.pad(arr, ((0, 0), (0, rows - rows_used), (0, 0)))
    dwdw_dev = jnp.transpose(d_wdw.reshape(CONV_HALO, N_DEV, D // N_DEV), (1, 0, 2))
    lane_rows = D // N_DEV // LANES
    small_grad = jnp.concatenate([
        own_rows(d_bin, 2 * lane_rows, 8), dwdw_dev if lane_rows == 1 else dwdw_dev.reshape(N_DEV, -1, LANES),
        own_rows(d_bdw, lane_rows, 8), own_rows(d_cln_g, lane_rows, 8), own_rows(d_cln_b, lane_rows, 8),
        own_rows(db_out, lane_rows, 8)], axis=1)
    n_small = small_grad.shape[1]

    rep_local = {'mix_ln_g': jnp.concatenate([grads[('mix_ln_g', li)] for li in range(DEPTH)], axis=0),
                 'mix_ln_b': jnp.concatenate([grads[('mix_ln_b', li)] for li in range(DEPTH)], axis=0),
                 'mlp_ln_g': jnp.concatenate([grads[('mlp_ln_g', li)] for li in range(DEPTH)], axis=0),
                 'mlp_ln_b': jnp.concatenate([grads[('mlp_ln_b', li)] for li in range(DEPTH)], axis=0),
                 'attn_sinks': d_sinks}
    rep_grad = _rep_rows(rep_local, '')
    n_rep = rep_grad.shape[0]
    last = _Jobs([last_chip, _DirectJob([small_grad, jnp.broadcast_to(rep_grad[None], (N_DEV, n_rep, LANES))])])

    def dx_epi(acc, ex, out, i):
        out[0][...] = acc + alpha * ex[0][...]
    (grad_x,), got = mm_nt("conv_dx", dh0, W_in, [(dz1, 'tile')], [(_sds((T, D), F32), 'tile')], dx_epi,
                           tm=tm, tko=D, tc=D, comm=last)
    got, (recv_small, recv_rep) = last.split(got)
    hosted((None, got), last_chip)

    result = {}
    kinds = ('grad', 'delta', 'new_m', 'new_v')
    w, m, v = (_small_rows(a, pre)[None] for pre in ('', 'm_', 'v_'))
    for kind, arr in zip(kinds, adamw("adamw_small", [recv_small], w, m, v, ta=n_small)):
        for pname, val in _unpack_small(arr[0], a).items():
            result[(kind, pname)] = val
    w, m, v = (_rep_rows(a, pre)[None] for pre in ('', 'm_', 'v_'))
    for kind, arr in zip(kinds, adamw("adamw_rep", [recv_rep], w, m, v, ta=n_rep)):
        for pname, val in _unpack_rep(arr[0], a).items():
            result[(kind, pname)] = val
    for name in BIG_WEIGHTS:
        w, m, v = (shard3(a[pre + name]) for pre in ('', 'm_', 'v_'))
        recvs = [join_halves(recv[(name, li, 0)], recv[(name, li, 1)]) if name in halved else recv[(name, li)]
                 for li in range(w.shape[0])]
        for kind, arr in zip(kinds, adamw("adamw_" + name, recvs, w, m, v, ta=min(256, w.shape[1]))):
            result[(kind, name)] = arr.reshape(a[name].shape)

    loss = lax.psum(loss_local, ("x", "y", "c"))
    out = [loss, grad_x[None]]
    for kind in ('grad', 'delta', 'new_m', 'new_v'):
        out += [result[(kind, name)] for name in WEIGHT_NAMES]
    return tuple(out)


def _glu(x, W_in, b_in, T, D, tm, comm=None):
    n = W_in.shape[2]
    q = 2 if D // n % 2 == 0 else 1
    nt = D // (q * n)
    tn = q * n

    def body(x_ref, wa_ref, wg_ref, ba_ref, bg_ref, g_ref, ha_ref, hg_ref):
        xb = x_ref[...].astype(BF16)
        ha = jnp.concatenate([_dot(xb, wa_ref[s], ((1,), (0,))) for s in range(q)], axis=1) + ba_ref[...]
        hg = jnp.concatenate([_dot(xb, wg_ref[s], ((1,), (0,))) for s in range(q)], axis=1) + bg_ref[...]
        g_ref[...] = ha * _sigmoid(hg)
        ha_ref[...] = ha.astype(ha_ref.dtype)
        hg_ref[...] = hg.astype(hg_ref.dtype)

    return _call(
        body, comm, name="conv_in_glu", grid=(T // tm, nt),
        in_specs=[pl.BlockSpec((tm, D), lambda i, j: (i, 0)),
                  pl.BlockSpec((q, D, n), lambda i, j: (j, 0, 0)),
                  pl.BlockSpec((q, D, n), lambda i, j: (j + nt, 0, 0)),
                  pl.BlockSpec((1, tn), lambda i, j: (0, j)), pl.BlockSpec((1, tn), lambda i, j: (0, j + nt))],
        out_specs=[pl.BlockSpec((tm, tn), lambda i, j: (i, j))] * 3,
        out_shape=[_sds((T, D), F32), _sds((T, D), BF16), _sds((T, D), BF16)],
        sem=("parallel", "parallel"), operands=[x, W_in, W_in, b_in, b_in])


def kernel(x, p, conv_w_in, conv_b_in, conv_w_dw, conv_b_dw, conv_ln_g, conv_ln_b, conv_w_out, conv_b_out, kv_w_k, kv_w_v, attn_w_q, attn_sinks, attn_w_o, mix_ln_g, mix_ln_b, mlp_w_up, mlp_w_down, mlp_ln_g, mlp_ln_b, ple_w_proj, ple_w_gate, loss_target, m_conv_w_in, m_conv_b_in, m_conv_w_dw, m_conv_b_dw, m_conv_ln_g, m_conv_ln_b, m_conv_w_out, m_conv_b_out, m_kv_w_k, m_kv_w_v, m_attn_w_q, m_attn_sinks, m_attn_w_o, m_mix_ln_g, m_mix_ln_b, m_mlp_w_up, m_mlp_w_down, m_mlp_ln_g, m_mlp_ln_b, m_ple_w_proj, m_ple_w_gate, v_conv_w_in, v_conv_b_in, v_conv_w_dw, v_conv_b_dw, v_conv_ln_g, v_conv_ln_b, v_conv_w_out, v_conv_b_out, v_kv_w_k, v_kv_w_v, v_attn_w_q, v_attn_sinks, v_attn_w_o, v_mix_ln_g, v_mix_ln_b, v_mlp_w_up, v_mlp_w_down, v_mlp_ln_g, v_mlp_ln_b, v_ple_w_proj, v_ple_w_gate):
    return _step(dict(locals()))
```

```python
import functools
import math

import jax
import jax.numpy as jnp
from jax import lax
from jax.experimental import pallas as pl
from jax.experimental.pallas import tpu as pltpu

F32 = jnp.float32
BF16 = jnp.bfloat16

N_DEV = 8
HEAD_DIM = 64
ROPE_DIM = HEAD_DIM // 4
ROPE_HALF = ROPE_DIM // 2
ROPE_THETA = 500000.0
ATT_BLOCK = 128
CONV_WIDTH = 31
CONV_HALO = 32
CONV_ROWS = 64
LN_EPS = 1e-5
DEPTH = 2
DEEPNORM_ALPHA = (2 * DEPTH) ** 0.25
MASK_VALUE = -1e30

ADAM_LR = 0.001
ADAM_B1 = 0.9
ADAM_B2 = 0.999
ADAM_EPS = 1e-08
ADAM_WD = 0.01
ADAM_STEP = 10

LANES = 128
SUBLANES = 8
VMEM_LIMIT_BYTES = 52 * 1024 * 1024
TOKEN_TILE = 512
MESH_ID = pl.DeviceIdType.MESH
RELAY_AT_TENTHS = 5
FORWARD_AT_TENTHS = 8

WEIGHT_NAMES = ['conv_w_in', 'conv_b_in', 'conv_w_dw', 'conv_b_dw', 'conv_ln_g', 'conv_ln_b', 'conv_w_out',
                'conv_b_out', 'kv_w_k', 'kv_w_v', 'attn_w_q', 'attn_sinks', 'attn_w_o', 'mix_ln_g', 'mix_ln_b',
                'mlp_w_up', 'mlp_w_down', 'mlp_ln_g', 'mlp_ln_b', 'ple_w_proj', 'ple_w_gate']
BIG_WEIGHTS = ['conv_w_in', 'conv_w_out', 'kv_w_k', 'kv_w_v', 'attn_w_q', 'attn_w_o', 'mlp_w_up', 'mlp_w_down',
               'ple_w_proj', 'ple_w_gate']
SMALL_SHARDED = [('conv_b_in', 8), ('conv_w_dw', 32), ('conv_b_dw', 8), ('conv_ln_g', 8), ('conv_ln_b', 8),
                 ('conv_b_out', 8)]
REPLICATED = ['mix_ln_g', 'mix_ln_b', 'mlp_ln_g', 'mlp_ln_b', 'attn_sinks']


def _params(sem):
    return pltpu.CompilerParams(dimension_semantics=sem, vmem_limit_bytes=VMEM_LIMIT_BYTES)


def _sds(shape, dtype):
    return jax.ShapeDtypeStruct(shape, dtype)


def _my_place():
    x, y, c = lax.axis_index("x"), lax.axis_index("y"), lax.axis_index("c")
    return x, y, c, 4 * x + 2 * y + c


def _peers(x, y, c):
    out = []
    for dx in (0, 1):
        for dy in (0, 1):
            for dc in (0, 1):
                if dx or dy or dc:
                    px, py, pc = x ^ dx, y ^ dy, c ^ dc
                    out.append(((px, py, pc), 4 * px + 2 * py + pc))
    return out


N_CHIPS = 4


def _other_chips(x, y):
    return [((x ^ dx, y ^ dy), 2 * (x ^ dx) + (y ^ dy)) for dx, dy in ((1, 0), (0, 1), (1, 1))]


def _remote(src, dst, send, recv, to):
    return pltpu.make_async_remote_copy(src_ref=src, dst_ref=dst, send_sem=send, recv_sem=recv, device_id=to,
                                        device_id_type=MESH_ID)


def _wait_slabs(buf, count, send, recv, me, sent=True, received=True):
    part = buf.at[pl.ds(0, count)]
    cp = _remote(part, part, send, recv, me)
    if sent:
        cp.wait_send()
    if received:
        cp.wait_recv()


class _DirectJob:
    n_sems = 3

    def __init__(self, items):
        self.sources = list(items)
        self.dests = [_sds(it.shape, it.dtype) for it in items]
        self.n = len(items)

    def start(self, src, dst, send, recv, loc):
        x, y, c, me = _my_place()
        for k in range(self.n):
            here = dst[k].at[pl.ds(me, 1)]
            pltpu.make_async_copy(src[k].at[pl.ds(me, 1)], here, loc.at[k]).start()
            for peer, idx in _peers(x, y, c):
                _remote(src[k].at[pl.ds(idx, 1)], here, send.at[k], recv.at[k], peer).start()

    def relay(self, *refs):
        pass

    def forward(self, *refs):
        pass

    def finish(self, src, dst, send, recv, loc):
        x, y, c, me = _my_place()
        for k in range(self.n):
            pltpu.make_async_copy(src[k].at[pl.ds(me, 1)], dst[k].at[pl.ds(me, 1)], loc.at[k]).wait()
            _wait_slabs(dst[k], N_DEV - 1, send.at[k], recv.at[k], (x, y, c))


class _GatherJob:
    n_sems = 7

    def __init__(self, items):
        self.sources = [it.reshape((1,) + it.shape) for it in items]
        self.dests = [_sds((N_DEV,) + it.shape, it.dtype) for it in items]
        self.n = len(items)

    @staticmethod
    def _neighbours(x, y, c):
        out = []
        for dx, dy in ((c, 1 - c), (1 - c, c), (1, 1)):
            px, py = x ^ dx, y ^ dy
            out.append(((px, py, c), 4 * px + 2 * py + c))
        return out

    def start(self, src, dst, send_a, recv_a, send_b, recv_b, send_d, recv_d, loc):
        x, y, c, me = _my_place()
        (first, _), (second, _), _ = self._neighbours(x, y, c)
        for k in range(self.n):
            here = dst[k].at[pl.ds(me, 1)]
            pltpu.make_async_copy(src[k], here, loc.at[k]).start()
            _remote(src[k], here, send_d.at[k], recv_d.at[k], (x, y, 1 - c)).start()
            _remote(src[k], here, send_a.at[k], recv_a.at[k], first).start()
            _remote(src[k], here, send_b.at[k], recv_b.at[k], second).start()

    def relay(self, src, dst, send_a, recv_a, send_b, recv_b, send_d, recv_d, loc):
        x, y, c, me = _my_place()
        (_, first_idx), (second, _), _ = self._neighbours(x, y, c)
        for k in range(self.n):
            _wait_slabs(dst[k], 1, send_a.at[k], recv_a.at[k], (x, y, c), sent=False)
            rows = dst[k].at[pl.ds(first_idx, 1)]
            _remote(rows, rows, send_b.at[k], recv_b.at[k], second).start()

    def forward(self, src, dst, send_a, recv_a, send_b, recv_b, send_d, recv_d, loc):
        x, y, c, me = _my_place()
        for k in range(self.n):
            _wait_slabs(dst[k], 2, send_b.at[k], recv_b.at[k], (x, y, c), sent=False)
            for _, idx in self._neighbours(x, y, c):
                rows = dst[k].at[pl.ds(idx, 1)]
                _remote(rows, rows, send_d.at[k], recv_d.at[k], (x, y, 1 - c)).start()

    def finish(self, src, dst, send_a, recv_a, send_b, recv_b, send_d, recv_d, loc):
        x, y, c, me = _my_place()
        for k in range(self.n):
            pltpu.make_async_copy(src[k], dst[k].at[pl.ds(me, 1)], loc.at[k]).wait()
            _wait_slabs(dst[k], 1, send_a.at[k], recv_a.at[k], (x, y, c), received=False)
            _wait_slabs(dst[k], 2, send_b.at[k], recv_b.at[k], (x, y, c), received=False)
            _wait_slabs(dst[k], N_CHIPS, send_d.at[k], recv_d.at[k], (x, y, c))


class _PairJob:
    n_sems = 2

    def __init__(self, items):
        self.sources = list(items)
        self.dests = [_sds((N_CHIPS,) + it.shape[1:], it.dtype) for it in items]
        self.n = len(items)

    def start(self, src, dst, send, recv):
        x, y, c, me = _my_place()
        for k in range(self.n):
            for q in range(N_CHIPS):
                _remote(src[k].at[pl.ds(2 * q + 1 - c, 1)], dst[k].at[pl.ds(q, 1)], send.at[k], recv.at[k],
                        (x, y, 1 - c)).start()

    def relay(self, *refs):
        pass

    def forward(self, *refs):
        pass

    def finish(self, src, dst, send, recv):
        x, y, c, me = _my_place()
        for k in range(self.n):
            _wait_slabs(dst[k], N_CHIPS, send.at[k], recv.at[k], (x, y, c))


class _ChipJob:
    n_sems = 3

    def __init__(self, items):
        self.sources = list(items)
        self.dests = [_sds(it.shape, it.dtype) for it in items]
        self.n = len(items)

    def start(self, src, dst, send, recv, loc):
        x, y, c, me = _my_place()
        mine = 2 * x + y
        for k in range(self.n):
            here = dst[k].at[pl.ds(mine, 1)]
            pltpu.make_async_copy(src[k].at[pl.ds(mine, 1)], here, loc.at[k]).start()
            for (px, py), q in _other_chips(x, y):
                _remote(src[k].at[pl.ds(q, 1)], here, send.at[k], recv.at[k], (px, py, c)).start()

    def relay(self, *refs):
        pass

    def forward(self, *refs):
        pass

    def finish(self, src, dst, send, recv, loc):
        x, y, c, me = _my_place()
        mine = 2 * x + y
        for k in range(self.n):
            pltpu.make_async_copy(src[k].at[pl.ds(mine, 1)], dst[k].at[pl.ds(mine, 1)], loc.at[k]).wait()
            _wait_slabs(dst[k], N_CHIPS - 1, send.at[k], recv.at[k], (x, y, c))


class _Jobs:
    def __init__(self, jobs):
        self.jobs = list(jobs)
        self.sources = [s for job in self.jobs for s in job.sources]
        self.dests = [d for job in self.jobs for d in job.dests]
        self.n = len(self.sources)

    def _stage(self, stage, src, dst, sems):
        k0 = s0 = 0
        for job in self.jobs:
            getattr(job, stage)(src[k0:k0 + job.n], dst[k0:k0 + job.n], *sems[s0:s0 + job.n_sems])
            k0 += job.n
            s0 += job.n_sems

    def start(self, src, dst, *sems):
        self._stage('start', src, dst, sems)

    def relay(self, src, dst, *sems):
        self._stage('relay', src, dst, sems)

    def forward(self, src, dst, *sems):
        self._stage('forward', src, dst, sems)

    def finish(self, src, dst, *sems):
        self._stage('finish', src, dst, sems)

    def split(self, got):
        out, k0 = [], 0
        for job in self.jobs:
            out.append(got[k0:k0 + job.n])
            k0 += job.n
        return out


def _job_sems(job):
    if isinstance(job, _Jobs):
        return [s for part in job.jobs for s in _job_sems(part)]
    return [pltpu.SemaphoreType.DMA((job.n,))] * job.n_sems


def _call(body, comm, *, name, grid, in_specs, out_specs, out_shape, operands, sem, scratch_shapes=(), aliases=None):
    single = not isinstance(out_shape, (list, tuple))
    out_shape = [out_shape] if single else list(out_shape)
    out_specs = [out_specs] if single else list(out_specs)
    if comm is None:
        res = pl.pallas_call(body, name=name, grid=grid, in_specs=list(in_specs), out_specs=out_specs,
                             out_shape=out_shape, scratch_shapes=list(scratch_shapes),
                             input_output_aliases=aliases or {}, compiler_params=_params(sem))(*operands)
        return res[0] if single else res
    n_in, n_out, n_scr, n_c = len(in_specs), len(out_shape), len(scratch_shapes), comm.n
    any_spec = pl.BlockSpec(memory_space=pl.ANY)
    steps = math.prod(grid)
    early = min(steps - 1, (steps * RELAY_AT_TENTHS) // 10)
    mid = min(steps - 1, (steps * FORWARD_AT_TENTHS) // 10)

    def hosted(*refs):
        ins, c_src = refs[:n_in], refs[n_in:n_in + n_c]
        outs = refs[n_in + n_c:n_in + n_c + n_out]
        c_dst = refs[n_in + n_c + n_out:n_in + 2 * n_c + n_out]
        scr = refs[n_in + 2 * n_c + n_out:n_in + 2 * n_c + n_out + n_scr]
        sems = refs[n_in + 2 * n_c + n_out + n_scr:]
        step = pl.program_id(0)
        for d in range(1, len(grid)):
            step = step * grid[d] + pl.program_id(d)

        @pl.when(step == 0)
        def _():
            comm.start(c_src, c_dst, *sems)

        @pl.when(step == early)
        def _():
            comm.relay(c_src, c_dst, *sems)

        @pl.when(step == mid)
        def _():
            comm.forward(c_src, c_dst, *sems)

        body(*ins, *outs, *scr)

        @pl.when(step == steps - 1)
        def _():
            comm.finish(c_src, c_dst, *sems)

    res = pl.pallas_call(hosted, name=name, grid=grid, in_specs=list(in_specs) + [any_spec] * n_c,
                         out_specs=out_specs + [any_spec] * n_c, out_shape=out_shape + comm.dests,
                         scratch_shapes=list(scratch_shapes) + _job_sems(comm), input_output_aliases=aliases or {},
                         compiler_params=_params(("arbitrary",) * len(grid)))(*operands, *comm.sources)
    main = res[:n_out]
    return (main[0] if single else main), res[n_out:]


def comm_only(name, jobs):
    any_spec = pl.BlockSpec(memory_space=pl.ANY)
    n_all = sum(job.n for job in jobs)

    def body(*refs):
        srcs, dsts, sems = refs[:n_all], refs[n_all:2 * n_all], refs[2 * n_all:]
        parts, k0, s0 = [], 0, 0
        for job in jobs:
            parts.append((job, srcs[k0:k0 + job.n], dsts[k0:k0 + job.n], sems[s0:s0 + job.n_sems]))
            k0 += job.n
            s0 += job.n_sems
        for stage in ('start', 'relay', 'forward', 'finish'):
            for job, src, dst, sem in parts:
                getattr(job, stage)(src, dst, *sem)

    res = pl.pallas_call(body, name=name, in_specs=[any_spec] * n_all, out_specs=[any_spec] * n_all,
                         out_shape=[d for job in jobs for d in job.dests],
                         scratch_shapes=[s for job in jobs for s in _job_sems(job)],
                         )(*[s for job in jobs for s in job.sources])
    out, k0 = [], 0
    for job in jobs:
        out.append(res[k0:k0 + job.n])
        k0 += job.n
    return out


def _dot(a, b, dims):
    if a.dtype != BF16:
        a = a.astype(BF16)
    if b.dtype != BF16:
        b = b.astype(BF16)
    return lax.dot_general(a, b, (dims, ((), ())), preferred_element_type=F32)


def _sigmoid(v):
    return 1.0 / (1.0 + jnp.exp(-v))


def _sq_relu(u):
    return jnp.square(jnp.maximum(u.astype(F32), 0.0)).astype(BF16)


def _ln_stats(z):
    mu = jnp.mean(z, axis=-1, keepdims=True)
    zc = z - mu
    var = jnp.mean(zc * zc, axis=-1, keepdims=True)
    return zc * lax.rsqrt(var + LN_EPS)


def _ln_fwd(z, g, b):
    return _ln_stats(z) * g + b


def _ln_bwd(dy, z, g):
    xhat = _ln_stats(z)
    mu = jnp.mean(z, axis=-1, keepdims=True)
    zc = z - mu
    rstd = lax.rsqrt(jnp.mean(zc * zc, axis=-1, keepdims=True) + LN_EPS)
    dxh = dy * g
    m1 = jnp.mean(dxh, axis=-1, keepdims=True)
    m2 = jnp.mean(dxh * xhat, axis=-1, keepdims=True)
    dz = rstd * (dxh - m1 - xhat * m2)
    return dz, jnp.sum(dy * xhat, axis=0, keepdims=True), jnp.sum(dy, axis=0, keepdims=True)


def _extra_spec(shape, kind, tm, tn, ij):
    if kind == 'tile':
        return pl.BlockSpec((tm, tn), lambda *g: (ij(g)[0], ij(g)[1]))
    if kind in ('row', 'rowacc'):
        return pl.BlockSpec((1, tn), lambda *g: (0, ij(g)[1]))
    if kind == 'tab':
        return pl.BlockSpec((tm, LANES), lambda *g: (ij(g)[0], 0))
    if kind == 'rows':
        return pl.BlockSpec((tm, shape[1]), lambda *g: (ij(g)[0], 0))
    raise ValueError(kind)


def mm_nn(name, pairs, extras, outs, epi, *, tm, tn, tk=None, comm=None, a_pro=None):
    M = pairs[0][0].shape[0]
    N = pairs[0][1].shape[0] * pairs[0][1].shape[2]
    n_pairs = len(pairs)
    K0 = pairs[0][0].shape[1]
    tk = K0 if tk is None else tk
    nk = K0 // tk
    assert nk == 1 or n_pairs == 1
    assert M % tm == 0 and N % tn == 0 and K0 % tk == 0
    has_rowacc = any(kind == 'rowacc' for _, kind in outs)
    assert not has_rowacc or (N == tn and nk == 1)
    in_specs, operands, slabs = [], [], []
    for a, b, off in pairs:
        K = a.shape[1]
        ktile = K if n_pairs > 1 else tk
        n = b.shape[2]
        assert b.shape[1] == K
        in_specs.append(pl.BlockSpec((tm, ktile), lambda i, j, k: (i, k)))
        if tn <= n:
            assert n % tn == 0
            r = n // tn
            in_specs.append(pl.BlockSpec((None, ktile, tn),
                                         lambda i, j, k, r=r, off=off: ((j + off) // r, k, (j + off) % r)))
            slabs.append(0)
        else:
            assert tn % n == 0
            in_specs.append(pl.BlockSpec((tn // n, ktile, n), lambda i, j, k, off=off: (j + off, k, 0)))
            slabs.append(tn // n)
        operands += [a, b]
    ij = lambda g: (g[0], g[1])
    for arr, kind in extras:
        in_specs.append(_extra_spec(arr.shape, kind, tm, tn, ij))
        operands.append(arr)
    out_specs = [_extra_spec(o.shape, kind, tm, tn, ij) for o, kind in outs]
    n_ex, n_out = len(extras), len(outs)

    def pair_dot(ab, q):
        a = ab[2 * q][...]
        if a_pro is not None:
            a = a_pro(a)
        if not slabs[q]:
            return _dot(a, ab[2 * q + 1][...], ((1,), (0,)))
        return jnp.concatenate([_dot(a, ab[2 * q + 1][s], ((1,), (0,))) for s in range(slabs[q])], axis=1)

    def body(*refs):
        ab = refs[:2 * n_pairs]
        ex = refs[2 * n_pairs:2 * n_pairs + n_ex]
        out = refs[2 * n_pairs + n_ex:2 * n_pairs + n_ex + n_out]
        i = pl.program_id(0)
        if nk == 1:
            accs = [pair_dot(ab, q) for q in range(n_pairs)]
            epi(accs, ex, out, i)
        else:
            acc_ref = refs[-1]
            k = pl.program_id(2)

            @pl.when(k == 0)
            def _():
                acc_ref[...] = jnp.zeros_like(acc_ref)

            acc_ref[...] += pair_dot(ab, 0)

            @pl.when(k == nk - 1)
            def _():
                epi([acc_ref[...]], ex, out, i)

    scratch = [pltpu.VMEM((tm, tn), F32)] if nk > 1 else []
    sem = ("arbitrary",) * 3 if has_rowacc else ("parallel", "parallel", "arbitrary")
    return _call(body, comm, name=name, grid=(M // tm, N // tn, nk), in_specs=in_specs, out_specs=out_specs,
                 out_shape=[o for o, _ in outs], scratch_shapes=scratch, sem=sem, operands=operands)


def mm_nt(name, a, b, extras, outs, epi, *, tm, tko, tc, comm=None, a_pro=None, dws=()):
    M, N = a.shape
    J, K, n = b.shape
    assert J * n == N and M % tm == 0 and K % tko == 0 and N % tc == 0
    nc = N // tc
    assert a_pro is None or (nc == 1 and tc <= n and K == tko)
    assert not dws or (nc == 1 and K == tko and tc <= n)
    n_dw, nt = len(dws), M // tm
    has_rowacc = any(kind == 'rowacc' for _, kind in outs)
    assert not has_rowacc or K == tko
    if tc <= n:
        assert n % tc == 0
        r = n // tc
        slabs = 0
        b_spec = pl.BlockSpec((None, tko, tc), lambda i, j, c: (c // r, j, c % r))
    else:
        assert tc % n == 0
        slabs = tc // n
        b_spec = pl.BlockSpec((slabs, tko, n), lambda i, j, c: (c, j, 0))
    n_ab = 2 if a_pro is None else 1
    in_specs = ([pl.BlockSpec((tm, tc), lambda i, j, c: (i, c))] if a_pro is None else []) + [b_spec]

    def nt_dot(a_ref, b_ref):
        if not slabs:
            return _dot(a_ref[...], b_ref[...], ((1,), (1,)))
        acc = _dot(a_ref[:, 0:n], b_ref[0], ((1,), (1,)))
        for s in range(1, slabs):
            acc = acc + _dot(a_ref[:, s * n:(s + 1) * n], b_ref[s], ((1,), (1,)))
        return acc

    ij = lambda g: (g[0], g[1])
    operands = [a, b] if a_pro is None else [b]
    for arr, kind in extras:
        in_specs.append(_extra_spec(arr.shape, kind, tm, tko, ij))
        operands.append(arr)
    out_specs = [_extra_spec(o.shape, kind, tm, tko, ij) for o, kind in outs]
    n_ex, n_out = len(extras), len(outs)
    out_shape = [o for o, _ in outs]
    for lhs, _ in dws:
        in_specs.append(pl.BlockSpec((tm, lhs.shape[1]), lambda i, j, c: (i, 0)))
        operands.append(lhs)
        out_specs.append(pl.BlockSpec((None, lhs.shape[1], N), lambda i, j, c: (0, 0, 0)))
        out_shape.append(_sds((1, lhs.shape[1], N), BF16))

    def weight_grads(refs, a_tile, i):
        lhs_refs = refs[n_ab + n_ex:n_ab + n_ex + n_dw]
        dw_refs = refs[n_ab + n_ex + n_dw + n_out:n_ab + n_ex + n_dw + n_out + n_dw]
        acc_refs = refs[len(refs) - n_dw:]
        for (_, rhs_fn), lhs_ref, dw_ref, acc_ref in zip(dws, lhs_refs, dw_refs, acc_refs):
            part = _dot(lhs_ref[...], a_tile if rhs_fn is None else rhs_fn(), ((0,), (0,)))
            _init_or_add(acc_ref, i, part)

            @pl.when(i == nt - 1)
            def _(dw_ref=dw_ref, acc_ref=acc_ref):
                dw_ref[...] = acc_ref[...].astype(dw_ref.dtype)

    def body(*refs):
        ex = refs[n_ab:n_ab + n_ex]
        out = refs[n_ab + n_ex + n_dw:n_ab + n_ex + n_dw + n_out]
        i = pl.program_id(0)
        if a_pro is not None:
            a_tile = a_pro(ex)
            epi(_dot(a_tile, refs[0][...], ((1,), (1,))), ex, out, i)
            weight_grads(refs, a_tile, i)
            return
        a_ref, b_ref = refs[:2]
        if nc == 1:
            epi(nt_dot(a_ref, b_ref), ex, out, i)
            if dws:
                weight_grads(refs, a_ref[...], i)
        else:
            acc_ref = refs[-1]
            c = pl.program_id(2)

            @pl.when(c == 0)
            def _():
                acc_ref[...] = jnp.zeros_like(acc_ref)

            acc_ref[...] += nt_dot(a_ref, b_ref)

            @pl.when(c == nc - 1)
            def _():
                epi(acc_ref[...], ex, out, i)

    scratch = ([pltpu.VMEM((tm, tko), F32)] if nc > 1 else []) + [pltpu.VMEM((lhs.shape[1], N), F32) for lhs, _ in dws]
    sem = ("arbitrary",) * 3 if (has_rowacc or dws) else ("parallel", "parallel", "arbitrary")
    return _call(body, comm, name=name, grid=(M // tm, K // tko, nc), in_specs=in_specs, out_specs=out_specs,
                 out_shape=out_shape, scratch_shapes=scratch, sem=sem, operands=operands)


def mm_tn(name, a, d, n, out_dtype, *, tm, tk, tn, comm=None):
    M, K = a.shape
    N = d.shape[1]
    assert d.shape[0] == M and N % n == 0 and N % tn == 0 and K % tk == 0 and M % tm == 0
    nm = M // tm
    if tn <= n:
        assert n % tn == 0
        r = n // tn
        slabs = 0
        o_spec = pl.BlockSpec((None, tk, tn), lambda kk, j, m: (j // r, kk, j % r))
    else:
        assert tn % n == 0
        slabs = tn // n
        o_spec = pl.BlockSpec((slabs, tk, n), lambda kk, j, m: (j, kk, 0))

    def write(o_ref, acc):
        if not slabs:
            o_ref[...] = acc.astype(o_ref.dtype)
        else:
            for s in range(slabs):
                o_ref[s] = acc[:, s * n:(s + 1) * n].astype(o_ref.dtype)

    def body(a_ref, d_ref, o_ref, *scratch):
        if nm == 1:
            write(o_ref, _dot(a_ref[...], d_ref[...], ((0,), (0,))))
            return
        acc_ref, = scratch
        m = pl.program_id(2)

        @pl.when(m == 0)
        def _():
            acc_ref[...] = jnp.zeros_like(acc_ref)

        acc_ref[...] += _dot(a_ref[...], d_ref[...], ((0,), (0,)))

        @pl.when(m == nm - 1)
        def _():
            write(o_ref, acc_ref[...])

    return _call(
        body, comm, name=name, grid=(K // tk, N // tn, nm),
        in_specs=[pl.BlockSpec((tm, tk), lambda kk, j, m: (m, kk)), pl.BlockSpec((tm, tn), lambda kk, j, m: (m, j))],
        out_specs=o_spec, out_shape=_sds((N // n, K, n), out_dtype),
        scratch_shapes=[pltpu.VMEM((tk, tn), F32)] if nm > 1 else [],
        sem=("parallel", "parallel", "arbitrary"), operands=[a, d])


def mlp_du_dw(name, dzb, w_down, u, xin, n, *, tm, tf, comm=None):
    T, D = dzb.shape
    F = u.shape[1]
    assert T % tm == 0 and F % tf == 0 and tf % n == 0
    slabs, nt = tf // n, T // tm

    def body(dz_ref, w_ref, u_ref, x_ref, du_ref, dwu_ref, dwd_ref, up_acc, down_acc):
        i = pl.program_id(1)
        dz = dz_ref[...]
        da = _dot(dz, w_ref[...], ((1,), (1,)))
        relu = jnp.maximum(u_ref[...].astype(F32), 0.0)
        du = (da * (2.0 * relu)).astype(BF16)
        du_ref[...] = du
        _init_or_add(up_acc, i, _dot(x_ref[...], du, ((0,), (0,))))
        _init_or_add(down_acc, i, _dot(jnp.square(relu).astype(BF16), dz, ((0,), (0,))))

        @pl.when(i == nt - 1)
        def _():
            for s in range(slabs):
                dwu_ref[s] = up_acc[:, s * n:(s + 1) * n].astype(dwu_ref.dtype)
            dwd_ref[...] = down_acc[...].astype(dwd_ref.dtype)

    hidden = pl.BlockSpec((tm, tf), lambda j, i: (i, j))
    tokens = pl.BlockSpec((tm, D), lambda j, i: (i, 0))
    return _call(
        body, comm, name=name, grid=(F // tf, nt),
        in_specs=[tokens, pl.BlockSpec((None, tf, D), lambda j, i: (0, j, 0)), hidden, tokens],
        out_specs=[hidden, pl.BlockSpec((slabs, D, n), lambda j, i: (j, 0, 0)),
                   pl.BlockSpec((None, tf, D), lambda j, i: (0, j, 0))],
        out_shape=[_sds((T, F), BF16), _sds((F // n, D, n), BF16), _sds((1, F, D), BF16)],
        scratch_shapes=[pltpu.VMEM((D, tf), F32), pltpu.VMEM((tf, D), F32)], sem=("arbitrary", "arbitrary"),
        operands=[dzb, w_down, u, xin])


def _init_or_add(ref, i, value):
    @pl.when(i == 0)
    def _():
        ref[...] = value

    @pl.when(i > 0)
    def _():
        ref[...] += value


def _rope_tables(T):
    pos = jnp.arange(T, dtype=F32)
    inv_freq = ROPE_THETA ** (-jnp.arange(0, ROPE_DIM, 2, dtype=F32) / ROPE_DIM)
    ang = pos[:, None] * inv_freq[None, :]
    cos, sin = jnp.cos(ang), jnp.sin(ang)
    ones = jnp.ones((T, HEAD_DIM - ROPE_DIM), F32)
    zeros = jnp.zeros((T, HEAD_DIM - ROPE_DIM), F32)
    zh = jnp.zeros((T, ROPE_HALF), F32)
    c_head = jnp.concatenate([cos, cos, ones], axis=1)
    s_up = jnp.concatenate([-sin, zh, zeros], axis=1)
    s_dn = jnp.concatenate([zh, sin, zeros], axis=1)
    rep = LANES // HEAD_DIM
    return tuple(jnp.tile(t, (1, rep)) for t in (c_head, s_up, s_dn))


def _rope_chunk(t, c, s_up, s_dn):
    return t * c + pltpu.roll(t, LANES - ROPE_HALF, 1) * s_up + pltpu.roll(t, ROPE_HALF, 1) * s_dn


def _rope_chunk_bwd(d, c, s_up, s_dn):
    return d * c + pltpu.roll(d * s_up, ROPE_HALF, 1) + pltpu.roll(d * s_dn, LANES - ROPE_HALF, 1)


def _rope_wide(t, c, s_up, s_dn, fn):
    chunks = [fn(t[:, q * LANES:(q + 1) * LANES], c, s_up, s_dn) for q in range(t.shape[1] // LANES)]
    return chunks[0] if len(chunks) == 1 else jnp.concatenate(chunks, axis=1)


def _taps_by_residue(first):
    groups = {}
    for o in range(first, first + CONV_WIDTH):
        groups.setdefault(o % SUBLANES, []).append(o)
    return sorted(groups.items())


def _shifted_taps(win_ref, span_ref, r0, res, offs, rb, ls):
    if res == 0:
        return [functools.partial(lambda o: win_ref[pl.ds(r0 + o, rb), ls], o) for o in offs]
    n = rb + offs[-1] - res
    span_ref[0:n, :] = win_ref[pl.ds(r0 + res, n), ls]
    return [functools.partial(lambda o: span_ref[pl.ds(o - res, rb), :], o) for o in offs]


def _halo_before_spec(tm, D):
    return pl.BlockSpec((CONV_HALO, D), lambda i: (jnp.maximum(i * (tm // CONV_HALO) - 1, 0), 0))


def dwconv_fwd(g, w_dw, b_dw, ln_g, ln_b, *, tm, comm=None):
    T, D = g.shape
    nl = D // LANES

    rb = min(CONV_ROWS, tm)

    def body(g_ref, gh_ref, w_ref, b_ref, lg_ref, lb_ref, c_ref, s_ref, win_ref, span_ref):
        i = pl.program_id(0)
        win_ref[0:CONV_HALO, :] = jnp.where(i > 0, gh_ref[...], 0.0)
        win_ref[CONV_HALO:, :] = g_ref[...]

        def lane_chunk(q, carry):
            ls = pl.ds(pl.multiple_of(q * LANES, LANES), LANES)
            for r0 in range(0, tm, rb):
                acc = jnp.broadcast_to(b_ref[:, ls], (rb, LANES))
                for res, offs in _taps_by_residue(CONV_HALO - (CONV_WIDTH - 1)):
                    taps = _shifted_taps(win_ref, span_ref, r0, res, offs, rb, ls)
                    for o, tap in zip(offs, taps):
                        k = o - (CONV_HALO - (CONV_WIDTH - 1))
                        acc = acc + tap() * w_ref[k:k + 1, ls]
                c_ref[pl.ds(r0, rb), ls] = acc
            return carry

        lax.fori_loop(0, nl, lane_chunk, 0)
        n = _ln_fwd(c_ref[...], lg_ref[...], lb_ref[...])
        s_ref[...] = (n * _sigmoid(n)).astype(s_ref.dtype)

    row = pl.BlockSpec((1, D), lambda i: (0, 0))
    return _call(
        body, comm, name="dwconv_fwd", grid=(T // tm,),
        in_specs=[pl.BlockSpec((tm, D), lambda i: (i, 0)), _halo_before_spec(tm, D),
                  pl.BlockSpec((CONV_HALO, D), lambda i: (0, 0)), row, row, row],
        out_specs=[pl.BlockSpec((tm, D), lambda i: (i, 0)), pl.BlockSpec((tm, D), lambda i: (i, 0))],
        out_shape=[_sds((T, D), F32), _sds((T, D), BF16)],
        scratch_shapes=[pltpu.VMEM((tm + CONV_HALO, D), F32), pltpu.VMEM((rb + CONV_HALO, LANES), F32)],
        sem=("parallel",), operands=[g, g, w_dw, b_dw, ln_g, ln_b])


def dwconv_bwd(dc, g, ha, hg, w_dw, *, tm, comm=None):
    T, D = g.shape
    nl = D // LANES
    last = T // CONV_HALO - 1
    nt = T // tm

    rb = min(CONV_ROWS, tm)

    def body(dc_ref, dcn_ref, g_ref, gh_ref, ha_ref, hg_ref, w_ref, dh_ref, dw_ref, dbdw_ref, dbin_ref,
             win_ref, dwin_ref, dg_ref, dwp_ref, span_ref):
        i = pl.program_id(0)
        win_ref[0:CONV_HALO, :] = jnp.where(i > 0, gh_ref[...], 0.0)
        win_ref[CONV_HALO:, :] = g_ref[...]
        dwin_ref[0:tm, :] = dc_ref[...]
        dwin_ref[tm:, :] = jnp.where(i < nt - 1, dcn_ref[...], 0.0)

        @pl.when(i == 0)
        def _():
            dwp_ref[...] = jnp.zeros_like(dwp_ref)

        first = CONV_HALO - (CONV_WIDTH - 1)

        def lane_chunk(q, carry):
            ls = pl.ds(pl.multiple_of(q * LANES, LANES), LANES)
            for r0 in range(0, tm, rb):
                acc = jnp.zeros((rb, LANES), F32)
                for res, offs in _taps_by_residue(0):
                    taps = _shifted_taps(dwin_ref, span_ref, r0, res, offs, rb, ls)
                    for o, tap in zip(offs, taps):
                        k = CONV_WIDTH - 1 - o
                        acc = acc + tap() * w_ref[k:k + 1, ls]
                dg_ref[pl.ds(r0, rb), ls] = acc
                dcv = dwin_ref[pl.ds(r0, rb), ls]
                for res, offs in _taps_by_residue(first):
                    taps = _shifted_taps(win_ref, span_ref, r0, res, offs, rb, ls)
                    for o, tap in zip(offs, taps):
                        k = o - first
                        prod = dcv * tap()
                        dwp_ref[k, :, ls] += jnp.sum(prod.reshape(rb // SUBLANES, SUBLANES, LANES), axis=0)
            return carry

        lax.fori_loop(0, nl, lane_chunk, 0)

        @pl.when(i == nt - 1)
        def _():
            for k in range(CONV_WIDTH):
                dw_ref[k:k + 1, :] = jnp.sum(dwp_ref[k], axis=0, keepdims=True)
            dw_ref[CONV_WIDTH:, :] = jnp.zeros((CONV_HALO - CONV_WIDTH, D), F32)
        dg = dg_ref[...]
        ha = ha_ref[...].astype(F32)
        sg = _sigmoid(hg_ref[...].astype(F32))
        d_ha = dg * sg
        d_hg = dg * ha * sg * (1.0 - sg)
        dh_ref[:, 0:D] = d_ha.astype(dh_ref.dtype)
        dh_ref[:, D:] = d_hg.astype(dh_ref.dtype)
        _init_or_add(dbdw_ref, i, jnp.sum(dc_ref[...], axis=0, keepdims=True))
        _init_or_add(dbin_ref, i, jnp.concatenate([jnp.sum(d_ha, axis=0, keepdims=True),
                                                   jnp.sum(d_hg, axis=0, keepdims=True)], axis=1))

    tile = pl.BlockSpec((tm, D), lambda i: (i, 0))
    return _call(
        body, comm, name="dwconv_bwd", grid=(nt,),
        in_specs=[tile,
                  pl.BlockSpec((CONV_HALO, D), lambda i: (jnp.minimum((i + 1) * (tm // CONV_HALO), last), 0)),
                  tile, _halo_before_spec(tm, D),
                  tile, tile, pl.BlockSpec((CONV_HALO, D), lambda i: (0, 0))],
        out_specs=[pl.BlockSpec((tm, 2 * D), lambda i: (i, 0)), pl.BlockSpec((CONV_HALO, D), lambda i: (0, 0)),
                   pl.BlockSpec((1, D), lambda i: (0, 0)), pl.BlockSpec((1, 2 * D), lambda i: (0, 0))],
        out_shape=[_sds((T, 2 * D), BF16), _sds((CONV_HALO, D), F32), _sds((1, D), F32), _sds((1, 2 * D), F32)],
        scratch_shapes=[pltpu.VMEM((tm + CONV_HALO, D), F32), pltpu.VMEM((tm + CONV_HALO, D), F32),
                        pltpu.VMEM((tm, D), F32), pltpu.VMEM((CONV_WIDTH, SUBLANES, D), F32),
                        pltpu.VMEM((rb + CONV_HALO, LANES), F32)],
        sem=("arbitrary",), operands=[dc, dc, g, g, ha, hg, w_dw])


def _attn_specs(HD, W):
    B = ATT_BLOCK
    return [pl.BlockSpec((B, HD), lambda n: (n, 0)),
            pl.BlockSpec((B, 4 * W), lambda n: (n, 0)),
            pl.BlockSpec((B, 4 * W), lambda n: (jnp.maximum(n - 1, 0), 0))]


def _band_mask(n):
    r = lax.broadcasted_iota(jnp.int32, (ATT_BLOCK, 2 * ATT_BLOCK), 0)
    j = lax.broadcasted_iota(jnp.int32, (ATT_BLOCK, 2 * ATT_BLOCK), 1)
    return (j > r) & (j <= r + ATT_BLOCK) & ((n > 0) | (j >= ATT_BLOCK))


def _band(kvc_ref, kvp_ref, part, g, parity, W):
    lanes = slice((2 * part + parity) * W + g * LANES, (2 * part + parity) * W + (g + 1) * LANES)
    return jnp.concatenate([kvp_ref[:, lanes], kvc_ref[:, lanes]], axis=0)


def _half_mask(parity):
    lane = lax.broadcasted_iota(jnp.int32, (1, LANES), 1)
    return (lane < HEAD_DIM) if parity == 0 else (lane >= HEAD_DIM)


def widen_kv(k, v, out_ref, n_kv):
    W = n_kv * LANES
    low = _half_mask(0)
    for part, src in enumerate((k, v)):
        for cg in range(n_kv * HEAD_DIM // LANES):
            chunk = src[:, cg * LANES:(cg + 1) * LANES]
            swapped = pltpu.roll(chunk, HEAD_DIM, 1)
            for g, lo, hi in ((2 * cg, chunk, swapped), (2 * cg + 1, swapped, chunk)):
                base = 2 * part * W + g * LANES
                out_ref[:, base:base + LANES] = jnp.where(low, lo, 0.0).astype(out_ref.dtype)
                out_ref[:, base + W:base + W + LANES] = jnp.where(low, 0.0, hi).astype(out_ref.dtype)


def attn_fwd(q, kvx, sinks, n_heads, n_kv, comm=None):
    T, HD = q.shape
    W = n_kv * LANES
    B = ATT_BLOCK
    chunks_per_group = n_heads // n_kv // 2
    scale = 1.0 / math.sqrt(HEAD_DIM)

    def body(q_ref, kvc_ref, kvp_ref, sink_ref, o_ref, lse_ref):
        n = pl.program_id(0)
        mask = jnp.tile(_band_mask(n), (n_heads, 1))
        s = jnp.concatenate(
            [_dot(q_ref[:, (h // 2) * LANES:(h // 2 + 1) * LANES],
                  _band(kvc_ref, kvp_ref, 0, h // 2 // chunks_per_group, h % 2, W), ((1,), (1,)))
             for h in range(n_heads)], axis=0)
        sink = jnp.concatenate([jnp.broadcast_to(sink_ref[:, h:h + 1], (B, 1)) for h in range(n_heads)], axis=0)
        s = jnp.where(mask, s * scale, MASK_VALUE)
        m = jnp.maximum(jnp.max(s, axis=-1, keepdims=True), sink)
        e = jnp.exp(s - m)
        total = _dot(e, jnp.ones((2 * B, LANES), BF16), ((1,), (0,))) + jnp.exp(sink - m)
        lse = m + jnp.log(total[:, 0:1])
        inv = 1.0 / total
        probs = (e * jnp.concatenate([inv, inv], axis=1)).astype(BF16)
        for c in range(n_heads // 2):
            g = c // chunks_per_group
            out = (_dot(probs[2 * c * B:(2 * c + 1) * B], _band(kvc_ref, kvp_ref, 1, g, 0, W), ((1,), (0,)))
                   + _dot(probs[(2 * c + 1) * B:(2 * c + 2) * B], _band(kvc_ref, kvp_ref, 1, g, 1, W), ((1,), (0,))))
            o_ref[:, c * LANES:(c + 1) * LANES] = out.astype(o_ref.dtype)
        lse_ref[...] = jnp.concatenate([lse[h * B:(h + 1) * B] for h in range(n_heads)], axis=1)

    return _call(
        body, comm, name="attn_fwd", grid=(T // B,),
        in_specs=_attn_specs(HD, W) + [pl.BlockSpec((1, n_heads), lambda n: (0, 0))],
        out_specs=[pl.BlockSpec((B, HD), lambda n: (n, 0)), pl.BlockSpec((B, n_heads), lambda n: (n, 0))],
        out_shape=[_sds((T, HD), BF16), _sds((T, n_heads), F32)],
        sem=("parallel",), operands=[q, kvx, kvx, sinks])


def attn_bwd(q, kvx, do, lse, sinks, rope, n_heads, n_kv, comm=None):
    T, HD = q.shape
    KVD = n_kv * HEAD_DIM
    W = n_kv * LANES
    B = ATT_BLOCK
    chunks_per_group = n_heads // n_kv // 2
    scale = 1.0 / math.sqrt(HEAD_DIM)
    nb = T // B

    def body(q_ref, kvc_ref, kvp_ref, do_ref, lse_ref, sink_ref, c_ref, su_ref, sd_ref, dq_ref, dkv_ref, dsink_ref):
        n = pl.program_id(0)

        @pl.when(n == 0)
        def _():
            dkv_ref[...] = jnp.zeros_like(dkv_ref)
            dsink_ref[...] = jnp.zeros_like(dsink_ref)

        def chunk(ref, h):
            return ref[:, (h // 2) * LANES:(h // 2 + 1) * LANES]

        def band(part, h):
            return _band(kvc_ref, kvp_ref, part, h // 2 // chunks_per_group, h % 2, W)

        def stack(per_head):
            return jnp.concatenate([per_head(h) for h in range(n_heads)], axis=0)

        mask = jnp.tile(_band_mask(n), (n_heads, 1))
        s = stack(lambda h: _dot(chunk(q_ref, h), band(0, h), ((1,), (1,))))
        dp = stack(lambda h: _dot(chunk(do_ref, h), band(1, h), ((1,), (1,))))
        lse = stack(lambda h: lse_ref[:, h:h + 1])
        sink = stack(lambda h: jnp.broadcast_to(sink_ref[:, h:h + 1], (B, 1)))
        probs = jnp.exp(jnp.where(mask, s * scale, MASK_VALUE) - lse)
        delta = jnp.sum(probs * dp, axis=-1, keepdims=True)
        ds = (probs * (dp - delta) * scale).astype(BF16)
        probs = probs.astype(BF16)
        sink_term = jnp.exp(sink - lse) * delta
        dsk = [-jnp.sum(sink_term[h * B:(h + 1) * B], axis=0, keepdims=True) for h in range(n_heads)]

        dk_wide, dv_wide = [None] * n_kv, [None] * n_kv
        for c in range(n_heads // 2):
            g = c // chunks_per_group
            dq2 = None
            for h in (2 * c, 2 * c + 1):
                half = _half_mask(h % 2)
                q2, do2 = chunk(q_ref, h), chunk(do_ref, h)
                ds_h, p_h = ds[h * B:(h + 1) * B], probs[h * B:(h + 1) * B]
                part = _dot(ds_h, band(0, h), ((1,), (0,)))
                dq2 = part if dq2 is None else dq2 + part
                dk_h = _dot(ds_h, jnp.where(half, q2, jnp.zeros_like(q2)), ((0,), (0,)))
                dv_h = _dot(p_h, jnp.where(half, do2, jnp.zeros_like(do2)), ((0,), (0,)))
                dk_wide[g] = dk_h if dk_wide[g] is None else dk_wide[g] + dk_h
                dv_wide[g] = dv_h if dv_wide[g] is None else dv_wide[g] + dv_h
            dq_ref[:, c * LANES:(c + 1) * LANES] = _rope_chunk_bwd(
                dq2, c_ref[...], su_ref[...], sd_ref[...]).astype(dq_ref.dtype)

        def fold(wide):
            low = _half_mask(0)
            both = [w + pltpu.roll(w, HEAD_DIM, 1) for w in wide]
            return jnp.concatenate([jnp.where(low, both[2 * cg], both[2 * cg + 1]) for cg in range(n_kv // 2)], axis=1)

        dkv = jnp.concatenate([fold(dk_wide), fold(dv_wide)], axis=1)
        prev = pl.ds(pl.multiple_of(jnp.maximum(n - 1, 0) * B, B), B)
        cur = pl.ds(pl.multiple_of(n * B, B), B)
        dkv_ref[prev, :] += dkv[0:B, :]
        dkv_ref[cur, :] += dkv[B:, :]
        dsink_ref[...] += jnp.concatenate(dsk, axis=1)

    tab = pl.BlockSpec((B, LANES), lambda n: (n, 0))
    return _call(
        body, comm, name="attn_bwd", grid=(nb,),
        in_specs=_attn_specs(HD, W) + [pl.BlockSpec((B, HD), lambda n: (n, 0)),
                                       pl.BlockSpec((B, n_heads), lambda n: (n, 0)),
                                       pl.BlockSpec((1, n_heads), lambda n: (0, 0)), tab, tab, tab],
        out_specs=[pl.BlockSpec((B, HD), lambda n: (n, 0)), pl.BlockSpec((T, 2 * KVD), lambda n: (0, 0)),
                   pl.BlockSpec((1, n_heads), lambda n: (0, 0))],
        out_shape=[_sds((T, HD), BF16), _sds((T, 2 * KVD), F32), _sds((1, n_heads), F32)],
        sem=("arbitrary",), operands=[q, kvx, kvx, do, lse, sinks, *rope])


def chip_sums(name, pairs):
    n = len(pairs)

    def body(core_ref, *refs):
        del core_ref
        for k in range(n):
            refs[2 * n + k][...] = (refs[2 * k][...].astype(F32) + refs[2 * k + 1][...].astype(F32)
                                    ).astype(refs[2 * n + k].dtype)

    in_specs, out_specs, out_shape, operands = [], [], [], []
    for g, p_sib in pairs:
        _, a, b = g.shape
        in_specs += [pl.BlockSpec((None, None, a, b), lambda q, core: (q, core[0], 0, 0)),
                     pl.BlockSpec((None, a, b), lambda q, core: (q, 0, 0))]
        out_specs.append(pl.BlockSpec((None, a, b), lambda q, core: (q, 0, 0)))
        out_shape.append(_sds((N_CHIPS, a, b), g.dtype))
        operands += [g.reshape(N_CHIPS, 2, a, b), p_sib]
    my_core = lax.axis_index("c").astype(jnp.int32).reshape(1)
    return pl.pallas_call(
        body, name=name, out_shape=out_shape,
        grid_spec=pltpu.PrefetchScalarGridSpec(num_scalar_prefetch=1, grid=(N_CHIPS,), in_specs=in_specs,
                                               out_specs=out_specs),
        compiler_params=_params(("arbitrary",)))(my_core, *operands)


def adamw(name, recvs, w, m, v, *, ta):
    L, a, b = w.shape
    n_terms = recvs[0].shape[0]
    assert a % ta == 0 and len(recvs) == L
    c1 = 1.0 - ADAM_B1 ** ADAM_STEP
    c2 = 1.0 - ADAM_B2 ** ADAM_STEP

    def body(*refs):
        r_refs = refs[:L]
        w_ref, m_ref, v_ref, g_ref, d_ref, nm_ref, nv_ref = refs[L:]
        layer = pl.program_id(0)
        for l in range(L):
            @pl.when(layer == l)
            def _(r_ref=r_refs[l]):
                g = r_ref[0].astype(F32)
                for s in range(1, n_terms):
                    g = g + r_ref[s].astype(F32)
                nm = ADAM_B1 * m_ref[...] + (1.0 - ADAM_B1) * g
                nv = ADAM_B2 * v_ref[...] + (1.0 - ADAM_B2) * jnp.square(g)
                m_hat = nm / c1
                v_hat = nv / c2
                g_ref[...] = g
                d_ref[...] = -ADAM_LR * (m_hat / (jnp.sqrt(v_hat) + ADAM_EPS) + ADAM_WD * w_ref[...])
                nm_ref[...] = nm
                nv_ref[...] = nv

    blk = pl.BlockSpec((None, ta, b), lambda l, i: (l, i, 0))
    out = _sds((L, a, b), F32)
    r_specs = [pl.BlockSpec((n_terms, ta, b), lambda l, i, ll=ll: (0, jnp.where(l == ll, i, 0), 0)) for ll in range(L)]
    return pl.pallas_call(
        body, name=name, grid=(L, a // ta), in_specs=r_specs + [blk, blk, blk],
        out_specs=[blk, blk, blk, blk], out_shape=[out, out, out, out],
        compiler_params=_params(("arbitrary", "arbitrary")))(*recvs, w, m, v)


def _pack_rows(parts):
    out = []
    for arr, rows in parts:
        arr = arr.reshape(-1, LANES).astype(F32)
        out.append(jnp.pad(arr, ((0, rows - arr.shape[0]), (0, 0))))
    return jnp.concatenate(out, axis=0)


def _small_rows(a, prefix):
    return _pack_rows([(a[prefix + name], rows) for name, rows in SMALL_SHARDED])


def _unpack_small(packed, a):
    out, r0 = {}, 0
    for name, rows in SMALL_SHARDED:
        shape = a[name].shape
        used = math.prod(shape) // LANES
        out[name] = packed[r0:r0 + used].reshape(shape)
        r0 += rows
    return out


def _rep_rows(a, prefix):
    parts = []
    for name in REPLICATED:
        arr = a[prefix + name]
        if arr.size % LANES:
            arr = jnp.pad(arr.reshape(1, -1), ((0, 0), (0, LANES - arr.size % LANES)))
        rows = -(-arr.size // LANES)
        parts.append((arr, -(-rows // SUBLANES) * SUBLANES))
    return _pack_rows(parts)


def _unpack_rep(packed, a):
    out, r0 = {}, 0
    for name in REPLICATED:
        shape = a[name].shape
        size = math.prod(shape)
        rows = -(-size // LANES)
        out[name] = packed[r0:r0 + rows].reshape(-1)[:size].reshape(shape)
        r0 += -(-rows // SUBLANES) * SUBLANES
    return out


def _step(a):
    x = a['x'][0]
    T, D = x.shape
    tgt = a['loss_target'][0]
    p_in = [a['p'][i, 0] for i in range(DEPTH)]
    PLE = p_in[0].shape[1]
    n_heads = a['attn_sinks'].shape[1]
    HD = n_heads * HEAD_DIM
    KVD = a['kv_w_k'].shape[1]
    n_kv = KVD // HEAD_DIM
    F = a['mlp_w_down'].shape[1] * N_DEV
    tm = min(TOKEN_TILE, T)
    tm2 = min(2 * TOKEN_TILE, T)
    tmc = min(TOKEN_TILE, T)
    tw = 512
    alpha = DEEPNORM_ALPHA
    xb, p_b = x, p_in

    def shard3(w):
        return w.reshape((1,) + w.shape) if w.ndim == 2 else w

    def gather(*specs):
        return _GatherJob([shard3(a[nm])[li].astype(BF16) for nm, li in specs])

    (W_in, small_full), = comm_only("gather_first",
                                    [_GatherJob([a['conv_w_in'][0].astype(BF16), _small_rows(a, '')])])
    r0, small = 0, {}
    for name, rows in SMALL_SHARDED:
        small[name] = small_full[:, r0:r0 + rows]
        r0 += rows
    b_in = small['conv_b_in'][:, 0:2 * D // N_DEV // LANES].reshape(1, 2 * D)
    w_dw = jnp.transpose(small['conv_w_dw'], (1, 0, 2)).reshape(CONV_HALO, D)
    b_dw, cln_g, cln_b, b_out = (small[nm][:, 0].reshape(1, D) for nm in
                                 ('conv_b_dw', 'conv_ln_g', 'conv_ln_b', 'conv_b_out'))
    W_up, W_down, W_proj, W_gate = {}, {}, {}, {}
    mix_g, mix_b, mlp_g, mlp_b = a['mix_ln_g'], a['mix_ln_b'], a['mlp_ln_g'], a['mlp_ln_b']
    rope = _rope_tables(T)

    def set_ple_weights(li, g_proj, g_gate):
        W_proj[li] = jnp.transpose(g_proj, (1, 0, 2)).reshape(1, PLE, D)
        W_gate[li] = g_gate.reshape(1, D, D)

    def row(v, i):
        return v[i:i + 1]

    def res_ln_epi(coef):
        def epi(accs, ex, out, i):
            acc = accs[0] if isinstance(accs, list) else accs
            n_ex = len(ex)
            res_ref, g_ref, b_ref = ex[n_ex - 3], ex[n_ex - 2], ex[n_ex - 1]
            z = coef * res_ref[...] + acc
            if n_ex == 4:
                z = z + ex[0][...]
            out[0][...] = z
            xo = _ln_fwd(z, g_ref[...], b_ref[...])
            out[1][...] = xo
            out[2][...] = xo.astype(BF16)
        return epi

    res_ln_outs = [(_sds((T, D), F32), 'tile'), (_sds((T, D), F32), 'tile'), (_sds((T, D), BF16), 'tile')]

    def mlp_fwd(li, xin, xin_b, down_comm=None):
        def up_epi(accs, ex, out, i):
            out[0][...] = accs[0].astype(BF16)
        (u,), got_up = mm_nn(f"mlp_up_{li}", [(xin_b, W_up[li], 0)], [], [(_sds((T, F), BF16), 'tile')],
                             up_epi, tm=tm2, tn=min(1024, F), comm=gather(('mlp_w_down', li)))
        W_down[li] = got_up[0].reshape(1, F, D)
        res = mm_nn(f"mlp_down_{li}", [(u, W_down[li], 0)],
                    [(xin, 'tile'), (row(mlp_g, li), 'row'), (row(mlp_b, li), 'row')],
                    res_ln_outs, res_ln_epi(alpha), tm=tm, tn=D, tk=F, comm=down_comm, a_pro=_sq_relu)
        (z, xo, xo_b), got_down = res if down_comm is not None else (res, ())
        return u, z, xo, xo_b, got_down

    def ple_fwd(li, xin, xin_b, with_loss, comm=None):
        def epi(accs, ex, out, i):
            pp, gg = accs
            xo = ex[0][...] + pp * _sigmoid(gg)
            out[1][...] = pp.astype(BF16)
            out[2][...] = gg.astype(BF16)
            if with_loss:
                err = xo - ex[1][...]
                out[0][...] = err * (1.0 / D)
                _init_or_add(out[3], i, jnp.sum(err * err, axis=0, keepdims=True) * (0.5 / D))
            else:
                out[0][...] = xo
                out[3][...] = xo.astype(BF16)
        extras = [(xin, 'tile')] + ([(tgt, 'tile')] if with_loss else [])
        outs = [(_sds((T, D), F32), 'tile'), (_sds((T, D), BF16), 'tile'), (_sds((T, D), BF16), 'tile')]
        outs.append((_sds((1, D), F32), 'rowacc') if with_loss else (_sds((T, D), BF16), 'tile'))
        return mm_nn(f"ple_{li}", [(p_b[li], W_proj[li], 0), (xin_b, W_gate[li], 0)], extras, outs, epi, tm=tm, tn=D,
                     comm=comm)

    assert D // N_DEV == LANES
    (g0, ha0, hg0), (g_wout,) = _glu(xb, W_in, b_in, T, D, tm, gather(('conv_w_out', 0)))
    W_out = g_wout.reshape(1, D, D)
    (c0, s0), (W_up[0],) = dwconv_fwd(g0, w_dw, b_dw, cln_g, cln_b, tm=tmc, comm=gather(('mlp_w_up', 0)))
    (z1, x1, x1b), got = mm_nn("conv_out", [(s0, W_out, 0)],
                               [(b_out, 'row'), (x, 'tile'), (row(mix_g, 0), 'row'), (row(mix_b, 0), 'row')],
                               res_ln_outs, res_ln_epi(alpha), tm=tm, tn=D,
                               comm=gather(('ple_w_proj', 0), ('ple_w_gate', 0)))
    set_ple_weights(0, got[0], got[1])
    u0, z2, x2, x2b, got = mlp_fwd(
        0, x1, x1b, down_comm=gather(('attn_w_q', 0), ('kv_w_k', 0), ('kv_w_v', 0), ('attn_w_o', 0)))
    W_qkv = jnp.concatenate([got[0].reshape(D, HD), got[1].reshape(D, KVD), got[2].reshape(D, KVD)], axis=1)[None]
    W_o = got[3].reshape(1, HD, D)
    (x3, pp0, gg0, x3b), got = ple_fwd(0, x2, x2b, False, comm=gather(('ple_w_proj', 1), ('ple_w_gate', 1)))
    set_ple_weights(1, got[0], got[1])

    def qkv_epi(accs, ex, out, i):
        t = accs[0]
        c, su, sd = ex[0][...], ex[1][...], ex[2][...]
        out[0][...] = _rope_wide(t[:, 0:HD], c, su, sd, _rope_chunk).astype(BF16)
        widen_kv(_rope_wide(t[:, HD:HD + KVD], c, su, sd, _rope_chunk), t[:, HD + KVD:], out[1], n_kv)
    NQ = HD + 2 * KVD
    q1, kvx1 = mm_nn("qkv_rope", [(x3b, W_qkv, 0)], [(t, 'tab') for t in rope],
                     [(_sds((T, HD), BF16), 'rows'), (_sds((T, 4 * n_kv * LANES), BF16), 'rows')], qkv_epi,
                     tm=tm, tn=NQ)
    (o1, lse1), (W_up[1],) = attn_fwd(q1, kvx1, a['attn_sinks'], n_heads, n_kv, comm=gather(('mlp_w_up', 1)))
    z3, x4, x4b = mm_nn("attn_out", [(o1, W_o, 0)], [(x3, 'tile'), (row(mix_g, 1), 'row'), (row(mix_b, 1), 'row')],
                        res_ln_outs, res_ln_epi(alpha), tm=tm, tn=D)
    u1, z4, x5, x5b, _ = mlp_fwd(1, x4, x4b)
    dy, pp1, gg1, loss_row = ple_fwd(1, x5, x5b, True)
    loss_local = jnp.sum(loss_row)

    grads = {}

    def ln_bwd_epi(coef, with_colsum):
        def epi(acc, ex, out, i):
            d_x = acc + coef * ex[0][...]
            dz, dg, db = _ln_bwd(d_x, ex[1][...], ex[2][...])
            out[0][...] = dz
            out[1][...] = dz.astype(BF16)
            _init_or_add(out[2], i, dg)
            _init_or_add(out[3], i, db)
            if with_colsum:
                _init_or_add(out[4], i, jnp.sum(dz, axis=0, keepdims=True))
        return epi

    def ln_bwd_outs(with_colsum):
        outs = [(_sds((T, D), F32), 'tile'), (_sds((T, D), BF16), 'tile'), (_sds((1, D), F32), 'rowacc'),
                (_sds((1, D), F32), 'rowacc')]
        return outs + ([(_sds((1, D), F32), 'rowacc')] if with_colsum else [])

    def ple_bwd(li, d_out, xin, pp, gg, z_mlp, pair_specs=None, chip_keys=None):
        side = {}

        def gate_grads(ex):
            d = ex[0][...]
            sg = _sigmoid(ex[4][...].astype(F32))
            side['d_pp'] = (d * sg).astype(BF16)
            side['d_gg'] = (d * ex[3][...].astype(F32) * sg * (1.0 - sg)).astype(BF16)
            return side['d_gg']

        jobs = _Jobs([pair_stage(*pair_specs), chip_stage(*chip_keys)]) if pair_specs else None
        res = mm_nt(f"ple_dx_{li}", _sds((T, D), BF16), W_gate[li],
                    [(d_out, 'tile'), (z_mlp, 'tile'), (row(mlp_g, li), 'row'), (pp, 'tile'), (gg, 'tile')],
                    ln_bwd_outs(False), ln_bwd_epi(1.0, False), tm=tm, tko=D, tc=D, comm=jobs, a_pro=gate_grads,
                    dws=[(xin, lambda: side['d_gg']), (p_b[li], lambda: side['d_pp'])])
        if jobs is not None:
            res = split_hosted(res, jobs, chips=[1], pairs=[0])
        dz, dzb, dg, db, dw_gate, dw_proj = res
        grads[('mlp_ln_g', li)], grads[('mlp_ln_b', li)] = dg, db
        grads[('ple_w_gate', li)], grads[('ple_w_proj', li)] = dw_gate, dw_proj
        return dz, dzb

    recv = {}
    wqkv_cols = {'attn_w_q': (0, HD), 'kv_w_k': (HD, HD + KVD), 'kv_w_v': (HD + KVD, NQ)}

    def piece(name, li):
        if name == 'conv_w_in':
            return grads['conv_w_in']
        if name == 'mlp_w_up':
            return grads[('mlp_w_up', li)]
        if name == 'ple_w_proj':
            return jnp.transpose(grads[('ple_w_proj', li)][0].reshape(PLE, N_DEV, D // N_DEV), (1, 0, 2))
        if name in wqkv_cols:
            g = grads['w_qkv'][:, wqkv_cols[name][0]:wqkv_cols[name][1]]
        else:
            g = grads[name] if name in grads else grads[(name, li)]
            g = g[0]
        return g.reshape(N_DEV, g.shape[0] // N_DEV, g.shape[1])

    def pair_stage(*specs):
        job = _PairJob([piece(nm, li) for nm, li in specs])
        job.specs = specs
        return job

    sums = {}

    def pair_done(pair_job, got):
        name = "chip_sum_" + "_".join(f"{nm}_{li}" for nm, li in pair_job.specs)
        for spec, total in zip(pair_job.specs, chip_sums(name, list(zip(pair_job.sources, got)))):
            sums[spec] = total

    def chip_stage(*keys):
        job = _ChipJob([sums[k] for k in keys])
        job.specs = keys
        return job

    def hosted(res, job):
        main, got = res
        for spec, r in zip(job.specs, got):
            recv[spec] = r
        return main

    def split_hosted(res, jobs, chips, pairs):
        main, got = res
        parts = jobs.split(got)
        for k in chips:
            hosted((None, parts[k]), jobs.jobs[k])
        for k in pairs:
            pair_done(jobs.jobs[k], parts[k])
        return main

    def mlp_bwd(li, dz, dzb, xin, u, z_mix, with_colsum, du_pairs, du_chips):
        jobs = _Jobs([pair_stage(*du_pairs)] + ([chip_stage(*du_chips)] if du_chips else []))
        du, grads[('mlp_w_up', li)], grads[('mlp_w_down', li)] = split_hosted(
            mlp_du_dw(f"mlp_du_{li}", dzb, W_down[li], u, xin, F // N_DEV, tm=tm2, tf=min(1024, F), comm=jobs),
            jobs, chips=[1] if du_chips else [], pairs=[0])
        pair = pair_stage(('mlp_w_up', li), ('mlp_w_down', li))
        res, got = mm_nt(f"mlp_dx_{li}", du, W_up[li], [(dz, 'tile'), (z_mix, 'tile'), (row(mix_g, li), 'row')],
                         ln_bwd_outs(with_colsum), ln_bwd_epi(alpha, with_colsum), tm=tm, tko=D, tc=F, comm=pair)
        pair_done(pair, got)
        grads[('mix_ln_g', li)], grads[('mix_ln_b', li)] = res[2], res[3]
        return res

    dz4, dz4b = ple_bwd(1, dy, x5b, pp1, gg1, z4)
    dz3, dz3b, _, _ = mlp_bwd(1, dz4, dz4b, x4b, u1, z3, False, [('ple_w_gate', 1), ('ple_w_proj', 1)], None)

    def do_epi(acc, ex, out, i):
        out[0][...] = acc.astype(BF16)
    job = chip_stage(('ple_w_gate', 1), ('ple_w_proj', 1))
    do1, grads['attn_w_o'] = hosted(
        mm_nt("attn_do", dz3b, W_o, [], [(_sds((T, HD), BF16), 'tile')], do_epi, tm=tm, tko=HD, tc=D, comm=job,
              dws=[(o1, None)]), job)
    jobs = _Jobs([chip_stage(('mlp_w_up', 1)), pair_stage(('attn_w_o', 0))])
    dq, dkv, d_sinks = split_hosted(
        attn_bwd(q1, kvx1, do1, lse1, a['attn_sinks'], rope, n_heads, n_kv, comm=jobs), jobs, chips=[0], pairs=[1])

    def d_qkv_tile(ex):
        dk = _rope_wide(ex[2][:, 0:KVD], ex[3][...], ex[4][...], ex[5][...], _rope_chunk_bwd)
        return jnp.concatenate([ex[1][...], dk.astype(BF16), ex[2][:, KVD:].astype(BF16)], axis=1)

    def dx3_epi(acc, ex, out, i):
        out[0][...] = acc + alpha * ex[0][...]
    job = chip_stage(('attn_w_o', 0))
    dx3, dw_qkv = hosted(mm_nt("attn_dx", _sds((T, NQ), BF16), W_qkv,
                               [(dz3, 'tile'), (dq, 'rows'), (dkv, 'rows')] + [(t, 'tab') for t in rope],
                               [(_sds((T, D), F32), 'tile')], dx3_epi, tm=tm, tko=D, tc=NQ, comm=job,
                               a_pro=d_qkv_tile, dws=[(x3b, None)]), job)
    grads['w_qkv'] = dw_qkv[0]

    dz2, dz2b = ple_bwd(0, dx3, x2b, pp0, gg0, z2, [('attn_w_q', 0), ('kv_w_k', 0), ('kv_w_v', 0)],
                        [('mlp_w_down', 1)])
    dz1, dz1b, _, _, db_out = mlp_bwd(0, dz2, dz2b, x1b, u0, z1, True, [('ple_w_gate', 0), ('ple_w_proj', 0)],
                                      [('attn_w_q', 0), ('kv_w_k', 0), ('kv_w_v', 0)])

    def ds_epi(acc, ex, out, i):
        n = _ln_fwd(ex[0][...], ex[1][...], ex[2][...])
        sg = _sigmoid(n)
        dn = acc * (sg * (1.0 + n * (1.0 - sg)))
        dc, dg, db = _ln_bwd(dn, ex[0][...], ex[1][...])
        out[0][...] = dc
        _init_or_add(out[1], i, dg)
        _init_or_add(out[2], i, db)
    job = chip_stage(('ple_w_gate', 0), ('ple_w_proj', 0))
    dc0, d_cln_g, d_cln_b, grads['conv_w_out'] = hosted(
        mm_nt("conv_ds", dz1b, W_out, [(c0, 'tile'), (cln_g, 'row'), (cln_b, 'row')],
              [(_sds((T, D), F32), 'tile'), (_sds((1, D), F32), 'rowacc'), (_sds((1, D), F32), 'rowacc')], ds_epi,
              tm=tm, tko=D, tc=D, comm=job, dws=[(s0, None)]), job)
    job = chip_stage(('mlp_w_up', 0), ('mlp_w_down', 0))
    dh0, d_wdw, d_bdw, d_bin = hosted(dwconv_bwd(dc0, g0, ha0, hg0, w_dw, tm=tmc, comm=job), job)
    grads['conv_w_in'] = mm_tn("d_conv_win", xb, dh0, 2 * D // N_DEV, BF16, tm=T, tk=tw, tn=tw)

    pair = pair_stage(('conv_w_in', 0), ('conv_w_out', 0))
    pair_done(pair, comm_only("pair_last", [pair])[0])
    last_chip = chip_stage(('conv_w_in', 0), ('conv_w_out', 0))

    def own_rows(vec, rows_used, rows):
        arr = vec.reshape(N_DEV, rows_used, LANES)
        return jnp.pad(arr, ((0, 0), (0, rows - rows_used), (0, 0)))
    dwdw_dev = jnp.transpose(d_wdw.reshape(CONV_HALO, N_DEV, D // N_DEV), (1, 0, 2))
    lane_rows = D // N_DEV // LANES
    small_grad = jnp.concatenate([
        own_rows(d_bin, 2 * lane_rows, 8), dwdw_dev if lane_rows == 1 else dwdw_dev.reshape(N_DEV, -1, LANES),
        own_rows(d_bdw, lane_rows, 8), own_rows(d_cln_g, lane_rows, 8), own_rows(d_cln_b, lane_rows, 8),
        own_rows(db_out, lane_rows, 8)], axis=1)
    n_small = small_grad.shape[1]

    rep_local = {'mix_ln_g': jnp.concatenate([grads[('mix_ln_g', li)] for li in range(DEPTH)], axis=0),
                 'mix_ln_b': jnp.concatenate([grads[('mix_ln_b', li)] for li in range(DEPTH)], axis=0),
                 'mlp_ln_g': jnp.concatenate([grads[('mlp_ln_g', li)] for li in range(DEPTH)], axis=0),
                 'mlp_ln_b': jnp.concatenate([grads[('mlp_ln_b', li)] for li in range(DEPTH)], axis=0),
                 'attn_sinks': d_sinks}
    rep_grad = _rep_rows(rep_local, '')
    n_rep = rep_grad.shape[0]
    last = _Jobs([last_chip, _DirectJob([small_grad, jnp.broadcast_to(rep_grad[None], (N_DEV, n_rep, LANES))])])

    def dx_epi(acc, ex, out, i):
        out[0][...] = acc + alpha * ex[0][...]
    (grad_x,), got = mm_nt("conv_dx", dh0, W_in, [(dz1, 'tile')], [(_sds((T, D), F32), 'tile')], dx_epi,
                           tm=tm, tko=D, tc=D, comm=last)
    got, (recv_small, recv_rep) = last.split(got)
    hosted((None, got), last_chip)

    result = {}
    kinds = ('grad', 'delta', 'new_m', 'new_v')
    w, m, v = (_small_rows(a, pre)[None] for pre in ('', 'm_', 'v_'))
    for kind, arr in zip(kinds, adamw("adamw_small", [recv_small], w, m, v, ta=n_small)):
        for pname, val in _unpack_small(arr[0], a).items():
            result[(kind, pname)] = val
    w, m, v = (_rep_rows(a, pre)[None] for pre in ('', 'm_', 'v_'))
    for kind, arr in zip(kinds, adamw("adamw_rep", [recv_rep], w, m, v, ta=n_rep)):
        for pname, val in _unpack_rep(arr[0], a).items():
            result[(kind, pname)] = val
    for name in BIG_WEIGHTS:
        w, m, v = (shard3(a[pre + name]) for pre in ('', 'm_', 'v_'))
        recvs = [recv[(name, li)] for li in range(w.shape[0])]
        for kind, arr in zip(kinds, adamw("adamw_" + name, recvs, w, m, v, ta=min(256, w.shape[1]))):
            result[(kind, name)] = arr.reshape(a[name].shape)

    loss = lax.psum(loss_local, ("x", "y", "c"))
    out = [loss, grad_x[None]]
    for kind in ('grad', 'delta', 'new_m', 'new_v'):
        out += [result[(kind, name)] for name in WEIGHT_NAMES]
    return tuple(out)


def _glu(x, W_in, b_in, T, D, tm, comm=None):
    n = W_in.shape[2]
    q = 2 if D // n % 2 == 0 else 1
    nt = D // (q * n)
    tn = q * n

    def body(x_ref, wa_ref, wg_ref, ba_ref, bg_ref, g_ref, ha_ref, hg_ref):
        xb = x_ref[...].astype(BF16)
        ha = jnp.concatenate([_dot(xb, wa_ref[s], ((1,), (0,))) for s in range(q)], axis=1) + ba_ref[...]
        hg = jnp.concatenate([_dot(xb, wg_ref[s], ((1,), (0,))) for s in range(q)], axis=1) + bg_ref[...]
        g_ref[...] = ha * _sigmoid(hg)
        ha_ref[...] = ha.astype(ha_ref.dtype)
        hg_ref[...] = hg.astype(hg_ref.dtype)

    return _call(
        body, comm, name="conv_in_glu", grid=(T // tm, nt),
        in_specs=[pl.BlockSpec((tm, D), lambda i, j: (i, 0)),
                  pl.BlockSpec((q, D, n), lambda i, j: (j, 0, 0)),
                  pl.BlockSpec((q, D, n), lambda i, j: (j + nt, 0, 0)),
                  pl.BlockSpec((1, tn), lambda i, j: (0, j)), pl.BlockSpec((1, tn), lambda i, j: (0, j + nt))],
        out_specs=[pl.BlockSpec((tm, tn), lambda i, j: (i, j))] * 3,
        out_shape=[_sds((T, D), F32), _sds((T, D), BF16), _sds((T, D), BF16)],
        sem=("parallel", "parallel"), operands=[x, W_in, W_in, b_in, b_in])


def kernel(x, p, conv_w_in, conv_b_in, conv_w_dw, conv_b_dw, conv_ln_g, conv_ln_b, conv_w_out, conv_b_out, kv_w_k, kv_w_v, attn_w_q, attn_sinks, attn_w_o, mix_ln_g, mix_ln_b, mlp_w_up, mlp_w_down, mlp_ln_g, mlp_ln_b, ple_w_proj, ple_w_gate, loss_target, m_conv_w_in, m_conv_b_in, m_conv_w_dw, m_conv_b_dw, m_conv_ln_g, m_conv_ln_b, m_conv_w_out, m_conv_b_out, m_kv_w_k, m_kv_w_v, m_attn_w_q, m_attn_sinks, m_attn_w_o, m_mix_ln_g, m_mix_ln_b, m_mlp_w_up, m_mlp_w_down, m_mlp_ln_g, m_mlp_ln_b, m_ple_w_proj, m_ple_w_gate, v_conv_w_in, v_conv_b_in, v_conv_w_dw, v_conv_b_dw, v_conv_ln_g, v_conv_ln_b, v_conv_w_out, v_conv_b_out, v_kv_w_k, v_kv_w_v, v_attn_w_q, v_attn_sinks, v_attn_w_o, v_mix_ln_g, v_mix_ln_b, v_mlp_w_up, v_mlp_w_down, v_mlp_ln_g, v_mlp_ln_b, v_ple_w_proj, v_ple_w_gate):
    return _step(dict(locals()))
```

```python
import functools
import math

import jax
import jax.numpy as jnp
from jax import lax
from jax.experimental import pallas as pl
from jax.experimental.pallas import tpu as pltpu

F32 = jnp.float32
BF16 = jnp.bfloat16

N_DEV = 8
HEAD_DIM = 64
ROPE_DIM = HEAD_DIM // 4
ROPE_HALF = ROPE_DIM // 2
ROPE_THETA = 500000.0
ATT_BLOCK = 128
CONV_WIDTH = 31
CONV_HALO = 32
CONV_ROWS = 64
LN_EPS = 1e-5
DEPTH = 2
DEEPNORM_ALPHA = (2 * DEPTH) ** 0.25
MASK_VALUE = -1e30

ADAM_LR = 0.001
ADAM_B1 = 0.9
ADAM_B2 = 0.999
ADAM_EPS = 1e-08
ADAM_WD = 0.01
ADAM_STEP = 10

LANES = 128
SUBLANES = 8
VMEM_LIMIT_BYTES = 52 * 1024 * 1024
TOKEN_TILE = 512
MESH_ID = pl.DeviceIdType.MESH
RELAY_AT_TENTHS = 5
FORWARD_AT_TENTHS = 8

WEIGHT_NAMES = ['conv_w_in', 'conv_b_in', 'conv_w_dw', 'conv_b_dw', 'conv_ln_g', 'conv_ln_b', 'conv_w_out',
                'conv_b_out', 'kv_w_k', 'kv_w_v', 'attn_w_q', 'attn_sinks', 'attn_w_o', 'mix_ln_g', 'mix_ln_b',
                'mlp_w_up', 'mlp_w_down', 'mlp_ln_g', 'mlp_ln_b', 'ple_w_proj', 'ple_w_gate']
BIG_WEIGHTS = ['conv_w_in', 'conv_w_out', 'kv_w_k', 'kv_w_v', 'attn_w_q', 'attn_w_o', 'mlp_w_up', 'mlp_w_down',
               'ple_w_proj', 'ple_w_gate']
SMALL_SHARDED = [('conv_b_in', 8), ('conv_w_dw', 32), ('conv_b_dw', 8), ('conv_ln_g', 8), ('conv_ln_b', 8),
                 ('conv_b_out', 8)]
REPLICATED = ['mix_ln_g', 'mix_ln_b', 'mlp_ln_g', 'mlp_ln_b', 'attn_sinks']


def _params(sem):
    return pltpu.CompilerParams(dimension_semantics=sem, vmem_limit_bytes=VMEM_LIMIT_BYTES)


def _sds(shape, dtype):
    return jax.ShapeDtypeStruct(shape, dtype)


def _my_place():
    x, y, c = lax.axis_index("x"), lax.axis_index("y"), lax.axis_index("c")
    return x, y, c, 4 * x + 2 * y + c


def _peers(x, y, c):
    out = []
    for dx in (0, 1):
        for dy in (0, 1):
            for dc in (0, 1):
                if dx or dy or dc:
                    px, py, pc = x ^ dx, y ^ dy, c ^ dc
                    out.append(((px, py, pc), 4 * px + 2 * py + pc))
    return out


N_CHIPS = 4


def _other_chips(x, y):
    return [((x ^ dx, y ^ dy), 2 * (x ^ dx) + (y ^ dy)) for dx, dy in ((1, 0), (0, 1), (1, 1))]


def _remote(src, dst, send, recv, to):
    return pltpu.make_async_remote_copy(src_ref=src, dst_ref=dst, send_sem=send, recv_sem=recv, device_id=to,
                                        device_id_type=MESH_ID)


def _wait_slabs(buf, count, send, recv, me, sent=True, received=True):
    part = buf.at[pl.ds(0, count)]
    cp = _remote(part, part, send, recv, me)
    if sent:
        cp.wait_send()
    if received:
        cp.wait_recv()


class _DirectJob:
    n_sems = 3

    def __init__(self, items):
        self.sources = list(items)
        self.dests = [_sds(it.shape, it.dtype) for it in items]
        self.n = len(items)

    def start(self, src, dst, send, recv, loc):
        x, y, c, me = _my_place()
        for k in range(self.n):
            here = dst[k].at[pl.ds(me, 1)]
            pltpu.make_async_copy(src[k].at[pl.ds(me, 1)], here, loc.at[k]).start()
            for peer, idx in _peers(x, y, c):
                _remote(src[k].at[pl.ds(idx, 1)], here, send.at[k], recv.at[k], peer).start()

    def relay(self, *refs):
        pass

    def forward(self, *refs):
        pass

    def finish(self, src, dst, send, recv, loc):
        x, y, c, me = _my_place()
        for k in range(self.n):
            pltpu.make_async_copy(src[k].at[pl.ds(me, 1)], dst[k].at[pl.ds(me, 1)], loc.at[k]).wait()
            _wait_slabs(dst[k], N_DEV - 1, send.at[k], recv.at[k], (x, y, c))


class _GatherJob:
    n_sems = 7

    def __init__(self, items):
        self.sources = [it.reshape((1,) + it.shape) for it in items]
        self.dests = [_sds((N_DEV,) + it.shape, it.dtype) for it in items]
        self.n = len(items)

    @staticmethod
    def _neighbours(x, y, c):
        out = []
        for dx, dy in ((c, 1 - c), (1 - c, c), (1, 1)):
            px, py = x ^ dx, y ^ dy
            out.append(((px, py, c), 4 * px + 2 * py + c))
        return out

    def start(self, src, dst, send_a, recv_a, send_b, recv_b, send_d, recv_d, loc):
        x, y, c, me = _my_place()
        (first, _), (second, _), _ = self._neighbours(x, y, c)
        for k in range(self.n):
            here = dst[k].at[pl.ds(me, 1)]
            pltpu.make_async_copy(src[k], here, loc.at[k]).start()
            _remote(src[k], here, send_d.at[k], recv_d.at[k], (x, y, 1 - c)).start()
            _remote(src[k], here, send_a.at[k], recv_a.at[k], first).start()
            _remote(src[k], here, send_b.at[k], recv_b.at[k], second).start()

    def relay(self, src, dst, send_a, recv_a, send_b, recv_b, send_d, recv_d, loc):
        x, y, c, me = _my_place()
        (_, first_idx), (second, _), _ = self._neighbours(x, y, c)
        for k in range(self.n):
            _wait_slabs(dst[k], 1, send_a.at[k], recv_a.at[k], (x, y, c), sent=False)
            rows = dst[k].at[pl.ds(first_idx, 1)]
            _remote(rows, rows, send_b.at[k], recv_b.at[k], second).start()

    def forward(self, src, dst, send_a, recv_a, send_b, recv_b, send_d, recv_d, loc):
        x, y, c, me = _my_place()
        for k in range(self.n):
            _wait_slabs(dst[k], 2, send_b.at[k], recv_b.at[k], (x, y, c), sent=False)
            for _, idx in self._neighbours(x, y, c):
                rows = dst[k].at[pl.ds(idx, 1)]
                _remote(rows, rows, send_d.at[k], recv_d.at[k], (x, y, 1 - c)).start()

    def finish(self, src, dst, send_a, recv_a, send_b, recv_b, send_d, recv_d, loc):
        x, y, c, me = _my_place()
        for k in range(self.n):
            pltpu.make_async_copy(src[k], dst[k].at[pl.ds(me, 1)], loc.at[k]).wait()
            _wait_slabs(dst[k], 1, send_a.at[k], recv_a.at[k], (x, y, c), received=False)
            _wait_slabs(dst[k], 2, send_b.at[k], recv_b.at[k], (x, y, c), received=False)
            _wait_slabs(dst[k], N_CHIPS, send_d.at[k], recv_d.at[k], (x, y, c))


class _PairJob:
    n_sems = 2

    def __init__(self, items):
        self.sources = list(items)
        self.dests = [_sds((N_CHIPS,) + it.shape[1:], it.dtype) for it in items]
        self.n = len(items)

    def start(self, src, dst, send, recv):
        x, y, c, me = _my_place()
        for k in range(self.n):
            for q in range(N_CHIPS):
                _remote(src[k].at[pl.ds(2 * q + 1 - c, 1)], dst[k].at[pl.ds(q, 1)], send.at[k], recv.at[k],
                        (x, y, 1 - c)).start()

    def relay(self, *refs):
        pass

    def forward(self, *refs):
        pass

    def finish(self, src, dst, send, recv):
        x, y, c, me = _my_place()
        for k in range(self.n):
            _wait_slabs(dst[k], N_CHIPS, send.at[k], recv.at[k], (x, y, c))


class _ChipJob:
    n_sems = 3

    def __init__(self, items):
        self.sources = list(items)
        self.dests = [_sds(it.shape, it.dtype) for it in items]
        self.n = len(items)

    def start(self, src, dst, send, recv, loc):
        x, y, c, me = _my_place()
        mine = 2 * x + y
        for k in range(self.n):
            here = dst[k].at[pl.ds(mine, 1)]
            pltpu.make_async_copy(src[k].at[pl.ds(mine, 1)], here, loc.at[k]).start()
            for (px, py), q in _other_chips(x, y):
                _remote(src[k].at[pl.ds(q, 1)], here, send.at[k], recv.at[k], (px, py, c)).start()

    def relay(self, *refs):
        pass

    def forward(self, *refs):
        pass

    def finish(self, src, dst, send, recv, loc):
        x, y, c, me = _my_place()
        mine = 2 * x + y
        for k in range(self.n):
            pltpu.make_async_copy(src[k].at[pl.ds(mine, 1)], dst[k].at[pl.ds(mine, 1)], loc.at[k]).wait()
            _wait_slabs(dst[k], N_CHIPS - 1, send.at[k], recv.at[k], (x, y, c))


class _Jobs:
    def __init__(self, jobs):
        self.jobs = list(jobs)
        self.sources = [s for job in self.jobs for s in job.sources]
        self.dests = [d for job in self.jobs for d in job.dests]
        self.n = len(self.sources)

    def _stage(self, stage, src, dst, sems):
        k0 = s0 = 0
        for job in self.jobs:
            getattr(job, stage)(src[k0:k0 + job.n], dst[k0:k0 + job.n], *sems[s0:s0 + job.n_sems])
            k0 += job.n
            s0 += job.n_sems

    def start(self, src, dst, *sems):
        self._stage('start', src, dst, sems)

    def relay(self, src, dst, *sems):
        self._stage('relay', src, dst, sems)

    def forward(self, src, dst, *sems):
        self._stage('forward', src, dst, sems)

    def finish(self, src, dst, *sems):
        self._stage('finish', src, dst, sems)

    def split(self, got):
        out, k0 = [], 0
        for job in self.jobs:
            out.append(got[k0:k0 + job.n])
            k0 += job.n
        return out


def _job_sems(job):
    if isinstance(job, _Jobs):
        return [s for part in job.jobs for s in _job_sems(part)]
    return [pltpu.SemaphoreType.DMA((job.n,))] * job.n_sems


def _call(body, comm, *, name, grid, in_specs, out_specs, out_shape, operands, sem, scratch_shapes=(), aliases=None):
    single = not isinstance(out_shape, (list, tuple))
    out_shape = [out_shape] if single else list(out_shape)
    out_specs = [out_specs] if single else list(out_specs)
    if comm is None:
        res = pl.pallas_call(body, name=name, grid=grid, in_specs=list(in_specs), out_specs=out_specs,
                             out_shape=out_shape, scratch_shapes=list(scratch_shapes),
                             input_output_aliases=aliases or {}, compiler_params=_params(sem))(*operands)
        return res[0] if single else res
    n_in, n_out, n_scr, n_c = len(in_specs), len(out_shape), len(scratch_shapes), comm.n
    any_spec = pl.BlockSpec(memory_space=pl.ANY)
    steps = math.prod(grid)
    early = min(steps - 1, (steps * RELAY_AT_TENTHS) // 10)
    mid = min(steps - 1, (steps * FORWARD_AT_TENTHS) // 10)

    def hosted(*refs):
        ins, c_src = refs[:n_in], refs[n_in:n_in + n_c]
        outs = refs[n_in + n_c:n_in + n_c + n_out]
        c_dst = refs[n_in + n_c + n_out:n_in + 2 * n_c + n_out]
        scr = refs[n_in + 2 * n_c + n_out:n_in + 2 * n_c + n_out + n_scr]
        sems = refs[n_in + 2 * n_c + n_out + n_scr:]
        step = pl.program_id(0)
        for d in range(1, len(grid)):
            step = step * grid[d] + pl.program_id(d)

        @pl.when(step == 0)
        def _():
            comm.start(c_src, c_dst, *sems)

        @pl.when(step == early)
        def _():
            comm.relay(c_src, c_dst, *sems)

        @pl.when(step == mid)
        def _():
            comm.forward(c_src, c_dst, *sems)

        body(*ins, *outs, *scr)

        @pl.when(step == steps - 1)
        def _():
            comm.finish(c_src, c_dst, *sems)

    res = pl.pallas_call(hosted, name=name, grid=grid, in_specs=list(in_specs) + [any_spec] * n_c,
                         out_specs=out_specs + [any_spec] * n_c, out_shape=out_shape + comm.dests,
                         scratch_shapes=list(scratch_shapes) + _job_sems(comm), input_output_aliases=aliases or {},
                         compiler_params=_params(("arbitrary",) * len(grid)))(*operands, *comm.sources)
    main = res[:n_out]
    return (main[0] if single else main), res[n_out:]


def comm_only(name, jobs):
    any_spec = pl.BlockSpec(memory_space=pl.ANY)
    n_all = sum(job.n for job in jobs)

    def body(*refs):
        srcs, dsts, sems = refs[:n_all], refs[n_all:2 * n_all], refs[2 * n_all:]
        parts, k0, s0 = [], 0, 0
        for job in jobs:
            parts.append((job, srcs[k0:k0 + job.n], dsts[k0:k0 + job.n], sems[s0:s0 + job.n_sems]))
            k0 += job.n
            s0 += job.n_sems
        for stage in ('start', 'relay', 'forward', 'finish'):
            for job, src, dst, sem in parts:
                getattr(job, stage)(src, dst, *sem)

    res = pl.pallas_call(body, name=name, in_specs=[any_spec] * n_all, out_specs=[any_spec] * n_all,
                         out_shape=[d for job in jobs for d in job.dests],
                         scratch_shapes=[s for job in jobs for s in _job_sems(job)],
                         )(*[s for job in jobs for s in job.sources])
    out, k0 = [], 0
    for job in jobs:
        out.append(res[k0:k0 + job.n])
        k0 += job.n
    return out


def _dot(a, b, dims):
    if a.dtype != BF16:
        a = a.astype(BF16)
    if b.dtype != BF16:
        b = b.astype(BF16)
    return lax.dot_general(a, b, (dims, ((), ())), preferred_element_type=F32)


def _sigmoid(v):
    return 1.0 / (1.0 + jnp.exp(-v))


def _sq_relu(u):
    return jnp.square(jnp.maximum(u.astype(F32), 0.0)).astype(BF16)


def _ln_stats(z):
    mu = jnp.mean(z, axis=-1, keepdims=True)
    zc = z - mu
    var = jnp.mean(zc * zc, axis=-1, keepdims=True)
    return zc * lax.rsqrt(var + LN_EPS)


def _ln_fwd(z, g, b):
    return _ln_stats(z) * g + b


def _ln_bwd(dy, z, g):
    xhat = _ln_stats(z)
    mu = jnp.mean(z, axis=-1, keepdims=True)
    zc = z - mu
    rstd = lax.rsqrt(jnp.mean(zc * zc, axis=-1, keepdims=True) + LN_EPS)
    dxh = dy * g
    m1 = jnp.mean(dxh, axis=-1, keepdims=True)
    m2 = jnp.mean(dxh * xhat, axis=-1, keepdims=True)
    dz = rstd * (dxh - m1 - xhat * m2)
    return dz, jnp.sum(dy * xhat, axis=0, keepdims=True), jnp.sum(dy, axis=0, keepdims=True)


def _extra_spec(shape, kind, tm, tn, ij):
    if kind == 'tile':
        return pl.BlockSpec((tm, tn), lambda *g: (ij(g)[0], ij(g)[1]))
    if kind in ('row', 'rowacc'):
        return pl.BlockSpec((1, tn), lambda *g: (0, ij(g)[1]))
    if kind == 'tab':
        return pl.BlockSpec((tm, LANES), lambda *g: (ij(g)[0], 0))
    if kind == 'rows':
        return pl.BlockSpec((tm, shape[1]), lambda *g: (ij(g)[0], 0))
    raise ValueError(kind)


def mm_nn(name, pairs, extras, outs, epi, *, tm, tn, tk=None, comm=None, a_pro=None):
    M = pairs[0][0].shape[0]
    N = pairs[0][1].shape[0] * pairs[0][1].shape[2]
    n_pairs = len(pairs)
    K0 = pairs[0][0].shape[1]
    tk = K0 if tk is None else tk
    nk = K0 // tk
    assert nk == 1 or n_pairs == 1
    assert M % tm == 0 and N % tn == 0 and K0 % tk == 0
    has_rowacc = any(kind == 'rowacc' for _, kind in outs)
    assert not has_rowacc or (N == tn and nk == 1)
    in_specs, operands, slabs = [], [], []
    for a, b, off in pairs:
        K = a.shape[1]
        ktile = K if n_pairs > 1 else tk
        n = b.shape[2]
        assert b.shape[1] == K
        in_specs.append(pl.BlockSpec((tm, ktile), lambda i, j, k: (i, k)))
        if tn <= n:
            assert n % tn == 0
            r = n // tn
            in_specs.append(pl.BlockSpec((None, ktile, tn),
                                         lambda i, j, k, r=r, off=off: ((j + off) // r, k, (j + off) % r)))
            slabs.append(0)
        else:
            assert tn % n == 0
            in_specs.append(pl.BlockSpec((tn // n, ktile, n), lambda i, j, k, off=off: (j + off, k, 0)))
            slabs.append(tn // n)
        operands += [a, b]
    ij = lambda g: (g[0], g[1])
    for arr, kind in extras:
        in_specs.append(_extra_spec(arr.shape, kind, tm, tn, ij))
        operands.append(arr)
    out_specs = [_extra_spec(o.shape, kind, tm, tn, ij) for o, kind in outs]
    n_ex, n_out = len(extras), len(outs)

    def pair_dot(ab, q):
        a = ab[2 * q][...]
        if a_pro is not None:
            a = a_pro(a)
        if not slabs[q]:
            return _dot(a, ab[2 * q + 1][...], ((1,), (0,)))
        return jnp.concatenate([_dot(a, ab[2 * q + 1][s], ((1,), (0,))) for s in range(slabs[q])], axis=1)

    def body(*refs):
        ab = refs[:2 * n_pairs]
        ex = refs[2 * n_pairs:2 * n_pairs + n_ex]
        out = refs[2 * n_pairs + n_ex:2 * n_pairs + n_ex + n_out]
        i = pl.program_id(0)
        if nk == 1:
            accs = [pair_dot(ab, q) for q in range(n_pairs)]
            epi(accs, ex, out, i)
        else:
            acc_ref = refs[-1]
            k = pl.program_id(2)

            @pl.when(k == 0)
            def _():
                acc_ref[...] = jnp.zeros_like(acc_ref)

            acc_ref[...] += pair_dot(ab, 0)

            @pl.when(k == nk - 1)
            def _():
                epi([acc_ref[...]], ex, out, i)

    scratch = [pltpu.VMEM((tm, tn), F32)] if nk > 1 else []
    sem = ("arbitrary",) * 3 if has_rowacc else ("parallel", "parallel", "arbitrary")
    return _call(body, comm, name=name, grid=(M // tm, N // tn, nk), in_specs=in_specs, out_specs=out_specs,
                 out_shape=[o for o, _ in outs], scratch_shapes=scratch, sem=sem, operands=operands)


def mm_nt(name, a, b, extras, outs, epi, *, tm, tko, tc, comm=None, a_pro=None, dws=()):
    M, N = a.shape
    J, K, n = b.shape
    assert J * n == N and M % tm == 0 and K % tko == 0 and N % tc == 0
    nc = N // tc
    assert a_pro is None or (nc == 1 and tc <= n and K == tko)
    assert not dws or (nc == 1 and K == tko and tc <= n)
    n_dw, nt = len(dws), M // tm
    has_rowacc = any(kind == 'rowacc' for _, kind in outs)
    assert not has_rowacc or K == tko
    if tc <= n:
        assert n % tc == 0
        r = n // tc
        slabs = 0
        b_spec = pl.BlockSpec((None, tko, tc), lambda i, j, c: (c // r, j, c % r))
    else:
        assert tc % n == 0
        slabs = tc // n
        b_spec = pl.BlockSpec((slabs, tko, n), lambda i, j, c: (c, j, 0))
    n_ab = 2 if a_pro is None else 1
    in_specs = ([pl.BlockSpec((tm, tc), lambda i, j, c: (i, c))] if a_pro is None else []) + [b_spec]

    def nt_dot(a_ref, b_ref):
        if not slabs:
            return _dot(a_ref[...], b_ref[...], ((1,), (1,)))
        acc = _dot(a_ref[:, 0:n], b_ref[0], ((1,), (1,)))
        for s in range(1, slabs):
            acc = acc + _dot(a_ref[:, s * n:(s + 1) * n], b_ref[s], ((1,), (1,)))
        return acc

    ij = lambda g: (g[0], g[1])
    operands = [a, b] if a_pro is None else [b]
    for arr, kind in extras:
        in_specs.append(_extra_spec(arr.shape, kind, tm, tko, ij))
        operands.append(arr)
    out_specs = [_extra_spec(o.shape, kind, tm, tko, ij) for o, kind in outs]
    n_ex, n_out = len(extras), len(outs)
    out_shape = [o for o, _ in outs]
    for lhs, _ in dws:
        in_specs.append(pl.BlockSpec((tm, lhs.shape[1]), lambda i, j, c: (i, 0)))
        operands.append(lhs)
        out_specs.append(pl.BlockSpec((None, lhs.shape[1], N), lambda i, j, c: (0, 0, 0)))
        out_shape.append(_sds((1, lhs.shape[1], N), BF16))

    def weight_grads(refs, a_tile, i):
        lhs_refs = refs[n_ab + n_ex:n_ab + n_ex + n_dw]
        dw_refs = refs[n_ab + n_ex + n_dw + n_out:n_ab + n_ex + n_dw + n_out + n_dw]
        acc_refs = refs[len(refs) - n_dw:]
        for (_, rhs_fn), lhs_ref, dw_ref, acc_ref in zip(dws, lhs_refs, dw_refs, acc_refs):
            part = _dot(lhs_ref[...], a_tile if rhs_fn is None else rhs_fn(), ((0,), (0,)))
            _init_or_add(acc_ref, i, part)

            @pl.when(i == nt - 1)
            def _(dw_ref=dw_ref, acc_ref=acc_ref):
                dw_ref[...] = acc_ref[...].astype(dw_ref.dtype)

    def body(*refs):
        ex = refs[n_ab:n_ab + n_ex]
        out = refs[n_ab + n_ex + n_dw:n_ab + n_ex + n_dw + n_out]
        i = pl.program_id(0)
        if a_pro is not None:
            a_tile = a_pro(ex)
            epi(_dot(a_tile, refs[0][...], ((1,), (1,))), ex, out, i)
            weight_grads(refs, a_tile, i)
            return
        a_ref, b_ref = refs[:2]
        if nc == 1:
            epi(nt_dot(a_ref, b_ref), ex, out, i)
            if dws:
                weight_grads(refs, a_ref[...], i)
        else:
            acc_ref = refs[-1]
            c = pl.program_id(2)

            @pl.when(c == 0)
            def _():
                acc_ref[...] = jnp.zeros_like(acc_ref)

            acc_ref[...] += nt_dot(a_ref, b_ref)

            @pl.when(c == nc - 1)
            def _():
                epi(acc_ref[...], ex, out, i)

    scratch = ([pltpu.VMEM((tm, tko), F32)] if nc > 1 else []) + [pltpu.VMEM((lhs.shape[1], N), F32) for lhs, _ in dws]
    sem = ("arbitrary",) * 3 if (has_rowacc or dws) else ("parallel", "parallel", "arbitrary")
    return _call(body, comm, name=name, grid=(M // tm, K // tko, nc), in_specs=in_specs, out_specs=out_specs,
                 out_shape=out_shape, scratch_shapes=scratch, sem=sem, operands=operands)


def mm_tn(name, a, d, n, out_dtype, *, tm, tk, tn, comm=None):
    M, K = a.shape
    N = d.shape[1]
    assert d.shape[0] == M and N % n == 0 and N % tn == 0 and K % tk == 0 and M % tm == 0
    nm = M // tm
    if tn <= n:
        assert n % tn == 0
        r = n // tn
        slabs = 0
        o_spec = pl.BlockSpec((None, tk, tn), lambda kk, j, m: (j // r, kk, j % r))
    else:
        assert tn % n == 0
        slabs = tn // n
        o_spec = pl.BlockSpec((slabs, tk, n), lambda kk, j, m: (j, kk, 0))

    def write(o_ref, acc):
        if not slabs:
            o_ref[...] = acc.astype(o_ref.dtype)
        else:
            for s in range(slabs):
                o_ref[s] = acc[:, s * n:(s + 1) * n].astype(o_ref.dtype)

    def body(a_ref, d_ref, o_ref, *scratch):
        if nm == 1:
            write(o_ref, _dot(a_ref[...], d_ref[...], ((0,), (0,))))
            return
        acc_ref, = scratch
        m = pl.program_id(2)

        @pl.when(m == 0)
        def _():
            acc_ref[...] = jnp.zeros_like(acc_ref)

        acc_ref[...] += _dot(a_ref[...], d_ref[...], ((0,), (0,)))

        @pl.when(m == nm - 1)
        def _():
            write(o_ref, acc_ref[...])

    return _call(
        body, comm, name=name, grid=(K // tk, N // tn, nm),
        in_specs=[pl.BlockSpec((tm, tk), lambda kk, j, m: (m, kk)), pl.BlockSpec((tm, tn), lambda kk, j, m: (m, j))],
        out_specs=o_spec, out_shape=_sds((N // n, K, n), out_dtype),
        scratch_shapes=[pltpu.VMEM((tk, tn), F32)] if nm > 1 else [],
        sem=("parallel", "parallel", "arbitrary"), operands=[a, d])


def mlp_du_dw(name, dzb, w_down, u, xin, n, *, tm, tf, comm=None):
    T, D = dzb.shape
    F = u.shape[1]
    assert T % tm == 0 and F % tf == 0 and tf % n == 0
    slabs, nt = tf // n, T // tm

    def body(dz_ref, w_ref, u_ref, x_ref, du_ref, dwu_ref, dwd_ref, up_acc, down_acc):
        i = pl.program_id(1)
        dz = dz_ref[...]
        da = _dot(dz, w_ref[...], ((1,), (1,)))
        relu = jnp.maximum(u_ref[...].astype(F32), 0.0)
        du = (da * (2.0 * relu)).astype(BF16)
        du_ref[...] = du
        _init_or_add(up_acc, i, _dot(x_ref[...], du, ((0,), (0,))))
        _init_or_add(down_acc, i, _dot(jnp.square(relu).astype(BF16), dz, ((0,), (0,))))

        @pl.when(i == nt - 1)
        def _():
            for s in range(slabs):
                dwu_ref[s] = up_acc[:, s * n:(s + 1) * n].astype(dwu_ref.dtype)
            dwd_ref[...] = down_acc[...].astype(dwd_ref.dtype)

    hidden = pl.BlockSpec((tm, tf), lambda j, i: (i, j))
    tokens = pl.BlockSpec((tm, D), lambda j, i: (i, 0))
    return _call(
        body, comm, name=name, grid=(F // tf, nt),
        in_specs=[tokens, pl.BlockSpec((None, tf, D), lambda j, i: (0, j, 0)), hidden, tokens],
        out_specs=[hidden, pl.BlockSpec((slabs, D, n), lambda j, i: (j, 0, 0)),
                   pl.BlockSpec((None, tf, D), lambda j, i: (0, j, 0))],
        out_shape=[_sds((T, F), BF16), _sds((F // n, D, n), BF16), _sds((1, F, D), BF16)],
        scratch_shapes=[pltpu.VMEM((D, tf), F32), pltpu.VMEM((tf, D), F32)], sem=("arbitrary", "arbitrary"),
        operands=[dzb, w_down, u, xin])


def _init_or_add(ref, i, value):
    @pl.when(i == 0)
    def _():
        ref[...] = value

    @pl.when(i > 0)
    def _():
        ref[...] += value


def _rope_tables(T):
    pos = jnp.arange(T, dtype=F32)
    inv_freq = ROPE_THETA ** (-jnp.arange(0, ROPE_DIM, 2, dtype=F32) / ROPE_DIM)
    ang = pos[:, None] * inv_freq[None, :]
    cos, sin = jnp.cos(ang), jnp.sin(ang)
    ones = jnp.ones((T, HEAD_DIM - ROPE_DIM), F32)
    zeros = jnp.zeros((T, HEAD_DIM - ROPE_DIM), F32)
    zh = jnp.zeros((T, ROPE_HALF), F32)
    c_head = jnp.concatenate([cos, cos, ones], axis=1)
    s_up = jnp.concatenate([-sin, zh, zeros], axis=1)
    s_dn = jnp.concatenate([zh, sin, zeros], axis=1)
    rep = LANES // HEAD_DIM
    return tuple(jnp.tile(t, (1, rep)) for t in (c_head, s_up, s_dn))


def _rope_chunk(t, c, s_up, s_dn):
    return t * c + pltpu.roll(t, LANES - ROPE_HALF, 1) * s_up + pltpu.roll(t, ROPE_HALF, 1) * s_dn


def _rope_chunk_bwd(d, c, s_up, s_dn):
    return d * c + pltpu.roll(d * s_up, ROPE_HALF, 1) + pltpu.roll(d * s_dn, LANES - ROPE_HALF, 1)


def _rope_wide(t, c, s_up, s_dn, fn):
    chunks = [fn(t[:, q * LANES:(q + 1) * LANES], c, s_up, s_dn) for q in range(t.shape[1] // LANES)]
    return chunks[0] if len(chunks) == 1 else jnp.concatenate(chunks, axis=1)


def _taps_by_residue(first):
    groups = {}
    for o in range(first, first + CONV_WIDTH):
        groups.setdefault(o % SUBLANES, []).append(o)
    return sorted(groups.items())


def _shifted_taps(win_ref, span_ref, r0, res, offs, rb, ls):
    if res == 0:
        return [functools.partial(lambda o: win_ref[pl.ds(r0 + o, rb), ls], o) for o in offs]
    n = rb + offs[-1] - res
    span_ref[0:n, :] = win_ref[pl.ds(r0 + res, n), ls]
    return [functools.partial(lambda o: span_ref[pl.ds(o - res, rb), :], o) for o in offs]


def _halo_before_spec(tm, D):
    return pl.BlockSpec((CONV_HALO, D), lambda i: (jnp.maximum(i * (tm // CONV_HALO) - 1, 0), 0))


def dwconv_fwd(g, w_dw, b_dw, ln_g, ln_b, *, tm, comm=None):
    T, D = g.shape
    nl = D // LANES

    rb = min(CONV_ROWS, tm)

    def body(g_ref, gh_ref, w_ref, b_ref, lg_ref, lb_ref, c_ref, s_ref, win_ref, span_ref):
        i = pl.program_id(0)
        win_ref[0:CONV_HALO, :] = jnp.where(i > 0, gh_ref[...], 0.0)
        win_ref[CONV_HALO:, :] = g_ref[...]

        def lane_chunk(q, carry):
            ls = pl.ds(pl.multiple_of(q * LANES, LANES), LANES)
            for r0 in range(0, tm, rb):
                acc = jnp.broadcast_to(b_ref[:, ls], (rb, LANES))
                for res, offs in _taps_by_residue(CONV_HALO - (CONV_WIDTH - 1)):
                    taps = _shifted_taps(win_ref, span_ref, r0, res, offs, rb, ls)
                    for o, tap in zip(offs, taps):
                        k = o - (CONV_HALO - (CONV_WIDTH - 1))
                        acc = acc + tap() * w_ref[k:k + 1, ls]
                c_ref[pl.ds(r0, rb), ls] = acc
            return carry

        lax.fori_loop(0, nl, lane_chunk, 0)
        n = _ln_fwd(c_ref[...], lg_ref[...], lb_ref[...])
        s_ref[...] = (n * _sigmoid(n)).astype(s_ref.dtype)

    row = pl.BlockSpec((1, D), lambda i: (0, 0))
    return _call(
        body, comm, name="dwconv_fwd", grid=(T // tm,),
        in_specs=[pl.BlockSpec((tm, D), lambda i: (i, 0)), _halo_before_spec(tm, D),
                  pl.BlockSpec((CONV_HALO, D), lambda i: (0, 0)), row, row, row],
        out_specs=[pl.BlockSpec((tm, D), lambda i: (i, 0)), pl.BlockSpec((tm, D), lambda i: (i, 0))],
        out_shape=[_sds((T, D), F32), _sds((T, D), BF16)],
        scratch_shapes=[pltpu.VMEM((tm + CONV_HALO, D), F32), pltpu.VMEM((rb + CONV_HALO, LANES), F32)],
        sem=("parallel",), operands=[g, g, w_dw, b_dw, ln_g, ln_b])


def dwconv_bwd(dc, g, ha, hg, w_dw, *, tm, comm=None):
    T, D = g.shape
    nl = D // LANES
    last = T // CONV_HALO - 1
    nt = T // tm

    rb = min(CONV_ROWS, tm)

    def body(dc_ref, dcn_ref, g_ref, gh_ref, ha_ref, hg_ref, w_ref, dh_ref, dw_ref, dbdw_ref, dbin_ref,
             win_ref, dwin_ref, dg_ref, dwp_ref, span_ref):
        i = pl.program_id(0)
        win_ref[0:CONV_HALO, :] = jnp.where(i > 0, gh_ref[...], 0.0)
        win_ref[CONV_HALO:, :] = g_ref[...]
        dwin_ref[0:tm, :] = dc_ref[...]
        dwin_ref[tm:, :] = jnp.where(i < nt - 1, dcn_ref[...], 0.0)

        @pl.when(i == 0)
        def _():
            dwp_ref[...] = jnp.zeros_like(dwp_ref)

        first = CONV_HALO - (CONV_WIDTH - 1)

        def lane_chunk(q, carry):
            ls = pl.ds(pl.multiple_of(q * LANES, LANES), LANES)
            for r0 in range(0, tm, rb):
                acc = jnp.zeros((rb, LANES), F32)
                for res, offs in _taps_by_residue(0):
                    taps = _shifted_taps(dwin_ref, span_ref, r0, res, offs, rb, ls)
                    for o, tap in zip(offs, taps):
                        k = CONV_WIDTH - 1 - o
                        acc = acc + tap() * w_ref[k:k + 1, ls]
                dg_ref[pl.ds(r0, rb), ls] = acc
                dcv = dwin_ref[pl.ds(r0, rb), ls]
                for res, offs in _taps_by_residue(first):
                    taps = _shifted_taps(win_ref, span_ref, r0, res, offs, rb, ls)
                    for o, tap in zip(offs, taps):
                        k = o - first
                        prod = dcv * tap()
                        dwp_ref[k, :, ls] += jnp.sum(prod.reshape(rb // SUBLANES, SUBLANES, LANES), axis=0)
            return carry

        lax.fori_loop(0, nl, lane_chunk, 0)

        @pl.when(i == nt - 1)
        def _():
            for k in range(CONV_WIDTH):
                dw_ref[k:k + 1, :] = jnp.sum(dwp_ref[k], axis=0, keepdims=True)
            dw_ref[CONV_WIDTH:, :] = jnp.zeros((CONV_HALO - CONV_WIDTH, D), F32)
        dg = dg_ref[...]
        ha = ha_ref[...].astype(F32)
        sg = _sigmoid(hg_ref[...].astype(F32))
        d_ha = dg * sg
        d_hg = dg * ha * sg * (1.0 - sg)
        dh_ref[:, 0:D] = d_ha.astype(dh_ref.dtype)
        dh_ref[:, D:] = d_hg.astype(dh_ref.dtype)
        _init_or_add(dbdw_ref, i, jnp.sum(dc_ref[...], axis=0, keepdims=True))
        _init_or_add(dbin_ref, i, jnp.concatenate([jnp.sum(d_ha, axis=0, keepdims=True),
                                                   jnp.sum(d_hg, axis=0, keepdims=True)], axis=1))

    tile = pl.BlockSpec((tm, D), lambda i: (i, 0))
    return _call(
        body, comm, name="dwconv_bwd", grid=(nt,),
        in_specs=[tile,
                  pl.BlockSpec((CONV_HALO, D), lambda i: (jnp.minimum((i + 1) * (tm // CONV_HALO), last), 0)),
                  tile, _halo_before_spec(tm, D),
                  tile, tile, pl.BlockSpec((CONV_HALO, D), lambda i: (0, 0))],
        out_specs=[pl.BlockSpec((tm, 2 * D), lambda i: (i, 0)), pl.BlockSpec((CONV_HALO, D), lambda i: (0, 0)),
                   pl.BlockSpec((1, D), lambda i: (0, 0)), pl.BlockSpec((1, 2 * D), lambda i: (0, 0))],
        out_shape=[_sds((T, 2 * D), BF16), _sds((CONV_HALO, D), F32), _sds((1, D), F32), _sds((1, 2 * D), F32)],
        scratch_shapes=[pltpu.VMEM((tm + CONV_HALO, D), F32), pltpu.VMEM((tm + CONV_HALO, D), F32),
                        pltpu.VMEM((tm, D), F32), pltpu.VMEM((CONV_WIDTH, SUBLANES, D), F32),
                        pltpu.VMEM((rb + CONV_HALO, LANES), F32)],
        sem=("arbitrary",), operands=[dc, dc, g, g, ha, hg, w_dw])


def _attn_specs(HD, W):
    B = ATT_BLOCK
    return [pl.BlockSpec((B, HD), lambda n: (n, 0)),
            pl.BlockSpec((B, 4 * W), lambda n: (n, 0)),
            pl.BlockSpec((B, 4 * W), lambda n: (jnp.maximum(n - 1, 0), 0))]


def _band_mask(n):
    r = lax.broadcasted_iota(jnp.int32, (ATT_BLOCK, 2 * ATT_BLOCK), 0)
    j = lax.broadcasted_iota(jnp.int32, (ATT_BLOCK, 2 * ATT_BLOCK), 1)
    return (j > r) & (j <= r + ATT_BLOCK) & ((n > 0) | (j >= ATT_BLOCK))


def _band(kvc_ref, kvp_ref, part, g, parity, W):
    lanes = slice((2 * part + parity) * W + g * LANES, (2 * part + parity) * W + (g + 1) * LANES)
    return jnp.concatenate([kvp_ref[:, lanes], kvc_ref[:, lanes]], axis=0)


def _half_mask(parity):
    lane = lax.broadcasted_iota(jnp.int32, (1, LANES), 1)
    return (lane < HEAD_DIM) if parity == 0 else (lane >= HEAD_DIM)


def widen_kv(k, v, out_ref, n_kv):
    W = n_kv * LANES
    low = _half_mask(0)
    for part, src in enumerate((k, v)):
        for cg in range(n_kv * HEAD_DIM // LANES):
            chunk = src[:, cg * LANES:(cg + 1) * LANES]
            swapped = pltpu.roll(chunk, HEAD_DIM, 1)
            for g, lo, hi in ((2 * cg, chunk, swapped), (2 * cg + 1, swapped, chunk)):
                base = 2 * part * W + g * LANES
                out_ref[:, base:base + LANES] = jnp.where(low, lo, 0.0).astype(out_ref.dtype)
                out_ref[:, base + W:base + W + LANES] = jnp.where(low, 0.0, hi).astype(out_ref.dtype)


def attn_fwd(q, kvx, sinks, n_heads, n_kv, comm=None):
    T, HD = q.shape
    W = n_kv * LANES
    B = ATT_BLOCK
    chunks_per_group = n_heads // n_kv // 2
    scale = 1.0 / math.sqrt(HEAD_DIM)

    def body(q_ref, kvc_ref, kvp_ref, sink_ref, o_ref, lse_ref):
        n = pl.program_id(0)
        mask = jnp.tile(_band_mask(n), (n_heads, 1))
        s = jnp.concatenate(
            [_dot(q_ref[:, (h // 2) * LANES:(h // 2 + 1) * LANES],
                  _band(kvc_ref, kvp_ref, 0, h // 2 // chunks_per_group, h % 2, W), ((1,), (1,)))
             for h in range(n_heads)], axis=0)
        sink = jnp.concatenate([jnp.broadcast_to(sink_ref[:, h:h + 1], (B, 1)) for h in range(n_heads)], axis=0)
        s = jnp.where(mask, s * scale, MASK_VALUE)
        m = jnp.maximum(jnp.max(s, axis=-1, keepdims=True), sink)
        e = jnp.exp(s - m)
        total = _dot(e, jnp.ones((2 * B, LANES), BF16), ((1,), (0,))) + jnp.exp(sink - m)
        lse = m + jnp.log(total[:, 0:1])
        inv = 1.0 / total
        probs = (e * jnp.concatenate([inv, inv], axis=1)).astype(BF16)
        for c in range(n_heads // 2):
            g = c // chunks_per_group
            out = (_dot(probs[2 * c * B:(2 * c + 1) * B], _band(kvc_ref, kvp_ref, 1, g, 0, W), ((1,), (0,)))
                   + _dot(probs[(2 * c + 1) * B:(2 * c + 2) * B], _band(kvc_ref, kvp_ref, 1, g, 1, W), ((1,), (0,))))
            o_ref[:, c * LANES:(c + 1) * LANES] = out.astype(o_ref.dtype)
        lse_ref[...] = jnp.concatenate([lse[h * B:(h + 1) * B] for h in range(n_heads)], axis=1)

    return _call(
        body, comm, name="attn_fwd", grid=(T // B,),
        in_specs=_attn_specs(HD, W) + [pl.BlockSpec((1, n_heads), lambda n: (0, 0))],
        out_specs=[pl.BlockSpec((B, HD), lambda n: (n, 0)), pl.BlockSpec((B, n_heads), lambda n: (n, 0))],
        out_shape=[_sds((T, HD), BF16), _sds((T, n_heads), F32)],
        sem=("parallel",), operands=[q, kvx, kvx, sinks])


def attn_bwd(q, kvx, do, lse, sinks, rope, n_heads, n_kv, comm=None):
    T, HD = q.shape
    KVD = n_kv * HEAD_DIM
    W = n_kv * LANES
    B = ATT_BLOCK
    chunks_per_group = n_heads // n_kv // 2
    scale = 1.0 / math.sqrt(HEAD_DIM)
    nb = T // B

    def body(q_ref, kvc_ref, kvp_ref, do_ref, lse_ref, sink_ref, c_ref, su_ref, sd_ref, dq_ref, dkv_ref, dsink_ref):
        n = pl.program_id(0)

        @pl.when(n == 0)
        def _():
            dkv_ref[...] = jnp.zeros_like(dkv_ref)
            dsink_ref[...] = jnp.zeros_like(dsink_ref)

        def chunk(ref, h):
            return ref[:, (h // 2) * LANES:(h // 2 + 1) * LANES]

        def band(part, h):
            return _band(kvc_ref, kvp_ref, part, h // 2 // chunks_per_group, h % 2, W)

        def stack(per_head):
            return jnp.concatenate([per_head(h) for h in range(n_heads)], axis=0)

        mask = jnp.tile(_band_mask(n), (n_heads, 1))
        s = stack(lambda h: _dot(chunk(q_ref, h), band(0, h), ((1,), (1,))))
        dp = stack(lambda h: _dot(chunk(do_ref, h), band(1, h), ((1,), (1,))))
        lse = stack(lambda h: lse_ref[:, h:h + 1])
        sink = stack(lambda h: jnp.broadcast_to(sink_ref[:, h:h + 1], (B, 1)))
        probs = jnp.exp(jnp.where(mask, s * scale, MASK_VALUE) - lse)
        delta = jnp.sum(probs * dp, axis=-1, keepdims=True)
        ds = (probs * (dp - delta) * scale).astype(BF16)
        probs = probs.astype(BF16)
        sink_term = jnp.exp(sink - lse) * delta
        dsk = [-jnp.sum(sink_term[h * B:(h + 1) * B], axis=0, keepdims=True) for h in range(n_heads)]

        dk_wide, dv_wide = [None] * n_kv, [None] * n_kv
        for c in range(n_heads // 2):
            g = c // chunks_per_group
            dq2 = None
            for h in (2 * c, 2 * c + 1):
                half = _half_mask(h % 2)
                q2, do2 = chunk(q_ref, h), chunk(do_ref, h)
                ds_h, p_h = ds[h * B:(h + 1) * B], probs[h * B:(h + 1) * B]
                part = _dot(ds_h, band(0, h), ((1,), (0,)))
                dq2 = part if dq2 is None else dq2 + part
                dk_h = _dot(ds_h, jnp.where(half, q2, jnp.zeros_like(q2)), ((0,), (0,)))
                dv_h = _dot(p_h, jnp.where(half, do2, jnp.zeros_like(do2)), ((0,), (0,)))
                dk_wide[g] = dk_h if dk_wide[g] is None else dk_wide[g] + dk_h
                dv_wide[g] = dv_h if dv_wide[g] is None else dv_wide[g] + dv_h
            dq_ref[:, c * LANES:(c + 1) * LANES] = _rope_chunk_bwd(
                dq2, c_ref[...], su_ref[...], sd_ref[...]).astype(dq_ref.dtype)

        def fold(wide):
            low = _half_mask(0)
            both = [w + pltpu.roll(w, HEAD_DIM, 1) for w in wide]
            return jnp.concatenate([jnp.where(low, both[2 * cg], both[2 * cg + 1]) for cg in range(n_kv // 2)], axis=1)

        dkv = jnp.concatenate([fold(dk_wide), fold(dv_wide)], axis=1)
        prev = pl.ds(pl.multiple_of(jnp.maximum(n - 1, 0) * B, B), B)
        cur = pl.ds(pl.multiple_of(n * B, B), B)
        dkv_ref[prev, :] += dkv[0:B, :]
        dkv_ref[cur, :] += dkv[B:, :]
        dsink_ref[...] += jnp.concatenate(dsk, axis=1)

    tab = pl.BlockSpec((B, LANES), lambda n: (n, 0))
    return _call(
        body, comm, name="attn_bwd", grid=(nb,),
        in_specs=_attn_specs(HD, W) + [pl.BlockSpec((B, HD), lambda n: (n, 0)),
                                       pl.BlockSpec((B, n_heads), lambda n: (n, 0)),
                                       pl.BlockSpec((1, n_heads), lambda n: (0, 0)), tab, tab, tab],
        out_specs=[pl.BlockSpec((B, HD), lambda n: (n, 0)), pl.BlockSpec((T, 2 * KVD), lambda n: (0, 0)),
                   pl.BlockSpec((1, n_heads), lambda n: (0, 0))],
        out_shape=[_sds((T, HD), BF16), _sds((T, 2 * KVD), F32), _sds((1, n_heads), F32)],
        sem=("arbitrary",), operands=[q, kvx, kvx, do, lse, sinks, *rope])


def chip_sums(name, pairs):
    n = len(pairs)

    def body(core_ref, *refs):
        del core_ref
        for k in range(n):
            refs[2 * n + k][...] = (refs[2 * k][...].astype(F32) + refs[2 * k + 1][...].astype(F32)
                                    ).astype(refs[2 * n + k].dtype)

    in_specs, out_specs, out_shape, operands = [], [], [], []
    for g, p_sib in pairs:
        _, a, b = g.shape
        in_specs += [pl.BlockSpec((None, None, a, b), lambda q, core: (q, core[0], 0, 0)),
                     pl.BlockSpec((None, a, b), lambda q, core: (q, 0, 0))]
        out_specs.append(pl.BlockSpec((None, a, b), lambda q, core: (q, 0, 0)))
        out_shape.append(_sds((N_CHIPS, a, b), g.dtype))
        operands += [g.reshape(N_CHIPS, 2, a, b), p_sib]
    my_core = lax.axis_index("c").astype(jnp.int32).reshape(1)
    return pl.pallas_call(
        body, name=name, out_shape=out_shape,
        grid_spec=pltpu.PrefetchScalarGridSpec(num_scalar_prefetch=1, grid=(N_CHIPS,), in_specs=in_specs,
                                               out_specs=out_specs),
        compiler_params=_params(("arbitrary",)))(my_core, *operands)


def adamw(name, recvs, w, m, v, *, ta):
    L, a, b = w.shape
    n_terms = recvs[0].shape[0]
    assert a % ta == 0 and len(recvs) == L
    c1 = 1.0 - ADAM_B1 ** ADAM_STEP
    c2 = 1.0 - ADAM_B2 ** ADAM_STEP

    def body(*refs):
        r_refs = refs[:L]
        w_ref, m_ref, v_ref, g_ref, d_ref, nm_ref, nv_ref = refs[L:]
        layer = pl.program_id(0)
        for l in range(L):
            @pl.when(layer == l)
            def _(r_ref=r_refs[l]):
                g = r_ref[0].astype(F32)
                for s in range(1, n_terms):
                    g = g + r_ref[s].astype(F32)
                nm = ADAM_B1 * m_ref[...] + (1.0 - ADAM_B1) * g
                nv = ADAM_B2 * v_ref[...] + (1.0 - ADAM_B2) * jnp.square(g)
                m_hat = nm / c1
                v_hat = nv / c2
                g_ref[...] = g
                d_ref[...] = -ADAM_LR * (m_hat / (jnp.sqrt(v_hat) + ADAM_EPS) + ADAM_WD * w_ref[...])
                nm_ref[...] = nm
                nv_ref[...] = nv

    blk = pl.BlockSpec((None, ta, b), lambda l, i: (l, i, 0))
    out = _sds((L, a, b), F32)
    r_specs = [pl.BlockSpec((n_terms, ta, b), lambda l, i, ll=ll: (0, jnp.where(l == ll, i, 0), 0)) for ll in range(L)]
    return pl.pallas_call(
        body, name=name, grid=(L, a // ta), in_specs=r_specs + [blk, blk, blk],
        out_specs=[blk, blk, blk, blk], out_shape=[out, out, out, out],
        compiler_params=_params(("arbitrary", "arbitrary")))(*recvs, w, m, v)


def _pack_rows(parts):
    out = []
    for arr, rows in parts:
        arr = arr.reshape(-1, LANES).astype(F32)
        out.append(jnp.pad(arr, ((0, rows - arr.shape[0]), (0, 0))))
    return jnp.concatenate(out, axis=0)


def _small_rows(a, prefix):
    return _pack_rows([(a[prefix + name], rows) for name, rows in SMALL_SHARDED])


def _unpack_small(packed, a):
    out, r0 = {}, 0
    for name, rows in SMALL_SHARDED:
        shape = a[name].shape
        used = math.prod(shape) // LANES
        out[name] = packed[r0:r0 + used].reshape(shape)
        r0 += rows
    return out


def _rep_rows(a, prefix):
    parts = []
    for name in REPLICATED:
        arr = a[prefix + name]
        if arr.size % LANES:
            arr = jnp.pad(arr.reshape(1, -1), ((0, 0), (0, LANES - arr.size % LANES)))
        rows = -(-arr.size // LANES)
        parts.append((arr, -(-rows // SUBLANES) * SUBLANES))
    return _pack_rows(parts)


def _unpack_rep(packed, a):
    out, r0 = {}, 0
    for name in REPLICATED:
        shape = a[name].shape
        size = math.prod(shape)
        rows = -(-size // LANES)
        out[name] = packed[r0:r0 + rows].reshape(-1)[:size].reshape(shape)
        r0 += -(-rows // SUBLANES) * SUBLANES
    return out


def _step(a):
    x = a['x'][0]
    T, D = x.shape
    tgt = a['loss_target'][0]
    p_in = [a['p'][i, 0] for i in range(DEPTH)]
    PLE = p_in[0].shape[1]
    n_heads = a['attn_sinks'].shape[1]
    HD = n_heads * HEAD_DIM
    KVD = a['kv_w_k'].shape[1]
    n_kv = KVD // HEAD_DIM
    F = a['mlp_w_down'].shape[1] * N_DEV
    tm = min(TOKEN_TILE, T)
    tm2 = min(2 * TOKEN_TILE, T)
    tmc = min(TOKEN_TILE, T)
    tw = 512
    alpha = DEEPNORM_ALPHA
    xb, p_b = x, p_in

    def shard3(w):
        return w.reshape((1,) + w.shape) if w.ndim == 2 else w

    def gather(*specs):
        return _GatherJob([shard3(a[nm])[li].astype(BF16) for nm, li in specs])

    (W_in, small_full), = comm_only("gather_first",
                                    [_GatherJob([a['conv_w_in'][0].astype(BF16), _small_rows(a, '')])])
    r0, small = 0, {}
    for name, rows in SMALL_SHARDED:
        small[name] = small_full[:, r0:r0 + rows]
        r0 += rows
    b_in = small['conv_b_in'][:, 0:2 * D // N_DEV // LANES].reshape(1, 2 * D)
    w_dw = jnp.transpose(small['conv_w_dw'], (1, 0, 2)).reshape(CONV_HALO, D)
    b_dw, cln_g, cln_b, b_out = (small[nm][:, 0].reshape(1, D) for nm in
                                 ('conv_b_dw', 'conv_ln_g', 'conv_ln_b', 'conv_b_out'))
    W_up, W_down, W_proj, W_gate = {}, {}, {}, {}
    mix_g, mix_b, mlp_g, mlp_b = a['mix_ln_g'], a['mix_ln_b'], a['mlp_ln_g'], a['mlp_ln_b']
    rope = _rope_tables(T)

    def set_ple_weights(li, g_proj, g_gate):
        W_proj[li] = jnp.transpose(g_proj, (1, 0, 2)).reshape(1, PLE, D)
        W_gate[li] = g_gate.reshape(1, D, D)

    def row(v, i):
        return v[i:i + 1]

    def res_ln_epi(coef):
        def epi(accs, ex, out, i):
            acc = accs[0] if isinstance(accs, list) else accs
            n_ex = len(ex)
            res_ref, g_ref, b_ref = ex[n_ex - 3], ex[n_ex - 2], ex[n_ex - 1]
            z = coef * res_ref[...] + acc
            if n_ex == 4:
                z = z + ex[0][...]
            out[0][...] = z
            xo = _ln_fwd(z, g_ref[...], b_ref[...])
            out[1][...] = xo
            out[2][...] = xo.astype(BF16)
        return epi

    res_ln_outs = [(_sds((T, D), F32), 'tile'), (_sds((T, D), F32), 'tile'), (_sds((T, D), BF16), 'tile')]

    def mlp_fwd(li, xin, xin_b, down_comm=None):
        def up_epi(accs, ex, out, i):
            out[0][...] = accs[0].astype(BF16)
        res = mm_nn(f"mlp_up_{li}", [(xin_b, W_up[li], 0)], [], [(_sds((T, F), BF16), 'tile')], up_epi, tm=tm2,
                    tn=min(1024, F), comm=None if li in W_down else gather(('mlp_w_down', li)))
        if li in W_down:
            u, = res
        else:
            (u,), got_up = res
            W_down[li] = got_up[0].reshape(1, F, D)
        res = mm_nn(f"mlp_down_{li}", [(u, W_down[li], 0)],
                    [(xin, 'tile'), (row(mlp_g, li), 'row'), (row(mlp_b, li), 'row')],
                    res_ln_outs, res_ln_epi(alpha), tm=tm, tn=D, tk=F, comm=down_comm, a_pro=_sq_relu)
        (z, xo, xo_b), got_down = res if down_comm is not None else (res, ())
        return u, z, xo, xo_b, got_down

    def ple_fwd(li, xin, xin_b, with_loss, comm=None):
        def epi(accs, ex, out, i):
            pp, gg = accs
            xo = ex[0][...] + pp * _sigmoid(gg)
            out[1][...] = pp.astype(BF16)
            out[2][...] = gg.astype(BF16)
            if with_loss:
                err = xo - ex[1][...]
                out[0][...] = err * (1.0 / D)
                _init_or_add(out[3], i, jnp.sum(err * err, axis=0, keepdims=True) * (0.5 / D))
            else:
                out[0][...] = xo
                out[3][...] = xo.astype(BF16)
        extras = [(xin, 'tile')] + ([(tgt, 'tile')] if with_loss else [])
        outs = [(_sds((T, D), F32), 'tile'), (_sds((T, D), BF16), 'tile'), (_sds((T, D), BF16), 'tile')]
        outs.append((_sds((1, D), F32), 'rowacc') if with_loss else (_sds((T, D), BF16), 'tile'))
        return mm_nn(f"ple_{li}", [(p_b[li], W_proj[li], 0), (xin_b, W_gate[li], 0)], extras, outs, epi, tm=tm, tn=D,
                     comm=comm)

    assert D // N_DEV == LANES
    (g0, ha0, hg0), got = _glu(xb, W_in, b_in, T, D, tm,
                               gather(('conv_w_out', 0), ('ple_w_proj', 0), ('ple_w_gate', 0)))
    W_out = got[0].reshape(1, D, D)
    set_ple_weights(0, got[1], got[2])
    (c0, s0), (W_up[0],) = dwconv_fwd(g0, w_dw, b_dw, cln_g, cln_b, tm=tmc, comm=gather(('mlp_w_up', 0)))
    z1, x1, x1b = mm_nn("conv_out", [(s0, W_out, 0)],
                        [(b_out, 'row'), (x, 'tile'), (row(mix_g, 0), 'row'), (row(mix_b, 0), 'row')],
                        res_ln_outs, res_ln_epi(alpha), tm=tm, tn=D)
    u0, z2, x2, x2b, got = mlp_fwd(
        0, x1, x1b, down_comm=gather(('attn_w_q', 0), ('kv_w_k', 0), ('kv_w_v', 0), ('attn_w_o', 0),
                                     ('ple_w_proj', 1), ('ple_w_gate', 1)))
    W_qkv = jnp.concatenate([got[0].reshape(D, HD), got[1].reshape(D, KVD), got[2].reshape(D, KVD)], axis=1)[None]
    W_o = got[3].reshape(1, HD, D)
    set_ple_weights(1, got[4], got[5])
    x3, pp0, gg0, x3b = ple_fwd(0, x2, x2b, False)

    def qkv_epi(accs, ex, out, i):
        t = accs[0]
        c, su, sd = ex[0][...], ex[1][...], ex[2][...]
        out[0][...] = _rope_wide(t[:, 0:HD], c, su, sd, _rope_chunk).astype(BF16)
        widen_kv(_rope_wide(t[:, HD:HD + KVD], c, su, sd, _rope_chunk), t[:, HD + KVD:], out[1], n_kv)
    NQ = HD + 2 * KVD
    q1, kvx1 = mm_nn("qkv_rope", [(x3b, W_qkv, 0)], [(t, 'tab') for t in rope],
                     [(_sds((T, HD), BF16), 'rows'), (_sds((T, 4 * n_kv * LANES), BF16), 'rows')], qkv_epi,
                     tm=tm, tn=NQ)
    (o1, lse1), (W_up[1], g_down1) = attn_fwd(q1, kvx1, a['attn_sinks'], n_heads, n_kv,
                                              comm=gather(('mlp_w_up', 1), ('mlp_w_down', 1)))
    W_down[1] = g_down1.reshape(1, F, D)
    z3, x4, x4b = mm_nn("attn_out", [(o1, W_o, 0)], [(x3, 'tile'), (row(mix_g, 1), 'row'), (row(mix_b, 1), 'row')],
                        res_ln_outs, res_ln_epi(alpha), tm=tm, tn=D)
    u1, z4, x5, x5b, _ = mlp_fwd(1, x4, x4b)
    dy, pp1, gg1, loss_row = ple_fwd(1, x5, x5b, True)
    loss_local = jnp.sum(loss_row)

    grads = {}

    def ln_bwd_epi(coef, with_colsum):
        def epi(acc, ex, out, i):
            d_x = acc + coef * ex[0][...]
            dz, dg, db = _ln_bwd(d_x, ex[1][...], ex[2][...])
            out[0][...] = dz
            out[1][...] = dz.astype(BF16)
            _init_or_add(out[2], i, dg)
            _init_or_add(out[3], i, db)
            if with_colsum:
                _init_or_add(out[4], i, jnp.sum(dz, axis=0, keepdims=True))
        return epi

    def ln_bwd_outs(with_colsum):
        outs = [(_sds((T, D), F32), 'tile'), (_sds((T, D), BF16), 'tile'), (_sds((1, D), F32), 'rowacc'),
                (_sds((1, D), F32), 'rowacc')]
        return outs + ([(_sds((1, D), F32), 'rowacc')] if with_colsum else [])

    def ple_bwd(li, d_out, xin, pp, gg, z_mlp, pair_specs=None, chip_keys=None):
        side = {}

        def gate_grads(ex):
            d = ex[0][...]
            sg = _sigmoid(ex[4][...].astype(F32))
            side['d_pp'] = (d * sg).astype(BF16)
            side['d_gg'] = (d * ex[3][...].astype(F32) * sg * (1.0 - sg)).astype(BF16)
            return side['d_gg']

        jobs = None
        if pair_specs:
            jobs = _Jobs([pair_stage(*pair_specs)] + ([chip_stage(*chip_keys)] if chip_keys else []))
        res = mm_nt(f"ple_dx_{li}", _sds((T, D), BF16), W_gate[li],
                    [(d_out, 'tile'), (z_mlp, 'tile'), (row(mlp_g, li), 'row'), (pp, 'tile'), (gg, 'tile')],
                    ln_bwd_outs(False), ln_bwd_epi(1.0, False), tm=tm, tko=D, tc=D, comm=jobs, a_pro=gate_grads,
                    dws=[(xin, lambda: side['d_gg']), (p_b[li], lambda: side['d_pp'])])
        if jobs is not None:
            res = split_hosted(res, jobs, chips=[1] if chip_keys else [], pairs=[0])
        dz, dzb, dg, db, dw_gate, dw_proj = res
        grads[('mlp_ln_g', li)], grads[('mlp_ln_b', li)] = dg, db
        grads[('ple_w_gate', li)], grads[('ple_w_proj', li)] = dw_gate, dw_proj
        return dz, dzb

    recv = {}
    wqkv_cols = {'attn_w_q': (0, HD), 'kv_w_k': (HD, HD + KVD), 'kv_w_v': (HD + KVD, NQ)}

    def piece(name, li):
        if name == 'conv_w_in':
            return grads['conv_w_in']
        if name == 'mlp_w_up':
            return grads[('mlp_w_up', li)]
        if name == 'ple_w_proj':
            return jnp.transpose(grads[('ple_w_proj', li)][0].reshape(PLE, N_DEV, D // N_DEV), (1, 0, 2))
        if name in wqkv_cols:
            g = grads['w_qkv'][:, wqkv_cols[name][0]:wqkv_cols[name][1]]
        else:
            g = grads[name] if name in grads else grads[(name, li)]
            g = g[0]
        return g.reshape(N_DEV, g.shape[0] // N_DEV, g.shape[1])

    def pair_stage(*specs):
        job = _PairJob([piece(nm, li) for nm, li in specs])
        job.specs = specs
        return job

    sums = {}

    def pair_done(pair_job, got):
        name = "chip_sum_" + "_".join(f"{nm}_{li}" for nm, li in pair_job.specs)
        for spec, total in zip(pair_job.specs, chip_sums(name, list(zip(pair_job.sources, got)))):
            sums[spec] = total

    def chip_stage(*keys):
        job = _ChipJob([sums[k] for k in keys])
        job.specs = keys
        return job

    def hosted(res, job):
        main, got = res
        for spec, r in zip(job.specs, got):
            recv[spec] = r
        return main

    def split_hosted(res, jobs, chips, pairs):
        main, got = res
        parts = jobs.split(got)
        for k in chips:
            hosted((None, parts[k]), jobs.jobs[k])
        for k in pairs:
            pair_done(jobs.jobs[k], parts[k])
        return main

    def mlp_bwd(li, dz, dzb, xin, u, z_mix, with_colsum, du_pairs, du_chips, dx_chips=None):
        jobs = _Jobs([pair_stage(*du_pairs)] + ([chip_stage(*du_chips)] if du_chips else []))
        du, grads[('mlp_w_up', li)], grads[('mlp_w_down', li)] = split_hosted(
            mlp_du_dw(f"mlp_du_{li}", dzb, W_down[li], u, xin, F // N_DEV, tm=tm2, tf=min(1024, F), comm=jobs),
            jobs, chips=[1] if du_chips else [], pairs=[0])
        jobs = _Jobs([pair_stage(('mlp_w_up', li), ('mlp_w_down', li))] + ([chip_stage(*dx_chips)] if dx_chips else []))
        res = split_hosted(
            mm_nt(f"mlp_dx_{li}", du, W_up[li], [(dz, 'tile'), (z_mix, 'tile'), (row(mix_g, li), 'row')],
                  ln_bwd_outs(with_colsum), ln_bwd_epi(alpha, with_colsum), tm=tm, tko=D, tc=F, comm=jobs),
            jobs, chips=[1] if dx_chips else [], pairs=[0])
        grads[('mix_ln_g', li)], grads[('mix_ln_b', li)] = res[2], res[3]
        return res

    dz4, dz4b = ple_bwd(1, dy, x5b, pp1, gg1, z4)
    dz3, dz3b, _, _ = mlp_bwd(1, dz4, dz4b, x4b, u1, z3, False, [('ple_w_gate', 1), ('ple_w_proj', 1)], None)

    def do_epi(acc, ex, out, i):
        out[0][...] = acc.astype(BF16)
    do1, grads['attn_w_o'] = mm_nt("attn_do", dz3b, W_o, [], [(_sds((T, HD), BF16), 'tile')], do_epi, tm=tm, tko=HD,
                                   tc=D, dws=[(o1, None)])
    jobs = _Jobs([chip_stage(('mlp_w_up', 1)), pair_stage(('attn_w_o', 0)),
                  chip_stage(('ple_w_gate', 1), ('ple_w_proj', 1))])
    dq, dkv, d_sinks = split_hosted(
        attn_bwd(q1, kvx1, do1, lse1, a['attn_sinks'], rope, n_heads, n_kv, comm=jobs), jobs, chips=[0, 2], pairs=[1])

    def d_qkv_tile(ex):
        dk = _rope_wide(ex[2][:, 0:KVD], ex[3][...], ex[4][...], ex[5][...], _rope_chunk_bwd)
        return jnp.concatenate([ex[1][...], dk.astype(BF16), ex[2][:, KVD:].astype(BF16)], axis=1)

    def dx3_epi(acc, ex, out, i):
        out[0][...] = acc + alpha * ex[0][...]
    dx3, dw_qkv = mm_nt("attn_dx", _sds((T, NQ), BF16), W_qkv,
                        [(dz3, 'tile'), (dq, 'rows'), (dkv, 'rows')] + [(t, 'tab') for t in rope],
                        [(_sds((T, D), F32), 'tile')], dx3_epi, tm=tm, tko=D, tc=NQ, a_pro=d_qkv_tile,
                        dws=[(x3b, None)])
    grads['w_qkv'] = dw_qkv[0]

    dz2, dz2b = ple_bwd(0, dx3, x2b, pp0, gg0, z2, [('attn_w_q', 0), ('kv_w_k', 0), ('kv_w_v', 0)])
    dz1, dz1b, _, _, db_out = mlp_bwd(
        0, dz2, dz2b, x1b, u0, z1, True, [('ple_w_gate', 0), ('ple_w_proj', 0)],
        [('attn_w_q', 0), ('kv_w_k', 0), ('kv_w_v', 0), ('mlp_w_down', 1), ('attn_w_o', 0)],
        dx_chips=[('ple_w_gate', 0), ('ple_w_proj', 0)])

    def ds_epi(acc, ex, out, i):
        n = _ln_fwd(ex[0][...], ex[1][...], ex[2][...])
        sg = _sigmoid(n)
        dn = acc * (sg * (1.0 + n * (1.0 - sg)))
        dc, dg, db = _ln_bwd(dn, ex[0][...], ex[1][...])
        out[0][...] = dc
        _init_or_add(out[1], i, dg)
        _init_or_add(out[2], i, db)
    dc0, d_cln_g, d_cln_b, grads['conv_w_out'] = mm_nt(
        "conv_ds", dz1b, W_out, [(c0, 'tile'), (cln_g, 'row'), (cln_b, 'row')],
        [(_sds((T, D), F32), 'tile'), (_sds((1, D), F32), 'rowacc'), (_sds((1, D), F32), 'rowacc')], ds_epi,
        tm=tm, tko=D, tc=D, dws=[(s0, None)])
    job = chip_stage(('mlp_w_up', 0), ('mlp_w_down', 0))
    dh0, d_wdw, d_bdw, d_bin = hosted(dwconv_bwd(dc0, g0, ha0, hg0, w_dw, tm=tmc, comm=job), job)
    grads['conv_w_in'] = mm_tn("d_conv_win", xb, dh0, 2 * D // N_DEV, BF16, tm=T, tk=tw, tn=tw)

    pair = pair_stage(('conv_w_in', 0), ('conv_w_out', 0))
    pair_done(pair, comm_only("pair_last", [pair])[0])
    last_chip = chip_stage(('conv_w_in', 0), ('conv_w_out', 0))

    def own_rows(vec, rows_used, rows):
        arr = vec.reshape(N_DEV, rows_used, LANES)
        return jnp.pad(arr, ((0, 0), (0, rows - rows_used), (0, 0)))
    dwdw_dev = jnp.transpose(d_wdw.reshape(CONV_HALO, N_DEV, D // N_DEV), (1, 0, 2))
    lane_rows = D // N_DEV // LANES
    small_grad = jnp.concatenate([
        own_rows(d_bin, 2 * lane_rows, 8), dwdw_dev if lane_rows == 1 else dwdw_dev.reshape(N_DEV, -1, LANES),
        own_rows(d_bdw, lane_rows, 8), own_rows(d_cln_g, lane_rows, 8), own_rows(d_cln_b, lane_rows, 8),
        own_rows(db_out, lane_rows, 8)], axis=1)
    n_small = small_grad.shape[1]

    rep_local = {'mix_ln_g': jnp.concatenate([grads[('mix_ln_g', li)] for li in range(DEPTH)], axis=0),
                 'mix_ln_b': jnp.concatenate([grads[('mix_ln_b', li)] for li in range(DEPTH)], axis=0),
                 'mlp_ln_g': jnp.concatenate([grads[('mlp_ln_g', li)] for li in range(DEPTH)], axis=0),
                 'mlp_ln_b': jnp.concatenate([grads[('mlp_ln_b', li)] for li in range(DEPTH)], axis=0),
                 'attn_sinks': d_sinks}
    rep_grad = _rep_rows(rep_local, '')
    n_rep = rep_grad.shape[0]
    last = _Jobs([last_chip, _DirectJob([small_grad, jnp.broadcast_to(rep_grad[None], (N_DEV, n_rep, LANES))])])

    def dx_epi(acc, ex, out, i):
        out[0][...] = acc + alpha * ex[0][...]
    (grad_x,), got = mm_nt("conv_dx", dh0, W_in, [(dz1, 'tile')], [(_sds((T, D), F32), 'tile')], dx_epi,
                           tm=tm, tko=D, tc=D, comm=last)
    got, (recv_small, recv_rep) = last.split(got)
    hosted((None, got), last_chip)

    result = {}
    kinds = ('grad', 'delta', 'new_m', 'new_v')
    w, m, v = (_small_rows(a, pre)[None] for pre in ('', 'm_', 'v_'))
    for kind, arr in zip(kinds, adamw("adamw_small", [recv_small], w, m, v, ta=n_small)):
        for pname, val in _unpack_small(arr[0], a).items():
            result[(kind, pname)] = val
    w, m, v = (_rep_rows(a, pre)[None] for pre in ('', 'm_', 'v_'))
    for kind, arr in zip(kinds, adamw("adamw_rep", [recv_rep], w, m, v, ta=n_rep)):
        for pname, val in _unpack_rep(arr[0], a).items():
            result[(kind, pname)] = val
    for name in BIG_WEIGHTS:
        w, m, v = (shard3(a[pre + name]) for pre in ('', 'm_', 'v_'))
        recvs = [recv[(name, li)] for li in range(w.shape[0])]
        for kind, arr in zip(kinds, adamw("adamw_" + name, recvs, w, m, v, ta=min(256, w.shape[1]))):
            result[(kind, name)] = arr.reshape(a[name].shape)

    loss = lax.psum(loss_local, ("x", "y", "c"))
    out = [loss, grad_x[None]]
    for kind in ('grad', 'delta', 'new_m', 'new_v'):
        out += [result[(kind, name)] for name in WEIGHT_NAMES]
    return tuple(out)


def _glu(x, W_in, b_in, T, D, tm, comm=None):
    n = W_in.shape[2]
    q = 2 if D // n % 2 == 0 else 1
    nt = D // (q * n)
    tn = q * n

    def body(x_ref, wa_ref, wg_ref, ba_ref, bg_ref, g_ref, ha_ref, hg_ref):
        xb = x_ref[...].astype(BF16)
        ha = jnp.concatenate([_dot(xb, wa_ref[s], ((1,), (0,))) for s in range(q)], axis=1) + ba_ref[...]
        hg = jnp.concatenate([_dot(xb, wg_ref[s], ((1,), (0,))) for s in range(q)], axis=1) + bg_ref[...]
        g_ref[...] = ha * _sigmoid(hg)
        ha_ref[...] = ha.astype(ha_ref.dtype)
        hg_ref[...] = hg.astype(hg_ref.dtype)

    return _call(
        body, comm, name="conv_in_glu", grid=(T // tm, nt),
        in_specs=[pl.BlockSpec((tm, D), lambda i, j: (i, 0)),
                  pl.BlockSpec((q, D, n), lambda i, j: (j, 0, 0)),
                  pl.BlockSpec((q, D, n), lambda i, j: (j + nt, 0, 0)),
                  pl.BlockSpec((1, tn), lambda i, j: (0, j)), pl.BlockSpec((1, tn), lambda i, j: (0, j + nt))],
        out_specs=[pl.BlockSpec((tm, tn), lambda i, j: (i, j))] * 3,
        out_shape=[_sds((T, D), F32), _sds((T, D), BF16), _sds((T, D), BF16)],
        sem=("parallel", "parallel"), operands=[x, W_in, W_in, b_in, b_in])


def kernel(x, p, conv_w_in, conv_b_in, conv_w_dw, conv_b_dw, conv_ln_g, conv_ln_b, conv_w_out, conv_b_out, kv_w_k, kv_w_v, attn_w_q, attn_sinks, attn_w_o, mix_ln_g, mix_ln_b, mlp_w_up, mlp_w_down, mlp_ln_g, mlp_ln_b, ple_w_proj, ple_w_gate, loss_target, m_conv_w_in, m_conv_b_in, m_conv_w_dw, m_conv_b_dw, m_conv_ln_g, m_conv_ln_b, m_conv_w_out, m_conv_b_out, m_kv_w_k, m_kv_w_v, m_attn_w_q, m_attn_sinks, m_attn_w_o, m_mix_ln_g, m_mix_ln_b, m_mlp_w_up, m_mlp_w_down, m_mlp_ln_g, m_mlp_ln_b, m_ple_w_proj, m_ple_w_gate, v_conv_w_in, v_conv_b_in, v_conv_w_dw, v_conv_b_dw, v_conv_ln_g, v_conv_ln_b, v_conv_w_out, v_conv_b_out, v_kv_w_k, v_kv_w_v, v_attn_w_q, v_attn_sinks, v_attn_w_o, v_mix_ln_g, v_mix_ln_b, v_mlp_w_up, v_mlp_w_down, v_mlp_ln_g, v_mlp_ln_b, v_ple_w_proj, v_ple_w_gate):
    return _step(dict(locals()))
```

```python
import functools
import math

import jax
import jax.numpy as jnp
from jax import lax
from jax.experimental import pallas as pl
from jax.experimental.pallas import tpu as pltpu

F32 = jnp.float32
BF16 = jnp.bfloat16

N_DEV = 8
HEAD_DIM = 64
ROPE_DIM = HEAD_DIM // 4
ROPE_HALF = ROPE_DIM // 2
ROPE_THETA = 500000.0
ATT_BLOCK = 128
CONV_WIDTH = 31
CONV_HALO = 32
CONV_ROWS = 64
LN_EPS = 1e-5
DEPTH = 2
DEEPNORM_ALPHA = (2 * DEPTH) ** 0.25
MASK_VALUE = -1e30

ADAM_LR = 0.001
ADAM_B1 = 0.9
ADAM_B2 = 0.999
ADAM_EPS = 1e-08
ADAM_WD = 0.01
ADAM_STEP = 10

LANES = 128
SUBLANES = 8
VMEM_LIMIT_BYTES = 52 * 1024 * 1024
TOKEN_TILE = 512
MESH_ID = pl.DeviceIdType.MESH
RELAY_AT_TENTHS = 5
FORWARD_AT_TENTHS = 8

WEIGHT_NAMES = ['conv_w_in', 'conv_b_in', 'conv_w_dw', 'conv_b_dw', 'conv_ln_g', 'conv_ln_b', 'conv_w_out',
                'conv_b_out', 'kv_w_k', 'kv_w_v', 'attn_w_q', 'attn_sinks', 'attn_w_o', 'mix_ln_g', 'mix_ln_b',
                'mlp_w_up', 'mlp_w_down', 'mlp_ln_g', 'mlp_ln_b', 'ple_w_proj', 'ple_w_gate']
BIG_WEIGHTS = ['conv_w_in', 'conv_w_out', 'kv_w_k', 'kv_w_v', 'attn_w_q', 'attn_w_o', 'mlp_w_up', 'mlp_w_down',
               'ple_w_proj', 'ple_w_gate']
SMALL_SHARDED = [('conv_b_in', 8), ('conv_w_dw', 32), ('conv_b_dw', 8), ('conv_ln_g', 8), ('conv_ln_b', 8),
                 ('conv_b_out', 8)]
REPLICATED = ['mix_ln_g', 'mix_ln_b', 'mlp_ln_g', 'mlp_ln_b', 'attn_sinks']


def _params(sem):
    return pltpu.CompilerParams(dimension_semantics=sem, vmem_limit_bytes=VMEM_LIMIT_BYTES)


def _sds(shape, dtype):
    return jax.ShapeDtypeStruct(shape, dtype)


def _my_place():
    x, y, c = lax.axis_index("x"), lax.axis_index("y"), lax.axis_index("c")
    return x, y, c, 4 * x + 2 * y + c


def _peers(x, y, c):
    out = []
    for dx in (0, 1):
        for dy in (0, 1):
            for dc in (0, 1):
                if dx or dy or dc:
                    px, py, pc = x ^ dx, y ^ dy, c ^ dc
                    out.append(((px, py, pc), 4 * px + 2 * py + pc))
    return out


N_CHIPS = 4


def _other_chips(x, y):
    return [((x ^ dx, y ^ dy), 2 * (x ^ dx) + (y ^ dy)) for dx, dy in ((1, 0), (0, 1), (1, 1))]


def _remote(src, dst, send, recv, to):
    return pltpu.make_async_remote_copy(src_ref=src, dst_ref=dst, send_sem=send, recv_sem=recv, device_id=to,
                                        device_id_type=MESH_ID)


def _wait_slabs(buf, count, send, recv, me, sent=True, received=True):
    part = buf.at[pl.ds(0, count)]
    cp = _remote(part, part, send, recv, me)
    if sent:
        cp.wait_send()
    if received:
        cp.wait_recv()


class _DirectJob:
    n_sems = 3

    def __init__(self, items):
        self.sources = list(items)
        self.dests = [_sds(it.shape, it.dtype) for it in items]
        self.n = len(items)

    def start(self, src, dst, send, recv, loc):
        x, y, c, me = _my_place()
        for k in range(self.n):
            here = dst[k].at[pl.ds(me, 1)]
            pltpu.make_async_copy(src[k].at[pl.ds(me, 1)], here, loc.at[k]).start()
            for peer, idx in _peers(x, y, c):
                _remote(src[k].at[pl.ds(idx, 1)], here, send.at[k], recv.at[k], peer).start()

    def relay(self, *refs):
        pass

    def forward(self, *refs):
        pass

    def finish(self, src, dst, send, recv, loc):
        x, y, c, me = _my_place()
        for k in range(self.n):
            pltpu.make_async_copy(src[k].at[pl.ds(me, 1)], dst[k].at[pl.ds(me, 1)], loc.at[k]).wait()
            _wait_slabs(dst[k], N_DEV - 1, send.at[k], recv.at[k], (x, y, c))


class _GatherJob:
    n_sems = 7

    def __init__(self, items):
        self.sources = [it.reshape((1,) + it.shape) for it in items]
        self.dests = [_sds((N_DEV,) + it.shape, it.dtype) for it in items]
        self.n = len(items)

    @staticmethod
    def _neighbours(x, y, c):
        out = []
        for dx, dy in ((c, 1 - c), (1 - c, c), (1, 1)):
            px, py = x ^ dx, y ^ dy
            out.append(((px, py, c), 4 * px + 2 * py + c))
        return out

    def start(self, src, dst, send_a, recv_a, send_b, recv_b, send_d, recv_d, loc):
        x, y, c, me = _my_place()
        (first, _), (second, _), _ = self._neighbours(x, y, c)
        for k in range(self.n):
            here = dst[k].at[pl.ds(me, 1)]
            pltpu.make_async_copy(src[k], here, loc.at[k]).start()
            _remote(src[k], here, send_d.at[k], recv_d.at[k], (x, y, 1 - c)).start()
            _remote(src[k], here, send_a.at[k], recv_a.at[k], first).start()
            _remote(src[k], here, send_b.at[k], recv_b.at[k], second).start()

    def relay(self, src, dst, send_a, recv_a, send_b, recv_b, send_d, recv_d, loc):
        x, y, c, me = _my_place()
        (_, first_idx), (second, _), _ = self._neighbours(x, y, c)
        for k in range(self.n):
            _wait_slabs(dst[k], 1, send_a.at[k], recv_a.at[k], (x, y, c), sent=False)
            rows = dst[k].at[pl.ds(first_idx, 1)]
            _remote(rows, rows, send_b.at[k], recv_b.at[k], second).start()

    def forward(self, src, dst, send_a, recv_a, send_b, recv_b, send_d, recv_d, loc):
        x, y, c, me = _my_place()
        for k in range(self.n):
            _wait_slabs(dst[k], 2, send_b.at[k], recv_b.at[k], (x, y, c), sent=False)
            for _, idx in self._neighbours(x, y, c):
                rows = dst[k].at[pl.ds(idx, 1)]
                _remote(rows, rows, send_d.at[k], recv_d.at[k], (x, y, 1 - c)).start()

    def finish(self, src, dst, send_a, recv_a, send_b, recv_b, send_d, recv_d, loc):
        x, y, c, me = _my_place()
        for k in range(self.n):
            pltpu.make_async_copy(src[k], dst[k].at[pl.ds(me, 1)], loc.at[k]).wait()
            _wait_slabs(dst[k], 1, send_a.at[k], recv_a.at[k], (x, y, c), received=False)
            _wait_slabs(dst[k], 2, send_b.at[k], recv_b.at[k], (x, y, c), received=False)
            _wait_slabs(dst[k], N_CHIPS, send_d.at[k], recv_d.at[k], (x, y, c))


class _PairJob:
    n_sems = 2

    def __init__(self, items):
        self.sources = list(items)
        self.dests = [_sds((N_CHIPS,) + it.shape[1:], it.dtype) for it in items]
        self.n = len(items)

    def start(self, src, dst, send, recv):
        x, y, c, me = _my_place()
        for k in range(self.n):
            for q in range(N_CHIPS):
                _remote(src[k].at[pl.ds(2 * q + 1 - c, 1)], dst[k].at[pl.ds(q, 1)], send.at[k], recv.at[k],
                        (x, y, 1 - c)).start()

    def relay(self, *refs):
        pass

    def forward(self, *refs):
        pass

    def finish(self, src, dst, send, recv):
        x, y, c, me = _my_place()
        for k in range(self.n):
            _wait_slabs(dst[k], N_CHIPS, send.at[k], recv.at[k], (x, y, c))


class _ChipJob:
    n_sems = 3

    def __init__(self, items):
        self.sources = list(items)
        self.dests = [_sds(it.shape, it.dtype) for it in items]
        self.n = len(items)

    def start(self, src, dst, send, recv, loc):
        x, y, c, me = _my_place()
        mine = 2 * x + y
        for k in range(self.n):
            here = dst[k].at[pl.ds(mine, 1)]
            pltpu.make_async_copy(src[k].at[pl.ds(mine, 1)], here, loc.at[k]).start()
            for (px, py), q in _other_chips(x, y):
                _remote(src[k].at[pl.ds(q, 1)], here, send.at[k], recv.at[k], (px, py, c)).start()

    def relay(self, *refs):
        pass

    def forward(self, *refs):
        pass

    def finish(self, src, dst, send, recv, loc):
        x, y, c, me = _my_place()
        mine = 2 * x + y
        for k in range(self.n):
            pltpu.make_async_copy(src[k].at[pl.ds(mine, 1)], dst[k].at[pl.ds(mine, 1)], loc.at[k]).wait()
            _wait_slabs(dst[k], N_CHIPS - 1, send.at[k], recv.at[k], (x, y, c))


class _Jobs:
    def __init__(self, jobs):
        self.jobs = list(jobs)
        self.sources = [s for job in self.jobs for s in job.sources]
        self.dests = [d for job in self.jobs for d in job.dests]
        self.n = len(self.sources)

    def _stage(self, stage, src, dst, sems):
        k0 = s0 = 0
        for job in self.jobs:
            getattr(job, stage)(src[k0:k0 + job.n], dst[k0:k0 + job.n], *sems[s0:s0 + job.n_sems])
            k0 += job.n
            s0 += job.n_sems

    def start(self, src, dst, *sems):
        self._stage('start', src, dst, sems)

    def relay(self, src, dst, *sems):
        self._stage('relay', src, dst, sems)

    def forward(self, src, dst, *sems):
        self._stage('forward', src, dst, sems)

    def finish(self, src, dst, *sems):
        self._stage('finish', src, dst, sems)

    def split(self, got):
        out, k0 = [], 0
        for job in self.jobs:
            out.append(got[k0:k0 + job.n])
            k0 += job.n
        return out


def _job_sems(job):
    if isinstance(job, _Jobs):
        return [s for part in job.jobs for s in _job_sems(part)]
    return [pltpu.SemaphoreType.DMA((job.n,))] * job.n_sems


def _call(body, comm, *, name, grid, in_specs, out_specs, out_shape, operands, sem, scratch_shapes=(), aliases=None):
    single = not isinstance(out_shape, (list, tuple))
    out_shape = [out_shape] if single else list(out_shape)
    out_specs = [out_specs] if single else list(out_specs)
    if comm is None:
        res = pl.pallas_call(body, name=name, grid=grid, in_specs=list(in_specs), out_specs=out_specs,
                             out_shape=out_shape, scratch_shapes=list(scratch_shapes),
                             input_output_aliases=aliases or {}, compiler_params=_params(sem))(*operands)
        return res[0] if single else res
    n_in, n_out, n_scr, n_c = len(in_specs), len(out_shape), len(scratch_shapes), comm.n
    any_spec = pl.BlockSpec(memory_space=pl.ANY)
    steps = math.prod(grid)
    early = min(steps - 1, (steps * RELAY_AT_TENTHS) // 10)
    mid = min(steps - 1, (steps * FORWARD_AT_TENTHS) // 10)

    def hosted(*refs):
        ins, c_src = refs[:n_in], refs[n_in:n_in + n_c]
        outs = refs[n_in + n_c:n_in + n_c + n_out]
        c_dst = refs[n_in + n_c + n_out:n_in + 2 * n_c + n_out]
        scr = refs[n_in + 2 * n_c + n_out:n_in + 2 * n_c + n_out + n_scr]
        sems = refs[n_in + 2 * n_c + n_out + n_scr:]
        step = pl.program_id(0)
        for d in range(1, len(grid)):
            step = step * grid[d] + pl.program_id(d)

        @pl.when(step == 0)
        def _():
            comm.start(c_src, c_dst, *sems)

        @pl.when(step == early)
        def _():
            comm.relay(c_src, c_dst, *sems)

        @pl.when(step == mid)
        def _():
            comm.forward(c_src, c_dst, *sems)

        body(*ins, *outs, *scr)

        @pl.when(step == steps - 1)
        def _():
            comm.finish(c_src, c_dst, *sems)

    res = pl.pallas_call(hosted, name=name, grid=grid, in_specs=list(in_specs) + [any_spec] * n_c,
                         out_specs=out_specs + [any_spec] * n_c, out_shape=out_shape + comm.dests,
                         scratch_shapes=list(scratch_shapes) + _job_sems(comm), input_output_aliases=aliases or {},
                         compiler_params=_params(("arbitrary",) * len(grid)))(*operands, *comm.sources)
    main = res[:n_out]
    return (main[0] if single else main), res[n_out:]


def comm_only(name, jobs):
    any_spec = pl.BlockSpec(memory_space=pl.ANY)
    n_all = sum(job.n for job in jobs)

    def body(*refs):
        srcs, dsts, sems = refs[:n_all], refs[n_all:2 * n_all], refs[2 * n_all:]
        parts, k0, s0 = [], 0, 0
        for job in jobs:
            parts.append((job, srcs[k0:k0 + job.n], dsts[k0:k0 + job.n], sems[s0:s0 + job.n_sems]))
            k0 += job.n
            s0 += job.n_sems
        for stage in ('start', 'relay', 'forward', 'finish'):
            for job, src, dst, sem in parts:
                getattr(job, stage)(src, dst, *sem)

    res = pl.pallas_call(body, name=name, in_specs=[any_spec] * n_all, out_specs=[any_spec] * n_all,
                         out_shape=[d for job in jobs for d in job.dests],
                         scratch_shapes=[s for job in jobs for s in _job_sems(job)],
                         )(*[s for job in jobs for s in job.sources])
    out, k0 = [], 0
    for job in jobs:
        out.append(res[k0:k0 + job.n])
        k0 += job.n
    return out


def _dot(a, b, dims):
    if a.dtype != BF16:
        a = a.astype(BF16)
    if b.dtype != BF16:
        b = b.astype(BF16)
    return lax.dot_general(a, b, (dims, ((), ())), preferred_element_type=F32)


def _sigmoid(v):
    return 1.0 / (1.0 + jnp.exp(-v))


def _sq_relu(u):
    return jnp.square(jnp.maximum(u.astype(F32), 0.0)).astype(BF16)


def _ln_stats(z):
    mu = jnp.mean(z, axis=-1, keepdims=True)
    zc = z - mu
    var = jnp.mean(zc * zc, axis=-1, keepdims=True)
    return zc * lax.rsqrt(var + LN_EPS)


def _ln_fwd(z, g, b):
    return _ln_stats(z) * g + b


def _ln_bwd(dy, z, g):
    xhat = _ln_stats(z)
    mu = jnp.mean(z, axis=-1, keepdims=True)
    zc = z - mu
    rstd = lax.rsqrt(jnp.mean(zc * zc, axis=-1, keepdims=True) + LN_EPS)
    dxh = dy * g
    m1 = jnp.mean(dxh, axis=-1, keepdims=True)
    m2 = jnp.mean(dxh * xhat, axis=-1, keepdims=True)
    dz = rstd * (dxh - m1 - xhat * m2)
    return dz, jnp.sum(dy * xhat, axis=0, keepdims=True), jnp.sum(dy, axis=0, keepdims=True)


def _extra_spec(shape, kind, tm, tn, ij):
    if kind == 'tile':
        return pl.BlockSpec((tm, tn), lambda *g: (ij(g)[0], ij(g)[1]))
    if kind in ('row', 'rowacc'):
        return pl.BlockSpec((1, tn), lambda *g: (0, ij(g)[1]))
    if kind == 'tab':
        return pl.BlockSpec((tm, LANES), lambda *g: (ij(g)[0], 0))
    if kind == 'rows':
        return pl.BlockSpec((tm, shape[1]), lambda *g: (ij(g)[0], 0))
    raise ValueError(kind)


def mm_nn(name, pairs, extras, outs, epi, *, tm, tn, tk=None, comm=None, a_pro=None):
    M = pairs[0][0].shape[0]
    N = pairs[0][1].shape[0] * pairs[0][1].shape[2]
    n_pairs = len(pairs)
    K0 = pairs[0][0].shape[1]
    tk = K0 if tk is None else tk
    nk = K0 // tk
    assert nk == 1 or n_pairs == 1
    assert M % tm == 0 and N % tn == 0 and K0 % tk == 0
    has_rowacc = any(kind == 'rowacc' for _, kind in outs)
    assert not has_rowacc or (N == tn and nk == 1)
    in_specs, operands, slabs = [], [], []
    for a, b, off in pairs:
        K = a.shape[1]
        ktile = K if n_pairs > 1 else tk
        n = b.shape[2]
        assert b.shape[1] == K
        in_specs.append(pl.BlockSpec((tm, ktile), lambda i, j, k: (i, k)))
        if tn <= n:
            assert n % tn == 0
            r = n // tn
            in_specs.append(pl.BlockSpec((None, ktile, tn),
                                         lambda i, j, k, r=r, off=off: ((j + off) // r, k, (j + off) % r)))
            slabs.append(0)
        else:
            assert tn % n == 0
            in_specs.append(pl.BlockSpec((tn // n, ktile, n), lambda i, j, k, off=off: (j + off, k, 0)))
            slabs.append(tn // n)
        operands += [a, b]
    ij = lambda g: (g[0], g[1])
    for arr, kind in extras:
        in_specs.append(_extra_spec(arr.shape, kind, tm, tn, ij))
        operands.append(arr)
    out_specs = [_extra_spec(o.shape, kind, tm, tn, ij) for o, kind in outs]
    n_ex, n_out = len(extras), len(outs)

    def pair_dot(ab, q):
        a = ab[2 * q][...]
        if a_pro is not None:
            a = a_pro(a)
        if not slabs[q]:
            return _dot(a, ab[2 * q + 1][...], ((1,), (0,)))
        return jnp.concatenate([_dot(a, ab[2 * q + 1][s], ((1,), (0,))) for s in range(slabs[q])], axis=1)

    def body(*refs):
        ab = refs[:2 * n_pairs]
        ex = refs[2 * n_pairs:2 * n_pairs + n_ex]
        out = refs[2 * n_pairs + n_ex:2 * n_pairs + n_ex + n_out]
        i = pl.program_id(0)
        if nk == 1:
            accs = [pair_dot(ab, q) for q in range(n_pairs)]
            epi(accs, ex, out, i)
        else:
            acc_ref = refs[-1]
            k = pl.program_id(2)

            @pl.when(k == 0)
            def _():
                acc_ref[...] = jnp.zeros_like(acc_ref)

            acc_ref[...] += pair_dot(ab, 0)

            @pl.when(k == nk - 1)
            def _():
                epi([acc_ref[...]], ex, out, i)

    scratch = [pltpu.VMEM((tm, tn), F32)] if nk > 1 else []
    sem = ("arbitrary",) * 3 if has_rowacc else ("parallel", "parallel", "arbitrary")
    return _call(body, comm, name=name, grid=(M // tm, N // tn, nk), in_specs=in_specs, out_specs=out_specs,
                 out_shape=[o for o, _ in outs], scratch_shapes=scratch, sem=sem, operands=operands)


def mm_nt(name, a, b, extras, outs, epi, *, tm, tko, tc, comm=None, a_pro=None, dws=()):
    M, N = a.shape
    J, K, n = b.shape
    assert J * n == N and M % tm == 0 and K % tko == 0 and N % tc == 0
    nc = N // tc
    assert a_pro is None or (nc == 1 and tc <= n and K == tko)
    assert not dws or (nc == 1 and K == tko and tc <= n)
    n_dw, nt = len(dws), M // tm
    has_rowacc = any(kind == 'rowacc' for _, kind in outs)
    assert not has_rowacc or K == tko
    if tc <= n:
        assert n % tc == 0
        r = n // tc
        slabs = 0
        b_spec = pl.BlockSpec((None, tko, tc), lambda i, j, c: (c // r, j, c % r))
    else:
        assert tc % n == 0
        slabs = tc // n
        b_spec = pl.BlockSpec((slabs, tko, n), lambda i, j, c: (c, j, 0))
    n_ab = 2 if a_pro is None else 1
    in_specs = ([pl.BlockSpec((tm, tc), lambda i, j, c: (i, c))] if a_pro is None else []) + [b_spec]

    def nt_dot(a_ref, b_ref):
        if not slabs:
            return _dot(a_ref[...], b_ref[...], ((1,), (1,)))
        acc = _dot(a_ref[:, 0:n], b_ref[0], ((1,), (1,)))
        for s in range(1, slabs):
            acc = acc + _dot(a_ref[:, s * n:(s + 1) * n], b_ref[s], ((1,), (1,)))
        return acc

    ij = lambda g: (g[0], g[1])
    operands = [a, b] if a_pro is None else [b]
    for arr, kind in extras:
        in_specs.append(_extra_spec(arr.shape, kind, tm, tko, ij))
        operands.append(arr)
    out_specs = [_extra_spec(o.shape, kind, tm, tko, ij) for o, kind in outs]
    n_ex, n_out = len(extras), len(outs)
    out_shape = [o for o, _ in outs]
    for lhs, _ in dws:
        in_specs.append(pl.BlockSpec((tm, lhs.shape[1]), lambda i, j, c: (i, 0)))
        operands.append(lhs)
        out_specs.append(pl.BlockSpec((None, lhs.shape[1], N), lambda i, j, c: (0, 0, 0)))
        out_shape.append(_sds((1, lhs.shape[1], N), BF16))

    def weight_grads(refs, a_tile, i):
        lhs_refs = refs[n_ab + n_ex:n_ab + n_ex + n_dw]
        dw_refs = refs[n_ab + n_ex + n_dw + n_out:n_ab + n_ex + n_dw + n_out + n_dw]
        acc_refs = refs[len(refs) - n_dw:]
        for (_, rhs_fn), lhs_ref, dw_ref, acc_ref in zip(dws, lhs_refs, dw_refs, acc_refs):
            part = _dot(lhs_ref[...], a_tile if rhs_fn is None else rhs_fn(), ((0,), (0,)))
            _init_or_add(acc_ref, i, part)

            @pl.when(i == nt - 1)
            def _(dw_ref=dw_ref, acc_ref=acc_ref):
                dw_ref[...] = acc_ref[...].astype(dw_ref.dtype)

    def body(*refs):
        ex = refs[n_ab:n_ab + n_ex]
        out = refs[n_ab + n_ex + n_dw:n_ab + n_ex + n_dw + n_out]
        i = pl.program_id(0)
        if a_pro is not None:
            a_tile = a_pro(ex)
            epi(_dot(a_tile, refs[0][...], ((1,), (1,))), ex, out, i)
            weight_grads(refs, a_tile, i)
            return
        a_ref, b_ref = refs[:2]
        if nc == 1:
            epi(nt_dot(a_ref, b_ref), ex, out, i)
            if dws:
                weight_grads(refs, a_ref[...], i)
        else:
            acc_ref = refs[-1]
            c = pl.program_id(2)

            @pl.when(c == 0)
            def _():
                acc_ref[...] = jnp.zeros_like(acc_ref)

            acc_ref[...] += nt_dot(a_ref, b_ref)

            @pl.when(c == nc - 1)
            def _():
                epi(acc_ref[...], ex, out, i)

    scratch = ([pltpu.VMEM((tm, tko), F32)] if nc > 1 else []) + [pltpu.VMEM((lhs.shape[1], N), F32) for lhs, _ in dws]
    sem = ("arbitrary",) * 3 if (has_rowacc or dws) else ("parallel", "parallel", "arbitrary")
    return _call(body, comm, name=name, grid=(M // tm, K // tko, nc), in_specs=in_specs, out_specs=out_specs,
                 out_shape=out_shape, scratch_shapes=scratch, sem=sem, operands=operands)


def mm_tn(name, a, d, n, out_dtype, *, tm, tk, tn, comm=None):
    M, K = a.shape
    N = d.shape[1]
    assert d.shape[0] == M and N % n == 0 and N % tn == 0 and K % tk == 0 and M % tm == 0
    nm = M // tm
    if tn <= n:
        assert n % tn == 0
        r = n // tn
        slabs = 0
        o_spec = pl.BlockSpec((None, tk, tn), lambda kk, j, m: (j // r, kk, j % r))
    else:
        assert tn % n == 0
        slabs = tn // n
        o_spec = pl.BlockSpec((slabs, tk, n), lambda kk, j, m: (j, kk, 0))

    def write(o_ref, acc):
        if not slabs:
            o_ref[...] = acc.astype(o_ref.dtype)
        else:
            for s in range(slabs):
                o_ref[s] = acc[:, s * n:(s + 1) * n].astype(o_ref.dtype)

    def body(a_ref, d_ref, o_ref, *scratch):
        if nm == 1:
            write(o_ref, _dot(a_ref[...], d_ref[...], ((0,), (0,))))
            return
        acc_ref, = scratch
        m = pl.program_id(2)

        @pl.when(m == 0)
        def _():
            acc_ref[...] = jnp.zeros_like(acc_ref)

        acc_ref[...] += _dot(a_ref[...], d_ref[...], ((0,), (0,)))

        @pl.when(m == nm - 1)
        def _():
            write(o_ref, acc_ref[...])

    return _call(
        body, comm, name=name, grid=(K // tk, N // tn, nm),
        in_specs=[pl.BlockSpec((tm, tk), lambda kk, j, m: (m, kk)), pl.BlockSpec((tm, tn), lambda kk, j, m: (m, j))],
        out_specs=o_spec, out_shape=_sds((N // n, K, n), out_dtype),
        scratch_shapes=[pltpu.VMEM((tk, tn), F32)] if nm > 1 else [],
        sem=("parallel", "parallel", "arbitrary"), operands=[a, d])


def mlp_du_dw(name, dzb, w_down, u, xin, n, *, tm, tf, comm=None):
    T, D = dzb.shape
    F = u.shape[1]
    assert T % tm == 0 and F % tf == 0 and tf % n == 0
    slabs, nt = tf // n, T // tm

    def body(dz_ref, w_ref, u_ref, x_ref, du_ref, dwu_ref, dwd_ref, up_acc, down_acc):
        i = pl.program_id(1)
        dz = dz_ref[...]
        da = _dot(dz, w_ref[...], ((1,), (1,)))
        relu = jnp.maximum(u_ref[...].astype(F32), 0.0)
        du = (da * (2.0 * relu)).astype(BF16)
        du_ref[...] = du
        _init_or_add(up_acc, i, _dot(x_ref[...], du, ((0,), (0,))))
        _init_or_add(down_acc, i, _dot(jnp.square(relu).astype(BF16), dz, ((0,), (0,))))

        @pl.when(i == nt - 1)
        def _():
            for s in range(slabs):
                dwu_ref[s] = up_acc[:, s * n:(s + 1) * n].astype(dwu_ref.dtype)
            dwd_ref[...] = down_acc[...].astype(dwd_ref.dtype)

    hidden = pl.BlockSpec((tm, tf), lambda j, i: (i, j))
    tokens = pl.BlockSpec((tm, D), lambda j, i: (i, 0))
    return _call(
        body, comm, name=name, grid=(F // tf, nt),
        in_specs=[tokens, pl.BlockSpec((None, tf, D), lambda j, i: (0, j, 0)), hidden, tokens],
        out_specs=[hidden, pl.BlockSpec((slabs, D, n), lambda j, i: (j, 0, 0)),
                   pl.BlockSpec((None, tf, D), lambda j, i: (0, j, 0))],
        out_shape=[_sds((T, F), BF16), _sds((F // n, D, n), BF16), _sds((1, F, D), BF16)],
        scratch_shapes=[pltpu.VMEM((D, tf), F32), pltpu.VMEM((tf, D), F32)], sem=("arbitrary", "arbitrary"),
        operands=[dzb, w_down, u, xin])


def _init_or_add(ref, i, value):
    @pl.when(i == 0)
    def _():
        ref[...] = value

    @pl.when(i > 0)
    def _():
        ref[...] += value


def _rope_tables(T):
    pos = jnp.arange(T, dtype=F32)
    inv_freq = ROPE_THETA ** (-jnp.arange(0, ROPE_DIM, 2, dtype=F32) / ROPE_DIM)
    ang = pos[:, None] * inv_freq[None, :]
    cos, sin = jnp.cos(ang), jnp.sin(ang)
    ones = jnp.ones((T, HEAD_DIM - ROPE_DIM), F32)
    zeros = jnp.zeros((T, HEAD_DIM - ROPE_DIM), F32)
    zh = jnp.zeros((T, ROPE_HALF), F32)
    c_head = jnp.concatenate([cos, cos, ones], axis=1)
    s_up = jnp.concatenate([-sin, zh, zeros], axis=1)
    s_dn = jnp.concatenate([zh, sin, zeros], axis=1)
    rep = LANES // HEAD_DIM
    return tuple(jnp.tile(t, (1, rep)) for t in (c_head, s_up, s_dn))


def _rope_chunk(t, c, s_up, s_dn):
    return t * c + pltpu.roll(t, LANES - ROPE_HALF, 1) * s_up + pltpu.roll(t, ROPE_HALF, 1) * s_dn


def _rope_chunk_bwd(d, c, s_up, s_dn):
    return d * c + pltpu.roll(d * s_up, ROPE_HALF, 1) + pltpu.roll(d * s_dn, LANES - ROPE_HALF, 1)


def _rope_wide(t, c, s_up, s_dn, fn):
    chunks = [fn(t[:, q * LANES:(q + 1) * LANES], c, s_up, s_dn) for q in range(t.shape[1] // LANES)]
    return chunks[0] if len(chunks) == 1 else jnp.concatenate(chunks, axis=1)


def _taps_by_residue(first):
    groups = {}
    for o in range(first, first + CONV_WIDTH):
        groups.setdefault(o % SUBLANES, []).append(o)
    return sorted(groups.items())


def _shifted_taps(win_ref, span_ref, r0, res, offs, rb, ls):
    if res == 0:
        return [functools.partial(lambda o: win_ref[pl.ds(r0 + o, rb), ls], o) for o in offs]
    n = rb + offs[-1] - res
    span_ref[0:n, :] = win_ref[pl.ds(r0 + res, n), ls]
    return [functools.partial(lambda o: span_ref[pl.ds(o - res, rb), :], o) for o in offs]


def _halo_before_spec(tm, D):
    return pl.BlockSpec((CONV_HALO, D), lambda i: (jnp.maximum(i * (tm // CONV_HALO) - 1, 0), 0))


def dwconv_fwd(g, w_dw, b_dw, ln_g, ln_b, *, tm, comm=None):
    T, D = g.shape
    nl = D // LANES

    rb = min(CONV_ROWS, tm)

    def body(g_ref, gh_ref, w_ref, b_ref, lg_ref, lb_ref, c_ref, s_ref, win_ref, span_ref):
        i = pl.program_id(0)
        win_ref[0:CONV_HALO, :] = jnp.where(i > 0, gh_ref[...], 0.0)
        win_ref[CONV_HALO:, :] = g_ref[...]

        def lane_chunk(q, carry):
            ls = pl.ds(pl.multiple_of(q * LANES, LANES), LANES)
            for r0 in range(0, tm, rb):
                acc = jnp.broadcast_to(b_ref[:, ls], (rb, LANES))
                for res, offs in _taps_by_residue(CONV_HALO - (CONV_WIDTH - 1)):
                    taps = _shifted_taps(win_ref, span_ref, r0, res, offs, rb, ls)
                    for o, tap in zip(offs, taps):
                        k = o - (CONV_HALO - (CONV_WIDTH - 1))
                        acc = acc + tap() * w_ref[k:k + 1, ls]
                c_ref[pl.ds(r0, rb), ls] = acc
            return carry

        lax.fori_loop(0, nl, lane_chunk, 0)
        n = _ln_fwd(c_ref[...], lg_ref[...], lb_ref[...])
        s_ref[...] = (n * _sigmoid(n)).astype(s_ref.dtype)

    row = pl.BlockSpec((1, D), lambda i: (0, 0))
    return _call(
        body, comm, name="dwconv_fwd", grid=(T // tm,),
        in_specs=[pl.BlockSpec((tm, D), lambda i: (i, 0)), _halo_before_spec(tm, D),
                  pl.BlockSpec((CONV_HALO, D), lambda i: (0, 0)), row, row, row],
        out_specs=[pl.BlockSpec((tm, D), lambda i: (i, 0)), pl.BlockSpec((tm, D), lambda i: (i, 0))],
        out_shape=[_sds((T, D), F32), _sds((T, D), BF16)],
        scratch_shapes=[pltpu.VMEM((tm + CONV_HALO, D), F32), pltpu.VMEM((rb + CONV_HALO, LANES), F32)],
        sem=("parallel",), operands=[g, g, w_dw, b_dw, ln_g, ln_b])


def dwconv_bwd(dc, g, ha, hg, w_dw, *, tm, comm=None):
    T, D = g.shape
    nl = D // LANES
    last = T // CONV_HALO - 1
    nt = T // tm

    rb = min(CONV_ROWS, tm)

    def body(dc_ref, dcn_ref, g_ref, gh_ref, ha_ref, hg_ref, w_ref, dh_ref, dw_ref, dbdw_ref, dbin_ref,
             win_ref, dwin_ref, dg_ref, dwp_ref, span_ref):
        i = pl.program_id(0)
        win_ref[0:CONV_HALO, :] = jnp.where(i > 0, gh_ref[...], 0.0)
        win_ref[CONV_HALO:, :] = g_ref[...]
        dwin_ref[0:tm, :] = dc_ref[...]
        dwin_ref[tm:, :] = jnp.where(i < nt - 1, dcn_ref[...], 0.0)

        @pl.when(i == 0)
        def _():
            dwp_ref[...] = jnp.zeros_like(dwp_ref)

        first = CONV_HALO - (CONV_WIDTH - 1)

        def lane_chunk(q, carry):
            ls = pl.ds(pl.multiple_of(q * LANES, LANES), LANES)
            for r0 in range(0, tm, rb):
                acc = jnp.zeros((rb, LANES), F32)
                for res, offs in _taps_by_residue(0):
                    taps = _shifted_taps(dwin_ref, span_ref, r0, res, offs, rb, ls)
                    for o, tap in zip(offs, taps):
                        k = CONV_WIDTH - 1 - o
                        acc = acc + tap() * w_ref[k:k + 1, ls]
                dg_ref[pl.ds(r0, rb), ls] = acc
                dcv = dwin_ref[pl.ds(r0, rb), ls]
                for res, offs in _taps_by_residue(first):
                    taps = _shifted_taps(win_ref, span_ref, r0, res, offs, rb, ls)
                    for o, tap in zip(offs, taps):
                        k = o - first
                        prod = dcv * tap()
                        dwp_ref[k, :, ls] += jnp.sum(prod.reshape(rb // SUBLANES, SUBLANES, LANES), axis=0)
            return carry

        lax.fori_loop(0, nl, lane_chunk, 0)

        @pl.when(i == nt - 1)
        def _():
            for k in range(CONV_WIDTH):
                dw_ref[k:k + 1, :] = jnp.sum(dwp_ref[k], axis=0, keepdims=True)
            dw_ref[CONV_WIDTH:, :] = jnp.zeros((CONV_HALO - CONV_WIDTH, D), F32)
        dg = dg_ref[...]
        ha = ha_ref[...].astype(F32)
        sg = _sigmoid(hg_ref[...].astype(F32))
        d_ha = dg * sg
        d_hg = dg * ha * sg * (1.0 - sg)
        dh_ref[:, 0:D] = d_ha.astype(dh_ref.dtype)
        dh_ref[:, D:] = d_hg.astype(dh_ref.dtype)
        _init_or_add(dbdw_ref, i, jnp.sum(dc_ref[...], axis=0, keepdims=True))
        _init_or_add(dbin_ref, i, jnp.concatenate([jnp.sum(d_ha, axis=0, keepdims=True),
                                                   jnp.sum(d_hg, axis=0, keepdims=True)], axis=1))

    tile = pl.BlockSpec((tm, D), lambda i: (i, 0))
    return _call(
        body, comm, name="dwconv_bwd", grid=(nt,),
        in_specs=[tile,
                  pl.BlockSpec((CONV_HALO, D), lambda i: (jnp.minimum((i + 1) * (tm // CONV_HALO), last), 0)),
                  tile, _halo_before_spec(tm, D),
                  tile, tile, pl.BlockSpec((CONV_HALO, D), lambda i: (0, 0))],
        out_specs=[pl.BlockSpec((tm, 2 * D), lambda i: (i, 0)), pl.BlockSpec((CONV_HALO, D), lambda i: (0, 0)),
                   pl.BlockSpec((1, D), lambda i: (0, 0)), pl.BlockSpec((1, 2 * D), lambda i: (0, 0))],
        out_shape=[_sds((T, 2 * D), BF16), _sds((CONV_HALO, D), F32), _sds((1, D), F32), _sds((1, 2 * D), F32)],
        scratch_shapes=[pltpu.VMEM((tm + CONV_HALO, D), F32), pltpu.VMEM((tm + CONV_HALO, D), F32),
                        pltpu.VMEM((tm, D), F32), pltpu.VMEM((CONV_WIDTH, SUBLANES, D), F32),
                        pltpu.VMEM((rb + CONV_HALO, LANES), F32)],
        sem=("arbitrary",), operands=[dc, dc, g, g, ha, hg, w_dw])


def _attn_specs(HD, W):
    B = ATT_BLOCK
    return [pl.BlockSpec((B, HD), lambda n: (n, 0)),
            pl.BlockSpec((B, 4 * W), lambda n: (n, 0)),
            pl.BlockSpec((B, 4 * W), lambda n: (jnp.maximum(n - 1, 0), 0))]


def _band_mask(n):
    r = lax.broadcasted_iota(jnp.int32, (ATT_BLOCK, 2 * ATT_BLOCK), 0)
    j = lax.broadcasted_iota(jnp.int32, (ATT_BLOCK, 2 * ATT_BLOCK), 1)
    return (j > r) & (j <= r + ATT_BLOCK) & ((n > 0) | (j >= ATT_BLOCK))


def _band(kvc_ref, kvp_ref, part, g, parity, W):
    lanes = slice((2 * part + parity) * W + g * LANES, (2 * part + parity) * W + (g + 1) * LANES)
    return jnp.concatenate([kvp_ref[:, lanes], kvc_ref[:, lanes]], axis=0)


def _half_mask(parity):
    lane = lax.broadcasted_iota(jnp.int32, (1, LANES), 1)
    return (lane < HEAD_DIM) if parity == 0 else (lane >= HEAD_DIM)


def widen_kv(k, v, out_ref, n_kv):
    W = n_kv * LANES
    low = _half_mask(0)
    for part, src in enumerate((k, v)):
        for cg in range(n_kv * HEAD_DIM // LANES):
            chunk = src[:, cg * LANES:(cg + 1) * LANES]
            swapped = pltpu.roll(chunk, HEAD_DIM, 1)
            for g, lo, hi in ((2 * cg, chunk, swapped), (2 * cg + 1, swapped, chunk)):
                base = 2 * part * W + g * LANES
                out_ref[:, base:base + LANES] = jnp.where(low, lo, 0.0).astype(out_ref.dtype)
                out_ref[:, base + W:base + W + LANES] = jnp.where(low, 0.0, hi).astype(out_ref.dtype)


def attn_fwd(q, kvx, sinks, n_heads, n_kv, comm=None):
    T, HD = q.shape
    W = n_kv * LANES
    B = ATT_BLOCK
    chunks_per_group = n_heads // n_kv // 2
    scale = 1.0 / math.sqrt(HEAD_DIM)

    def body(q_ref, kvc_ref, kvp_ref, sink_ref, o_ref, lse_ref):
        n = pl.program_id(0)
        mask = jnp.tile(_band_mask(n), (n_heads, 1))
        s = jnp.concatenate(
            [_dot(q_ref[:, (h // 2) * LANES:(h // 2 + 1) * LANES],
                  _band(kvc_ref, kvp_ref, 0, h // 2 // chunks_per_group, h % 2, W), ((1,), (1,)))
             for h in range(n_heads)], axis=0)
        sink = jnp.concatenate([jnp.broadcast_to(sink_ref[:, h:h + 1], (B, 1)) for h in range(n_heads)], axis=0)
        s = jnp.where(mask, s * scale, MASK_VALUE)
        m = jnp.maximum(jnp.max(s, axis=-1, keepdims=True), sink)
        e = jnp.exp(s - m)
        total = _dot(e, jnp.ones((2 * B, LANES), BF16), ((1,), (0,))) + jnp.exp(sink - m)
        lse = m + jnp.log(total[:, 0:1])
        inv = 1.0 / total
        probs = (e * jnp.concatenate([inv, inv], axis=1)).astype(BF16)
        for c in range(n_heads // 2):
            g = c // chunks_per_group
            out = (_dot(probs[2 * c * B:(2 * c + 1) * B], _band(kvc_ref, kvp_ref, 1, g, 0, W), ((1,), (0,)))
                   + _dot(probs[(2 * c + 1) * B:(2 * c + 2) * B], _band(kvc_ref, kvp_ref, 1, g, 1, W), ((1,), (0,))))
            o_ref[:, c * LANES:(c + 1) * LANES] = out.astype(o_ref.dtype)
        lse_ref[...] = jnp.concatenate([lse[h * B:(h + 1) * B] for h in range(n_heads)], axis=1)

    return _call(
        body, comm, name="attn_fwd", grid=(T // B,),
        in_specs=_attn_specs(HD, W) + [pl.BlockSpec((1, n_heads), lambda n: (0, 0))],
        out_specs=[pl.BlockSpec((B, HD), lambda n: (n, 0)), pl.BlockSpec((B, n_heads), lambda n: (n, 0))],
        out_shape=[_sds((T, HD), BF16), _sds((T, n_heads), F32)],
        sem=("parallel",), operands=[q, kvx, kvx, sinks])


def attn_bwd(q, kvx, do, lse, sinks, rope, n_heads, n_kv, comm=None):
    T, HD = q.shape
    KVD = n_kv * HEAD_DIM
    W = n_kv * LANES
    B = ATT_BLOCK
    chunks_per_group = n_heads // n_kv // 2
    scale = 1.0 / math.sqrt(HEAD_DIM)
    nb = T // B

    def body(q_ref, kvc_ref, kvp_ref, do_ref, lse_ref, sink_ref, c_ref, su_ref, sd_ref, dq_ref, dkv_ref, dsink_ref):
        n = pl.program_id(0)

        @pl.when(n == 0)
        def _():
            dkv_ref[...] = jnp.zeros_like(dkv_ref)
            dsink_ref[...] = jnp.zeros_like(dsink_ref)

        def chunk(ref, h):
            return ref[:, (h // 2) * LANES:(h // 2 + 1) * LANES]

        def band(part, h):
            return _band(kvc_ref, kvp_ref, part, h // 2 // chunks_per_group, h % 2, W)

        def stack(per_head):
            return jnp.concatenate([per_head(h) for h in range(n_heads)], axis=0)

        mask = jnp.tile(_band_mask(n), (n_heads, 1))
        s = stack(lambda h: _dot(chunk(q_ref, h), band(0, h), ((1,), (1,))))
        dp = stack(lambda h: _dot(chunk(do_ref, h), band(1, h), ((1,), (1,))))
        lse = stack(lambda h: lse_ref[:, h:h + 1])
        sink = stack(lambda h: jnp.broadcast_to(sink_ref[:, h:h + 1], (B, 1)))
        probs = jnp.exp(jnp.where(mask, s * scale, MASK_VALUE) - lse)
        delta = jnp.sum(probs * dp, axis=-1, keepdims=True)
        ds = (probs * (dp - delta) * scale).astype(BF16)
        probs = probs.astype(BF16)
        sink_term = jnp.exp(sink - lse) * delta
        dsk = [-jnp.sum(sink_term[h * B:(h + 1) * B], axis=0, keepdims=True) for h in range(n_heads)]

        dk_wide, dv_wide = [None] * n_kv, [None] * n_kv
        for c in range(n_heads // 2):
            g = c // chunks_per_group
            dq2 = None
            for h in (2 * c, 2 * c + 1):
                half = _half_mask(h % 2)
                q2, do2 = chunk(q_ref, h), chunk(do_ref, h)
                ds_h, p_h = ds[h * B:(h + 1) * B], probs[h * B:(h + 1) * B]
                part = _dot(ds_h, band(0, h), ((1,), (0,)))
                dq2 = part if dq2 is None else dq2 + part
                dk_h = _dot(ds_h, jnp.where(half, q2, jnp.zeros_like(q2)), ((0,), (0,)))
                dv_h = _dot(p_h, jnp.where(half, do2, jnp.zeros_like(do2)), ((0,), (0,)))
                dk_wide[g] = dk_h if dk_wide[g] is None else dk_wide[g] + dk_h
                dv_wide[g] = dv_h if dv_wide[g] is None else dv_wide[g] + dv_h
            dq_ref[:, c * LANES:(c + 1) * LANES] = _rope_chunk_bwd(
                dq2, c_ref[...], su_ref[...], sd_ref[...]).astype(dq_ref.dtype)

        def fold(wide):
            low = _half_mask(0)
            both = [w + pltpu.roll(w, HEAD_DIM, 1) for w in wide]
            return jnp.concatenate([jnp.where(low, both[2 * cg], both[2 * cg + 1]) for cg in range(n_kv // 2)], axis=1)

        dkv = jnp.concatenate([fold(dk_wide), fold(dv_wide)], axis=1)
        prev = pl.ds(pl.multiple_of(jnp.maximum(n - 1, 0) * B, B), B)
        cur = pl.ds(pl.multiple_of(n * B, B), B)
        dkv_ref[prev, :] += dkv[0:B, :]
        dkv_ref[cur, :] += dkv[B:, :]
        dsink_ref[...] += jnp.concatenate(dsk, axis=1)

    tab = pl.BlockSpec((B, LANES), lambda n: (n, 0))
    return _call(
        body, comm, name="attn_bwd", grid=(nb,),
        in_specs=_attn_specs(HD, W) + [pl.BlockSpec((B, HD), lambda n: (n, 0)),
                                       pl.BlockSpec((B, n_heads), lambda n: (n, 0)),
                                       pl.BlockSpec((1, n_heads), lambda n: (0, 0)), tab, tab, tab],
        out_specs=[pl.BlockSpec((B, HD), lambda n: (n, 0)), pl.BlockSpec((T, 2 * KVD), lambda n: (0, 0)),
                   pl.BlockSpec((1, n_heads), lambda n: (0, 0))],
        out_shape=[_sds((T, HD), BF16), _sds((T, 2 * KVD), F32), _sds((1, n_heads), F32)],
        sem=("arbitrary",), operands=[q, kvx, kvx, do, lse, sinks, *rope])


def chip_sums(name, pairs):
    n = len(pairs)

    def body(core_ref, *refs):
        del core_ref
        for k in range(n):
            refs[2 * n + k][...] = (refs[2 * k][...].astype(F32) + refs[2 * k + 1][...].astype(F32)
                                    ).astype(refs[2 * n + k].dtype)

    in_specs, out_specs, out_shape, operands = [], [], [], []
    for g, p_sib in pairs:
        _, a, b = g.shape
        in_specs += [pl.BlockSpec((None, None, a, b), lambda q, core: (q, core[0], 0, 0)),
                     pl.BlockSpec((None, a, b), lambda q, core: (q, 0, 0))]
        out_specs.append(pl.BlockSpec((None, a, b), lambda q, core: (q, 0, 0)))
        out_shape.append(_sds((N_CHIPS, a, b), g.dtype))
        operands += [g.reshape(N_CHIPS, 2, a, b), p_sib]
    my_core = lax.axis_index("c").astype(jnp.int32).reshape(1)
    return pl.pallas_call(
        body, name=name, out_shape=out_shape,
        grid_spec=pltpu.PrefetchScalarGridSpec(num_scalar_prefetch=1, grid=(N_CHIPS,), in_specs=in_specs,
                                               out_specs=out_specs),
        compiler_params=_params(("arbitrary",)))(my_core, *operands)


def adamw(name, recvs, w, m, v, *, ta):
    L, a, b = w.shape
    n_terms = recvs[0].shape[0]
    assert a % ta == 0 and len(recvs) == L
    c1 = 1.0 - ADAM_B1 ** ADAM_STEP
    c2 = 1.0 - ADAM_B2 ** ADAM_STEP

    def body(*refs):
        r_refs = refs[:L]
        w_ref, m_ref, v_ref, g_ref, d_ref, nm_ref, nv_ref = refs[L:]
        layer = pl.program_id(0)
        for l in range(L):
            @pl.when(layer == l)
            def _(r_ref=r_refs[l]):
                g = r_ref[0].astype(F32)
                for s in range(1, n_terms):
                    g = g + r_ref[s].astype(F32)
                nm = ADAM_B1 * m_ref[...] + (1.0 - ADAM_B1) * g
                nv = ADAM_B2 * v_ref[...] + (1.0 - ADAM_B2) * jnp.square(g)
                m_hat = nm / c1
                v_hat = nv / c2
                g_ref[...] = g
                d_ref[...] = -ADAM_LR * (m_hat / (jnp.sqrt(v_hat) + ADAM_EPS) + ADAM_WD * w_ref[...])
                nm_ref[...] = nm
                nv_ref[...] = nv

    blk = pl.BlockSpec((None, ta, b), lambda l, i: (l, i, 0))
    out = _sds((L, a, b), F32)
    r_specs = [pl.BlockSpec((n_terms, ta, b), lambda l, i, ll=ll: (0, jnp.where(l == ll, i, 0), 0)) for ll in range(L)]
    return pl.pallas_call(
        body, name=name, grid=(L, a // ta), in_specs=r_specs + [blk, blk, blk],
        out_specs=[blk, blk, blk, blk], out_shape=[out, out, out, out],
        compiler_params=_params(("arbitrary", "arbitrary")))(*recvs, w, m, v)


def _pack_rows(parts):
    out = []
    for arr, rows in parts:
        arr = arr.reshape(-1, LANES).astype(F32)
        out.append(jnp.pad(arr, ((0, rows - arr.shape[0]), (0, 0))))
    return jnp.concatenate(out, axis=0)


def _small_rows(a, prefix):
    return _pack_rows([(a[prefix + name], rows) for name, rows in SMALL_SHARDED])


def _unpack_small(packed, a):
    out, r0 = {}, 0
    for name, rows in SMALL_SHARDED:
        shape = a[name].shape
        used = math.prod(shape) // LANES
        out[name] = packed[r0:r0 + used].reshape(shape)
        r0 += rows
    return out


def _rep_rows(a, prefix):
    parts = []
    for name in REPLICATED:
        arr = a[prefix + name]
        if arr.size % LANES:
            arr = jnp.pad(arr.reshape(1, -1), ((0, 0), (0, LANES - arr.size % LANES)))
        rows = -(-arr.size // LANES)
        parts.append((arr, -(-rows // SUBLANES) * SUBLANES))
    return _pack_rows(parts)


def _unpack_rep(packed, a):
    out, r0 = {}, 0
    for name in REPLICATED:
        shape = a[name].shape
        size = math.prod(shape)
        rows = -(-size // LANES)
        out[name] = packed[r0:r0 + rows].reshape(-1)[:size].reshape(shape)
        r0 += -(-rows // SUBLANES) * SUBLANES
    return out


def _step(a):
    x = a['x'][0]
    T, D = x.shape
    tgt = a['loss_target'][0]
    p_in = [a['p'][i, 0] for i in range(DEPTH)]
    PLE = p_in[0].shape[1]
    n_heads = a['attn_sinks'].shape[1]
    HD = n_heads * HEAD_DIM
    KVD = a['kv_w_k'].shape[1]
    n_kv = KVD // HEAD_DIM
    F = a['mlp_w_down'].shape[1] * N_DEV
    tm = min(TOKEN_TILE, T)
    tm2 = min(2 * TOKEN_TILE, T)
    tmc = min(TOKEN_TILE, T)
    tw = 512
    alpha = DEEPNORM_ALPHA
    xb, p_b = x, p_in

    def shard3(w):
        return w.reshape((1,) + w.shape) if w.ndim == 2 else w

    def gather(*specs):
        return _GatherJob([shard3(a[nm])[li].astype(BF16) for nm, li in specs])

    (W_in, small_full), = comm_only("gather_first",
                                    [_GatherJob([a['conv_w_in'][0].astype(BF16), _small_rows(a, '')])])
    r0, small = 0, {}
    for name, rows in SMALL_SHARDED:
        small[name] = small_full[:, r0:r0 + rows]
        r0 += rows
    b_in = small['conv_b_in'][:, 0:2 * D // N_DEV // LANES].reshape(1, 2 * D)
    w_dw = jnp.transpose(small['conv_w_dw'], (1, 0, 2)).reshape(CONV_HALO, D)
    b_dw, cln_g, cln_b, b_out = (small[nm][:, 0].reshape(1, D) for nm in
                                 ('conv_b_dw', 'conv_ln_g', 'conv_ln_b', 'conv_b_out'))
    W_up, W_down, W_proj, W_gate = {}, {}, {}, {}
    mix_g, mix_b, mlp_g, mlp_b = a['mix_ln_g'], a['mix_ln_b'], a['mlp_ln_g'], a['mlp_ln_b']
    rope = _rope_tables(T)

    def set_ple_weights(li, g_proj, g_gate):
        W_proj[li] = jnp.transpose(g_proj, (1, 0, 2)).reshape(1, PLE, D)
        W_gate[li] = g_gate.reshape(1, D, D)

    def row(v, i):
        return v[i:i + 1]

    def res_ln_epi(coef):
        def epi(accs, ex, out, i):
            acc = accs[0] if isinstance(accs, list) else accs
            n_ex = len(ex)
            res_ref, g_ref, b_ref = ex[n_ex - 3], ex[n_ex - 2], ex[n_ex - 1]
            z = coef * res_ref[...] + acc
            if n_ex == 4:
                z = z + ex[0][...]
            out[0][...] = z
            xo = _ln_fwd(z, g_ref[...], b_ref[...])
            out[1][...] = xo
            out[2][...] = xo.astype(BF16)
        return epi

    res_ln_outs = [(_sds((T, D), F32), 'tile'), (_sds((T, D), F32), 'tile'), (_sds((T, D), BF16), 'tile')]

    def mlp_fwd(li, xin, xin_b, down_comm=None):
        def up_epi(accs, ex, out, i):
            out[0][...] = accs[0].astype(BF16)
        res = mm_nn(f"mlp_up_{li}", [(xin_b, W_up[li], 0)], [], [(_sds((T, F), BF16), 'tile')], up_epi, tm=tm2,
                    tn=min(1024, F), comm=None if li in W_down else gather(('mlp_w_down', li)))
        if li in W_down:
            u, = res
        else:
            (u,), got_up = res
            W_down[li] = got_up[0].reshape(1, F, D)
        res = mm_nn(f"mlp_down_{li}", [(u, W_down[li], 0)],
                    [(xin, 'tile'), (row(mlp_g, li), 'row'), (row(mlp_b, li), 'row')],
                    res_ln_outs, res_ln_epi(alpha), tm=tm, tn=D, tk=F, comm=down_comm, a_pro=_sq_relu)
        (z, xo, xo_b), got_down = res if down_comm is not None else (res, ())
        return u, z, xo, xo_b, got_down

    def ple_fwd(li, xin, xin_b, with_loss, comm=None):
        def epi(accs, ex, out, i):
            pp, gg = accs
            xo = ex[0][...] + pp * _sigmoid(gg)
            out[1][...] = pp.astype(BF16)
            out[2][...] = gg.astype(BF16)
            if with_loss:
                err = xo - ex[1][...]
                out[0][...] = err * (1.0 / D)
                _init_or_add(out[3], i, jnp.sum(err * err, axis=0, keepdims=True) * (0.5 / D))
            else:
                out[0][...] = xo
                out[3][...] = xo.astype(BF16)
        extras = [(xin, 'tile')] + ([(tgt, 'tile')] if with_loss else [])
        outs = [(_sds((T, D), F32), 'tile'), (_sds((T, D), BF16), 'tile'), (_sds((T, D), BF16), 'tile')]
        outs.append((_sds((1, D), F32), 'rowacc') if with_loss else (_sds((T, D), BF16), 'tile'))
        return mm_nn(f"ple_{li}", [(p_b[li], W_proj[li], 0), (xin_b, W_gate[li], 0)], extras, outs, epi, tm=tm, tn=D,
                     comm=comm)

    assert D // N_DEV == LANES
    (g0, ha0, hg0), got = _glu(xb, W_in, b_in, T, D, tm,
                               gather(('conv_w_out', 0), ('ple_w_proj', 0), ('ple_w_gate', 0)))
    W_out = got[0].reshape(1, D, D)
    set_ple_weights(0, got[1], got[2])
    (c0, s0), (W_up[0],) = dwconv_fwd(g0, w_dw, b_dw, cln_g, cln_b, tm=tmc, comm=gather(('mlp_w_up', 0)))
    z1, x1, x1b = mm_nn("conv_out", [(s0, W_out, 0)],
                        [(b_out, 'row'), (x, 'tile'), (row(mix_g, 0), 'row'), (row(mix_b, 0), 'row')],
                        res_ln_outs, res_ln_epi(alpha), tm=tm, tn=D)
    u0, z2, x2, x2b, got = mlp_fwd(
        0, x1, x1b, down_comm=gather(('attn_w_q', 0), ('kv_w_k', 0), ('kv_w_v', 0), ('attn_w_o', 0)))
    W_qkv = jnp.concatenate([got[0].reshape(D, HD), got[1].reshape(D, KVD), got[2].reshape(D, KVD)], axis=1)[None]
    W_o = got[3].reshape(1, HD, D)
    x3, pp0, gg0, x3b = ple_fwd(0, x2, x2b, False)

    def qkv_epi(accs, ex, out, i):
        t = accs[0]
        c, su, sd = ex[0][...], ex[1][...], ex[2][...]
        out[0][...] = _rope_wide(t[:, 0:HD], c, su, sd, _rope_chunk).astype(BF16)
        widen_kv(_rope_wide(t[:, HD:HD + KVD], c, su, sd, _rope_chunk), t[:, HD + KVD:], out[1], n_kv)
    NQ = HD + 2 * KVD
    q1, kvx1 = mm_nn("qkv_rope", [(x3b, W_qkv, 0)], [(t, 'tab') for t in rope],
                     [(_sds((T, HD), BF16), 'rows'), (_sds((T, 4 * n_kv * LANES), BF16), 'rows')], qkv_epi,
                     tm=tm, tn=NQ)
    (o1, lse1), got = attn_fwd(q1, kvx1, a['attn_sinks'], n_heads, n_kv,
                               comm=gather(('mlp_w_up', 1), ('mlp_w_down', 1), ('ple_w_proj', 1), ('ple_w_gate', 1)))
    W_up[1], W_down[1] = got[0], got[1].reshape(1, F, D)
    set_ple_weights(1, got[2], got[3])
    z3, x4, x4b = mm_nn("attn_out", [(o1, W_o, 0)], [(x3, 'tile'), (row(mix_g, 1), 'row'), (row(mix_b, 1), 'row')],
                        res_ln_outs, res_ln_epi(alpha), tm=tm, tn=D)
    u1, z4, x5, x5b, _ = mlp_fwd(1, x4, x4b)
    dy, pp1, gg1, loss_row = ple_fwd(1, x5, x5b, True)
    loss_local = jnp.sum(loss_row)

    grads = {}

    def ln_bwd_epi(coef, with_colsum):
        def epi(acc, ex, out, i):
            d_x = acc + coef * ex[0][...]
            dz, dg, db = _ln_bwd(d_x, ex[1][...], ex[2][...])
            out[0][...] = dz
            out[1][...] = dz.astype(BF16)
            _init_or_add(out[2], i, dg)
            _init_or_add(out[3], i, db)
            if with_colsum:
                _init_or_add(out[4], i, jnp.sum(dz, axis=0, keepdims=True))
        return epi

    def ln_bwd_outs(with_colsum):
        outs = [(_sds((T, D), F32), 'tile'), (_sds((T, D), BF16), 'tile'), (_sds((1, D), F32), 'rowacc'),
                (_sds((1, D), F32), 'rowacc')]
        return outs + ([(_sds((1, D), F32), 'rowacc')] if with_colsum else [])

    def ple_bwd(li, d_out, xin, pp, gg, z_mlp, pair_specs=None, chip_keys=None):
        side = {}

        def gate_grads(ex):
            d = ex[0][...]
            sg = _sigmoid(ex[4][...].astype(F32))
            side['d_pp'] = (d * sg).astype(BF16)
            side['d_gg'] = (d * ex[3][...].astype(F32) * sg * (1.0 - sg)).astype(BF16)
            return side['d_gg']

        jobs = None
        if pair_specs:
            jobs = _Jobs([pair_stage(*pair_specs)] + ([chip_stage(*chip_keys)] if chip_keys else []))
        res = mm_nt(f"ple_dx_{li}", _sds((T, D), BF16), W_gate[li],
                    [(d_out, 'tile'), (z_mlp, 'tile'), (row(mlp_g, li), 'row'), (pp, 'tile'), (gg, 'tile')],
                    ln_bwd_outs(False), ln_bwd_epi(1.0, False), tm=tm, tko=D, tc=D, comm=jobs, a_pro=gate_grads,
                    dws=[(xin, lambda: side['d_gg']), (p_b[li], lambda: side['d_pp'])])
        if jobs is not None:
            res = split_hosted(res, jobs, chips=[1] if chip_keys else [], pairs=[0])
        dz, dzb, dg, db, dw_gate, dw_proj = res
        grads[('mlp_ln_g', li)], grads[('mlp_ln_b', li)] = dg, db
        grads[('ple_w_gate', li)], grads[('ple_w_proj', li)] = dw_gate, dw_proj
        return dz, dzb

    recv = {}
    wqkv_cols = {'attn_w_q': (0, HD), 'kv_w_k': (HD, HD + KVD), 'kv_w_v': (HD + KVD, NQ)}

    def piece(name, li):
        if name == 'conv_w_in':
            return grads['conv_w_in']
        if name == 'mlp_w_up':
            return grads[('mlp_w_up', li)]
        if name == 'ple_w_proj':
            return jnp.transpose(grads[('ple_w_proj', li)][0].reshape(PLE, N_DEV, D // N_DEV), (1, 0, 2))
        if name in wqkv_cols:
            g = grads['w_qkv'][:, wqkv_cols[name][0]:wqkv_cols[name][1]]
        else:
            g = grads[name] if name in grads else grads[(name, li)]
            g = g[0]
        return g.reshape(N_DEV, g.shape[0] // N_DEV, g.shape[1])

    def pair_stage(*specs):
        job = _PairJob([piece(nm, li) for nm, li in specs])
        job.specs = specs
        return job

    sums = {}

    def pair_done(pair_job, got):
        name = "chip_sum_" + "_".join(f"{nm}_{li}" for nm, li in pair_job.specs)
        for spec, total in zip(pair_job.specs, chip_sums(name, list(zip(pair_job.sources, got)))):
            sums[spec] = total

    def chip_stage(*keys):
        job = _ChipJob([sums[k] for k in keys])
        job.specs = keys
        return job

    def hosted(res, job):
        main, got = res
        for spec, r in zip(job.specs, got):
            recv[spec] = r
        return main

    def split_hosted(res, jobs, chips, pairs):
        main, got = res
        parts = jobs.split(got)
        for k in chips:
            hosted((None, parts[k]), jobs.jobs[k])
        for k in pairs:
            pair_done(jobs.jobs[k], parts[k])
        return main

    def mlp_bwd(li, dz, dzb, xin, u, z_mix, with_colsum, du_pairs, du_chips, dx_chips=None):
        jobs = _Jobs([pair_stage(*du_pairs)] + ([chip_stage(*du_chips)] if du_chips else []))
        du, grads[('mlp_w_up', li)], grads[('mlp_w_down', li)] = split_hosted(
            mlp_du_dw(f"mlp_du_{li}", dzb, W_down[li], u, xin, F // N_DEV, tm=tm2, tf=min(1024, F), comm=jobs),
            jobs, chips=[1] if du_chips else [], pairs=[0])
        jobs = _Jobs([pair_stage(('mlp_w_up', li), ('mlp_w_down', li))] + ([chip_stage(*dx_chips)] if dx_chips else []))
        res = split_hosted(
            mm_nt(f"mlp_dx_{li}", du, W_up[li], [(dz, 'tile'), (z_mix, 'tile'), (row(mix_g, li), 'row')],
                  ln_bwd_outs(with_colsum), ln_bwd_epi(alpha, with_colsum), tm=tm, tko=D, tc=F, comm=jobs),
            jobs, chips=[1] if dx_chips else [], pairs=[0])
        grads[('mix_ln_g', li)], grads[('mix_ln_b', li)] = res[2], res[3]
        return res

    dz4, dz4b = ple_bwd(1, dy, x5b, pp1, gg1, z4)
    dz3, dz3b, _, _ = mlp_bwd(1, dz4, dz4b, x4b, u1, z3, False, [('ple_w_gate', 1), ('ple_w_proj', 1)], None)

    def do_epi(acc, ex, out, i):
        out[0][...] = acc.astype(BF16)
    do1, grads['attn_w_o'] = mm_nt("attn_do", dz3b, W_o, [], [(_sds((T, HD), BF16), 'tile')], do_epi, tm=tm, tko=HD,
                                   tc=D, dws=[(o1, None)])
    jobs = _Jobs([chip_stage(('mlp_w_up', 1)), pair_stage(('attn_w_o', 0)),
                  chip_stage(('ple_w_gate', 1), ('ple_w_proj', 1))])
    dq, dkv, d_sinks = split_hosted(
        attn_bwd(q1, kvx1, do1, lse1, a['attn_sinks'], rope, n_heads, n_kv, comm=jobs), jobs, chips=[0, 2], pairs=[1])

    def d_qkv_tile(ex):
        dk = _rope_wide(ex[2][:, 0:KVD], ex[3][...], ex[4][...], ex[5][...], _rope_chunk_bwd)
        return jnp.concatenate([ex[1][...], dk.astype(BF16), ex[2][:, KVD:].astype(BF16)], axis=1)

    def dx3_epi(acc, ex, out, i):
        out[0][...] = acc + alpha * ex[0][...]
    dx3, dw_qkv = mm_nt("attn_dx", _sds((T, NQ), BF16), W_qkv,
                        [(dz3, 'tile'), (dq, 'rows'), (dkv, 'rows')] + [(t, 'tab') for t in rope],
                        [(_sds((T, D), F32), 'tile')], dx3_epi, tm=tm, tko=D, tc=NQ, a_pro=d_qkv_tile,
                        dws=[(x3b, None)])
    grads['w_qkv'] = dw_qkv[0]

    dz2, dz2b = ple_bwd(0, dx3, x2b, pp0, gg0, z2)
    dz1, dz1b, _, _, db_out = mlp_bwd(
        0, dz2, dz2b, x1b, u0, z1, True,
        [('ple_w_gate', 0), ('ple_w_proj', 0), ('attn_w_q', 0), ('kv_w_k', 0), ('kv_w_v', 0)],
        [('mlp_w_down', 1), ('attn_w_o', 0)],
        dx_chips=[('ple_w_gate', 0), ('ple_w_proj', 0), ('attn_w_q', 0), ('kv_w_k', 0), ('kv_w_v', 0)])

    def ds_epi(acc, ex, out, i):
        n = _ln_fwd(ex[0][...], ex[1][...], ex[2][...])
        sg = _sigmoid(n)
        dn = acc * (sg * (1.0 + n * (1.0 - sg)))
        dc, dg, db = _ln_bwd(dn, ex[0][...], ex[1][...])
        out[0][...] = dc
        _init_or_add(out[1], i, dg)
        _init_or_add(out[2], i, db)
    dc0, d_cln_g, d_cln_b, grads['conv_w_out'] = mm_nt(
        "conv_ds", dz1b, W_out, [(c0, 'tile'), (cln_g, 'row'), (cln_b, 'row')],
        [(_sds((T, D), F32), 'tile'), (_sds((1, D), F32), 'rowacc'), (_sds((1, D), F32), 'rowacc')], ds_epi,
        tm=tm, tko=D, tc=D, dws=[(s0, None)])
    job = chip_stage(('mlp_w_up', 0), ('mlp_w_down', 0))
    dh0, d_wdw, d_bdw, d_bin = hosted(dwconv_bwd(dc0, g0, ha0, hg0, w_dw, tm=tmc, comm=job), job)
    grads['conv_w_in'] = mm_tn("d_conv_win", xb, dh0, 2 * D // N_DEV, BF16, tm=T, tk=tw, tn=tw)

    pair = pair_stage(('conv_w_in', 0), ('conv_w_out', 0))
    pair_done(pair, comm_only("pair_last", [pair])[0])
    last_chip = chip_stage(('conv_w_in', 0), ('conv_w_out', 0))

    def own_rows(vec, rows_used, rows):
        arr = vec.reshape(N_DEV, rows_used, LANES)
        return jnp.pad(arr, ((0, 0), (0, rows - rows_used), (0, 0)))
    dwdw_dev = jnp.transpose(d_wdw.reshape(CONV_HALO, N_DEV, D // N_DEV), (1, 0, 2))
    lane_rows = D // N_DEV // LANES
    small_grad = jnp.concatenate([
        own_rows(d_bin, 2 * lane_rows, 8), dwdw_dev if lane_rows == 1 else dwdw_dev.reshape(N_DEV, -1, LANES),
        own_rows(d_bdw, lane_rows, 8), own_rows(d_cln_g, lane_rows, 8), own_rows(d_cln_b, lane_rows, 8),
        own_rows(db_out, lane_rows, 8)], axis=1)
    n_small = small_grad.shape[1]

    rep_local = {'mix_ln_g': jnp.concatenate([grads[('mix_ln_g', li)] for li in range(DEPTH)], axis=0),
                 'mix_ln_b': jnp.concatenate([grads[('mix_ln_b', li)] for li in range(DEPTH)], axis=0),
                 'mlp_ln_g': jnp.concatenate([grads[('mlp_ln_g', li)] for li in range(DEPTH)], axis=0),
                 'mlp_ln_b': jnp.concatenate([grads[('mlp_ln_b', li)] for li in range(DEPTH)], axis=0),
                 'attn_sinks': d_sinks}
    rep_grad = _rep_rows(rep_local, '')
    n_rep = rep_grad.shape[0]
    last = _Jobs([last_chip, _DirectJob([small_grad, jnp.broadcast_to(rep_grad[None], (N_DEV, n_rep, LANES))])])

    def dx_epi(acc, ex, out, i):
        out[0][...] = acc + alpha * ex[0][...]
    (grad_x,), got = mm_nt("conv_dx", dh0, W_in, [(dz1, 'tile')], [(_sds((T, D), F32), 'tile')], dx_epi,
                           tm=tm, tko=D, tc=D, comm=last)
    got, (recv_small, recv_rep) = last.split(got)
    hosted((None, got), last_chip)

    result = {}
    kinds = ('grad', 'delta', 'new_m', 'new_v')
    w, m, v = (_small_rows(a, pre)[None] for pre in ('', 'm_', 'v_'))
    for kind, arr in zip(kinds, adamw("adamw_small", [recv_small], w, m, v, ta=n_small)):
        for pname, val in _unpack_small(arr[0], a).items():
            result[(kind, pname)] = val
    w, m, v = (_rep_rows(a, pre)[None] for pre in ('', 'm_', 'v_'))
    for kind, arr in zip(kinds, adamw("adamw_rep", [recv_rep], w, m, v, ta=n_rep)):
        for pname, val in _unpack_rep(arr[0], a).items():
            result[(kind, pname)] = val
    for name in BIG_WEIGHTS:
        w, m, v = (shard3(a[pre + name]) for pre in ('', 'm_', 'v_'))
        recvs = [recv[(name, li)] for li in range(w.shape[0])]
        for kind, arr in zip(kinds, adamw("adamw_" + name, recvs, w, m, v, ta=min(256, w.shape[1]))):
            result[(kind, name)] = arr.reshape(a[name].shape)

    loss = lax.psum(loss_local, ("x", "y", "c"))
    out = [loss, grad_x[None]]
    for kind in ('grad', 'delta', 'new_m', 'new_v'):
        out += [result[(kind, name)] for name in WEIGHT_NAMES]
    return tuple(out)


def _glu(x, W_in, b_in, T, D, tm, comm=None):
    n = W_in.shape[2]
    q = 2 if D // n % 2 == 0 else 1
    nt = D // (q * n)
    tn = q * n

    def body(x_ref, wa_ref, wg_ref, ba_ref, bg_ref, g_ref, ha_ref, hg_ref):
        xb = x_ref[...].astype(BF16)
        ha = jnp.concatenate([_dot(xb, wa_ref[s], ((1,), (0,))) for s in range(q)], axis=1) + ba_ref[...]
        hg = jnp.concatenate([_dot(xb, wg_ref[s], ((1,), (0,))) for s in range(q)], axis=1) + bg_ref[...]
        g_ref[...] = ha * _sigmoid(hg)
        ha_ref[...] = ha.astype(ha_ref.dtype)
        hg_ref[...] = hg.astype(hg_ref.dtype)

    return _call(
        body, comm, name="conv_in_glu", grid=(T // tm, nt),
        in_specs=[pl.BlockSpec((tm, D), lambda i, j: (i, 0)),
                  pl.BlockSpec((q, D, n), lambda i, j: (j, 0, 0)),
                  pl.BlockSpec((q, D, n), lambda i, j: (j + nt, 0, 0)),
                  pl.BlockSpec((1, tn), lambda i, j: (0, j)), pl.BlockSpec((1, tn), lambda i, j: (0, j + nt))],
        out_specs=[pl.BlockSpec((tm, tn), lambda i, j: (i, j))] * 3,
        out_shape=[_sds((T, D), F32), _sds((T, D), BF16), _sds((T, D), BF16)],
        sem=("parallel", "parallel"), operands=[x, W_in, W_in, b_in, b_in])


def kernel(x, p, conv_w_in, conv_b_in, conv_w_dw, conv_b_dw, conv_ln_g, conv_ln_b, conv_w_out, conv_b_out, kv_w_k, kv_w_v, attn_w_q, attn_sinks, attn_w_o, mix_ln_g, mix_ln_b, mlp_w_up, mlp_w_down, mlp_ln_g, mlp_ln_b, ple_w_proj, ple_w_gate, loss_target, m_conv_w_in, m_conv_b_in, m_conv_w_dw, m_conv_b_dw, m_conv_ln_g, m_conv_ln_b, m_conv_w_out, m_conv_b_out, m_kv_w_k, m_kv_w_v, m_attn_w_q, m_attn_sinks, m_attn_w_o, m_mix_ln_g, m_mix_ln_b, m_mlp_w_up, m_mlp_w_down, m_mlp_ln_g, m_mlp_ln_b, m_ple_w_proj, m_ple_w_gate, v_conv_w_in, v_conv_b_in, v_conv_w_dw, v_conv_b_dw, v_conv_ln_g, v_conv_ln_b, v_conv_w_out, v_conv_b_out, v_kv_w_k, v_kv_w_v, v_attn_w_q, v_attn_sinks, v_attn_w_o, v_mix_ln_g, v_mix_ln_b, v_mlp_w_up, v_mlp_w_down, v_mlp_ln_g, v_mlp_ln_b, v_ple_w_proj, v_ple_w_gate):
    return _step(dict(locals()))
```

```python
import functools
import math

import jax
import jax.numpy as jnp
from jax import lax
from jax.experimental import pallas as pl
from jax.experimental.pallas import tpu as pltpu

F32 = jnp.float32
BF16 = jnp.bfloat16

N_DEV = 8
HEAD_DIM = 64
ROPE_DIM = HEAD_DIM // 4
ROPE_HALF = ROPE_DIM // 2
ROPE_THETA = 500000.0
ATT_BLOCK = 128
CONV_WIDTH = 31
CONV_HALO = 32
CONV_ROWS = 64
LN_EPS = 1e-5
DEPTH = 2
DEEPNORM_ALPHA = (2 * DEPTH) ** 0.25
MASK_VALUE = -1e30

ADAM_LR = 0.001
ADAM_B1 = 0.9
ADAM_B2 = 0.999
ADAM_EPS = 1e-08
ADAM_WD = 0.01
ADAM_STEP = 10

LANES = 128
SUBLANES = 8
VMEM_LIMIT_BYTES = 52 * 1024 * 1024
TOKEN_TILE = 512
MESH_ID = pl.DeviceIdType.MESH
RELAY_AT_TENTHS = 5
FORWARD_AT_TENTHS = 8

WEIGHT_NAMES = ['conv_w_in', 'conv_b_in', 'conv_w_dw', 'conv_b_dw', 'conv_ln_g', 'conv_ln_b', 'conv_w_out',
                'conv_b_out', 'kv_w_k', 'kv_w_v', 'attn_w_q', 'attn_sinks', 'attn_w_o', 'mix_ln_g', 'mix_ln_b',
                'mlp_w_up', 'mlp_w_down', 'mlp_ln_g', 'mlp_ln_b', 'ple_w_proj', 'ple_w_gate']
BIG_WEIGHTS = ['conv_w_in', 'conv_w_out', 'kv_w_k', 'kv_w_v', 'attn_w_q', 'attn_w_o', 'mlp_w_up', 'mlp_w_down',
               'ple_w_proj', 'ple_w_gate']
SMALL_SHARDED = [('conv_b_in', 8), ('conv_w_dw', 32), ('conv_b_dw', 8), ('conv_ln_g', 8), ('conv_ln_b', 8),
                 ('conv_b_out', 8)]
REPLICATED = ['mix_ln_g', 'mix_ln_b', 'mlp_ln_g', 'mlp_ln_b', 'attn_sinks']


def _params(sem):
    return pltpu.CompilerParams(dimension_semantics=sem, vmem_limit_bytes=VMEM_LIMIT_BYTES)


def _sds(shape, dtype):
    return jax.ShapeDtypeStruct(shape, dtype)


def _my_place():
    x, y, c = lax.axis_index("x"), lax.axis_index("y"), lax.axis_index("c")
    return x, y, c, 4 * x + 2 * y + c


def _peers(x, y, c):
    out = []
    for dx in (0, 1):
        for dy in (0, 1):
            for dc in (0, 1):
                if dx or dy or dc:
                    px, py, pc = x ^ dx, y ^ dy, c ^ dc
                    out.append(((px, py, pc), 4 * px + 2 * py + pc))
    return out


N_CHIPS = 4


def _other_chips(x, y):
    return [((x ^ dx, y ^ dy), 2 * (x ^ dx) + (y ^ dy)) for dx, dy in ((1, 0), (0, 1), (1, 1))]


def _remote(src, dst, send, recv, to):
    return pltpu.make_async_remote_copy(src_ref=src, dst_ref=dst, send_sem=send, recv_sem=recv, device_id=to,
                                        device_id_type=MESH_ID)


def _wait_slabs(buf, count, send, recv, me, sent=True, received=True):
    part = buf.at[pl.ds(0, count)]
    cp = _remote(part, part, send, recv, me)
    if sent:
        cp.wait_send()
    if received:
        cp.wait_recv()


class _DirectJob:
    n_sems = 3

    def __init__(self, items):
        self.sources = list(items)
        self.dests = [_sds(it.shape, it.dtype) for it in items]
        self.n = len(items)

    def start(self, src, dst, send, recv, loc):
        x, y, c, me = _my_place()
        for k in range(self.n):
            here = dst[k].at[pl.ds(me, 1)]
            pltpu.make_async_copy(src[k].at[pl.ds(me, 1)], here, loc.at[k]).start()
            for peer, idx in _peers(x, y, c):
                _remote(src[k].at[pl.ds(idx, 1)], here, send.at[k], recv.at[k], peer).start()

    def relay(self, *refs):
        pass

    def forward(self, *refs):
        pass

    def finish(self, src, dst, send, recv, loc):
        x, y, c, me = _my_place()
        for k in range(self.n):
            pltpu.make_async_copy(src[k].at[pl.ds(me, 1)], dst[k].at[pl.ds(me, 1)], loc.at[k]).wait()
            _wait_slabs(dst[k], N_DEV - 1, send.at[k], recv.at[k], (x, y, c))


class _GatherJob:
    n_sems = 7

    def __init__(self, items):
        self.sources = [it.reshape((1,) + it.shape) for it in items]
        self.dests = [_sds((N_DEV,) + it.shape, it.dtype) for it in items]
        self.n = len(items)

    @staticmethod
    def _neighbours(x, y, c):
        out = []
        for dx, dy in ((c, 1 - c), (1 - c, c), (1, 1)):
            px, py = x ^ dx, y ^ dy
            out.append(((px, py, c), 4 * px + 2 * py + c))
        return out

    def start(self, src, dst, send_a, recv_a, send_b, recv_b, send_d, recv_d, loc):
        x, y, c, me = _my_place()
        (first, _), (second, _), _ = self._neighbours(x, y, c)
        for k in range(self.n):
            here = dst[k].at[pl.ds(me, 1)]
            pltpu.make_async_copy(src[k], here, loc.at[k]).start()
            _remote(src[k], here, send_d.at[k], recv_d.at[k], (x, y, 1 - c)).start()
            _remote(src[k], here, send_a.at[k], recv_a.at[k], first).start()
            _remote(src[k], here, send_b.at[k], recv_b.at[k], second).start()

    def relay(self, src, dst, send_a, recv_a, send_b, recv_b, send_d, recv_d, loc):
        x, y, c, me = _my_place()
        (_, first_idx), (second, _), _ = self._neighbours(x, y, c)
        for k in range(self.n):
            _wait_slabs(dst[k], 1, send_a.at[k], recv_a.at[k], (x, y, c), sent=False)
            rows = dst[k].at[pl.ds(first_idx, 1)]
            _remote(rows, rows, send_b.at[k], recv_b.at[k], second).start()

    def forward(self, src, dst, send_a, recv_a, send_b, recv_b, send_d, recv_d, loc):
        x, y, c, me = _my_place()
        for k in range(self.n):
            _wait_slabs(dst[k], 2, send_b.at[k], recv_b.at[k], (x, y, c), sent=False)
            for _, idx in self._neighbours(x, y, c):
                rows = dst[k].at[pl.ds(idx, 1)]
                _remote(rows, rows, send_d.at[k], recv_d.at[k], (x, y, 1 - c)).start()

    def finish(self, src, dst, send_a, recv_a, send_b, recv_b, send_d, recv_d, loc):
        x, y, c, me = _my_place()
        for k in range(self.n):
            pltpu.make_async_copy(src[k], dst[k].at[pl.ds(me, 1)], loc.at[k]).wait()
            _wait_slabs(dst[k], 1, send_a.at[k], recv_a.at[k], (x, y, c), received=False)
            _wait_slabs(dst[k], 2, send_b.at[k], recv_b.at[k], (x, y, c), received=False)
            _wait_slabs(dst[k], N_CHIPS, send_d.at[k], recv_d.at[k], (x, y, c))


class _PairJob:
    n_sems = 2

    def __init__(self, items):
        self.sources = list(items)
        self.dests = [_sds((N_CHIPS,) + it.shape[1:], it.dtype) for it in items]
        self.n = len(items)

    def start(self, src, dst, send, recv):
        x, y, c, me = _my_place()
        for k in range(self.n):
            for q in range(N_CHIPS):
                _remote(src[k].at[pl.ds(2 * q + 1 - c, 1)], dst[k].at[pl.ds(q, 1)], send.at[k], recv.at[k],
                        (x, y, 1 - c)).start()

    def relay(self, *refs):
        pass

    def forward(self, *refs):
        pass

    def finish(self, src, dst, send, recv):
        x, y, c, me = _my_place()
        for k in range(self.n):
            _wait_slabs(dst[k], N_CHIPS, send.at[k], recv.at[k], (x, y, c))


class _ChipJob:
    n_sems = 3

    def __init__(self, items):
        self.sources = list(items)
        self.dests = [_sds(it.shape, it.dtype) for it in items]
        self.n = len(items)

    def start(self, src, dst, send, recv, loc):
        x, y, c, me = _my_place()
        mine = 2 * x + y
        for k in range(self.n):
            here = dst[k].at[pl.ds(mine, 1)]
            pltpu.make_async_copy(src[k].at[pl.ds(mine, 1)], here, loc.at[k]).start()
            for (px, py), q in _other_chips(x, y):
                _remote(src[k].at[pl.ds(q, 1)], here, send.at[k], recv.at[k], (px, py, c)).start()

    def relay(self, *refs):
        pass

    def forward(self, *refs):
        pass

    def finish(self, src, dst, send, recv, loc):
        x, y, c, me = _my_place()
        mine = 2 * x + y
        for k in range(self.n):
            pltpu.make_async_copy(src[k].at[pl.ds(mine, 1)], dst[k].at[pl.ds(mine, 1)], loc.at[k]).wait()
            _wait_slabs(dst[k], N_CHIPS - 1, send.at[k], recv.at[k], (x, y, c))


class _Jobs:
    def __init__(self, jobs):
        self.jobs = list(jobs)
        self.sources = [s for job in self.jobs for s in job.sources]
        self.dests = [d for job in self.jobs for d in job.dests]
        self.n = len(self.sources)

    def _stage(self, stage, src, dst, sems):
        k0 = s0 = 0
        for job in self.jobs:
            getattr(job, stage)(src[k0:k0 + job.n], dst[k0:k0 + job.n], *sems[s0:s0 + job.n_sems])
            k0 += job.n
            s0 += job.n_sems

    def start(self, src, dst, *sems):
        self._stage('start', src, dst, sems)

    def relay(self, src, dst, *sems):
        self._stage('relay', src, dst, sems)

    def forward(self, src, dst, *sems):
        self._stage('forward', src, dst, sems)

    def finish(self, src, dst, *sems):
        self._stage('finish', src, dst, sems)

    def split(self, got):
        out, k0 = [], 0
        for job in self.jobs:
            out.append(got[k0:k0 + job.n])
            k0 += job.n
        return out


def _job_sems(job):
    if isinstance(job, _Jobs):
        return [s for part in job.jobs for s in _job_sems(part)]
    return [pltpu.SemaphoreType.DMA((job.n,))] * job.n_sems


def _call(body, comm, *, name, grid, in_specs, out_specs, out_shape, operands, sem, scratch_shapes=(), aliases=None):
    single = not isinstance(out_shape, (list, tuple))
    out_shape = [out_shape] if single else list(out_shape)
    out_specs = [out_specs] if single else list(out_specs)
    if comm is None:
        res = pl.pallas_call(body, name=name, grid=grid, in_specs=list(in_specs), out_specs=out_specs,
                             out_shape=out_shape, scratch_shapes=list(scratch_shapes),
                             input_output_aliases=aliases or {}, compiler_params=_params(sem))(*operands)
        return res[0] if single else res
    n_in, n_out, n_scr, n_c = len(in_specs), len(out_shape), len(scratch_shapes), comm.n
    any_spec = pl.BlockSpec(memory_space=pl.ANY)
    steps = math.prod(grid)
    early = min(steps - 1, (steps * RELAY_AT_TENTHS) // 10)
    mid = min(steps - 1, (steps * FORWARD_AT_TENTHS) // 10)

    def hosted(*refs):
        ins, c_src = refs[:n_in], refs[n_in:n_in + n_c]
        outs = refs[n_in + n_c:n_in + n_c + n_out]
        c_dst = refs[n_in + n_c + n_out:n_in + 2 * n_c + n_out]
        scr = refs[n_in + 2 * n_c + n_out:n_in + 2 * n_c + n_out + n_scr]
        sems = refs[n_in + 2 * n_c + n_out + n_scr:]
        step = pl.program_id(0)
        for d in range(1, len(grid)):
            step = step * grid[d] + pl.program_id(d)

        @pl.when(step == 0)
        def _():
            comm.start(c_src, c_dst, *sems)

        @pl.when(step == early)
        def _():
            comm.relay(c_src, c_dst, *sems)

        @pl.when(step == mid)
        def _():
            comm.forward(c_src, c_dst, *sems)

        body(*ins, *outs, *scr)

        @pl.when(step == steps - 1)
        def _():
            comm.finish(c_src, c_dst, *sems)

    res = pl.pallas_call(hosted, name=name, grid=grid, in_specs=list(in_specs) + [any_spec] * n_c,
                         out_specs=out_specs + [any_spec] * n_c, out_shape=out_shape + comm.dests,
                         scratch_shapes=list(scratch_shapes) + _job_sems(comm), input_output_aliases=aliases or {},
                         compiler_params=_params(("arbitrary",) * len(grid)))(*operands, *comm.sources)
    main = res[:n_out]
    return (main[0] if single else main), res[n_out:]


def comm_only(name, jobs):
    any_spec = pl.BlockSpec(memory_space=pl.ANY)
    n_all = sum(job.n for job in jobs)

    def body(*refs):
        srcs, dsts, sems = refs[:n_all], refs[n_all:2 * n_all], refs[2 * n_all:]
        parts, k0, s0 = [], 0, 0
        for job in jobs:
            parts.append((job, srcs[k0:k0 + job.n], dsts[k0:k0 + job.n], sems[s0:s0 + job.n_sems]))
            k0 += job.n
            s0 += job.n_sems
        for stage in ('start', 'relay', 'forward', 'finish'):
            for job, src, dst, sem in parts:
                getattr(job, stage)(src, dst, *sem)

    res = pl.pallas_call(body, name=name, in_specs=[any_spec] * n_all, out_specs=[any_spec] * n_all,
                         out_shape=[d for job in jobs for d in job.dests],
                         scratch_shapes=[s for job in jobs for s in _job_sems(job)],
                         )(*[s for job in jobs for s in job.sources])
    out, k0 = [], 0
    for job in jobs:
        out.append(res[k0:k0 + job.n])
        k0 += job.n
    return out


def _dot(a, b, dims):
    if a.dtype != BF16:
        a = a.astype(BF16)
    if b.dtype != BF16:
        b = b.astype(BF16)
    return lax.dot_general(a, b, (dims, ((), ())), preferred_element_type=F32)


def _sigmoid(v):
    return 1.0 / (1.0 + jnp.exp(-v))


def _sq_relu(u):
    return jnp.square(jnp.maximum(u.astype(F32), 0.0)).astype(BF16)


def _ln_stats(z):
    mu = jnp.mean(z, axis=-1, keepdims=True)
    zc = z - mu
    var = jnp.mean(zc * zc, axis=-1, keepdims=True)
    return zc * lax.rsqrt(var + LN_EPS)


def _ln_fwd(z, g, b):
    return _ln_stats(z) * g + b


def _ln_bwd(dy, z, g):
    xhat = _ln_stats(z)
    mu = jnp.mean(z, axis=-1, keepdims=True)
    zc = z - mu
    rstd = lax.rsqrt(jnp.mean(zc * zc, axis=-1, keepdims=True) + LN_EPS)
    dxh = dy * g
    m1 = jnp.mean(dxh, axis=-1, keepdims=True)
    m2 = jnp.mean(dxh * xhat, axis=-1, keepdims=True)
    dz = rstd * (dxh - m1 - xhat * m2)
    return dz, jnp.sum(dy * xhat, axis=0, keepdims=True), jnp.sum(dy, axis=0, keepdims=True)


def _extra_spec(shape, kind, tm, tn, ij):
    if kind == 'tile':
        return pl.BlockSpec((tm, tn), lambda *g: (ij(g)[0], ij(g)[1]))
    if kind in ('row', 'rowacc'):
        return pl.BlockSpec((1, tn), lambda *g: (0, ij(g)[1]))
    if kind == 'tab':
        return pl.BlockSpec((tm, LANES), lambda *g: (ij(g)[0], 0))
    if kind == 'rows':
        return pl.BlockSpec((tm, shape[1]), lambda *g: (ij(g)[0], 0))
    raise ValueError(kind)


def mm_nn(name, pairs, extras, outs, epi, *, tm, tn, tk=None, comm=None, a_pro=None):
    M = pairs[0][0].shape[0]
    N = pairs[0][1].shape[0] * pairs[0][1].shape[2]
    n_pairs = len(pairs)
    K0 = pairs[0][0].shape[1]
    tk = K0 if tk is None else tk
    nk = K0 // tk
    assert nk == 1 or n_pairs == 1
    assert M % tm == 0 and N % tn == 0 and K0 % tk == 0
    has_rowacc = any(kind == 'rowacc' for _, kind in outs)
    assert not has_rowacc or (N == tn and nk == 1)
    in_specs, operands, slabs = [], [], []
    for a, b, off in pairs:
        K = a.shape[1]
        ktile = K if n_pairs > 1 else tk
        n = b.shape[2]
        assert b.shape[1] == K
        in_specs.append(pl.BlockSpec((tm, ktile), lambda i, j, k: (i, k)))
        if tn <= n:
            assert n % tn == 0
            r = n // tn
            in_specs.append(pl.BlockSpec((None, ktile, tn),
                                         lambda i, j, k, r=r, off=off: ((j + off) // r, k, (j + off) % r)))
            slabs.append(0)
        else:
            assert tn % n == 0
            in_specs.append(pl.BlockSpec((tn // n, ktile, n), lambda i, j, k, off=off: (j + off, k, 0)))
            slabs.append(tn // n)
        operands += [a, b]
    ij = lambda g: (g[0], g[1])
    for arr, kind in extras:
        in_specs.append(_extra_spec(arr.shape, kind, tm, tn, ij))
        operands.append(arr)
    out_specs = [_extra_spec(o.shape, kind, tm, tn, ij) for o, kind in outs]
    n_ex, n_out = len(extras), len(outs)

    def pair_dot(ab, q):
        a = ab[2 * q][...]
        if a_pro is not None:
            a = a_pro(a)
        if not slabs[q]:
            return _dot(a, ab[2 * q + 1][...], ((1,), (0,)))
        return jnp.concatenate([_dot(a, ab[2 * q + 1][s], ((1,), (0,))) for s in range(slabs[q])], axis=1)

    def body(*refs):
        ab = refs[:2 * n_pairs]
        ex = refs[2 * n_pairs:2 * n_pairs + n_ex]
        out = refs[2 * n_pairs + n_ex:2 * n_pairs + n_ex + n_out]
        i = pl.program_id(0)
        if nk == 1:
            accs = [pair_dot(ab, q) for q in range(n_pairs)]
            epi(accs, ex, out, i)
        else:
            acc_ref = refs[-1]
            k = pl.program_id(2)

            @pl.when(k == 0)
            def _():
                acc_ref[...] = jnp.zeros_like(acc_ref)

            acc_ref[...] += pair_dot(ab, 0)

            @pl.when(k == nk - 1)
            def _():
                epi([acc_ref[...]], ex, out, i)

    scratch = [pltpu.VMEM((tm, tn), F32)] if nk > 1 else []
    sem = ("arbitrary",) * 3 if has_rowacc else ("parallel", "parallel", "arbitrary")
    return _call(body, comm, name=name, grid=(M // tm, N // tn, nk), in_specs=in_specs, out_specs=out_specs,
                 out_shape=[o for o, _ in outs], scratch_shapes=scratch, sem=sem, operands=operands)


def mm_nt(name, a, b, extras, outs, epi, *, tm, tko, tc, comm=None, a_pro=None, dws=()):
    M, N = a.shape
    J, K, n = b.shape
    assert J * n == N and M % tm == 0 and K % tko == 0 and N % tc == 0
    nc = N // tc
    assert a_pro is None or (nc == 1 and tc <= n and K == tko)
    assert not dws or (nc == 1 and K == tko and tc <= n)
    n_dw, nt = len(dws), M // tm
    has_rowacc = any(kind == 'rowacc' for _, kind in outs)
    assert not has_rowacc or K == tko
    if tc <= n:
        assert n % tc == 0
        r = n // tc
        slabs = 0
        b_spec = pl.BlockSpec((None, tko, tc), lambda i, j, c: (c // r, j, c % r))
    else:
        assert tc % n == 0
        slabs = tc // n
        b_spec = pl.BlockSpec((slabs, tko, n), lambda i, j, c: (c, j, 0))
    n_ab = 2 if a_pro is None else 1
    in_specs = ([pl.BlockSpec((tm, tc), lambda i, j, c: (i, c))] if a_pro is None else []) + [b_spec]

    def nt_dot(a_ref, b_ref):
        if not slabs:
            return _dot(a_ref[...], b_ref[...], ((1,), (1,)))
        acc = _dot(a_ref[:, 0:n], b_ref[0], ((1,), (1,)))
        for s in range(1, slabs):
            acc = acc + _dot(a_ref[:, s * n:(s + 1) * n], b_ref[s], ((1,), (1,)))
        return acc

    ij = lambda g: (g[0], g[1])
    operands = [a, b] if a_pro is None else [b]
    for arr, kind in extras:
        in_specs.append(_extra_spec(arr.shape, kind, tm, tko, ij))
        operands.append(arr)
    out_specs = [_extra_spec(o.shape, kind, tm, tko, ij) for o, kind in outs]
    n_ex, n_out = len(extras), len(outs)
    out_shape = [o for o, _ in outs]
    for lhs, _ in dws:
        in_specs.append(pl.BlockSpec((tm, lhs.shape[1]), lambda i, j, c: (i, 0)))
        operands.append(lhs)
        out_specs.append(pl.BlockSpec((None, lhs.shape[1], N), lambda i, j, c: (0, 0, 0)))
        out_shape.append(_sds((1, lhs.shape[1], N), BF16))

    def weight_grads(refs, a_tile, i):
        lhs_refs = refs[n_ab + n_ex:n_ab + n_ex + n_dw]
        dw_refs = refs[n_ab + n_ex + n_dw + n_out:n_ab + n_ex + n_dw + n_out + n_dw]
        acc_refs = refs[len(refs) - n_dw:]
        for (_, rhs_fn), lhs_ref, dw_ref, acc_ref in zip(dws, lhs_refs, dw_refs, acc_refs):
            part = _dot(lhs_ref[...], a_tile if rhs_fn is None else rhs_fn(), ((0,), (0,)))
            _init_or_add(acc_ref, i, part)

            @pl.when(i == nt - 1)
            def _(dw_ref=dw_ref, acc_ref=acc_ref):
                dw_ref[...] = acc_ref[...].astype(dw_ref.dtype)

    def body(*refs):
        ex = refs[n_ab:n_ab + n_ex]
        out = refs[n_ab + n_ex + n_dw:n_ab + n_ex + n_dw + n_out]
        i = pl.program_id(0)
        if a_pro is not None:
            a_tile = a_pro(ex)
            epi(_dot(a_tile, refs[0][...], ((1,), (1,))), ex, out, i)
            weight_grads(refs, a_tile, i)
            return
        a_ref, b_ref = refs[:2]
        if nc == 1:
            epi(nt_dot(a_ref, b_ref), ex, out, i)
            if dws:
                weight_grads(refs, a_ref[...], i)
        else:
            acc_ref = refs[-1]
            c = pl.program_id(2)

            @pl.when(c == 0)
            def _():
                acc_ref[...] = jnp.zeros_like(acc_ref)

            acc_ref[...] += nt_dot(a_ref, b_ref)

            @pl.when(c == nc - 1)
            def _():
                epi(acc_ref[...], ex, out, i)

    scratch = ([pltpu.VMEM((tm, tko), F32)] if nc > 1 else []) + [pltpu.VMEM((lhs.shape[1], N), F32) for lhs, _ in dws]
    sem = ("arbitrary",) * 3 if (has_rowacc or dws) else ("parallel", "parallel", "arbitrary")
    return _call(body, comm, name=name, grid=(M // tm, K // tko, nc), in_specs=in_specs, out_specs=out_specs,
                 out_shape=out_shape, scratch_shapes=scratch, sem=sem, operands=operands)


def mm_tn(name, a, d, n, out_dtype, *, tm, tk, tn, comm=None):
    M, K = a.shape
    N = d.shape[1]
    assert d.shape[0] == M and N % n == 0 and N % tn == 0 and K % tk == 0 and M % tm == 0
    nm = M // tm
    if tn <= n:
        assert n % tn == 0
        r = n // tn
        slabs = 0
        o_spec = pl.BlockSpec((None, tk, tn), lambda kk, j, m: (j // r, kk, j % r))
    else:
        assert tn % n == 0
        slabs = tn // n
        o_spec = pl.BlockSpec((slabs, tk, n), lambda kk, j, m: (j, kk, 0))

    def write(o_ref, acc):
        if not slabs:
            o_ref[...] = acc.astype(o_ref.dtype)
        else:
            for s in range(slabs):
                o_ref[s] = acc[:, s * n:(s + 1) * n].astype(o_ref.dtype)

    def body(a_ref, d_ref, o_ref, *scratch):
        if nm == 1:
            write(o_ref, _dot(a_ref[...], d_ref[...], ((0,), (0,))))
            return
        acc_ref, = scratch
        m = pl.program_id(2)

        @pl.when(m == 0)
        def _():
            acc_ref[...] = jnp.zeros_like(acc_ref)

        acc_ref[...] += _dot(a_ref[...], d_ref[...], ((0,), (0,)))

        @pl.when(m == nm - 1)
        def _():
            write(o_ref, acc_ref[...])

    return _call(
        body, comm, name=name, grid=(K // tk, N // tn, nm),
        in_specs=[pl.BlockSpec((tm, tk), lambda kk, j, m: (m, kk)), pl.BlockSpec((tm, tn), lambda kk, j, m: (m, j))],
        out_specs=o_spec, out_shape=_sds((N // n, K, n), out_dtype),
        scratch_shapes=[pltpu.VMEM((tk, tn), F32)] if nm > 1 else [],
        sem=("parallel", "parallel", "arbitrary"), operands=[a, d])


def mlp_du_dw(name, dzb, w_down, u, xin, n, *, tm, tf, comm=None):
    T, D = dzb.shape
    F = u.shape[1]
    assert T % tm == 0 and F % tf == 0 and tf % n == 0
    slabs, nt = tf // n, T // tm

    def body(dz_ref, w_ref, u_ref, x_ref, du_ref, dwu_ref, dwd_ref, up_acc, down_acc):
        i = pl.program_id(1)
        dz = dz_ref[...]
        da = _dot(dz, w_ref[...], ((1,), (1,)))
        relu = jnp.maximum(u_ref[...].astype(F32), 0.0)
        du = (da * (2.0 * relu)).astype(BF16)
        du_ref[...] = du
        _init_or_add(up_acc, i, _dot(x_ref[...], du, ((0,), (0,))))
        _init_or_add(down_acc, i, _dot(jnp.square(relu).astype(BF16), dz, ((0,), (0,))))

        @pl.when(i == nt - 1)
        def _():
            for s in range(slabs):
                dwu_ref[s] = up_acc[:, s * n:(s + 1) * n].astype(dwu_ref.dtype)
            dwd_ref[...] = down_acc[...].astype(dwd_ref.dtype)

    hidden = pl.BlockSpec((tm, tf), lambda j, i: (i, j))
    tokens = pl.BlockSpec((tm, D), lambda j, i: (i, 0))
    return _call(
        body, comm, name=name, grid=(F // tf, nt),
        in_specs=[tokens, pl.BlockSpec((None, tf, D), lambda j, i: (0, j, 0)), hidden, tokens],
        out_specs=[hidden, pl.BlockSpec((slabs, D, n), lambda j, i: (j, 0, 0)),
                   pl.BlockSpec((None, tf, D), lambda j, i: (0, j, 0))],
        out_shape=[_sds((T, F), BF16), _sds((F // n, D, n), BF16), _sds((1, F, D), BF16)],
        scratch_shapes=[pltpu.VMEM((D, tf), F32), pltpu.VMEM((tf, D), F32)], sem=("arbitrary", "arbitrary"),
        operands=[dzb, w_down, u, xin])


def _init_or_add(ref, i, value):
    @pl.when(i == 0)
    def _():
        ref[...] = value

    @pl.when(i > 0)
    def _():
        ref[...] += value


def _rope_tables(T):
    pos = jnp.arange(T, dtype=F32)
    inv_freq = ROPE_THETA ** (-jnp.arange(0, ROPE_DIM, 2, dtype=F32) / ROPE_DIM)
    ang = pos[:, None] * inv_freq[None, :]
    cos, sin = jnp.cos(ang), jnp.sin(ang)
    ones = jnp.ones((T, HEAD_DIM - ROPE_DIM), F32)
    zeros = jnp.zeros((T, HEAD_DIM - ROPE_DIM), F32)
    zh = jnp.zeros((T, ROPE_HALF), F32)
    c_head = jnp.concatenate([cos, cos, ones], axis=1)
    s_up = jnp.concatenate([-sin, zh, zeros], axis=1)
    s_dn = jnp.concatenate([zh, sin, zeros], axis=1)
    rep = LANES // HEAD_DIM
    return tuple(jnp.tile(t, (1, rep)) for t in (c_head, s_up, s_dn))


def _rope_chunk(t, c, s_up, s_dn):
    return t * c + pltpu.roll(t, LANES - ROPE_HALF, 1) * s_up + pltpu.roll(t, ROPE_HALF, 1) * s_dn


def _rope_chunk_bwd(d, c, s_up, s_dn):
    return d * c + pltpu.roll(d * s_up, ROPE_HALF, 1) + pltpu.roll(d * s_dn, LANES - ROPE_HALF, 1)


def _rope_wide(t, c, s_up, s_dn, fn):
    chunks = [fn(t[:, q * LANES:(q + 1) * LANES], c, s_up, s_dn) for q in range(t.shape[1] // LANES)]
    return chunks[0] if len(chunks) == 1 else jnp.concatenate(chunks, axis=1)


def _taps_by_residue(first):
    groups = {}
    for o in range(first, first + CONV_WIDTH):
        groups.setdefault(o % SUBLANES, []).append(o)
    return sorted(groups.items())


def _shifted_taps(win_ref, span_ref, r0, res, offs, rb, ls):
    if res == 0:
        return [functools.partial(lambda o: win_ref[pl.ds(r0 + o, rb), ls], o) for o in offs]
    n = rb + offs[-1] - res
    span_ref[0:n, :] = win_ref[pl.ds(r0 + res, n), ls]
    return [functools.partial(lambda o: span_ref[pl.ds(o - res, rb), :], o) for o in offs]


def _halo_before_spec(tm, D):
    return pl.BlockSpec((CONV_HALO, D), lambda i: (jnp.maximum(i * (tm // CONV_HALO) - 1, 0), 0))


def dwconv_fwd(g, w_dw, b_dw, ln_g, ln_b, *, tm, comm=None):
    T, D = g.shape
    nl = D // LANES

    rb = min(CONV_ROWS, tm)

    def body(g_ref, gh_ref, w_ref, b_ref, lg_ref, lb_ref, c_ref, s_ref, win_ref, span_ref):
        i = pl.program_id(0)
        win_ref[0:CONV_HALO, :] = jnp.where(i > 0, gh_ref[...], 0.0)
        win_ref[CONV_HALO:, :] = g_ref[...]

        def lane_chunk(q, carry):
            ls = pl.ds(pl.multiple_of(q * LANES, LANES), LANES)
            for r0 in range(0, tm, rb):
                acc = jnp.broadcast_to(b_ref[:, ls], (rb, LANES))
                for res, offs in _taps_by_residue(CONV_HALO - (CONV_WIDTH - 1)):
                    taps = _shifted_taps(win_ref, span_ref, r0, res, offs, rb, ls)
                    for o, tap in zip(offs, taps):
                        k = o - (CONV_HALO - (CONV_WIDTH - 1))
                        acc = acc + tap() * w_ref[k:k + 1, ls]
                c_ref[pl.ds(r0, rb), ls] = acc
            return carry

        lax.fori_loop(0, nl, lane_chunk, 0)
        n = _ln_fwd(c_ref[...], lg_ref[...], lb_ref[...])
        s_ref[...] = (n * _sigmoid(n)).astype(s_ref.dtype)

    row = pl.BlockSpec((1, D), lambda i: (0, 0))
    return _call(
        body, comm, name="dwconv_fwd", grid=(T // tm,),
        in_specs=[pl.BlockSpec((tm, D), lambda i: (i, 0)), _halo_before_spec(tm, D),
                  pl.BlockSpec((CONV_HALO, D), lambda i: (0, 0)), row, row, row],
        out_specs=[pl.BlockSpec((tm, D), lambda i: (i, 0)), pl.BlockSpec((tm, D), lambda i: (i, 0))],
        out_shape=[_sds((T, D), F32), _sds((T, D), BF16)],
        scratch_shapes=[pltpu.VMEM((tm + CONV_HALO, D), F32), pltpu.VMEM((rb + CONV_HALO, LANES), F32)],
        sem=("parallel",), operands=[g, g, w_dw, b_dw, ln_g, ln_b])


def dwconv_bwd(dc, g, ha, hg, w_dw, *, tm, comm=None):
    T, D = g.shape
    nl = D // LANES
    last = T // CONV_HALO - 1
    nt = T // tm

    rb = min(CONV_ROWS, tm)

    def body(dc_ref, dcn_ref, g_ref, gh_ref, ha_ref, hg_ref, w_ref, dh_ref, dw_ref, dbdw_ref, dbin_ref,
             win_ref, dwin_ref, dg_ref, dwp_ref, span_ref):
        i = pl.program_id(0)
        win_ref[0:CONV_HALO, :] = jnp.where(i > 0, gh_ref[...], 0.0)
        win_ref[CONV_HALO:, :] = g_ref[...]
        dwin_ref[0:tm, :] = dc_ref[...]
        dwin_ref[tm:, :] = jnp.where(i < nt - 1, dcn_ref[...], 0.0)

        @pl.when(i == 0)
        def _():
            dwp_ref[...] = jnp.zeros_like(dwp_ref)

        first = CONV_HALO - (CONV_WIDTH - 1)

        def lane_chunk(q, carry):
            ls = pl.ds(pl.multiple_of(q * LANES, LANES), LANES)
            for r0 in range(0, tm, rb):
                acc = jnp.zeros((rb, LANES), F32)
                for res, offs in _taps_by_residue(0):
                    taps = _shifted_taps(dwin_ref, span_ref, r0, res, offs, rb, ls)
                    for o, tap in zip(offs, taps):
                        k = CONV_WIDTH - 1 - o
                        acc = acc + tap() * w_ref[k:k + 1, ls]
                dg_ref[pl.ds(r0, rb), ls] = acc
                dcv = dwin_ref[pl.ds(r0, rb), ls]
                for res, offs in _taps_by_residue(first):
                    taps = _shifted_taps(win_ref, span_ref, r0, res, offs, rb, ls)
                    for o, tap in zip(offs, taps):
                        k = o - first
                        prod = dcv * tap()
                        dwp_ref[k, :, ls] += jnp.sum(prod.reshape(rb // SUBLANES, SUBLANES, LANES), axis=0)
            return carry

        lax.fori_loop(0, nl, lane_chunk, 0)

        @pl.when(i == nt - 1)
        def _():
            for k in range(CONV_WIDTH):
                dw_ref[k:k + 1, :] = jnp.sum(dwp_ref[k], axis=0, keepdims=True)
            dw_ref[CONV_WIDTH:, :] = jnp.zeros((CONV_HALO - CONV_WIDTH, D), F32)
        dg = dg_ref[...]
        ha = ha_ref[...].astype(F32)
        sg = _sigmoid(hg_ref[...].astype(F32))
        d_ha = dg * sg
        d_hg = dg * ha * sg * (1.0 - sg)
        dh_ref[:, 0:D] = d_ha.astype(dh_ref.dtype)
        dh_ref[:, D:] = d_hg.astype(dh_ref.dtype)
        _init_or_add(dbdw_ref, i, jnp.sum(dc_ref[...], axis=0, keepdims=True))
        _init_or_add(dbin_ref, i, jnp.concatenate([jnp.sum(d_ha, axis=0, keepdims=True),
                                                   jnp.sum(d_hg, axis=0, keepdims=True)], axis=1))

    tile = pl.BlockSpec((tm, D), lambda i: (i, 0))
    return _call(
        body, comm, name="dwconv_bwd", grid=(nt,),
        in_specs=[tile,
                  pl.BlockSpec((CONV_HALO, D), lambda i: (jnp.minimum((i + 1) * (tm // CONV_HALO), last), 0)),
                  tile, _halo_before_spec(tm, D),
                  tile, tile, pl.BlockSpec((CONV_HALO, D), lambda i: (0, 0))],
        out_specs=[pl.BlockSpec((tm, 2 * D), lambda i: (i, 0)), pl.BlockSpec((CONV_HALO, D), lambda i: (0, 0)),
                   pl.BlockSpec((1, D), lambda i: (0, 0)), pl.BlockSpec((1, 2 * D), lambda i: (0, 0))],
        out_shape=[_sds((T, 2 * D), BF16), _sds((CONV_HALO, D), F32), _sds((1, D), F32), _sds((1, 2 * D), F32)],
        scratch_shapes=[pltpu.VMEM((tm + CONV_HALO, D), F32), pltpu.VMEM((tm + CONV_HALO, D), F32),
                        pltpu.VMEM((tm, D), F32), pltpu.VMEM((CONV_WIDTH, SUBLANES, D), F32),
                        pltpu.VMEM((rb + CONV_HALO, LANES), F32)],
        sem=("arbitrary",), operands=[dc, dc, g, g, ha, hg, w_dw])


def _attn_specs(HD, W):
    B = ATT_BLOCK
    return [pl.BlockSpec((B, HD), lambda n: (n, 0)),
            pl.BlockSpec((B, 4 * W), lambda n: (n, 0)),
            pl.BlockSpec((B, 4 * W), lambda n: (jnp.maximum(n - 1, 0), 0))]


def _band_mask(n):
    r = lax.broadcasted_iota(jnp.int32, (ATT_BLOCK, 2 * ATT_BLOCK), 0)
    j = lax.broadcasted_iota(jnp.int32, (ATT_BLOCK, 2 * ATT_BLOCK), 1)
    return (j > r) & (j <= r + ATT_BLOCK) & ((n > 0) | (j >= ATT_BLOCK))


def _band(kvc_ref, kvp_ref, part, g, parity, W):
    lanes = slice((2 * part + parity) * W + g * LANES, (2 * part + parity) * W + (g + 1) * LANES)
    return jnp.concatenate([kvp_ref[:, lanes], kvc_ref[:, lanes]], axis=0)


def _half_mask(parity):
    lane = lax.broadcasted_iota(jnp.int32, (1, LANES), 1)
    return (lane < HEAD_DIM) if parity == 0 else (lane >= HEAD_DIM)


def widen_kv(k, v, out_ref, n_kv):
    W = n_kv * LANES
    low = _half_mask(0)
    for part, src in enumerate((k, v)):
        for cg in range(n_kv * HEAD_DIM // LANES):
            chunk = src[:, cg * LANES:(cg + 1) * LANES]
            swapped = pltpu.roll(chunk, HEAD_DIM, 1)
            for g, lo, hi in ((2 * cg, chunk, swapped), (2 * cg + 1, swapped, chunk)):
                base = 2 * part * W + g * LANES
                out_ref[:, base:base + LANES] = jnp.where(low, lo, 0.0).astype(out_ref.dtype)
                out_ref[:, base + W:base + W + LANES] = jnp.where(low, 0.0, hi).astype(out_ref.dtype)


def attn_fwd(q, kvx, sinks, n_heads, n_kv, comm=None):
    T, HD = q.shape
    W = n_kv * LANES
    B = ATT_BLOCK
    chunks_per_group = n_heads // n_kv // 2
    scale = 1.0 / math.sqrt(HEAD_DIM)

    def body(q_ref, kvc_ref, kvp_ref, sink_ref, o_ref, lse_ref):
        n = pl.program_id(0)
        mask = jnp.tile(_band_mask(n), (n_heads, 1))
        s = jnp.concatenate(
            [_dot(q_ref[:, (h // 2) * LANES:(h // 2 + 1) * LANES],
                  _band(kvc_ref, kvp_ref, 0, h // 2 // chunks_per_group, h % 2, W), ((1,), (1,)))
             for h in range(n_heads)], axis=0)
        sink = jnp.concatenate([jnp.broadcast_to(sink_ref[:, h:h + 1], (B, 1)) for h in range(n_heads)], axis=0)
        s = jnp.where(mask, s * scale, MASK_VALUE)
        m = jnp.maximum(jnp.max(s, axis=-1, keepdims=True), sink)
        e = jnp.exp(s - m)
        total = _dot(e, jnp.ones((2 * B, LANES), BF16), ((1,), (0,))) + jnp.exp(sink - m)
        lse = m + jnp.log(total[:, 0:1])
        inv = 1.0 / total
        probs = (e * jnp.concatenate([inv, inv], axis=1)).astype(BF16)
        for c in range(n_heads // 2):
            g = c // chunks_per_group
            out = (_dot(probs[2 * c * B:(2 * c + 1) * B], _band(kvc_ref, kvp_ref, 1, g, 0, W), ((1,), (0,)))
                   + _dot(probs[(2 * c + 1) * B:(2 * c + 2) * B], _band(kvc_ref, kvp_ref, 1, g, 1, W), ((1,), (0,))))
            o_ref[:, c * LANES:(c + 1) * LANES] = out.astype(o_ref.dtype)
        lse_ref[...] = jnp.concatenate([lse[h * B:(h + 1) * B] for h in range(n_heads)], axis=1)

    return _call(
        body, comm, name="attn_fwd", grid=(T // B,),
        in_specs=_attn_specs(HD, W) + [pl.BlockSpec((1, n_heads), lambda n: (0, 0))],
        out_specs=[pl.BlockSpec((B, HD), lambda n: (n, 0)), pl.BlockSpec((B, n_heads), lambda n: (n, 0))],
        out_shape=[_sds((T, HD), BF16), _sds((T, n_heads), F32)],
        sem=("parallel",), operands=[q, kvx, kvx, sinks])


def attn_bwd(q, kvx, do, lse, sinks, rope, n_heads, n_kv, comm=None):
    T, HD = q.shape
    KVD = n_kv * HEAD_DIM
    W = n_kv * LANES
    B = ATT_BLOCK
    chunks_per_group = n_heads // n_kv // 2
    scale = 1.0 / math.sqrt(HEAD_DIM)
    nb = T // B

    def body(q_ref, kvc_ref, kvp_ref, do_ref, lse_ref, sink_ref, c_ref, su_ref, sd_ref, dq_ref, dkv_ref, dsink_ref):
        n = pl.program_id(0)

        @pl.when(n == 0)
        def _():
            dkv_ref[...] = jnp.zeros_like(dkv_ref)
            dsink_ref[...] = jnp.zeros_like(dsink_ref)

        def chunk(ref, h):
            return ref[:, (h // 2) * LANES:(h // 2 + 1) * LANES]

        def band(part, h):
            return _band(kvc_ref, kvp_ref, part, h // 2 // chunks_per_group, h % 2, W)

        def stack(per_head):
            return jnp.concatenate([per_head(h) for h in range(n_heads)], axis=0)

        mask = jnp.tile(_band_mask(n), (n_heads, 1))
        s = stack(lambda h: _dot(chunk(q_ref, h), band(0, h), ((1,), (1,))))
        dp = stack(lambda h: _dot(chunk(do_ref, h), band(1, h), ((1,), (1,))))
        lse = stack(lambda h: lse_ref[:, h:h + 1])
        sink = stack(lambda h: jnp.broadcast_to(sink_ref[:, h:h + 1], (B, 1)))
        probs = jnp.exp(jnp.where(mask, s * scale, MASK_VALUE) - lse)
        delta = jnp.sum(probs * dp, axis=-1, keepdims=True)
        ds = (probs * (dp - delta) * scale).astype(BF16)
        probs = probs.astype(BF16)
        sink_term = jnp.exp(sink - lse) * delta
        dsk = [-jnp.sum(sink_term[h * B:(h + 1) * B], axis=0, keepdims=True) for h in range(n_heads)]

        dk_wide, dv_wide = [None] * n_kv, [None] * n_kv
        for c in range(n_heads // 2):
            g = c // chunks_per_group
            dq2 = None
            for h in (2 * c, 2 * c + 1):
                half = _half_mask(h % 2)
                q2, do2 = chunk(q_ref, h), chunk(do_ref, h)
                ds_h, p_h = ds[h * B:(h + 1) * B], probs[h * B:(h + 1) * B]
                part = _dot(ds_h, band(0, h), ((1,), (0,)))
                dq2 = part if dq2 is None else dq2 + part
                dk_h = _dot(ds_h, jnp.where(half, q2, jnp.zeros_like(q2)), ((0,), (0,)))
                dv_h = _dot(p_h, jnp.where(half, do2, jnp.zeros_like(do2)), ((0,), (0,)))
                dk_wide[g] = dk_h if dk_wide[g] is None else dk_wide[g] + dk_h
                dv_wide[g] = dv_h if dv_wide[g] is None else dv_wide[g] + dv_h
            dq_ref[:, c * LANES:(c + 1) * LANES] = _rope_chunk_bwd(
                dq2, c_ref[...], su_ref[...], sd_ref[...]).astype(dq_ref.dtype)

        def fold(wide):
            low = _half_mask(0)
            both = [w + pltpu.roll(w, HEAD_DIM, 1) for w in wide]
            return jnp.concatenate([jnp.where(low, both[2 * cg], both[2 * cg + 1]) for cg in range(n_kv // 2)], axis=1)

        dkv = jnp.concatenate([fold(dk_wide), fold(dv_wide)], axis=1)
        prev = pl.ds(pl.multiple_of(jnp.maximum(n - 1, 0) * B, B), B)
        cur = pl.ds(pl.multiple_of(n * B, B), B)
        dkv_ref[prev, :] += dkv[0:B, :]
        dkv_ref[cur, :] += dkv[B:, :]
        dsink_ref[...] += jnp.concatenate(dsk, axis=1)

    tab = pl.BlockSpec((B, LANES), lambda n: (n, 0))
    return _call(
        body, comm, name="attn_bwd", grid=(nb,),
        in_specs=_attn_specs(HD, W) + [pl.BlockSpec((B, HD), lambda n: (n, 0)),
                                       pl.BlockSpec((B, n_heads), lambda n: (n, 0)),
                                       pl.BlockSpec((1, n_heads), lambda n: (0, 0)), tab, tab, tab],
        out_specs=[pl.BlockSpec((B, HD), lambda n: (n, 0)), pl.BlockSpec((T, 2 * KVD), lambda n: (0, 0)),
                   pl.BlockSpec((1, n_heads), lambda n: (0, 0))],
        out_shape=[_sds((T, HD), BF16), _sds((T, 2 * KVD), F32), _sds((1, n_heads), F32)],
        sem=("arbitrary",), operands=[q, kvx, kvx, do, lse, sinks, *rope])


def chip_sums(name, pairs):
    n = len(pairs)

    def body(core_ref, *refs):
        del core_ref
        for k in range(n):
            refs[2 * n + k][...] = (refs[2 * k][...].astype(F32) + refs[2 * k + 1][...].astype(F32)
                                    ).astype(refs[2 * n + k].dtype)

    in_specs, out_specs, out_shape, operands = [], [], [], []
    for g, p_sib in pairs:
        _, a, b = g.shape
        in_specs += [pl.BlockSpec((None, None, a, b), lambda q, core: (q, core[0], 0, 0)),
                     pl.BlockSpec((None, a, b), lambda q, core: (q, 0, 0))]
        out_specs.append(pl.BlockSpec((None, a, b), lambda q, core: (q, 0, 0)))
        out_shape.append(_sds((N_CHIPS, a, b), g.dtype))
        operands += [g.reshape(N_CHIPS, 2, a, b), p_sib]
    my_core = lax.axis_index("c").astype(jnp.int32).reshape(1)
    return pl.pallas_call(
        body, name=name, out_shape=out_shape,
        grid_spec=pltpu.PrefetchScalarGridSpec(num_scalar_prefetch=1, grid=(N_CHIPS,), in_specs=in_specs,
                                               out_specs=out_specs),
        compiler_params=_params(("arbitrary",)))(my_core, *operands)


def adamw(name, recvs, w, m, v, *, ta):
    L, a, b = w.shape
    n_terms = recvs[0].shape[0]
    assert a % ta == 0 and len(recvs) == L
    c1 = 1.0 - ADAM_B1 ** ADAM_STEP
    c2 = 1.0 - ADAM_B2 ** ADAM_STEP

    def body(*refs):
        r_refs = refs[:L]
        w_ref, m_ref, v_ref, g_ref, d_ref, nm_ref, nv_ref = refs[L:]
        layer = pl.program_id(0)
        for l in range(L):
            @pl.when(layer == l)
            def _(r_ref=r_refs[l]):
                g = r_ref[0].astype(F32)
                for s in range(1, n_terms):
                    g = g + r_ref[s].astype(F32)
                nm = ADAM_B1 * m_ref[...] + (1.0 - ADAM_B1) * g
                nv = ADAM_B2 * v_ref[...] + (1.0 - ADAM_B2) * jnp.square(g)
                m_hat = nm / c1
                v_hat = nv / c2
                g_ref[...] = g
                d_ref[...] = -ADAM_LR * (m_hat / (jnp.sqrt(v_hat) + ADAM_EPS) + ADAM_WD * w_ref[...])
                nm_ref[...] = nm
                nv_ref[...] = nv

    blk = pl.BlockSpec((None, ta, b), lambda l, i: (l, i, 0))
    out = _sds((L, a, b), F32)
    r_specs = [pl.BlockSpec((n_terms, ta, b), lambda l, i, ll=ll: (0, jnp.where(l == ll, i, 0), 0)) for ll in range(L)]
    return pl.pallas_call(
        body, name=name, grid=(L, a // ta), in_specs=r_specs + [blk, blk, blk],
        out_specs=[blk, blk, blk, blk], out_shape=[out, out, out, out],
        compiler_params=_params(("arbitrary", "arbitrary")))(*recvs, w, m, v)


def _pack_rows(parts):
    out = []
    for arr, rows in parts:
        arr = arr.reshape(-1, LANES).astype(F32)
        out.append(jnp.pad(arr, ((0, rows - arr.shape[0]), (0, 0))))
    return jnp.concatenate(out, axis=0)


def _small_rows(a, prefix):
    return _pack_rows([(a[prefix + name], rows) for name, rows in SMALL_SHARDED])


def _unpack_small(packed, a):
    out, r0 = {}, 0
    for name, rows in SMALL_SHARDED:
        shape = a[name].shape
        used = math.prod(shape) // LANES
        out[name] = packed[r0:r0 + used].reshape(shape)
        r0 += rows
    return out


def _rep_rows(a, prefix):
    parts = []
    for name in REPLICATED:
        arr = a[prefix + name]
        if arr.size % LANES:
            arr = jnp.pad(arr.reshape(1, -1), ((0, 0), (0, LANES - arr.size % LANES)))
        rows = -(-arr.size // LANES)
        parts.append((arr, -(-rows // SUBLANES) * SUBLANES))
    return _pack_rows(parts)


def _unpack_rep(packed, a):
    out, r0 = {}, 0
    for name in REPLICATED:
        shape = a[name].shape
        size = math.prod(shape)
        rows = -(-size // LANES)
        out[name] = packed[r0:r0 + rows].reshape(-1)[:size].reshape(shape)
        r0 += -(-rows // SUBLANES) * SUBLANES
    return out


def _step(a):
    x = a['x'][0]
    T, D = x.shape
    tgt = a['loss_target'][0]
    p_in = [a['p'][i, 0] for i in range(DEPTH)]
    PLE = p_in[0].shape[1]
    n_heads = a['attn_sinks'].shape[1]
    HD = n_heads * HEAD_DIM
    KVD = a['kv_w_k'].shape[1]
    n_kv = KVD // HEAD_DIM
    F = a['mlp_w_down'].shape[1] * N_DEV
    tm = min(TOKEN_TILE, T)
    tm2 = min(2 * TOKEN_TILE, T)
    tmc = min(TOKEN_TILE, T)
    tw = 512
    alpha = DEEPNORM_ALPHA
    xb, p_b = x, p_in

    def shard3(w):
        return w.reshape((1,) + w.shape) if w.ndim == 2 else w

    def gather(*specs):
        return _GatherJob([shard3(a[nm])[li].astype(BF16) for nm, li in specs])

    (W_in, small_full), = comm_only("gather_first",
                                    [_GatherJob([a['conv_w_in'][0].astype(BF16), _small_rows(a, '')])])
    r0, small = 0, {}
    for name, rows in SMALL_SHARDED:
        small[name] = small_full[:, r0:r0 + rows]
        r0 += rows
    b_in = small['conv_b_in'][:, 0:2 * D // N_DEV // LANES].reshape(1, 2 * D)
    w_dw = jnp.transpose(small['conv_w_dw'], (1, 0, 2)).reshape(CONV_HALO, D)
    b_dw, cln_g, cln_b, b_out = (small[nm][:, 0].reshape(1, D) for nm in
                                 ('conv_b_dw', 'conv_ln_g', 'conv_ln_b', 'conv_b_out'))
    W_up, W_down, W_proj, W_gate = {}, {}, {}, {}
    mix_g, mix_b, mlp_g, mlp_b = a['mix_ln_g'], a['mix_ln_b'], a['mlp_ln_g'], a['mlp_ln_b']
    rope = _rope_tables(T)

    def set_ple_weights(li, g_proj, g_gate):
        W_proj[li] = jnp.transpose(g_proj, (1, 0, 2)).reshape(1, PLE, D)
        W_gate[li] = g_gate.reshape(1, D, D)

    def row(v, i):
        return v[i:i + 1]

    def res_ln_epi(coef):
        def epi(accs, ex, out, i):
            acc = accs[0] if isinstance(accs, list) else accs
            n_ex = len(ex)
            res_ref, g_ref, b_ref = ex[n_ex - 3], ex[n_ex - 2], ex[n_ex - 1]
            z = coef * res_ref[...] + acc
            if n_ex == 4:
                z = z + ex[0][...]
            out[0][...] = z
            xo = _ln_fwd(z, g_ref[...], b_ref[...])
            out[1][...] = xo
            out[2][...] = xo.astype(BF16)
        return epi

    res_ln_outs = [(_sds((T, D), F32), 'tile'), (_sds((T, D), F32), 'tile'), (_sds((T, D), BF16), 'tile')]

    def mlp_fwd(li, xin, xin_b, down_comm=None):
        def up_epi(accs, ex, out, i):
            out[0][...] = accs[0].astype(BF16)
        res = mm_nn(f"mlp_up_{li}", [(xin_b, W_up[li], 0)], [], [(_sds((T, F), BF16), 'tile')], up_epi, tm=tm2,
                    tn=min(1024, F), comm=None if li in W_down else gather(('mlp_w_down', li)))
        if li in W_down:
            u, = res
        else:
            (u,), got_up = res
            W_down[li] = got_up[0].reshape(1, F, D)
        res = mm_nn(f"mlp_down_{li}", [(u, W_down[li], 0)],
                    [(xin, 'tile'), (row(mlp_g, li), 'row'), (row(mlp_b, li), 'row')],
                    res_ln_outs, res_ln_epi(alpha), tm=tm, tn=D, tk=F, comm=down_comm, a_pro=_sq_relu)
        (z, xo, xo_b), got_down = res if down_comm is not None else (res, ())
        return u, z, xo, xo_b, got_down

    def ple_fwd(li, xin, xin_b, with_loss, comm=None):
        def epi(accs, ex, out, i):
            pp, gg = accs
            xo = ex[0][...] + pp * _sigmoid(gg)
            out[1][...] = pp.astype(BF16)
            out[2][...] = gg.astype(BF16)
            if with_loss:
                err = xo - ex[1][...]
                out[0][...] = err * (1.0 / D)
                _init_or_add(out[3], i, jnp.sum(err * err, axis=0, keepdims=True) * (0.5 / D))
            else:
                out[0][...] = xo
                out[3][...] = xo.astype(BF16)
        extras = [(xin, 'tile')] + ([(tgt, 'tile')] if with_loss else [])
        outs = [(_sds((T, D), F32), 'tile'), (_sds((T, D), BF16), 'tile'), (_sds((T, D), BF16), 'tile')]
        outs.append((_sds((1, D), F32), 'rowacc') if with_loss else (_sds((T, D), BF16), 'tile'))
        return mm_nn(f"ple_{li}", [(p_b[li], W_proj[li], 0), (xin_b, W_gate[li], 0)], extras, outs, epi, tm=tm, tn=D,
                     comm=comm)

    assert D // N_DEV == LANES
    (g0, ha0, hg0), got = _glu(xb, W_in, b_in, T, D, tm,
                               gather(('conv_w_out', 0), ('ple_w_proj', 0), ('ple_w_gate', 0)))
    W_out = got[0].reshape(1, D, D)
    set_ple_weights(0, got[1], got[2])
    (c0, s0), (W_up[0],) = dwconv_fwd(g0, w_dw, b_dw, cln_g, cln_b, tm=tmc, comm=gather(('mlp_w_up', 0)))
    z1, x1, x1b = mm_nn("conv_out", [(s0, W_out, 0)],
                        [(b_out, 'row'), (x, 'tile'), (row(mix_g, 0), 'row'), (row(mix_b, 0), 'row')],
                        res_ln_outs, res_ln_epi(alpha), tm=tm, tn=D)
    u0, z2, x2, x2b, got = mlp_fwd(
        0, x1, x1b, down_comm=gather(('attn_w_q', 0), ('kv_w_k', 0), ('kv_w_v', 0), ('attn_w_o', 0),
                                     ('ple_w_proj', 1), ('ple_w_gate', 1)))
    W_qkv = jnp.concatenate([got[0].reshape(D, HD), got[1].reshape(D, KVD), got[2].reshape(D, KVD)], axis=1)[None]
    W_o = got[3].reshape(1, HD, D)
    set_ple_weights(1, got[4], got[5])
    x3, pp0, gg0, x3b = ple_fwd(0, x2, x2b, False)

    def qkv_epi(accs, ex, out, i):
        t = accs[0]
        c, su, sd = ex[0][...], ex[1][...], ex[2][...]
        out[0][...] = _rope_wide(t[:, 0:HD], c, su, sd, _rope_chunk).astype(BF16)
        widen_kv(_rope_wide(t[:, HD:HD + KVD], c, su, sd, _rope_chunk), t[:, HD + KVD:], out[1], n_kv)
    NQ = HD + 2 * KVD
    q1, kvx1 = mm_nn("qkv_rope", [(x3b, W_qkv, 0)], [(t, 'tab') for t in rope],
                     [(_sds((T, HD), BF16), 'rows'), (_sds((T, 4 * n_kv * LANES), BF16), 'rows')], qkv_epi,
                     tm=tm, tn=NQ)
    (o1, lse1), (W_up[1], g_down1) = attn_fwd(q1, kvx1, a['attn_sinks'], n_heads, n_kv,
                                              comm=gather(('mlp_w_up', 1), ('mlp_w_down', 1)))
    W_down[1] = g_down1.reshape(1, F, D)
    z3, x4, x4b = mm_nn("attn_out", [(o1, W_o, 0)], [(x3, 'tile'), (row(mix_g, 1), 'row'), (row(mix_b, 1), 'row')],
                        res_ln_outs, res_ln_epi(alpha), tm=tm, tn=D)
    u1, z4, x5, x5b, _ = mlp_fwd(1, x4, x4b)
    dy, pp1, gg1, loss_row = ple_fwd(1, x5, x5b, True)
    loss_local = jnp.sum(loss_row)

    grads = {}

    def ln_bwd_epi(coef, with_colsum):
        def epi(acc, ex, out, i):
            d_x = acc + coef * ex[0][...]
            dz, dg, db = _ln_bwd(d_x, ex[1][...], ex[2][...])
            out[0][...] = dz
            out[1][...] = dz.astype(BF16)
            _init_or_add(out[2], i, dg)
            _init_or_add(out[3], i, db)
            if with_colsum:
                _init_or_add(out[4], i, jnp.sum(dz, axis=0, keepdims=True))
        return epi

    def ln_bwd_outs(with_colsum):
        outs = [(_sds((T, D), F32), 'tile'), (_sds((T, D), BF16), 'tile'), (_sds((1, D), F32), 'rowacc'),
                (_sds((1, D), F32), 'rowacc')]
        return outs + ([(_sds((1, D), F32), 'rowacc')] if with_colsum else [])

    def ple_bwd(li, d_out, xin, pp, gg, z_mlp, pair_specs=None, chip_keys=None):
        side = {}

        def gate_grads(ex):
            d = ex[0][...]
            sg = _sigmoid(ex[4][...].astype(F32))
            side['d_pp'] = (d * sg).astype(BF16)
            side['d_gg'] = (d * ex[3][...].astype(F32) * sg * (1.0 - sg)).astype(BF16)
            return side['d_gg']

        jobs = None
        if pair_specs:
            jobs = _Jobs([pair_stage(*pair_specs)] + ([chip_stage(*chip_keys)] if chip_keys else []))
        res = mm_nt(f"ple_dx_{li}", _sds((T, D), BF16), W_gate[li],
                    [(d_out, 'tile'), (z_mlp, 'tile'), (row(mlp_g, li), 'row'), (pp, 'tile'), (gg, 'tile')],
                    ln_bwd_outs(False), ln_bwd_epi(1.0, False), tm=tm, tko=D, tc=D, comm=jobs, a_pro=gate_grads,
                    dws=[(xin, lambda: side['d_gg']), (p_b[li], lambda: side['d_pp'])])
        if jobs is not None:
            res = split_hosted(res, jobs, chips=[1] if chip_keys else [], pairs=[0])
        dz, dzb, dg, db, dw_gate, dw_proj = res
        grads[('mlp_ln_g', li)], grads[('mlp_ln_b', li)] = dg, db
        grads[('ple_w_gate', li)], grads[('ple_w_proj', li)] = dw_gate, dw_proj
        return dz, dzb

    recv = {}
    wqkv_cols = {'attn_w_q': (0, HD), 'kv_w_k': (HD, HD + KVD), 'kv_w_v': (HD + KVD, NQ)}

    def piece(name, li):
        if name == 'conv_w_in':
            return grads['conv_w_in']
        if name == 'mlp_w_up':
            return grads[('mlp_w_up', li)]
        if name == 'ple_w_proj':
            return jnp.transpose(grads[('ple_w_proj', li)][0].reshape(PLE, N_DEV, D // N_DEV), (1, 0, 2))
        if name in wqkv_cols:
            g = grads['w_qkv'][:, wqkv_cols[name][0]:wqkv_cols[name][1]]
        else:
            g = grads[name] if name in grads else grads[(name, li)]
            g = g[0]
        return g.reshape(N_DEV, g.shape[0] // N_DEV, g.shape[1])

    def pair_stage(*specs):
        job = _PairJob([piece(nm, li) for nm, li in specs])
        job.specs = specs
        return job

    sums = {}

    def pair_done(pair_job, got):
        name = "chip_sum_" + "_".join(f"{nm}_{li}" for nm, li in pair_job.specs)
        for spec, total in zip(pair_job.specs, chip_sums(name, list(zip(pair_job.sources, got)))):
            sums[spec] = total

    def chip_stage(*keys):
        job = _ChipJob([sums[k] for k in keys])
        job.specs = keys
        return job

    def hosted(res, job):
        main, got = res
        for spec, r in zip(job.specs, got):
            recv[spec] = r
        return main

    def split_hosted(res, jobs, chips, pairs):
        main, got = res
        parts = jobs.split(got)
        for k in chips:
            hosted((None, parts[k]), jobs.jobs[k])
        for k in pairs:
            pair_done(jobs.jobs[k], parts[k])
        return main

    def mlp_bwd(li, dz, dzb, xin, u, z_mix, with_colsum, du_pairs, du_chips, dx_chips=None):
        jobs = _Jobs([pair_stage(*du_pairs)] + ([chip_stage(*du_chips)] if du_chips else []))
        du, grads[('mlp_w_up', li)], grads[('mlp_w_down', li)] = split_hosted(
            mlp_du_dw(f"mlp_du_{li}", dzb, W_down[li], u, xin, F // N_DEV, tm=tm2, tf=min(1024, F), comm=jobs),
            jobs, chips=[1] if du_chips else [], pairs=[0])
        jobs = _Jobs([pair_stage(('mlp_w_up', li), ('mlp_w_down', li))] + ([chip_stage(*dx_chips)] if dx_chips else []))
        res = split_hosted(
            mm_nt(f"mlp_dx_{li}", du, W_up[li], [(dz, 'tile'), (z_mix, 'tile'), (row(mix_g, li), 'row')],
                  ln_bwd_outs(with_colsum), ln_bwd_epi(alpha, with_colsum), tm=tm, tko=D, tc=F, comm=jobs),
            jobs, chips=[1] if dx_chips else [], pairs=[0])
        grads[('mix_ln_g', li)], grads[('mix_ln_b', li)] = res[2], res[3]
        return res

    dz4, dz4b = ple_bwd(1, dy, x5b, pp1, gg1, z4)
    dz3, dz3b, _, _ = mlp_bwd(1, dz4, dz4b, x4b, u1, z3, False, [('ple_w_gate', 1), ('ple_w_proj', 1)], None)

    def do_epi(acc, ex, out, i):
        out[0][...] = acc.astype(BF16)
    do1, grads['attn_w_o'] = mm_nt("attn_do", dz3b, W_o, [], [(_sds((T, HD), BF16), 'tile')], do_epi, tm=tm, tko=HD,
                                   tc=D, dws=[(o1, None)])
    jobs = _Jobs([chip_stage(('mlp_w_up', 1)), pair_stage(('attn_w_o', 0)),
                  chip_stage(('ple_w_gate', 1), ('ple_w_proj', 1))])
    dq, dkv, d_sinks = split_hosted(
        attn_bwd(q1, kvx1, do1, lse1, a['attn_sinks'], rope, n_heads, n_kv, comm=jobs), jobs, chips=[0, 2], pairs=[1])

    def d_qkv_tile(ex):
        dk = _rope_wide(ex[2][:, 0:KVD], ex[3][...], ex[4][...], ex[5][...], _rope_chunk_bwd)
        return jnp.concatenate([ex[1][...], dk.astype(BF16), ex[2][:, KVD:].astype(BF16)], axis=1)

    def dx3_epi(acc, ex, out, i):
        out[0][...] = acc + alpha * ex[0][...]
    dx3, dw_qkv = mm_nt("attn_dx", _sds((T, NQ), BF16), W_qkv,
                        [(dz3, 'tile'), (dq, 'rows'), (dkv, 'rows')] + [(t, 'tab') for t in rope],
                        [(_sds((T, D), F32), 'tile')], dx3_epi, tm=tm, tko=D, tc=NQ, a_pro=d_qkv_tile,
                        dws=[(x3b, None)])
    grads['w_qkv'] = dw_qkv[0]

    dz2, dz2b = ple_bwd(0, dx3, x2b, pp0, gg0, z2)
    dz1, dz1b, _, _, db_out = mlp_bwd(
        0, dz2, dz2b, x1b, u0, z1, True,
        [('ple_w_gate', 0), ('ple_w_proj', 0), ('attn_w_q', 0), ('kv_w_k', 0), ('kv_w_v', 0)],
        [('mlp_w_down', 1), ('attn_w_o', 0)],
        dx_chips=[('ple_w_gate', 0), ('ple_w_proj', 0), ('attn_w_q', 0), ('kv_w_k', 0), ('kv_w_v', 0)])

    def ds_epi(acc, ex, out, i):
        n = _ln_fwd(ex[0][...], ex[1][...], ex[2][...])
        sg = _sigmoid(n)
        dn = acc * (sg * (1.0 + n * (1.0 - sg)))
        dc, dg, db = _ln_bwd(dn, ex[0][...], ex[1][...])
        out[0][...] = dc
        _init_or_add(out[1], i, dg)
        _init_or_add(out[2], i, db)
    dc0, d_cln_g, d_cln_b, grads['conv_w_out'] = mm_nt(
        "conv_ds", dz1b, W_out, [(c0, 'tile'), (cln_g, 'row'), (cln_b, 'row')],
        [(_sds((T, D), F32), 'tile'), (_sds((1, D), F32), 'rowacc'), (_sds((1, D), F32), 'rowacc')], ds_epi,
        tm=tm, tko=D, tc=D, dws=[(s0, None)])
    job = chip_stage(('mlp_w_up', 0), ('mlp_w_down', 0))
    dh0, d_wdw, d_bdw, d_bin = hosted(dwconv_bwd(dc0, g0, ha0, hg0, w_dw, tm=tmc, comm=job), job)
    grads['conv_w_in'] = mm_tn("d_conv_win", xb, dh0, 2 * D // N_DEV, BF16, tm=T, tk=tw, tn=tw)

    pair = pair_stage(('conv_w_in', 0), ('conv_w_out', 0))
    pair_done(pair, comm_only("pair_last", [pair])[0])
    last_chip = chip_stage(('conv_w_in', 0), ('conv_w_out', 0))

    def own_rows(vec, rows_used, rows):
        arr = vec.reshape(N_DEV, rows_used, LANES)
        return jnp.pad(arr, ((0, 0), (0, rows - rows_used), (0, 0)))
    dwdw_dev = jnp.transpose(d_wdw.reshape(CONV_HALO, N_DEV, D // N_DEV), (1, 0, 2))
    lane_rows = D // N_DEV // LANES
    small_grad = jnp.concatenate([
        own_rows(d_bin, 2 * lane_rows, 8), dwdw_dev if lane_rows == 1 else dwdw_dev.reshape(N_DEV, -1, LANES),
        own_rows(d_bdw, lane_rows, 8), own_rows(d_cln_g, lane_rows, 8), own_rows(d_cln_b, lane_rows, 8),
        own_rows(db_out, lane_rows, 8)], axis=1)
    n_small = small_grad.shape[1]

    rep_local = {'mix_ln_g': jnp.concatenate([grads[('mix_ln_g', li)] for li in range(DEPTH)], axis=0),
                 'mix_ln_b': jnp.concatenate([grads[('mix_ln_b', li)] for li in range(DEPTH)], axis=0),
                 'mlp_ln_g': jnp.concatenate([grads[('mlp_ln_g', li)] for li in range(DEPTH)], axis=0),
                 'mlp_ln_b': jnp.concatenate([grads[('mlp_ln_b', li)] for li in range(DEPTH)], axis=0),
                 'attn_sinks': d_sinks}
    rep_grad = _rep_rows(rep_local, '')
    n_rep = rep_grad.shape[0]
    last = _Jobs([last_chip, _DirectJob([small_grad, jnp.broadcast_to(rep_grad[None], (N_DEV, n_rep, LANES))])])

    def dx_epi(acc, ex, out, i):
        out[0][...] = acc + alpha * ex[0][...]
    (grad_x,), got = mm_nt("conv_dx", dh0, W_in, [(dz1, 'tile')], [(_sds((T, D), F32), 'tile')], dx_epi,
                           tm=tm, tko=D, tc=D, comm=last)
    got, (recv_small, recv_rep) = last.split(got)
    hosted((None, got), last_chip)

    result = {}
    kinds = ('grad', 'delta', 'new_m', 'new_v')
    w, m, v = (_small_rows(a, pre)[None] for pre in ('', 'm_', 'v_'))
    for kind, arr in zip(kinds, adamw("adamw_small", [recv_small], w, m, v, ta=n_small)):
        for pname, val in _unpack_small(arr[0], a).items():
            result[(kind, pname)] = val
    w, m, v = (_rep_rows(a, pre)[None] for pre in ('', 'm_', 'v_'))
    for kind, arr in zip(kinds, adamw("adamw_rep", [recv_rep], w, m, v, ta=n_rep)):
        for pname, val in _unpack_rep(arr[0], a).items():
            result[(kind, pname)] = val
    for name in BIG_WEIGHTS:
        w, m, v = (shard3(a[pre + name]) for pre in ('', 'm_', 'v_'))
        recvs = [recv[(name, li)] for li in range(w.shape[0])]
        for kind, arr in zip(kinds, adamw("adamw_" + name, recvs, w, m, v, ta=min(256, w.shape[1]))):
            result[(kind, name)] = arr.reshape(a[name].shape)

    loss = lax.psum(loss_local, ("x", "y", "c"))
    out = [loss, grad_x[None]]
    for kind in ('grad', 'delta', 'new_m', 'new_v'):
        out += [result[(kind, name)] for name in WEIGHT_NAMES]
    return tuple(out)


def _glu(x, W_in, b_in, T, D, tm, comm=None):
    n = W_in.shape[2]
    q = 2 if D // n % 2 == 0 else 1
    nt = D // (q * n)
    tn = q * n

    def body(x_ref, wa_ref, wg_ref, ba_ref, bg_ref, g_ref, ha_ref, hg_ref):
        xb = x_ref[...].astype(BF16)
        ha = jnp.concatenate([_dot(xb, wa_ref[s], ((1,), (0,))) for s in range(q)], axis=1) + ba_ref[...]
        hg = jnp.concatenate([_dot(xb, wg_ref[s], ((1,), (0,))) for s in range(q)], axis=1) + bg_ref[...]
        g_ref[...] = ha * _sigmoid(hg)
        ha_ref[...] = ha.astype(ha_ref.dtype)
        hg_ref[...] = hg.astype(hg_ref.dtype)

    return _call(
        body, comm, name="conv_in_glu", grid=(T // tm, nt),
        in_specs=[pl.BlockSpec((tm, D), lambda i, j: (i, 0)),
                  pl.BlockSpec((q, D, n), lambda i, j: (j, 0, 0)),
                  pl.BlockSpec((q, D, n), lambda i, j: (j + nt, 0, 0)),
                  pl.BlockSpec((1, tn), lambda i, j: (0, j)), pl.BlockSpec((1, tn), lambda i, j: (0, j + nt))],
        out_specs=[pl.BlockSpec((tm, tn), lambda i, j: (i, j))] * 3,
        out_shape=[_sds((T, D), F32), _sds((T, D), BF16), _sds((T, D), BF16)],
        sem=("parallel", "parallel"), operands=[x, W_in, W_in, b_in, b_in])


def kernel(x, p, conv_w_in, conv_b_in, conv_w_dw, conv_b_dw, conv_ln_g, conv_ln_b, conv_w_out, conv_b_out, kv_w_k, kv_w_v, attn_w_q, attn_sinks, attn_w_o, mix_ln_g, mix_ln_b, mlp_w_up, mlp_w_down, mlp_ln_g, mlp_ln_b, ple_w_proj, ple_w_gate, loss_target, m_conv_w_in, m_conv_b_in, m_conv_w_dw, m_conv_b_dw, m_conv_ln_g, m_conv_ln_b, m_conv_w_out, m_conv_b_out, m_kv_w_k, m_kv_w_v, m_attn_w_q, m_attn_sinks, m_attn_w_o, m_mix_ln_g, m_mix_ln_b, m_mlp_w_up, m_mlp_w_down, m_mlp_ln_g, m_mlp_ln_b, m_ple_w_proj, m_ple_w_gate, v_conv_w_in, v_conv_b_in, v_conv_w_dw, v_conv_b_dw, v_conv_ln_g, v_conv_ln_b, v_conv_w_out, v_conv_b_out, v_kv_w_k, v_kv_w_v, v_attn_w_q, v_attn_sinks, v_attn_w_o, v_mix_ln_g, v_mix_ln_b, v_mlp_w_up, v_mlp_w_down, v_mlp_ln_g, v_mlp_ln_b, v_ple_w_proj, v_ple_w_gate):
    return _step(dict(locals()))
```

```python
import functools
import math

import jax
import jax.numpy as jnp
from jax import lax
from jax.experimental import pallas as pl
from jax.experimental.pallas import tpu as pltpu

F32 = jnp.float32
BF16 = jnp.bfloat16

N_DEV = 8
HEAD_DIM = 64
ROPE_DIM = HEAD_DIM // 4
ROPE_HALF = ROPE_DIM // 2
ROPE_THETA = 500000.0
ATT_BLOCK = 128
CONV_WIDTH = 31
CONV_HALO = 32
CONV_ROWS = 64
LN_EPS = 1e-5
DEPTH = 2
DEEPNORM_ALPHA = (2 * DEPTH) ** 0.25
MASK_VALUE = -1e30

ADAM_LR = 0.001
ADAM_B1 = 0.9
ADAM_B2 = 0.999
ADAM_EPS = 1e-08
ADAM_WD = 0.01
ADAM_STEP = 10

LANES = 128
SUBLANES = 8
VMEM_LIMIT_BYTES = 52 * 1024 * 1024
TOKEN_TILE = 512
MESH_ID = pl.DeviceIdType.MESH
RELAY_AT_TENTHS = 6
FORWARD_AT_TENTHS = 9

WEIGHT_NAMES = ['conv_w_in', 'conv_b_in', 'conv_w_dw', 'conv_b_dw', 'conv_ln_g', 'conv_ln_b', 'conv_w_out',
                'conv_b_out', 'kv_w_k', 'kv_w_v', 'attn_w_q', 'attn_sinks', 'attn_w_o', 'mix_ln_g', 'mix_ln_b',
                'mlp_w_up', 'mlp_w_down', 'mlp_ln_g', 'mlp_ln_b', 'ple_w_proj', 'ple_w_gate']
BIG_WEIGHTS = ['conv_w_in', 'conv_w_out', 'kv_w_k', 'kv_w_v', 'attn_w_q', 'attn_w_o', 'mlp_w_up', 'mlp_w_down',
               'ple_w_proj', 'ple_w_gate']
SMALL_SHARDED = [('conv_b_in', 8), ('conv_w_dw', 32), ('conv_b_dw', 8), ('conv_ln_g', 8), ('conv_ln_b', 8),
                 ('conv_b_out', 8)]
REPLICATED = ['mix_ln_g', 'mix_ln_b', 'mlp_ln_g', 'mlp_ln_b', 'attn_sinks']


def _params(sem):
    return pltpu.CompilerParams(dimension_semantics=sem, vmem_limit_bytes=VMEM_LIMIT_BYTES)


def _sds(shape, dtype):
    return jax.ShapeDtypeStruct(shape, dtype)


def _my_place():
    x, y, c = lax.axis_index("x"), lax.axis_index("y"), lax.axis_index("c")
    return x, y, c, 4 * x + 2 * y + c


def _peers(x, y, c):
    out = []
    for dx in (0, 1):
        for dy in (0, 1):
            for dc in (0, 1):
                if dx or dy or dc:
                    px, py, pc = x ^ dx, y ^ dy, c ^ dc
                    out.append(((px, py, pc), 4 * px + 2 * py + pc))
    return out


N_CHIPS = 4


def _other_chips(x, y):
    return [((x ^ dx, y ^ dy), 2 * (x ^ dx) + (y ^ dy)) for dx, dy in ((1, 0), (0, 1), (1, 1))]


def _remote(src, dst, send, recv, to):
    return pltpu.make_async_remote_copy(src_ref=src, dst_ref=dst, send_sem=send, recv_sem=recv, device_id=to,
                                        device_id_type=MESH_ID)


def _wait_slabs(buf, count, send, recv, me, sent=True, received=True):
    part = buf.at[pl.ds(0, count)]
    cp = _remote(part, part, send, recv, me)
    if sent:
        cp.wait_send()
    if received:
        cp.wait_recv()


class _DirectJob:
    n_sems = 3

    def __init__(self, items):
        self.sources = list(items)
        self.dests = [_sds(it.shape, it.dtype) for it in items]
        self.n = len(items)

    def start(self, src, dst, send, recv, loc):
        x, y, c, me = _my_place()
        for k in range(self.n):
            here = dst[k].at[pl.ds(me, 1)]
            pltpu.make_async_copy(src[k].at[pl.ds(me, 1)], here, loc.at[k]).start()
            for peer, idx in _peers(x, y, c):
                _remote(src[k].at[pl.ds(idx, 1)], here, send.at[k], recv.at[k], peer).start()

    def relay(self, *refs):
        pass

    def forward(self, *refs):
        pass

    def finish(self, src, dst, send, recv, loc):
        x, y, c, me = _my_place()
        for k in range(self.n):
            pltpu.make_async_copy(src[k].at[pl.ds(me, 1)], dst[k].at[pl.ds(me, 1)], loc.at[k]).wait()
            _wait_slabs(dst[k], N_DEV - 1, send.at[k], recv.at[k], (x, y, c))


class _GatherJob:
    n_sems = 7

    def __init__(self, items):
        self.sources = [it.reshape((1,) + it.shape) for it in items]
        self.dests = [_sds((N_DEV,) + it.shape, it.dtype) for it in items]
        self.n = len(items)

    @staticmethod
    def _neighbours(x, y, c):
        out = []
        for dx, dy in ((c, 1 - c), (1 - c, c), (1, 1)):
            px, py = x ^ dx, y ^ dy
            out.append(((px, py, c), 4 * px + 2 * py + c))
        return out

    def start(self, src, dst, send_a, recv_a, send_b, recv_b, send_d, recv_d, loc):
        x, y, c, me = _my_place()
        (first, _), (second, _), _ = self._neighbours(x, y, c)
        for k in range(self.n):
            here = dst[k].at[pl.ds(me, 1)]
            pltpu.make_async_copy(src[k], here, loc.at[k]).start()
            _remote(src[k], here, send_d.at[k], recv_d.at[k], (x, y, 1 - c)).start()
            _remote(src[k], here, send_a.at[k], recv_a.at[k], first).start()
            _remote(src[k], here, send_b.at[k], recv_b.at[k], second).start()

    def relay(self, src, dst, send_a, recv_a, send_b, recv_b, send_d, recv_d, loc):
        x, y, c, me = _my_place()
        (_, first_idx), (second, _), _ = self._neighbours(x, y, c)
        for k in range(self.n):
            _wait_slabs(dst[k], 1, send_a.at[k], recv_a.at[k], (x, y, c), sent=False)
            rows = dst[k].at[pl.ds(first_idx, 1)]
            _remote(rows, rows, send_b.at[k], recv_b.at[k], second).start()

    def forward(self, src, dst, send_a, recv_a, send_b, recv_b, send_d, recv_d, loc):
        x, y, c, me = _my_place()
        for k in range(self.n):
            _wait_slabs(dst[k], 2, send_b.at[k], recv_b.at[k], (x, y, c), sent=False)
            for _, idx in self._neighbours(x, y, c):
                rows = dst[k].at[pl.ds(idx, 1)]
                _remote(rows, rows, send_d.at[k], recv_d.at[k], (x, y, 1 - c)).start()

    def finish(self, src, dst, send_a, recv_a, send_b, recv_b, send_d, recv_d, loc):
        x, y, c, me = _my_place()
        for k in range(self.n):
            pltpu.make_async_copy(src[k], dst[k].at[pl.ds(me, 1)], loc.at[k]).wait()
            _wait_slabs(dst[k], 1, send_a.at[k], recv_a.at[k], (x, y, c), received=False)
            _wait_slabs(dst[k], 2, send_b.at[k], recv_b.at[k], (x, y, c), received=False)
            _wait_slabs(dst[k], N_CHIPS, send_d.at[k], recv_d.at[k], (x, y, c))


class _PairJob:
    n_sems = 2

    def __init__(self, items):
        self.sources = list(items)
        self.dests = [_sds((N_CHIPS,) + it.shape[1:], it.dtype) for it in items]
        self.n = len(items)

    def start(self, src, dst, send, recv):
        x, y, c, me = _my_place()
        for k in range(self.n):
            for q in range(N_CHIPS):
                _remote(src[k].at[pl.ds(2 * q + 1 - c, 1)], dst[k].at[pl.ds(q, 1)], send.at[k], recv.at[k],
                        (x, y, 1 - c)).start()

    def relay(self, *refs):
        pass

    def forward(self, *refs):
        pass

    def finish(self, src, dst, send, recv):
        x, y, c, me = _my_place()
        for k in range(self.n):
            _wait_slabs(dst[k], N_CHIPS, send.at[k], recv.at[k], (x, y, c))


class _ChipJob:
    n_sems = 3

    def __init__(self, items):
        self.sources = list(items)
        self.dests = [_sds(it.shape, it.dtype) for it in items]
        self.n = len(items)

    def start(self, src, dst, send, recv, loc):
        x, y, c, me = _my_place()
        mine = 2 * x + y
        for k in range(self.n):
            here = dst[k].at[pl.ds(mine, 1)]
            pltpu.make_async_copy(src[k].at[pl.ds(mine, 1)], here, loc.at[k]).start()
            for (px, py), q in _other_chips(x, y):
                _remote(src[k].at[pl.ds(q, 1)], here, send.at[k], recv.at[k], (px, py, c)).start()

    def relay(self, *refs):
        pass

    def forward(self, *refs):
        pass

    def finish(self, src, dst, send, recv, loc):
        x, y, c, me = _my_place()
        mine = 2 * x + y
        for k in range(self.n):
            pltpu.make_async_copy(src[k].at[pl.ds(mine, 1)], dst[k].at[pl.ds(mine, 1)], loc.at[k]).wait()
            _wait_slabs(dst[k], N_CHIPS - 1, send.at[k], recv.at[k], (x, y, c))


class _Jobs:
    def __init__(self, jobs):
        self.jobs = list(jobs)
        self.sources = [s for job in self.jobs for s in job.sources]
        self.dests = [d for job in self.jobs for d in job.dests]
        self.n = len(self.sources)

    def _stage(self, stage, src, dst, sems):
        k0 = s0 = 0
        for job in self.jobs:
            getattr(job, stage)(src[k0:k0 + job.n], dst[k0:k0 + job.n], *sems[s0:s0 + job.n_sems])
            k0 += job.n
            s0 += job.n_sems

    def start(self, src, dst, *sems):
        self._stage('start', src, dst, sems)

    def relay(self, src, dst, *sems):
        self._stage('relay', src, dst, sems)

    def forward(self, src, dst, *sems):
        self._stage('forward', src, dst, sems)

    def finish(self, src, dst, *sems):
        self._stage('finish', src, dst, sems)

    def split(self, got):
        out, k0 = [], 0
        for job in self.jobs:
            out.append(got[k0:k0 + job.n])
            k0 += job.n
        return out


def _job_sems(job):
    if isinstance(job, _Jobs):
        return [s for part in job.jobs for s in _job_sems(part)]
    return [pltpu.SemaphoreType.DMA((job.n,))] * job.n_sems


def _call(body, comm, *, name, grid, in_specs, out_specs, out_shape, operands, sem, scratch_shapes=(), aliases=None):
    single = not isinstance(out_shape, (list, tuple))
    out_shape = [out_shape] if single else list(out_shape)
    out_specs = [out_specs] if single else list(out_specs)
    if comm is None:
        res = pl.pallas_call(body, name=name, grid=grid, in_specs=list(in_specs), out_specs=out_specs,
                             out_shape=out_shape, scratch_shapes=list(scratch_shapes),
                             input_output_aliases=aliases or {}, compiler_params=_params(sem))(*operands)
        return res[0] if single else res
    n_in, n_out, n_scr, n_c = len(in_specs), len(out_shape), len(scratch_shapes), comm.n
    any_spec = pl.BlockSpec(memory_space=pl.ANY)
    steps = math.prod(grid)
    early = min(steps - 1, (steps * RELAY_AT_TENTHS) // 10)
    mid = min(steps - 1, (steps * FORWARD_AT_TENTHS) // 10)

    def hosted(*refs):
        ins, c_src = refs[:n_in], refs[n_in:n_in + n_c]
        outs = refs[n_in + n_c:n_in + n_c + n_out]
        c_dst = refs[n_in + n_c + n_out:n_in + 2 * n_c + n_out]
        scr = refs[n_in + 2 * n_c + n_out:n_in + 2 * n_c + n_out + n_scr]
        sems = refs[n_in + 2 * n_c + n_out + n_scr:]
        step = pl.program_id(0)
        for d in range(1, len(grid)):
            step = step * grid[d] + pl.program_id(d)

        @pl.when(step == 0)
        def _():
            comm.start(c_src, c_dst, *sems)

        @pl.when(step == early)
        def _():
            comm.relay(c_src, c_dst, *sems)

        @pl.when(step == mid)
        def _():
            comm.forward(c_src, c_dst, *sems)

        body(*ins, *outs, *scr)

        @pl.when(step == steps - 1)
        def _():
            comm.finish(c_src, c_dst, *sems)

    res = pl.pallas_call(hosted, name=name, grid=grid, in_specs=list(in_specs) + [any_spec] * n_c,
                         out_specs=out_specs + [any_spec] * n_c, out_shape=out_shape + comm.dests,
                         scratch_shapes=list(scratch_shapes) + _job_sems(comm), input_output_aliases=aliases or {},
                         compiler_params=_params(("arbitrary",) * len(grid)))(*operands, *comm.sources)
    main = res[:n_out]
    return (main[0] if single else main), res[n_out:]


def comm_only(name, jobs):
    any_spec = pl.BlockSpec(memory_space=pl.ANY)
    n_all = sum(job.n for job in jobs)

    def body(*refs):
        srcs, dsts, sems = refs[:n_all], refs[n_all:2 * n_all], refs[2 * n_all:]
        parts, k0, s0 = [], 0, 0
        for job in jobs:
            parts.append((job, srcs[k0:k0 + job.n], dsts[k0:k0 + job.n], sems[s0:s0 + job.n_sems]))
            k0 += job.n
            s0 += job.n_sems
        for stage in ('start', 'relay', 'forward', 'finish'):
            for job, src, dst, sem in parts:
                getattr(job, stage)(src, dst, *sem)

    res = pl.pallas_call(body, name=name, in_specs=[any_spec] * n_all, out_specs=[any_spec] * n_all,
                         out_shape=[d for job in jobs for d in job.dests],
                         scratch_shapes=[s for job in jobs for s in _job_sems(job)],
                         )(*[s for job in jobs for s in job.sources])
    out, k0 = [], 0
    for job in jobs:
        out.append(res[k0:k0 + job.n])
        k0 += job.n
    return out


def _dot(a, b, dims):
    if a.dtype != BF16:
        a = a.astype(BF16)
    if b.dtype != BF16:
        b = b.astype(BF16)
    return lax.dot_general(a, b, (dims, ((), ())), preferred_element_type=F32)


def _sigmoid(v):
    return 1.0 / (1.0 + jnp.exp(-v))


def _sq_relu(u):
    return jnp.square(jnp.maximum(u.astype(F32), 0.0)).astype(BF16)


def _ln_stats(z):
    mu = jnp.mean(z, axis=-1, keepdims=True)
    zc = z - mu
    var = jnp.mean(zc * zc, axis=-1, keepdims=True)
    return zc * lax.rsqrt(var + LN_EPS)


def _ln_fwd(z, g, b):
    return _ln_stats(z) * g + b


def _ln_bwd(dy, z, g):
    xhat = _ln_stats(z)
    mu = jnp.mean(z, axis=-1, keepdims=True)
    zc = z - mu
    rstd = lax.rsqrt(jnp.mean(zc * zc, axis=-1, keepdims=True) + LN_EPS)
    dxh = dy * g
    m1 = jnp.mean(dxh, axis=-1, keepdims=True)
    m2 = jnp.mean(dxh * xhat, axis=-1, keepdims=True)
    dz = rstd * (dxh - m1 - xhat * m2)
    return dz, jnp.sum(dy * xhat, axis=0, keepdims=True), jnp.sum(dy, axis=0, keepdims=True)


def _extra_spec(shape, kind, tm, tn, ij):
    if kind == 'tile':
        return pl.BlockSpec((tm, tn), lambda *g: (ij(g)[0], ij(g)[1]))
    if kind in ('row', 'rowacc'):
        return pl.BlockSpec((1, tn), lambda *g: (0, ij(g)[1]))
    if kind == 'tab':
        return pl.BlockSpec((tm, LANES), lambda *g: (ij(g)[0], 0))
    if kind == 'rows':
        return pl.BlockSpec((tm, shape[1]), lambda *g: (ij(g)[0], 0))
    raise ValueError(kind)


def mm_nn(name, pairs, extras, outs, epi, *, tm, tn, tk=None, comm=None, a_pro=None):
    M = pairs[0][0].shape[0]
    N = pairs[0][1].shape[0] * pairs[0][1].shape[2]
    n_pairs = len(pairs)
    K0 = pairs[0][0].shape[1]
    tk = K0 if tk is None else tk
    nk = K0 // tk
    assert nk == 1 or n_pairs == 1
    assert M % tm == 0 and N % tn == 0 and K0 % tk == 0
    has_rowacc = any(kind == 'rowacc' for _, kind in outs)
    assert not has_rowacc or (N == tn and nk == 1)
    in_specs, operands, slabs = [], [], []
    for a, b, off in pairs:
        K = a.shape[1]
        ktile = K if n_pairs > 1 else tk
        n = b.shape[2]
        assert b.shape[1] == K
        in_specs.append(pl.BlockSpec((tm, ktile), lambda i, j, k: (i, k)))
        if tn <= n:
            assert n % tn == 0
            r = n // tn
            in_specs.append(pl.BlockSpec((None, ktile, tn),
                                         lambda i, j, k, r=r, off=off: ((j + off) // r, k, (j + off) % r)))
            slabs.append(0)
        else:
            assert tn % n == 0
            in_specs.append(pl.BlockSpec((tn // n, ktile, n), lambda i, j, k, off=off: (j + off, k, 0)))
            slabs.append(tn // n)
        operands += [a, b]
    ij = lambda g: (g[0], g[1])
    for arr, kind in extras:
        in_specs.append(_extra_spec(arr.shape, kind, tm, tn, ij))
        operands.append(arr)
    out_specs = [_extra_spec(o.shape, kind, tm, tn, ij) for o, kind in outs]
    n_ex, n_out = len(extras), len(outs)

    def pair_dot(ab, q):
        a = ab[2 * q][...]
        if a_pro is not None:
            a = a_pro(a)
        if not slabs[q]:
            return _dot(a, ab[2 * q + 1][...], ((1,), (0,)))
        return jnp.concatenate([_dot(a, ab[2 * q + 1][s], ((1,), (0,))) for s in range(slabs[q])], axis=1)

    def body(*refs):
        ab = refs[:2 * n_pairs]
        ex = refs[2 * n_pairs:2 * n_pairs + n_ex]
        out = refs[2 * n_pairs + n_ex:2 * n_pairs + n_ex + n_out]
        i = pl.program_id(0)
        if nk == 1:
            accs = [pair_dot(ab, q) for q in range(n_pairs)]
            epi(accs, ex, out, i)
        else:
            acc_ref = refs[-1]
            k = pl.program_id(2)

            @pl.when(k == 0)
            def _():
                acc_ref[...] = jnp.zeros_like(acc_ref)

            acc_ref[...] += pair_dot(ab, 0)

            @pl.when(k == nk - 1)
            def _():
                epi([acc_ref[...]], ex, out, i)

    scratch = [pltpu.VMEM((tm, tn), F32)] if nk > 1 else []
    sem = ("arbitrary",) * 3 if has_rowacc else ("parallel", "parallel", "arbitrary")
    return _call(body, comm, name=name, grid=(M // tm, N // tn, nk), in_specs=in_specs, out_specs=out_specs,
                 out_shape=[o for o, _ in outs], scratch_shapes=scratch, sem=sem, operands=operands)


def mm_nt(name, a, b, extras, outs, epi, *, tm, tko, tc, comm=None, a_pro=None, dws=()):
    M, N = a.shape
    J, K, n = b.shape
    assert J * n == N and M % tm == 0 and K % tko == 0 and N % tc == 0
    nc = N // tc
    assert a_pro is None or (nc == 1 and tc <= n and K == tko)
    assert not dws or (nc == 1 and K == tko and tc <= n)
    n_dw, nt = len(dws), M // tm
    has_rowacc = any(kind == 'rowacc' for _, kind in outs)
    assert not has_rowacc or K == tko
    if tc <= n:
        assert n % tc == 0
        r = n // tc
        slabs = 0
        b_spec = pl.BlockSpec((None, tko, tc), lambda i, j, c: (c // r, j, c % r))
    else:
        assert tc % n == 0
        slabs = tc // n
        b_spec = pl.BlockSpec((slabs, tko, n), lambda i, j, c: (c, j, 0))
    n_ab = 2 if a_pro is None else 1
    in_specs = ([pl.BlockSpec((tm, tc), lambda i, j, c: (i, c))] if a_pro is None else []) + [b_spec]

    def nt_dot(a_ref, b_ref):
        if not slabs:
            return _dot(a_ref[...], b_ref[...], ((1,), (1,)))
        acc = _dot(a_ref[:, 0:n], b_ref[0], ((1,), (1,)))
        for s in range(1, slabs):
            acc = acc + _dot(a_ref[:, s * n:(s + 1) * n], b_ref[s], ((1,), (1,)))
        return acc

    ij = lambda g: (g[0], g[1])
    operands = [a, b] if a_pro is None else [b]
    for arr, kind in extras:
        in_specs.append(_extra_spec(arr.shape, kind, tm, tko, ij))
        operands.append(arr)
    out_specs = [_extra_spec(o.shape, kind, tm, tko, ij) for o, kind in outs]
    n_ex, n_out = len(extras), len(outs)
    out_shape = [o for o, _ in outs]
    for lhs, _ in dws:
        in_specs.append(pl.BlockSpec((tm, lhs.shape[1]), lambda i, j, c: (i, 0)))
        operands.append(lhs)
        out_specs.append(pl.BlockSpec((None, lhs.shape[1], N), lambda i, j, c: (0, 0, 0)))
        out_shape.append(_sds((1, lhs.shape[1], N), BF16))

    def weight_grads(refs, a_tile, i):
        lhs_refs = refs[n_ab + n_ex:n_ab + n_ex + n_dw]
        dw_refs = refs[n_ab + n_ex + n_dw + n_out:n_ab + n_ex + n_dw + n_out + n_dw]
        acc_refs = refs[len(refs) - n_dw:]
        for (_, rhs_fn), lhs_ref, dw_ref, acc_ref in zip(dws, lhs_refs, dw_refs, acc_refs):
            part = _dot(lhs_ref[...], a_tile if rhs_fn is None else rhs_fn(), ((0,), (0,)))
            _init_or_add(acc_ref, i, part)

            @pl.when(i == nt - 1)
            def _(dw_ref=dw_ref, acc_ref=acc_ref):
                dw_ref[...] = acc_ref[...].astype(dw_ref.dtype)

    def body(*refs):
        ex = refs[n_ab:n_ab + n_ex]
        out = refs[n_ab + n_ex + n_dw:n_ab + n_ex + n_dw + n_out]
        i = pl.program_id(0)
        if a_pro is not None:
            a_tile = a_pro(ex)
            epi(_dot(a_tile, refs[0][...], ((1,), (1,))), ex, out, i)
            weight_grads(refs, a_tile, i)
            return
        a_ref, b_ref = refs[:2]
        if nc == 1:
            epi(nt_dot(a_ref, b_ref), ex, out, i)
            if dws:
                weight_grads(refs, a_ref[...], i)
        else:
            acc_ref = refs[-1]
            c = pl.program_id(2)

            @pl.when(c == 0)
            def _():
                acc_ref[...] = jnp.zeros_like(acc_ref)

            acc_ref[...] += nt_dot(a_ref, b_ref)

            @pl.when(c == nc - 1)
            def _():
                epi(acc_ref[...], ex, out, i)

    scratch = ([pltpu.VMEM((tm, tko), F32)] if nc > 1 else []) + [pltpu.VMEM((lhs.shape[1], N), F32) for lhs, _ in dws]
    sem = ("arbitrary",) * 3 if (has_rowacc or dws) else ("parallel", "parallel", "arbitrary")
    return _call(body, comm, name=name, grid=(M // tm, K // tko, nc), in_specs=in_specs, out_specs=out_specs,
                 out_shape=out_shape, scratch_shapes=scratch, sem=sem, operands=operands)


def mm_tn(name, a, d, n, out_dtype, *, tm, tk, tn, comm=None):
    M, K = a.shape
    N = d.shape[1]
    assert d.shape[0] == M and N % n == 0 and N % tn == 0 and K % tk == 0 and M % tm == 0
    nm = M // tm
    if tn <= n:
        assert n % tn == 0
        r = n // tn
        slabs = 0
        o_spec = pl.BlockSpec((None, tk, tn), lambda kk, j, m: (j // r, kk, j % r))
    else:
        assert tn % n == 0
        slabs = tn // n
        o_spec = pl.BlockSpec((slabs, tk, n), lambda kk, j, m: (j, kk, 0))

    def write(o_ref, acc):
        if not slabs:
            o_ref[...] = acc.astype(o_ref.dtype)
        else:
            for s in range(slabs):
                o_ref[s] = acc[:, s * n:(s + 1) * n].astype(o_ref.dtype)

    def body(a_ref, d_ref, o_ref, *scratch):
        if nm == 1:
            write(o_ref, _dot(a_ref[...], d_ref[...], ((0,), (0,))))
            return
        acc_ref, = scratch
        m = pl.program_id(2)

        @pl.when(m == 0)
        def _():
            acc_ref[...] = jnp.zeros_like(acc_ref)

        acc_ref[...] += _dot(a_ref[...], d_ref[...], ((0,), (0,)))

        @pl.when(m == nm - 1)
        def _():
            write(o_ref, acc_ref[...])

    return _call(
        body, comm, name=name, grid=(K // tk, N // tn, nm),
        in_specs=[pl.BlockSpec((tm, tk), lambda kk, j, m: (m, kk)), pl.BlockSpec((tm, tn), lambda kk, j, m: (m, j))],
        out_specs=o_spec, out_shape=_sds((N // n, K, n), out_dtype),
        scratch_shapes=[pltpu.VMEM((tk, tn), F32)] if nm > 1 else [],
        sem=("parallel", "parallel", "arbitrary"), operands=[a, d])


def mlp_du_dw(name, dzb, w_down, u, xin, n, *, tm, tf, comm=None):
    T, D = dzb.shape
    F = u.shape[1]
    assert T % tm == 0 and F % tf == 0 and tf % n == 0
    slabs, nt = tf // n, T // tm

    def body(dz_ref, w_ref, u_ref, x_ref, du_ref, dwu_ref, dwd_ref, up_acc, down_acc):
        i = pl.program_id(1)
        dz = dz_ref[...]
        da = _dot(dz, w_ref[...], ((1,), (1,)))
        relu = jnp.maximum(u_ref[...].astype(F32), 0.0)
        du = (da * (2.0 * relu)).astype(BF16)
        du_ref[...] = du
        _init_or_add(up_acc, i, _dot(x_ref[...], du, ((0,), (0,))))
        _init_or_add(down_acc, i, _dot(jnp.square(relu).astype(BF16), dz, ((0,), (0,))))

        @pl.when(i == nt - 1)
        def _():
            for s in range(slabs):
                dwu_ref[s] = up_acc[:, s * n:(s + 1) * n].astype(dwu_ref.dtype)
            dwd_ref[...] = down_acc[...].astype(dwd_ref.dtype)

    hidden = pl.BlockSpec((tm, tf), lambda j, i: (i, j))
    tokens = pl.BlockSpec((tm, D), lambda j, i: (i, 0))
    return _call(
        body, comm, name=name, grid=(F // tf, nt),
        in_specs=[tokens, pl.BlockSpec((None, tf, D), lambda j, i: (0, j, 0)), hidden, tokens],
        out_specs=[hidden, pl.BlockSpec((slabs, D, n), lambda j, i: (j, 0, 0)),
                   pl.BlockSpec((None, tf, D), lambda j, i: (0, j, 0))],
        out_shape=[_sds((T, F), BF16), _sds((F // n, D, n), BF16), _sds((1, F, D), BF16)],
        scratch_shapes=[pltpu.VMEM((D, tf), F32), pltpu.VMEM((tf, D), F32)], sem=("arbitrary", "arbitrary"),
        operands=[dzb, w_down, u, xin])


def _init_or_add(ref, i, value):
    @pl.when(i == 0)
    def _():
        ref[...] = value

    @pl.when(i > 0)
    def _():
        ref[...] += value


def _rope_tables(T):
    pos = jnp.arange(T, dtype=F32)
    inv_freq = ROPE_THETA ** (-jnp.arange(0, ROPE_DIM, 2, dtype=F32) / ROPE_DIM)
    ang = pos[:, None] * inv_freq[None, :]
    cos, sin = jnp.cos(ang), jnp.sin(ang)
    ones = jnp.ones((T, HEAD_DIM - ROPE_DIM), F32)
    zeros = jnp.zeros((T, HEAD_DIM - ROPE_DIM), F32)
    zh = jnp.zeros((T, ROPE_HALF), F32)
    c_head = jnp.concatenate([cos, cos, ones], axis=1)
    s_up = jnp.concatenate([-sin, zh, zeros], axis=1)
    s_dn = jnp.concatenate([zh, sin, zeros], axis=1)
    rep = LANES // HEAD_DIM
    return tuple(jnp.tile(t, (1, rep)) for t in (c_head, s_up, s_dn))


def _rope_chunk(t, c, s_up, s_dn):
    return t * c + pltpu.roll(t, LANES - ROPE_HALF, 1) * s_up + pltpu.roll(t, ROPE_HALF, 1) * s_dn


def _rope_chunk_bwd(d, c, s_up, s_dn):
    return d * c + pltpu.roll(d * s_up, ROPE_HALF, 1) + pltpu.roll(d * s_dn, LANES - ROPE_HALF, 1)


def _rope_wide(t, c, s_up, s_dn, fn):
    chunks = [fn(t[:, q * LANES:(q + 1) * LANES], c, s_up, s_dn) for q in range(t.shape[1] // LANES)]
    return chunks[0] if len(chunks) == 1 else jnp.concatenate(chunks, axis=1)


def _taps_by_residue(first):
    groups = {}
    for o in range(first, first + CONV_WIDTH):
        groups.setdefault(o % SUBLANES, []).append(o)
    return sorted(groups.items())


def _shifted_taps(win_ref, span_ref, r0, res, offs, rb, ls):
    if res == 0:
        return [functools.partial(lambda o: win_ref[pl.ds(r0 + o, rb), ls], o) for o in offs]
    n = rb + offs[-1] - res
    span_ref[0:n, :] = win_ref[pl.ds(r0 + res, n), ls]
    return [functools.partial(lambda o: span_ref[pl.ds(o - res, rb), :], o) for o in offs]


def _halo_before_spec(tm, D):
    return pl.BlockSpec((CONV_HALO, D), lambda i: (jnp.maximum(i * (tm // CONV_HALO) - 1, 0), 0))


def dwconv_fwd(g, w_dw, b_dw, ln_g, ln_b, *, tm, comm=None):
    T, D = g.shape
    nl = D // LANES

    rb = min(CONV_ROWS, tm)

    def body(g_ref, gh_ref, w_ref, b_ref, lg_ref, lb_ref, c_ref, s_ref, win_ref, span_ref):
        i = pl.program_id(0)
        win_ref[0:CONV_HALO, :] = jnp.where(i > 0, gh_ref[...], 0.0)
        win_ref[CONV_HALO:, :] = g_ref[...]

        def lane_chunk(q, carry):
            ls = pl.ds(pl.multiple_of(q * LANES, LANES), LANES)
            for r0 in range(0, tm, rb):
                acc = jnp.broadcast_to(b_ref[:, ls], (rb, LANES))
                for res, offs in _taps_by_residue(CONV_HALO - (CONV_WIDTH - 1)):
                    taps = _shifted_taps(win_ref, span_ref, r0, res, offs, rb, ls)
                    for o, tap in zip(offs, taps):
                        k = o - (CONV_HALO - (CONV_WIDTH - 1))
                        acc = acc + tap() * w_ref[k:k + 1, ls]
                c_ref[pl.ds(r0, rb), ls] = acc
            return carry

        lax.fori_loop(0, nl, lane_chunk, 0)
        n = _ln_fwd(c_ref[...], lg_ref[...], lb_ref[...])
        s_ref[...] = (n * _sigmoid(n)).astype(s_ref.dtype)

    row = pl.BlockSpec((1, D), lambda i: (0, 0))
    return _call(
        body, comm, name="dwconv_fwd", grid=(T // tm,),
        in_specs=[pl.BlockSpec((tm, D), lambda i: (i, 0)), _halo_before_spec(tm, D),
                  pl.BlockSpec((CONV_HALO, D), lambda i: (0, 0)), row, row, row],
        out_specs=[pl.BlockSpec((tm, D), lambda i: (i, 0)), pl.BlockSpec((tm, D), lambda i: (i, 0))],
        out_shape=[_sds((T, D), F32), _sds((T, D), BF16)],
        scratch_shapes=[pltpu.VMEM((tm + CONV_HALO, D), F32), pltpu.VMEM((rb + CONV_HALO, LANES), F32)],
        sem=("parallel",), operands=[g, g, w_dw, b_dw, ln_g, ln_b])


def dwconv_bwd(dc, g, ha, hg, w_dw, *, tm, comm=None):
    T, D = g.shape
    nl = D // LANES
    last = T // CONV_HALO - 1
    nt = T // tm

    rb = min(CONV_ROWS, tm)

    def body(dc_ref, dcn_ref, g_ref, gh_ref, ha_ref, hg_ref, w_ref, dh_ref, dw_ref, dbdw_ref, dbin_ref,
             win_ref, dwin_ref, dg_ref, dwp_ref, span_ref):
        i = pl.program_id(0)
        win_ref[0:CONV_HALO, :] = jnp.where(i > 0, gh_ref[...], 0.0)
        win_ref[CONV_HALO:, :] = g_ref[...]
        dwin_ref[0:tm, :] = dc_ref[...]
        dwin_ref[tm:, :] = jnp.where(i < nt - 1, dcn_ref[...], 0.0)

        @pl.when(i == 0)
        def _():
            dwp_ref[...] = jnp.zeros_like(dwp_ref)

        first = CONV_HALO - (CONV_WIDTH - 1)

        def lane_chunk(q, carry):
            ls = pl.ds(pl.multiple_of(q * LANES, LANES), LANES)
            for r0 in range(0, tm, rb):
                acc = jnp.zeros((rb, LANES), F32)
                for res, offs in _taps_by_residue(0):
                    taps = _shifted_taps(dwin_ref, span_ref, r0, res, offs, rb, ls)
                    for o, tap in zip(offs, taps):
                        k = CONV_WIDTH - 1 - o
                        acc = acc + tap() * w_ref[k:k + 1, ls]
                dg_ref[pl.ds(r0, rb), ls] = acc
                dcv = dwin_ref[pl.ds(r0, rb), ls]
                for res, offs in _taps_by_residue(first):
                    taps = _shifted_taps(win_ref, span_ref, r0, res, offs, rb, ls)
                    for o, tap in zip(offs, taps):
                        k = o - first
                        prod = dcv * tap()
                        dwp_ref[k, :, ls] += jnp.sum(prod.reshape(rb // SUBLANES, SUBLANES, LANES), axis=0)
            return carry

        lax.fori_loop(0, nl, lane_chunk, 0)

        @pl.when(i == nt - 1)
        def _():
            for k in range(CONV_WIDTH):
                dw_ref[k:k + 1, :] = jnp.sum(dwp_ref[k], axis=0, keepdims=True)
            dw_ref[CONV_WIDTH:, :] = jnp.zeros((CONV_HALO - CONV_WIDTH, D), F32)
        dg = dg_ref[...]
        ha = ha_ref[...].astype(F32)
        sg = _sigmoid(hg_ref[...].astype(F32))
        d_ha = dg * sg
        d_hg = dg * ha * sg * (1.0 - sg)
        dh_ref[:, 0:D] = d_ha.astype(dh_ref.dtype)
        dh_ref[:, D:] = d_hg.astype(dh_ref.dtype)
        _init_or_add(dbdw_ref, i, jnp.sum(dc_ref[...], axis=0, keepdims=True))
        _init_or_add(dbin_ref, i, jnp.concatenate([jnp.sum(d_ha, axis=0, keepdims=True),
                                                   jnp.sum(d_hg, axis=0, keepdims=True)], axis=1))

    tile = pl.BlockSpec((tm, D), lambda i: (i, 0))
    return _call(
        body, comm, name="dwconv_bwd", grid=(nt,),
        in_specs=[tile,
                  pl.BlockSpec((CONV_HALO, D), lambda i: (jnp.minimum((i + 1) * (tm // CONV_HALO), last), 0)),
                  tile, _halo_before_spec(tm, D),
                  tile, tile, pl.BlockSpec((CONV_HALO, D), lambda i: (0, 0))],
        out_specs=[pl.BlockSpec((tm, 2 * D), lambda i: (i, 0)), pl.BlockSpec((CONV_HALO, D), lambda i: (0, 0)),
                   pl.BlockSpec((1, D), lambda i: (0, 0)), pl.BlockSpec((1, 2 * D), lambda i: (0, 0))],
        out_shape=[_sds((T, 2 * D), BF16), _sds((CONV_HALO, D), F32), _sds((1, D), F32), _sds((1, 2 * D), F32)],
        scratch_shapes=[pltpu.VMEM((tm + CONV_HALO, D), F32), pltpu.VMEM((tm + CONV_HALO, D), F32),
                        pltpu.VMEM((tm, D), F32), pltpu.VMEM((CONV_WIDTH, SUBLANES, D), F32),
                        pltpu.VMEM((rb + CONV_HALO, LANES), F32)],
        sem=("arbitrary",), operands=[dc, dc, g, g, ha, hg, w_dw])


def _attn_specs(HD, W):
    B = ATT_BLOCK
    return [pl.BlockSpec((B, HD), lambda n: (n, 0)),
            pl.BlockSpec((B, 4 * W), lambda n: (n, 0)),
            pl.BlockSpec((B, 4 * W), lambda n: (jnp.maximum(n - 1, 0), 0))]


def _band_mask(n):
    r = lax.broadcasted_iota(jnp.int32, (ATT_BLOCK, 2 * ATT_BLOCK), 0)
    j = lax.broadcasted_iota(jnp.int32, (ATT_BLOCK, 2 * ATT_BLOCK), 1)
    return (j > r) & (j <= r + ATT_BLOCK) & ((n > 0) | (j >= ATT_BLOCK))


def _band(kvc_ref, kvp_ref, part, g, parity, W):
    lanes = slice((2 * part + parity) * W + g * LANES, (2 * part + parity) * W + (g + 1) * LANES)
    return jnp.concatenate([kvp_ref[:, lanes], kvc_ref[:, lanes]], axis=0)


def _half_mask(parity):
    lane = lax.broadcasted_iota(jnp.int32, (1, LANES), 1)
    return (lane < HEAD_DIM) if parity == 0 else (lane >= HEAD_DIM)


def widen_kv(k, v, out_ref, n_kv):
    W = n_kv * LANES
    low = _half_mask(0)
    for part, src in enumerate((k, v)):
        for cg in range(n_kv * HEAD_DIM // LANES):
            chunk = src[:, cg * LANES:(cg + 1) * LANES]
            swapped = pltpu.roll(chunk, HEAD_DIM, 1)
            for g, lo, hi in ((2 * cg, chunk, swapped), (2 * cg + 1, swapped, chunk)):
                base = 2 * part * W + g * LANES
                out_ref[:, base:base + LANES] = jnp.where(low, lo, 0.0).astype(out_ref.dtype)
                out_ref[:, base + W:base + W + LANES] = jnp.where(low, 0.0, hi).astype(out_ref.dtype)


def attn_fwd(q, kvx, sinks, n_heads, n_kv, comm=None):
    T, HD = q.shape
    W = n_kv * LANES
    B = ATT_BLOCK
    chunks_per_group = n_heads // n_kv // 2
    scale = 1.0 / math.sqrt(HEAD_DIM)

    def body(q_ref, kvc_ref, kvp_ref, sink_ref, o_ref, lse_ref):
        n = pl.program_id(0)
        mask = jnp.tile(_band_mask(n), (n_heads, 1))
        s = jnp.concatenate(
            [_dot(q_ref[:, (h // 2) * LANES:(h // 2 + 1) * LANES],
                  _band(kvc_ref, kvp_ref, 0, h // 2 // chunks_per_group, h % 2, W), ((1,), (1,)))
             for h in range(n_heads)], axis=0)
        sink = jnp.concatenate([jnp.broadcast_to(sink_ref[:, h:h + 1], (B, 1)) for h in range(n_heads)], axis=0)
        s = jnp.where(mask, s * scale, MASK_VALUE)
        m = jnp.maximum(jnp.max(s, axis=-1, keepdims=True), sink)
        e = jnp.exp(s - m)
        total = _dot(e, jnp.ones((2 * B, LANES), BF16), ((1,), (0,))) + jnp.exp(sink - m)
        lse = m + jnp.log(total[:, 0:1])
        inv = 1.0 / total
        probs = (e * jnp.concatenate([inv, inv], axis=1)).astype(BF16)
        for c in range(n_heads // 2):
            g = c // chunks_per_group
            out = (_dot(probs[2 * c * B:(2 * c + 1) * B], _band(kvc_ref, kvp_ref, 1, g, 0, W), ((1,), (0,)))
                   + _dot(probs[(2 * c + 1) * B:(2 * c + 2) * B], _band(kvc_ref, kvp_ref, 1, g, 1, W), ((1,), (0,))))
            o_ref[:, c * LANES:(c + 1) * LANES] = out.astype(o_ref.dtype)
        lse_ref[...] = jnp.concatenate([lse[h * B:(h + 1) * B] for h in range(n_heads)], axis=1)

    return _call(
        body, comm, name="attn_fwd", grid=(T // B,),
        in_specs=_attn_specs(HD, W) + [pl.BlockSpec((1, n_heads), lambda n: (0, 0))],
        out_specs=[pl.BlockSpec((B, HD), lambda n: (n, 0)), pl.BlockSpec((B, n_heads), lambda n: (n, 0))],
        out_shape=[_sds((T, HD), BF16), _sds((T, n_heads), F32)],
        sem=("parallel",), operands=[q, kvx, kvx, sinks])


def attn_bwd(q, kvx, do, lse, sinks, rope, n_heads, n_kv, comm=None):
    T, HD = q.shape
    KVD = n_kv * HEAD_DIM
    W = n_kv * LANES
    B = ATT_BLOCK
    chunks_per_group = n_heads // n_kv // 2
    scale = 1.0 / math.sqrt(HEAD_DIM)
    nb = T // B

    def body(q_ref, kvc_ref, kvp_ref, do_ref, lse_ref, sink_ref, c_ref, su_ref, sd_ref, dq_ref, dkv_ref, dsink_ref):
        n = pl.program_id(0)

        @pl.when(n == 0)
        def _():
            dkv_ref[...] = jnp.zeros_like(dkv_ref)
            dsink_ref[...] = jnp.zeros_like(dsink_ref)

        def chunk(ref, h):
            return ref[:, (h // 2) * LANES:(h // 2 + 1) * LANES]

        def band(part, h):
            return _band(kvc_ref, kvp_ref, part, h // 2 // chunks_per_group, h % 2, W)

        def stack(per_head):
            return jnp.concatenate([per_head(h) for h in range(n_heads)], axis=0)

        mask = jnp.tile(_band_mask(n), (n_heads, 1))
        s = stack(lambda h: _dot(chunk(q_ref, h), band(0, h), ((1,), (1,))))
        dp = stack(lambda h: _dot(chunk(do_ref, h), band(1, h), ((1,), (1,))))
        lse = stack(lambda h: lse_ref[:, h:h + 1])
        sink = stack(lambda h: jnp.broadcast_to(sink_ref[:, h:h + 1], (B, 1)))
        probs = jnp.exp(jnp.where(mask, s * scale, MASK_VALUE) - lse)
        delta = jnp.sum(probs * dp, axis=-1, keepdims=True)
        ds = (probs * (dp - delta) * scale).astype(BF16)
        probs = probs.astype(BF16)
        sink_term = jnp.exp(sink - lse) * delta
        dsk = [-jnp.sum(sink_term[h * B:(h + 1) * B], axis=0, keepdims=True) for h in range(n_heads)]

        dk_wide, dv_wide = [None] * n_kv, [None] * n_kv
        for c in range(n_heads // 2):
            g = c // chunks_per_group
            dq2 = None
            for h in (2 * c, 2 * c + 1):
                half = _half_mask(h % 2)
                q2, do2 = chunk(q_ref, h), chunk(do_ref, h)
                ds_h, p_h = ds[h * B:(h + 1) * B], probs[h * B:(h + 1) * B]
                part = _dot(ds_h, band(0, h), ((1,), (0,)))
                dq2 = part if dq2 is None else dq2 + part
                dk_h = _dot(ds_h, jnp.where(half, q2, jnp.zeros_like(q2)), ((0,), (0,)))
                dv_h = _dot(p_h, jnp.where(half, do2, jnp.zeros_like(do2)), ((0,), (0,)))
                dk_wide[g] = dk_h if dk_wide[g] is None else dk_wide[g] + dk_h
                dv_wide[g] = dv_h if dv_wide[g] is None else dv_wide[g] + dv_h
            dq_ref[:, c * LANES:(c + 1) * LANES] = _rope_chunk_bwd(
                dq2, c_ref[...], su_ref[...], sd_ref[...]).astype(dq_ref.dtype)

        def fold(wide):
            low = _half_mask(0)
            both = [w + pltpu.roll(w, HEAD_DIM, 1) for w in wide]
            return jnp.concatenate([jnp.where(low, both[2 * cg], both[2 * cg + 1]) for cg in range(n_kv // 2)], axis=1)

        dkv = jnp.concatenate([fold(dk_wide), fold(dv_wide)], axis=1)
        prev = pl.ds(pl.multiple_of(jnp.maximum(n - 1, 0) * B, B), B)
        cur = pl.ds(pl.multiple_of(n * B, B), B)
        dkv_ref[prev, :] += dkv[0:B, :]
        dkv_ref[cur, :] += dkv[B:, :]
        dsink_ref[...] += jnp.concatenate(dsk, axis=1)

    tab = pl.BlockSpec((B, LANES), lambda n: (n, 0))
    return _call(
        body, comm, name="attn_bwd", grid=(nb,),
        in_specs=_attn_specs(HD, W) + [pl.BlockSpec((B, HD), lambda n: (n, 0)),
                                       pl.BlockSpec((B, n_heads), lambda n: (n, 0)),
                                       pl.BlockSpec((1, n_heads), lambda n: (0, 0)), tab, tab, tab],
        out_specs=[pl.BlockSpec((B, HD), lambda n: (n, 0)), pl.BlockSpec((T, 2 * KVD), lambda n: (0, 0)),
                   pl.BlockSpec((1, n_heads), lambda n: (0, 0))],
        out_shape=[_sds((T, HD), BF16), _sds((T, 2 * KVD), F32), _sds((1, n_heads), F32)],
        sem=("arbitrary",), operands=[q, kvx, kvx, do, lse, sinks, *rope])


def chip_sums(name, pairs):
    n = len(pairs)

    def body(core_ref, *refs):
        del core_ref
        for k in range(n):
            refs[2 * n + k][...] = (refs[2 * k][...].astype(F32) + refs[2 * k + 1][...].astype(F32)
                                    ).astype(refs[2 * n + k].dtype)

    in_specs, out_specs, out_shape, operands = [], [], [], []
    for g, p_sib in pairs:
        _, a, b = g.shape
        in_specs += [pl.BlockSpec((None, None, a, b), lambda q, core: (q, core[0], 0, 0)),
                     pl.BlockSpec((None, a, b), lambda q, core: (q, 0, 0))]
        out_specs.append(pl.BlockSpec((None, a, b), lambda q, core: (q, 0, 0)))
        out_shape.append(_sds((N_CHIPS, a, b), g.dtype))
        operands += [g.reshape(N_CHIPS, 2, a, b), p_sib]
    my_core = lax.axis_index("c").astype(jnp.int32).reshape(1)
    return pl.pallas_call(
        body, name=name, out_shape=out_shape,
        grid_spec=pltpu.PrefetchScalarGridSpec(num_scalar_prefetch=1, grid=(N_CHIPS,), in_specs=in_specs,
                                               out_specs=out_specs),
        compiler_params=_params(("arbitrary",)))(my_core, *operands)


def adamw(name, recvs, w, m, v, *, ta):
    L, a, b = w.shape
    n_terms = recvs[0].shape[0]
    assert a % ta == 0 and len(recvs) == L
    c1 = 1.0 - ADAM_B1 ** ADAM_STEP
    c2 = 1.0 - ADAM_B2 ** ADAM_STEP

    def body(*refs):
        r_refs = refs[:L]
        w_ref, m_ref, v_ref, g_ref, d_ref, nm_ref, nv_ref = refs[L:]
        layer = pl.program_id(0)
        for l in range(L):
            @pl.when(layer == l)
            def _(r_ref=r_refs[l]):
                g = r_ref[0].astype(F32)
                for s in range(1, n_terms):
                    g = g + r_ref[s].astype(F32)
                nm = ADAM_B1 * m_ref[...] + (1.0 - ADAM_B1) * g
                nv = ADAM_B2 * v_ref[...] + (1.0 - ADAM_B2) * jnp.square(g)
                m_hat = nm / c1
                v_hat = nv / c2
                g_ref[...] = g
                d_ref[...] = -ADAM_LR * (m_hat / (jnp.sqrt(v_hat) + ADAM_EPS) + ADAM_WD * w_ref[...])
                nm_ref[...] = nm
                nv_ref[...] = nv

    blk = pl.BlockSpec((None, ta, b), lambda l, i: (l, i, 0))
    out = _sds((L, a, b), F32)
    r_specs = [pl.BlockSpec((n_terms, ta, b), lambda l, i, ll=ll: (0, jnp.where(l == ll, i, 0), 0)) for ll in range(L)]
    return pl.pallas_call(
        body, name=name, grid=(L, a // ta), in_specs=r_specs + [blk, blk, blk],
        out_specs=[blk, blk, blk, blk], out_shape=[out, out, out, out],
        compiler_params=_params(("arbitrary", "arbitrary")))(*recvs, w, m, v)


def _pack_rows(parts):
    out = []
    for arr, rows in parts:
        arr = arr.reshape(-1, LANES).astype(F32)
        out.append(jnp.pad(arr, ((0, rows - arr.shape[0]), (0, 0))))
    return jnp.concatenate(out, axis=0)


def _small_rows(a, prefix):
    return _pack_rows([(a[prefix + name], rows) for name, rows in SMALL_SHARDED])


def _unpack_small(packed, a):
    out, r0 = {}, 0
    for name, rows in SMALL_SHARDED:
        shape = a[name].shape
        used = math.prod(shape) // LANES
        out[name] = packed[r0:r0 + used].reshape(shape)
        r0 += rows
    return out


def _rep_rows(a, prefix):
    parts = []
    for name in REPLICATED:
        arr = a[prefix + name]
        if arr.size % LANES:
            arr = jnp.pad(arr.reshape(1, -1), ((0, 0), (0, LANES - arr.size % LANES)))
        rows = -(-arr.size // LANES)
        parts.append((arr, -(-rows // SUBLANES) * SUBLANES))
    return _pack_rows(parts)


def _unpack_rep(packed, a):
    out, r0 = {}, 0
    for name in REPLICATED:
        shape = a[name].shape
        size = math.prod(shape)
        rows = -(-size // LANES)
        out[name] = packed[r0:r0 + rows].reshape(-1)[:size].reshape(shape)
        r0 += -(-rows // SUBLANES) * SUBLANES
    return out


def _step(a):
    x = a['x'][0]
    T, D = x.shape
    tgt = a['loss_target'][0]
    p_in = [a['p'][i, 0] for i in range(DEPTH)]
    PLE = p_in[0].shape[1]
    n_heads = a['attn_sinks'].shape[1]
    HD = n_heads * HEAD_DIM
    KVD = a['kv_w_k'].shape[1]
    n_kv = KVD // HEAD_DIM
    F = a['mlp_w_down'].shape[1] * N_DEV
    tm = min(TOKEN_TILE, T)
    tm2 = min(2 * TOKEN_TILE, T)
    tmc = min(TOKEN_TILE, T)
    tw = 512
    alpha = DEEPNORM_ALPHA
    xb, p_b = x, p_in

    def shard3(w):
        return w.reshape((1,) + w.shape) if w.ndim == 2 else w

    def gather(*specs):
        return _GatherJob([shard3(a[nm])[li].astype(BF16) for nm, li in specs])

    (W_in, small_full), = comm_only("gather_first",
                                    [_GatherJob([a['conv_w_in'][0].astype(BF16), _small_rows(a, '')])])
    r0, small = 0, {}
    for name, rows in SMALL_SHARDED:
        small[name] = small_full[:, r0:r0 + rows]
        r0 += rows
    b_in = small['conv_b_in'][:, 0:2 * D // N_DEV // LANES].reshape(1, 2 * D)
    w_dw = jnp.transpose(small['conv_w_dw'], (1, 0, 2)).reshape(CONV_HALO, D)
    b_dw, cln_g, cln_b, b_out = (small[nm][:, 0].reshape(1, D) for nm in
                                 ('conv_b_dw', 'conv_ln_g', 'conv_ln_b', 'conv_b_out'))
    W_up, W_down, W_proj, W_gate = {}, {}, {}, {}
    mix_g, mix_b, mlp_g, mlp_b = a['mix_ln_g'], a['mix_ln_b'], a['mlp_ln_g'], a['mlp_ln_b']
    rope = _rope_tables(T)

    def set_ple_weights(li, g_proj, g_gate):
        W_proj[li] = jnp.transpose(g_proj, (1, 0, 2)).reshape(1, PLE, D)
        W_gate[li] = g_gate.reshape(1, D, D)

    def row(v, i):
        return v[i:i + 1]

    def res_ln_epi(coef):
        def epi(accs, ex, out, i):
            acc = accs[0] if isinstance(accs, list) else accs
            n_ex = len(ex)
            res_ref, g_ref, b_ref = ex[n_ex - 3], ex[n_ex - 2], ex[n_ex - 1]
            z = coef * res_ref[...] + acc
            if n_ex == 4:
                z = z + ex[0][...]
            out[0][...] = z
            xo = _ln_fwd(z, g_ref[...], b_ref[...])
            out[1][...] = xo
            out[2][...] = xo.astype(BF16)
        return epi

    res_ln_outs = [(_sds((T, D), F32), 'tile'), (_sds((T, D), F32), 'tile'), (_sds((T, D), BF16), 'tile')]

    def mlp_fwd(li, xin, xin_b, down_comm=None):
        def up_epi(accs, ex, out, i):
            out[0][...] = accs[0].astype(BF16)
        res = mm_nn(f"mlp_up_{li}", [(xin_b, W_up[li], 0)], [], [(_sds((T, F), BF16), 'tile')], up_epi, tm=tm2,
                    tn=min(1024, F), comm=None if li in W_down else gather(('mlp_w_down', li)))
        if li in W_down:
            u, = res
        else:
            (u,), got_up = res
            W_down[li] = got_up[0].reshape(1, F, D)
        res = mm_nn(f"mlp_down_{li}", [(u, W_down[li], 0)],
                    [(xin, 'tile'), (row(mlp_g, li), 'row'), (row(mlp_b, li), 'row')],
                    res_ln_outs, res_ln_epi(alpha), tm=tm, tn=D, tk=F, comm=down_comm, a_pro=_sq_relu)
        (z, xo, xo_b), got_down = res if down_comm is not None else (res, ())
        return u, z, xo, xo_b, got_down

    def ple_fwd(li, xin, xin_b, with_loss, comm=None):
        def epi(accs, ex, out, i):
            pp, gg = accs
            xo = ex[0][...] + pp * _sigmoid(gg)
            out[1][...] = pp.astype(BF16)
            out[2][...] = gg.astype(BF16)
            if with_loss:
                err = xo - ex[1][...]
                out[0][...] = err * (1.0 / D)
                _init_or_add(out[3], i, jnp.sum(err * err, axis=0, keepdims=True) * (0.5 / D))
            else:
                out[0][...] = xo
                out[3][...] = xo.astype(BF16)
        extras = [(xin, 'tile')] + ([(tgt, 'tile')] if with_loss else [])
        outs = [(_sds((T, D), F32), 'tile'), (_sds((T, D), BF16), 'tile'), (_sds((T, D), BF16), 'tile')]
        outs.append((_sds((1, D), F32), 'rowacc') if with_loss else (_sds((T, D), BF16), 'tile'))
        return mm_nn(f"ple_{li}", [(p_b[li], W_proj[li], 0), (xin_b, W_gate[li], 0)], extras, outs, epi, tm=tm, tn=D,
                     comm=comm)

    assert D // N_DEV == LANES
    (g0, ha0, hg0), got = _glu(xb, W_in, b_in, T, D, tm,
                               gather(('conv_w_out', 0), ('ple_w_proj', 0), ('ple_w_gate', 0)))
    W_out = got[0].reshape(1, D, D)
    set_ple_weights(0, got[1], got[2])
    (c0, s0), (W_up[0],) = dwconv_fwd(g0, w_dw, b_dw, cln_g, cln_b, tm=tmc, comm=gather(('mlp_w_up', 0)))
    z1, x1, x1b = mm_nn("conv_out", [(s0, W_out, 0)],
                        [(b_out, 'row'), (x, 'tile'), (row(mix_g, 0), 'row'), (row(mix_b, 0), 'row')],
                        res_ln_outs, res_ln_epi(alpha), tm=tm, tn=D)
    u0, z2, x2, x2b, got = mlp_fwd(
        0, x1, x1b, down_comm=gather(('attn_w_q', 0), ('kv_w_k', 0), ('kv_w_v', 0), ('attn_w_o', 0),
                                     ('ple_w_proj', 1), ('ple_w_gate', 1)))
    W_qkv = jnp.concatenate([got[0].reshape(D, HD), got[1].reshape(D, KVD), got[2].reshape(D, KVD)], axis=1)[None]
    W_o = got[3].reshape(1, HD, D)
    set_ple_weights(1, got[4], got[5])
    x3, pp0, gg0, x3b = ple_fwd(0, x2, x2b, False)

    def qkv_epi(accs, ex, out, i):
        t = accs[0]
        c, su, sd = ex[0][...], ex[1][...], ex[2][...]
        out[0][...] = _rope_wide(t[:, 0:HD], c, su, sd, _rope_chunk).astype(BF16)
        widen_kv(_rope_wide(t[:, HD:HD + KVD], c, su, sd, _rope_chunk), t[:, HD + KVD:], out[1], n_kv)
    NQ = HD + 2 * KVD
    q1, kvx1 = mm_nn("qkv_rope", [(x3b, W_qkv, 0)], [(t, 'tab') for t in rope],
                     [(_sds((T, HD), BF16), 'rows'), (_sds((T, 4 * n_kv * LANES), BF16), 'rows')], qkv_epi,
                     tm=tm, tn=NQ)
    (o1, lse1), (W_up[1], g_down1) = attn_fwd(q1, kvx1, a['attn_sinks'], n_heads, n_kv,
                                              comm=gather(('mlp_w_up', 1), ('mlp_w_down', 1)))
    W_down[1] = g_down1.reshape(1, F, D)
    z3, x4, x4b = mm_nn("attn_out", [(o1, W_o, 0)], [(x3, 'tile'), (row(mix_g, 1), 'row'), (row(mix_b, 1), 'row')],
                        res_ln_outs, res_ln_epi(alpha), tm=tm, tn=D)
    u1, z4, x5, x5b, _ = mlp_fwd(1, x4, x4b)
    dy, pp1, gg1, loss_row = ple_fwd(1, x5, x5b, True)
    loss_local = jnp.sum(loss_row)

    grads = {}

    def ln_bwd_epi(coef, with_colsum):
        def epi(acc, ex, out, i):
            d_x = acc + coef * ex[0][...]
            dz, dg, db = _ln_bwd(d_x, ex[1][...], ex[2][...])
            out[0][...] = dz
            out[1][...] = dz.astype(BF16)
            _init_or_add(out[2], i, dg)
            _init_or_add(out[3], i, db)
            if with_colsum:
                _init_or_add(out[4], i, jnp.sum(dz, axis=0, keepdims=True))
        return epi

    def ln_bwd_outs(with_colsum):
        outs = [(_sds((T, D), F32), 'tile'), (_sds((T, D), BF16), 'tile'), (_sds((1, D), F32), 'rowacc'),
                (_sds((1, D), F32), 'rowacc')]
        return outs + ([(_sds((1, D), F32), 'rowacc')] if with_colsum else [])

    def ple_bwd(li, d_out, xin, pp, gg, z_mlp, pair_specs=None, chip_keys=None):
        side = {}

        def gate_grads(ex):
            d = ex[0][...]
            sg = _sigmoid(ex[4][...].astype(F32))
            side['d_pp'] = (d * sg).astype(BF16)
            side['d_gg'] = (d * ex[3][...].astype(F32) * sg * (1.0 - sg)).astype(BF16)
            return side['d_gg']

        jobs = None
        if pair_specs:
            jobs = _Jobs([pair_stage(*pair_specs)] + ([chip_stage(*chip_keys)] if chip_keys else []))
        res = mm_nt(f"ple_dx_{li}", _sds((T, D), BF16), W_gate[li],
                    [(d_out, 'tile'), (z_mlp, 'tile'), (row(mlp_g, li), 'row'), (pp, 'tile'), (gg, 'tile')],
                    ln_bwd_outs(False), ln_bwd_epi(1.0, False), tm=tm, tko=D, tc=D, comm=jobs, a_pro=gate_grads,
                    dws=[(xin, lambda: side['d_gg']), (p_b[li], lambda: side['d_pp'])])
        if jobs is not None:
            res = split_hosted(res, jobs, chips=[1] if chip_keys else [], pairs=[0])
        dz, dzb, dg, db, dw_gate, dw_proj = res
        grads[('mlp_ln_g', li)], grads[('mlp_ln_b', li)] = dg, db
        grads[('ple_w_gate', li)], grads[('ple_w_proj', li)] = dw_gate, dw_proj
        return dz, dzb

    recv = {}
    wqkv_cols = {'attn_w_q': (0, HD), 'kv_w_k': (HD, HD + KVD), 'kv_w_v': (HD + KVD, NQ)}

    def piece(name, li):
        if name == 'conv_w_in':
            return grads['conv_w_in']
        if name == 'mlp_w_up':
            return grads[('mlp_w_up', li)]
        if name == 'ple_w_proj':
            return jnp.transpose(grads[('ple_w_proj', li)][0].reshape(PLE, N_DEV, D // N_DEV), (1, 0, 2))
        if name in wqkv_cols:
            g = grads['w_qkv'][:, wqkv_cols[name][0]:wqkv_cols[name][1]]
        else:
            g = grads[name] if name in grads else grads[(name, li)]
            g = g[0]
        return g.reshape(N_DEV, g.shape[0] // N_DEV, g.shape[1])

    def pair_stage(*specs):
        job = _PairJob([piece(nm, li) for nm, li in specs])
        job.specs = specs
        return job

    sums = {}

    def pair_done(pair_job, got):
        name = "chip_sum_" + "_".join(f"{nm}_{li}" for nm, li in pair_job.specs)
        for spec, total in zip(pair_job.specs, chip_sums(name, list(zip(pair_job.sources, got)))):
            sums[spec] = total

    def chip_stage(*keys):
        job = _ChipJob([sums[k] for k in keys])
        job.specs = keys
        return job

    def hosted(res, job):
        main, got = res
        for spec, r in zip(job.specs, got):
            recv[spec] = r
        return main

    def split_hosted(res, jobs, chips, pairs):
        main, got = res
        parts = jobs.split(got)
        for k in chips:
            hosted((None, parts[k]), jobs.jobs[k])
        for k in pairs:
            pair_done(jobs.jobs[k], parts[k])
        return main

    def mlp_bwd(li, dz, dzb, xin, u, z_mix, with_colsum, du_pairs, du_chips, dx_chips=None):
        jobs = _Jobs([pair_stage(*du_pairs)] + ([chip_stage(*du_chips)] if du_chips else []))
        du, grads[('mlp_w_up', li)], grads[('mlp_w_down', li)] = split_hosted(
            mlp_du_dw(f"mlp_du_{li}", dzb, W_down[li], u, xin, F // N_DEV, tm=tm2, tf=min(1024, F), comm=jobs),
            jobs, chips=[1] if du_chips else [], pairs=[0])
        jobs = _Jobs([pair_stage(('mlp_w_up', li), ('mlp_w_down', li))] + ([chip_stage(*dx_chips)] if dx_chips else []))
        res = split_hosted(
            mm_nt(f"mlp_dx_{li}", du, W_up[li], [(dz, 'tile'), (z_mix, 'tile'), (row(mix_g, li), 'row')],
                  ln_bwd_outs(with_colsum), ln_bwd_epi(alpha, with_colsum), tm=tm, tko=D, tc=F, comm=jobs),
            jobs, chips=[1] if dx_chips else [], pairs=[0])
        grads[('mix_ln_g', li)], grads[('mix_ln_b', li)] = res[2], res[3]
        return res

    dz4, dz4b = ple_bwd(1, dy, x5b, pp1, gg1, z4)
    dz3, dz3b, _, _ = mlp_bwd(1, dz4, dz4b, x4b, u1, z3, False, [('ple_w_gate', 1), ('ple_w_proj', 1)], None)

    def do_epi(acc, ex, out, i):
        out[0][...] = acc.astype(BF16)
    do1, grads['attn_w_o'] = mm_nt("attn_do", dz3b, W_o, [], [(_sds((T, HD), BF16), 'tile')], do_epi, tm=tm, tko=HD,
                                   tc=D, dws=[(o1, None)])
    jobs = _Jobs([chip_stage(('mlp_w_up', 1)), pair_stage(('attn_w_o', 0)),
                  chip_stage(('ple_w_gate', 1), ('ple_w_proj', 1))])
    dq, dkv, d_sinks = split_hosted(
        attn_bwd(q1, kvx1, do1, lse1, a['attn_sinks'], rope, n_heads, n_kv, comm=jobs), jobs, chips=[0, 2], pairs=[1])

    def d_qkv_tile(ex):
        dk = _rope_wide(ex[2][:, 0:KVD], ex[3][...], ex[4][...], ex[5][...], _rope_chunk_bwd)
        return jnp.concatenate([ex[1][...], dk.astype(BF16), ex[2][:, KVD:].astype(BF16)], axis=1)

    def dx3_epi(acc, ex, out, i):
        out[0][...] = acc + alpha * ex[0][...]
    dx3, dw_qkv = mm_nt("attn_dx", _sds((T, NQ), BF16), W_qkv,
                        [(dz3, 'tile'), (dq, 'rows'), (dkv, 'rows')] + [(t, 'tab') for t in rope],
                        [(_sds((T, D), F32), 'tile')], dx3_epi, tm=tm, tko=D, tc=NQ, a_pro=d_qkv_tile,
                        dws=[(x3b, None)])
    grads['w_qkv'] = dw_qkv[0]

    dz2, dz2b = ple_bwd(0, dx3, x2b, pp0, gg0, z2)
    dz1, dz1b, _, _, db_out = mlp_bwd(
        0, dz2, dz2b, x1b, u0, z1, True,
        [('ple_w_gate', 0), ('ple_w_proj', 0), ('attn_w_q', 0), ('kv_w_k', 0), ('kv_w_v', 0)],
        [('mlp_w_down', 1), ('attn_w_o', 0)],
        dx_chips=[('ple_w_gate', 0), ('ple_w_proj', 0), ('attn_w_q', 0), ('kv_w_k', 0), ('kv_w_v', 0)])

    def ds_epi(acc, ex, out, i):
        n = _ln_fwd(ex[0][...], ex[1][...], ex[2][...])
        sg = _sigmoid(n)
        dn = acc * (sg * (1.0 + n * (1.0 - sg)))
        dc, dg, db = _ln_bwd(dn, ex[0][...], ex[1][...])
        out[0][...] = dc
        _init_or_add(out[1], i, dg)
        _init_or_add(out[2], i, db)
    dc0, d_cln_g, d_cln_b, grads['conv_w_out'] = mm_nt(
        "conv_ds", dz1b, W_out, [(c0, 'tile'), (cln_g, 'row'), (cln_b, 'row')],
        [(_sds((T, D), F32), 'tile'), (_sds((1, D), F32), 'rowacc'), (_sds((1, D), F32), 'rowacc')], ds_epi,
        tm=tm, tko=D, tc=D, dws=[(s0, None)])
    job = chip_stage(('mlp_w_up', 0), ('mlp_w_down', 0))
    dh0, d_wdw, d_bdw, d_bin = hosted(dwconv_bwd(dc0, g0, ha0, hg0, w_dw, tm=tmc, comm=job), job)
    grads['conv_w_in'] = mm_tn("d_conv_win", xb, dh0, 2 * D // N_DEV, BF16, tm=T, tk=tw, tn=tw)

    pair = pair_stage(('conv_w_in', 0), ('conv_w_out', 0))
    pair_done(pair, comm_only("pair_last", [pair])[0])
    last_chip = chip_stage(('conv_w_in', 0), ('conv_w_out', 0))

    def own_rows(vec, rows_used, rows):
        arr = vec.reshape(N_DEV, rows_used, LANES)
        return jnp.pad(arr, ((0, 0), (0, rows - rows_used), (0, 0)))
    dwdw_dev = jnp.transpose(d_wdw.reshape(CONV_HALO, N_DEV, D // N_DEV), (1, 0, 2))
    lane_rows = D // N_DEV // LANES
    small_grad = jnp.concatenate([
        own_rows(d_bin, 2 * lane_rows, 8), dwdw_dev if lane_rows == 1 else dwdw_dev.reshape(N_DEV, -1, LANES),
        own_rows(d_bdw, lane_rows, 8), own_rows(d_cln_g, lane_rows, 8), own_rows(d_cln_b, lane_rows, 8),
        own_rows(db_out, lane_rows, 8)], axis=1)
    n_small = small_grad.shape[1]

    rep_local = {'mix_ln_g': jnp.concatenate([grads[('mix_ln_g', li)] for li in range(DEPTH)], axis=0),
                 'mix_ln_b': jnp.concatenate([grads[('mix_ln_b', li)] for li in range(DEPTH)], axis=0),
                 'mlp_ln_g': jnp.concatenate([grads[('mlp_ln_g', li)] for li in range(DEPTH)], axis=0),
                 'mlp_ln_b': jnp.concatenate([grads[('mlp_ln_b', li)] for li in range(DEPTH)], axis=0),
                 'attn_sinks': d_sinks}
    rep_grad = _rep_rows(rep_local, '')
    n_rep = rep_grad.shape[0]
    last = _Jobs([last_chip, _DirectJob([small_grad, jnp.broadcast_to(rep_grad[None], (N_DEV, n_rep, LANES))])])

    def dx_epi(acc, ex, out, i):
        out[0][...] = acc + alpha * ex[0][...]
    (grad_x,), got = mm_nt("conv_dx", dh0, W_in, [(dz1, 'tile')], [(_sds((T, D), F32), 'tile')], dx_epi,
                           tm=tm, tko=D, tc=D, comm=last)
    got, (recv_small, recv_rep) = last.split(got)
    hosted((None, got), last_chip)

    result = {}
    kinds = ('grad', 'delta', 'new_m', 'new_v')
    w, m, v = (_small_rows(a, pre)[None] for pre in ('', 'm_', 'v_'))
    for kind, arr in zip(kinds, adamw("adamw_small", [recv_small], w, m, v, ta=n_small)):
        for pname, val in _unpack_small(arr[0], a).items():
            result[(kind, pname)] = val
    w, m, v = (_rep_rows(a, pre)[None] for pre in ('', 'm_', 'v_'))
    for kind, arr in zip(kinds, adamw("adamw_rep", [recv_rep], w, m, v, ta=n_rep)):
        for pname, val in _unpack_rep(arr[0], a).items():
            result[(kind, pname)] = val
    for name in BIG_WEIGHTS:
        w, m, v = (shard3(a[pre + name]) for pre in ('', 'm_', 'v_'))
        recvs = [recv[(name, li)] for li in range(w.shape[0])]
        for kind, arr in zip(kinds, adamw("adamw_" + name, recvs, w, m, v, ta=min(256, w.shape[1]))):
            result[(kind, name)] = arr.reshape(a[name].shape)

    loss = lax.psum(loss_local, ("x", "y", "c"))
    out = [loss, grad_x[None]]
    for kind in ('grad', 'delta', 'new_m', 'new_v'):
        out += [result[(kind, name)] for name in WEIGHT_NAMES]
    return tuple(out)


def _glu(x, W_in, b_in, T, D, tm, comm=None):
    n = W_in.shape[2]
    q = 2 if D // n % 2 == 0 else 1
    nt = D // (q * n)
    tn = q * n

    def body(x_ref, wa_ref, wg_ref, ba_ref, bg_ref, g_ref, ha_ref, hg_ref):
        xb = x_ref[...].astype(BF16)
        ha = jnp.concatenate([_dot(xb, wa_ref[s], ((1,), (0,))) for s in range(q)], axis=1) + ba_ref[...]
        hg = jnp.concatenate([_dot(xb, wg_ref[s], ((1,), (0,))) for s in range(q)], axis=1) + bg_ref[...]
        g_ref[...] = ha * _sigmoid(hg)
        ha_ref[...] = ha.astype(ha_ref.dtype)
        hg_ref[...] = hg.astype(hg_ref.dtype)

    return _call(
        body, comm, name="conv_in_glu", grid=(T // tm, nt),
        in_specs=[pl.BlockSpec((tm, D), lambda i, j: (i, 0)),
                  pl.BlockSpec((q, D, n), lambda i, j: (j, 0, 0)),
                  pl.BlockSpec((q, D, n), lambda i, j: (j + nt, 0, 0)),
                  pl.BlockSpec((1, tn), lambda i, j: (0, j)), pl.BlockSpec((1, tn), lambda i, j: (0, j + nt))],
        out_specs=[pl.BlockSpec((tm, tn), lambda i, j: (i, j))] * 3,
        out_shape=[_sds((T, D), F32), _sds((T, D), BF16), _sds((T, D), BF16)],
        sem=("parallel", "parallel"), operands=[x, W_in, W_in, b_in, b_in])


def kernel(x, p, conv_w_in, conv_b_in, conv_w_dw, conv_b_dw, conv_ln_g, conv_ln_b, conv_w_out, conv_b_out, kv_w_k, kv_w_v, attn_w_q, attn_sinks, attn_w_o, mix_ln_g, mix_ln_b, mlp_w_up, mlp_w_down, mlp_ln_g, mlp_ln_b, ple_w_proj, ple_w_gate, loss_target, m_conv_w_in, m_conv_b_in, m_conv_w_dw, m_conv_b_dw, m_conv_ln_g, m_conv_ln_b, m_conv_w_out, m_conv_b_out, m_kv_w_k, m_kv_w_v, m_attn_w_q, m_attn_sinks, m_attn_w_o, m_mix_ln_g, m_mix_ln_b, m_mlp_w_up, m_mlp_w_down, m_mlp_ln_g, m_mlp_ln_b, m_ple_w_proj, m_ple_w_gate, v_conv_w_in, v_conv_b_in, v_conv_w_dw, v_conv_b_dw, v_conv_ln_g, v_conv_ln_b, v_conv_w_out, v_conv_b_out, v_kv_w_k, v_kv_w_v, v_attn_w_q, v_attn_sinks, v_attn_w_o, v_mix_ln_g, v_mix_ln_b, v_mlp_w_up, v_mlp_w_down, v_mlp_ln_g, v_mlp_ln_b, v_ple_w_proj, v_ple_w_gate):
    return _step(dict(locals()))
```
